```python
import math
import jax
import jax.numpy as jnp
from jax import lax
import numpy as np

D_MODEL = 1024
BATCH = 2
SEQ = 8192
DEPTH = 2
DEC_BATCH = 32
DEC_SEQ = 1
PAST_LEN = 8192
PAGE_SIZE = 128

N_MIXERS = 2
N_ATTN_LAYERS = (DEPTH + 1) // 2
N_GM_LAYERS = DEPTH // 2
N_HEADS = 16
HEAD_DIM = 64
KV_HEADS = 4
Q_PER_KV = N_HEADS // KV_HEADS
CMP_LEN = 32
CMP_STRIDE = 16
CMP_HID = 64
SEL_BLOCK = 64
CMP_PER_SEL = SEL_BLOCK // CMP_STRIDE
N_SEL = 16
N_LOCAL = 2
WINDOW = 512
Q_BLOCK = 128
NSA_IN = N_HEADS * HEAD_DIM + 6 * KV_HEADS * HEAD_DIM + 3 * N_HEADS
N_BUCKETS = 32
MAX_DISTANCE = 128
CHUNK = 128
GM_WIDTH = 2048
GM_GROUPS = 8
GM_GROUP_DIM = GM_WIDTH // GM_GROUPS
N_GROUPS = 4
EXPERTS_PER_GROUP = 8
N_EXPERTS = N_GROUPS * EXPERTS_PER_GROUP
TOP_K = 2
D_EXPERT = 512
MOE_BLOCK = 128
EPS = 1e-6

kernel_name = 'nsa_gmlp_hmoe_hybrid_step'


def rmsnorm(x, g):
    xf = x.astype(jnp.float32)
    y = xf * lax.rsqrt(jnp.mean(xf * xf, axis=-1, keepdims=True) + EPS)
    return (y * g.astype(jnp.float32)).astype(x.dtype)


def layernorm(x, g, b):
    xf = x.astype(jnp.float32)
    mu = jnp.mean(xf, axis=-1, keepdims=True)
    var = jnp.mean(jnp.square(xf - mu), axis=-1, keepdims=True)
    return ((xf - mu) * lax.rsqrt(var + EPS) * g.astype(jnp.float32) + b.astype(jnp.float32)).astype(x.dtype)


def rel_bucket(n):
    n = jnp.maximum(n, 0)
    max_exact = N_BUCKETS // 2
    nf = jnp.maximum(n, 1).astype(jnp.float32)
    large = max_exact + (jnp.log(nf / max_exact) / math.log(MAX_DISTANCE / max_exact)
                         * (N_BUCKETS - max_exact)).astype(jnp.int32)
    large = jnp.minimum(large, N_BUCKETS - 1)
    return jnp.where(n < max_exact, n, large)


def head_bias(rel, rel_bias):
    q, k = rel.shape
    b = rel_bias[rel_bucket(rel)]
    return b.reshape(q, k, KV_HEADS, Q_PER_KV).transpose(2, 3, 0, 1)


def masked_softmax(logits, mask):
    z = jnp.where(mask, logits.astype(jnp.float32), -1e30)
    return jax.nn.softmax(z, axis=-1) * mask


def nsa_project(xn, w_in, q_gain, k_gain):
    B, T, _ = xn.shape
    hq = N_HEADS * HEAD_DIM
    hkv = 6 * KV_HEADS * HEAD_DIM
    h = xn @ w_in
    q = rmsnorm(h[..., :hq].reshape(B, T, N_HEADS, HEAD_DIM), q_gain)
    kv = h[..., hq:hq + hkv].reshape(B, T, 6, KV_HEADS, HEAD_DIM)
    gates = jax.nn.sigmoid(h[..., hq + hkv:].astype(jnp.float32)).reshape(B, T, N_HEADS, 3)
    k_sel = rmsnorm(kv[:, :, 2], k_gain[1])
    k_win = rmsnorm(kv[:, :, 4], k_gain[2])
    return q, kv[:, :, 0], kv[:, :, 1], k_sel, kv[:, :, 3], k_win, kv[:, :, 5], gates


def compress(rows, pe, w1, w2):
    B, T = rows.shape[:2]
    n_ch = T // CMP_STRIDE
    c = rows[:, :n_ch * CMP_STRIDE].reshape(B, n_ch, CMP_STRIDE, KV_HEADS, HEAD_DIM)
    blk = jnp.concatenate([c[:, :-1], c[:, 1:]], axis=2) + pe[None, None, :, None, :]
    h = jax.nn.gelu(jnp.einsum('bnlkd,ldf->bnkf', blk, w1.reshape(CMP_LEN, HEAD_DIM, CMP_HID)))
    return jnp.einsum('bnkf,fd->bnkd', h, w2)


def compressed_kv(k_raw, v_raw, k_gain0, pe, w1, w2):
    kc = rmsnorm(compress(k_raw, pe[0], w1[0], w2[0]), k_gain0)
    vc = compress(v_raw, pe[1], w1[1], w2[1])
    end = jnp.arange(kc.shape[1]) * CMP_STRIDE + (CMP_LEN - 1)
    return kc, vc, end


def nsa_core(q, q_pos, kc, vc, kc_end, ks, vs, kw, vw, kw_pos, gates, rel_bias):
    B, Q = q.shape[:2]
    nc = kc.shape[1]
    scale = HEAD_DIM ** -0.5
    qh = q.reshape(B, Q, KV_HEADS, Q_PER_KV, HEAD_DIM)
    lc = jnp.einsum('bqkgd,bnkd->bkgqn', qh, kc) * scale + head_bias(q_pos[:, None] - kc_end[None, :], rel_bias)
    pc = masked_softmax(lc, kc_end[None, :] <= q_pos[:, None])
    oc = jnp.einsum('bkgqn,bnkd->bqkgd', pc.astype(vc.dtype), vc)
    nb = ks.shape[1] // SEL_BLOCK
    imp = jnp.pad(pc.sum(axis=2), ((0, 0), (0, 0), (0, 0), (0, nb * CMP_PER_SEL - nc)))
    imp = imp.reshape(B, KV_HEADS, Q, nb, CMP_PER_SEL).sum(-1)
    blk = jnp.arange(nb)
    d = (q_pos // SEL_BLOCK)[:, None] - blk[None, :]
    forced = (blk[None, :] == 0) | ((d >= 0) & (d < N_LOCAL))
    score = jnp.where(forced, jnp.inf, jnp.where(d >= 0, imp, -jnp.inf))
    n_sel = min(N_SEL, nb)
    _, idx = lax.top_k(score, n_sel)
    bi = jnp.arange(B)[:, None, None, None]
    ki = jnp.arange(KV_HEADS)[None, :, None, None]
    ksb = ks.reshape(B, nb, SEL_BLOCK, KV_HEADS, HEAD_DIM).transpose(0, 3, 1, 2, 4)
    vsb = vs.reshape(B, nb, SEL_BLOCK, KV_HEADS, HEAD_DIM).transpose(0, 3, 1, 2, 4)
    kg = ksb[bi, ki, idx].reshape(B, KV_HEADS, Q, n_sel * SEL_BLOCK, HEAD_DIM)
    vg = vsb[bi, ki, idx].reshape(B, KV_HEADS, Q, n_sel * SEL_BLOCK, HEAD_DIM)
    kpos = (idx[..., None] * SEL_BLOCK + jnp.arange(SEL_BLOCK)).reshape(B, KV_HEADS, Q, n_sel * SEL_BLOCK)
    rel_s = q_pos[None, None, :, None] - kpos
    tb = rel_bias.reshape(N_BUCKETS, KV_HEADS, Q_PER_KV)
    bs = tb[rel_bucket(rel_s), ki].transpose(0, 1, 4, 2, 3)
    ls = jnp.einsum('bqkgd,bkqjd->bkgqj', qh, kg) * scale + bs
    ps = masked_softmax(ls, (rel_s >= 0)[:, :, None])
    os_ = jnp.einsum('bkgqj,bkqjd->bqkgd', ps.astype(vg.dtype), vg)
    rel_w = q_pos[:, None] - kw_pos[None, :]
    mw = (rel_w >= 0) & (rel_w <= WINDOW) & (kw_pos[None, :] >= 0)
    lw = jnp.einsum('bqkgd,bwkd->bkgqw', qh, kw) * scale + head_bias(rel_w, rel_bias)
    pw = masked_softmax(lw, mw)
    ow = jnp.einsum('bkgqw,bwkd->bqkgd', pw.astype(vw.dtype), vw)
    g = gates.reshape(B, Q, KV_HEADS, Q_PER_KV, 3).astype(oc.dtype)
    o = g[..., 0:1] * oc + g[..., 1:2] * os_ + g[..., 2:3] * ow
    return o.reshape(B, Q, N_HEADS * HEAD_DIM)


def nsa_prompt(xn, w_in, w_out, q_gain, k_gain, pe, w1, w2, rel_bias):
    B, S, _ = xn.shape
    q, kcr, vcr, ks, vs, kw, vw, gates = nsa_project(xn, w_in, q_gain, k_gain)
    kc, vc, kc_end = compressed_kv(kcr, vcr, k_gain[0], pe, w1, w2)
    pad = ((0, 0), (WINDOW, 0), (0, 0), (0, 0))
    kw_pad = jnp.pad(kw, pad)
    vw_pad = jnp.pad(vw, pad)

    def block(i):
        q0 = i * Q_BLOCK
        qb = lax.dynamic_slice_in_dim(q, q0, Q_BLOCK, 1)
        gb = lax.dynamic_slice_in_dim(gates, q0, Q_BLOCK, 1)
        kwb = lax.dynamic_slice_in_dim(kw_pad, q0, WINDOW + Q_BLOCK, 1)
        vwb = lax.dynamic_slice_in_dim(vw_pad, q0, WINDOW + Q_BLOCK, 1)
        q_pos = q0 + jnp.arange(Q_BLOCK)
        kw_pos = q0 - WINDOW + jnp.arange(WINDOW + Q_BLOCK)
        return nsa_core(qb, q_pos, kc, vc, kc_end, ks, vs, kwb, vwb, kw_pos, gb, rel_bias)

    o = lax.map(block, jnp.arange(S // Q_BLOCK))
    o = o.transpose(1, 0, 2, 3).reshape(B, S, N_HEADS * HEAD_DIM)
    w_keep = min(WINDOW, S)
    new_rows = jnp.stack([kcr, vcr, ks, vs], axis=2)
    new_win = jnp.stack([kw[:, S - w_keep:], vw[:, S - w_keep:]], axis=2)
    return o @ w_out, new_rows, new_win


def nsa_sample(xn, past_rows, win_buf, w_in, w_out, q_gain, k_gain, pe, w1, w2, rel_bias):
    B, L, _ = xn.shape
    P = past_rows.shape[1]
    q, kcr, vcr, ks, vs, kw, vw, gates = nsa_project(xn, w_in, q_gain, k_gain)
    new_rows = jnp.stack([kcr, vcr, ks, vs], axis=2)
    rows = jnp.concatenate([past_rows, new_rows], axis=1)
    kc, vc, kc_end = compressed_kv(rows[:, :, 0], rows[:, :, 1], k_gain[0], pe, w1, w2)
    T = P + L
    tp = -(-T // SEL_BLOCK) * SEL_BLOCK
    pad = ((0, 0), (0, tp - T), (0, 0), (0, 0))
    ks_all = jnp.pad(rows[:, :, 2], pad)
    vs_all = jnp.pad(rows[:, :, 3], pad)
    win_all = jnp.concatenate([win_buf, jnp.stack([kw, vw], axis=2)], axis=1)
    wb = win_buf.shape[1]
    kw_pos = P - wb + jnp.arange(wb + L)
    q_pos = P + jnp.arange(L)
    o = nsa_core(q, q_pos, kc, vc, kc_end, ks_all, vs_all, win_all[:, :, 0], win_all[:, :, 1],
                 kw_pos, gates, rel_bias)
    w_keep = min(WINDOW, wb + L)
    return o @ w_out, new_rows, win_all[:, wb + L - w_keep:]


def gmlp_mixer(xn, w_in, b_in, ln_g, ln_b, w_s, b_s, w_out):
    B, L, _ = xn.shape
    z = jax.nn.gelu(xn @ w_in + b_in)
    u = z[..., :GM_WIDTH]
    v = layernorm(z[..., GM_WIDTH:], ln_g, ln_b)
    lp = -(-L // CHUNK) * CHUNK
    vp = jnp.pad(v, ((0, 0), (0, lp - L), (0, 0))).reshape(B, lp // CHUNK, CHUNK, GM_GROUPS, GM_GROUP_DIM)
    causal = jnp.tril(jnp.ones((CHUNK, CHUNK), dtype=bool))
    ws = jnp.where(causal[None], w_s, 0)
    s = jnp.einsum('gts,bcsgd->bctgd', ws, vp) + b_s.T[None, None, :, :, None]
    s = s.reshape(B, lp, GM_WIDTH)[:, :L]
    start = ((L - 1) // CHUNK) * CHUNK
    return (u * s) @ w_out, v[:, start:]


def hier_moe(x, w_grp, b_grp, w_exp, b_exp, w_gu, w_dn):
    T, D = x.shape
    pg = jax.nn.softmax((x @ w_grp + b_grp).astype(jnp.float32), axis=-1)
    grp = jnp.argmax(pg, axis=-1)
    g_w = jnp.take_along_axis(pg, grp[:, None], axis=-1)
    le = (x @ w_exp + b_exp).astype(jnp.float32).reshape(T, N_GROUPS, EXPERTS_PER_GROUP)
    le = jnp.take_along_axis(le, grp[:, None, None], axis=1)[:, 0]
    top_v, top_i = lax.top_k(le, TOP_K)
    wts = (jax.nn.softmax(top_v, axis=-1) * g_w).reshape(-1)
    eid = (grp[:, None] * EXPERTS_PER_GROUP + top_i).reshape(-1)
    n = T * TOP_K
    order = jnp.argsort(eid)
    e_s = eid[order]
    tok_s = order // TOP_K
    w_s = wts[order]
    counts = jnp.bincount(eid, length=N_EXPERTS)
    padded = (counts + MOE_BLOCK - 1) // MOE_BLOCK * MOE_BLOCK
    pad_end = jnp.cumsum(padded)
    pad_start = pad_end - padded
    start = jnp.cumsum(counts) - counts
    dest = pad_start[e_s] + jnp.arange(n) - start[e_s]
    n_blocks = -(-n // MOE_BLOCK) + N_EXPERTS
    xbuf = jnp.zeros((n_blocks * MOE_BLOCK, D), x.dtype).at[dest].set(x[tok_s])
    blk_e = jnp.minimum(jnp.searchsorted(pad_end, jnp.arange(n_blocks) * MOE_BLOCK, side='right'), N_EXPERTS - 1)

    def expert(args):
        xb, e = args
        gu = xb @ w_gu[e]
        return (jax.nn.silu(gu[:, :D_EXPERT]) * gu[:, D_EXPERT:]) @ w_dn[e]

    ybuf = lax.map(expert, (xbuf.reshape(n_blocks, MOE_BLOCK, D), blk_e)).reshape(-1, D)
    return jax.ops.segment_sum(ybuf[dest] * w_s[:, None].astype(x.dtype), tok_s, num_segments=T)


def setup_inputs(seed: int = 0) -> dict:
    key = jax.random.key(seed)
    ks = jax.random.split(key, 32)
    f32 = jnp.float32

    def nrm(k, shape, s):
        return jax.random.normal(k, shape, f32) * s

    n_pages = PAST_LEN // PAGE_SIZE
    n_pool = (5 * DEC_BATCH * n_pages + 3) // 4
    w_buf = min(WINDOW, PAST_LEN)
    page_table = jax.random.permutation(ks[3], n_pool)[:DEC_BATCH * n_pages].reshape(DEC_BATCH, n_pages).astype(jnp.int32)
    return {
        'x_prompt': nrm(ks[0], (BATCH, SEQ, D_MODEL), 1.0),
        'x_sample': nrm(ks[1], (DEC_BATCH, DEC_SEQ, D_MODEL), 1.0),
        'cache_nsa_kv': nrm(ks[2], (n_pool, PAGE_SIZE, N_ATTN_LAYERS, 4, KV_HEADS, HEAD_DIM), 1.0),
        'state_win_kv': nrm(ks[4], (N_ATTN_LAYERS, DEC_BATCH, w_buf, 2, KV_HEADS, HEAD_DIM), 1.0),
        'page_table': page_table,
        'rel_bias': nrm(ks[5], (N_BUCKETS, N_HEADS), 0.3),
        'ln_mix': 1.0 + nrm(ks[6], (DEPTH, D_MODEL), 0.1),
        'ln_ffn': 1.0 + nrm(ks[7], (DEPTH, D_MODEL), 0.1),
        'nsa_w_in': nrm(ks[8], (N_ATTN_LAYERS, D_MODEL, NSA_IN), D_MODEL ** -0.5),
        'nsa_w_out': nrm(ks[9], (N_ATTN_LAYERS, N_HEADS * HEAD_DIM, D_MODEL), (N_HEADS * HEAD_DIM) ** -0.5),
        'nsa_q_gain': 1.0 + nrm(ks[10], (N_ATTN_LAYERS, HEAD_DIM), 0.1),
        'nsa_k_gain': 1.0 + nrm(ks[11], (N_ATTN_LAYERS, 3, HEAD_DIM), 0.1),
        'cmp_pe': nrm(ks[12], (N_ATTN_LAYERS, 2, CMP_LEN, HEAD_DIM), 0.3),
        'cmp_w1': nrm(ks[13], (N_ATTN_LAYERS, 2, CMP_LEN * HEAD_DIM, CMP_HID), (CMP_LEN * HEAD_DIM) ** -0.5),
        'cmp_w2': nrm(ks[14], (N_ATTN_LAYERS, 2, CMP_HID, HEAD_DIM), CMP_HID ** -0.5),
        'gm_w_in': nrm(ks[15], (N_GM_LAYERS, D_MODEL, 2 * GM_WIDTH), D_MODEL ** -0.5),
        'gm_b_in': nrm(ks[16], (N_GM_LAYERS, 2 * GM_WIDTH), 0.01),
        'gm_ln_g': 1.0 + nrm(ks[17], (N_GM_LAYERS, GM_WIDTH), 0.1),
        'gm_ln_b': nrm(ks[18], (N_GM_LAYERS, GM_WIDTH), 0.01),
        'gm_w_s': nrm(ks[19], (N_GM_LAYERS, GM_GROUPS, CHUNK, CHUNK), CHUNK ** -0.5),
        'gm_b_s': 1.0 + nrm(ks[20], (N_GM_LAYERS, GM_GROUPS, CHUNK), 0.1),
        'gm_w_out': nrm(ks[21], (N_GM_LAYERS, GM_WIDTH, D_MODEL), GM_WIDTH ** -0.5),
        'moe_w_grp': nrm(ks[22], (DEPTH, D_MODEL, N_GROUPS), D_MODEL ** -0.5),
        'moe_b_grp': nrm(ks[23], (DEPTH, N_GROUPS), 0.01),
        'moe_w_exp': nrm(ks[24], (DEPTH, D_MODEL, N_EXPERTS), D_MODEL ** -0.5),
        'moe_b_exp': nrm(ks[25], (DEPTH, N_EXPERTS), 0.01),
        'moe_w_gu': nrm(ks[26], (DEPTH, N_EXPERTS, D_MODEL, 2 * D_EXPERT), D_MODEL ** -0.5),
        'moe_w_dn': nrm(ks[27], (DEPTH, N_EXPERTS, D_EXPERT, D_MODEL), D_EXPERT ** -0.5),
    }


def reference(x_prompt, x_sample, cache_nsa_kv, state_win_kv, page_table, rel_bias, ln_mix, ln_ffn,
              nsa_w_in, nsa_w_out, nsa_q_gain, nsa_k_gain, cmp_pe, cmp_w1, cmp_w2,
              gm_w_in, gm_b_in, gm_ln_g, gm_ln_b, gm_w_s, gm_b_s, gm_w_out,
              moe_w_grp, moe_b_grp, moe_w_exp, moe_b_exp, moe_w_gu, moe_w_dn):
    xp = x_prompt
    xs = x_sample
    dec_b = x_sample.shape[0]
    past_len = page_table.shape[1] * PAGE_SIZE
    kv_p, kv_s, win_p, win_s, gv_p, gv_s = [], [], [], [], [], []
    for i in range(DEPTH):
        a = i // N_MIXERS
        hp = rmsnorm(xp, ln_mix[i])
        hs = rmsnorm(xs, ln_mix[i])
        if i % N_MIXERS == 0:
            yp, rp, wp = nsa_prompt(hp, nsa_w_in[a], nsa_w_out[a], nsa_q_gain[a], nsa_k_gain[a],
                                    cmp_pe[a], cmp_w1[a], cmp_w2[a], rel_bias)
            past = jnp.take(cache_nsa_kv, page_table, axis=0)[:, :, :, a]
            past = past.reshape(dec_b, past_len, 4, KV_HEADS, HEAD_DIM)
            ys, rs, ws = nsa_sample(hs, past, state_win_kv[a], nsa_w_in[a], nsa_w_out[a], nsa_q_gain[a],
                                    nsa_k_gain[a], cmp_pe[a], cmp_w1[a], cmp_w2[a], rel_bias)
            kv_p.append(rp)
            kv_s.append(rs)
            win_p.append(wp)
            win_s.append(ws)
        else:
            yp, vp = gmlp_mixer(hp, gm_w_in[a], gm_b_in[a], gm_ln_g[a], gm_ln_b[a], gm_w_s[a], gm_b_s[a], gm_w_out[a])
            ys, vs = gmlp_mixer(hs, gm_w_in[a], gm_b_in[a], gm_ln_g[a], gm_ln_b[a], gm_w_s[a], gm_b_s[a], gm_w_out[a])
            gv_p.append(vp)
            gv_s.append(vs)
        xp = xp + yp
        xs = xs + ys
        hp = rmsnorm(xp, ln_ffn[i])
        hs = rmsnorm(xs, ln_ffn[i])
        xp = xp + hier_moe(hp.reshape(-1, D_MODEL), moe_w_grp[i], moe_b_grp[i], moe_w_exp[i], moe_b_exp[i],
                           moe_w_gu[i], moe_w_dn[i]).reshape(xp.shape)
        xs = xs + hier_moe(hs.reshape(-1, D_MODEL), moe_w_grp[i], moe_b_grp[i], moe_w_exp[i], moe_b_exp[i],
                           moe_w_gu[i], moe_w_dn[i]).reshape(xs.shape)
    new_kv_prompt = jnp.stack(kv_p, axis=2)
    new_kv_sample = jnp.stack(kv_s, axis=2)
    new_win_prompt = jnp.stack(win_p, axis=0)
    new_win_sample = jnp.stack(win_s, axis=0)
    new_gm_v_prompt = jnp.stack(gv_p, axis=0)
    new_gm_v_sample = jnp.stack(gv_s, axis=0)
    return (xp, xs, new_kv_prompt, new_kv_sample, new_win_prompt, new_win_sample, new_gm_v_prompt, new_gm_v_sample)
```

```python
import functools
import math

import jax
import jax.numpy as jnp
import numpy as np
from jax import lax
from jax.experimental import pallas as pl
from jax.experimental.pallas import tpu as pltpu

D_MODEL = 1024
PAGE_SIZE = 128
DEPTH = 2
N_MIXERS = 2
N_HEADS = 16
HEAD_DIM = 64
KV_HEADS = 4
Q_PER_KV = N_HEADS // KV_HEADS
CMP_LEN = 32
CMP_STRIDE = 16
CMP_HID = 64
SEL_BLOCK = 64
CMP_PER_SEL = SEL_BLOCK // CMP_STRIDE
N_SEL = 16
N_LOCAL = 2
WINDOW = 512
Q_BLOCK = 128
N_BUCKETS = 32
MAX_DISTANCE = 128
CHUNK = 128
GM_WIDTH = 2048
GM_GROUPS = 8
GM_GROUP_DIM = GM_WIDTH // GM_GROUPS
N_GROUPS = 4
EXPERTS_PER_GROUP = 8
N_EXPERTS = N_GROUPS * EXPERTS_PER_GROUP
TOP_K = 2
D_EXPERT = 512
MOE_BLOCK = 128
EPS = 1e-6


def _mm_kernel(x_ref, w_ref, o_ref):
    o_ref[...] = jnp.dot(x_ref[...].astype(jnp.bfloat16), w_ref[...].astype(jnp.bfloat16),
                         preferred_element_type=jnp.float32)


def _mm(x, w, tm=512, tn=512):
    m, k = x.shape
    n = w.shape[1]
    tm = min(tm, m)
    tn = min(tn, n)
    return pl.pallas_call(
        _mm_kernel,
        grid=(m // tm, n // tn),
        in_specs=[pl.BlockSpec((tm, k), lambda i, j: (i, 0)),
                  pl.BlockSpec((k, tn), lambda i, j: (0, j))],
        out_specs=pl.BlockSpec((tm, tn), lambda i, j: (i, j)),
        out_shape=jax.ShapeDtypeStruct((m, n), jnp.float32),
        name="mm",
    )(x, w)


def rmsnorm(x, g):
    xf = x.astype(jnp.float32)
    y = xf * lax.rsqrt(jnp.mean(xf * xf, axis=-1, keepdims=True) + EPS)
    return (y * g.astype(jnp.float32)).astype(x.dtype)


def layernorm(x, g, b):
    xf = x.astype(jnp.float32)
    mu = jnp.mean(xf, axis=-1, keepdims=True)
    var = jnp.mean(jnp.square(xf - mu), axis=-1, keepdims=True)
    return ((xf - mu) * lax.rsqrt(var + EPS) * g.astype(jnp.float32) + b.astype(jnp.float32)).astype(x.dtype)


def rel_bucket(n):
    n = jnp.maximum(n, 0)
    max_exact = N_BUCKETS // 2
    nf = jnp.maximum(n, 1).astype(jnp.float32)
    large = max_exact + (jnp.log(nf / max_exact) / math.log(MAX_DISTANCE / max_exact)
                         * (N_BUCKETS - max_exact)).astype(jnp.int32)
    large = jnp.minimum(large, N_BUCKETS - 1)
    return jnp.where(n < max_exact, n, large)


def head_bias(rel, rel_bias):
    q, k = rel.shape
    b = rel_bias[rel_bucket(rel)]
    return b.reshape(q, k, KV_HEADS, Q_PER_KV).transpose(2, 3, 0, 1)


def masked_softmax(logits, mask):
    z = jnp.where(mask, logits.astype(jnp.float32), -1e30)
    return jax.nn.softmax(z, axis=-1) * mask


def nsa_project(xn, w_in, q_gain, k_gain):
    B, T, _ = xn.shape
    hq = N_HEADS * HEAD_DIM
    hkv = 6 * KV_HEADS * HEAD_DIM
    h = xn @ w_in
    q = rmsnorm(h[..., :hq].reshape(B, T, N_HEADS, HEAD_DIM), q_gain)
    kv = h[..., hq:hq + hkv].reshape(B, T, 6, KV_HEADS, HEAD_DIM)
    gates = jax.nn.sigmoid(h[..., hq + hkv:].astype(jnp.float32)).reshape(B, T, N_HEADS, 3)
    k_sel = rmsnorm(kv[:, :, 2], k_gain[1])
    k_win = rmsnorm(kv[:, :, 4], k_gain[2])
    return q, kv[:, :, 0], kv[:, :, 1], k_sel, kv[:, :, 3], k_win, kv[:, :, 5], gates


def compress(rows, pe, w1, w2):
    B, T = rows.shape[:2]
    n_ch = T // CMP_STRIDE
    c = rows[:, :n_ch * CMP_STRIDE].reshape(B, n_ch, CMP_STRIDE, KV_HEADS, HEAD_DIM)
    blk = jnp.concatenate([c[:, :-1], c[:, 1:]], axis=2) + pe[None, None, :, None, :]
    h = jax.nn.gelu(jnp.einsum('bnlkd,ldf->bnkf', blk, w1.reshape(CMP_LEN, HEAD_DIM, CMP_HID)))
    return jnp.einsum('bnkf,fd->bnkd', h, w2)


def compressed_kv(k_raw, v_raw, k_gain0, pe, w1, w2):
    kc = rmsnorm(compress(k_raw, pe[0], w1[0], w2[0]), k_gain0)
    vc = compress(v_raw, pe[1], w1[1], w2[1])
    end = jnp.arange(kc.shape[1]) * CMP_STRIDE + (CMP_LEN - 1)
    return kc, vc, end


def nsa_core(q, q_pos, kc, vc, kc_end, ks, vs, kw, vw, kw_pos, gates, rel_bias):
    B, Q = q.shape[:2]
    nc = kc.shape[1]
    scale = HEAD_DIM ** -0.5
    qh = q.reshape(B, Q, KV_HEADS, Q_PER_KV, HEAD_DIM)
    lc = jnp.einsum('bqkgd,bnkd->bkgqn', qh, kc) * scale + head_bias(q_pos[:, None] - kc_end[None, :], rel_bias)
    pc = masked_softmax(lc, kc_end[None, :] <= q_pos[:, None])
    oc = jnp.einsum('bkgqn,bnkd->bqkgd', pc.astype(vc.dtype), vc)
    nb = ks.shape[1] // SEL_BLOCK
    imp = jnp.pad(pc.sum(axis=2), ((0, 0), (0, 0), (0, 0), (0, nb * CMP_PER_SEL - nc)))
    imp = imp.reshape(B, KV_HEADS, Q, nb, CMP_PER_SEL).sum(-1)
    blk = jnp.arange(nb)
    d = (q_pos // SEL_BLOCK)[:, None] - blk[None, :]
    forced = (blk[None, :] == 0) | ((d >= 0) & (d < N_LOCAL))
    score = jnp.where(forced, jnp.inf, jnp.where(d >= 0, imp, -jnp.inf))
    n_sel = min(N_SEL, nb)
    _, idx = lax.top_k(score, n_sel)
    bi = jnp.arange(B)[:, None, None, None]
    ki = jnp.arange(KV_HEADS)[None, :, None, None]
    ksb = ks.reshape(B, nb, SEL_BLOCK, KV_HEADS, HEAD_DIM).transpose(0, 3, 1, 2, 4)
    vsb = vs.reshape(B, nb, SEL_BLOCK, KV_HEADS, HEAD_DIM).transpose(0, 3, 1, 2, 4)
    kg = ksb[bi, ki, idx].reshape(B, KV_HEADS, Q, n_sel * SEL_BLOCK, HEAD_DIM)
    vg = vsb[bi, ki, idx].reshape(B, KV_HEADS, Q, n_sel * SEL_BLOCK, HEAD_DIM)
    kpos = (idx[..., None] * SEL_BLOCK + jnp.arange(SEL_BLOCK)).reshape(B, KV_HEADS, Q, n_sel * SEL_BLOCK)
    rel_s = q_pos[None, None, :, None] - kpos
    tb = rel_bias.reshape(N_BUCKETS, KV_HEADS, Q_PER_KV)
    bs = tb[rel_bucket(rel_s), ki].transpose(0, 1, 4, 2, 3)
    ls = jnp.einsum('bqkgd,bkqjd->bkgqj', qh, kg) * scale + bs
    ps = masked_softmax(ls, (rel_s >= 0)[:, :, None])
    os_ = jnp.einsum('bkgqj,bkqjd->bqkgd', ps.astype(vg.dtype), vg)
    rel_w = q_pos[:, None] - kw_pos[None, :]
    mw = (rel_w >= 0) & (rel_w <= WINDOW) & (kw_pos[None, :] >= 0)
    lw = jnp.einsum('bqkgd,bwkd->bkgqw', qh, kw) * scale + head_bias(rel_w, rel_bias)
    pw = masked_softmax(lw, mw)
    ow = jnp.einsum('bkgqw,bwkd->bqkgd', pw.astype(vw.dtype), vw)
    g = gates.reshape(B, Q, KV_HEADS, Q_PER_KV, 3).astype(oc.dtype)
    o = g[..., 0:1] * oc + g[..., 1:2] * os_ + g[..., 2:3] * ow
    return o.reshape(B, Q, N_HEADS * HEAD_DIM)


def nsa_prompt(xn, w_in, w_out, q_gain, k_gain, pe, w1, w2, rel_bias):
    B, S, _ = xn.shape
    q, kcr, vcr, ks, vs, kw, vw, gates = nsa_project(xn, w_in, q_gain, k_gain)
    kc, vc, kc_end = compressed_kv(kcr, vcr, k_gain[0], pe, w1, w2)
    pad = ((0, 0), (WINDOW, 0), (0, 0), (0, 0))
    kw_pad = jnp.pad(kw, pad)
    vw_pad = jnp.pad(vw, pad)

    def block(i):
        q0 = i * Q_BLOCK
        qb = lax.dynamic_slice_in_dim(q, q0, Q_BLOCK, 1)
        gb = lax.dynamic_slice_in_dim(gates, q0, Q_BLOCK, 1)
        kwb = lax.dynamic_slice_in_dim(kw_pad, q0, WINDOW + Q_BLOCK, 1)
        vwb = lax.dynamic_slice_in_dim(vw_pad, q0, WINDOW + Q_BLOCK, 1)
        q_pos = q0 + jnp.arange(Q_BLOCK)
        kw_pos = q0 - WINDOW + jnp.arange(WINDOW + Q_BLOCK)
        return nsa_core(qb, q_pos, kc, vc, kc_end, ks, vs, kwb, vwb, kw_pos, gb, rel_bias)

    o = lax.map(block, jnp.arange(S // Q_BLOCK))
    o = o.transpose(1, 0, 2, 3).reshape(B, S, N_HEADS * HEAD_DIM)
    w_keep = min(WINDOW, S)
    new_rows = jnp.stack([kcr, vcr, ks, vs], axis=2)
    new_win = jnp.stack([kw[:, S - w_keep:], vw[:, S - w_keep:]], axis=2)
    y = _mm(o.reshape(B * S, -1), w_out).reshape(B, S, -1)
    return y, new_rows, new_win


def nsa_sample(xn, past_rows, win_buf, w_in, w_out, q_gain, k_gain, pe, w1, w2, rel_bias):
    B, L, _ = xn.shape
    P = past_rows.shape[1]
    q, kcr, vcr, ks, vs, kw, vw, gates = nsa_project(xn, w_in, q_gain, k_gain)
    new_rows = jnp.stack([kcr, vcr, ks, vs], axis=2)
    rows = jnp.concatenate([past_rows, new_rows], axis=1)
    kc, vc, kc_end = compressed_kv(rows[:, :, 0], rows[:, :, 1], k_gain[0], pe, w1, w2)
    T = P + L
    tp = -(-T // SEL_BLOCK) * SEL_BLOCK
    pad = ((0, 0), (0, tp - T), (0, 0), (0, 0))
    ks_all = jnp.pad(rows[:, :, 2], pad)
    vs_all = jnp.pad(rows[:, :, 3], pad)
    win_all = jnp.concatenate([win_buf, jnp.stack([kw, vw], axis=2)], axis=1)
    wb = win_buf.shape[1]
    kw_pos = P - wb + jnp.arange(wb + L)
    q_pos = P + jnp.arange(L)
    o = nsa_core(q, q_pos, kc, vc, kc_end, ks_all, vs_all, win_all[:, :, 0], win_all[:, :, 1],
                 kw_pos, gates, rel_bias)
    w_keep = min(WINDOW, wb + L)
    return o @ w_out, new_rows, win_all[:, wb + L - w_keep:]


def gmlp_mixer(xn, w_in, b_in, ln_g, ln_b, w_s, b_s, w_out):
    B, L, _ = xn.shape
    z = jax.nn.gelu(xn @ w_in + b_in)
    u = z[..., :GM_WIDTH]
    v = layernorm(z[..., GM_WIDTH:], ln_g, ln_b)
    lp = -(-L // CHUNK) * CHUNK
    vp = jnp.pad(v, ((0, 0), (0, lp - L), (0, 0))).reshape(B, lp // CHUNK, CHUNK, GM_GROUPS, GM_GROUP_DIM)
    causal = jnp.tril(jnp.ones((CHUNK, CHUNK), dtype=bool))
    ws = jnp.where(causal[None], w_s, 0)
    s = jnp.einsum('gts,bcsgd->bctgd', ws, vp) + b_s.T[None, None, :, :, None]
    s = s.reshape(B, lp, GM_WIDTH)[:, :L]
    start = ((L - 1) // CHUNK) * CHUNK
    return (u * s) @ w_out, v[:, start:]


def hier_moe(x, w_grp, b_grp, w_exp, b_exp, w_gu, w_dn):
    T, D = x.shape
    pg = jax.nn.softmax((x @ w_grp + b_grp).astype(jnp.float32), axis=-1)
    grp = jnp.argmax(pg, axis=-1)
    g_w = jnp.take_along_axis(pg, grp[:, None], axis=-1)
    le = (x @ w_exp + b_exp).astype(jnp.float32).reshape(T, N_GROUPS, EXPERTS_PER_GROUP)
    le = jnp.take_along_axis(le, grp[:, None, None], axis=1)[:, 0]
    top_v, top_i = lax.top_k(le, TOP_K)
    wts = (jax.nn.softmax(top_v, axis=-1) * g_w).reshape(-1)
    eid = (grp[:, None] * EXPERTS_PER_GROUP + top_i).reshape(-1)
    n = T * TOP_K
    order = jnp.argsort(eid)
    e_s = eid[order]
    tok_s = order // TOP_K
    w_s = wts[order]
    counts = jnp.bincount(eid, length=N_EXPERTS)
    padded = (counts + MOE_BLOCK - 1) // MOE_BLOCK * MOE_BLOCK
    pad_end = jnp.cumsum(padded)
    pad_start = pad_end - padded
    start = jnp.cumsum(counts) - counts
    dest = pad_start[e_s] + jnp.arange(n) - start[e_s]
    n_blocks = -(-n // MOE_BLOCK) + N_EXPERTS
    xbuf = jnp.zeros((n_blocks * MOE_BLOCK, D), x.dtype).at[dest].set(x[tok_s])
    blk_e = jnp.minimum(jnp.searchsorted(pad_end, jnp.arange(n_blocks) * MOE_BLOCK, side='right'), N_EXPERTS - 1)

    def expert(args):
        xb, e = args
        gu = xb @ w_gu[e]
        return (jax.nn.silu(gu[:, :D_EXPERT]) * gu[:, D_EXPERT:]) @ w_dn[e]

    ybuf = lax.map(expert, (xbuf.reshape(n_blocks, MOE_BLOCK, D), blk_e)).reshape(-1, D)
    return jax.ops.segment_sum(ybuf[dest] * w_s[:, None].astype(x.dtype), tok_s, num_segments=T)


def kernel(x_prompt, x_sample, cache_nsa_kv, state_win_kv, page_table, rel_bias, ln_mix, ln_ffn,
           nsa_w_in, nsa_w_out, nsa_q_gain, nsa_k_gain, cmp_pe, cmp_w1, cmp_w2,
           gm_w_in, gm_b_in, gm_ln_g, gm_ln_b, gm_w_s, gm_b_s, gm_w_out,
           moe_w_grp, moe_b_grp, moe_w_exp, moe_b_exp, moe_w_gu, moe_w_dn):
    xp = x_prompt
    xs = x_sample
    dec_b = x_sample.shape[0]
    past_len = page_table.shape[1] * PAGE_SIZE
    kv_p, kv_s, win_p, win_s, gv_p, gv_s = [], [], [], [], [], []
    for i in range(DEPTH):
        a = i // N_MIXERS
        hp = rmsnorm(xp, ln_mix[i])
        hs = rmsnorm(xs, ln_mix[i])
        if i % N_MIXERS == 0:
            yp, rp, wp = nsa_prompt(hp, nsa_w_in[a], nsa_w_out[a], nsa_q_gain[a], nsa_k_gain[a],
                                    cmp_pe[a], cmp_w1[a], cmp_w2[a], rel_bias)
            past = jnp.take(cache_nsa_kv, page_table, axis=0)[:, :, :, a]
            past = past.reshape(dec_b, past_len, 4, KV_HEADS, HEAD_DIM)
            ys, rs, ws = nsa_sample(hs, past, state_win_kv[a], nsa_w_in[a], nsa_w_out[a], nsa_q_gain[a],
                                    nsa_k_gain[a], cmp_pe[a], cmp_w1[a], cmp_w2[a], rel_bias)
            kv_p.append(rp)
            kv_s.append(rs)
            win_p.append(wp)
            win_s.append(ws)
        else:
            yp, vp = gmlp_mixer(hp, gm_w_in[a], gm_b_in[a], gm_ln_g[a], gm_ln_b[a], gm_w_s[a], gm_b_s[a], gm_w_out[a])
            ys, vs = gmlp_mixer(hs, gm_w_in[a], gm_b_in[a], gm_ln_g[a], gm_ln_b[a], gm_w_s[a], gm_b_s[a], gm_w_out[a])
            gv_p.append(vp)
            gv_s.append(vs)
        xp = xp + yp
        xs = xs + ys
        hp = rmsnorm(xp, ln_ffn[i])
        hs = rmsnorm(xs, ln_ffn[i])
        xp = xp + hier_moe(hp.reshape(-1, D_MODEL), moe_w_grp[i], moe_b_grp[i], moe_w_exp[i], moe_b_exp[i],
                           moe_w_gu[i], moe_w_dn[i]).reshape(xp.shape)
        xs = xs + hier_moe(hs.reshape(-1, D_MODEL), moe_w_grp[i], moe_b_grp[i], moe_w_exp[i], moe_b_exp[i],
                           moe_w_gu[i], moe_w_dn[i]).reshape(xs.shape)
    new_kv_prompt = jnp.stack(kv_p, axis=2)
    new_kv_sample = jnp.stack(kv_s, axis=2)
    new_win_prompt = jnp.stack(win_p, axis=0)
    new_win_sample = jnp.stack(win_s, axis=0)
    new_gm_v_prompt = jnp.stack(gv_p, axis=0)
    new_gm_v_sample = jnp.stack(gv_s, axis=0)
    return (xp, xs, new_kv_prompt, new_kv_sample, new_win_prompt, new_win_sample, new_gm_v_prompt, new_gm_v_sample)
```

```python
import functools
import math

import jax
import jax.numpy as jnp
import numpy as np
from jax import lax
from jax.experimental import pallas as pl
from jax.experimental.pallas import tpu as pltpu

D_MODEL = 1024
PAGE_SIZE = 128
DEPTH = 2
N_MIXERS = 2
N_HEADS = 16
HEAD_DIM = 64
KV_HEADS = 4
Q_PER_KV = N_HEADS // KV_HEADS
CMP_LEN = 32
CMP_STRIDE = 16
CMP_HID = 64
SEL_BLOCK = 64
CMP_PER_SEL = SEL_BLOCK // CMP_STRIDE
N_SEL = 16
N_LOCAL = 2
WINDOW = 512
Q_BLOCK = 128
N_BUCKETS = 32
MAX_DISTANCE = 128
CHUNK = 128
GM_WIDTH = 2048
GM_GROUPS = 8
GM_GROUP_DIM = GM_WIDTH // GM_GROUPS
N_GROUPS = 4
EXPERTS_PER_GROUP = 8
N_EXPERTS = N_GROUPS * EXPERTS_PER_GROUP
TOP_K = 2
D_EXPERT = 512
MOE_BLOCK = 128
EPS = 1e-6


def _mm_kernel(x_ref, w_ref, o_ref):
    o_ref[...] = jnp.dot(x_ref[...].astype(jnp.bfloat16), w_ref[...].astype(jnp.bfloat16),
                         preferred_element_type=jnp.float32)


def _mm(x, w, tm=512, tn=512):
    m, k = x.shape
    n = w.shape[1]
    tm = min(tm, m)
    tn = min(tn, n)
    return pl.pallas_call(
        _mm_kernel,
        grid=(m // tm, n // tn),
        in_specs=[pl.BlockSpec((tm, k), lambda i, j: (i, 0)),
                  pl.BlockSpec((k, tn), lambda i, j: (0, j))],
        out_specs=pl.BlockSpec((tm, tn), lambda i, j: (i, j)),
        out_shape=jax.ShapeDtypeStruct((m, n), jnp.float32),
        name="mm",
    )(x, w)


NEG = -1e30
MXU_DTYPE = jnp.bfloat16
TQ = 256
TK = 256
LANES = 128
BIAS_SPAN = 128


def _bucket_of_distance():
    d = np.arange(BIAS_SPAN)
    max_exact = N_BUCKETS // 2
    nf = np.maximum(d, 1).astype(np.float32)
    large = max_exact + (np.log(nf / np.float32(max_exact)) / np.float32(math.log(MAX_DISTANCE / max_exact))
                         * np.float32(N_BUCKETS - max_exact)).astype(np.int32)
    large = np.minimum(large, N_BUCKETS - 1)
    return np.where(d < max_exact, d, large).astype(np.int32)


def _attn_bias_tiles(rel_bias, seq):
    fd = rel_bias[_bucket_of_distance()].T
    far = fd[:, BIAS_SPAN - 1]
    fd = fd - far[:, None]
    i = np.arange(TQ)[:, None]
    j = np.arange(TK)[None, :]
    d0 = i - j
    b0 = jnp.where(d0 >= 0, jnp.take(fd, np.clip(d0, 0, BIAS_SPAN - 1), axis=1), NEG)
    b1 = jnp.take(fd, np.clip(TQ + i - j, 0, BIAS_SPAN - 1), axis=1)
    w2 = np.where(2 * TQ + i - j <= WINDOW, 0.0, NEG).astype(np.float32)
    nc = seq // CMP_STRIDE
    r = np.arange(2 * nc)[:, None]
    dc = np.arange(TQ)[None, :] - CMP_STRIDE * (r - nc) - (CMP_LEN - 1)
    pat = jnp.where(dc >= 0, jnp.take(fd, np.clip(dc, 0, BIAS_SPAN - 1), axis=1), NEG)
    return b0, b1, jnp.asarray(w2), pat


def _nsa_attn_kernel(q_ref, kc_ref, vct_ref, kst_ref, vs_ref, kwt_ref, vw_ref, oh_ref, g_ref,
                     b0_ref, b1_ref, w2_ref, pat_ref, o_ref, m_sc, l_sc, acc_sc, imp_sc, *, nc, nb):
    f32 = jnp.float32
    qb = pl.program_id(2)

    r0 = pl.multiple_of(nc - (TQ // CMP_STRIDE) * qb, TQ // CMP_STRIDE)
    t_row = qb * TQ + lax.broadcasted_iota(jnp.int32, (1, TQ), 1)
    has_cmp = t_row >= CMP_LEN - 1
    for h in range(Q_PER_KV):
        st = lax.dot_general(kc_ref[0, 0], q_ref[0, h], (((1,), (1,)), ((), ())),
                             preferred_element_type=f32)
        st = st + pat_ref[h, pl.ds(r0, nc), :]
        e = jnp.exp(st - jnp.max(st, axis=0, keepdims=True))
        inv = jnp.where(has_cmp, 1.0 / jnp.sum(e, axis=0, keepdims=True), 0.0)
        pt = e * inv
        for c in range(TQ // LANES):
            part = pt[:, c * LANES:(c + 1) * LANES]
            if h == 0:
                imp_sc[c] = part
            else:
                imp_sc[c] += part
        oct_h = jnp.dot(vct_ref[0, 0], pt.astype(MXU_DTYPE), preferred_element_type=f32)
        o_ref[0, :, h * HEAD_DIM:(h + 1) * HEAD_DIM] = g_ref[0, h][:, 0:1] * oct_h.T

    imp = jnp.concatenate(
        [sum(imp_sc[c, pl.ds(r, nb, stride=CMP_PER_SEL), :] for r in range(CMP_PER_SEL))
         for c in range(TQ // LANES)], axis=1)
    blk = lax.broadcasted_iota(jnp.int32, (nb, TQ), 0)
    t_blk = (qb * TQ + lax.broadcasted_iota(jnp.int32, (nb, TQ), 1)) // SEL_BLOCK
    dd = t_blk - blk
    forced = (blk == 0) | ((dd >= 0) & (dd < N_LOCAL))
    score = jnp.where(forced, jnp.inf, jnp.where(dd >= 0, imp, -jnp.inf))
    pen_t = jnp.full((nb, TQ), NEG, f32)
    for _ in range(min(N_SEL, nb)):
        best = jnp.max(score, axis=0, keepdims=True)
        first = jnp.min(jnp.where(score == best, blk, nb), axis=0, keepdims=True)
        hit = blk == first
        pen_t = jnp.where(hit, 0.0, pen_t)
        score = jnp.where(hit, -jnp.inf, score)
    pen = pen_t.T.astype(MXU_DTYPE)

    def reset():
        m_sc[...] = jnp.full(m_sc.shape, NEG, f32)
        l_sc[...] = jnp.zeros(l_sc.shape, f32)
        acc_sc[...] = jnp.zeros(acc_sc.shape, f32)

    def attend(h, kt, v, extra):
        s = jnp.dot(q_ref[0, h], kt, preferred_element_type=f32) + extra
        m_prev = m_sc[h]
        m_new = jnp.maximum(m_prev, jnp.max(s, axis=1, keepdims=True))
        p = jnp.exp(s - jnp.concatenate([m_new] * (TK // LANES), axis=1))
        alpha = jnp.exp(m_prev - m_new)
        l_sc[h] = alpha * l_sc[h] + jnp.sum(p, axis=1, keepdims=True)
        acc_sc[h] = alpha[:, :HEAD_DIM] * acc_sc[h] + jnp.dot(p.astype(MXU_DTYPE), v,
                                                             preferred_element_type=f32)
        m_sc[h] = m_new

    def emit(gate_col):
        for h in range(Q_PER_KV):
            w = g_ref[0, h][:, gate_col:gate_col + 1] / l_sc[h][:, :HEAD_DIM]
            o_ref[0, :, h * HEAD_DIM:(h + 1) * HEAD_DIM] += w * acc_sc[h]

    def selected_chunk(c, bias_ref):
        mterm = jnp.dot(pen, oh_ref[c], preferred_element_type=f32)
        for h in range(Q_PER_KV):
            extra = mterm if bias_ref is None else mterm + bias_ref[h]
            attend(h, kst_ref[0, 0, c], vs_ref[0, 0, c], extra)

    reset()
    selected_chunk(qb, b0_ref)

    @pl.when(qb >= 1)
    def _():
        selected_chunk(qb - 1, b1_ref)

    def far_body(c, carry):
        selected_chunk(c, None)
        return carry

    lax.fori_loop(0, jnp.maximum(qb - 1, 0), far_body, 0)
    emit(1)

    reset()
    for h in range(Q_PER_KV):
        attend(h, kwt_ref[0, 0, qb], vw_ref[0, 0, qb], b0_ref[h])

    @pl.when(qb >= 1)
    def _():
        for h in range(Q_PER_KV):
            attend(h, kwt_ref[0, 0, qb - 1], vw_ref[0, 0, qb - 1], b1_ref[h])

    @pl.when(qb >= 2)
    def _():
        for h in range(Q_PER_KV):
            attend(h, kwt_ref[0, 0, qb - 2], vw_ref[0, 0, qb - 2], w2_ref[...])

    emit(2)


def _nsa_attention(q, kc, vc, ks, vs, kw, vw, gates, rel_bias):
    B, S = q.shape[:2]
    assert S % TQ == 0 and TQ == TK and WINDOW == 2 * TQ and S // SEL_BLOCK >= N_SEL
    nc = S // CMP_STRIDE
    nb = S // SEL_BLOCK
    nch = S // TK
    cd = MXU_DTYPE
    qs = (q * (HEAD_DIM ** -0.5)).astype(cd).transpose(0, 2, 1, 3)
    kcp = jnp.pad(kc, ((0, 0), (0, nc - kc.shape[1]), (0, 0), (0, 0))).astype(cd).transpose(0, 2, 1, 3)
    vct = jnp.pad(vc, ((0, 0), (0, nc - vc.shape[1]), (0, 0), (0, 0))).astype(cd).transpose(0, 2, 3, 1)

    def keys_t(k):
        return k.astype(cd).reshape(B, nch, TK, KV_HEADS, HEAD_DIM).transpose(0, 3, 1, 4, 2)

    def vals(v):
        return v.astype(cd).reshape(B, nch, TK, KV_HEADS, HEAD_DIM).transpose(0, 3, 1, 2, 4)

    onehot = (np.arange(nb)[None, :, None] ==
              (np.arange(S) // SEL_BLOCK).reshape(nch, 1, TK)).astype(np.float32)
    onehot = jnp.asarray(onehot, dtype=cd)
    gt = gates.transpose(0, 2, 1, 3)
    b0, b1, w2, pat = _attn_bias_tiles(rel_bias, S)
    G = Q_PER_KV
    kv_chunks = pl.BlockSpec((1, 1, nch, HEAD_DIM, TK), lambda b, k, i: (b, k, 0, 0, 0))
    v_chunks = pl.BlockSpec((1, 1, nch, TK, HEAD_DIM), lambda b, k, i: (b, k, 0, 0, 0))
    return pl.pallas_call(
        functools.partial(_nsa_attn_kernel, nc=nc, nb=nb),
        grid=(B, KV_HEADS, S // TQ),
        in_specs=[
            pl.BlockSpec((1, G, TQ, HEAD_DIM), lambda b, k, i: (b, k, i, 0)),
            pl.BlockSpec((1, 1, nc, HEAD_DIM), lambda b, k, i: (b, k, 0, 0)),
            pl.BlockSpec((1, 1, HEAD_DIM, nc), lambda b, k, i: (b, k, 0, 0)),
            kv_chunks, v_chunks, kv_chunks, v_chunks,
            pl.BlockSpec((nch, nb, TK), lambda b, k, i: (0, 0, 0)),
            pl.BlockSpec((1, G, TQ, 3), lambda b, k, i: (b, k, i, 0)),
            pl.BlockSpec((G, TQ, TK), lambda b, k, i: (k, 0, 0)),
            pl.BlockSpec((G, TQ, TK), lambda b, k, i: (k, 0, 0)),
            pl.BlockSpec((TQ, TK), lambda b, k, i: (0, 0)),
            pl.BlockSpec((G, 2 * nc, TQ), lambda b, k, i: (k, 0, 0)),
        ],
        out_specs=pl.BlockSpec((1, TQ, G * HEAD_DIM), lambda b, k, i: (b, i, k)),
        out_shape=jax.ShapeDtypeStruct((B, S, N_HEADS * HEAD_DIM), jnp.float32),
        scratch_shapes=[
            pltpu.VMEM((G, TQ, LANES), jnp.float32),
            pltpu.VMEM((G, TQ, LANES), jnp.float32),
            pltpu.VMEM((G, TQ, HEAD_DIM), jnp.float32),
            pltpu.VMEM((TQ // LANES, nc, LANES), jnp.float32),
        ],
        compiler_params=pltpu.CompilerParams(
            dimension_semantics=("arbitrary", "arbitrary", "arbitrary"),
            vmem_limit_bytes=56 * 1024 * 1024),
        name="nsa_attn",
    )(qs, kcp, vct, keys_t(ks), vals(vs), keys_t(kw), vals(vw), onehot, gt, b0, b1, w2, pat)


def rmsnorm(x, g):
    xf = x.astype(jnp.float32)
    y = xf * lax.rsqrt(jnp.mean(xf * xf, axis=-1, keepdims=True) + EPS)
    return (y * g.astype(jnp.float32)).astype(x.dtype)


def layernorm(x, g, b):
    xf = x.astype(jnp.float32)
    mu = jnp.mean(xf, axis=-1, keepdims=True)
    var = jnp.mean(jnp.square(xf - mu), axis=-1, keepdims=True)
    return ((xf - mu) * lax.rsqrt(var + EPS) * g.astype(jnp.float32) + b.astype(jnp.float32)).astype(x.dtype)


def rel_bucket(n):
    n = jnp.maximum(n, 0)
    max_exact = N_BUCKETS // 2
    nf = jnp.maximum(n, 1).astype(jnp.float32)
    large = max_exact + (jnp.log(nf / max_exact) / math.log(MAX_DISTANCE / max_exact)
                         * (N_BUCKETS - max_exact)).astype(jnp.int32)
    large = jnp.minimum(large, N_BUCKETS - 1)
    return jnp.where(n < max_exact, n, large)


def head_bias(rel, rel_bias):
    q, k = rel.shape
    b = rel_bias[rel_bucket(rel)]
    return b.reshape(q, k, KV_HEADS, Q_PER_KV).transpose(2, 3, 0, 1)


def masked_softmax(logits, mask):
    z = jnp.where(mask, logits.astype(jnp.float32), -1e30)
    return jax.nn.softmax(z, axis=-1) * mask


def nsa_project(xn, w_in, q_gain, k_gain):
    B, T, _ = xn.shape
    hq = N_HEADS * HEAD_DIM
    hkv = 6 * KV_HEADS * HEAD_DIM
    h = xn @ w_in
    q = rmsnorm(h[..., :hq].reshape(B, T, N_HEADS, HEAD_DIM), q_gain)
    kv = h[..., hq:hq + hkv].reshape(B, T, 6, KV_HEADS, HEAD_DIM)
    gates = jax.nn.sigmoid(h[..., hq + hkv:].astype(jnp.float32)).reshape(B, T, N_HEADS, 3)
    k_sel = rmsnorm(kv[:, :, 2], k_gain[1])
    k_win = rmsnorm(kv[:, :, 4], k_gain[2])
    return q, kv[:, :, 0], kv[:, :, 1], k_sel, kv[:, :, 3], k_win, kv[:, :, 5], gates


def compress(rows, pe, w1, w2):
    B, T = rows.shape[:2]
    n_ch = T // CMP_STRIDE
    c = rows[:, :n_ch * CMP_STRIDE].reshape(B, n_ch, CMP_STRIDE, KV_HEADS, HEAD_DIM)
    blk = jnp.concatenate([c[:, :-1], c[:, 1:]], axis=2) + pe[None, None, :, None, :]
    h = jax.nn.gelu(jnp.einsum('bnlkd,ldf->bnkf', blk, w1.reshape(CMP_LEN, HEAD_DIM, CMP_HID)))
    return jnp.einsum('bnkf,fd->bnkd', h, w2)


def compressed_kv(k_raw, v_raw, k_gain0, pe, w1, w2):
    kc = rmsnorm(compress(k_raw, pe[0], w1[0], w2[0]), k_gain0)
    vc = compress(v_raw, pe[1], w1[1], w2[1])
    end = jnp.arange(kc.shape[1]) * CMP_STRIDE + (CMP_LEN - 1)
    return kc, vc, end


def nsa_core(q, q_pos, kc, vc, kc_end, ks, vs, kw, vw, kw_pos, gates, rel_bias):
    B, Q = q.shape[:2]
    nc = kc.shape[1]
    scale = HEAD_DIM ** -0.5
    qh = q.reshape(B, Q, KV_HEADS, Q_PER_KV, HEAD_DIM)
    lc = jnp.einsum('bqkgd,bnkd->bkgqn', qh, kc) * scale + head_bias(q_pos[:, None] - kc_end[None, :], rel_bias)
    pc = masked_softmax(lc, kc_end[None, :] <= q_pos[:, None])
    oc = jnp.einsum('bkgqn,bnkd->bqkgd', pc.astype(vc.dtype), vc)
    nb = ks.shape[1] // SEL_BLOCK
    imp = jnp.pad(pc.sum(axis=2), ((0, 0), (0, 0), (0, 0), (0, nb * CMP_PER_SEL - nc)))
    imp = imp.reshape(B, KV_HEADS, Q, nb, CMP_PER_SEL).sum(-1)
    blk = jnp.arange(nb)
    d = (q_pos // SEL_BLOCK)[:, None] - blk[None, :]
    forced = (blk[None, :] == 0) | ((d >= 0) & (d < N_LOCAL))
    score = jnp.where(forced, jnp.inf, jnp.where(d >= 0, imp, -jnp.inf))
    n_sel = min(N_SEL, nb)
    _, idx = lax.top_k(score, n_sel)
    bi = jnp.arange(B)[:, None, None, None]
    ki = jnp.arange(KV_HEADS)[None, :, None, None]
    ksb = ks.reshape(B, nb, SEL_BLOCK, KV_HEADS, HEAD_DIM).transpose(0, 3, 1, 2, 4)
    vsb = vs.reshape(B, nb, SEL_BLOCK, KV_HEADS, HEAD_DIM).transpose(0, 3, 1, 2, 4)
    kg = ksb[bi, ki, idx].reshape(B, KV_HEADS, Q, n_sel * SEL_BLOCK, HEAD_DIM)
    vg = vsb[bi, ki, idx].reshape(B, KV_HEADS, Q, n_sel * SEL_BLOCK, HEAD_DIM)
    kpos = (idx[..., None] * SEL_BLOCK + jnp.arange(SEL_BLOCK)).reshape(B, KV_HEADS, Q, n_sel * SEL_BLOCK)
    rel_s = q_pos[None, None, :, None] - kpos
    tb = rel_bias.reshape(N_BUCKETS, KV_HEADS, Q_PER_KV)
    bs = tb[rel_bucket(rel_s), ki].transpose(0, 1, 4, 2, 3)
    ls = jnp.einsum('bqkgd,bkqjd->bkgqj', qh, kg) * scale + bs
    ps = masked_softmax(ls, (rel_s >= 0)[:, :, None])
    os_ = jnp.einsum('bkgqj,bkqjd->bqkgd', ps.astype(vg.dtype), vg)
    rel_w = q_pos[:, None] - kw_pos[None, :]
    mw = (rel_w >= 0) & (rel_w <= WINDOW) & (kw_pos[None, :] >= 0)
    lw = jnp.einsum('bqkgd,bwkd->bkgqw', qh, kw) * scale + head_bias(rel_w, rel_bias)
    pw = masked_softmax(lw, mw)
    ow = jnp.einsum('bkgqw,bwkd->bqkgd', pw.astype(vw.dtype), vw)
    g = gates.reshape(B, Q, KV_HEADS, Q_PER_KV, 3).astype(oc.dtype)
    o = g[..., 0:1] * oc + g[..., 1:2] * os_ + g[..., 2:3] * ow
    return o.reshape(B, Q, N_HEADS * HEAD_DIM)


def nsa_prompt(xn, w_in, w_out, q_gain, k_gain, pe, w1, w2, rel_bias):
    B, S, _ = xn.shape
    q, kcr, vcr, ks, vs, kw, vw, gates = nsa_project(xn, w_in, q_gain, k_gain)
    kc, vc, kc_end = compressed_kv(kcr, vcr, k_gain[0], pe, w1, w2)
    o = _nsa_attention(q, kc, vc, ks, vs, kw, vw, gates, rel_bias)
    w_keep = min(WINDOW, S)
    new_rows = jnp.stack([kcr, vcr, ks, vs], axis=2)
    new_win = jnp.stack([kw[:, S - w_keep:], vw[:, S - w_keep:]], axis=2)
    y = _mm(o.reshape(B * S, -1), w_out).reshape(B, S, -1)
    return y, new_rows, new_win


def nsa_sample(xn, past_rows, win_buf, w_in, w_out, q_gain, k_gain, pe, w1, w2, rel_bias):
    B, L, _ = xn.shape
    P = past_rows.shape[1]
    q, kcr, vcr, ks, vs, kw, vw, gates = nsa_project(xn, w_in, q_gain, k_gain)
    new_rows = jnp.stack([kcr, vcr, ks, vs], axis=2)
    rows = jnp.concatenate([past_rows, new_rows], axis=1)
    kc, vc, kc_end = compressed_kv(rows[:, :, 0], rows[:, :, 1], k_gain[0], pe, w1, w2)
    T = P + L
    tp = -(-T // SEL_BLOCK) * SEL_BLOCK
    pad = ((0, 0), (0, tp - T), (0, 0), (0, 0))
    ks_all = jnp.pad(rows[:, :, 2], pad)
    vs_all = jnp.pad(rows[:, :, 3], pad)
    win_all = jnp.concatenate([win_buf, jnp.stack([kw, vw], axis=2)], axis=1)
    wb = win_buf.shape[1]
    kw_pos = P - wb + jnp.arange(wb + L)
    q_pos = P + jnp.arange(L)
    o = nsa_core(q, q_pos, kc, vc, kc_end, ks_all, vs_all, win_all[:, :, 0], win_all[:, :, 1],
                 kw_pos, gates, rel_bias)
    w_keep = min(WINDOW, wb + L)
    return o @ w_out, new_rows, win_all[:, wb + L - w_keep:]


def gmlp_mixer(xn, w_in, b_in, ln_g, ln_b, w_s, b_s, w_out):
    B, L, _ = xn.shape
    z = jax.nn.gelu(xn @ w_in + b_in)
    u = z[..., :GM_WIDTH]
    v = layernorm(z[..., GM_WIDTH:], ln_g, ln_b)
    lp = -(-L // CHUNK) * CHUNK
    vp = jnp.pad(v, ((0, 0), (0, lp - L), (0, 0))).reshape(B, lp // CHUNK, CHUNK, GM_GROUPS, GM_GROUP_DIM)
    causal = jnp.tril(jnp.ones((CHUNK, CHUNK), dtype=bool))
    ws = jnp.where(causal[None], w_s, 0)
    s = jnp.einsum('gts,bcsgd->bctgd', ws, vp) + b_s.T[None, None, :, :, None]
    s = s.reshape(B, lp, GM_WIDTH)[:, :L]
    start = ((L - 1) // CHUNK) * CHUNK
    return (u * s) @ w_out, v[:, start:]


def hier_moe(x, w_grp, b_grp, w_exp, b_exp, w_gu, w_dn):
    T, D = x.shape
    pg = jax.nn.softmax((x @ w_grp + b_grp).astype(jnp.float32), axis=-1)
    grp = jnp.argmax(pg, axis=-1)
    g_w = jnp.take_along_axis(pg, grp[:, None], axis=-1)
    le = (x @ w_exp + b_exp).astype(jnp.float32).reshape(T, N_GROUPS, EXPERTS_PER_GROUP)
    le = jnp.take_along_axis(le, grp[:, None, None], axis=1)[:, 0]
    top_v, top_i = lax.top_k(le, TOP_K)
    wts = (jax.nn.softmax(top_v, axis=-1) * g_w).reshape(-1)
    eid = (grp[:, None] * EXPERTS_PER_GROUP + top_i).reshape(-1)
    n = T * TOP_K
    order = jnp.argsort(eid)
    e_s = eid[order]
    tok_s = order // TOP_K
    w_s = wts[order]
    counts = jnp.bincount(eid, length=N_EXPERTS)
    padded = (counts + MOE_BLOCK - 1) // MOE_BLOCK * MOE_BLOCK
    pad_end = jnp.cumsum(padded)
    pad_start = pad_end - padded
    start = jnp.cumsum(counts) - counts
    dest = pad_start[e_s] + jnp.arange(n) - start[e_s]
    n_blocks = -(-n // MOE_BLOCK) + N_EXPERTS
    xbuf = jnp.zeros((n_blocks * MOE_BLOCK, D), x.dtype).at[dest].set(x[tok_s])
    blk_e = jnp.minimum(jnp.searchsorted(pad_end, jnp.arange(n_blocks) * MOE_BLOCK, side='right'), N_EXPERTS - 1)

    def expert(args):
        xb, e = args
        gu = xb @ w_gu[e]
        return (jax.nn.silu(gu[:, :D_EXPERT]) * gu[:, D_EXPERT:]) @ w_dn[e]

    ybuf = lax.map(expert, (xbuf.reshape(n_blocks, MOE_BLOCK, D), blk_e)).reshape(-1, D)
    return jax.ops.segment_sum(ybuf[dest] * w_s[:, None].astype(x.dtype), tok_s, num_segments=T)


def kernel(x_prompt, x_sample, cache_nsa_kv, state_win_kv, page_table, rel_bias, ln_mix, ln_ffn,
           nsa_w_in, nsa_w_out, nsa_q_gain, nsa_k_gain, cmp_pe, cmp_w1, cmp_w2,
           gm_w_in, gm_b_in, gm_ln_g, gm_ln_b, gm_w_s, gm_b_s, gm_w_out,
           moe_w_grp, moe_b_grp, moe_w_exp, moe_b_exp, moe_w_gu, moe_w_dn):
    xp = x_prompt
    xs = x_sample
    dec_b = x_sample.shape[0]
    past_len = page_table.shape[1] * PAGE_SIZE
    kv_p, kv_s, win_p, win_s, gv_p, gv_s = [], [], [], [], [], []
    for i in range(DEPTH):
        a = i // N_MIXERS
        hp = rmsnorm(xp, ln_mix[i])
        hs = rmsnorm(xs, ln_mix[i])
        if i % N_MIXERS == 0:
            yp, rp, wp = nsa_prompt(hp, nsa_w_in[a], nsa_w_out[a], nsa_q_gain[a], nsa_k_gain[a],
                                    cmp_pe[a], cmp_w1[a], cmp_w2[a], rel_bias)
            past = jnp.take(cache_nsa_kv, page_table, axis=0)[:, :, :, a]
            past = past.reshape(dec_b, past_len, 4, KV_HEADS, HEAD_DIM)
            ys, rs, ws = nsa_sample(hs, past, state_win_kv[a], nsa_w_in[a], nsa_w_out[a], nsa_q_gain[a],
                                    nsa_k_gain[a], cmp_pe[a], cmp_w1[a], cmp_w2[a], rel_bias)
            kv_p.append(rp)
            kv_s.append(rs)
            win_p.append(wp)
            win_s.append(ws)
        else:
            yp, vp = gmlp_mixer(hp, gm_w_in[a], gm_b_in[a], gm_ln_g[a], gm_ln_b[a], gm_w_s[a], gm_b_s[a], gm_w_out[a])
            ys, vs = gmlp_mixer(hs, gm_w_in[a], gm_b_in[a], gm_ln_g[a], gm_ln_b[a], gm_w_s[a], gm_b_s[a], gm_w_out[a])
            gv_p.append(vp)
            gv_s.append(vs)
        xp = xp + yp
        xs = xs + ys
        hp = rmsnorm(xp, ln_ffn[i])
        hs = rmsnorm(xs, ln_ffn[i])
        xp = xp + hier_moe(hp.reshape(-1, D_MODEL), moe_w_grp[i], moe_b_grp[i], moe_w_exp[i], moe_b_exp[i],
                           moe_w_gu[i], moe_w_dn[i]).reshape(xp.shape)
        xs = xs + hier_moe(hs.reshape(-1, D_MODEL), moe_w_grp[i], moe_b_grp[i], moe_w_exp[i], moe_b_exp[i],
                           moe_w_gu[i], moe_w_dn[i]).reshape(xs.shape)
    new_kv_prompt = jnp.stack(kv_p, axis=2)
    new_kv_sample = jnp.stack(kv_s, axis=2)
    new_win_prompt = jnp.stack(win_p, axis=0)
    new_win_sample = jnp.stack(win_s, axis=0)
    new_gm_v_prompt = jnp.stack(gv_p, axis=0)
    new_gm_v_sample = jnp.stack(gv_s, axis=0)
    return (xp, xs, new_kv_prompt, new_kv_sample, new_win_prompt, new_win_sample, new_gm_v_prompt, new_gm_v_sample)
```

```python
import functools
import math

import jax
import jax.numpy as jnp
import numpy as np
from jax import lax
from jax.experimental import pallas as pl
from jax.experimental.pallas import tpu as pltpu

D_MODEL = 1024
PAGE_SIZE = 128
DEPTH = 2
N_MIXERS = 2
N_HEADS = 16
HEAD_DIM = 64
KV_HEADS = 4
Q_PER_KV = N_HEADS // KV_HEADS
CMP_LEN = 32
CMP_STRIDE = 16
CMP_HID = 64
SEL_BLOCK = 64
CMP_PER_SEL = SEL_BLOCK // CMP_STRIDE
N_SEL = 16
N_LOCAL = 2
WINDOW = 512
Q_BLOCK = 128
N_BUCKETS = 32
MAX_DISTANCE = 128
CHUNK = 128
GM_WIDTH = 2048
GM_GROUPS = 8
GM_GROUP_DIM = GM_WIDTH // GM_GROUPS
N_GROUPS = 4
EXPERTS_PER_GROUP = 8
N_EXPERTS = N_GROUPS * EXPERTS_PER_GROUP
TOP_K = 2
D_EXPERT = 512
MOE_BLOCK = 128
EPS = 1e-6


def _mm_kernel(x_ref, w_ref, o_ref):
    o_ref[...] = jnp.dot(x_ref[...].astype(jnp.bfloat16), w_ref[...].astype(jnp.bfloat16),
                         preferred_element_type=jnp.float32)


def _mm(x, w, tm=512, tn=512):
    m, k = x.shape
    n = w.shape[1]
    tm = min(tm, m)
    tn = min(tn, n)
    return pl.pallas_call(
        _mm_kernel,
        grid=(m // tm, n // tn),
        in_specs=[pl.BlockSpec((tm, k), lambda i, j: (i, 0)),
                  pl.BlockSpec((k, tn), lambda i, j: (0, j))],
        out_specs=pl.BlockSpec((tm, tn), lambda i, j: (i, j)),
        out_shape=jax.ShapeDtypeStruct((m, n), jnp.float32),
        name="mm",
    )(x, w)


NEG = -1e30
MXU_DTYPE = jnp.bfloat16
TQ = 256
TK = 256
LANES = 128
BIAS_SPAN = 128


def _bucket_of_distance():
    d = np.arange(BIAS_SPAN)
    max_exact = N_BUCKETS // 2
    nf = np.maximum(d, 1).astype(np.float32)
    large = max_exact + (np.log(nf / np.float32(max_exact)) / np.float32(math.log(MAX_DISTANCE / max_exact))
                         * np.float32(N_BUCKETS - max_exact)).astype(np.int32)
    large = np.minimum(large, N_BUCKETS - 1)
    return np.where(d < max_exact, d, large).astype(np.int32)


def _attn_bias_tiles(rel_bias, seq):
    fd = rel_bias[_bucket_of_distance()].T
    far = fd[:, BIAS_SPAN - 1]
    fd = fd - far[:, None]
    i = np.arange(TQ)[:, None]
    j = np.arange(TK)[None, :]
    d0 = i - j
    b0 = jnp.where(d0 >= 0, jnp.take(fd, np.clip(d0, 0, BIAS_SPAN - 1), axis=1), NEG)
    b1 = jnp.take(fd, np.clip(TQ + i - j, 0, BIAS_SPAN - 1), axis=1)
    w2 = np.where(2 * TQ + i - j <= WINDOW, 0.0, NEG).astype(np.float32)
    nc = seq // CMP_STRIDE
    r = np.arange(2 * nc)[:, None]
    dc = np.arange(TQ)[None, :] - CMP_STRIDE * (r - nc) - (CMP_LEN - 1)
    pat = jnp.where(dc >= 0, jnp.take(fd, np.clip(dc, 0, BIAS_SPAN - 1), axis=1), NEG)
    return b0, b1, jnp.asarray(w2), pat


def _nsa_attn_kernel(q_ref, kc_ref, vct_ref, kst_ref, vs_ref, kwt_ref, vw_ref, oh_ref, g_ref,
                     b0_ref, b1_ref, w2_ref, pat_ref, o_ref, m_sc, l_sc, acc_sc, imp_sc, *, nc, nb):
    f32 = jnp.float32
    qb = pl.program_id(2)

    r0 = pl.multiple_of(nc - (TQ // CMP_STRIDE) * qb, TQ // CMP_STRIDE)
    t_row = qb * TQ + lax.broadcasted_iota(jnp.int32, (1, TQ), 1)
    has_cmp = t_row >= CMP_LEN - 1
    for h in range(Q_PER_KV):
        st = lax.dot_general(kc_ref[0, 0], q_ref[0, h], (((1,), (1,)), ((), ())),
                             preferred_element_type=f32)
        st = st + pat_ref[h, pl.ds(r0, nc), :]
        e = jnp.exp(st - jnp.max(st, axis=0, keepdims=True))
        inv = jnp.where(has_cmp, 1.0 / jnp.sum(e, axis=0, keepdims=True), 0.0)
        pt = e * inv
        for c in range(TQ // LANES):
            part = pt[:, c * LANES:(c + 1) * LANES]
            if h == 0:
                imp_sc[c] = part
            else:
                imp_sc[c] += part
        oct_h = jnp.dot(vct_ref[0, 0], pt.astype(MXU_DTYPE), preferred_element_type=f32)
        o_ref[0, :, h * HEAD_DIM:(h + 1) * HEAD_DIM] = g_ref[0, h][:, 0:1] * oct_h.T

    imp = jnp.concatenate(
        [sum(imp_sc[c, pl.ds(r, nb, stride=CMP_PER_SEL), :] for r in range(CMP_PER_SEL))
         for c in range(TQ // LANES)], axis=1)
    blk = lax.broadcasted_iota(jnp.int32, (nb, TQ), 0)
    t_blk = (qb * TQ + lax.broadcasted_iota(jnp.int32, (nb, TQ), 1)) // SEL_BLOCK
    dd = t_blk - blk
    forced = (blk == 0) | ((dd >= 0) & (dd < N_LOCAL))
    score = jnp.where(forced, jnp.inf, jnp.where(dd >= 0, imp, -jnp.inf))
    pen_t = jnp.full((nb, TQ), NEG, f32)
    for _ in range(min(N_SEL, nb)):
        best = jnp.max(score, axis=0, keepdims=True)
        first = jnp.min(jnp.where(score == best, blk, nb), axis=0, keepdims=True)
        hit = blk == first
        pen_t = jnp.where(hit, 0.0, pen_t)
        score = jnp.where(hit, -jnp.inf, score)
    pen = pen_t.T.astype(MXU_DTYPE)

    def reset():
        m_sc[...] = jnp.full(m_sc.shape, NEG, f32)
        l_sc[...] = jnp.zeros(l_sc.shape, f32)
        acc_sc[...] = jnp.zeros(acc_sc.shape, f32)

    def attend(h, kt, v, extra):
        s = jnp.dot(q_ref[0, h], kt, preferred_element_type=f32) + extra
        m_prev = m_sc[h]
        m_new = jnp.maximum(m_prev, jnp.max(s, axis=1, keepdims=True))
        p = jnp.exp(s - jnp.concatenate([m_new] * (TK // LANES), axis=1))
        alpha = jnp.exp(m_prev - m_new)
        l_sc[h] = alpha * l_sc[h] + jnp.sum(p, axis=1, keepdims=True)
        acc_sc[h] = alpha[:, :HEAD_DIM] * acc_sc[h] + jnp.dot(p.astype(MXU_DTYPE), v,
                                                             preferred_element_type=f32)
        m_sc[h] = m_new

    def emit(gate_col):
        for h in range(Q_PER_KV):
            w = g_ref[0, h][:, gate_col:gate_col + 1] / l_sc[h][:, :HEAD_DIM]
            o_ref[0, :, h * HEAD_DIM:(h + 1) * HEAD_DIM] += w * acc_sc[h]

    def selected_chunk(c, bias_ref):
        mterm = jnp.dot(pen, oh_ref[c], preferred_element_type=f32)
        for h in range(Q_PER_KV):
            extra = mterm if bias_ref is None else mterm + bias_ref[h]
            attend(h, kst_ref[0, 0, c], vs_ref[0, 0, c], extra)

    reset()
    selected_chunk(qb, b0_ref)

    @pl.when(qb >= 1)
    def _():
        selected_chunk(qb - 1, b1_ref)

    def far_body(c, carry):
        selected_chunk(c, None)
        return carry

    lax.fori_loop(0, jnp.maximum(qb - 1, 0), far_body, 0)
    emit(1)

    reset()
    for h in range(Q_PER_KV):
        attend(h, kwt_ref[0, 0, qb], vw_ref[0, 0, qb], b0_ref[h])

    @pl.when(qb >= 1)
    def _():
        for h in range(Q_PER_KV):
            attend(h, kwt_ref[0, 0, qb - 1], vw_ref[0, 0, qb - 1], b1_ref[h])

    @pl.when(qb >= 2)
    def _():
        for h in range(Q_PER_KV):
            attend(h, kwt_ref[0, 0, qb - 2], vw_ref[0, 0, qb - 2], w2_ref[...])

    emit(2)


def _nsa_attention(qs, kcp, vct, kst, vs, kwt, vw, gt, rel_bias):
    B, _, S, _ = qs.shape
    assert S % TQ == 0 and TQ == TK and WINDOW == 2 * TQ and S // SEL_BLOCK >= N_SEL
    nc = S // CMP_STRIDE
    nb = S // SEL_BLOCK
    nch = S // TK
    onehot = (np.arange(nb)[None, :, None] ==
              (np.arange(S) // SEL_BLOCK).reshape(nch, 1, TK)).astype(np.float32)
    onehot = jnp.asarray(onehot, dtype=MXU_DTYPE)
    b0, b1, w2, pat = _attn_bias_tiles(rel_bias, S)
    G = Q_PER_KV
    kv_chunks = pl.BlockSpec((1, 1, nch, HEAD_DIM, TK), lambda b, k, i: (b, k, 0, 0, 0))
    v_chunks = pl.BlockSpec((1, 1, nch, TK, HEAD_DIM), lambda b, k, i: (b, k, 0, 0, 0))
    return pl.pallas_call(
        functools.partial(_nsa_attn_kernel, nc=nc, nb=nb),
        grid=(B, KV_HEADS, S // TQ),
        in_specs=[
            pl.BlockSpec((1, G, TQ, HEAD_DIM), lambda b, k, i: (b, k, i, 0)),
            pl.BlockSpec((1, 1, nc, HEAD_DIM), lambda b, k, i: (b, k, 0, 0)),
            pl.BlockSpec((1, 1, HEAD_DIM, nc), lambda b, k, i: (b, k, 0, 0)),
            kv_chunks, v_chunks, kv_chunks, v_chunks,
            pl.BlockSpec((nch, nb, TK), lambda b, k, i: (0, 0, 0)),
            pl.BlockSpec((1, G, TQ, 3), lambda b, k, i: (b, k, i, 0)),
            pl.BlockSpec((G, TQ, TK), lambda b, k, i: (k, 0, 0)),
            pl.BlockSpec((G, TQ, TK), lambda b, k, i: (k, 0, 0)),
            pl.BlockSpec((TQ, TK), lambda b, k, i: (0, 0)),
            pl.BlockSpec((G, 2 * nc, TQ), lambda b, k, i: (k, 0, 0)),
        ],
        out_specs=pl.BlockSpec((1, TQ, G * HEAD_DIM), lambda b, k, i: (b, i, k)),
        out_shape=jax.ShapeDtypeStruct((B, S, N_HEADS * HEAD_DIM), jnp.float32),
        scratch_shapes=[
            pltpu.VMEM((G, TQ, LANES), jnp.float32),
            pltpu.VMEM((G, TQ, LANES), jnp.float32),
            pltpu.VMEM((G, TQ, HEAD_DIM), jnp.float32),
            pltpu.VMEM((TQ // LANES, nc, LANES), jnp.float32),
        ],
        compiler_params=pltpu.CompilerParams(
            dimension_semantics=("arbitrary", "arbitrary", "arbitrary"),
            vmem_limit_bytes=56 * 1024 * 1024),
        name="nsa_attn",
    )(qs, kcp, vct, kst, vs, kwt, vw, onehot, gt, b0, b1, w2, pat)


def _nsa_prompt_layer(x, ln_g, w_in, w_out, q_gain, k_gain, pe, w1, w2, rel_bias):
    B, S, D = x.shape
    x2 = x.reshape(B * S, D)
    qs, rows, win, kst, vs, kwt, vw, c, gates = _nsa_project(x2, ln_g, w_in, q_gain, k_gain, B, S)
    nc = S // CMP_STRIDE
    cmp = _compress(c.reshape(2, B, nc, CMP_STRIDE * SLOT), pe, w1, w2, k_gain[0])
    cmp = cmp.reshape(2, B, nc, KV_HEADS, HEAD_DIM).astype(MXU_DTYPE)
    gt = gates[:, :3 * N_HEADS].reshape(B, S, N_HEADS, 3).transpose(0, 2, 1, 3)
    o = _nsa_attention(qs, cmp[0].transpose(0, 2, 1, 3), cmp[1].transpose(0, 2, 3, 1), kst, vs, kwt, vw, gt,
                       rel_bias)
    y = _proj_residual(x2, o.reshape(B * S, HQ), w_out)
    return (y.reshape(B, S, D), rows.reshape(B, S, 4, KV_HEADS, HEAD_DIM),
            win.reshape(B, WINDOW, 2, KV_HEADS, HEAD_DIM))


HQ = N_HEADS * HEAD_DIM
SLOT = KV_HEADS * HEAD_DIM
NSA_COLS = HQ + 6 * SLOT + 3 * N_HEADS
NSA_COLS_PAD = -(-NSA_COLS // LANES) * LANES
GATE_COLS = NSA_COLS_PAD - HQ - 6 * SLOT


def _nsa_proj_kernel(x_ref, g_ref, w_ref, qg_ref, kg_ref, *outs, prompt):
    f32 = jnp.float32
    x = x_ref[...]
    xn = x * lax.rsqrt(jnp.mean(x * x, axis=-1, keepdims=True) + EPS) * g_ref[...]
    h = jnp.dot(xn.astype(MXU_DTYPE), w_ref[...], preferred_element_type=f32)

    def head_norm(v, gain):
        return v * lax.rsqrt(jnp.mean(v * v, axis=-1, keepdims=True) + EPS) * gain

    def slot(s, kv):
        lo = HQ + s * SLOT + kv * HEAD_DIM
        return h[:, lo:lo + HEAD_DIM]

    gates = 1.0 / (1.0 + jnp.exp(-h[:, HQ + 6 * SLOT:]))
    if prompt:
        q_ref, rows_ref, win_ref, kst_ref, vs_ref, kwt_ref, vw_ref, c_ref, gate_ref = outs
        for hh in range(N_HEADS):
            qn = head_norm(h[:, hh * HEAD_DIM:(hh + 1) * HEAD_DIM], qg_ref[...])
            q_ref[0, hh] = (qn * HEAD_DIM ** -0.5).astype(MXU_DTYPE)
        c_ref[0, 0] = h[:, HQ:HQ + SLOT].astype(MXU_DTYPE)
        c_ref[1, 0] = h[:, HQ + SLOT:HQ + 2 * SLOT].astype(MXU_DTYPE)
    else:
        q_ref, rows_ref, win_ref, gate_ref = outs
        for hh in range(N_HEADS):
            q_ref[:, hh * HEAD_DIM:(hh + 1) * HEAD_DIM] = head_norm(h[:, hh * HEAD_DIM:(hh + 1) * HEAD_DIM],
                                                                    qg_ref[...])
    rows_ref[:, 0:2 * SLOT] = h[:, HQ:HQ + 2 * SLOT]
    rows_ref[:, 3 * SLOT:4 * SLOT] = h[:, HQ + 3 * SLOT:HQ + 4 * SLOT]
    for kv in range(KV_HEADS):
        lanes = slice(kv * HEAD_DIM, (kv + 1) * HEAD_DIM)
        ksn = head_norm(slot(2, kv), kg_ref[1:2, :])
        kwn = head_norm(slot(4, kv), kg_ref[2:3, :])
        rows_ref[:, 2 * SLOT + kv * HEAD_DIM:2 * SLOT + (kv + 1) * HEAD_DIM] = ksn
        if prompt:
            win_ref[0, 0, :, lanes] = kwn
            kst_ref[0, kv, 0] = ksn.T.astype(MXU_DTYPE)
            kwt_ref[0, kv, 0] = kwn.T.astype(MXU_DTYPE)
            vs_ref[0, kv, 0] = slot(3, kv).astype(MXU_DTYPE)
            vw_ref[0, kv, 0] = slot(5, kv).astype(MXU_DTYPE)
        else:
            win_ref[:, lanes] = kwn
    if prompt:
        win_ref[0, 0, :, SLOT:2 * SLOT] = h[:, HQ + 5 * SLOT:HQ + 6 * SLOT]
    else:
        win_ref[:, SLOT:2 * SLOT] = h[:, HQ + 5 * SLOT:HQ + 6 * SLOT]
    gate_ref[...] = gates


def _nsa_project(x, ln_g, w_in, q_gain, k_gain, batch, seq):
    T, D = x.shape
    f32 = jnp.float32
    cd = MXU_DTYPE
    prompt = seq > 1
    tb = TK if prompt else T
    w = jnp.pad(w_in, ((0, 0), (0, NSA_COLS_PAD - NSA_COLS))).astype(cd)
    fixed = lambda i: (0, 0)
    row = lambda i: (i, 0)
    in_specs = [pl.BlockSpec((tb, D), row), pl.BlockSpec((1, D), fixed), pl.BlockSpec((D, NSA_COLS_PAD), fixed),
                pl.BlockSpec((1, HEAD_DIM), fixed), pl.BlockSpec((3, HEAD_DIM), fixed)]
    if prompt:
        assert seq % tb == 0 and WINDOW == 2 * tb
        n = seq // tb
        kt_spec = pl.BlockSpec((1, KV_HEADS, 1, HEAD_DIM, tb), lambda i: (i // n, 0, i % n, 0, 0))
        v_spec = pl.BlockSpec((1, KV_HEADS, 1, tb, HEAD_DIM), lambda i: (i // n, 0, i % n, 0, 0))
        kt_shape = jax.ShapeDtypeStruct((batch, KV_HEADS, n, HEAD_DIM, tb), cd)
        v_shape = jax.ShapeDtypeStruct((batch, KV_HEADS, n, tb, HEAD_DIM), cd)
        out_specs = [
            pl.BlockSpec((1, N_HEADS, tb, HEAD_DIM), lambda i: (i // n, 0, i % n, 0)),
            pl.BlockSpec((tb, 4 * SLOT), row),
            pl.BlockSpec((1, 1, tb, 2 * SLOT), lambda i: (i // n, jnp.maximum(i % n - (n - 2), 0), 0, 0)),
            kt_spec, v_spec, kt_spec, v_spec,
            pl.BlockSpec((2, 1, tb, SLOT), lambda i: (0, i // n, i % n, 0)),
            pl.BlockSpec((tb, GATE_COLS), row)]
        out_shape = [
            jax.ShapeDtypeStruct((batch, N_HEADS, seq, HEAD_DIM), cd),
            jax.ShapeDtypeStruct((T, 4 * SLOT), f32),
            jax.ShapeDtypeStruct((batch, 2, tb, 2 * SLOT), f32),
            kt_shape, v_shape, kt_shape, v_shape,
            jax.ShapeDtypeStruct((2, batch, seq, SLOT), cd),
            jax.ShapeDtypeStruct((T, GATE_COLS), f32)]
    else:
        out_specs = [pl.BlockSpec((tb, HQ), row), pl.BlockSpec((tb, 4 * SLOT), row),
                     pl.BlockSpec((tb, 2 * SLOT), row), pl.BlockSpec((tb, GATE_COLS), row)]
        out_shape = [jax.ShapeDtypeStruct((T, HQ), f32), jax.ShapeDtypeStruct((T, 4 * SLOT), f32),
                     jax.ShapeDtypeStruct((T, 2 * SLOT), f32), jax.ShapeDtypeStruct((T, GATE_COLS), f32)]
    return pl.pallas_call(
        functools.partial(_nsa_proj_kernel, prompt=prompt),
        grid=(T // tb,),
        in_specs=in_specs, out_specs=out_specs, out_shape=out_shape,
        compiler_params=pltpu.CompilerParams(dimension_semantics=("arbitrary",),
                                             vmem_limit_bytes=48 * 1024 * 1024),
        name="nsa_proj",
    )(x, ln_g.reshape(1, D), w, q_gain.reshape(1, HEAD_DIM), k_gain)


def _compress_kernel(c_ref, w1_ref, pe_ref, w2_ref, kg_ref, o_ref):
    f32 = jnp.float32
    u = jnp.dot(c_ref[0, 0].astype(MXU_DTYPE), w1_ref[0], preferred_element_type=f32)
    pe = jnp.dot(pe_ref[0], w1_ref[0], preferred_element_type=f32)
    nc = u.shape[0]
    hid = u[:, :SLOT] + pltpu.roll(u[:, SLOT:], nc - 1, 0) + (pe[0:1, :SLOT] + pe[1:2, SLOT:])
    out = jnp.dot(_gelu_tanh(hid).astype(MXU_DTYPE), w2_ref[0], preferred_element_type=f32)
    is_key = pl.program_id(1) == 0
    for kv in range(KV_HEADS):
        lanes = slice(kv * HEAD_DIM, (kv + 1) * HEAD_DIM)
        v = out[:, lanes]
        vn = v * lax.rsqrt(jnp.mean(v * v, axis=-1, keepdims=True) + EPS) * kg_ref[...]
        o_ref[0, 0, :, lanes] = jnp.where(is_key, vn, v)


def _compress(c, pe, w1, w2, k_gain0):
    n, nc = c.shape[1:3]
    cd = MXU_DTYPE
    kc = CMP_STRIDE * SLOT
    eye = jnp.eye(KV_HEADS, dtype=jnp.float32)
    w1h = w1.reshape(2, 2, CMP_STRIDE, HEAD_DIM, CMP_HID)
    w1b = jnp.einsum('shldf,kj->slkdhjf', w1h, eye).reshape(2, kc, 2 * SLOT).astype(cd)
    w2b = jnp.einsum('sdf,kj->skdjf', w2, eye).reshape(2, SLOT, SLOT).astype(cd)
    peh = jnp.broadcast_to(pe.reshape(2, 2, CMP_STRIDE, 1, HEAD_DIM), (2, 2, CMP_STRIDE, KV_HEADS, HEAD_DIM))
    peh = jnp.pad(peh.reshape(2, 2, kc), ((0, 0), (0, 14), (0, 0))).astype(cd)
    return pl.pallas_call(
        _compress_kernel,
        grid=(n, 2),
        in_specs=[pl.BlockSpec((1, 1, nc, kc), lambda b, s: (s, b, 0, 0)),
                  pl.BlockSpec((1, kc, 2 * SLOT), lambda b, s: (s, 0, 0)),
                  pl.BlockSpec((1, 16, kc), lambda b, s: (s, 0, 0)),
                  pl.BlockSpec((1, SLOT, SLOT), lambda b, s: (s, 0, 0)),
                  pl.BlockSpec((1, HEAD_DIM), lambda b, s: (0, 0))],
        out_specs=pl.BlockSpec((1, 1, nc, SLOT), lambda b, s: (s, b, 0, 0)),
        out_shape=jax.ShapeDtypeStruct((2, n, nc, SLOT), jnp.float32),
        compiler_params=pltpu.CompilerParams(dimension_semantics=("arbitrary", "arbitrary"),
                                             vmem_limit_bytes=48 * 1024 * 1024),
        name="nsa_compress",
    )(c, w1b, peh, w2b, k_gain0.reshape(1, HEAD_DIM))


def _proj_residual_kernel(x_ref, a_ref, w_ref, o_ref):
    o_ref[...] = x_ref[...] + jnp.dot(a_ref[...].astype(MXU_DTYPE), w_ref[...],
                                      preferred_element_type=jnp.float32)


def _proj_residual(x, a, w):
    T, D = x.shape
    k = a.shape[1]
    tb = min(512, T)
    assert T % tb == 0
    return pl.pallas_call(
        _proj_residual_kernel,
        grid=(T // tb,),
        in_specs=[pl.BlockSpec((tb, D), lambda i: (i, 0)), pl.BlockSpec((tb, k), lambda i: (i, 0)),
                  pl.BlockSpec((k, D), lambda i: (0, 0))],
        out_specs=pl.BlockSpec((tb, D), lambda i: (i, 0)),
        out_shape=jax.ShapeDtypeStruct((T, D), jnp.float32),
        compiler_params=pltpu.CompilerParams(dimension_semantics=("arbitrary",)),
        name="proj_residual",
    )(x, a, w.astype(MXU_DTYPE))


MOE_ROWS = 256
ROUTE_COLS = LANES


def _moe_route_kernel(x_ref, g_ref, w_ref, b_ref, tri_ref, h_ref, meta_ref, wt_ref, cnt_ref, carry_sc):
    f32 = jnp.float32
    step = pl.program_id(0)

    @pl.when(step == 0)
    def _():
        carry_sc[...] = jnp.zeros(carry_sc.shape, f32)

    x = x_ref[...]
    h = x * lax.rsqrt(jnp.mean(x * x, axis=-1, keepdims=True) + EPS) * g_ref[...]
    h_ref[...] = h
    logits = jnp.dot(h.astype(MXU_DTYPE), w_ref[...], preferred_element_type=f32) + b_ref[...]
    tb = logits.shape[0]
    col = lax.broadcasted_iota(jnp.int32, (tb, ROUTE_COLS), 1)

    def first_max(vals):
        best = jnp.max(vals, axis=1, keepdims=True)
        return best, jnp.min(jnp.where(vals == best, col, ROUTE_COLS), axis=1, keepdims=True)

    lg = jnp.where(col < N_GROUPS, logits, -jnp.inf)
    g_best, grp = first_max(lg)
    g_w = 1.0 / jnp.sum(jnp.exp(lg - g_best), axis=1, keepdims=True)
    lo = N_GROUPS + EXPERTS_PER_GROUP * grp
    le = jnp.where((col >= lo) & (col < lo + EXPERTS_PER_GROUP), logits, -jnp.inf)
    v0, c0 = first_max(le)
    v1, c1 = first_max(jnp.where(col == c0, -jnp.inf, le))
    e1 = jnp.exp(v1 - v0)
    w0 = g_w / (1.0 + e1)
    w1 = g_w * e1 / (1.0 + e1)
    chosen = (col == c0) | (col == c1)
    before = jnp.dot(tri_ref[...], jnp.where(chosen, 1.0, 0.0).astype(MXU_DTYPE),
                     preferred_element_type=f32) + carry_sc[...]
    r0 = jnp.sum(jnp.where(col == c0, before, 0.0), axis=1, keepdims=True).astype(jnp.int32)
    r1 = jnp.sum(jnp.where(col == c1, before, 0.0), axis=1, keepdims=True).astype(jnp.int32)
    carry_sc[...] += jnp.sum(jnp.where(chosen, 1.0, 0.0), axis=0, keepdims=True)
    cnt_ref[...] = carry_sc[...]
    meta_ref[...] = jnp.where(col == 0, c0 - N_GROUPS, jnp.where(col == 1, c1 - N_GROUPS,
                              jnp.where(col == 2, r0, jnp.where(col == 3, r1, 0))))
    wt_ref[...] = jnp.where(col == 0, w0, jnp.where(col == 1, w1, 0.0))


def _moe_dispatch_kernel(dest_ref, h_ref, xbuf_in, xbuf_ref, sem):
    del xbuf_in
    tb = h_ref.shape[0]

    def row_copy(r, k):
        return pltpu.make_async_copy(h_ref.at[pl.ds(r, 1)], xbuf_ref.at[pl.ds(dest_ref[0, 0, 2 * r + k], 1)], sem)

    def start(r, c):
        row_copy(r, 0).start()
        row_copy(r, 1).start()
        return c

    def wait(r, c):
        row_copy(r, 0).wait()
        row_copy(r, 1).wait()
        return c

    lax.fori_loop(0, tb, start, 0)
    lax.fori_loop(0, tb, wait, 0)


def _moe_expert_kernel(blk_e_ref, nblk_ref, x_ref, wgu_ref, wdn_ref, y_ref, wgu_sc, wdn_sc):
    i = pl.program_id(0)
    f32 = jnp.float32

    @pl.when(i < nblk_ref[0])
    def _():
        changed = jnp.logical_or(i == 0, blk_e_ref[i] != blk_e_ref[jnp.maximum(i - 1, 0)])

        @pl.when(changed)
        def _():
            wgu_sc[...] = wgu_ref[0].astype(MXU_DTYPE)
            wdn_sc[...] = wdn_ref[0].astype(MXU_DTYPE)

        gu = jnp.dot(x_ref[...].astype(MXU_DTYPE), wgu_sc[...], preferred_element_type=f32)
        gate = gu[:, :D_EXPERT]
        act = gate * (1.0 / (1.0 + jnp.exp(-gate))) * gu[:, D_EXPERT:]
        y_ref[...] = jnp.dot(act.astype(MXU_DTYPE), wdn_sc[...], preferred_element_type=f32)

    @pl.when(i >= nblk_ref[0])
    def _():
        y_ref[...] = jnp.zeros(y_ref.shape, f32)


def _moe_combine_kernel(dest_ref, x_ref, wt_ref, ybuf_ref, o_ref, rows_sc, sem):
    tb = x_ref.shape[0]

    def row_copy(r, k):
        return pltpu.make_async_copy(ybuf_ref.at[pl.ds(dest_ref[0, 0, 2 * r + k], 1)],
                                     rows_sc.at[k, pl.ds(r, 1)], sem)

    def start(r, c):
        row_copy(r, 0).start()
        row_copy(r, 1).start()
        return c

    def wait(r, c):
        row_copy(r, 0).wait()
        row_copy(r, 1).wait()
        return c

    lax.fori_loop(0, tb, start, 0)
    lax.fori_loop(0, tb, wait, 0)
    wt = wt_ref[...]
    o_ref[...] = x_ref[...] + (wt[:, 0:1] * rows_sc[0] + wt[:, 1:2] * rows_sc[1])


def _hier_moe_residual(x, ln_g, w_grp, b_grp, w_exp, b_exp, w_gu, w_dn):
    T, D = x.shape
    f32 = jnp.float32
    tb = min(256, T)
    assert T % tb == 0
    nt = T // tb
    pad_cols = ROUTE_COLS - N_GROUPS - N_EXPERTS
    w_r = jnp.pad(jnp.concatenate([w_grp, w_exp], axis=1), ((0, 0), (0, pad_cols))).astype(MXU_DTYPE)
    b_r = jnp.pad(jnp.concatenate([b_grp, b_exp]), (0, pad_cols)).reshape(1, ROUTE_COLS)
    tri = jnp.asarray(np.tril(np.ones((tb, tb), np.float32), -1), dtype=MXU_DTYPE)
    row = lambda i: (i, 0)
    fixed = lambda i: (0, 0)
    h, meta, wt, cnt = pl.pallas_call(
        _moe_route_kernel,
        grid=(nt,),
        in_specs=[pl.BlockSpec((tb, D), row), pl.BlockSpec((1, D), fixed),
                  pl.BlockSpec((D, ROUTE_COLS), fixed), pl.BlockSpec((1, ROUTE_COLS), fixed),
                  pl.BlockSpec((tb, tb), fixed)],
        out_specs=[pl.BlockSpec((tb, D), row), pl.BlockSpec((tb, ROUTE_COLS), row),
                   pl.BlockSpec((tb, ROUTE_COLS), row), pl.BlockSpec((1, ROUTE_COLS), fixed)],
        out_shape=[jax.ShapeDtypeStruct((T, D), f32), jax.ShapeDtypeStruct((T, ROUTE_COLS), jnp.int32),
                   jax.ShapeDtypeStruct((T, ROUTE_COLS), f32), jax.ShapeDtypeStruct((1, ROUTE_COLS), f32)],
        scratch_shapes=[pltpu.VMEM((1, ROUTE_COLS), f32)],
        compiler_params=pltpu.CompilerParams(dimension_semantics=("arbitrary",)),
        name="moe_route",
    )(x, ln_g.reshape(1, D), w_r, b_r, tri)

    counts = cnt[0, N_GROUPS:N_GROUPS + N_EXPERTS].astype(jnp.int32)
    padded = (counts + MOE_ROWS - 1) // MOE_ROWS * MOE_ROWS
    pad_end = jnp.cumsum(padded)
    pad_start = pad_end - padded
    n_blocks = -(-(T * TOP_K) // MOE_ROWS) + N_EXPERTS
    n_slots = n_blocks * MOE_ROWS
    dest = pad_start[meta[:, 0:2]] + meta[:, 2:4]
    dest = dest.reshape(nt, 1, 2 * tb)
    starts = jnp.arange(n_blocks, dtype=jnp.int32) * MOE_ROWS
    used = (pad_end[-1] // MOE_ROWS).astype(jnp.int32).reshape(1)
    blk_e = jnp.minimum(jnp.searchsorted(pad_end, jnp.minimum(starts, pad_end[-1] - 1), side='right'),
                        N_EXPERTS - 1).astype(jnp.int32)

    dest_spec = pl.BlockSpec((1, 1, 2 * tb), lambda i: (i, 0, 0), memory_space=pltpu.SMEM)
    xbuf = pl.pallas_call(
        _moe_dispatch_kernel,
        grid=(nt,),
        in_specs=[dest_spec, pl.BlockSpec((tb, D), row), pl.BlockSpec(memory_space=pl.ANY)],
        out_specs=pl.BlockSpec(memory_space=pl.ANY),
        out_shape=jax.ShapeDtypeStruct((n_slots, D), f32),
        scratch_shapes=[pltpu.SemaphoreType.DMA(())],
        input_output_aliases={2: 0},
        compiler_params=pltpu.CompilerParams(dimension_semantics=("arbitrary",)),
        name="moe_dispatch",
    )(dest, h, jnp.zeros((n_slots, D), f32))

    ybuf = pl.pallas_call(
        _moe_expert_kernel,
        grid_spec=pltpu.PrefetchScalarGridSpec(
            num_scalar_prefetch=2,
            grid=(n_blocks,),
            in_specs=[pl.BlockSpec((MOE_ROWS, D), lambda i, be, nb: (i, 0)),
                      pl.BlockSpec((1, D, 2 * D_EXPERT), lambda i, be, nb: (be[i], 0, 0)),
                      pl.BlockSpec((1, D_EXPERT, D), lambda i, be, nb: (be[i], 0, 0))],
            out_specs=pl.BlockSpec((MOE_ROWS, D), lambda i, be, nb: (i, 0)),
            scratch_shapes=[pltpu.VMEM((D, 2 * D_EXPERT), MXU_DTYPE), pltpu.VMEM((D_EXPERT, D), MXU_DTYPE)]),
        out_shape=jax.ShapeDtypeStruct((n_slots, D), f32),
        compiler_params=pltpu.CompilerParams(dimension_semantics=("arbitrary",),
                                             vmem_limit_bytes=48 * 1024 * 1024),
        name="moe_experts",
    )(blk_e, used, xbuf, w_gu, w_dn)

    return pl.pallas_call(
        _moe_combine_kernel,
        grid=(nt,),
        in_specs=[dest_spec, pl.BlockSpec((tb, D), row), pl.BlockSpec((tb, ROUTE_COLS), row),
                  pl.BlockSpec(memory_space=pl.ANY)],
        out_specs=pl.BlockSpec((tb, D), row),
        out_shape=jax.ShapeDtypeStruct((T, D), f32),
        scratch_shapes=[pltpu.VMEM((TOP_K, tb, D), f32), pltpu.SemaphoreType.DMA(())],
        compiler_params=pltpu.CompilerParams(dimension_semantics=("arbitrary",)),
        name="moe_combine",
    )(dest, x, wt, ybuf)


def _gelu_tanh(x):
    return 0.5 * x * (1.0 + jnp.tanh(math.sqrt(2.0 / math.pi) * (x + 0.044715 * (x * x * x))))


def _gmlp_kernel(x_ref, g_ref, win_ref, bin_ref, lng_ref, lnb_ref, ws_ref, bs_ref, wout_ref,
                 o_ref, v_ref, *, single_position):
    f32 = jnp.float32
    x = x_ref[...]
    rows = x.shape[0]
    h = x * lax.rsqrt(jnp.mean(x * x, axis=-1, keepdims=True) + EPS) * g_ref[...]
    z = _gelu_tanh(jnp.dot(h.astype(MXU_DTYPE), win_ref[...], preferred_element_type=f32) + bin_ref[...])
    u = z[:, :GM_WIDTH]
    v = z[:, GM_WIDTH:]
    mu = jnp.mean(v, axis=-1, keepdims=True)
    var = jnp.mean(jnp.square(v - mu), axis=-1, keepdims=True)
    v = (v - mu) * lax.rsqrt(var + EPS) * lng_ref[...] + lnb_ref[...]
    v_ref[0] = v
    vb = v.astype(MXU_DTYPE)
    if single_position:
        s = ws_ref[...].astype(f32) * vb.astype(f32) + bs_ref[...]
    else:
        parts = []
        for c in range(rows // CHUNK):
            vc = vb[c * CHUNK:(c + 1) * CHUNK]
            parts.append(jnp.concatenate(
                [jnp.dot(ws_ref[g], vc[:, g * GM_GROUP_DIM:(g + 1) * GM_GROUP_DIM], preferred_element_type=f32)
                 + bs_ref[g] for g in range(GM_GROUPS)], axis=1))
        s = jnp.concatenate(parts, axis=0)
    y = jnp.dot((u * s).astype(MXU_DTYPE), wout_ref[...], preferred_element_type=f32)
    o_ref[...] = x + y


def _gmlp_residual(x, ln_g, w_in, b_in, ln2_g, ln2_b, w_s, b_s, w_out, seq):
    T, D = x.shape
    f32 = jnp.float32
    cd = MXU_DTYPE
    single = seq == 1
    tb = T if single else 2 * CHUNK
    assert T % tb == 0 and (single or seq % tb == 0)
    steps_per_seq = 1 if single else seq // tb
    if single:
        ws = jnp.repeat(w_s[:, 0, 0], GM_GROUP_DIM).reshape(1, GM_WIDTH).astype(cd)
        bs = jnp.repeat(b_s[:, 0], GM_GROUP_DIM).reshape(1, GM_WIDTH)
        ws_spec = pl.BlockSpec((1, GM_WIDTH), lambda i: (0, 0))
        bs_spec = pl.BlockSpec((1, GM_WIDTH), lambda i: (0, 0))
    else:
        causal = np.tril(np.ones((CHUNK, CHUNK), bool))
        ws = jnp.where(causal[None], w_s, 0).astype(cd)
        bs = jnp.broadcast_to(b_s[:, :, None], (GM_GROUPS, CHUNK, GM_GROUP_DIM))
        ws_spec = pl.BlockSpec((GM_GROUPS, CHUNK, CHUNK), lambda i: (0, 0, 0))
        bs_spec = pl.BlockSpec((GM_GROUPS, CHUNK, GM_GROUP_DIM), lambda i: (0, 0, 0))
    fixed = lambda i: (0, 0)
    row = lambda i: (i, 0)
    return pl.pallas_call(
        functools.partial(_gmlp_kernel, single_position=single),
        grid=(T // tb,),
        in_specs=[pl.BlockSpec((tb, D), row), pl.BlockSpec((1, D), fixed),
                  pl.BlockSpec((D, 2 * GM_WIDTH), fixed), pl.BlockSpec((1, 2 * GM_WIDTH), fixed),
                  pl.BlockSpec((1, GM_WIDTH), fixed), pl.BlockSpec((1, GM_WIDTH), fixed),
                  ws_spec, bs_spec, pl.BlockSpec((GM_WIDTH, D), fixed)],
        out_specs=[pl.BlockSpec((tb, D), row), pl.BlockSpec((1, tb, GM_WIDTH), lambda i: (i // steps_per_seq, 0, 0))],
        out_shape=[jax.ShapeDtypeStruct((T, D), f32),
                   jax.ShapeDtypeStruct((T // (tb * steps_per_seq), tb, GM_WIDTH), f32)],
        compiler_params=pltpu.CompilerParams(dimension_semantics=("arbitrary",),
                                             vmem_limit_bytes=56 * 1024 * 1024),
        name="gmlp",
    )(x, ln_g.reshape(1, D), w_in.astype(cd), b_in.reshape(1, -1), ln2_g.reshape(1, -1), ln2_b.reshape(1, -1),
      ws, bs, w_out.astype(cd))


def rmsnorm(x, g):
    xf = x.astype(jnp.float32)
    y = xf * lax.rsqrt(jnp.mean(xf * xf, axis=-1, keepdims=True) + EPS)
    return (y * g.astype(jnp.float32)).astype(x.dtype)


def layernorm(x, g, b):
    xf = x.astype(jnp.float32)
    mu = jnp.mean(xf, axis=-1, keepdims=True)
    var = jnp.mean(jnp.square(xf - mu), axis=-1, keepdims=True)
    return ((xf - mu) * lax.rsqrt(var + EPS) * g.astype(jnp.float32) + b.astype(jnp.float32)).astype(x.dtype)


def rel_bucket(n):
    n = jnp.maximum(n, 0)
    max_exact = N_BUCKETS // 2
    nf = jnp.maximum(n, 1).astype(jnp.float32)
    large = max_exact + (jnp.log(nf / max_exact) / math.log(MAX_DISTANCE / max_exact)
                         * (N_BUCKETS - max_exact)).astype(jnp.int32)
    large = jnp.minimum(large, N_BUCKETS - 1)
    return jnp.where(n < max_exact, n, large)


def head_bias(rel, rel_bias):
    q, k = rel.shape
    b = rel_bias[rel_bucket(rel)]
    return b.reshape(q, k, KV_HEADS, Q_PER_KV).transpose(2, 3, 0, 1)


def masked_softmax(logits, mask):
    z = jnp.where(mask, logits.astype(jnp.float32), -1e30)
    return jax.nn.softmax(z, axis=-1) * mask


def nsa_project(xn, w_in, q_gain, k_gain):
    B, T, _ = xn.shape
    hq = N_HEADS * HEAD_DIM
    hkv = 6 * KV_HEADS * HEAD_DIM
    h = xn @ w_in
    q = rmsnorm(h[..., :hq].reshape(B, T, N_HEADS, HEAD_DIM), q_gain)
    kv = h[..., hq:hq + hkv].reshape(B, T, 6, KV_HEADS, HEAD_DIM)
    gates = jax.nn.sigmoid(h[..., hq + hkv:].astype(jnp.float32)).reshape(B, T, N_HEADS, 3)
    k_sel = rmsnorm(kv[:, :, 2], k_gain[1])
    k_win = rmsnorm(kv[:, :, 4], k_gain[2])
    return q, kv[:, :, 0], kv[:, :, 1], k_sel, kv[:, :, 3], k_win, kv[:, :, 5], gates


def compress(rows, pe, w1, w2):
    B, T = rows.shape[:2]
    n_ch = T // CMP_STRIDE
    c = rows[:, :n_ch * CMP_STRIDE].reshape(B, n_ch, CMP_STRIDE, KV_HEADS, HEAD_DIM)
    blk = jnp.concatenate([c[:, :-1], c[:, 1:]], axis=2) + pe[None, None, :, None, :]
    h = jax.nn.gelu(jnp.einsum('bnlkd,ldf->bnkf', blk, w1.reshape(CMP_LEN, HEAD_DIM, CMP_HID)))
    return jnp.einsum('bnkf,fd->bnkd', h, w2)


def compressed_kv(k_raw, v_raw, k_gain0, pe, w1, w2):
    kc = rmsnorm(compress(k_raw, pe[0], w1[0], w2[0]), k_gain0)
    vc = compress(v_raw, pe[1], w1[1], w2[1])
    end = jnp.arange(kc.shape[1]) * CMP_STRIDE + (CMP_LEN - 1)
    return kc, vc, end


def nsa_core(q, q_pos, kc, vc, kc_end, ks, vs, kw, vw, kw_pos, gates, rel_bias):
    B, Q = q.shape[:2]
    nc = kc.shape[1]
    scale = HEAD_DIM ** -0.5
    qh = q.reshape(B, Q, KV_HEADS, Q_PER_KV, HEAD_DIM)
    lc = jnp.einsum('bqkgd,bnkd->bkgqn', qh, kc) * scale + head_bias(q_pos[:, None] - kc_end[None, :], rel_bias)
    pc = masked_softmax(lc, kc_end[None, :] <= q_pos[:, None])
    oc = jnp.einsum('bkgqn,bnkd->bqkgd', pc.astype(vc.dtype), vc)
    nb = ks.shape[1] // SEL_BLOCK
    imp = jnp.pad(pc.sum(axis=2), ((0, 0), (0, 0), (0, 0), (0, nb * CMP_PER_SEL - nc)))
    imp = imp.reshape(B, KV_HEADS, Q, nb, CMP_PER_SEL).sum(-1)
    blk = jnp.arange(nb)
    d = (q_pos // SEL_BLOCK)[:, None] - blk[None, :]
    forced = (blk[None, :] == 0) | ((d >= 0) & (d < N_LOCAL))
    score = jnp.where(forced, jnp.inf, jnp.where(d >= 0, imp, -jnp.inf))
    n_sel = min(N_SEL, nb)
    _, idx = lax.top_k(score, n_sel)
    bi = jnp.arange(B)[:, None, None, None]
    ki = jnp.arange(KV_HEADS)[None, :, None, None]
    ksb = ks.reshape(B, nb, SEL_BLOCK, KV_HEADS, HEAD_DIM).transpose(0, 3, 1, 2, 4)
    vsb = vs.reshape(B, nb, SEL_BLOCK, KV_HEADS, HEAD_DIM).transpose(0, 3, 1, 2, 4)
    kg = ksb[bi, ki, idx].reshape(B, KV_HEADS, Q, n_sel * SEL_BLOCK, HEAD_DIM)
    vg = vsb[bi, ki, idx].reshape(B, KV_HEADS, Q, n_sel * SEL_BLOCK, HEAD_DIM)
    kpos = (idx[..., None] * SEL_BLOCK + jnp.arange(SEL_BLOCK)).reshape(B, KV_HEADS, Q, n_sel * SEL_BLOCK)
    rel_s = q_pos[None, None, :, None] - kpos
    tb = rel_bias.reshape(N_BUCKETS, KV_HEADS, Q_PER_KV)
    bs = tb[rel_bucket(rel_s), ki].transpose(0, 1, 4, 2, 3)
    ls = jnp.einsum('bqkgd,bkqjd->bkgqj', qh, kg) * scale + bs
    ps = masked_softmax(ls, (rel_s >= 0)[:, :, None])
    os_ = jnp.einsum('bkgqj,bkqjd->bqkgd', ps.astype(vg.dtype), vg)
    rel_w = q_pos[:, None] - kw_pos[None, :]
    mw = (rel_w >= 0) & (rel_w <= WINDOW) & (kw_pos[None, :] >= 0)
    lw = jnp.einsum('bqkgd,bwkd->bkgqw', qh, kw) * scale + head_bias(rel_w, rel_bias)
    pw = masked_softmax(lw, mw)
    ow = jnp.einsum('bkgqw,bwkd->bqkgd', pw.astype(vw.dtype), vw)
    g = gates.reshape(B, Q, KV_HEADS, Q_PER_KV, 3).astype(oc.dtype)
    o = g[..., 0:1] * oc + g[..., 1:2] * os_ + g[..., 2:3] * ow
    return o.reshape(B, Q, N_HEADS * HEAD_DIM)


def nsa_prompt(xn, w_in, w_out, q_gain, k_gain, pe, w1, w2, rel_bias):
    B, S, _ = xn.shape
    q, kcr, vcr, ks, vs, kw, vw, gates = nsa_project(xn, w_in, q_gain, k_gain)
    kc, vc, kc_end = compressed_kv(kcr, vcr, k_gain[0], pe, w1, w2)
    o = _nsa_attention(q, kc, vc, ks, vs, kw, vw, gates, rel_bias)
    w_keep = min(WINDOW, S)
    new_rows = jnp.stack([kcr, vcr, ks, vs], axis=2)
    new_win = jnp.stack([kw[:, S - w_keep:], vw[:, S - w_keep:]], axis=2)
    y = _mm(o.reshape(B * S, -1), w_out).reshape(B, S, -1)
    return y, new_rows, new_win


def nsa_sample(xn, past_rows, win_buf, w_in, w_out, q_gain, k_gain, pe, w1, w2, rel_bias):
    B, L, _ = xn.shape
    P = past_rows.shape[1]
    q, kcr, vcr, ks, vs, kw, vw, gates = nsa_project(xn, w_in, q_gain, k_gain)
    new_rows = jnp.stack([kcr, vcr, ks, vs], axis=2)
    rows = jnp.concatenate([past_rows, new_rows], axis=1)
    kc, vc, kc_end = compressed_kv(rows[:, :, 0], rows[:, :, 1], k_gain[0], pe, w1, w2)
    T = P + L
    tp = -(-T // SEL_BLOCK) * SEL_BLOCK
    pad = ((0, 0), (0, tp - T), (0, 0), (0, 0))
    ks_all = jnp.pad(rows[:, :, 2], pad)
    vs_all = jnp.pad(rows[:, :, 3], pad)
    win_all = jnp.concatenate([win_buf, jnp.stack([kw, vw], axis=2)], axis=1)
    wb = win_buf.shape[1]
    kw_pos = P - wb + jnp.arange(wb + L)
    q_pos = P + jnp.arange(L)
    o = nsa_core(q, q_pos, kc, vc, kc_end, ks_all, vs_all, win_all[:, :, 0], win_all[:, :, 1],
                 kw_pos, gates, rel_bias)
    w_keep = min(WINDOW, wb + L)
    return o @ w_out, new_rows, win_all[:, wb + L - w_keep:]


def gmlp_mixer(xn, w_in, b_in, ln_g, ln_b, w_s, b_s, w_out):
    B, L, _ = xn.shape
    z = jax.nn.gelu(xn @ w_in + b_in)
    u = z[..., :GM_WIDTH]
    v = layernorm(z[..., GM_WIDTH:], ln_g, ln_b)
    lp = -(-L // CHUNK) * CHUNK
    vp = jnp.pad(v, ((0, 0), (0, lp - L), (0, 0))).reshape(B, lp // CHUNK, CHUNK, GM_GROUPS, GM_GROUP_DIM)
    causal = jnp.tril(jnp.ones((CHUNK, CHUNK), dtype=bool))
    ws = jnp.where(causal[None], w_s, 0)
    s = jnp.einsum('gts,bcsgd->bctgd', ws, vp) + b_s.T[None, None, :, :, None]
    s = s.reshape(B, lp, GM_WIDTH)[:, :L]
    start = ((L - 1) // CHUNK) * CHUNK
    return (u * s) @ w_out, v[:, start:]


def hier_moe(x, w_grp, b_grp, w_exp, b_exp, w_gu, w_dn):
    T, D = x.shape
    pg = jax.nn.softmax((x @ w_grp + b_grp).astype(jnp.float32), axis=-1)
    grp = jnp.argmax(pg, axis=-1)
    g_w = jnp.take_along_axis(pg, grp[:, None], axis=-1)
    le = (x @ w_exp + b_exp).astype(jnp.float32).reshape(T, N_GROUPS, EXPERTS_PER_GROUP)
    le = jnp.take_along_axis(le, grp[:, None, None], axis=1)[:, 0]
    top_v, top_i = lax.top_k(le, TOP_K)
    wts = (jax.nn.softmax(top_v, axis=-1) * g_w).reshape(-1)
    eid = (grp[:, None] * EXPERTS_PER_GROUP + top_i).reshape(-1)
    n = T * TOP_K
    order = jnp.argsort(eid)
    e_s = eid[order]
    tok_s = order // TOP_K
    w_s = wts[order]
    counts = jnp.bincount(eid, length=N_EXPERTS)
    padded = (counts + MOE_BLOCK - 1) // MOE_BLOCK * MOE_BLOCK
    pad_end = jnp.cumsum(padded)
    pad_start = pad_end - padded
    start = jnp.cumsum(counts) - counts
    dest = pad_start[e_s] + jnp.arange(n) - start[e_s]
    n_blocks = -(-n // MOE_BLOCK) + N_EXPERTS
    xbuf = jnp.zeros((n_blocks * MOE_BLOCK, D), x.dtype).at[dest].set(x[tok_s])
    blk_e = jnp.minimum(jnp.searchsorted(pad_end, jnp.arange(n_blocks) * MOE_BLOCK, side='right'), N_EXPERTS - 1)

    def expert(args):
        xb, e = args
        gu = xb @ w_gu[e]
        return (jax.nn.silu(gu[:, :D_EXPERT]) * gu[:, D_EXPERT:]) @ w_dn[e]

    ybuf = lax.map(expert, (xbuf.reshape(n_blocks, MOE_BLOCK, D), blk_e)).reshape(-1, D)
    return jax.ops.segment_sum(ybuf[dest] * w_s[:, None].astype(x.dtype), tok_s, num_segments=T)


def kernel(x_prompt, x_sample, cache_nsa_kv, state_win_kv, page_table, rel_bias, ln_mix, ln_ffn,
           nsa_w_in, nsa_w_out, nsa_q_gain, nsa_k_gain, cmp_pe, cmp_w1, cmp_w2,
           gm_w_in, gm_b_in, gm_ln_g, gm_ln_b, gm_w_s, gm_b_s, gm_w_out,
           moe_w_grp, moe_b_grp, moe_w_exp, moe_b_exp, moe_w_gu, moe_w_dn):
    xp = x_prompt
    xs = x_sample
    dec_b = x_sample.shape[0]
    past_len = page_table.shape[1] * PAGE_SIZE
    kv_p, kv_s, win_p, win_s, gv_p, gv_s = [], [], [], [], [], []
    for i in range(DEPTH):
        a = i // N_MIXERS
        if i % N_MIXERS == 0:
            hs = rmsnorm(xs, ln_mix[i])
            xp_new, rp, wp = _nsa_prompt_layer(xp, ln_mix[i], nsa_w_in[a], nsa_w_out[a], nsa_q_gain[a],
                                               nsa_k_gain[a], cmp_pe[a], cmp_w1[a], cmp_w2[a], rel_bias)
            past = jnp.take(cache_nsa_kv, page_table, axis=0)[:, :, :, a]
            past = past.reshape(dec_b, past_len, 4, KV_HEADS, HEAD_DIM)
            ys, rs, ws = nsa_sample(hs, past, state_win_kv[a], nsa_w_in[a], nsa_w_out[a], nsa_q_gain[a],
                                    nsa_k_gain[a], cmp_pe[a], cmp_w1[a], cmp_w2[a], rel_bias)
            kv_p.append(rp)
            kv_s.append(rs)
            win_p.append(wp)
            win_s.append(ws)
            xp = xp_new
            xs = xs + ys
        else:
            gm = (ln_mix[i], gm_w_in[a], gm_b_in[a], gm_ln_g[a], gm_ln_b[a], gm_w_s[a], gm_b_s[a], gm_w_out[a])
            bp, sp = xp.shape[:2]
            bs_, ss = xs.shape[:2]
            xp2, vp = _gmlp_residual(xp.reshape(-1, D_MODEL), *gm, seq=sp)
            xs2, vs = _gmlp_residual(xs.reshape(-1, D_MODEL), *gm, seq=ss)
            xp = xp2.reshape(xp.shape)
            xs = xs2.reshape(xs.shape)
            start = ((sp - 1) // CHUNK) * CHUNK
            gv_p.append(vp[:, vp.shape[1] - (sp - start):])
            gv_s.append(vs.reshape(bs_, ss, GM_WIDTH))
        moe = (ln_ffn[i], moe_w_grp[i], moe_b_grp[i], moe_w_exp[i], moe_b_exp[i], moe_w_gu[i], moe_w_dn[i])
        xp = _hier_moe_residual(xp.reshape(-1, D_MODEL), *moe).reshape(xp.shape)
        xs = _hier_moe_residual(xs.reshape(-1, D_MODEL), *moe).reshape(xs.shape)
    new_kv_prompt = jnp.stack(kv_p, axis=2)
    new_kv_sample = jnp.stack(kv_s, axis=2)
    new_win_prompt = jnp.stack(win_p, axis=0)
    new_win_sample = jnp.stack(win_s, axis=0)
    new_gm_v_prompt = jnp.stack(gv_p, axis=0)
    new_gm_v_sample = jnp.stack(gv_s, axis=0)
    return (xp, xs, new_kv_prompt, new_kv_sample, new_win_prompt, new_win_sample, new_gm_v_prompt, new_gm_v_sample)
```

```python
import functools
import math

import jax
import jax.numpy as jnp
import numpy as np
from jax import lax
from jax.experimental import pallas as pl
from jax.experimental.pallas import tpu as pltpu

D_MODEL = 1024
PAGE_SIZE = 128
DEPTH = 2
N_MIXERS = 2
N_HEADS = 16
HEAD_DIM = 64
KV_HEADS = 4
Q_PER_KV = N_HEADS // KV_HEADS
CMP_LEN = 32
CMP_STRIDE = 16
CMP_HID = 64
SEL_BLOCK = 64
CMP_PER_SEL = SEL_BLOCK // CMP_STRIDE
N_SEL = 16
N_LOCAL = 2
WINDOW = 512
Q_BLOCK = 128
N_BUCKETS = 32
MAX_DISTANCE = 128
CHUNK = 128
GM_WIDTH = 2048
GM_GROUPS = 8
GM_GROUP_DIM = GM_WIDTH // GM_GROUPS
N_GROUPS = 4
EXPERTS_PER_GROUP = 8
N_EXPERTS = N_GROUPS * EXPERTS_PER_GROUP
TOP_K = 2
D_EXPERT = 512
MOE_BLOCK = 128
EPS = 1e-6


def _mm_kernel(x_ref, w_ref, o_ref):
    o_ref[...] = jnp.dot(x_ref[...].astype(jnp.bfloat16), w_ref[...].astype(jnp.bfloat16),
                         preferred_element_type=jnp.float32)


def _mm(x, w, tm=512, tn=512):
    m, k = x.shape
    n = w.shape[1]
    tm = min(tm, m)
    tn = min(tn, n)
    return pl.pallas_call(
        _mm_kernel,
        grid=(m // tm, n // tn),
        in_specs=[pl.BlockSpec((tm, k), lambda i, j: (i, 0)),
                  pl.BlockSpec((k, tn), lambda i, j: (0, j))],
        out_specs=pl.BlockSpec((tm, tn), lambda i, j: (i, j)),
        out_shape=jax.ShapeDtypeStruct((m, n), jnp.float32),
        name="mm",
    )(x, w)


NEG = -1e30
MXU_DTYPE = jnp.bfloat16
TQ = 256
TK = 256
LANES = 128
BIAS_SPAN = 128


def _bucket_of_distance():
    d = np.arange(BIAS_SPAN)
    max_exact = N_BUCKETS // 2
    nf = np.maximum(d, 1).astype(np.float32)
    large = max_exact + (np.log(nf / np.float32(max_exact)) / np.float32(math.log(MAX_DISTANCE / max_exact))
                         * np.float32(N_BUCKETS - max_exact)).astype(np.int32)
    large = np.minimum(large, N_BUCKETS - 1)
    return np.where(d < max_exact, d, large).astype(np.int32)


def _attn_bias_tiles(rel_bias, seq):
    fd = rel_bias[_bucket_of_distance()].T
    far = fd[:, BIAS_SPAN - 1]
    fd = fd - far[:, None]
    i = np.arange(TQ)[:, None]
    j = np.arange(TK)[None, :]
    d0 = i - j
    b0 = jnp.where(d0 >= 0, jnp.take(fd, np.clip(d0, 0, BIAS_SPAN - 1), axis=1), NEG)
    b1 = jnp.take(fd, np.clip(TQ + i - j, 0, BIAS_SPAN - 1), axis=1)
    w2 = np.where(2 * TQ + i - j <= WINDOW, 0.0, NEG).astype(np.float32)
    nc = seq // CMP_STRIDE
    r = np.arange(2 * nc)[:, None]
    dc = np.arange(TQ)[None, :] - CMP_STRIDE * (r - nc) - (CMP_LEN - 1)
    pat = jnp.where(dc >= 0, jnp.take(fd, np.clip(dc, 0, BIAS_SPAN - 1), axis=1), NEG)
    return b0, b1, jnp.asarray(w2), pat


def _nsa_attn_kernel(q_ref, kc_ref, vct_ref, kst_ref, vs_ref, kwt_ref, vw_ref, oh_ref, g_ref,
                     b0_ref, b1_ref, w2_ref, pat_ref, o_ref, m_sc, l_sc, acc_sc, imp_sc, *, nc, nb):
    f32 = jnp.float32
    qb = pl.program_id(2)

    r0 = pl.multiple_of(nc - (TQ // CMP_STRIDE) * qb, TQ // CMP_STRIDE)
    t_row = qb * TQ + lax.broadcasted_iota(jnp.int32, (1, TQ), 1)
    has_cmp = t_row >= CMP_LEN - 1
    for h in range(Q_PER_KV):
        st = lax.dot_general(kc_ref[0, 0], q_ref[0, h], (((1,), (1,)), ((), ())),
                             preferred_element_type=f32)
        st = st + pat_ref[h, pl.ds(r0, nc), :]
        e = jnp.exp(st - jnp.max(st, axis=0, keepdims=True))
        inv = jnp.where(has_cmp, 1.0 / jnp.sum(e, axis=0, keepdims=True), 0.0)
        pt = e * inv
        for c in range(TQ // LANES):
            part = pt[:, c * LANES:(c + 1) * LANES]
            if h == 0:
                imp_sc[c] = part
            else:
                imp_sc[c] += part
        oct_h = jnp.dot(vct_ref[0, 0], pt.astype(MXU_DTYPE), preferred_element_type=f32)
        o_ref[0, :, h * HEAD_DIM:(h + 1) * HEAD_DIM] = g_ref[0, h][:, 0:1] * oct_h.T

    imp = jnp.concatenate(
        [sum(imp_sc[c, pl.ds(r, nb, stride=CMP_PER_SEL), :] for r in range(CMP_PER_SEL))
         for c in range(TQ // LANES)], axis=1)
    blk = lax.broadcasted_iota(jnp.int32, (nb, TQ), 0)
    t_blk = (qb * TQ + lax.broadcasted_iota(jnp.int32, (nb, TQ), 1)) // SEL_BLOCK
    dd = t_blk - blk
    forced = (blk == 0) | ((dd >= 0) & (dd < N_LOCAL))
    score = jnp.where(forced, jnp.inf, jnp.where(dd >= 0, imp, -jnp.inf))
    pen_t = jnp.full((nb, TQ), NEG, f32)
    for _ in range(min(N_SEL, nb)):
        best = jnp.max(score, axis=0, keepdims=True)
        first = jnp.min(jnp.where(score == best, blk, nb), axis=0, keepdims=True)
        hit = blk == first
        pen_t = jnp.where(hit, 0.0, pen_t)
        score = jnp.where(hit, -jnp.inf, score)
    pen = pen_t.T.astype(MXU_DTYPE)

    def reset():
        m_sc[...] = jnp.full(m_sc.shape, NEG, f32)
        l_sc[...] = jnp.zeros(l_sc.shape, f32)
        acc_sc[...] = jnp.zeros(acc_sc.shape, f32)

    def attend(h, kt, v, extra):
        s = jnp.dot(q_ref[0, h], kt, preferred_element_type=f32) + extra
        m_prev = m_sc[h]
        m_new = jnp.maximum(m_prev, jnp.max(s, axis=1, keepdims=True))
        p = jnp.exp(s - jnp.concatenate([m_new] * (TK // LANES), axis=1))
        alpha = jnp.exp(m_prev - m_new)
        l_sc[h] = alpha * l_sc[h] + jnp.sum(p, axis=1, keepdims=True)
        acc_sc[h] = alpha[:, :HEAD_DIM] * acc_sc[h] + jnp.dot(p.astype(MXU_DTYPE), v,
                                                             preferred_element_type=f32)
        m_sc[h] = m_new

    def emit(gate_col):
        for h in range(Q_PER_KV):
            w = g_ref[0, h][:, gate_col:gate_col + 1] / l_sc[h][:, :HEAD_DIM]
            o_ref[0, :, h * HEAD_DIM:(h + 1) * HEAD_DIM] += w * acc_sc[h]

    def selected_chunk(c, bias_ref):
        mterm = jnp.dot(pen, oh_ref[c], preferred_element_type=f32)
        for h in range(Q_PER_KV):
            extra = mterm if bias_ref is None else mterm + bias_ref[h]
            attend(h, kst_ref[0, 0, c], vs_ref[0, 0, c], extra)

    reset()
    selected_chunk(qb, b0_ref)

    @pl.when(qb >= 1)
    def _():
        selected_chunk(qb - 1, b1_ref)

    def far_body(c, carry):
        selected_chunk(c, None)
        return carry

    lax.fori_loop(0, jnp.maximum(qb - 1, 0), far_body, 0)
    emit(1)

    reset()
    for h in range(Q_PER_KV):
        attend(h, kwt_ref[0, 0, qb], vw_ref[0, 0, qb], b0_ref[h])

    @pl.when(qb >= 1)
    def _():
        for h in range(Q_PER_KV):
            attend(h, kwt_ref[0, 0, qb - 1], vw_ref[0, 0, qb - 1], b1_ref[h])

    @pl.when(qb >= 2)
    def _():
        for h in range(Q_PER_KV):
            attend(h, kwt_ref[0, 0, qb - 2], vw_ref[0, 0, qb - 2], w2_ref[...])

    emit(2)


def _nsa_attention(qs, kcp, vct, kst, vs, kwt, vw, gt, rel_bias):
    B, _, S, _ = qs.shape
    assert S % TQ == 0 and TQ == TK and WINDOW == 2 * TQ and S // SEL_BLOCK >= N_SEL
    nc = S // CMP_STRIDE
    nb = S // SEL_BLOCK
    nch = S // TK
    onehot = (np.arange(nb)[None, :, None] ==
              (np.arange(S) // SEL_BLOCK).reshape(nch, 1, TK)).astype(np.float32)
    onehot = jnp.asarray(onehot, dtype=MXU_DTYPE)
    b0, b1, w2, pat = _attn_bias_tiles(rel_bias, S)
    G = Q_PER_KV
    kv_chunks = pl.BlockSpec((1, 1, nch, HEAD_DIM, TK), lambda b, k, i: (b, k, 0, 0, 0))
    v_chunks = pl.BlockSpec((1, 1, nch, TK, HEAD_DIM), lambda b, k, i: (b, k, 0, 0, 0))
    return pl.pallas_call(
        functools.partial(_nsa_attn_kernel, nc=nc, nb=nb),
        grid=(B, KV_HEADS, S // TQ),
        in_specs=[
            pl.BlockSpec((1, G, TQ, HEAD_DIM), lambda b, k, i: (b, k, i, 0)),
            pl.BlockSpec((1, 1, nc, HEAD_DIM), lambda b, k, i: (b, k, 0, 0)),
            pl.BlockSpec((1, 1, HEAD_DIM, nc), lambda b, k, i: (b, k, 0, 0)),
            kv_chunks, v_chunks, kv_chunks, v_chunks,
            pl.BlockSpec((nch, nb, TK), lambda b, k, i: (0, 0, 0)),
            pl.BlockSpec((1, G, TQ, 3), lambda b, k, i: (b, k, i, 0)),
            pl.BlockSpec((G, TQ, TK), lambda b, k, i: (k, 0, 0)),
            pl.BlockSpec((G, TQ, TK), lambda b, k, i: (k, 0, 0)),
            pl.BlockSpec((TQ, TK), lambda b, k, i: (0, 0)),
            pl.BlockSpec((G, 2 * nc, TQ), lambda b, k, i: (k, 0, 0)),
        ],
        out_specs=pl.BlockSpec((1, TQ, G * HEAD_DIM), lambda b, k, i: (b, i, k)),
        out_shape=jax.ShapeDtypeStruct((B, S, N_HEADS * HEAD_DIM), jnp.float32),
        scratch_shapes=[
            pltpu.VMEM((G, TQ, LANES), jnp.float32),
            pltpu.VMEM((G, TQ, LANES), jnp.float32),
            pltpu.VMEM((G, TQ, HEAD_DIM), jnp.float32),
            pltpu.VMEM((TQ // LANES, nc, LANES), jnp.float32),
        ],
        compiler_params=pltpu.CompilerParams(
            dimension_semantics=("arbitrary", "arbitrary", "arbitrary"),
            vmem_limit_bytes=56 * 1024 * 1024),
        name="nsa_attn",
    )(qs, kcp, vct, kst, vs, kwt, vw, onehot, gt, b0, b1, w2, pat)


def _nsa_prompt_layer(x, ln_g, w_in, w_out, q_gain, k_gain, pe, w1, w2, rel_bias):
    B, S, D = x.shape
    x2 = x.reshape(B * S, D)
    qs, rows, win, kst, vs, kwt, vw, c, gates = _nsa_project(x2, ln_g, w_in, q_gain, k_gain, B, S)
    nc = S // CMP_STRIDE
    cmp = _compress(c.reshape(2, B, nc, CMP_STRIDE * SLOT), pe, w1, w2, k_gain[0])
    cmp = cmp.reshape(2, B, nc, KV_HEADS, HEAD_DIM).astype(MXU_DTYPE)
    gt = gates[:, :3 * N_HEADS].reshape(B, S, N_HEADS, 3).transpose(0, 2, 1, 3)
    o = _nsa_attention(qs, cmp[0].transpose(0, 2, 1, 3), cmp[1].transpose(0, 2, 3, 1), kst, vs, kwt, vw, gt,
                       rel_bias)
    y = _proj_residual(x2, o.reshape(B * S, HQ), w_out)
    return (y.reshape(B, S, D), rows.reshape(B, S, 4, KV_HEADS, HEAD_DIM),
            win.reshape(B, WINDOW, 2, KV_HEADS, HEAD_DIM))


def _page_gather_kernel(pt_ref, cache_ref, past_ref, sem, *, layer_slot0, n_slots):
    s = pl.program_id(0)
    n_pages = pt_ref.shape[2]

    def page_copy(pg, slot):
        return pltpu.make_async_copy(cache_ref.at[pt_ref[0, 0, pg], :, layer_slot0 + slot, :],
                                     past_ref.at[slot, s, pl.ds(pg * PAGE_SIZE, PAGE_SIZE), :], sem)

    def start(pg, c):
        for slot in range(n_slots):
            page_copy(pg, slot).start()
        return c

    def wait(pg, c):
        for slot in range(n_slots):
            page_copy(pg, slot).wait()
        return c

    lax.fori_loop(0, n_pages, start, 0)
    lax.fori_loop(0, n_pages, wait, 0)


def _page_gather(cache, page_table, layer):
    n_pool, _, n_layers = cache.shape[:3]
    nseq, n_pages = page_table.shape
    cache_v = cache.reshape(n_pool, PAGE_SIZE, n_layers * 4, SLOT)
    return pl.pallas_call(
        functools.partial(_page_gather_kernel, layer_slot0=layer * 4, n_slots=4),
        grid=(nseq,),
        in_specs=[pl.BlockSpec((1, 1, n_pages), lambda s: (s, 0, 0), memory_space=pltpu.SMEM),
                  pl.BlockSpec(memory_space=pl.ANY)],
        out_specs=pl.BlockSpec(memory_space=pl.ANY),
        out_shape=jax.ShapeDtypeStruct((4, nseq, n_pages * PAGE_SIZE, SLOT), cache.dtype),
        scratch_shapes=[pltpu.SemaphoreType.DMA(())],
        compiler_params=pltpu.CompilerParams(dimension_semantics=("arbitrary",)),
        name="page_gather",
    )(page_table.reshape(nseq, 1, n_pages), cache_v)


def _split3(x):
    hi = x.astype(MXU_DTYPE)
    r1 = x - hi.astype(jnp.float32)
    mid = r1.astype(MXU_DTYPE)
    return hi, mid, (r1 - mid.astype(jnp.float32)).astype(MXU_DTYPE)


def _nsa_sample_kernel(q_ref, kc_ref, vc_ref, ks_ref, vs_ref, win_ref, rows_ref, wnew_ref, g_ref,
                       oh_ref, grp_ref, own_ref, bc_ref, bs_ref, bw_ref, bn_ref, o_ref, *, nb):
    f32 = jnp.float32
    cd = MXU_DTYPE
    nt = (((1,), (1,)), ((), ()))
    q = q_ref[0]
    qf = q.astype(f32)

    def softmax_with_new(s, s_new):
        m = jnp.maximum(jnp.max(s, axis=1, keepdims=True), s_new)
        e = jnp.exp(s - m)
        e_new = jnp.exp(s_new - m)
        return e, e_new, 1.0 / (jnp.sum(e, axis=1, keepdims=True) + e_new)

    def new_key(row):
        return jnp.sum(qf * row.astype(cd).astype(f32), axis=1, keepdims=True) + bn_ref[...]

    sc = lax.dot_general(q, kc_ref[0], nt, preferred_element_type=f32) + bc_ref[...]
    ec = jnp.exp(sc - jnp.max(sc, axis=1, keepdims=True))
    pc = ec / jnp.sum(ec, axis=1, keepdims=True)
    out = g_ref[0][:, 0:1] * jnp.dot(pc.astype(cd), vc_ref[0], preferred_element_type=f32)

    imp = sum(pc[:, r * nb:(r + 1) * nb] for r in range(CMP_PER_SEL))
    imp = sum(jnp.dot(grp_ref[...], part, preferred_element_type=f32) for part in _split3(imp))
    blk = lax.broadcasted_iota(jnp.int32, imp.shape, 1)
    forced = (blk == 0) | (blk == nb - 1)
    score = jnp.where(forced, jnp.inf, imp)
    pen = jnp.full(imp.shape, NEG, f32)
    for _ in range(N_SEL - 1):
        best = jnp.max(score, axis=1, keepdims=True)
        first = jnp.min(jnp.where(score == best, blk, nb), axis=1, keepdims=True)
        hit = blk == first
        pen = jnp.where(hit, 0.0, pen)
        score = jnp.where(hit, -jnp.inf, score)

    ss = (lax.dot_general(q, ks_ref[0, 0].astype(cd), nt, preferred_element_type=f32)
          + jnp.dot(pen.astype(cd), oh_ref[...], preferred_element_type=f32) + bs_ref[...])
    rows = rows_ref[0]
    e, e_new, inv = softmax_with_new(ss, new_key(rows[:, 2 * SLOT:3 * SLOT]))
    acc = (jnp.dot(e.astype(cd), vs_ref[0, 0].astype(cd), preferred_element_type=f32)
           + e_new.astype(cd).astype(f32) * rows[:, 3 * SLOT:4 * SLOT].astype(cd).astype(f32))
    out = out + g_ref[0][:, 1:2] * inv * acc

    win = win_ref[0]
    wnew = wnew_ref[0]
    sw = lax.dot_general(q, win[:, :SLOT].astype(cd), nt, preferred_element_type=f32) + bw_ref[...]
    e, e_new, inv = softmax_with_new(sw, new_key(wnew[:, :SLOT]))
    acc = (jnp.dot(e.astype(cd), win[:, SLOT:].astype(cd), preferred_element_type=f32)
           + e_new.astype(cd).astype(f32) * wnew[:, SLOT:].astype(cd).astype(f32))
    out = out + g_ref[0][:, 2:3] * inv * acc
    o_ref[0] = out * own_ref[...]


def _nsa_sample_layer(x, past, win_buf, ln_g, w_in, w_out, q_gain, k_gain, pe, w1, w2, rel_bias):
    N, L, D = x.shape
    P = past.shape[2]
    wb = win_buf.shape[1]
    assert L == 1 and P % SEL_BLOCK == 0 and wb == WINDOW and P >= BIAS_SPAN and P // SEL_BLOCK >= N_SEL - 1
    f32 = jnp.float32
    cd = MXU_DTYPE
    nc = P // CMP_STRIDE
    nb = P // SEL_BLOCK
    x2 = x.reshape(N, D)
    q, rows, wnew, gates = _nsa_project(x2, ln_g, w_in, q_gain, k_gain, N, 1)
    cmp = _compress(past.reshape(4, N, nc, CMP_STRIDE * SLOT), pe, w1, w2, k_gain[0])
    cmp = cmp.reshape(2, N, nb, CMP_PER_SEL, SLOT).transpose(0, 1, 3, 2, 4).reshape(2, N, nc, SLOT).astype(cd)
    head_kv = np.arange(N_HEADS) // Q_PER_KV
    own = (head_kv[:, None] == (np.arange(SLOT) // HEAD_DIM)[None, :]).astype(np.float32)
    qbd = (jnp.tile(q.reshape(N, N_HEADS, HEAD_DIM), (1, 1, KV_HEADS)) * own * HEAD_DIM ** -0.5).astype(cd)
    grp = jnp.asarray(head_kv[:, None] == head_kv[None, :], dtype=cd)
    onehot = jnp.asarray(np.arange(nb)[:, None] == (np.arange(P) // SEL_BLOCK)[None, :], dtype=cd)
    fd = rel_bias[_bucket_of_distance()].T
    fd = fd - fd[:, BIAS_SPAN - 1:]
    cidx = (np.arange(nc) % nb) * CMP_PER_SEL + np.arange(nc) // nb
    dc = P - (cidx * CMP_STRIDE + CMP_LEN - 1)
    b_cmp = jnp.where(dc >= 0, jnp.take(fd, np.clip(dc, 0, BIAS_SPAN - 1), axis=1), NEG)
    b_sel = jnp.take(fd, np.clip(P - np.arange(P), 0, BIAS_SPAN - 1), axis=1)
    b_win = jnp.take(fd, np.clip(wb - np.arange(wb), 0, BIAS_SPAN - 1), axis=1)
    b_new = fd[:, 0:1]
    seq3 = lambda n: (n, 0, 0)
    fixed = lambda n: (0, 0)
    o = pl.pallas_call(
        functools.partial(_nsa_sample_kernel, nb=nb),
        grid=(N,),
        in_specs=[pl.BlockSpec((1, N_HEADS, SLOT), seq3),
                  pl.BlockSpec((1, nc, SLOT), seq3),
                  pl.BlockSpec((1, nc, SLOT), seq3),
                  pl.BlockSpec((1, 1, P, SLOT), lambda n: (2, n, 0, 0)),
                  pl.BlockSpec((1, 1, P, SLOT), lambda n: (3, n, 0, 0)),
                  pl.BlockSpec((1, wb, 2 * SLOT), seq3),
                  pl.BlockSpec((1, 1, 4 * SLOT), seq3),
                  pl.BlockSpec((1, 1, 2 * SLOT), seq3),
                  pl.BlockSpec((1, N_HEADS, 3), seq3),
                  pl.BlockSpec((nb, P), fixed), pl.BlockSpec((N_HEADS, N_HEADS), fixed),
                  pl.BlockSpec((N_HEADS, SLOT), fixed), pl.BlockSpec((N_HEADS, nc), fixed),
                  pl.BlockSpec((N_HEADS, P), fixed), pl.BlockSpec((N_HEADS, wb), fixed),
                  pl.BlockSpec((N_HEADS, 1), fixed)],
        out_specs=pl.BlockSpec((1, N_HEADS, SLOT), seq3),
        out_shape=jax.ShapeDtypeStruct((N, N_HEADS, SLOT), f32),
        compiler_params=pltpu.CompilerParams(dimension_semantics=("arbitrary",),
                                             vmem_limit_bytes=56 * 1024 * 1024),
        name="nsa_sample_attn",
    )(qbd, cmp[0], cmp[1], past, past, win_buf.reshape(N, wb, 2 * SLOT), rows.reshape(N, 1, 4 * SLOT),
      wnew.reshape(N, 1, 2 * SLOT), gates[:, :3 * N_HEADS].reshape(N, N_HEADS, 3),
      onehot, grp, jnp.asarray(own), b_cmp, b_sel, b_win, b_new)
    o = o.reshape(N, KV_HEADS, Q_PER_KV, KV_HEADS, HEAD_DIM)
    o = jnp.stack([o[:, k, :, k] for k in range(KV_HEADS)], axis=1).reshape(N, HQ)
    y = _proj_residual(x2, o, w_out)
    new_win = jnp.concatenate([win_buf, wnew.reshape(N, 1, 2, KV_HEADS, HEAD_DIM)], axis=1)[:, 1:]
    return y.reshape(N, 1, D), rows.reshape(N, 1, 4, KV_HEADS, HEAD_DIM), new_win


HQ = N_HEADS * HEAD_DIM
SLOT = KV_HEADS * HEAD_DIM
NSA_COLS = HQ + 6 * SLOT + 3 * N_HEADS
NSA_COLS_PAD = -(-NSA_COLS // LANES) * LANES
GATE_COLS = NSA_COLS_PAD - HQ - 6 * SLOT


def _nsa_proj_kernel(x_ref, g_ref, w_ref, qg_ref, kg_ref, *outs, prompt):
    f32 = jnp.float32
    x = x_ref[...]
    xn = x * lax.rsqrt(jnp.mean(x * x, axis=-1, keepdims=True) + EPS) * g_ref[...]
    h = jnp.dot(xn.astype(MXU_DTYPE), w_ref[...], preferred_element_type=f32)

    def head_norm(v, gain):
        return v * lax.rsqrt(jnp.mean(v * v, axis=-1, keepdims=True) + EPS) * gain

    def slot(s, kv):
        lo = HQ + s * SLOT + kv * HEAD_DIM
        return h[:, lo:lo + HEAD_DIM]

    gates = 1.0 / (1.0 + jnp.exp(-h[:, HQ + 6 * SLOT:]))
    if prompt:
        q_ref, rows_ref, win_ref, kst_ref, vs_ref, kwt_ref, vw_ref, c_ref, gate_ref = outs
        for hh in range(N_HEADS):
            qn = head_norm(h[:, hh * HEAD_DIM:(hh + 1) * HEAD_DIM], qg_ref[...])
            q_ref[0, hh] = (qn * HEAD_DIM ** -0.5).astype(MXU_DTYPE)
        c_ref[0, 0] = h[:, HQ:HQ + SLOT].astype(MXU_DTYPE)
        c_ref[1, 0] = h[:, HQ + SLOT:HQ + 2 * SLOT].astype(MXU_DTYPE)
    else:
        q_ref, rows_ref, win_ref, gate_ref = outs
        for hh in range(N_HEADS):
            q_ref[:, hh * HEAD_DIM:(hh + 1) * HEAD_DIM] = head_norm(h[:, hh * HEAD_DIM:(hh + 1) * HEAD_DIM],
                                                                    qg_ref[...])
    rows_ref[:, 0:2 * SLOT] = h[:, HQ:HQ + 2 * SLOT]
    rows_ref[:, 3 * SLOT:4 * SLOT] = h[:, HQ + 3 * SLOT:HQ + 4 * SLOT]
    for kv in range(KV_HEADS):
        lanes = slice(kv * HEAD_DIM, (kv + 1) * HEAD_DIM)
        ksn = head_norm(slot(2, kv), kg_ref[1:2, :])
        kwn = head_norm(slot(4, kv), kg_ref[2:3, :])
        rows_ref[:, 2 * SLOT + kv * HEAD_DIM:2 * SLOT + (kv + 1) * HEAD_DIM] = ksn
        if prompt:
            win_ref[0, 0, :, lanes] = kwn
            kst_ref[0, kv, 0] = ksn.T.astype(MXU_DTYPE)
            kwt_ref[0, kv, 0] = kwn.T.astype(MXU_DTYPE)
            vs_ref[0, kv, 0] = slot(3, kv).astype(MXU_DTYPE)
            vw_ref[0, kv, 0] = slot(5, kv).astype(MXU_DTYPE)
        else:
            win_ref[:, lanes] = kwn
    if prompt:
        win_ref[0, 0, :, SLOT:2 * SLOT] = h[:, HQ + 5 * SLOT:HQ + 6 * SLOT]
    else:
        win_ref[:, SLOT:2 * SLOT] = h[:, HQ + 5 * SLOT:HQ + 6 * SLOT]
    gate_ref[...] = gates


def _nsa_project(x, ln_g, w_in, q_gain, k_gain, batch, seq):
    T, D = x.shape
    f32 = jnp.float32
    cd = MXU_DTYPE
    prompt = seq > 1
    tb = TK if prompt else T
    w = jnp.pad(w_in, ((0, 0), (0, NSA_COLS_PAD - NSA_COLS))).astype(cd)
    fixed = lambda i: (0, 0)
    row = lambda i: (i, 0)
    in_specs = [pl.BlockSpec((tb, D), row), pl.BlockSpec((1, D), fixed), pl.BlockSpec((D, NSA_COLS_PAD), fixed),
                pl.BlockSpec((1, HEAD_DIM), fixed), pl.BlockSpec((3, HEAD_DIM), fixed)]
    if prompt:
        assert seq % tb == 0 and WINDOW == 2 * tb
        n = seq // tb
        kt_spec = pl.BlockSpec((1, KV_HEADS, 1, HEAD_DIM, tb), lambda i: (i // n, 0, i % n, 0, 0))
        v_spec = pl.BlockSpec((1, KV_HEADS, 1, tb, HEAD_DIM), lambda i: (i // n, 0, i % n, 0, 0))
        kt_shape = jax.ShapeDtypeStruct((batch, KV_HEADS, n, HEAD_DIM, tb), cd)
        v_shape = jax.ShapeDtypeStruct((batch, KV_HEADS, n, tb, HEAD_DIM), cd)
        out_specs = [
            pl.BlockSpec((1, N_HEADS, tb, HEAD_DIM), lambda i: (i // n, 0, i % n, 0)),
            pl.BlockSpec((tb, 4 * SLOT), row),
            pl.BlockSpec((1, 1, tb, 2 * SLOT), lambda i: (i // n, jnp.maximum(i % n - (n - 2), 0), 0, 0)),
            kt_spec, v_spec, kt_spec, v_spec,
            pl.BlockSpec((2, 1, tb, SLOT), lambda i: (0, i // n, i % n, 0)),
            pl.BlockSpec((tb, GATE_COLS), row)]
        out_shape = [
            jax.ShapeDtypeStruct((batch, N_HEADS, seq, HEAD_DIM), cd),
            jax.ShapeDtypeStruct((T, 4 * SLOT), f32),
            jax.ShapeDtypeStruct((batch, 2, tb, 2 * SLOT), f32),
            kt_shape, v_shape, kt_shape, v_shape,
            jax.ShapeDtypeStruct((2, batch, seq, SLOT), cd),
            jax.ShapeDtypeStruct((T, GATE_COLS), f32)]
    else:
        out_specs = [pl.BlockSpec((tb, HQ), row), pl.BlockSpec((tb, 4 * SLOT), row),
                     pl.BlockSpec((tb, 2 * SLOT), row), pl.BlockSpec((tb, GATE_COLS), row)]
        out_shape = [jax.ShapeDtypeStruct((T, HQ), f32), jax.ShapeDtypeStruct((T, 4 * SLOT), f32),
                     jax.ShapeDtypeStruct((T, 2 * SLOT), f32), jax.ShapeDtypeStruct((T, GATE_COLS), f32)]
    return pl.pallas_call(
        functools.partial(_nsa_proj_kernel, prompt=prompt),
        grid=(T // tb,),
        in_specs=in_specs, out_specs=out_specs, out_shape=out_shape,
        compiler_params=pltpu.CompilerParams(dimension_semantics=("arbitrary",),
                                             vmem_limit_bytes=48 * 1024 * 1024),
        name="nsa_proj",
    )(x, ln_g.reshape(1, D), w, q_gain.reshape(1, HEAD_DIM), k_gain)


def _compress_kernel(c_ref, w1_ref, pe_ref, w2_ref, kg_ref, o_ref):
    f32 = jnp.float32
    u = jnp.dot(c_ref[0, 0].astype(MXU_DTYPE), w1_ref[0], preferred_element_type=f32)
    pe = jnp.dot(pe_ref[0], w1_ref[0], preferred_element_type=f32)
    nc = u.shape[0]
    hid = u[:, :SLOT] + pltpu.roll(u[:, SLOT:], nc - 1, 0) + (pe[0:1, :SLOT] + pe[1:2, SLOT:])
    out = jnp.dot(_gelu_tanh(hid).astype(MXU_DTYPE), w2_ref[0], preferred_element_type=f32)
    is_key = pl.program_id(1) == 0
    for kv in range(KV_HEADS):
        lanes = slice(kv * HEAD_DIM, (kv + 1) * HEAD_DIM)
        v = out[:, lanes]
        vn = v * lax.rsqrt(jnp.mean(v * v, axis=-1, keepdims=True) + EPS) * kg_ref[...]
        o_ref[0, 0, :, lanes] = jnp.where(is_key, vn, v)


def _compress(c, pe, w1, w2, k_gain0):
    n, nc = c.shape[1:3]
    cd = MXU_DTYPE
    kc = CMP_STRIDE * SLOT
    eye = jnp.eye(KV_HEADS, dtype=jnp.float32)
    w1h = w1.reshape(2, 2, CMP_STRIDE, HEAD_DIM, CMP_HID)
    w1b = jnp.einsum('shldf,kj->slkdhjf', w1h, eye).reshape(2, kc, 2 * SLOT).astype(cd)
    w2b = jnp.einsum('sdf,kj->skdjf', w2, eye).reshape(2, SLOT, SLOT).astype(cd)
    peh = jnp.broadcast_to(pe.reshape(2, 2, CMP_STRIDE, 1, HEAD_DIM), (2, 2, CMP_STRIDE, KV_HEADS, HEAD_DIM))
    peh = jnp.pad(peh.reshape(2, 2, kc), ((0, 0), (0, 14), (0, 0))).astype(cd)
    return pl.pallas_call(
        _compress_kernel,
        grid=(n, 2),
        in_specs=[pl.BlockSpec((1, 1, nc, kc), lambda b, s: (s, b, 0, 0)),
                  pl.BlockSpec((1, kc, 2 * SLOT), lambda b, s: (s, 0, 0)),
                  pl.BlockSpec((1, 16, kc), lambda b, s: (s, 0, 0)),
                  pl.BlockSpec((1, SLOT, SLOT), lambda b, s: (s, 0, 0)),
                  pl.BlockSpec((1, HEAD_DIM), lambda b, s: (0, 0))],
        out_specs=pl.BlockSpec((1, 1, nc, SLOT), lambda b, s: (s, b, 0, 0)),
        out_shape=jax.ShapeDtypeStruct((2, n, nc, SLOT), jnp.float32),
        compiler_params=pltpu.CompilerParams(dimension_semantics=("arbitrary", "arbitrary"),
                                             vmem_limit_bytes=48 * 1024 * 1024),
        name="nsa_compress",
    )(c, w1b, peh, w2b, k_gain0.reshape(1, HEAD_DIM))


def _proj_residual_kernel(x_ref, a_ref, w_ref, o_ref):
    o_ref[...] = x_ref[...] + jnp.dot(a_ref[...].astype(MXU_DTYPE), w_ref[...],
                                      preferred_element_type=jnp.float32)


def _proj_residual(x, a, w):
    T, D = x.shape
    k = a.shape[1]
    tb = min(512, T)
    assert T % tb == 0
    return pl.pallas_call(
        _proj_residual_kernel,
        grid=(T // tb,),
        in_specs=[pl.BlockSpec((tb, D), lambda i: (i, 0)), pl.BlockSpec((tb, k), lambda i: (i, 0)),
                  pl.BlockSpec((k, D), lambda i: (0, 0))],
        out_specs=pl.BlockSpec((tb, D), lambda i: (i, 0)),
        out_shape=jax.ShapeDtypeStruct((T, D), jnp.float32),
        compiler_params=pltpu.CompilerParams(dimension_semantics=("arbitrary",)),
        name="proj_residual",
    )(x, a, w.astype(MXU_DTYPE))


MOE_ROWS = 256
ROUTE_COLS = LANES


def _moe_route_kernel(x_ref, g_ref, w_ref, b_ref, tri_ref, h_ref, meta_ref, wt_ref, cnt_ref, carry_sc):
    f32 = jnp.float32
    step = pl.program_id(0)

    @pl.when(step == 0)
    def _():
        carry_sc[...] = jnp.zeros(carry_sc.shape, f32)

    x = x_ref[...]
    h = x * lax.rsqrt(jnp.mean(x * x, axis=-1, keepdims=True) + EPS) * g_ref[...]
    h_ref[...] = h
    logits = jnp.dot(h.astype(MXU_DTYPE), w_ref[...], preferred_element_type=f32) + b_ref[...]
    tb = logits.shape[0]
    col = lax.broadcasted_iota(jnp.int32, (tb, ROUTE_COLS), 1)

    def first_max(vals):
        best = jnp.max(vals, axis=1, keepdims=True)
        return best, jnp.min(jnp.where(vals == best, col, ROUTE_COLS), axis=1, keepdims=True)

    lg = jnp.where(col < N_GROUPS, logits, -jnp.inf)
    g_best, grp = first_max(lg)
    g_w = 1.0 / jnp.sum(jnp.exp(lg - g_best), axis=1, keepdims=True)
    lo = N_GROUPS + EXPERTS_PER_GROUP * grp
    le = jnp.where((col >= lo) & (col < lo + EXPERTS_PER_GROUP), logits, -jnp.inf)
    v0, c0 = first_max(le)
    v1, c1 = first_max(jnp.where(col == c0, -jnp.inf, le))
    e1 = jnp.exp(v1 - v0)
    w0 = g_w / (1.0 + e1)
    w1 = g_w * e1 / (1.0 + e1)
    chosen = (col == c0) | (col == c1)
    before = jnp.dot(tri_ref[...], jnp.where(chosen, 1.0, 0.0).astype(MXU_DTYPE),
                     preferred_element_type=f32) + carry_sc[...]
    r0 = jnp.sum(jnp.where(col == c0, before, 0.0), axis=1, keepdims=True).astype(jnp.int32)
    r1 = jnp.sum(jnp.where(col == c1, before, 0.0), axis=1, keepdims=True).astype(jnp.int32)
    carry_sc[...] += jnp.sum(jnp.where(chosen, 1.0, 0.0), axis=0, keepdims=True)
    cnt_ref[...] = carry_sc[...]
    meta_ref[...] = jnp.where(col == 0, c0 - N_GROUPS, jnp.where(col == 1, c1 - N_GROUPS,
                              jnp.where(col == 2, r0, jnp.where(col == 3, r1, 0))))
    wt_ref[...] = jnp.where(col == 0, w0, jnp.where(col == 1, w1, 0.0))


def _moe_dispatch_kernel(dest_ref, h_ref, xbuf_in, xbuf_ref, sem):
    del xbuf_in
    tb = h_ref.shape[0]

    def row_copy(r, k):
        return pltpu.make_async_copy(h_ref.at[pl.ds(r, 1)], xbuf_ref.at[pl.ds(dest_ref[0, 0, 2 * r + k], 1)], sem)

    def start(r, c):
        row_copy(r, 0).start()
        row_copy(r, 1).start()
        return c

    def wait(r, c):
        row_copy(r, 0).wait()
        row_copy(r, 1).wait()
        return c

    lax.fori_loop(0, tb, start, 0)
    lax.fori_loop(0, tb, wait, 0)


def _moe_expert_kernel(blk_e_ref, nblk_ref, x_ref, wgu_ref, wdn_ref, y_ref, wgu_sc, wdn_sc):
    i = pl.program_id(0)
    f32 = jnp.float32

    @pl.when(i < nblk_ref[0])
    def _():
        changed = jnp.logical_or(i == 0, blk_e_ref[i] != blk_e_ref[jnp.maximum(i - 1, 0)])

        @pl.when(changed)
        def _():
            wgu_sc[...] = wgu_ref[0].astype(MXU_DTYPE)
            wdn_sc[...] = wdn_ref[0].astype(MXU_DTYPE)

        gu = jnp.dot(x_ref[...].astype(MXU_DTYPE), wgu_sc[...], preferred_element_type=f32)
        gate = gu[:, :D_EXPERT]
        act = gate * (1.0 / (1.0 + jnp.exp(-gate))) * gu[:, D_EXPERT:]
        y_ref[...] = jnp.dot(act.astype(MXU_DTYPE), wdn_sc[...], preferred_element_type=f32)

    @pl.when(i >= nblk_ref[0])
    def _():
        y_ref[...] = jnp.zeros(y_ref.shape, f32)


def _moe_combine_kernel(dest_ref, x_ref, wt_ref, ybuf_ref, o_ref, rows_sc, sem):
    tb = x_ref.shape[0]

    def row_copy(r, k):
        return pltpu.make_async_copy(ybuf_ref.at[pl.ds(dest_ref[0, 0, 2 * r + k], 1)],
                                     rows_sc.at[k, pl.ds(r, 1)], sem)

    def start(r, c):
        row_copy(r, 0).start()
        row_copy(r, 1).start()
        return c

    def wait(r, c):
        row_copy(r, 0).wait()
        row_copy(r, 1).wait()
        return c

    lax.fori_loop(0, tb, start, 0)
    lax.fori_loop(0, tb, wait, 0)
    wt = wt_ref[...]
    o_ref[...] = x_ref[...] + (wt[:, 0:1] * rows_sc[0] + wt[:, 1:2] * rows_sc[1])


def _hier_moe_residual(x, ln_g, w_grp, b_grp, w_exp, b_exp, w_gu, w_dn):
    T, D = x.shape
    f32 = jnp.float32
    tb = min(256, T)
    assert T % tb == 0
    nt = T // tb
    pad_cols = ROUTE_COLS - N_GROUPS - N_EXPERTS
    w_r = jnp.pad(jnp.concatenate([w_grp, w_exp], axis=1), ((0, 0), (0, pad_cols))).astype(MXU_DTYPE)
    b_r = jnp.pad(jnp.concatenate([b_grp, b_exp]), (0, pad_cols)).reshape(1, ROUTE_COLS)
    tri = jnp.asarray(np.tril(np.ones((tb, tb), np.float32), -1), dtype=MXU_DTYPE)
    row = lambda i: (i, 0)
    fixed = lambda i: (0, 0)
    h, meta, wt, cnt = pl.pallas_call(
        _moe_route_kernel,
        grid=(nt,),
        in_specs=[pl.BlockSpec((tb, D), row), pl.BlockSpec((1, D), fixed),
                  pl.BlockSpec((D, ROUTE_COLS), fixed), pl.BlockSpec((1, ROUTE_COLS), fixed),
                  pl.BlockSpec((tb, tb), fixed)],
        out_specs=[pl.BlockSpec((tb, D), row), pl.BlockSpec((tb, ROUTE_COLS), row),
                   pl.BlockSpec((tb, ROUTE_COLS), row), pl.BlockSpec((1, ROUTE_COLS), fixed)],
        out_shape=[jax.ShapeDtypeStruct((T, D), f32), jax.ShapeDtypeStruct((T, ROUTE_COLS), jnp.int32),
                   jax.ShapeDtypeStruct((T, ROUTE_COLS), f32), jax.ShapeDtypeStruct((1, ROUTE_COLS), f32)],
        scratch_shapes=[pltpu.VMEM((1, ROUTE_COLS), f32)],
        compiler_params=pltpu.CompilerParams(dimension_semantics=("arbitrary",)),
        name="moe_route",
    )(x, ln_g.reshape(1, D), w_r, b_r, tri)

    counts = cnt[0, N_GROUPS:N_GROUPS + N_EXPERTS].astype(jnp.int32)
    padded = (counts + MOE_ROWS - 1) // MOE_ROWS * MOE_ROWS
    pad_end = jnp.cumsum(padded)
    pad_start = pad_end - padded
    n_blocks = -(-(T * TOP_K) // MOE_ROWS) + N_EXPERTS
    n_slots = n_blocks * MOE_ROWS
    dest = pad_start[meta[:, 0:2]] + meta[:, 2:4]
    dest = dest.reshape(nt, 1, 2 * tb)
    starts = jnp.arange(n_blocks, dtype=jnp.int32) * MOE_ROWS
    used = (pad_end[-1] // MOE_ROWS).astype(jnp.int32).reshape(1)
    blk_e = jnp.minimum(jnp.searchsorted(pad_end, jnp.minimum(starts, pad_end[-1] - 1), side='right'),
                        N_EXPERTS - 1).astype(jnp.int32)

    dest_spec = pl.BlockSpec((1, 1, 2 * tb), lambda i: (i, 0, 0), memory_space=pltpu.SMEM)
    xbuf = pl.pallas_call(
        _moe_dispatch_kernel,
        grid=(nt,),
        in_specs=[dest_spec, pl.BlockSpec((tb, D), row), pl.BlockSpec(memory_space=pl.ANY)],
        out_specs=pl.BlockSpec(memory_space=pl.ANY),
        out_shape=jax.ShapeDtypeStruct((n_slots, D), f32),
        scratch_shapes=[pltpu.SemaphoreType.DMA(())],
        input_output_aliases={2: 0},
        compiler_params=pltpu.CompilerParams(dimension_semantics=("arbitrary",)),
        name="moe_dispatch",
    )(dest, h, jnp.zeros((n_slots, D), f32))

    ybuf = pl.pallas_call(
        _moe_expert_kernel,
        grid_spec=pltpu.PrefetchScalarGridSpec(
            num_scalar_prefetch=2,
            grid=(n_blocks,),
            in_specs=[pl.BlockSpec((MOE_ROWS, D), lambda i, be, nb: (i, 0)),
                      pl.BlockSpec((1, D, 2 * D_EXPERT), lambda i, be, nb: (be[i], 0, 0)),
                      pl.BlockSpec((1, D_EXPERT, D), lambda i, be, nb: (be[i], 0, 0))],
            out_specs=pl.BlockSpec((MOE_ROWS, D), lambda i, be, nb: (i, 0)),
            scratch_shapes=[pltpu.VMEM((D, 2 * D_EXPERT), MXU_DTYPE), pltpu.VMEM((D_EXPERT, D), MXU_DTYPE)]),
        out_shape=jax.ShapeDtypeStruct((n_slots, D), f32),
        compiler_params=pltpu.CompilerParams(dimension_semantics=("arbitrary",),
                                             vmem_limit_bytes=48 * 1024 * 1024),
        name="moe_experts",
    )(blk_e, used, xbuf, w_gu, w_dn)

    return pl.pallas_call(
        _moe_combine_kernel,
        grid=(nt,),
        in_specs=[dest_spec, pl.BlockSpec((tb, D), row), pl.BlockSpec((tb, ROUTE_COLS), row),
                  pl.BlockSpec(memory_space=pl.ANY)],
        out_specs=pl.BlockSpec((tb, D), row),
        out_shape=jax.ShapeDtypeStruct((T, D), f32),
        scratch_shapes=[pltpu.VMEM((TOP_K, tb, D), f32), pltpu.SemaphoreType.DMA(())],
        compiler_params=pltpu.CompilerParams(dimension_semantics=("arbitrary",)),
        name="moe_combine",
    )(dest, x, wt, ybuf)


def _gelu_tanh(x):
    return 0.5 * x * (1.0 + jnp.tanh(math.sqrt(2.0 / math.pi) * (x + 0.044715 * (x * x * x))))


def _gmlp_kernel(x_ref, g_ref, win_ref, bin_ref, lng_ref, lnb_ref, ws_ref, bs_ref, wout_ref,
                 o_ref, v_ref, *, single_position):
    f32 = jnp.float32
    x = x_ref[...]
    rows = x.shape[0]
    h = x * lax.rsqrt(jnp.mean(x * x, axis=-1, keepdims=True) + EPS) * g_ref[...]
    z = _gelu_tanh(jnp.dot(h.astype(MXU_DTYPE), win_ref[...], preferred_element_type=f32) + bin_ref[...])
    u = z[:, :GM_WIDTH]
    v = z[:, GM_WIDTH:]
    mu = jnp.mean(v, axis=-1, keepdims=True)
    var = jnp.mean(jnp.square(v - mu), axis=-1, keepdims=True)
    v = (v - mu) * lax.rsqrt(var + EPS) * lng_ref[...] + lnb_ref[...]
    v_ref[0] = v
    vb = v.astype(MXU_DTYPE)
    if single_position:
        s = ws_ref[...].astype(f32) * vb.astype(f32) + bs_ref[...]
    else:
        parts = []
        for c in range(rows // CHUNK):
            vc = vb[c * CHUNK:(c + 1) * CHUNK]
            parts.append(jnp.concatenate(
                [jnp.dot(ws_ref[g], vc[:, g * GM_GROUP_DIM:(g + 1) * GM_GROUP_DIM], preferred_element_type=f32)
                 + bs_ref[g] for g in range(GM_GROUPS)], axis=1))
        s = jnp.concatenate(parts, axis=0)
    y = jnp.dot((u * s).astype(MXU_DTYPE), wout_ref[...], preferred_element_type=f32)
    o_ref[...] = x + y


def _gmlp_residual(x, ln_g, w_in, b_in, ln2_g, ln2_b, w_s, b_s, w_out, seq):
    T, D = x.shape
    f32 = jnp.float32
    cd = MXU_DTYPE
    single = seq == 1
    tb = T if single else 2 * CHUNK
    assert T % tb == 0 and (single or seq % tb == 0)
    steps_per_seq = 1 if single else seq // tb
    if single:
        ws = jnp.repeat(w_s[:, 0, 0], GM_GROUP_DIM).reshape(1, GM_WIDTH).astype(cd)
        bs = jnp.repeat(b_s[:, 0], GM_GROUP_DIM).reshape(1, GM_WIDTH)
        ws_spec = pl.BlockSpec((1, GM_WIDTH), lambda i: (0, 0))
        bs_spec = pl.BlockSpec((1, GM_WIDTH), lambda i: (0, 0))
    else:
        causal = np.tril(np.ones((CHUNK, CHUNK), bool))
        ws = jnp.where(causal[None], w_s, 0).astype(cd)
        bs = jnp.broadcast_to(b_s[:, :, None], (GM_GROUPS, CHUNK, GM_GROUP_DIM))
        ws_spec = pl.BlockSpec((GM_GROUPS, CHUNK, CHUNK), lambda i: (0, 0, 0))
        bs_spec = pl.BlockSpec((GM_GROUPS, CHUNK, GM_GROUP_DIM), lambda i: (0, 0, 0))
    fixed = lambda i: (0, 0)
    row = lambda i: (i, 0)
    return pl.pallas_call(
        functools.partial(_gmlp_kernel, single_position=single),
        grid=(T // tb,),
        in_specs=[pl.BlockSpec((tb, D), row), pl.BlockSpec((1, D), fixed),
                  pl.BlockSpec((D, 2 * GM_WIDTH), fixed), pl.BlockSpec((1, 2 * GM_WIDTH), fixed),
                  pl.BlockSpec((1, GM_WIDTH), fixed), pl.BlockSpec((1, GM_WIDTH), fixed),
                  ws_spec, bs_spec, pl.BlockSpec((GM_WIDTH, D), fixed)],
        out_specs=[pl.BlockSpec((tb, D), row), pl.BlockSpec((1, tb, GM_WIDTH), lambda i: (i // steps_per_seq, 0, 0))],
        out_shape=[jax.ShapeDtypeStruct((T, D), f32),
                   jax.ShapeDtypeStruct((T // (tb * steps_per_seq), tb, GM_WIDTH), f32)],
        compiler_params=pltpu.CompilerParams(dimension_semantics=("arbitrary",),
                                             vmem_limit_bytes=56 * 1024 * 1024),
        name="gmlp",
    )(x, ln_g.reshape(1, D), w_in.astype(cd), b_in.reshape(1, -1), ln2_g.reshape(1, -1), ln2_b.reshape(1, -1),
      ws, bs, w_out.astype(cd))


def rmsnorm(x, g):
    xf = x.astype(jnp.float32)
    y = xf * lax.rsqrt(jnp.mean(xf * xf, axis=-1, keepdims=True) + EPS)
    return (y * g.astype(jnp.float32)).astype(x.dtype)


def layernorm(x, g, b):
    xf = x.astype(jnp.float32)
    mu = jnp.mean(xf, axis=-1, keepdims=True)
    var = jnp.mean(jnp.square(xf - mu), axis=-1, keepdims=True)
    return ((xf - mu) * lax.rsqrt(var + EPS) * g.astype(jnp.float32) + b.astype(jnp.float32)).astype(x.dtype)


def rel_bucket(n):
    n = jnp.maximum(n, 0)
    max_exact = N_BUCKETS // 2
    nf = jnp.maximum(n, 1).astype(jnp.float32)
    large = max_exact + (jnp.log(nf / max_exact) / math.log(MAX_DISTANCE / max_exact)
                         * (N_BUCKETS - max_exact)).astype(jnp.int32)
    large = jnp.minimum(large, N_BUCKETS - 1)
    return jnp.where(n < max_exact, n, large)


def head_bias(rel, rel_bias):
    q, k = rel.shape
    b = rel_bias[rel_bucket(rel)]
    return b.reshape(q, k, KV_HEADS, Q_PER_KV).transpose(2, 3, 0, 1)


def masked_softmax(logits, mask):
    z = jnp.where(mask, logits.astype(jnp.float32), -1e30)
    return jax.nn.softmax(z, axis=-1) * mask


def nsa_project(xn, w_in, q_gain, k_gain):
    B, T, _ = xn.shape
    hq = N_HEADS * HEAD_DIM
    hkv = 6 * KV_HEADS * HEAD_DIM
    h = xn @ w_in
    q = rmsnorm(h[..., :hq].reshape(B, T, N_HEADS, HEAD_DIM), q_gain)
    kv = h[..., hq:hq + hkv].reshape(B, T, 6, KV_HEADS, HEAD_DIM)
    gates = jax.nn.sigmoid(h[..., hq + hkv:].astype(jnp.float32)).reshape(B, T, N_HEADS, 3)
    k_sel = rmsnorm(kv[:, :, 2], k_gain[1])
    k_win = rmsnorm(kv[:, :, 4], k_gain[2])
    return q, kv[:, :, 0], kv[:, :, 1], k_sel, kv[:, :, 3], k_win, kv[:, :, 5], gates


def compress(rows, pe, w1, w2):
    B, T = rows.shape[:2]
    n_ch = T // CMP_STRIDE
    c = rows[:, :n_ch * CMP_STRIDE].reshape(B, n_ch, CMP_STRIDE, KV_HEADS, HEAD_DIM)
    blk = jnp.concatenate([c[:, :-1], c[:, 1:]], axis=2) + pe[None, None, :, None, :]
    h = jax.nn.gelu(jnp.einsum('bnlkd,ldf->bnkf', blk, w1.reshape(CMP_LEN, HEAD_DIM, CMP_HID)))
    return jnp.einsum('bnkf,fd->bnkd', h, w2)


def compressed_kv(k_raw, v_raw, k_gain0, pe, w1, w2):
    kc = rmsnorm(compress(k_raw, pe[0], w1[0], w2[0]), k_gain0)
    vc = compress(v_raw, pe[1], w1[1], w2[1])
    end = jnp.arange(kc.shape[1]) * CMP_STRIDE + (CMP_LEN - 1)
    return kc, vc, end


def nsa_core(q, q_pos, kc, vc, kc_end, ks, vs, kw, vw, kw_pos, gates, rel_bias):
    B, Q = q.shape[:2]
    nc = kc.shape[1]
    scale = HEAD_DIM ** -0.5
    qh = q.reshape(B, Q, KV_HEADS, Q_PER_KV, HEAD_DIM)
    lc = jnp.einsum('bqkgd,bnkd->bkgqn', qh, kc) * scale + head_bias(q_pos[:, None] - kc_end[None, :], rel_bias)
    pc = masked_softmax(lc, kc_end[None, :] <= q_pos[:, None])
    oc = jnp.einsum('bkgqn,bnkd->bqkgd', pc.astype(vc.dtype), vc)
    nb = ks.shape[1] // SEL_BLOCK
    imp = jnp.pad(pc.sum(axis=2), ((0, 0), (0, 0), (0, 0), (0, nb * CMP_PER_SEL - nc)))
    imp = imp.reshape(B, KV_HEADS, Q, nb, CMP_PER_SEL).sum(-1)
    blk = jnp.arange(nb)
    d = (q_pos // SEL_BLOCK)[:, None] - blk[None, :]
    forced = (blk[None, :] == 0) | ((d >= 0) & (d < N_LOCAL))
    score = jnp.where(forced, jnp.inf, jnp.where(d >= 0, imp, -jnp.inf))
    n_sel = min(N_SEL, nb)
    _, idx = lax.top_k(score, n_sel)
    bi = jnp.arange(B)[:, None, None, None]
    ki = jnp.arange(KV_HEADS)[None, :, None, None]
    ksb = ks.reshape(B, nb, SEL_BLOCK, KV_HEADS, HEAD_DIM).transpose(0, 3, 1, 2, 4)
    vsb = vs.reshape(B, nb, SEL_BLOCK, KV_HEADS, HEAD_DIM).transpose(0, 3, 1, 2, 4)
    kg = ksb[bi, ki, idx].reshape(B, KV_HEADS, Q, n_sel * SEL_BLOCK, HEAD_DIM)
    vg = vsb[bi, ki, idx].reshape(B, KV_HEADS, Q, n_sel * SEL_BLOCK, HEAD_DIM)
    kpos = (idx[..., None] * SEL_BLOCK + jnp.arange(SEL_BLOCK)).reshape(B, KV_HEADS, Q, n_sel * SEL_BLOCK)
    rel_s = q_pos[None, None, :, None] - kpos
    tb = rel_bias.reshape(N_BUCKETS, KV_HEADS, Q_PER_KV)
    bs = tb[rel_bucket(rel_s), ki].transpose(0, 1, 4, 2, 3)
    ls = jnp.einsum('bqkgd,bkqjd->bkgqj', qh, kg) * scale + bs
    ps = masked_softmax(ls, (rel_s >= 0)[:, :, None])
    os_ = jnp.einsum('bkgqj,bkqjd->bqkgd', ps.astype(vg.dtype), vg)
    rel_w = q_pos[:, None] - kw_pos[None, :]
    mw = (rel_w >= 0) & (rel_w <= WINDOW) & (kw_pos[None, :] >= 0)
    lw = jnp.einsum('bqkgd,bwkd->bkgqw', qh, kw) * scale + head_bias(rel_w, rel_bias)
    pw = masked_softmax(lw, mw)
    ow = jnp.einsum('bkgqw,bwkd->bqkgd', pw.astype(vw.dtype), vw)
    g = gates.reshape(B, Q, KV_HEADS, Q_PER_KV, 3).astype(oc.dtype)
    o = g[..., 0:1] * oc + g[..., 1:2] * os_ + g[..., 2:3] * ow
    return o.reshape(B, Q, N_HEADS * HEAD_DIM)


def nsa_prompt(xn, w_in, w_out, q_gain, k_gain, pe, w1, w2, rel_bias):
    B, S, _ = xn.shape
    q, kcr, vcr, ks, vs, kw, vw, gates = nsa_project(xn, w_in, q_gain, k_gain)
    kc, vc, kc_end = compressed_kv(kcr, vcr, k_gain[0], pe, w1, w2)
    o = _nsa_attention(q, kc, vc, ks, vs, kw, vw, gates, rel_bias)
    w_keep = min(WINDOW, S)
    new_rows = jnp.stack([kcr, vcr, ks, vs], axis=2)
    new_win = jnp.stack([kw[:, S - w_keep:], vw[:, S - w_keep:]], axis=2)
    y = _mm(o.reshape(B * S, -1), w_out).reshape(B, S, -1)
    return y, new_rows, new_win


def nsa_sample(xn, past_rows, win_buf, w_in, w_out, q_gain, k_gain, pe, w1, w2, rel_bias):
    B, L, _ = xn.shape
    P = past_rows.shape[1]
    q, kcr, vcr, ks, vs, kw, vw, gates = nsa_project(xn, w_in, q_gain, k_gain)
    new_rows = jnp.stack([kcr, vcr, ks, vs], axis=2)
    rows = jnp.concatenate([past_rows, new_rows], axis=1)
    kc, vc, kc_end = compressed_kv(rows[:, :, 0], rows[:, :, 1], k_gain[0], pe, w1, w2)
    T = P + L
    tp = -(-T // SEL_BLOCK) * SEL_BLOCK
    pad = ((0, 0), (0, tp - T), (0, 0), (0, 0))
    ks_all = jnp.pad(rows[:, :, 2], pad)
    vs_all = jnp.pad(rows[:, :, 3], pad)
    win_all = jnp.concatenate([win_buf, jnp.stack([kw, vw], axis=2)], axis=1)
    wb = win_buf.shape[1]
    kw_pos = P - wb + jnp.arange(wb + L)
    q_pos = P + jnp.arange(L)
    o = nsa_core(q, q_pos, kc, vc, kc_end, ks_all, vs_all, win_all[:, :, 0], win_all[:, :, 1],
                 kw_pos, gates, rel_bias)
    w_keep = min(WINDOW, wb + L)
    return o @ w_out, new_rows, win_all[:, wb + L - w_keep:]


def gmlp_mixer(xn, w_in, b_in, ln_g, ln_b, w_s, b_s, w_out):
    B, L, _ = xn.shape
    z = jax.nn.gelu(xn @ w_in + b_in)
    u = z[..., :GM_WIDTH]
    v = layernorm(z[..., GM_WIDTH:], ln_g, ln_b)
    lp = -(-L // CHUNK) * CHUNK
    vp = jnp.pad(v, ((0, 0), (0, lp - L), (0, 0))).reshape(B, lp // CHUNK, CHUNK, GM_GROUPS, GM_GROUP_DIM)
    causal = jnp.tril(jnp.ones((CHUNK, CHUNK), dtype=bool))
    ws = jnp.where(causal[None], w_s, 0)
    s = jnp.einsum('gts,bcsgd->bctgd', ws, vp) + b_s.T[None, None, :, :, None]
    s = s.reshape(B, lp, GM_WIDTH)[:, :L]
    start = ((L - 1) // CHUNK) * CHUNK
    return (u * s) @ w_out, v[:, start:]


def hier_moe(x, w_grp, b_grp, w_exp, b_exp, w_gu, w_dn):
    T, D = x.shape
    pg = jax.nn.softmax((x @ w_grp + b_grp).astype(jnp.float32), axis=-1)
    grp = jnp.argmax(pg, axis=-1)
    g_w = jnp.take_along_axis(pg, grp[:, None], axis=-1)
    le = (x @ w_exp + b_exp).astype(jnp.float32).reshape(T, N_GROUPS, EXPERTS_PER_GROUP)
    le = jnp.take_along_axis(le, grp[:, None, None], axis=1)[:, 0]
    top_v, top_i = lax.top_k(le, TOP_K)
    wts = (jax.nn.softmax(top_v, axis=-1) * g_w).reshape(-1)
    eid = (grp[:, None] * EXPERTS_PER_GROUP + top_i).reshape(-1)
    n = T * TOP_K
    order = jnp.argsort(eid)
    e_s = eid[order]
    tok_s = order // TOP_K
    w_s = wts[order]
    counts = jnp.bincount(eid, length=N_EXPERTS)
    padded = (counts + MOE_BLOCK - 1) // MOE_BLOCK * MOE_BLOCK
    pad_end = jnp.cumsum(padded)
    pad_start = pad_end - padded
    start = jnp.cumsum(counts) - counts
    dest = pad_start[e_s] + jnp.arange(n) - start[e_s]
    n_blocks = -(-n // MOE_BLOCK) + N_EXPERTS
    xbuf = jnp.zeros((n_blocks * MOE_BLOCK, D), x.dtype).at[dest].set(x[tok_s])
    blk_e = jnp.minimum(jnp.searchsorted(pad_end, jnp.arange(n_blocks) * MOE_BLOCK, side='right'), N_EXPERTS - 1)

    def expert(args):
        xb, e = args
        gu = xb @ w_gu[e]
        return (jax.nn.silu(gu[:, :D_EXPERT]) * gu[:, D_EXPERT:]) @ w_dn[e]

    ybuf = lax.map(expert, (xbuf.reshape(n_blocks, MOE_BLOCK, D), blk_e)).reshape(-1, D)
    return jax.ops.segment_sum(ybuf[dest] * w_s[:, None].astype(x.dtype), tok_s, num_segments=T)


def kernel(x_prompt, x_sample, cache_nsa_kv, state_win_kv, page_table, rel_bias, ln_mix, ln_ffn,
           nsa_w_in, nsa_w_out, nsa_q_gain, nsa_k_gain, cmp_pe, cmp_w1, cmp_w2,
           gm_w_in, gm_b_in, gm_ln_g, gm_ln_b, gm_w_s, gm_b_s, gm_w_out,
           moe_w_grp, moe_b_grp, moe_w_exp, moe_b_exp, moe_w_gu, moe_w_dn):
    xp = x_prompt
    xs = x_sample
    kv_p, kv_s, win_p, win_s, gv_p, gv_s = [], [], [], [], [], []
    for i in range(DEPTH):
        a = i // N_MIXERS
        if i % N_MIXERS == 0:
            nsa = (ln_mix[i], nsa_w_in[a], nsa_w_out[a], nsa_q_gain[a], nsa_k_gain[a],
                   cmp_pe[a], cmp_w1[a], cmp_w2[a], rel_bias)
            xp, rp, wp = _nsa_prompt_layer(xp, *nsa)
            past = _page_gather(cache_nsa_kv, page_table, a)
            xs, rs, ws = _nsa_sample_layer(xs, past, state_win_kv[a], *nsa)
            kv_p.append(rp)
            kv_s.append(rs)
            win_p.append(wp)
            win_s.append(ws)
        else:
            gm = (ln_mix[i], gm_w_in[a], gm_b_in[a], gm_ln_g[a], gm_ln_b[a], gm_w_s[a], gm_b_s[a], gm_w_out[a])
            bp, sp = xp.shape[:2]
            bs_, ss = xs.shape[:2]
            xp2, vp = _gmlp_residual(xp.reshape(-1, D_MODEL), *gm, seq=sp)
            xs2, vs = _gmlp_residual(xs.reshape(-1, D_MODEL), *gm, seq=ss)
            xp = xp2.reshape(xp.shape)
            xs = xs2.reshape(xs.shape)
            start = ((sp - 1) // CHUNK) * CHUNK
            gv_p.append(vp[:, vp.shape[1] - (sp - start):])
            gv_s.append(vs.reshape(bs_, ss, GM_WIDTH))
        moe = (ln_ffn[i], moe_w_grp[i], moe_b_grp[i], moe_w_exp[i], moe_b_exp[i], moe_w_gu[i], moe_w_dn[i])
        xp = _hier_moe_residual(xp.reshape(-1, D_MODEL), *moe).reshape(xp.shape)
        xs = _hier_moe_residual(xs.reshape(-1, D_MODEL), *moe).reshape(xs.shape)
    new_kv_prompt = jnp.stack(kv_p, axis=2)
    new_kv_sample = jnp.stack(kv_s, axis=2)
    new_win_prompt = jnp.stack(win_p, axis=0)
    new_win_sample = jnp.stack(win_s, axis=0)
    new_gm_v_prompt = jnp.stack(gv_p, axis=0)
    new_gm_v_sample = jnp.stack(gv_s, axis=0)
    return (xp, xs, new_kv_prompt, new_kv_sample, new_win_prompt, new_win_sample, new_gm_v_prompt, new_gm_v_sample)
```

```python
import functools
import math

import jax
import jax.numpy as jnp
import numpy as np
from jax import lax
from jax.experimental import pallas as pl
from jax.experimental.pallas import tpu as pltpu

D_MODEL = 1024
PAGE_SIZE = 128
DEPTH = 2
N_MIXERS = 2
N_HEADS = 16
HEAD_DIM = 64
KV_HEADS = 4
Q_PER_KV = N_HEADS // KV_HEADS
CMP_LEN = 32
CMP_STRIDE = 16
CMP_HID = 64
SEL_BLOCK = 64
CMP_PER_SEL = SEL_BLOCK // CMP_STRIDE
N_SEL = 16
N_LOCAL = 2
WINDOW = 512
Q_BLOCK = 128
N_BUCKETS = 32
MAX_DISTANCE = 128
CHUNK = 128
GM_WIDTH = 2048
GM_GROUPS = 8
GM_GROUP_DIM = GM_WIDTH // GM_GROUPS
N_GROUPS = 4
EXPERTS_PER_GROUP = 8
N_EXPERTS = N_GROUPS * EXPERTS_PER_GROUP
TOP_K = 2
D_EXPERT = 512
MOE_BLOCK = 128
EPS = 1e-6


def _mm_kernel(x_ref, w_ref, o_ref):
    o_ref[...] = jnp.dot(x_ref[...].astype(jnp.bfloat16), w_ref[...].astype(jnp.bfloat16),
                         preferred_element_type=jnp.float32)


def _mm(x, w, tm=512, tn=512):
    m, k = x.shape
    n = w.shape[1]
    tm = min(tm, m)
    tn = min(tn, n)
    return pl.pallas_call(
        _mm_kernel,
        grid=(m // tm, n // tn),
        in_specs=[pl.BlockSpec((tm, k), lambda i, j: (i, 0)),
                  pl.BlockSpec((k, tn), lambda i, j: (0, j))],
        out_specs=pl.BlockSpec((tm, tn), lambda i, j: (i, j)),
        out_shape=jax.ShapeDtypeStruct((m, n), jnp.float32),
        name="mm",
    )(x, w)


NEG = -1e30
MXU_DTYPE = jnp.bfloat16
TQ = 256
TK = 256
LANES = 128
BIAS_SPAN = 128


def _bucket_of_distance():
    d = np.arange(BIAS_SPAN)
    max_exact = N_BUCKETS // 2
    nf = np.maximum(d, 1).astype(np.float32)
    large = max_exact + (np.log(nf / np.float32(max_exact)) / np.float32(math.log(MAX_DISTANCE / max_exact))
                         * np.float32(N_BUCKETS - max_exact)).astype(np.int32)
    large = np.minimum(large, N_BUCKETS - 1)
    return np.where(d < max_exact, d, large).astype(np.int32)


def _attn_bias_tiles(rel_bias, seq):
    fd = rel_bias[_bucket_of_distance()].T
    far = fd[:, BIAS_SPAN - 1]
    fd = fd - far[:, None]

    def by_distance(d):
        return jnp.where(d >= 0, jnp.take(fd, np.clip(d, 0, BIAS_SPAN - 1), axis=1), NEG)

    def toeplitz(first_distance):
        c = np.arange(2 * TK)
        g = by_distance(first_distance - np.where(c < TK, c, c - 2 * TK))
        flat = jnp.tile(g, (1, TQ))[:, :TQ * (2 * TK - 1)]
        return flat.reshape(-1, TQ, 2 * TK - 1)[:, :, :TK]

    b0 = toeplitz(0)
    b1 = toeplitz(TQ)
    i = np.arange(TQ)[:, None]
    j = np.arange(TK)[None, :]
    w2 = np.where(2 * TQ + i - j <= WINDOW, 0.0, NEG).astype(np.float32)
    nc = seq // CMP_STRIDE
    span = -(-(BIAS_SPAN + TQ) // CMP_STRIDE)
    r = np.arange(nc - span, nc + span)[:, None]
    dc = np.arange(TQ)[None, :] - CMP_STRIDE * (r - nc) - (CMP_LEN - 1)
    n_heads = fd.shape[0]
    pat = jnp.concatenate([jnp.zeros((n_heads, nc - span, TQ), jnp.float32), by_distance(dc),
                           jnp.full((n_heads, nc - span, TQ), NEG, jnp.float32)], axis=1)
    return b0, b1, jnp.asarray(w2), pat


def _nsa_attn_kernel(q_ref, kc_ref, vct_ref, kst_ref, vs_ref, kwt_ref, vw_ref, oh_ref, g_ref,
                     b0_ref, b1_ref, w2_ref, pat_ref, o_ref, m_sc, l_sc, acc_sc, imp_sc, *, nc, nb):
    f32 = jnp.float32
    qb = pl.program_id(2)

    r0 = pl.multiple_of(nc - (TQ // CMP_STRIDE) * qb, TQ // CMP_STRIDE)
    t_row = qb * TQ + lax.broadcasted_iota(jnp.int32, (1, TQ), 1)
    has_cmp = t_row >= CMP_LEN - 1
    for h in range(Q_PER_KV):
        st = lax.dot_general(kc_ref[0, 0], q_ref[0, h], (((1,), (1,)), ((), ())),
                             preferred_element_type=f32)
        st = st + pat_ref[h, pl.ds(r0, nc), :]
        e = jnp.exp(st - jnp.max(st, axis=0, keepdims=True))
        inv = jnp.where(has_cmp, 1.0 / jnp.sum(e, axis=0, keepdims=True), 0.0)
        pt = e * inv
        for c in range(TQ // LANES):
            part = pt[:, c * LANES:(c + 1) * LANES]
            if h == 0:
                imp_sc[c] = part
            else:
                imp_sc[c] += part
        oct_h = jnp.dot(vct_ref[0, 0], pt.astype(MXU_DTYPE), preferred_element_type=f32)
        o_ref[0, :, h * HEAD_DIM:(h + 1) * HEAD_DIM] = g_ref[0, h][:, 0:1] * oct_h.T

    imp = jnp.concatenate(
        [sum(imp_sc[c, pl.ds(r, nb, stride=CMP_PER_SEL), :] for r in range(CMP_PER_SEL))
         for c in range(TQ // LANES)], axis=1)
    blk = lax.broadcasted_iota(jnp.int32, (nb, TQ), 0)
    t_blk = (qb * TQ + lax.broadcasted_iota(jnp.int32, (nb, TQ), 1)) // SEL_BLOCK
    dd = t_blk - blk
    forced = (blk == 0) | ((dd >= 0) & (dd < N_LOCAL))
    score = jnp.where(forced, jnp.inf, jnp.where(dd >= 0, imp, -jnp.inf))
    pen_t = jnp.full((nb, TQ), NEG, f32)
    for _ in range(min(N_SEL, nb)):
        best = jnp.max(score, axis=0, keepdims=True)
        first = jnp.min(jnp.where(score == best, blk, nb), axis=0, keepdims=True)
        hit = blk == first
        pen_t = jnp.where(hit, 0.0, pen_t)
        score = jnp.where(hit, -jnp.inf, score)
    pen = pen_t.T.astype(MXU_DTYPE)

    def reset():
        m_sc[...] = jnp.full(m_sc.shape, NEG, f32)
        l_sc[...] = jnp.zeros(l_sc.shape, f32)
        acc_sc[...] = jnp.zeros(acc_sc.shape, f32)

    def attend(h, kt, v, extra):
        s = jnp.dot(q_ref[0, h], kt, preferred_element_type=f32) + extra
        m_prev = m_sc[h]
        m_new = jnp.maximum(m_prev, jnp.max(s, axis=1, keepdims=True))
        p = jnp.exp(s - jnp.concatenate([m_new] * (TK // LANES), axis=1))
        alpha = jnp.exp(m_prev - m_new)
        l_sc[h] = alpha * l_sc[h] + jnp.sum(p, axis=1, keepdims=True)
        acc_sc[h] = alpha[:, :HEAD_DIM] * acc_sc[h] + jnp.dot(p.astype(MXU_DTYPE), v,
                                                             preferred_element_type=f32)
        m_sc[h] = m_new

    def emit(gate_col):
        for h in range(Q_PER_KV):
            w = g_ref[0, h][:, gate_col:gate_col + 1] / l_sc[h][:, :HEAD_DIM]
            o_ref[0, :, h * HEAD_DIM:(h + 1) * HEAD_DIM] += w * acc_sc[h]

    def selected_chunk(c, bias_ref):
        mterm = jnp.dot(pen, oh_ref[c], preferred_element_type=f32)
        for h in range(Q_PER_KV):
            extra = mterm if bias_ref is None else mterm + bias_ref[h]
            attend(h, kst_ref[0, 0, c], vs_ref[0, 0, c], extra)

    reset()
    selected_chunk(qb, b0_ref)

    @pl.when(qb >= 1)
    def _():
        selected_chunk(qb - 1, b1_ref)

    def far_body(c, carry):
        selected_chunk(c, None)
        return carry

    lax.fori_loop(0, jnp.maximum(qb - 1, 0), far_body, 0)
    emit(1)

    reset()
    for h in range(Q_PER_KV):
        attend(h, kwt_ref[0, 0, qb], vw_ref[0, 0, qb], b0_ref[h])

    @pl.when(qb >= 1)
    def _():
        for h in range(Q_PER_KV):
            attend(h, kwt_ref[0, 0, qb - 1], vw_ref[0, 0, qb - 1], b1_ref[h])

    @pl.when(qb >= 2)
    def _():
        for h in range(Q_PER_KV):
            attend(h, kwt_ref[0, 0, qb - 2], vw_ref[0, 0, qb - 2], w2_ref[...])

    emit(2)


def _nsa_attention(qs, kcp, vct, kst, vs, kwt, vw, gt, rel_bias):
    B, _, S, _ = qs.shape
    assert S % TQ == 0 and TQ == TK and WINDOW == 2 * TQ and S // SEL_BLOCK >= N_SEL
    nc = S // CMP_STRIDE
    nb = S // SEL_BLOCK
    nch = S // TK
    onehot = (np.arange(nb)[None, :, None] ==
              (np.arange(S) // SEL_BLOCK).reshape(nch, 1, TK)).astype(np.float32)
    onehot = jnp.asarray(onehot, dtype=MXU_DTYPE)
    b0, b1, w2, pat = _attn_bias_tiles(rel_bias, S)
    G = Q_PER_KV
    kv_chunks = pl.BlockSpec((1, 1, nch, HEAD_DIM, TK), lambda b, k, i: (b, k, 0, 0, 0))
    v_chunks = pl.BlockSpec((1, 1, nch, TK, HEAD_DIM), lambda b, k, i: (b, k, 0, 0, 0))
    return pl.pallas_call(
        functools.partial(_nsa_attn_kernel, nc=nc, nb=nb),
        grid=(B, KV_HEADS, S // TQ),
        in_specs=[
            pl.BlockSpec((1, G, TQ, HEAD_DIM), lambda b, k, i: (b, k, i, 0)),
            pl.BlockSpec((1, 1, nc, HEAD_DIM), lambda b, k, i: (b, k, 0, 0)),
            pl.BlockSpec((1, 1, HEAD_DIM, nc), lambda b, k, i: (b, k, 0, 0)),
            kv_chunks, v_chunks, kv_chunks, v_chunks,
            pl.BlockSpec((nch, nb, TK), lambda b, k, i: (0, 0, 0)),
            pl.BlockSpec((1, G, TQ, 3), lambda b, k, i: (b, k, i, 0)),
            pl.BlockSpec((G, TQ, TK), lambda b, k, i: (k, 0, 0)),
            pl.BlockSpec((G, TQ, TK), lambda b, k, i: (k, 0, 0)),
            pl.BlockSpec((TQ, TK), lambda b, k, i: (0, 0)),
            pl.BlockSpec((G, 2 * nc, TQ), lambda b, k, i: (k, 0, 0)),
        ],
        out_specs=pl.BlockSpec((1, TQ, G * HEAD_DIM), lambda b, k, i: (b, i, k)),
        out_shape=jax.ShapeDtypeStruct((B, S, N_HEADS * HEAD_DIM), jnp.float32),
        scratch_shapes=[
            pltpu.VMEM((G, TQ, LANES), jnp.float32),
            pltpu.VMEM((G, TQ, LANES), jnp.float32),
            pltpu.VMEM((G, TQ, HEAD_DIM), jnp.float32),
            pltpu.VMEM((TQ // LANES, nc, LANES), jnp.float32),
        ],
        compiler_params=pltpu.CompilerParams(
            dimension_semantics=("arbitrary", "arbitrary", "arbitrary"),
            vmem_limit_bytes=56 * 1024 * 1024),
        name="nsa_attn",
    )(qs, kcp, vct, kst, vs, kwt, vw, onehot, gt, b0, b1, w2, pat)


def _nsa_prompt_layer(x, ln_g, w_in, w_out, q_gain, k_gain, pe, w1, w2, rel_bias):
    B, S, D = x.shape
    x2 = x.reshape(B * S, D)
    qs, rows, win, kst, vs, kwt, vw, c, gates = _nsa_project(x2, ln_g, w_in, q_gain, k_gain, B, S)
    nc = S // CMP_STRIDE
    cmp = _compress(c.reshape(2, B, nc, CMP_STRIDE * SLOT), pe, w1, w2, k_gain[0])
    cmp = cmp.reshape(2, B, nc, KV_HEADS, HEAD_DIM).astype(MXU_DTYPE)
    gt = gates[:, :3 * N_HEADS].reshape(B, S, N_HEADS, 3).transpose(0, 2, 1, 3)
    o = _nsa_attention(qs, cmp[0].transpose(0, 2, 1, 3), cmp[1].transpose(0, 2, 3, 1), kst, vs, kwt, vw, gt,
                       rel_bias)
    y = _proj_residual(x2, o.reshape(B * S, HQ), w_out)
    return (y.reshape(B, S, D), rows.reshape(B, S, 4, KV_HEADS, HEAD_DIM),
            win.reshape(B, WINDOW, 2, KV_HEADS, HEAD_DIM))


PAGES_PER_STEP = 8


def _page_gather_kernel(pt_ref, *refs):
    del pt_ref
    pages = refs[:PAGES_PER_STEP]
    c_ref, kvt_ref, rows_sc = refs[PAGES_PER_STEP:]
    for p, pg in enumerate(pages):
        pos = slice(p * PAGE_SIZE, (p + 1) * PAGE_SIZE)
        for slot in range(2):
            for pair in range(KV_HEADS // 2):
                tile = pg[0, 0, slot, 2 * pair:2 * pair + 2].reshape(LANES, PAGE_SIZE)
                rows_sc[slot * 2 + pair, pos, :] = tile.T
        for slot in range(2):
            kvt_ref[slot, 0, :, :, pos] = pg[0, 0, 2 + slot].astype(MXU_DTYPE)
    n_chunks = PAGES_PER_STEP * PAGE_SIZE // CMP_STRIDE
    for l in range(CMP_STRIDE):
        for t in range(4):
            lo = l * SLOT + (t % 2) * LANES
            c_ref[t // 2, 0, :, lo:lo + LANES] = rows_sc[t, pl.ds(l, n_chunks, stride=CMP_STRIDE), :].astype(MXU_DTYPE)


def _page_gather(cache, page_table, layer):
    nseq, n_pages = page_table.shape
    assert n_pages % PAGES_PER_STEP == 0 and 2 * LANES == SLOT
    P = n_pages * PAGE_SIZE
    cache_t = cache.transpose(0, 2, 3, 4, 5, 1)
    page_specs = [
        pl.BlockSpec((1, 1, 4, KV_HEADS, HEAD_DIM, PAGE_SIZE),
                     functools.partial(lambda r, n, j, pt: (pt[n, PAGES_PER_STEP * j + r], layer, 0, 0, 0, 0), r))
        for r in range(PAGES_PER_STEP)]
    rows_per_step = PAGES_PER_STEP * PAGE_SIZE
    return pl.pallas_call(
        _page_gather_kernel,
        grid_spec=pltpu.PrefetchScalarGridSpec(
            num_scalar_prefetch=1,
            grid=(nseq, n_pages // PAGES_PER_STEP),
            in_specs=page_specs,
            out_specs=[pl.BlockSpec((2, 1, rows_per_step // CMP_STRIDE, CMP_STRIDE * SLOT),
                                    lambda n, j, pt: (0, n, j, 0)),
                       pl.BlockSpec((2, 1, KV_HEADS, HEAD_DIM, rows_per_step), lambda n, j, pt: (0, n, 0, 0, j))],
            scratch_shapes=[pltpu.VMEM((4, rows_per_step, LANES), jnp.float32)]),
        out_shape=[jax.ShapeDtypeStruct((2, nseq, P // CMP_STRIDE, CMP_STRIDE * SLOT), MXU_DTYPE),
                   jax.ShapeDtypeStruct((2, nseq, KV_HEADS, HEAD_DIM, P), MXU_DTYPE)],
        compiler_params=pltpu.CompilerParams(dimension_semantics=("arbitrary", "arbitrary")),
        name="page_gather",
    )(page_table, *([cache_t] * PAGES_PER_STEP))


def _nsa_decode_kernel(q_ref, kc_ref, vc_ref, kst_ref, vst_ref, win_ref, rows_ref, wnew_ref, g_ref,
                       oh_ref, bc_ref, bs_ref, bw_ref, bn_ref, o_ref, *, nb):
    f32 = jnp.float32
    cd = MXU_DTYPE
    nt = (((1,), (1,)), ((), ()))
    rows = rows_ref[0]
    wnew = wnew_ref[0]

    def softmax_with_new(s, s_new):
        m = jnp.maximum(jnp.max(s, axis=1, keepdims=True), s_new)
        e = jnp.exp(s - m)
        e_new = jnp.exp(s_new - m)
        return e, e_new, 1.0 / (jnp.sum(e, axis=1, keepdims=True) + e_new)

    def bf(x):
        return x.astype(cd).astype(f32)

    for k in range(KV_HEADS):
        hs = slice(k * Q_PER_KV, (k + 1) * Q_PER_KV)
        lanes = slice(k * HEAD_DIM, (k + 1) * HEAD_DIM)
        q = q_ref[0, hs, :]
        qf = q.astype(f32)
        gates = g_ref[0, hs, :]

        def new_key(row):
            return jnp.sum(qf * bf(row), axis=1, keepdims=True) + bn_ref[hs, :]

        sc = lax.dot_general(q, kc_ref[0, k], nt, preferred_element_type=f32) + bc_ref[hs, :]
        ec = jnp.exp(sc - jnp.max(sc, axis=1, keepdims=True))
        pc = ec / jnp.sum(ec, axis=1, keepdims=True)
        out = gates[:, 0:1] * jnp.dot(pc.astype(cd), vc_ref[0, k], preferred_element_type=f32)

        imp = sum(pc[:, r * nb:(r + 1) * nb] for r in range(CMP_PER_SEL))
        imp = jnp.sum(imp, axis=0, keepdims=True)
        blk = lax.broadcasted_iota(jnp.int32, imp.shape, 1)
        forced = (blk == 0) | (blk == nb - 1)
        score = jnp.where(forced, jnp.inf, imp)
        pen = jnp.full(imp.shape, NEG, f32)
        for _ in range(N_SEL - 1):
            best = jnp.max(score, axis=1, keepdims=True)
            first = jnp.min(jnp.where(score == best, blk, nb), axis=1, keepdims=True)
            hit = blk == first
            pen = jnp.where(hit, 0.0, pen)
            score = jnp.where(hit, -jnp.inf, score)
        pen8 = jnp.broadcast_to(pen, (8, nb)).astype(cd)
        key_pen = jnp.dot(pen8, oh_ref[...], preferred_element_type=f32)[0:1]

        ss = jnp.dot(q, kst_ref[0, 0, k], preferred_element_type=f32) + key_pen + bs_ref[hs, :]
        e, e_new, inv = softmax_with_new(ss, new_key(rows[:, 2 * SLOT:3 * SLOT][:, lanes]))
        acc = (lax.dot_general(e.astype(cd), vst_ref[0, 0, k], nt, preferred_element_type=f32)
               + bf(e_new) * bf(rows[:, 3 * SLOT:4 * SLOT][:, lanes]))
        out = out + gates[:, 1:2] * inv * acc

        sw = lax.dot_general(q, win_ref[0, 0, k], nt, preferred_element_type=f32) + bw_ref[hs, :]
        e, e_new, inv = softmax_with_new(sw, new_key(wnew[:, :SLOT][:, lanes]))
        acc = (jnp.dot(e.astype(cd), win_ref[0, 1, k], preferred_element_type=f32)
               + bf(e_new) * bf(wnew[:, SLOT:][:, lanes]))
        o_ref[0, hs, :] = out + gates[:, 2:3] * inv * acc


def _nsa_decode_layer(x, cmp_rows, sel_kvt, win_buf, ln_g, w_in, w_out, q_gain, k_gain, pe, w1, w2, rel_bias):
    N, L, D = x.shape
    P = sel_kvt.shape[-1]
    wb = win_buf.shape[1]
    assert L == 1 and P % SEL_BLOCK == 0 and wb == WINDOW and P >= BIAS_SPAN and P // SEL_BLOCK >= N_SEL - 1
    f32 = jnp.float32
    cd = MXU_DTYPE
    nc = P // CMP_STRIDE
    nb = P // SEL_BLOCK
    x2 = x.reshape(N, D)
    q, rows, wnew, gates = _nsa_project(x2, ln_g, w_in, q_gain, k_gain, N, 1)
    cmp = _compress(cmp_rows, pe, w1, w2, k_gain[0])
    cmp = cmp.reshape(2, N, nb, CMP_PER_SEL, KV_HEADS, HEAD_DIM).transpose(0, 1, 4, 3, 2, 5)
    cmp = cmp.reshape(2, N, KV_HEADS, nc, HEAD_DIM).astype(cd)
    qs = (q.reshape(N, N_HEADS, HEAD_DIM) * HEAD_DIM ** -0.5).astype(cd)
    win = win_buf.transpose(0, 2, 3, 1, 4).astype(cd)
    onehot = jnp.asarray(np.arange(nb)[:, None] == (np.arange(P) // SEL_BLOCK)[None, :], dtype=cd)
    fd = rel_bias[_bucket_of_distance()].T
    fd = fd - fd[:, BIAS_SPAN - 1:]
    cidx = (np.arange(nc) % nb) * CMP_PER_SEL + np.arange(nc) // nb
    dc = P - (cidx * CMP_STRIDE + CMP_LEN - 1)
    b_cmp = jnp.where(dc >= 0, jnp.take(fd, np.clip(dc, 0, BIAS_SPAN - 1), axis=1), NEG)
    b_sel = jnp.take(fd, np.clip(P - np.arange(P), 0, BIAS_SPAN - 1), axis=1)
    b_win = jnp.take(fd, np.clip(wb - np.arange(wb), 0, BIAS_SPAN - 1), axis=1)
    b_new = fd[:, 0:1]
    seq3 = lambda n: (n, 0, 0)
    seq4 = lambda n: (n, 0, 0, 0)
    fixed = lambda n: (0, 0)
    o = pl.pallas_call(
        functools.partial(_nsa_decode_kernel, nb=nb),
        grid=(N,),
        in_specs=[pl.BlockSpec((1, N_HEADS, HEAD_DIM), seq3),
                  pl.BlockSpec((1, KV_HEADS, nc, HEAD_DIM), seq4),
                  pl.BlockSpec((1, KV_HEADS, nc, HEAD_DIM), seq4),
                  pl.BlockSpec((1, 1, KV_HEADS, HEAD_DIM, P), lambda n: (0, n, 0, 0, 0)),
                  pl.BlockSpec((1, 1, KV_HEADS, HEAD_DIM, P), lambda n: (1, n, 0, 0, 0)),
                  pl.BlockSpec((1, 2, KV_HEADS, wb, HEAD_DIM), lambda n: (n, 0, 0, 0, 0)),
                  pl.BlockSpec((1, 1, 4 * SLOT), seq3),
                  pl.BlockSpec((1, 1, 2 * SLOT), seq3),
                  pl.BlockSpec((1, N_HEADS, 3), seq3),
                  pl.BlockSpec((nb, P), fixed), pl.BlockSpec((N_HEADS, nc), fixed),
                  pl.BlockSpec((N_HEADS, P), fixed), pl.BlockSpec((N_HEADS, wb), fixed),
                  pl.BlockSpec((N_HEADS, 1), fixed)],
        out_specs=pl.BlockSpec((1, N_HEADS, HEAD_DIM), seq3),
        out_shape=jax.ShapeDtypeStruct((N, N_HEADS, HEAD_DIM), f32),
        compiler_params=pltpu.CompilerParams(dimension_semantics=("arbitrary",),
                                             vmem_limit_bytes=48 * 1024 * 1024),
        name="nsa_decode_attn",
    )(qs, cmp[0], cmp[1], sel_kvt, sel_kvt, win, rows.reshape(N, 1, 4 * SLOT), wnew.reshape(N, 1, 2 * SLOT),
      gates[:, :3 * N_HEADS].reshape(N, N_HEADS, 3), onehot, b_cmp, b_sel, b_win, b_new)
    y = _proj_residual(x2, o.reshape(N, HQ), w_out)
    new_win = jnp.concatenate([win_buf, wnew.reshape(N, 1, 2, KV_HEADS, HEAD_DIM)], axis=1)[:, 1:]
    return y.reshape(N, 1, D), rows.reshape(N, 1, 4, KV_HEADS, HEAD_DIM), new_win


def _split3(x):
    hi = x.astype(MXU_DTYPE)
    r1 = x - hi.astype(jnp.float32)
    mid = r1.astype(MXU_DTYPE)
    return hi, mid, (r1 - mid.astype(jnp.float32)).astype(MXU_DTYPE)


def _nsa_sample_kernel(q_ref, kc_ref, vc_ref, ks_ref, vs_ref, win_ref, rows_ref, wnew_ref, g_ref,
                       oh_ref, grp_ref, own_ref, bc_ref, bs_ref, bw_ref, bn_ref, o_ref, *, nb):
    f32 = jnp.float32
    cd = MXU_DTYPE
    nt = (((1,), (1,)), ((), ()))
    q = q_ref[0]
    qf = q.astype(f32)

    def softmax_with_new(s, s_new):
        m = jnp.maximum(jnp.max(s, axis=1, keepdims=True), s_new)
        e = jnp.exp(s - m)
        e_new = jnp.exp(s_new - m)
        return e, e_new, 1.0 / (jnp.sum(e, axis=1, keepdims=True) + e_new)

    def new_key(row):
        return jnp.sum(qf * row.astype(cd).astype(f32), axis=1, keepdims=True) + bn_ref[...]

    sc = lax.dot_general(q, kc_ref[0], nt, preferred_element_type=f32) + bc_ref[...]
    ec = jnp.exp(sc - jnp.max(sc, axis=1, keepdims=True))
    pc = ec / jnp.sum(ec, axis=1, keepdims=True)
    out = g_ref[0][:, 0:1] * jnp.dot(pc.astype(cd), vc_ref[0], preferred_element_type=f32)

    imp = sum(pc[:, r * nb:(r + 1) * nb] for r in range(CMP_PER_SEL))
    imp = sum(jnp.dot(grp_ref[...], part, preferred_element_type=f32) for part in _split3(imp))
    blk = lax.broadcasted_iota(jnp.int32, imp.shape, 1)
    forced = (blk == 0) | (blk == nb - 1)
    score = jnp.where(forced, jnp.inf, imp)
    pen = jnp.full(imp.shape, NEG, f32)
    for _ in range(N_SEL - 1):
        best = jnp.max(score, axis=1, keepdims=True)
        first = jnp.min(jnp.where(score == best, blk, nb), axis=1, keepdims=True)
        hit = blk == first
        pen = jnp.where(hit, 0.0, pen)
        score = jnp.where(hit, -jnp.inf, score)

    ss = (lax.dot_general(q, ks_ref[0, 0].astype(cd), nt, preferred_element_type=f32)
          + jnp.dot(pen.astype(cd), oh_ref[...], preferred_element_type=f32) + bs_ref[...])
    rows = rows_ref[0]
    e, e_new, inv = softmax_with_new(ss, new_key(rows[:, 2 * SLOT:3 * SLOT]))
    acc = (jnp.dot(e.astype(cd), vs_ref[0, 0].astype(cd), preferred_element_type=f32)
           + e_new.astype(cd).astype(f32) * rows[:, 3 * SLOT:4 * SLOT].astype(cd).astype(f32))
    out = out + g_ref[0][:, 1:2] * inv * acc

    win = win_ref[0]
    wnew = wnew_ref[0]
    sw = lax.dot_general(q, win[:, :SLOT].astype(cd), nt, preferred_element_type=f32) + bw_ref[...]
    e, e_new, inv = softmax_with_new(sw, new_key(wnew[:, :SLOT]))
    acc = (jnp.dot(e.astype(cd), win[:, SLOT:].astype(cd), preferred_element_type=f32)
           + e_new.astype(cd).astype(f32) * wnew[:, SLOT:].astype(cd).astype(f32))
    out = out + g_ref[0][:, 2:3] * inv * acc
    o_ref[0] = out * own_ref[...]


def _nsa_sample_layer(x, past, win_buf, ln_g, w_in, w_out, q_gain, k_gain, pe, w1, w2, rel_bias):
    N, L, D = x.shape
    P = past.shape[2]
    wb = win_buf.shape[1]
    assert L == 1 and P % SEL_BLOCK == 0 and wb == WINDOW and P >= BIAS_SPAN and P // SEL_BLOCK >= N_SEL - 1
    f32 = jnp.float32
    cd = MXU_DTYPE
    nc = P // CMP_STRIDE
    nb = P // SEL_BLOCK
    x2 = x.reshape(N, D)
    q, rows, wnew, gates = _nsa_project(x2, ln_g, w_in, q_gain, k_gain, N, 1)
    cmp = _compress(past.reshape(4, N, nc, CMP_STRIDE * SLOT), pe, w1, w2, k_gain[0])
    cmp = cmp.reshape(2, N, nb, CMP_PER_SEL, SLOT).transpose(0, 1, 3, 2, 4).reshape(2, N, nc, SLOT).astype(cd)
    head_kv = np.arange(N_HEADS) // Q_PER_KV
    own = (head_kv[:, None] == (np.arange(SLOT) // HEAD_DIM)[None, :]).astype(np.float32)
    qbd = (jnp.tile(q.reshape(N, N_HEADS, HEAD_DIM), (1, 1, KV_HEADS)) * own * HEAD_DIM ** -0.5).astype(cd)
    grp = jnp.asarray(head_kv[:, None] == head_kv[None, :], dtype=cd)
    onehot = jnp.asarray(np.arange(nb)[:, None] == (np.arange(P) // SEL_BLOCK)[None, :], dtype=cd)
    fd = rel_bias[_bucket_of_distance()].T
    fd = fd - fd[:, BIAS_SPAN - 1:]
    cidx = (np.arange(nc) % nb) * CMP_PER_SEL + np.arange(nc) // nb
    dc = P - (cidx * CMP_STRIDE + CMP_LEN - 1)
    b_cmp = jnp.where(dc >= 0, jnp.take(fd, np.clip(dc, 0, BIAS_SPAN - 1), axis=1), NEG)
    b_sel = jnp.take(fd, np.clip(P - np.arange(P), 0, BIAS_SPAN - 1), axis=1)
    b_win = jnp.take(fd, np.clip(wb - np.arange(wb), 0, BIAS_SPAN - 1), axis=1)
    b_new = fd[:, 0:1]
    seq3 = lambda n: (n, 0, 0)
    fixed = lambda n: (0, 0)
    o = pl.pallas_call(
        functools.partial(_nsa_sample_kernel, nb=nb),
        grid=(N,),
        in_specs=[pl.BlockSpec((1, N_HEADS, SLOT), seq3),
                  pl.BlockSpec((1, nc, SLOT), seq3),
                  pl.BlockSpec((1, nc, SLOT), seq3),
                  pl.BlockSpec((1, 1, P, SLOT), lambda n: (2, n, 0, 0)),
                  pl.BlockSpec((1, 1, P, SLOT), lambda n: (3, n, 0, 0)),
                  pl.BlockSpec((1, wb, 2 * SLOT), seq3),
                  pl.BlockSpec((1, 1, 4 * SLOT), seq3),
                  pl.BlockSpec((1, 1, 2 * SLOT), seq3),
                  pl.BlockSpec((1, N_HEADS, 3), seq3),
                  pl.BlockSpec((nb, P), fixed), pl.BlockSpec((N_HEADS, N_HEADS), fixed),
                  pl.BlockSpec((N_HEADS, SLOT), fixed), pl.BlockSpec((N_HEADS, nc), fixed),
                  pl.BlockSpec((N_HEADS, P), fixed), pl.BlockSpec((N_HEADS, wb), fixed),
                  pl.BlockSpec((N_HEADS, 1), fixed)],
        out_specs=pl.BlockSpec((1, N_HEADS, SLOT), seq3),
        out_shape=jax.ShapeDtypeStruct((N, N_HEADS, SLOT), f32),
        compiler_params=pltpu.CompilerParams(dimension_semantics=("arbitrary",),
                                             vmem_limit_bytes=56 * 1024 * 1024),
        name="nsa_sample_attn",
    )(qbd, cmp[0], cmp[1], past, past, win_buf.reshape(N, wb, 2 * SLOT), rows.reshape(N, 1, 4 * SLOT),
      wnew.reshape(N, 1, 2 * SLOT), gates[:, :3 * N_HEADS].reshape(N, N_HEADS, 3),
      onehot, grp, jnp.asarray(own), b_cmp, b_sel, b_win, b_new)
    o = o.reshape(N, KV_HEADS, Q_PER_KV, KV_HEADS, HEAD_DIM)
    o = jnp.stack([o[:, k, :, k] for k in range(KV_HEADS)], axis=1).reshape(N, HQ)
    y = _proj_residual(x2, o, w_out)
    new_win = jnp.concatenate([win_buf, wnew.reshape(N, 1, 2, KV_HEADS, HEAD_DIM)], axis=1)[:, 1:]
    return y.reshape(N, 1, D), rows.reshape(N, 1, 4, KV_HEADS, HEAD_DIM), new_win


HQ = N_HEADS * HEAD_DIM
SLOT = KV_HEADS * HEAD_DIM
NSA_COLS = HQ + 6 * SLOT + 3 * N_HEADS
NSA_COLS_PAD = -(-NSA_COLS // LANES) * LANES
GATE_COLS = NSA_COLS_PAD - HQ - 6 * SLOT


def _nsa_proj_kernel(x_ref, g_ref, w_ref, qg_ref, kg_ref, *outs, prompt):
    f32 = jnp.float32
    x = x_ref[...]
    xn = x * lax.rsqrt(jnp.mean(x * x, axis=-1, keepdims=True) + EPS) * g_ref[...]
    h = jnp.dot(xn.astype(MXU_DTYPE), w_ref[...], preferred_element_type=f32)

    def head_norm(v, gain):
        return v * lax.rsqrt(jnp.mean(v * v, axis=-1, keepdims=True) + EPS) * gain

    def slot(s, kv):
        lo = HQ + s * SLOT + kv * HEAD_DIM
        return h[:, lo:lo + HEAD_DIM]

    gates = 1.0 / (1.0 + jnp.exp(-h[:, HQ + 6 * SLOT:]))
    if prompt:
        q_ref, rows_ref, win_ref, kst_ref, vs_ref, kwt_ref, vw_ref, c_ref, gate_ref = outs
        for hh in range(N_HEADS):
            qn = head_norm(h[:, hh * HEAD_DIM:(hh + 1) * HEAD_DIM], qg_ref[...])
            q_ref[0, hh] = (qn * HEAD_DIM ** -0.5).astype(MXU_DTYPE)
        c_ref[0, 0] = h[:, HQ:HQ + SLOT].astype(MXU_DTYPE)
        c_ref[1, 0] = h[:, HQ + SLOT:HQ + 2 * SLOT].astype(MXU_DTYPE)
    else:
        q_ref, rows_ref, win_ref, gate_ref = outs
        for hh in range(N_HEADS):
            q_ref[:, hh * HEAD_DIM:(hh + 1) * HEAD_DIM] = head_norm(h[:, hh * HEAD_DIM:(hh + 1) * HEAD_DIM],
                                                                    qg_ref[...])
    rows_ref[:, 0:2 * SLOT] = h[:, HQ:HQ + 2 * SLOT]
    rows_ref[:, 3 * SLOT:4 * SLOT] = h[:, HQ + 3 * SLOT:HQ + 4 * SLOT]
    for kv in range(KV_HEADS):
        lanes = slice(kv * HEAD_DIM, (kv + 1) * HEAD_DIM)
        ksn = head_norm(slot(2, kv), kg_ref[1:2, :])
        kwn = head_norm(slot(4, kv), kg_ref[2:3, :])
        rows_ref[:, 2 * SLOT + kv * HEAD_DIM:2 * SLOT + (kv + 1) * HEAD_DIM] = ksn
        if prompt:
            win_ref[0, 0, :, lanes] = kwn
            kst_ref[0, kv, 0] = ksn.T.astype(MXU_DTYPE)
            kwt_ref[0, kv, 0] = kwn.T.astype(MXU_DTYPE)
            vs_ref[0, kv, 0] = slot(3, kv).astype(MXU_DTYPE)
            vw_ref[0, kv, 0] = slot(5, kv).astype(MXU_DTYPE)
        else:
            win_ref[:, lanes] = kwn
    if prompt:
        win_ref[0, 0, :, SLOT:2 * SLOT] = h[:, HQ + 5 * SLOT:HQ + 6 * SLOT]
    else:
        win_ref[:, SLOT:2 * SLOT] = h[:, HQ + 5 * SLOT:HQ + 6 * SLOT]
    gate_ref[...] = gates


def _nsa_project(x, ln_g, w_in, q_gain, k_gain, batch, seq):
    T, D = x.shape
    f32 = jnp.float32
    cd = MXU_DTYPE
    prompt = seq > 1
    tb = TK if prompt else T
    w = jnp.pad(w_in, ((0, 0), (0, NSA_COLS_PAD - NSA_COLS))).astype(cd)
    fixed = lambda i: (0, 0)
    row = lambda i: (i, 0)
    in_specs = [pl.BlockSpec((tb, D), row), pl.BlockSpec((1, D), fixed), pl.BlockSpec((D, NSA_COLS_PAD), fixed),
                pl.BlockSpec((1, HEAD_DIM), fixed), pl.BlockSpec((3, HEAD_DIM), fixed)]
    if prompt:
        assert seq % tb == 0 and WINDOW == 2 * tb
        n = seq // tb
        kt_spec = pl.BlockSpec((1, KV_HEADS, 1, HEAD_DIM, tb), lambda i: (i // n, 0, i % n, 0, 0))
        v_spec = pl.BlockSpec((1, KV_HEADS, 1, tb, HEAD_DIM), lambda i: (i // n, 0, i % n, 0, 0))
        kt_shape = jax.ShapeDtypeStruct((batch, KV_HEADS, n, HEAD_DIM, tb), cd)
        v_shape = jax.ShapeDtypeStruct((batch, KV_HEADS, n, tb, HEAD_DIM), cd)
        out_specs = [
            pl.BlockSpec((1, N_HEADS, tb, HEAD_DIM), lambda i: (i // n, 0, i % n, 0)),
            pl.BlockSpec((tb, 4 * SLOT), row),
            pl.BlockSpec((1, 1, tb, 2 * SLOT), lambda i: (i // n, jnp.maximum(i % n - (n - 2), 0), 0, 0)),
            kt_spec, v_spec, kt_spec, v_spec,
            pl.BlockSpec((2, 1, tb, SLOT), lambda i: (0, i // n, i % n, 0)),
            pl.BlockSpec((tb, GATE_COLS), row)]
        out_shape = [
            jax.ShapeDtypeStruct((batch, N_HEADS, seq, HEAD_DIM), cd),
            jax.ShapeDtypeStruct((T, 4 * SLOT), f32),
            jax.ShapeDtypeStruct((batch, 2, tb, 2 * SLOT), f32),
            kt_shape, v_shape, kt_shape, v_shape,
            jax.ShapeDtypeStruct((2, batch, seq, SLOT), cd),
            jax.ShapeDtypeStruct((T, GATE_COLS), f32)]
    else:
        out_specs = [pl.BlockSpec((tb, HQ), row), pl.BlockSpec((tb, 4 * SLOT), row),
                     pl.BlockSpec((tb, 2 * SLOT), row), pl.BlockSpec((tb, GATE_COLS), row)]
        out_shape = [jax.ShapeDtypeStruct((T, HQ), f32), jax.ShapeDtypeStruct((T, 4 * SLOT), f32),
                     jax.ShapeDtypeStruct((T, 2 * SLOT), f32), jax.ShapeDtypeStruct((T, GATE_COLS), f32)]
    return pl.pallas_call(
        functools.partial(_nsa_proj_kernel, prompt=prompt),
        grid=(T // tb,),
        in_specs=in_specs, out_specs=out_specs, out_shape=out_shape,
        compiler_params=pltpu.CompilerParams(dimension_semantics=("arbitrary",),
                                             vmem_limit_bytes=48 * 1024 * 1024),
        name="nsa_proj",
    )(x, ln_g.reshape(1, D), w, q_gain.reshape(1, HEAD_DIM), k_gain)


def _compress_kernel(c_ref, w1_ref, pe_ref, w2_ref, kg_ref, o_ref):
    f32 = jnp.float32
    u = jnp.dot(c_ref[0, 0].astype(MXU_DTYPE), w1_ref[0], preferred_element_type=f32)
    pe = jnp.dot(pe_ref[0], w1_ref[0], preferred_element_type=f32)
    nc = u.shape[0]
    hid = u[:, :SLOT] + pltpu.roll(u[:, SLOT:], nc - 1, 0) + (pe[0:1, :SLOT] + pe[1:2, SLOT:])
    out = jnp.dot(_gelu_tanh(hid).astype(MXU_DTYPE), w2_ref[0], preferred_element_type=f32)
    is_key = pl.program_id(1) == 0
    for kv in range(KV_HEADS):
        lanes = slice(kv * HEAD_DIM, (kv + 1) * HEAD_DIM)
        v = out[:, lanes]
        vn = v * lax.rsqrt(jnp.mean(v * v, axis=-1, keepdims=True) + EPS) * kg_ref[...]
        o_ref[0, 0, :, lanes] = jnp.where(is_key, vn, v)


def _compress(c, pe, w1, w2, k_gain0):
    n, nc = c.shape[1:3]
    cd = MXU_DTYPE
    kc = CMP_STRIDE * SLOT
    eye = jnp.eye(KV_HEADS, dtype=jnp.float32)
    w1h = w1.reshape(2, 2, CMP_STRIDE, HEAD_DIM, CMP_HID)
    w1b = jnp.einsum('shldf,kj->slkdhjf', w1h, eye).reshape(2, kc, 2 * SLOT).astype(cd)
    w2b = jnp.einsum('sdf,kj->skdjf', w2, eye).reshape(2, SLOT, SLOT).astype(cd)
    peh = jnp.broadcast_to(pe.reshape(2, 2, CMP_STRIDE, 1, HEAD_DIM), (2, 2, CMP_STRIDE, KV_HEADS, HEAD_DIM))
    peh = jnp.pad(peh.reshape(2, 2, kc), ((0, 0), (0, 14), (0, 0))).astype(cd)
    return pl.pallas_call(
        _compress_kernel,
        grid=(n, 2),
        in_specs=[pl.BlockSpec((1, 1, nc, kc), lambda b, s: (s, b, 0, 0)),
                  pl.BlockSpec((1, kc, 2 * SLOT), lambda b, s: (s, 0, 0)),
                  pl.BlockSpec((1, 16, kc), lambda b, s: (s, 0, 0)),
                  pl.BlockSpec((1, SLOT, SLOT), lambda b, s: (s, 0, 0)),
                  pl.BlockSpec((1, HEAD_DIM), lambda b, s: (0, 0))],
        out_specs=pl.BlockSpec((1, 1, nc, SLOT), lambda b, s: (s, b, 0, 0)),
        out_shape=jax.ShapeDtypeStruct((2, n, nc, SLOT), jnp.float32),
        compiler_params=pltpu.CompilerParams(dimension_semantics=("arbitrary", "arbitrary"),
                                             vmem_limit_bytes=48 * 1024 * 1024),
        name="nsa_compress",
    )(c, w1b, peh, w2b, k_gain0.reshape(1, HEAD_DIM))


def _proj_residual_kernel(x_ref, a_ref, w_ref, o_ref):
    o_ref[...] = x_ref[...] + jnp.dot(a_ref[...].astype(MXU_DTYPE), w_ref[...],
                                      preferred_element_type=jnp.float32)


def _proj_residual(x, a, w):
    T, D = x.shape
    k = a.shape[1]
    tb = min(512, T)
    assert T % tb == 0
    return pl.pallas_call(
        _proj_residual_kernel,
        grid=(T // tb,),
        in_specs=[pl.BlockSpec((tb, D), lambda i: (i, 0)), pl.BlockSpec((tb, k), lambda i: (i, 0)),
                  pl.BlockSpec((k, D), lambda i: (0, 0))],
        out_specs=pl.BlockSpec((tb, D), lambda i: (i, 0)),
        out_shape=jax.ShapeDtypeStruct((T, D), jnp.float32),
        compiler_params=pltpu.CompilerParams(dimension_semantics=("arbitrary",)),
        name="proj_residual",
    )(x, a, w.astype(MXU_DTYPE))


MOE_ROWS = 256
ROUTE_COLS = LANES


def _moe_route_kernel(x_ref, g_ref, w_ref, b_ref, tri_ref, h_ref, meta_ref, wt_ref, cnt_ref, carry_sc):
    f32 = jnp.float32
    step = pl.program_id(0)

    @pl.when(step == 0)
    def _():
        carry_sc[...] = jnp.zeros(carry_sc.shape, f32)

    x = x_ref[...]
    h = x * lax.rsqrt(jnp.mean(x * x, axis=-1, keepdims=True) + EPS) * g_ref[...]
    h_ref[...] = h
    logits = jnp.dot(h.astype(MXU_DTYPE), w_ref[...], preferred_element_type=f32) + b_ref[...]
    tb = logits.shape[0]
    col = lax.broadcasted_iota(jnp.int32, (tb, ROUTE_COLS), 1)

    def first_max(vals):
        best = jnp.max(vals, axis=1, keepdims=True)
        return best, jnp.min(jnp.where(vals == best, col, ROUTE_COLS), axis=1, keepdims=True)

    lg = jnp.where(col < N_GROUPS, logits, -jnp.inf)
    g_best, grp = first_max(lg)
    g_w = 1.0 / jnp.sum(jnp.exp(lg - g_best), axis=1, keepdims=True)
    lo = N_GROUPS + EXPERTS_PER_GROUP * grp
    le = jnp.where((col >= lo) & (col < lo + EXPERTS_PER_GROUP), logits, -jnp.inf)
    v0, c0 = first_max(le)
    v1, c1 = first_max(jnp.where(col == c0, -jnp.inf, le))
    e1 = jnp.exp(v1 - v0)
    w0 = g_w / (1.0 + e1)
    w1 = g_w * e1 / (1.0 + e1)
    chosen = (col == c0) | (col == c1)
    before = jnp.dot(tri_ref[...], jnp.where(chosen, 1.0, 0.0).astype(MXU_DTYPE),
                     preferred_element_type=f32) + carry_sc[...]
    r0 = jnp.sum(jnp.where(col == c0, before, 0.0), axis=1, keepdims=True).astype(jnp.int32)
    r1 = jnp.sum(jnp.where(col == c1, before, 0.0), axis=1, keepdims=True).astype(jnp.int32)
    carry_sc[...] += jnp.sum(jnp.where(chosen, 1.0, 0.0), axis=0, keepdims=True)
    cnt_ref[...] = carry_sc[...]
    meta_ref[...] = jnp.where(col == 0, c0 - N_GROUPS, jnp.where(col == 1, c1 - N_GROUPS,
                              jnp.where(col == 2, r0, jnp.where(col == 3, r1, 0))))
    wt_ref[...] = jnp.where(col == 0, w0, jnp.where(col == 1, w1, 0.0))


def _moe_dispatch_kernel(dest_ref, h_ref, xbuf_in, xbuf_ref, sem):
    del xbuf_in
    tb = h_ref.shape[0]

    def row_copy(r, k):
        return pltpu.make_async_copy(h_ref.at[pl.ds(r, 1)], xbuf_ref.at[pl.ds(dest_ref[0, 0, 2 * r + k], 1)], sem)

    def start(r, c):
        row_copy(r, 0).start()
        row_copy(r, 1).start()
        return c

    def wait(r, c):
        row_copy(r, 0).wait()
        row_copy(r, 1).wait()
        return c

    lax.fori_loop(0, tb, start, 0)
    lax.fori_loop(0, tb, wait, 0)


def _moe_expert_kernel(blk_e_ref, nblk_ref, x_ref, wgu_ref, wdn_ref, y_ref, wgu_sc, wdn_sc):
    i = pl.program_id(0)
    f32 = jnp.float32

    @pl.when(i < nblk_ref[0])
    def _():
        changed = jnp.logical_or(i == 0, blk_e_ref[i] != blk_e_ref[jnp.maximum(i - 1, 0)])

        @pl.when(changed)
        def _():
            wgu_sc[...] = wgu_ref[0].astype(MXU_DTYPE)
            wdn_sc[...] = wdn_ref[0].astype(MXU_DTYPE)

        gu = jnp.dot(x_ref[...].astype(MXU_DTYPE), wgu_sc[...], preferred_element_type=f32)
        gate = gu[:, :D_EXPERT]
        act = gate * (1.0 / (1.0 + jnp.exp(-gate))) * gu[:, D_EXPERT:]
        y_ref[...] = jnp.dot(act.astype(MXU_DTYPE), wdn_sc[...], preferred_element_type=f32)

    @pl.when(i >= nblk_ref[0])
    def _():
        y_ref[...] = jnp.zeros(y_ref.shape, f32)


def _moe_combine_kernel(dest_ref, x_ref, wt_ref, ybuf_ref, o_ref, rows_sc, sem):
    tb = x_ref.shape[0]

    def row_copy(r, k):
        return pltpu.make_async_copy(ybuf_ref.at[pl.ds(dest_ref[0, 0, 2 * r + k], 1)],
                                     rows_sc.at[k, pl.ds(r, 1)], sem)

    def start(r, c):
        row_copy(r, 0).start()
        row_copy(r, 1).start()
        return c

    def wait(r, c):
        row_copy(r, 0).wait()
        row_copy(r, 1).wait()
        return c

    lax.fori_loop(0, tb, start, 0)
    lax.fori_loop(0, tb, wait, 0)
    wt = wt_ref[...]
    o_ref[...] = x_ref[...] + (wt[:, 0:1] * rows_sc[0] + wt[:, 1:2] * rows_sc[1])


def _hier_moe_residual(x, ln_g, w_grp, b_grp, w_exp, b_exp, w_gu, w_dn):
    T, D = x.shape
    f32 = jnp.float32
    tb = min(256, T)
    assert T % tb == 0
    nt = T // tb
    pad_cols = ROUTE_COLS - N_GROUPS - N_EXPERTS
    w_r = jnp.pad(jnp.concatenate([w_grp, w_exp], axis=1), ((0, 0), (0, pad_cols))).astype(MXU_DTYPE)
    b_r = jnp.pad(jnp.concatenate([b_grp, b_exp]), (0, pad_cols)).reshape(1, ROUTE_COLS)
    tri = jnp.asarray(np.tril(np.ones((tb, tb), np.float32), -1), dtype=MXU_DTYPE)
    row = lambda i: (i, 0)
    fixed = lambda i: (0, 0)
    h, meta, wt, cnt = pl.pallas_call(
        _moe_route_kernel,
        grid=(nt,),
        in_specs=[pl.BlockSpec((tb, D), row), pl.BlockSpec((1, D), fixed),
                  pl.BlockSpec((D, ROUTE_COLS), fixed), pl.BlockSpec((1, ROUTE_COLS), fixed),
                  pl.BlockSpec((tb, tb), fixed)],
        out_specs=[pl.BlockSpec((tb, D), row), pl.BlockSpec((tb, ROUTE_COLS), row),
                   pl.BlockSpec((tb, ROUTE_COLS), row), pl.BlockSpec((1, ROUTE_COLS), fixed)],
        out_shape=[jax.ShapeDtypeStruct((T, D), f32), jax.ShapeDtypeStruct((T, ROUTE_COLS), jnp.int32),
                   jax.ShapeDtypeStruct((T, ROUTE_COLS), f32), jax.ShapeDtypeStruct((1, ROUTE_COLS), f32)],
        scratch_shapes=[pltpu.VMEM((1, ROUTE_COLS), f32)],
        compiler_params=pltpu.CompilerParams(dimension_semantics=("arbitrary",)),
        name="moe_route",
    )(x, ln_g.reshape(1, D), w_r, b_r, tri)

    counts = cnt[0, N_GROUPS:N_GROUPS + N_EXPERTS].astype(jnp.int32)
    padded = (counts + MOE_ROWS - 1) // MOE_ROWS * MOE_ROWS
    pad_end = jnp.cumsum(padded)
    pad_start = pad_end - padded
    n_blocks = -(-(T * TOP_K) // MOE_ROWS) + N_EXPERTS
    n_slots = n_blocks * MOE_ROWS
    dest = pad_start[meta[:, 0:2]] + meta[:, 2:4]
    dest = dest.reshape(nt, 1, 2 * tb)
    starts = jnp.arange(n_blocks, dtype=jnp.int32) * MOE_ROWS
    used = (pad_end[-1] // MOE_ROWS).astype(jnp.int32).reshape(1)
    blk_e = jnp.minimum(jnp.searchsorted(pad_end, jnp.minimum(starts, pad_end[-1] - 1), side='right'),
                        N_EXPERTS - 1).astype(jnp.int32)

    dest_spec = pl.BlockSpec((1, 1, 2 * tb), lambda i: (i, 0, 0), memory_space=pltpu.SMEM)
    xbuf = pl.pallas_call(
        _moe_dispatch_kernel,
        grid=(nt,),
        in_specs=[dest_spec, pl.BlockSpec((tb, D), row), pl.BlockSpec(memory_space=pl.ANY)],
        out_specs=pl.BlockSpec(memory_space=pl.ANY),
        out_shape=jax.ShapeDtypeStruct((n_slots, D), f32),
        scratch_shapes=[pltpu.SemaphoreType.DMA(())],
        input_output_aliases={2: 0},
        compiler_params=pltpu.CompilerParams(dimension_semantics=("arbitrary",)),
        name="moe_dispatch",
    )(dest, h, jnp.zeros((n_slots, D), f32))

    ybuf = pl.pallas_call(
        _moe_expert_kernel,
        grid_spec=pltpu.PrefetchScalarGridSpec(
            num_scalar_prefetch=2,
            grid=(n_blocks,),
            in_specs=[pl.BlockSpec((MOE_ROWS, D), lambda i, be, nb: (i, 0)),
                      pl.BlockSpec((1, D, 2 * D_EXPERT), lambda i, be, nb: (be[i], 0, 0)),
                      pl.BlockSpec((1, D_EXPERT, D), lambda i, be, nb: (be[i], 0, 0))],
            out_specs=pl.BlockSpec((MOE_ROWS, D), lambda i, be, nb: (i, 0)),
            scratch_shapes=[pltpu.VMEM((D, 2 * D_EXPERT), MXU_DTYPE), pltpu.VMEM((D_EXPERT, D), MXU_DTYPE)]),
        out_shape=jax.ShapeDtypeStruct((n_slots, D), f32),
        compiler_params=pltpu.CompilerParams(dimension_semantics=("arbitrary",),
                                             vmem_limit_bytes=48 * 1024 * 1024),
        name="moe_experts",
    )(blk_e, used, xbuf, w_gu, w_dn)

    return pl.pallas_call(
        _moe_combine_kernel,
        grid=(nt,),
        in_specs=[dest_spec, pl.BlockSpec((tb, D), row), pl.BlockSpec((tb, ROUTE_COLS), row),
                  pl.BlockSpec(memory_space=pl.ANY)],
        out_specs=pl.BlockSpec((tb, D), row),
        out_shape=jax.ShapeDtypeStruct((T, D), f32),
        scratch_shapes=[pltpu.VMEM((TOP_K, tb, D), f32), pltpu.SemaphoreType.DMA(())],
        compiler_params=pltpu.CompilerParams(dimension_semantics=("arbitrary",)),
        name="moe_combine",
    )(dest, x, wt, ybuf)


def _gelu_tanh(x):
    return 0.5 * x * (1.0 + jnp.tanh(math.sqrt(2.0 / math.pi) * (x + 0.044715 * (x * x * x))))


def _gmlp_kernel(x_ref, g_ref, win_ref, bin_ref, lng_ref, lnb_ref, ws_ref, bs_ref, wout_ref,
                 o_ref, v_ref, *, single_position):
    f32 = jnp.float32
    x = x_ref[...]
    rows = x.shape[0]
    h = x * lax.rsqrt(jnp.mean(x * x, axis=-1, keepdims=True) + EPS) * g_ref[...]
    z = _gelu_tanh(jnp.dot(h.astype(MXU_DTYPE), win_ref[...], preferred_element_type=f32) + bin_ref[...])
    u = z[:, :GM_WIDTH]
    v = z[:, GM_WIDTH:]
    mu = jnp.mean(v, axis=-1, keepdims=True)
    var = jnp.mean(jnp.square(v - mu), axis=-1, keepdims=True)
    v = (v - mu) * lax.rsqrt(var + EPS) * lng_ref[...] + lnb_ref[...]
    v_ref[0] = v
    vb = v.astype(MXU_DTYPE)
    if single_position:
        s = ws_ref[...].astype(f32) * vb.astype(f32) + bs_ref[...]
    else:
        parts = []
        for c in range(rows // CHUNK):
            vc = vb[c * CHUNK:(c + 1) * CHUNK]
            parts.append(jnp.concatenate(
                [jnp.dot(ws_ref[g], vc[:, g * GM_GROUP_DIM:(g + 1) * GM_GROUP_DIM], preferred_element_type=f32)
                 + bs_ref[g] for g in range(GM_GROUPS)], axis=1))
        s = jnp.concatenate(parts, axis=0)
    y = jnp.dot((u * s).astype(MXU_DTYPE), wout_ref[...], preferred_element_type=f32)
    o_ref[...] = x + y


def _gmlp_residual(x, ln_g, w_in, b_in, ln2_g, ln2_b, w_s, b_s, w_out, seq):
    T, D = x.shape
    f32 = jnp.float32
    cd = MXU_DTYPE
    single = seq == 1
    tb = T if single else 2 * CHUNK
    assert T % tb == 0 and (single or seq % tb == 0)
    steps_per_seq = 1 if single else seq // tb
    if single:
        ws = jnp.repeat(w_s[:, 0, 0], GM_GROUP_DIM).reshape(1, GM_WIDTH).astype(cd)
        bs = jnp.repeat(b_s[:, 0], GM_GROUP_DIM).reshape(1, GM_WIDTH)
        ws_spec = pl.BlockSpec((1, GM_WIDTH), lambda i: (0, 0))
        bs_spec = pl.BlockSpec((1, GM_WIDTH), lambda i: (0, 0))
    else:
        causal = np.tril(np.ones((CHUNK, CHUNK), bool))
        ws = jnp.where(causal[None], w_s, 0).astype(cd)
        bs = jnp.broadcast_to(b_s[:, :, None], (GM_GROUPS, CHUNK, GM_GROUP_DIM))
        ws_spec = pl.BlockSpec((GM_GROUPS, CHUNK, CHUNK), lambda i: (0, 0, 0))
        bs_spec = pl.BlockSpec((GM_GROUPS, CHUNK, GM_GROUP_DIM), lambda i: (0, 0, 0))
    fixed = lambda i: (0, 0)
    row = lambda i: (i, 0)
    return pl.pallas_call(
        functools.partial(_gmlp_kernel, single_position=single),
        grid=(T // tb,),
        in_specs=[pl.BlockSpec((tb, D), row), pl.BlockSpec((1, D), fixed),
                  pl.BlockSpec((D, 2 * GM_WIDTH), fixed), pl.BlockSpec((1, 2 * GM_WIDTH), fixed),
                  pl.BlockSpec((1, GM_WIDTH), fixed), pl.BlockSpec((1, GM_WIDTH), fixed),
                  ws_spec, bs_spec, pl.BlockSpec((GM_WIDTH, D), fixed)],
        out_specs=[pl.BlockSpec((tb, D), row), pl.BlockSpec((1, tb, GM_WIDTH), lambda i: (i // steps_per_seq, 0, 0))],
        out_shape=[jax.ShapeDtypeStruct((T, D), f32),
                   jax.ShapeDtypeStruct((T // (tb * steps_per_seq), tb, GM_WIDTH), f32)],
        compiler_params=pltpu.CompilerParams(dimension_semantics=("arbitrary",),
                                             vmem_limit_bytes=56 * 1024 * 1024),
        name="gmlp",
    )(x, ln_g.reshape(1, D), w_in.astype(cd), b_in.reshape(1, -1), ln2_g.reshape(1, -1), ln2_b.reshape(1, -1),
      ws, bs, w_out.astype(cd))


def rmsnorm(x, g):
    xf = x.astype(jnp.float32)
    y = xf * lax.rsqrt(jnp.mean(xf * xf, axis=-1, keepdims=True) + EPS)
    return (y * g.astype(jnp.float32)).astype(x.dtype)


def layernorm(x, g, b):
    xf = x.astype(jnp.float32)
    mu = jnp.mean(xf, axis=-1, keepdims=True)
    var = jnp.mean(jnp.square(xf - mu), axis=-1, keepdims=True)
    return ((xf - mu) * lax.rsqrt(var + EPS) * g.astype(jnp.float32) + b.astype(jnp.float32)).astype(x.dtype)


def rel_bucket(n):
    n = jnp.maximum(n, 0)
    max_exact = N_BUCKETS // 2
    nf = jnp.maximum(n, 1).astype(jnp.float32)
    large = max_exact + (jnp.log(nf / max_exact) / math.log(MAX_DISTANCE / max_exact)
                         * (N_BUCKETS - max_exact)).astype(jnp.int32)
    large = jnp.minimum(large, N_BUCKETS - 1)
    return jnp.where(n < max_exact, n, large)


def head_bias(rel, rel_bias):
    q, k = rel.shape
    b = rel_bias[rel_bucket(rel)]
    return b.reshape(q, k, KV_HEADS, Q_PER_KV).transpose(2, 3, 0, 1)


def masked_softmax(logits, mask):
    z = jnp.where(mask, logits.astype(jnp.float32), -1e30)
    return jax.nn.softmax(z, axis=-1) * mask


def nsa_project(xn, w_in, q_gain, k_gain):
    B, T, _ = xn.shape
    hq = N_HEADS * HEAD_DIM
    hkv = 6 * KV_HEADS * HEAD_DIM
    h = xn @ w_in
    q = rmsnorm(h[..., :hq].reshape(B, T, N_HEADS, HEAD_DIM), q_gain)
    kv = h[..., hq:hq + hkv].reshape(B, T, 6, KV_HEADS, HEAD_DIM)
    gates = jax.nn.sigmoid(h[..., hq + hkv:].astype(jnp.float32)).reshape(B, T, N_HEADS, 3)
    k_sel = rmsnorm(kv[:, :, 2], k_gain[1])
    k_win = rmsnorm(kv[:, :, 4], k_gain[2])
    return q, kv[:, :, 0], kv[:, :, 1], k_sel, kv[:, :, 3], k_win, kv[:, :, 5], gates


def compress(rows, pe, w1, w2):
    B, T = rows.shape[:2]
    n_ch = T // CMP_STRIDE
    c = rows[:, :n_ch * CMP_STRIDE].reshape(B, n_ch, CMP_STRIDE, KV_HEADS, HEAD_DIM)
    blk = jnp.concatenate([c[:, :-1], c[:, 1:]], axis=2) + pe[None, None, :, None, :]
    h = jax.nn.gelu(jnp.einsum('bnlkd,ldf->bnkf', blk, w1.reshape(CMP_LEN, HEAD_DIM, CMP_HID)))
    return jnp.einsum('bnkf,fd->bnkd', h, w2)


def compressed_kv(k_raw, v_raw, k_gain0, pe, w1, w2):
    kc = rmsnorm(compress(k_raw, pe[0], w1[0], w2[0]), k_gain0)
    vc = compress(v_raw, pe[1], w1[1], w2[1])
    end = jnp.arange(kc.shape[1]) * CMP_STRIDE + (CMP_LEN - 1)
    return kc, vc, end


def nsa_core(q, q_pos, kc, vc, kc_end, ks, vs, kw, vw, kw_pos, gates, rel_bias):
    B, Q = q.shape[:2]
    nc = kc.shape[1]
    scale = HEAD_DIM ** -0.5
    qh = q.reshape(B, Q, KV_HEADS, Q_PER_KV, HEAD_DIM)
    lc = jnp.einsum('bqkgd,bnkd->bkgqn', qh, kc) * scale + head_bias(q_pos[:, None] - kc_end[None, :], rel_bias)
    pc = masked_softmax(lc, kc_end[None, :] <= q_pos[:, None])
    oc = jnp.einsum('bkgqn,bnkd->bqkgd', pc.astype(vc.dtype), vc)
    nb = ks.shape[1] // SEL_BLOCK
    imp = jnp.pad(pc.sum(axis=2), ((0, 0), (0, 0), (0, 0), (0, nb * CMP_PER_SEL - nc)))
    imp = imp.reshape(B, KV_HEADS, Q, nb, CMP_PER_SEL).sum(-1)
    blk = jnp.arange(nb)
    d = (q_pos // SEL_BLOCK)[:, None] - blk[None, :]
    forced = (blk[None, :] == 0) | ((d >= 0) & (d < N_LOCAL))
    score = jnp.where(forced, jnp.inf, jnp.where(d >= 0, imp, -jnp.inf))
    n_sel = min(N_SEL, nb)
    _, idx = lax.top_k(score, n_sel)
    bi = jnp.arange(B)[:, None, None, None]
    ki = jnp.arange(KV_HEADS)[None, :, None, None]
    ksb = ks.reshape(B, nb, SEL_BLOCK, KV_HEADS, HEAD_DIM).transpose(0, 3, 1, 2, 4)
    vsb = vs.reshape(B, nb, SEL_BLOCK, KV_HEADS, HEAD_DIM).transpose(0, 3, 1, 2, 4)
    kg = ksb[bi, ki, idx].reshape(B, KV_HEADS, Q, n_sel * SEL_BLOCK, HEAD_DIM)
    vg = vsb[bi, ki, idx].reshape(B, KV_HEADS, Q, n_sel * SEL_BLOCK, HEAD_DIM)
    kpos = (idx[..., None] * SEL_BLOCK + jnp.arange(SEL_BLOCK)).reshape(B, KV_HEADS, Q, n_sel * SEL_BLOCK)
    rel_s = q_pos[None, None, :, None] - kpos
    tb = rel_bias.reshape(N_BUCKETS, KV_HEADS, Q_PER_KV)
    bs = tb[rel_bucket(rel_s), ki].transpose(0, 1, 4, 2, 3)
    ls = jnp.einsum('bqkgd,bkqjd->bkgqj', qh, kg) * scale + bs
    ps = masked_softmax(ls, (rel_s >= 0)[:, :, None])
    os_ = jnp.einsum('bkgqj,bkqjd->bqkgd', ps.astype(vg.dtype), vg)
    rel_w = q_pos[:, None] - kw_pos[None, :]
    mw = (rel_w >= 0) & (rel_w <= WINDOW) & (kw_pos[None, :] >= 0)
    lw = jnp.einsum('bqkgd,bwkd->bkgqw', qh, kw) * scale + head_bias(rel_w, rel_bias)
    pw = masked_softmax(lw, mw)
    ow = jnp.einsum('bkgqw,bwkd->bqkgd', pw.astype(vw.dtype), vw)
    g = gates.reshape(B, Q, KV_HEADS, Q_PER_KV, 3).astype(oc.dtype)
    o = g[..., 0:1] * oc + g[..., 1:2] * os_ + g[..., 2:3] * ow
    return o.reshape(B, Q, N_HEADS * HEAD_DIM)


def nsa_prompt(xn, w_in, w_out, q_gain, k_gain, pe, w1, w2, rel_bias):
    B, S, _ = xn.shape
    q, kcr, vcr, ks, vs, kw, vw, gates = nsa_project(xn, w_in, q_gain, k_gain)
    kc, vc, kc_end = compressed_kv(kcr, vcr, k_gain[0], pe, w1, w2)
    o = _nsa_attention(q, kc, vc, ks, vs, kw, vw, gates, rel_bias)
    w_keep = min(WINDOW, S)
    new_rows = jnp.stack([kcr, vcr, ks, vs], axis=2)
    new_win = jnp.stack([kw[:, S - w_keep:], vw[:, S - w_keep:]], axis=2)
    y = _mm(o.reshape(B * S, -1), w_out).reshape(B, S, -1)
    return y, new_rows, new_win


def nsa_sample(xn, past_rows, win_buf, w_in, w_out, q_gain, k_gain, pe, w1, w2, rel_bias):
    B, L, _ = xn.shape
    P = past_rows.shape[1]
    q, kcr, vcr, ks, vs, kw, vw, gates = nsa_project(xn, w_in, q_gain, k_gain)
    new_rows = jnp.stack([kcr, vcr, ks, vs], axis=2)
    rows = jnp.concatenate([past_rows, new_rows], axis=1)
    kc, vc, kc_end = compressed_kv(rows[:, :, 0], rows[:, :, 1], k_gain[0], pe, w1, w2)
    T = P + L
    tp = -(-T // SEL_BLOCK) * SEL_BLOCK
    pad = ((0, 0), (0, tp - T), (0, 0), (0, 0))
    ks_all = jnp.pad(rows[:, :, 2], pad)
    vs_all = jnp.pad(rows[:, :, 3], pad)
    win_all = jnp.concatenate([win_buf, jnp.stack([kw, vw], axis=2)], axis=1)
    wb = win_buf.shape[1]
    kw_pos = P - wb + jnp.arange(wb + L)
    q_pos = P + jnp.arange(L)
    o = nsa_core(q, q_pos, kc, vc, kc_end, ks_all, vs_all, win_all[:, :, 0], win_all[:, :, 1],
                 kw_pos, gates, rel_bias)
    w_keep = min(WINDOW, wb + L)
    return o @ w_out, new_rows, win_all[:, wb + L - w_keep:]


def gmlp_mixer(xn, w_in, b_in, ln_g, ln_b, w_s, b_s, w_out):
    B, L, _ = xn.shape
    z = jax.nn.gelu(xn @ w_in + b_in)
    u = z[..., :GM_WIDTH]
    v = layernorm(z[..., GM_WIDTH:], ln_g, ln_b)
    lp = -(-L // CHUNK) * CHUNK
    vp = jnp.pad(v, ((0, 0), (0, lp - L), (0, 0))).reshape(B, lp // CHUNK, CHUNK, GM_GROUPS, GM_GROUP_DIM)
    causal = jnp.tril(jnp.ones((CHUNK, CHUNK), dtype=bool))
    ws = jnp.where(causal[None], w_s, 0)
    s = jnp.einsum('gts,bcsgd->bctgd', ws, vp) + b_s.T[None, None, :, :, None]
    s = s.reshape(B, lp, GM_WIDTH)[:, :L]
    start = ((L - 1) // CHUNK) * CHUNK
    return (u * s) @ w_out, v[:, start:]


def hier_moe(x, w_grp, b_grp, w_exp, b_exp, w_gu, w_dn):
    T, D = x.shape
    pg = jax.nn.softmax((x @ w_grp + b_grp).astype(jnp.float32), axis=-1)
    grp = jnp.argmax(pg, axis=-1)
    g_w = jnp.take_along_axis(pg, grp[:, None], axis=-1)
    le = (x @ w_exp + b_exp).astype(jnp.float32).reshape(T, N_GROUPS, EXPERTS_PER_GROUP)
    le = jnp.take_along_axis(le, grp[:, None, None], axis=1)[:, 0]
    top_v, top_i = lax.top_k(le, TOP_K)
    wts = (jax.nn.softmax(top_v, axis=-1) * g_w).reshape(-1)
    eid = (grp[:, None] * EXPERTS_PER_GROUP + top_i).reshape(-1)
    n = T * TOP_K
    order = jnp.argsort(eid)
    e_s = eid[order]
    tok_s = order // TOP_K
    w_s = wts[order]
    counts = jnp.bincount(eid, length=N_EXPERTS)
    padded = (counts + MOE_BLOCK - 1) // MOE_BLOCK * MOE_BLOCK
    pad_end = jnp.cumsum(padded)
    pad_start = pad_end - padded
    start = jnp.cumsum(counts) - counts
    dest = pad_start[e_s] + jnp.arange(n) - start[e_s]
    n_blocks = -(-n // MOE_BLOCK) + N_EXPERTS
    xbuf = jnp.zeros((n_blocks * MOE_BLOCK, D), x.dtype).at[dest].set(x[tok_s])
    blk_e = jnp.minimum(jnp.searchsorted(pad_end, jnp.arange(n_blocks) * MOE_BLOCK, side='right'), N_EXPERTS - 1)

    def expert(args):
        xb, e = args
        gu = xb @ w_gu[e]
        return (jax.nn.silu(gu[:, :D_EXPERT]) * gu[:, D_EXPERT:]) @ w_dn[e]

    ybuf = lax.map(expert, (xbuf.reshape(n_blocks, MOE_BLOCK, D), blk_e)).reshape(-1, D)
    return jax.ops.segment_sum(ybuf[dest] * w_s[:, None].astype(x.dtype), tok_s, num_segments=T)


def kernel(x_prompt, x_sample, cache_nsa_kv, state_win_kv, page_table, rel_bias, ln_mix, ln_ffn,
           nsa_w_in, nsa_w_out, nsa_q_gain, nsa_k_gain, cmp_pe, cmp_w1, cmp_w2,
           gm_w_in, gm_b_in, gm_ln_g, gm_ln_b, gm_w_s, gm_b_s, gm_w_out,
           moe_w_grp, moe_b_grp, moe_w_exp, moe_b_exp, moe_w_gu, moe_w_dn):
    xp = x_prompt
    xs = x_sample
    kv_p, kv_s, win_p, win_s, gv_p, gv_s = [], [], [], [], [], []
    for i in range(DEPTH):
        a = i // N_MIXERS
        if i % N_MIXERS == 0:
            nsa = (ln_mix[i], nsa_w_in[a], nsa_w_out[a], nsa_q_gain[a], nsa_k_gain[a],
                   cmp_pe[a], cmp_w1[a], cmp_w2[a], rel_bias)
            xp, rp, wp = _nsa_prompt_layer(xp, *nsa)
            cmp_rows, sel_kvt = _page_gather(cache_nsa_kv, page_table, a)
            xs, rs, ws = _nsa_decode_layer(xs, cmp_rows, sel_kvt, state_win_kv[a], *nsa)
            kv_p.append(rp)
            kv_s.append(rs)
            win_p.append(wp)
            win_s.append(ws)
        else:
            gm = (ln_mix[i], gm_w_in[a], gm_b_in[a], gm_ln_g[a], gm_ln_b[a], gm_w_s[a], gm_b_s[a], gm_w_out[a])
            bp, sp = xp.shape[:2]
            bs_, ss = xs.shape[:2]
            xp2, vp = _gmlp_residual(xp.reshape(-1, D_MODEL), *gm, seq=sp)
            xs2, vs = _gmlp_residual(xs.reshape(-1, D_MODEL), *gm, seq=ss)
            xp = xp2.reshape(xp.shape)
            xs = xs2.reshape(xs.shape)
            start = ((sp - 1) // CHUNK) * CHUNK
            gv_p.append(vp[:, vp.shape[1] - (sp - start):])
            gv_s.append(vs.reshape(bs_, ss, GM_WIDTH))
        moe = (ln_ffn[i], moe_w_grp[i], moe_b_grp[i], moe_w_exp[i], moe_b_exp[i], moe_w_gu[i], moe_w_dn[i])
        xp = _hier_moe_residual(xp.reshape(-1, D_MODEL), *moe).reshape(xp.shape)
        xs = _hier_moe_residual(xs.reshape(-1, D_MODEL), *moe).reshape(xs.shape)
    new_kv_prompt = jnp.stack(kv_p, axis=2)
    new_kv_sample = jnp.stack(kv_s, axis=2)
    new_win_prompt = jnp.stack(win_p, axis=0)
    new_win_sample = jnp.stack(win_s, axis=0)
    new_gm_v_prompt = jnp.stack(gv_p, axis=0)
    new_gm_v_sample = jnp.stack(gv_s, axis=0)
    return (xp, xs, new_kv_prompt, new_kv_sample, new_win_prompt, new_win_sample, new_gm_v_prompt, new_gm_v_sample)
```

```python
import functools
import math

import jax
import jax.numpy as jnp
import numpy as np
from jax import lax
from jax.experimental import pallas as pl
from jax.experimental.pallas import tpu as pltpu

D_MODEL = 1024
PAGE_SIZE = 128
DEPTH = 2
N_MIXERS = 2
N_HEADS = 16
HEAD_DIM = 64
KV_HEADS = 4
Q_PER_KV = N_HEADS // KV_HEADS
CMP_LEN = 32
CMP_STRIDE = 16
CMP_HID = 64
SEL_BLOCK = 64
CMP_PER_SEL = SEL_BLOCK // CMP_STRIDE
N_SEL = 16
N_LOCAL = 2
WINDOW = 512
Q_BLOCK = 128
N_BUCKETS = 32
MAX_DISTANCE = 128
CHUNK = 128
GM_WIDTH = 2048
GM_GROUPS = 8
GM_GROUP_DIM = GM_WIDTH // GM_GROUPS
N_GROUPS = 4
EXPERTS_PER_GROUP = 8
N_EXPERTS = N_GROUPS * EXPERTS_PER_GROUP
TOP_K = 2
D_EXPERT = 512
MOE_BLOCK = 128
EPS = 1e-6


def _mm_kernel(x_ref, w_ref, o_ref):
    o_ref[...] = jnp.dot(x_ref[...].astype(jnp.bfloat16), w_ref[...].astype(jnp.bfloat16),
                         preferred_element_type=jnp.float32)


def _mm(x, w, tm=512, tn=512):
    m, k = x.shape
    n = w.shape[1]
    tm = min(tm, m)
    tn = min(tn, n)
    return pl.pallas_call(
        _mm_kernel,
        grid=(m // tm, n // tn),
        in_specs=[pl.BlockSpec((tm, k), lambda i, j: (i, 0)),
                  pl.BlockSpec((k, tn), lambda i, j: (0, j))],
        out_specs=pl.BlockSpec((tm, tn), lambda i, j: (i, j)),
        out_shape=jax.ShapeDtypeStruct((m, n), jnp.float32),
        name="mm",
    )(x, w)


NEG = -1e30
MXU_DTYPE = jnp.bfloat16
TQ = 256
TK = 256
GROUP_CHUNKS = 3
LANES = 128
BIAS_SPAN = 128
LOG2E = 1.4426950408889634


def _bucket_of_distance():
    d = np.arange(BIAS_SPAN)
    max_exact = N_BUCKETS // 2
    nf = np.maximum(d, 1).astype(np.float32)
    large = max_exact + (np.log(nf / np.float32(max_exact)) / np.float32(math.log(MAX_DISTANCE / max_exact))
                         * np.float32(N_BUCKETS - max_exact)).astype(np.int32)
    large = np.minimum(large, N_BUCKETS - 1)
    return np.where(d < max_exact, d, large).astype(np.int32)


def _attn_bias_tiles(rel_bias, seq):
    fd = rel_bias[_bucket_of_distance()].T
    far = fd[:, BIAS_SPAN - 1]
    fd = fd - far[:, None]

    def by_distance(d):
        return jnp.where(d >= 0, LOG2E * jnp.take(fd, np.clip(d, 0, BIAS_SPAN - 1), axis=1), NEG)

    def toeplitz(first_distance):
        c = np.arange(2 * TK)
        g = by_distance(first_distance - np.where(c < TK, c, c - 2 * TK))
        flat = jnp.tile(g, (1, TQ))[:, :TQ * (2 * TK - 1)]
        return flat.reshape(-1, TQ, 2 * TK - 1)[:, :, :TK]

    b0 = toeplitz(0)
    b1 = toeplitz(TQ)
    i = np.arange(TQ)[:, None]
    j = np.arange(TK)[None, :]
    w2 = np.where(2 * TQ + i - j <= WINDOW, 0.0, NEG).astype(np.float32)
    nc = seq // CMP_STRIDE
    span = -(-(BIAS_SPAN + TQ) // CMP_STRIDE)
    r = np.arange(nc - span, nc + span)[:, None]
    dc = np.arange(TQ)[None, :] - CMP_STRIDE * (r - nc) - (CMP_LEN - 1)
    n_heads = fd.shape[0]
    pat = jnp.concatenate([jnp.zeros((n_heads, nc - span, TQ), jnp.float32), by_distance(dc),
                           jnp.full((n_heads, nc - span, TQ), NEG, jnp.float32)], axis=1)
    return b0, b1, jnp.asarray(w2), pat


def _nsa_attn_kernel(q_ref, kc_ref, vct_ref, kat_ref, vsa_ref, kwt_ref, vwa_ref, g_ref,
                     b0_ref, b1_ref, w2_ref, pat_ref, o_ref, m_sc, acc_sc, imp_sc, qa_sc, p_sc, a_sc, *, nc, nb):
    f32 = jnp.float32
    cd = MXU_DTYPE
    qb = pl.program_id(2)

    r0 = pl.multiple_of(nc - (TQ // CMP_STRIDE) * qb, TQ // CMP_STRIDE)
    t_row = qb * TQ + lax.broadcasted_iota(jnp.int32, (1, TQ), 1)
    has_cmp = t_row >= CMP_LEN - 1
    for h in range(Q_PER_KV):
        st = lax.dot_general(kc_ref[0, 0], q_ref[0, h], (((1,), (1,)), ((), ())),
                             preferred_element_type=f32)
        st = st + pat_ref[h, pl.ds(r0, nc), :]
        e = jnp.exp2(st - jnp.max(st, axis=0, keepdims=True))
        inv = jnp.where(has_cmp, 1.0 / jnp.sum(e, axis=0, keepdims=True), 0.0)
        pt = e * inv
        for c in range(TQ // LANES):
            part = pt[:, c * LANES:(c + 1) * LANES]
            if h == 0:
                imp_sc[c] = part
            else:
                imp_sc[c] += part
        oct_h = jnp.dot(vct_ref[0, 0], pt.astype(MXU_DTYPE), preferred_element_type=f32)
        o_ref[0, :, h * HEAD_DIM:(h + 1) * HEAD_DIM] = g_ref[0, h][:, 0:1] * oct_h.T

    imp = jnp.concatenate(
        [sum(imp_sc[c, pl.ds(r, nb, stride=CMP_PER_SEL), :] for r in range(CMP_PER_SEL))
         for c in range(TQ // LANES)], axis=1)
    blk = lax.broadcasted_iota(jnp.int32, (nb, TQ), 0)
    t_blk = (qb * TQ + lax.broadcasted_iota(jnp.int32, (nb, TQ), 1)) // SEL_BLOCK
    dd = t_blk - blk
    forced = (blk == 0) | ((dd >= 0) & (dd < N_LOCAL))
    score = jnp.where(forced, jnp.inf, jnp.where(dd >= 0, imp, -jnp.inf))
    pen_t = jnp.full((nb, TQ), NEG, f32)
    for _ in range(min(N_SEL, nb)):
        best = jnp.max(score, axis=0, keepdims=True)
        first = jnp.min(jnp.where(score == best, blk, nb), axis=0, keepdims=True)
        hit = blk == first
        pen_t = jnp.where(hit, 0.0, pen_t)
        score = jnp.where(hit, -jnp.inf, score)
    pen = pen_t.T.astype(MXU_DTYPE)

    pad = jnp.zeros((TQ, qa_sc.shape[2] - nb - HEAD_DIM), cd)
    for h in range(Q_PER_KV):
        qa_sc[h] = jnp.concatenate([pen, q_ref[0, h], pad], axis=1)

    def reset():
        m_sc[...] = jnp.full(m_sc.shape, NEG, f32)
        acc_sc[...] = jnp.zeros(acc_sc.shape, f32)

    def run(lhs, k_ref, v_ref, chunks):
        for slot, (c, bias) in enumerate(chunks):
            for h in range(Q_PER_KV):
                s = jnp.dot(lhs(h), k_ref[0, 0, c], preferred_element_type=f32)
                if bias is not None:
                    s = s + bias(h)
                m_prev = m_sc[h]
                m_new = jnp.maximum(m_prev, jnp.max(s, axis=1, keepdims=True))
                p_sc[slot, h] = jnp.exp2(s - jnp.concatenate([m_new] * (TK // LANES), axis=1)).astype(cd)
                a_sc[slot, h] = jnp.exp2(m_prev - m_new)
                m_sc[h] = m_new
        for slot, (c, _) in enumerate(chunks):
            for h in range(Q_PER_KV):
                acc_sc[h] = a_sc[slot, h] * acc_sc[h] + jnp.dot(p_sc[slot, h], v_ref[0, 0, c],
                                                                preferred_element_type=f32)

    def emit(gate_col):
        for h in range(Q_PER_KV):
            acc = acc_sc[h]
            w = g_ref[0, h][:, gate_col:gate_col + 1] / acc[:, HEAD_DIM:HEAD_DIM + 1]
            o_ref[0, :, h * HEAD_DIM:(h + 1) * HEAD_DIM] += w * acc[:, :HEAD_DIM]

    own = lambda h: b0_ref[h]
    prev = lambda h: b1_ref[h]
    edge = lambda h: w2_ref[...]

    sel_lhs = lambda h: qa_sc[h]
    reset()
    n_far = jnp.maximum(qb - 1, 0)

    def far_group(i, carry):
        run(sel_lhs, kat_ref, vsa_ref, [(GROUP_CHUNKS * i + j, None) for j in range(GROUP_CHUNKS)])
        return carry

    lax.fori_loop(0, n_far // GROUP_CHUNKS, far_group, 0)
    for left in range(1, GROUP_CHUNKS):
        @pl.when(n_far % GROUP_CHUNKS == left)
        def _():
            run(sel_lhs, kat_ref, vsa_ref, [(n_far - left + j, None) for j in range(left)])

    @pl.when(qb >= 1)
    def _():
        run(sel_lhs, kat_ref, vsa_ref, [(qb - 1, prev), (qb, own)])

    @pl.when(qb == 0)
    def _():
        run(sel_lhs, kat_ref, vsa_ref, [(qb, own)])

    emit(1)

    win_lhs = lambda h: q_ref[0, h]
    reset()

    @pl.when(qb >= 2)
    def _():
        run(win_lhs, kwt_ref, vwa_ref, [(qb - 2, edge), (qb - 1, prev), (qb, own)])

    @pl.when(qb == 1)
    def _():
        run(win_lhs, kwt_ref, vwa_ref, [(qb - 1, prev), (qb, own)])

    @pl.when(qb == 0)
    def _():
        run(win_lhs, kwt_ref, vwa_ref, [(qb, own)])

    emit(2)


def _nsa_attention(qs, kcp, vct, kat, vsa, kwt, vwa, gt, rel_bias):
    B, _, S, _ = qs.shape
    assert S % TQ == 0 and TQ == TK and WINDOW == 2 * TQ and S // SEL_BLOCK >= N_SEL
    nc = S // CMP_STRIDE
    nb = S // SEL_BLOCK
    nch = S // TK
    ka = kat.shape[3]
    b0, b1, w2, pat = _attn_bias_tiles(rel_bias, S)
    G = Q_PER_KV
    kv_chunks = pl.BlockSpec((1, 1, nch, HEAD_DIM, TK), lambda b, k, i: (b, k, 0, 0, 0))
    ka_chunks = pl.BlockSpec((1, 1, nch, ka, TK), lambda b, k, i: (b, k, 0, 0, 0))
    v_chunks = pl.BlockSpec((1, 1, nch, TK, LANES), lambda b, k, i: (b, k, 0, 0, 0))
    return pl.pallas_call(
        functools.partial(_nsa_attn_kernel, nc=nc, nb=nb),
        grid=(B, KV_HEADS, S // TQ),
        in_specs=[
            pl.BlockSpec((1, G, TQ, HEAD_DIM), lambda b, k, i: (b, k, i, 0)),
            pl.BlockSpec((1, 1, nc, HEAD_DIM), lambda b, k, i: (b, k, 0, 0)),
            pl.BlockSpec((1, 1, HEAD_DIM, nc), lambda b, k, i: (b, k, 0, 0)),
            ka_chunks, v_chunks, kv_chunks, v_chunks,
            pl.BlockSpec((1, G, TQ, 3), lambda b, k, i: (b, k, i, 0)),
            pl.BlockSpec((G, TQ, TK), lambda b, k, i: (k, 0, 0)),
            pl.BlockSpec((G, TQ, TK), lambda b, k, i: (k, 0, 0)),
            pl.BlockSpec((TQ, TK), lambda b, k, i: (0, 0)),
            pl.BlockSpec((G, 2 * nc, TQ), lambda b, k, i: (k, 0, 0)),
        ],
        out_specs=pl.BlockSpec((1, TQ, G * HEAD_DIM), lambda b, k, i: (b, i, k)),
        out_shape=jax.ShapeDtypeStruct((B, S, N_HEADS * HEAD_DIM), jnp.float32),
        scratch_shapes=[
            pltpu.VMEM((G, TQ, LANES), jnp.float32),
            pltpu.VMEM((G, TQ, LANES), jnp.float32),
            pltpu.VMEM((TQ // LANES, nc, LANES), jnp.float32),
            pltpu.VMEM((G, TQ, ka), MXU_DTYPE),
            pltpu.VMEM((GROUP_CHUNKS, G, TQ, TK), MXU_DTYPE),
            pltpu.VMEM((GROUP_CHUNKS, G, TQ, LANES), jnp.float32),
        ],
        compiler_params=pltpu.CompilerParams(
            dimension_semantics=("arbitrary", "arbitrary", "arbitrary"),
            vmem_limit_bytes=56 * 1024 * 1024),
        name="nsa_attn",
    )(qs, kcp, vct, kat, vsa, kwt, vwa, gt, b0, b1, w2, pat)


def _nsa_prompt_layer(x, ln_g, w_in, w_out, q_gain, k_gain, pe, w1, w2, rel_bias):
    B, S, D = x.shape
    x2 = x.reshape(B * S, D)
    qs, rows, win, kat, vsa, kwt, vwa, c, gates = _nsa_project(x2, ln_g, w_in, q_gain, k_gain, B, S)
    nc = S // CMP_STRIDE
    cmp = _compress(c.reshape(2, B, nc, CMP_STRIDE * SLOT), pe, w1, w2, k_gain[0])
    cmp = cmp.reshape(2, B, nc, KV_HEADS, HEAD_DIM).astype(MXU_DTYPE)
    gt = gates[:, :3 * N_HEADS].reshape(B, S, N_HEADS, 3).transpose(0, 2, 1, 3)
    o = _nsa_attention(qs, cmp[0].transpose(0, 2, 1, 3), cmp[1].transpose(0, 2, 3, 1), kat, vsa, kwt, vwa, gt,
                       rel_bias)
    y = _proj_residual(x2, o.reshape(B * S, HQ), w_out)
    return (y.reshape(B, S, D), rows.reshape(B, S, 4, KV_HEADS, HEAD_DIM),
            win.reshape(B, WINDOW, 2, KV_HEADS, HEAD_DIM))


PAGES_PER_STEP = 8


def _page_gather_kernel(pt_ref, *refs):
    del pt_ref
    pages = refs[:PAGES_PER_STEP]
    c_ref, kvt_ref, rows_sc = refs[PAGES_PER_STEP:]
    for p, pg in enumerate(pages):
        pos = slice(p * PAGE_SIZE, (p + 1) * PAGE_SIZE)
        for slot in range(2):
            for pair in range(KV_HEADS // 2):
                tile = pg[0, 0, slot, 2 * pair:2 * pair + 2].reshape(LANES, PAGE_SIZE)
                rows_sc[slot * 2 + pair, pos, :] = tile.T
        for slot in range(2):
            kvt_ref[slot, 0, :, :, pos] = pg[0, 0, 2 + slot].astype(MXU_DTYPE)
    n_chunks = PAGES_PER_STEP * PAGE_SIZE // CMP_STRIDE
    for l in range(CMP_STRIDE):
        for t in range(4):
            lo = l * SLOT + (t % 2) * LANES
            c_ref[t // 2, 0, :, lo:lo + LANES] = rows_sc[t, pl.ds(l, n_chunks, stride=CMP_STRIDE), :].astype(MXU_DTYPE)


def _page_gather(cache, page_table, layer):
    nseq, n_pages = page_table.shape
    assert n_pages % PAGES_PER_STEP == 0 and 2 * LANES == SLOT
    P = n_pages * PAGE_SIZE
    cache_t = cache.transpose(0, 2, 3, 4, 5, 1)
    page_specs = [
        pl.BlockSpec((1, 1, 4, KV_HEADS, HEAD_DIM, PAGE_SIZE),
                     functools.partial(lambda r, n, j, pt: (pt[n, PAGES_PER_STEP * j + r], layer, 0, 0, 0, 0), r))
        for r in range(PAGES_PER_STEP)]
    rows_per_step = PAGES_PER_STEP * PAGE_SIZE
    return pl.pallas_call(
        _page_gather_kernel,
        grid_spec=pltpu.PrefetchScalarGridSpec(
            num_scalar_prefetch=1,
            grid=(nseq, n_pages // PAGES_PER_STEP),
            in_specs=page_specs,
            out_specs=[pl.BlockSpec((2, 1, rows_per_step // CMP_STRIDE, CMP_STRIDE * SLOT),
                                    lambda n, j, pt: (0, n, j, 0)),
                       pl.BlockSpec((2, 1, KV_HEADS, HEAD_DIM, rows_per_step), lambda n, j, pt: (0, n, 0, 0, j))],
            scratch_shapes=[pltpu.VMEM((4, rows_per_step, LANES), jnp.float32)]),
        out_shape=[jax.ShapeDtypeStruct((2, nseq, P // CMP_STRIDE, CMP_STRIDE * SLOT), MXU_DTYPE),
                   jax.ShapeDtypeStruct((2, nseq, KV_HEADS, HEAD_DIM, P), MXU_DTYPE)],
        compiler_params=pltpu.CompilerParams(dimension_semantics=("arbitrary", "arbitrary")),
        name="page_gather",
    )(page_table, *([cache_t] * PAGES_PER_STEP))


def _nsa_decode_kernel(q_ref, kc_ref, vc_ref, kst_ref, vst_ref, win_ref, rows_ref, wnew_ref, g_ref,
                       oh_ref, bc_ref, bs_ref, bw_ref, bn_ref, o_ref, *, nb):
    f32 = jnp.float32
    cd = MXU_DTYPE
    nt = (((1,), (1,)), ((), ()))
    rows = rows_ref[0]
    wnew = wnew_ref[0]

    def softmax_with_new(s, s_new):
        m = jnp.maximum(jnp.max(s, axis=1, keepdims=True), s_new)
        e = jnp.exp(s - m)
        e_new = jnp.exp(s_new - m)
        return e, e_new, 1.0 / (jnp.sum(e, axis=1, keepdims=True) + e_new)

    def bf(x):
        return x.astype(cd).astype(f32)

    for k in range(KV_HEADS):
        hs = slice(k * Q_PER_KV, (k + 1) * Q_PER_KV)
        lanes = slice(k * HEAD_DIM, (k + 1) * HEAD_DIM)
        q = q_ref[0, hs, :]
        qf = q.astype(f32)
        gates = g_ref[0, hs, :]

        def new_key(row):
            return jnp.sum(qf * bf(row), axis=1, keepdims=True) + bn_ref[hs, :]

        sc = lax.dot_general(q, kc_ref[0, k], nt, preferred_element_type=f32) + bc_ref[hs, :]
        ec = jnp.exp(sc - jnp.max(sc, axis=1, keepdims=True))
        pc = ec / jnp.sum(ec, axis=1, keepdims=True)
        out = gates[:, 0:1] * jnp.dot(pc.astype(cd), vc_ref[0, k], preferred_element_type=f32)

        imp = sum(pc[:, r * nb:(r + 1) * nb] for r in range(CMP_PER_SEL))
        imp = jnp.sum(imp, axis=0, keepdims=True)
        blk = lax.broadcasted_iota(jnp.int32, imp.shape, 1)
        forced = (blk == 0) | (blk == nb - 1)
        score = jnp.where(forced, jnp.inf, imp)
        pen = jnp.full(imp.shape, NEG, f32)
        for _ in range(N_SEL - 1):
            best = jnp.max(score, axis=1, keepdims=True)
            first = jnp.min(jnp.where(score == best, blk, nb), axis=1, keepdims=True)
            hit = blk == first
            pen = jnp.where(hit, 0.0, pen)
            score = jnp.where(hit, -jnp.inf, score)
        pen8 = jnp.broadcast_to(pen, (8, nb)).astype(cd)
        key_pen = jnp.dot(pen8, oh_ref[...], preferred_element_type=f32)[0:1]

        ss = jnp.dot(q, kst_ref[0, 0, k], preferred_element_type=f32) + key_pen + bs_ref[hs, :]
        e, e_new, inv = softmax_with_new(ss, new_key(rows[:, 2 * SLOT:3 * SLOT][:, lanes]))
        acc = (lax.dot_general(e.astype(cd), vst_ref[0, 0, k], nt, preferred_element_type=f32)
               + bf(e_new) * bf(rows[:, 3 * SLOT:4 * SLOT][:, lanes]))
        out = out + gates[:, 1:2] * inv * acc

        sw = lax.dot_general(q, win_ref[0, 0, k], nt, preferred_element_type=f32) + bw_ref[hs, :]
        e, e_new, inv = softmax_with_new(sw, new_key(wnew[:, :SLOT][:, lanes]))
        acc = (jnp.dot(e.astype(cd), win_ref[0, 1, k], preferred_element_type=f32)
               + bf(e_new) * bf(wnew[:, SLOT:][:, lanes]))
        o_ref[0, hs, :] = out + gates[:, 2:3] * inv * acc


def _nsa_decode_layer(x, cmp_rows, sel_kvt, win_buf, ln_g, w_in, w_out, q_gain, k_gain, pe, w1, w2, rel_bias):
    N, L, D = x.shape
    P = sel_kvt.shape[-1]
    wb = win_buf.shape[1]
    assert L == 1 and P % SEL_BLOCK == 0 and wb == WINDOW and P >= BIAS_SPAN and P // SEL_BLOCK >= N_SEL - 1
    f32 = jnp.float32
    cd = MXU_DTYPE
    nc = P // CMP_STRIDE
    nb = P // SEL_BLOCK
    x2 = x.reshape(N, D)
    q, rows, wnew, gates = _nsa_project(x2, ln_g, w_in, q_gain, k_gain, N, 1)
    cmp = _compress(cmp_rows, pe, w1, w2, k_gain[0])
    cmp = cmp.reshape(2, N, nb, CMP_PER_SEL, KV_HEADS, HEAD_DIM).transpose(0, 1, 4, 3, 2, 5)
    cmp = cmp.reshape(2, N, KV_HEADS, nc, HEAD_DIM).astype(cd)
    qs = (q.reshape(N, N_HEADS, HEAD_DIM) * HEAD_DIM ** -0.5).astype(cd)
    win = win_buf.transpose(0, 2, 3, 1, 4).astype(cd)
    onehot = jnp.asarray(np.arange(nb)[:, None] == (np.arange(P) // SEL_BLOCK)[None, :], dtype=cd)
    fd = rel_bias[_bucket_of_distance()].T
    fd = fd - fd[:, BIAS_SPAN - 1:]
    cidx = (np.arange(nc) % nb) * CMP_PER_SEL + np.arange(nc) // nb
    dc = P - (cidx * CMP_STRIDE + CMP_LEN - 1)
    b_cmp = jnp.where(dc >= 0, jnp.take(fd, np.clip(dc, 0, BIAS_SPAN - 1), axis=1), NEG)
    b_sel = jnp.take(fd, np.clip(P - np.arange(P), 0, BIAS_SPAN - 1), axis=1)
    b_win = jnp.take(fd, np.clip(wb - np.arange(wb), 0, BIAS_SPAN - 1), axis=1)
    b_new = fd[:, 0:1]
    seq3 = lambda n: (n, 0, 0)
    seq4 = lambda n: (n, 0, 0, 0)
    fixed = lambda n: (0, 0)
    o = pl.pallas_call(
        functools.partial(_nsa_decode_kernel, nb=nb),
        grid=(N,),
        in_specs=[pl.BlockSpec((1, N_HEADS, HEAD_DIM), seq3),
                  pl.BlockSpec((1, KV_HEADS, nc, HEAD_DIM), seq4),
                  pl.BlockSpec((1, KV_HEADS, nc, HEAD_DIM), seq4),
                  pl.BlockSpec((1, 1, KV_HEADS, HEAD_DIM, P), lambda n: (0, n, 0, 0, 0)),
                  pl.BlockSpec((1, 1, KV_HEADS, HEAD_DIM, P), lambda n: (1, n, 0, 0, 0)),
                  pl.BlockSpec((1, 2, KV_HEADS, wb, HEAD_DIM), lambda n: (n, 0, 0, 0, 0)),
                  pl.BlockSpec((1, 1, 4 * SLOT), seq3),
                  pl.BlockSpec((1, 1, 2 * SLOT), seq3),
                  pl.BlockSpec((1, N_HEADS, 3), seq3),
                  pl.BlockSpec((nb, P), fixed), pl.BlockSpec((N_HEADS, nc), fixed),
                  pl.BlockSpec((N_HEADS, P), fixed), pl.BlockSpec((N_HEADS, wb), fixed),
                  pl.BlockSpec((N_HEADS, 1), fixed)],
        out_specs=pl.BlockSpec((1, N_HEADS, HEAD_DIM), seq3),
        out_shape=jax.ShapeDtypeStruct((N, N_HEADS, HEAD_DIM), f32),
        compiler_params=pltpu.CompilerParams(dimension_semantics=("arbitrary",),
                                             vmem_limit_bytes=48 * 1024 * 1024),
        name="nsa_decode_attn",
    )(qs, cmp[0], cmp[1], sel_kvt, sel_kvt, win, rows.reshape(N, 1, 4 * SLOT), wnew.reshape(N, 1, 2 * SLOT),
      gates[:, :3 * N_HEADS].reshape(N, N_HEADS, 3), onehot, b_cmp, b_sel, b_win, b_new)
    y = _proj_residual(x2, o.reshape(N, HQ), w_out)
    new_win = jnp.concatenate([win_buf, wnew.reshape(N, 1, 2, KV_HEADS, HEAD_DIM)], axis=1)[:, 1:]
    return y.reshape(N, 1, D), rows.reshape(N, 1, 4, KV_HEADS, HEAD_DIM), new_win


def _split3(x):
    hi = x.astype(MXU_DTYPE)
    r1 = x - hi.astype(jnp.float32)
    mid = r1.astype(MXU_DTYPE)
    return hi, mid, (r1 - mid.astype(jnp.float32)).astype(MXU_DTYPE)


def _nsa_sample_kernel(q_ref, kc_ref, vc_ref, ks_ref, vs_ref, win_ref, rows_ref, wnew_ref, g_ref,
                       oh_ref, grp_ref, own_ref, bc_ref, bs_ref, bw_ref, bn_ref, o_ref, *, nb):
    f32 = jnp.float32
    cd = MXU_DTYPE
    nt = (((1,), (1,)), ((), ()))
    q = q_ref[0]
    qf = q.astype(f32)

    def softmax_with_new(s, s_new):
        m = jnp.maximum(jnp.max(s, axis=1, keepdims=True), s_new)
        e = jnp.exp(s - m)
        e_new = jnp.exp(s_new - m)
        return e, e_new, 1.0 / (jnp.sum(e, axis=1, keepdims=True) + e_new)

    def new_key(row):
        return jnp.sum(qf * row.astype(cd).astype(f32), axis=1, keepdims=True) + bn_ref[...]

    sc = lax.dot_general(q, kc_ref[0], nt, preferred_element_type=f32) + bc_ref[...]
    ec = jnp.exp(sc - jnp.max(sc, axis=1, keepdims=True))
    pc = ec / jnp.sum(ec, axis=1, keepdims=True)
    out = g_ref[0][:, 0:1] * jnp.dot(pc.astype(cd), vc_ref[0], preferred_element_type=f32)

    imp = sum(pc[:, r * nb:(r + 1) * nb] for r in range(CMP_PER_SEL))
    imp = sum(jnp.dot(grp_ref[...], part, preferred_element_type=f32) for part in _split3(imp))
    blk = lax.broadcasted_iota(jnp.int32, imp.shape, 1)
    forced = (blk == 0) | (blk == nb - 1)
    score = jnp.where(forced, jnp.inf, imp)
    pen = jnp.full(imp.shape, NEG, f32)
    for _ in range(N_SEL - 1):
        best = jnp.max(score, axis=1, keepdims=True)
        first = jnp.min(jnp.where(score == best, blk, nb), axis=1, keepdims=True)
        hit = blk == first
        pen = jnp.where(hit, 0.0, pen)
        score = jnp.where(hit, -jnp.inf, score)

    ss = (lax.dot_general(q, ks_ref[0, 0].astype(cd), nt, preferred_element_type=f32)
          + jnp.dot(pen.astype(cd), oh_ref[...], preferred_element_type=f32) + bs_ref[...])
    rows = rows_ref[0]
    e, e_new, inv = softmax_with_new(ss, new_key(rows[:, 2 * SLOT:3 * SLOT]))
    acc = (jnp.dot(e.astype(cd), vs_ref[0, 0].astype(cd), preferred_element_type=f32)
           + e_new.astype(cd).astype(f32) * rows[:, 3 * SLOT:4 * SLOT].astype(cd).astype(f32))
    out = out + g_ref[0][:, 1:2] * inv * acc

    win = win_ref[0]
    wnew = wnew_ref[0]
    sw = lax.dot_general(q, win[:, :SLOT].astype(cd), nt, preferred_element_type=f32) + bw_ref[...]
    e, e_new, inv = softmax_with_new(sw, new_key(wnew[:, :SLOT]))
    acc = (jnp.dot(e.astype(cd), win[:, SLOT:].astype(cd), preferred_element_type=f32)
           + e_new.astype(cd).astype(f32) * wnew[:, SLOT:].astype(cd).astype(f32))
    out = out + g_ref[0][:, 2:3] * inv * acc
    o_ref[0] = out * own_ref[...]


def _nsa_sample_layer(x, past, win_buf, ln_g, w_in, w_out, q_gain, k_gain, pe, w1, w2, rel_bias):
    N, L, D = x.shape
    P = past.shape[2]
    wb = win_buf.shape[1]
    assert L == 1 and P % SEL_BLOCK == 0 and wb == WINDOW and P >= BIAS_SPAN and P // SEL_BLOCK >= N_SEL - 1
    f32 = jnp.float32
    cd = MXU_DTYPE
    nc = P // CMP_STRIDE
    nb = P // SEL_BLOCK
    x2 = x.reshape(N, D)
    q, rows, wnew, gates = _nsa_project(x2, ln_g, w_in, q_gain, k_gain, N, 1)
    cmp = _compress(past.reshape(4, N, nc, CMP_STRIDE * SLOT), pe, w1, w2, k_gain[0])
    cmp = cmp.reshape(2, N, nb, CMP_PER_SEL, SLOT).transpose(0, 1, 3, 2, 4).reshape(2, N, nc, SLOT).astype(cd)
    head_kv = np.arange(N_HEADS) // Q_PER_KV
    own = (head_kv[:, None] == (np.arange(SLOT) // HEAD_DIM)[None, :]).astype(np.float32)
    qbd = (jnp.tile(q.reshape(N, N_HEADS, HEAD_DIM), (1, 1, KV_HEADS)) * own * HEAD_DIM ** -0.5).astype(cd)
    grp = jnp.asarray(head_kv[:, None] == head_kv[None, :], dtype=cd)
    onehot = jnp.asarray(np.arange(nb)[:, None] == (np.arange(P) // SEL_BLOCK)[None, :], dtype=cd)
    fd = rel_bias[_bucket_of_distance()].T
    fd = fd - fd[:, BIAS_SPAN - 1:]
    cidx = (np.arange(nc) % nb) * CMP_PER_SEL + np.arange(nc) // nb
    dc = P - (cidx * CMP_STRIDE + CMP_LEN - 1)
    b_cmp = jnp.where(dc >= 0, jnp.take(fd, np.clip(dc, 0, BIAS_SPAN - 1), axis=1), NEG)
    b_sel = jnp.take(fd, np.clip(P - np.arange(P), 0, BIAS_SPAN - 1), axis=1)
    b_win = jnp.take(fd, np.clip(wb - np.arange(wb), 0, BIAS_SPAN - 1), axis=1)
    b_new = fd[:, 0:1]
    seq3 = lambda n: (n, 0, 0)
    fixed = lambda n: (0, 0)
    o = pl.pallas_call(
        functools.partial(_nsa_sample_kernel, nb=nb),
        grid=(N,),
        in_specs=[pl.BlockSpec((1, N_HEADS, SLOT), seq3),
                  pl.BlockSpec((1, nc, SLOT), seq3),
                  pl.BlockSpec((1, nc, SLOT), seq3),
                  pl.BlockSpec((1, 1, P, SLOT), lambda n: (2, n, 0, 0)),
                  pl.BlockSpec((1, 1, P, SLOT), lambda n: (3, n, 0, 0)),
                  pl.BlockSpec((1, wb, 2 * SLOT), seq3),
                  pl.BlockSpec((1, 1, 4 * SLOT), seq3),
                  pl.BlockSpec((1, 1, 2 * SLOT), seq3),
                  pl.BlockSpec((1, N_HEADS, 3), seq3),
                  pl.BlockSpec((nb, P), fixed), pl.BlockSpec((N_HEADS, N_HEADS), fixed),
                  pl.BlockSpec((N_HEADS, SLOT), fixed), pl.BlockSpec((N_HEADS, nc), fixed),
                  pl.BlockSpec((N_HEADS, P), fixed), pl.BlockSpec((N_HEADS, wb), fixed),
                  pl.BlockSpec((N_HEADS, 1), fixed)],
        out_specs=pl.BlockSpec((1, N_HEADS, SLOT), seq3),
        out_shape=jax.ShapeDtypeStruct((N, N_HEADS, SLOT), f32),
        compiler_params=pltpu.CompilerParams(dimension_semantics=("arbitrary",),
                                             vmem_limit_bytes=56 * 1024 * 1024),
        name="nsa_sample_attn",
    )(qbd, cmp[0], cmp[1], past, past, win_buf.reshape(N, wb, 2 * SLOT), rows.reshape(N, 1, 4 * SLOT),
      wnew.reshape(N, 1, 2 * SLOT), gates[:, :3 * N_HEADS].reshape(N, N_HEADS, 3),
      onehot, grp, jnp.asarray(own), b_cmp, b_sel, b_win, b_new)
    o = o.reshape(N, KV_HEADS, Q_PER_KV, KV_HEADS, HEAD_DIM)
    o = jnp.stack([o[:, k, :, k] for k in range(KV_HEADS)], axis=1).reshape(N, HQ)
    y = _proj_residual(x2, o, w_out)
    new_win = jnp.concatenate([win_buf, wnew.reshape(N, 1, 2, KV_HEADS, HEAD_DIM)], axis=1)[:, 1:]
    return y.reshape(N, 1, D), rows.reshape(N, 1, 4, KV_HEADS, HEAD_DIM), new_win


HQ = N_HEADS * HEAD_DIM
SLOT = KV_HEADS * HEAD_DIM
NSA_COLS = HQ + 6 * SLOT + 3 * N_HEADS
NSA_COLS_PAD = -(-NSA_COLS // LANES) * LANES
GATE_COLS = NSA_COLS_PAD - HQ - 6 * SLOT


def _nsa_proj_kernel(x_ref, g_ref, w_ref, qg_ref, kg_ref, *outs, prompt, n_blocks):
    f32 = jnp.float32
    x = x_ref[...]
    xn = x * lax.rsqrt(jnp.mean(x * x, axis=-1, keepdims=True) + EPS) * g_ref[...]
    h = jnp.dot(xn.astype(MXU_DTYPE), w_ref[...], preferred_element_type=f32)

    def head_norm(v, gain):
        return v * lax.rsqrt(jnp.mean(v * v, axis=-1, keepdims=True) + EPS) * gain

    def slot(s, kv):
        lo = HQ + s * SLOT + kv * HEAD_DIM
        return h[:, lo:lo + HEAD_DIM]

    gates = 1.0 / (1.0 + jnp.exp(-h[:, HQ + 6 * SLOT:]))
    if prompt:
        q_ref, rows_ref, win_ref, kat_ref, vsa_ref, kwt_ref, vwa_ref, c_ref, gate_ref = outs
        for hh in range(N_HEADS):
            qn = head_norm(h[:, hh * HEAD_DIM:(hh + 1) * HEAD_DIM], qg_ref[...])
            q_ref[0, hh] = (qn * (HEAD_DIM ** -0.5 * LOG2E)).astype(MXU_DTYPE)
        c_ref[0, 0] = h[:, HQ:HQ + SLOT].astype(MXU_DTYPE)
        c_ref[1, 0] = h[:, HQ + SLOT:HQ + 2 * SLOT].astype(MXU_DTYPE)
        rows, ka = x.shape[0], kat_ref.shape[3]
        nb = n_blocks
        first_key = (pl.program_id(0) % (nb * SEL_BLOCK // rows)) * rows
        key_blk = (first_key + lax.broadcasted_iota(jnp.int32, (nb, rows), 1)) // SEL_BLOCK
        onehot = jnp.where(key_blk == lax.broadcasted_iota(jnp.int32, (nb, rows), 0), 1.0, 0.0).astype(MXU_DTYPE)
        kpad = jnp.zeros((ka - nb - HEAD_DIM, rows), MXU_DTYPE)
        ones = jnp.ones((rows, LANES - HEAD_DIM), MXU_DTYPE)
    else:
        q_ref, rows_ref, win_ref, gate_ref = outs
        for hh in range(N_HEADS):
            q_ref[:, hh * HEAD_DIM:(hh + 1) * HEAD_DIM] = head_norm(h[:, hh * HEAD_DIM:(hh + 1) * HEAD_DIM],
                                                                    qg_ref[...])
    rows_ref[:, 0:2 * SLOT] = h[:, HQ:HQ + 2 * SLOT]
    rows_ref[:, 3 * SLOT:4 * SLOT] = h[:, HQ + 3 * SLOT:HQ + 4 * SLOT]
    for kv in range(KV_HEADS):
        lanes = slice(kv * HEAD_DIM, (kv + 1) * HEAD_DIM)
        ksn = head_norm(slot(2, kv), kg_ref[1:2, :])
        kwn = head_norm(slot(4, kv), kg_ref[2:3, :])
        rows_ref[:, 2 * SLOT + kv * HEAD_DIM:2 * SLOT + (kv + 1) * HEAD_DIM] = ksn
        if prompt:
            win_ref[0, 0, :, lanes] = kwn
            kat_ref[0, kv, 0] = jnp.concatenate([onehot, ksn.T.astype(MXU_DTYPE), kpad], axis=0)
            kwt_ref[0, kv, 0] = kwn.T.astype(MXU_DTYPE)
            vsa_ref[0, kv, 0] = jnp.concatenate([slot(3, kv).astype(MXU_DTYPE), ones], axis=1)
            vwa_ref[0, kv, 0] = jnp.concatenate([slot(5, kv).astype(MXU_DTYPE), ones], axis=1)
        else:
            win_ref[:, lanes] = kwn
    if prompt:
        win_ref[0, 0, :, SLOT:2 * SLOT] = h[:, HQ + 5 * SLOT:HQ + 6 * SLOT]
    else:
        win_ref[:, SLOT:2 * SLOT] = h[:, HQ + 5 * SLOT:HQ + 6 * SLOT]
    gate_ref[...] = gates


def _nsa_project(x, ln_g, w_in, q_gain, k_gain, batch, seq):
    T, D = x.shape
    f32 = jnp.float32
    cd = MXU_DTYPE
    prompt = seq > 1
    tb = TK if prompt else T
    w = jnp.pad(w_in, ((0, 0), (0, NSA_COLS_PAD - NSA_COLS))).astype(cd)
    fixed = lambda i: (0, 0)
    row = lambda i: (i, 0)
    in_specs = [pl.BlockSpec((tb, D), row), pl.BlockSpec((1, D), fixed), pl.BlockSpec((D, NSA_COLS_PAD), fixed),
                pl.BlockSpec((1, HEAD_DIM), fixed), pl.BlockSpec((3, HEAD_DIM), fixed)]
    if prompt:
        assert seq % tb == 0 and WINDOW == 2 * tb
        n = seq // tb
        n_blocks = seq // SEL_BLOCK
        ka = -(-(n_blocks + HEAD_DIM) // LANES) * LANES
        chunk = lambda i: (i // n, 0, i % n, 0, 0)
        kt_spec = pl.BlockSpec((1, KV_HEADS, 1, HEAD_DIM, tb), chunk)
        ka_spec = pl.BlockSpec((1, KV_HEADS, 1, ka, tb), chunk)
        v_spec = pl.BlockSpec((1, KV_HEADS, 1, tb, LANES), chunk)
        kt_shape = jax.ShapeDtypeStruct((batch, KV_HEADS, n, HEAD_DIM, tb), cd)
        ka_shape = jax.ShapeDtypeStruct((batch, KV_HEADS, n, ka, tb), cd)
        v_shape = jax.ShapeDtypeStruct((batch, KV_HEADS, n, tb, LANES), cd)
        out_specs = [
            pl.BlockSpec((1, N_HEADS, tb, HEAD_DIM), lambda i: (i // n, 0, i % n, 0)),
            pl.BlockSpec((tb, 4 * SLOT), row),
            pl.BlockSpec((1, 1, tb, 2 * SLOT), lambda i: (i // n, jnp.maximum(i % n - (n - 2), 0), 0, 0)),
            ka_spec, v_spec, kt_spec, v_spec,
            pl.BlockSpec((2, 1, tb, SLOT), lambda i: (0, i // n, i % n, 0)),
            pl.BlockSpec((tb, GATE_COLS), row)]
        out_shape = [
            jax.ShapeDtypeStruct((batch, N_HEADS, seq, HEAD_DIM), cd),
            jax.ShapeDtypeStruct((T, 4 * SLOT), f32),
            jax.ShapeDtypeStruct((batch, 2, tb, 2 * SLOT), f32),
            ka_shape, v_shape, kt_shape, v_shape,
            jax.ShapeDtypeStruct((2, batch, seq, SLOT), cd),
            jax.ShapeDtypeStruct((T, GATE_COLS), f32)]
    else:
        out_specs = [pl.BlockSpec((tb, HQ), row), pl.BlockSpec((tb, 4 * SLOT), row),
                     pl.BlockSpec((tb, 2 * SLOT), row), pl.BlockSpec((tb, GATE_COLS), row)]
        n_blocks = 0
        out_shape = [jax.ShapeDtypeStruct((T, HQ), f32), jax.ShapeDtypeStruct((T, 4 * SLOT), f32),
                     jax.ShapeDtypeStruct((T, 2 * SLOT), f32), jax.ShapeDtypeStruct((T, GATE_COLS), f32)]
    return pl.pallas_call(
        functools.partial(_nsa_proj_kernel, prompt=prompt, n_blocks=n_blocks),
        grid=(T // tb,),
        in_specs=in_specs, out_specs=out_specs, out_shape=out_shape,
        compiler_params=pltpu.CompilerParams(dimension_semantics=("arbitrary",),
                                             vmem_limit_bytes=48 * 1024 * 1024),
        name="nsa_proj",
    )(x, ln_g.reshape(1, D), w, q_gain.reshape(1, HEAD_DIM), k_gain)


def _compress_kernel(c_ref, w1_ref, pe_ref, w2_ref, kg_ref, o_ref):
    f32 = jnp.float32
    u = jnp.dot(c_ref[0, 0].astype(MXU_DTYPE), w1_ref[0], preferred_element_type=f32)
    pe = jnp.dot(pe_ref[0], w1_ref[0], preferred_element_type=f32)
    nc = u.shape[0]
    hid = u[:, :SLOT] + pltpu.roll(u[:, SLOT:], nc - 1, 0) + (pe[0:1, :SLOT] + pe[1:2, SLOT:])
    out = jnp.dot(_gelu_tanh(hid).astype(MXU_DTYPE), w2_ref[0], preferred_element_type=f32)
    is_key = pl.program_id(1) == 0
    for kv in range(KV_HEADS):
        lanes = slice(kv * HEAD_DIM, (kv + 1) * HEAD_DIM)
        v = out[:, lanes]
        vn = v * lax.rsqrt(jnp.mean(v * v, axis=-1, keepdims=True) + EPS) * kg_ref[...]
        o_ref[0, 0, :, lanes] = jnp.where(is_key, vn, v)


def _compress(c, pe, w1, w2, k_gain0):
    n, nc = c.shape[1:3]
    cd = MXU_DTYPE
    kc = CMP_STRIDE * SLOT
    eye = jnp.eye(KV_HEADS, dtype=jnp.float32)
    w1h = w1.reshape(2, 2, CMP_STRIDE, HEAD_DIM, CMP_HID)
    w1b = jnp.einsum('shldf,kj->slkdhjf', w1h, eye).reshape(2, kc, 2 * SLOT).astype(cd)
    w2b = jnp.einsum('sdf,kj->skdjf', w2, eye).reshape(2, SLOT, SLOT).astype(cd)
    peh = jnp.broadcast_to(pe.reshape(2, 2, CMP_STRIDE, 1, HEAD_DIM), (2, 2, CMP_STRIDE, KV_HEADS, HEAD_DIM))
    peh = jnp.pad(peh.reshape(2, 2, kc), ((0, 0), (0, 14), (0, 0))).astype(cd)
    return pl.pallas_call(
        _compress_kernel,
        grid=(n, 2),
        in_specs=[pl.BlockSpec((1, 1, nc, kc), lambda b, s: (s, b, 0, 0)),
                  pl.BlockSpec((1, kc, 2 * SLOT), lambda b, s: (s, 0, 0)),
                  pl.BlockSpec((1, 16, kc), lambda b, s: (s, 0, 0)),
                  pl.BlockSpec((1, SLOT, SLOT), lambda b, s: (s, 0, 0)),
                  pl.BlockSpec((1, HEAD_DIM), lambda b, s: (0, 0))],
        out_specs=pl.BlockSpec((1, 1, nc, SLOT), lambda b, s: (s, b, 0, 0)),
        out_shape=jax.ShapeDtypeStruct((2, n, nc, SLOT), jnp.float32),
        compiler_params=pltpu.CompilerParams(dimension_semantics=("arbitrary", "arbitrary"),
                                             vmem_limit_bytes=48 * 1024 * 1024),
        name="nsa_compress",
    )(c, w1b, peh, w2b, k_gain0.reshape(1, HEAD_DIM))


def _proj_residual_kernel(x_ref, a_ref, w_ref, o_ref):
    o_ref[...] = x_ref[...] + jnp.dot(a_ref[...].astype(MXU_DTYPE), w_ref[...],
                                      preferred_element_type=jnp.float32)


def _proj_residual(x, a, w):
    T, D = x.shape
    k = a.shape[1]
    tb = min(512, T)
    assert T % tb == 0
    return pl.pallas_call(
        _proj_residual_kernel,
        grid=(T // tb,),
        in_specs=[pl.BlockSpec((tb, D), lambda i: (i, 0)), pl.BlockSpec((tb, k), lambda i: (i, 0)),
                  pl.BlockSpec((k, D), lambda i: (0, 0))],
        out_specs=pl.BlockSpec((tb, D), lambda i: (i, 0)),
        out_shape=jax.ShapeDtypeStruct((T, D), jnp.float32),
        compiler_params=pltpu.CompilerParams(dimension_semantics=("arbitrary",)),
        name="proj_residual",
    )(x, a, w.astype(MXU_DTYPE))


MOE_ROWS = 256
ROUTE_COLS = LANES


def _moe_route_kernel(x_ref, g_ref, w_ref, b_ref, tri_ref, h_ref, meta_ref, wt_ref, cnt_ref, carry_sc):
    f32 = jnp.float32
    step = pl.program_id(0)

    @pl.when(step == 0)
    def _():
        carry_sc[...] = jnp.zeros(carry_sc.shape, f32)

    x = x_ref[...]
    h = x * lax.rsqrt(jnp.mean(x * x, axis=-1, keepdims=True) + EPS) * g_ref[...]
    h_ref[...] = h
    logits = jnp.dot(h.astype(MXU_DTYPE), w_ref[...], preferred_element_type=f32) + b_ref[...]
    tb = logits.shape[0]
    col = lax.broadcasted_iota(jnp.int32, (tb, ROUTE_COLS), 1)

    def first_max(vals):
        best = jnp.max(vals, axis=1, keepdims=True)
        return best, jnp.min(jnp.where(vals == best, col, ROUTE_COLS), axis=1, keepdims=True)

    lg = jnp.where(col < N_GROUPS, logits, -jnp.inf)
    g_best, grp = first_max(lg)
    g_w = 1.0 / jnp.sum(jnp.exp(lg - g_best), axis=1, keepdims=True)
    lo = N_GROUPS + EXPERTS_PER_GROUP * grp
    le = jnp.where((col >= lo) & (col < lo + EXPERTS_PER_GROUP), logits, -jnp.inf)
    v0, c0 = first_max(le)
    v1, c1 = first_max(jnp.where(col == c0, -jnp.inf, le))
    e1 = jnp.exp(v1 - v0)
    w0 = g_w / (1.0 + e1)
    w1 = g_w * e1 / (1.0 + e1)
    chosen = (col == c0) | (col == c1)
    before = jnp.dot(tri_ref[...], jnp.where(chosen, 1.0, 0.0).astype(MXU_DTYPE),
                     preferred_element_type=f32) + carry_sc[...]
    r0 = jnp.sum(jnp.where(col == c0, before, 0.0), axis=1, keepdims=True).astype(jnp.int32)
    r1 = jnp.sum(jnp.where(col == c1, before, 0.0), axis=1, keepdims=True).astype(jnp.int32)
    carry_sc[...] += jnp.sum(jnp.where(chosen, 1.0, 0.0), axis=0, keepdims=True)
    cnt_ref[...] = carry_sc[...]
    meta_ref[...] = jnp.where(col == 0, c0 - N_GROUPS, jnp.where(col == 1, c1 - N_GROUPS,
                              jnp.where(col == 2, r0, jnp.where(col == 3, r1, 0))))
    wt_ref[...] = jnp.where(col == 0, w0, jnp.where(col == 1, w1, 0.0))


def _moe_dispatch_kernel(dest_ref, h_ref, xbuf_in, xbuf_ref, sem):
    del xbuf_in
    tb = h_ref.shape[0]

    def row_copy(r, k):
        return pltpu.make_async_copy(h_ref.at[pl.ds(r, 1)], xbuf_ref.at[pl.ds(dest_ref[0, 0, 2 * r + k], 1)], sem)

    def start(r, c):
        row_copy(r, 0).start()
        row_copy(r, 1).start()
        return c

    def wait(r, c):
        row_copy(r, 0).wait()
        row_copy(r, 1).wait()
        return c

    lax.fori_loop(0, tb, start, 0)
    lax.fori_loop(0, tb, wait, 0)


def _moe_expert_kernel(blk_e_ref, nblk_ref, x_ref, wgu_ref, wdn_ref, y_ref, wgu_sc, wdn_sc):
    i = pl.program_id(0)
    f32 = jnp.float32

    @pl.when(i < nblk_ref[0])
    def _():
        changed = jnp.logical_or(i == 0, blk_e_ref[i] != blk_e_ref[jnp.maximum(i - 1, 0)])

        @pl.when(changed)
        def _():
            wgu_sc[...] = wgu_ref[0].astype(MXU_DTYPE)
            wdn_sc[...] = wdn_ref[0].astype(MXU_DTYPE)

        gu = jnp.dot(x_ref[...].astype(MXU_DTYPE), wgu_sc[...], preferred_element_type=f32)
        gate = gu[:, :D_EXPERT]
        act = gate * (1.0 / (1.0 + jnp.exp(-gate))) * gu[:, D_EXPERT:]
        y_ref[...] = jnp.dot(act.astype(MXU_DTYPE), wdn_sc[...], preferred_element_type=f32)

    @pl.when(i >= nblk_ref[0])
    def _():
        y_ref[...] = jnp.zeros(y_ref.shape, f32)


def _moe_combine_kernel(dest_ref, x_ref, wt_ref, ybuf_ref, o_ref, rows_sc, sem):
    tb = x_ref.shape[0]

    def row_copy(r, k):
        return pltpu.make_async_copy(ybuf_ref.at[pl.ds(dest_ref[0, 0, 2 * r + k], 1)],
                                     rows_sc.at[k, pl.ds(r, 1)], sem)

    def start(r, c):
        row_copy(r, 0).start()
        row_copy(r, 1).start()
        return c

    def wait(r, c):
        row_copy(r, 0).wait()
        row_copy(r, 1).wait()
        return c

    lax.fori_loop(0, tb, start, 0)
    lax.fori_loop(0, tb, wait, 0)
    wt = wt_ref[...]
    o_ref[...] = x_ref[...] + (wt[:, 0:1] * rows_sc[0] + wt[:, 1:2] * rows_sc[1])


def _hier_moe_residual(x, ln_g, w_grp, b_grp, w_exp, b_exp, w_gu, w_dn):
    T, D = x.shape
    f32 = jnp.float32
    tb = min(256, T)
    assert T % tb == 0
    nt = T // tb
    pad_cols = ROUTE_COLS - N_GROUPS - N_EXPERTS
    w_r = jnp.pad(jnp.concatenate([w_grp, w_exp], axis=1), ((0, 0), (0, pad_cols))).astype(MXU_DTYPE)
    b_r = jnp.pad(jnp.concatenate([b_grp, b_exp]), (0, pad_cols)).reshape(1, ROUTE_COLS)
    tri = jnp.asarray(np.tril(np.ones((tb, tb), np.float32), -1), dtype=MXU_DTYPE)
    row = lambda i: (i, 0)
    fixed = lambda i: (0, 0)
    h, meta, wt, cnt = pl.pallas_call(
        _moe_route_kernel,
        grid=(nt,),
        in_specs=[pl.BlockSpec((tb, D), row), pl.BlockSpec((1, D), fixed),
                  pl.BlockSpec((D, ROUTE_COLS), fixed), pl.BlockSpec((1, ROUTE_COLS), fixed),
                  pl.BlockSpec((tb, tb), fixed)],
        out_specs=[pl.BlockSpec((tb, D), row), pl.BlockSpec((tb, ROUTE_COLS), row),
                   pl.BlockSpec((tb, ROUTE_COLS), row), pl.BlockSpec((1, ROUTE_COLS), fixed)],
        out_shape=[jax.ShapeDtypeStruct((T, D), f32), jax.ShapeDtypeStruct((T, ROUTE_COLS), jnp.int32),
                   jax.ShapeDtypeStruct((T, ROUTE_COLS), f32), jax.ShapeDtypeStruct((1, ROUTE_COLS), f32)],
        scratch_shapes=[pltpu.VMEM((1, ROUTE_COLS), f32)],
        compiler_params=pltpu.CompilerParams(dimension_semantics=("arbitrary",)),
        name="moe_route",
    )(x, ln_g.reshape(1, D), w_r, b_r, tri)

    counts = cnt[0, N_GROUPS:N_GROUPS + N_EXPERTS].astype(jnp.int32)
    padded = (counts + MOE_ROWS - 1) // MOE_ROWS * MOE_ROWS
    pad_end = jnp.cumsum(padded)
    pad_start = pad_end - padded
    n_blocks = -(-(T * TOP_K) // MOE_ROWS) + N_EXPERTS
    n_slots = n_blocks * MOE_ROWS
    dest = pad_start[meta[:, 0:2]] + meta[:, 2:4]
    dest = dest.reshape(nt, 1, 2 * tb)
    starts = jnp.arange(n_blocks, dtype=jnp.int32) * MOE_ROWS
    used = (pad_end[-1] // MOE_ROWS).astype(jnp.int32).reshape(1)
    blk_start = jnp.minimum(starts, pad_end[-1] - 1)
    blk_e = jnp.minimum(jnp.sum(pad_end[None, :] <= blk_start[:, None], axis=1), N_EXPERTS - 1).astype(jnp.int32)

    dest_spec = pl.BlockSpec((1, 1, 2 * tb), lambda i: (i, 0, 0), memory_space=pltpu.SMEM)
    xbuf = pl.pallas_call(
        _moe_dispatch_kernel,
        grid=(nt,),
        in_specs=[dest_spec, pl.BlockSpec((tb, D), row), pl.BlockSpec(memory_space=pl.ANY)],
        out_specs=pl.BlockSpec(memory_space=pl.ANY),
        out_shape=jax.ShapeDtypeStruct((n_slots, D), f32),
        scratch_shapes=[pltpu.SemaphoreType.DMA(())],
        input_output_aliases={2: 0},
        compiler_params=pltpu.CompilerParams(dimension_semantics=("arbitrary",)),
        name="moe_dispatch",
    )(dest, h, jnp.zeros((n_slots, D), f32))

    ybuf = pl.pallas_call(
        _moe_expert_kernel,
        grid_spec=pltpu.PrefetchScalarGridSpec(
            num_scalar_prefetch=2,
            grid=(n_blocks,),
            in_specs=[pl.BlockSpec((MOE_ROWS, D), lambda i, be, nb: (i, 0)),
                      pl.BlockSpec((1, D, 2 * D_EXPERT), lambda i, be, nb: (be[i], 0, 0)),
                      pl.BlockSpec((1, D_EXPERT, D), lambda i, be, nb: (be[i], 0, 0))],
            out_specs=pl.BlockSpec((MOE_ROWS, D), lambda i, be, nb: (i, 0)),
            scratch_shapes=[pltpu.VMEM((D, 2 * D_EXPERT), MXU_DTYPE), pltpu.VMEM((D_EXPERT, D), MXU_DTYPE)]),
        out_shape=jax.ShapeDtypeStruct((n_slots, D), f32),
        compiler_params=pltpu.CompilerParams(dimension_semantics=("arbitrary",),
                                             vmem_limit_bytes=48 * 1024 * 1024),
        name="moe_experts",
    )(blk_e, used, xbuf, w_gu, w_dn)

    return pl.pallas_call(
        _moe_combine_kernel,
        grid=(nt,),
        in_specs=[dest_spec, pl.BlockSpec((tb, D), row), pl.BlockSpec((tb, ROUTE_COLS), row),
                  pl.BlockSpec(memory_space=pl.ANY)],
        out_specs=pl.BlockSpec((tb, D), row),
        out_shape=jax.ShapeDtypeStruct((T, D), f32),
        scratch_shapes=[pltpu.VMEM((TOP_K, tb, D), f32), pltpu.SemaphoreType.DMA(())],
        compiler_params=pltpu.CompilerParams(dimension_semantics=("arbitrary",)),
        name="moe_combine",
    )(dest, x, wt, ybuf)


def _gelu_tanh(x):
    return 0.5 * x * (1.0 + jnp.tanh(math.sqrt(2.0 / math.pi) * (x + 0.044715 * (x * x * x))))


def _gmlp_kernel(x_ref, g_ref, win_ref, bin_ref, lng_ref, lnb_ref, ws_ref, bs_ref, wout_ref,
                 o_ref, v_ref, *, single_position):
    f32 = jnp.float32
    x = x_ref[...]
    rows = x.shape[0]
    h = x * lax.rsqrt(jnp.mean(x * x, axis=-1, keepdims=True) + EPS) * g_ref[...]
    z = _gelu_tanh(jnp.dot(h.astype(MXU_DTYPE), win_ref[...], preferred_element_type=f32) + bin_ref[...])
    u = z[:, :GM_WIDTH]
    v = z[:, GM_WIDTH:]
    mu = jnp.mean(v, axis=-1, keepdims=True)
    var = jnp.mean(jnp.square(v - mu), axis=-1, keepdims=True)
    v = (v - mu) * lax.rsqrt(var + EPS) * lng_ref[...] + lnb_ref[...]
    v_ref[0] = v
    vb = v.astype(MXU_DTYPE)
    if single_position:
        s = ws_ref[...].astype(f32) * vb.astype(f32) + bs_ref[...]
    else:
        parts = []
        for c in range(rows // CHUNK):
            vc = vb[c * CHUNK:(c + 1) * CHUNK]
            parts.append(jnp.concatenate(
                [jnp.dot(ws_ref[g], vc[:, g * GM_GROUP_DIM:(g + 1) * GM_GROUP_DIM], preferred_element_type=f32)
                 + bs_ref[g] for g in range(GM_GROUPS)], axis=1))
        s = jnp.concatenate(parts, axis=0)
    y = jnp.dot((u * s).astype(MXU_DTYPE), wout_ref[...], preferred_element_type=f32)
    o_ref[...] = x + y


def _gmlp_residual(x, ln_g, w_in, b_in, ln2_g, ln2_b, w_s, b_s, w_out, seq):
    T, D = x.shape
    f32 = jnp.float32
    cd = MXU_DTYPE
    single = seq == 1
    tb = T if single else 2 * CHUNK
    assert T % tb == 0 and (single or seq % tb == 0)
    steps_per_seq = 1 if single else seq // tb
    if single:
        ws = jnp.repeat(w_s[:, 0, 0], GM_GROUP_DIM).reshape(1, GM_WIDTH).astype(cd)
        bs = jnp.repeat(b_s[:, 0], GM_GROUP_DIM).reshape(1, GM_WIDTH)
        ws_spec = pl.BlockSpec((1, GM_WIDTH), lambda i: (0, 0))
        bs_spec = pl.BlockSpec((1, GM_WIDTH), lambda i: (0, 0))
    else:
        causal = np.tril(np.ones((CHUNK, CHUNK), bool))
        ws = jnp.where(causal[None], w_s, 0).astype(cd)
        bs = jnp.broadcast_to(b_s[:, :, None], (GM_GROUPS, CHUNK, GM_GROUP_DIM))
        ws_spec = pl.BlockSpec((GM_GROUPS, CHUNK, CHUNK), lambda i: (0, 0, 0))
        bs_spec = pl.BlockSpec((GM_GROUPS, CHUNK, GM_GROUP_DIM), lambda i: (0, 0, 0))
    fixed = lambda i: (0, 0)
    row = lambda i: (i, 0)
    return pl.pallas_call(
        functools.partial(_gmlp_kernel, single_position=single),
        grid=(T // tb,),
        in_specs=[pl.BlockSpec((tb, D), row), pl.BlockSpec((1, D), fixed),
                  pl.BlockSpec((D, 2 * GM_WIDTH), fixed), pl.BlockSpec((1, 2 * GM_WIDTH), fixed),
                  pl.BlockSpec((1, GM_WIDTH), fixed), pl.BlockSpec((1, GM_WIDTH), fixed),
                  ws_spec, bs_spec, pl.BlockSpec((GM_WIDTH, D), fixed)],
        out_specs=[pl.BlockSpec((tb, D), row), pl.BlockSpec((1, tb, GM_WIDTH), lambda i: (i // steps_per_seq, 0, 0))],
        out_shape=[jax.ShapeDtypeStruct((T, D), f32),
                   jax.ShapeDtypeStruct((T // (tb * steps_per_seq), tb, GM_WIDTH), f32)],
        compiler_params=pltpu.CompilerParams(dimension_semantics=("arbitrary",),
                                             vmem_limit_bytes=56 * 1024 * 1024),
        name="gmlp",
    )(x, ln_g.reshape(1, D), w_in.astype(cd), b_in.reshape(1, -1), ln2_g.reshape(1, -1), ln2_b.reshape(1, -1),
      ws, bs, w_out.astype(cd))


def rmsnorm(x, g):
    xf = x.astype(jnp.float32)
    y = xf * lax.rsqrt(jnp.mean(xf * xf, axis=-1, keepdims=True) + EPS)
    return (y * g.astype(jnp.float32)).astype(x.dtype)


def layernorm(x, g, b):
    xf = x.astype(jnp.float32)
    mu = jnp.mean(xf, axis=-1, keepdims=True)
    var = jnp.mean(jnp.square(xf - mu), axis=-1, keepdims=True)
    return ((xf - mu) * lax.rsqrt(var + EPS) * g.astype(jnp.float32) + b.astype(jnp.float32)).astype(x.dtype)


def rel_bucket(n):
    n = jnp.maximum(n, 0)
    max_exact = N_BUCKETS // 2
    nf = jnp.maximum(n, 1).astype(jnp.float32)
    large = max_exact + (jnp.log(nf / max_exact) / math.log(MAX_DISTANCE / max_exact)
                         * (N_BUCKETS - max_exact)).astype(jnp.int32)
    large = jnp.minimum(large, N_BUCKETS - 1)
    return jnp.where(n < max_exact, n, large)


def head_bias(rel, rel_bias):
    q, k = rel.shape
    b = rel_bias[rel_bucket(rel)]
    return b.reshape(q, k, KV_HEADS, Q_PER_KV).transpose(2, 3, 0, 1)


def masked_softmax(logits, mask):
    z = jnp.where(mask, logits.astype(jnp.float32), -1e30)
    return jax.nn.softmax(z, axis=-1) * mask


def nsa_project(xn, w_in, q_gain, k_gain):
    B, T, _ = xn.shape
    hq = N_HEADS * HEAD_DIM
    hkv = 6 * KV_HEADS * HEAD_DIM
    h = xn @ w_in
    q = rmsnorm(h[..., :hq].reshape(B, T, N_HEADS, HEAD_DIM), q_gain)
    kv = h[..., hq:hq + hkv].reshape(B, T, 6, KV_HEADS, HEAD_DIM)
    gates = jax.nn.sigmoid(h[..., hq + hkv:].astype(jnp.float32)).reshape(B, T, N_HEADS, 3)
    k_sel = rmsnorm(kv[:, :, 2], k_gain[1])
    k_win = rmsnorm(kv[:, :, 4], k_gain[2])
    return q, kv[:, :, 0], kv[:, :, 1], k_sel, kv[:, :, 3], k_win, kv[:, :, 5], gates


def compress(rows, pe, w1, w2):
    B, T = rows.shape[:2]
    n_ch = T // CMP_STRIDE
    c = rows[:, :n_ch * CMP_STRIDE].reshape(B, n_ch, CMP_STRIDE, KV_HEADS, HEAD_DIM)
    blk = jnp.concatenate([c[:, :-1], c[:, 1:]], axis=2) + pe[None, None, :, None, :]
    h = jax.nn.gelu(jnp.einsum('bnlkd,ldf->bnkf', blk, w1.reshape(CMP_LEN, HEAD_DIM, CMP_HID)))
    return jnp.einsum('bnkf,fd->bnkd', h, w2)


def compressed_kv(k_raw, v_raw, k_gain0, pe, w1, w2):
    kc = rmsnorm(compress(k_raw, pe[0], w1[0], w2[0]), k_gain0)
    vc = compress(v_raw, pe[1], w1[1], w2[1])
    end = jnp.arange(kc.shape[1]) * CMP_STRIDE + (CMP_LEN - 1)
    return kc, vc, end


def nsa_core(q, q_pos, kc, vc, kc_end, ks, vs, kw, vw, kw_pos, gates, rel_bias):
    B, Q = q.shape[:2]
    nc = kc.shape[1]
    scale = HEAD_DIM ** -0.5
    qh = q.reshape(B, Q, KV_HEADS, Q_PER_KV, HEAD_DIM)
    lc = jnp.einsum('bqkgd,bnkd->bkgqn', qh, kc) * scale + head_bias(q_pos[:, None] - kc_end[None, :], rel_bias)
    pc = masked_softmax(lc, kc_end[None, :] <= q_pos[:, None])
    oc = jnp.einsum('bkgqn,bnkd->bqkgd', pc.astype(vc.dtype), vc)
    nb = ks.shape[1] // SEL_BLOCK
    imp = jnp.pad(pc.sum(axis=2), ((0, 0), (0, 0), (0, 0), (0, nb * CMP_PER_SEL - nc)))
    imp = imp.reshape(B, KV_HEADS, Q, nb, CMP_PER_SEL).sum(-1)
    blk = jnp.arange(nb)
    d = (q_pos // SEL_BLOCK)[:, None] - blk[None, :]
    forced = (blk[None, :] == 0) | ((d >= 0) & (d < N_LOCAL))
    score = jnp.where(forced, jnp.inf, jnp.where(d >= 0, imp, -jnp.inf))
    n_sel = min(N_SEL, nb)
    _, idx = lax.top_k(score, n_sel)
    bi = jnp.arange(B)[:, None, None, None]
    ki = jnp.arange(KV_HEADS)[None, :, None, None]
    ksb = ks.reshape(B, nb, SEL_BLOCK, KV_HEADS, HEAD_DIM).transpose(0, 3, 1, 2, 4)
    vsb = vs.reshape(B, nb, SEL_BLOCK, KV_HEADS, HEAD_DIM).transpose(0, 3, 1, 2, 4)
    kg = ksb[bi, ki, idx].reshape(B, KV_HEADS, Q, n_sel * SEL_BLOCK, HEAD_DIM)
    vg = vsb[bi, ki, idx].reshape(B, KV_HEADS, Q, n_sel * SEL_BLOCK, HEAD_DIM)
    kpos = (idx[..., None] * SEL_BLOCK + jnp.arange(SEL_BLOCK)).reshape(B, KV_HEADS, Q, n_sel * SEL_BLOCK)
    rel_s = q_pos[None, None, :, None] - kpos
    tb = rel_bias.reshape(N_BUCKETS, KV_HEADS, Q_PER_KV)
    bs = tb[rel_bucket(rel_s), ki].transpose(0, 1, 4, 2, 3)
    ls = jnp.einsum('bqkgd,bkqjd->bkgqj', qh, kg) * scale + bs
    ps = masked_softmax(ls, (rel_s >= 0)[:, :, None])
    os_ = jnp.einsum('bkgqj,bkqjd->bqkgd', ps.astype(vg.dtype), vg)
    rel_w = q_pos[:, None] - kw_pos[None, :]
    mw = (rel_w >= 0) & (rel_w <= WINDOW) & (kw_pos[None, :] >= 0)
    lw = jnp.einsum('bqkgd,bwkd->bkgqw', qh, kw) * scale + head_bias(rel_w, rel_bias)
    pw = masked_softmax(lw, mw)
    ow = jnp.einsum('bkgqw,bwkd->bqkgd', pw.astype(vw.dtype), vw)
    g = gates.reshape(B, Q, KV_HEADS, Q_PER_KV, 3).astype(oc.dtype)
    o = g[..., 0:1] * oc + g[..., 1:2] * os_ + g[..., 2:3] * ow
    return o.reshape(B, Q, N_HEADS * HEAD_DIM)


def nsa_prompt(xn, w_in, w_out, q_gain, k_gain, pe, w1, w2, rel_bias):
    B, S, _ = xn.shape
    q, kcr, vcr, ks, vs, kw, vw, gates = nsa_project(xn, w_in, q_gain, k_gain)
    kc, vc, kc_end = compressed_kv(kcr, vcr, k_gain[0], pe, w1, w2)
    o = _nsa_attention(q, kc, vc, ks, vs, kw, vw, gates, rel_bias)
    w_keep = min(WINDOW, S)
    new_rows = jnp.stack([kcr, vcr, ks, vs], axis=2)
    new_win = jnp.stack([kw[:, S - w_keep:], vw[:, S - w_keep:]], axis=2)
    y = _mm(o.reshape(B * S, -1), w_out).reshape(B, S, -1)
    return y, new_rows, new_win


def nsa_sample(xn, past_rows, win_buf, w_in, w_out, q_gain, k_gain, pe, w1, w2, rel_bias):
    B, L, _ = xn.shape
    P = past_rows.shape[1]
    q, kcr, vcr, ks, vs, kw, vw, gates = nsa_project(xn, w_in, q_gain, k_gain)
    new_rows = jnp.stack([kcr, vcr, ks, vs], axis=2)
    rows = jnp.concatenate([past_rows, new_rows], axis=1)
    kc, vc, kc_end = compressed_kv(rows[:, :, 0], rows[:, :, 1], k_gain[0], pe, w1, w2)
    T = P + L
    tp = -(-T // SEL_BLOCK) * SEL_BLOCK
    pad = ((0, 0), (0, tp - T), (0, 0), (0, 0))
    ks_all = jnp.pad(rows[:, :, 2], pad)
    vs_all = jnp.pad(rows[:, :, 3], pad)
    win_all = jnp.concatenate([win_buf, jnp.stack([kw, vw], axis=2)], axis=1)
    wb = win_buf.shape[1]
    kw_pos = P - wb + jnp.arange(wb + L)
    q_pos = P + jnp.arange(L)
    o = nsa_core(q, q_pos, kc, vc, kc_end, ks_all, vs_all, win_all[:, :, 0], win_all[:, :, 1],
                 kw_pos, gates, rel_bias)
    w_keep = min(WINDOW, wb + L)
    return o @ w_out, new_rows, win_all[:, wb + L - w_keep:]


def gmlp_mixer(xn, w_in, b_in, ln_g, ln_b, w_s, b_s, w_out):
    B, L, _ = xn.shape
    z = jax.nn.gelu(xn @ w_in + b_in)
    u = z[..., :GM_WIDTH]
    v = layernorm(z[..., GM_WIDTH:], ln_g, ln_b)
    lp = -(-L // CHUNK) * CHUNK
    vp = jnp.pad(v, ((0, 0), (0, lp - L), (0, 0))).reshape(B, lp // CHUNK, CHUNK, GM_GROUPS, GM_GROUP_DIM)
    causal = jnp.tril(jnp.ones((CHUNK, CHUNK), dtype=bool))
    ws = jnp.where(causal[None], w_s, 0)
    s = jnp.einsum('gts,bcsgd->bctgd', ws, vp) + b_s.T[None, None, :, :, None]
    s = s.reshape(B, lp, GM_WIDTH)[:, :L]
    start = ((L - 1) // CHUNK) * CHUNK
    return (u * s) @ w_out, v[:, start:]


def hier_moe(x, w_grp, b_grp, w_exp, b_exp, w_gu, w_dn):
    T, D = x.shape
    pg = jax.nn.softmax((x @ w_grp + b_grp).astype(jnp.float32), axis=-1)
    grp = jnp.argmax(pg, axis=-1)
    g_w = jnp.take_along_axis(pg, grp[:, None], axis=-1)
    le = (x @ w_exp + b_exp).astype(jnp.float32).reshape(T, N_GROUPS, EXPERTS_PER_GROUP)
    le = jnp.take_along_axis(le, grp[:, None, None], axis=1)[:, 0]
    top_v, top_i = lax.top_k(le, TOP_K)
    wts = (jax.nn.softmax(top_v, axis=-1) * g_w).reshape(-1)
    eid = (grp[:, None] * EXPERTS_PER_GROUP + top_i).reshape(-1)
    n = T * TOP_K
    order = jnp.argsort(eid)
    e_s = eid[order]
    tok_s = order // TOP_K
    w_s = wts[order]
    counts = jnp.bincount(eid, length=N_EXPERTS)
    padded = (counts + MOE_BLOCK - 1) // MOE_BLOCK * MOE_BLOCK
    pad_end = jnp.cumsum(padded)
    pad_start = pad_end - padded
    start = jnp.cumsum(counts) - counts
    dest = pad_start[e_s] + jnp.arange(n) - start[e_s]
    n_blocks = -(-n // MOE_BLOCK) + N_EXPERTS
    xbuf = jnp.zeros((n_blocks * MOE_BLOCK, D), x.dtype).at[dest].set(x[tok_s])
    blk_e = jnp.minimum(jnp.searchsorted(pad_end, jnp.arange(n_blocks) * MOE_BLOCK, side='right'), N_EXPERTS - 1)

    def expert(args):
        xb, e = args
        gu = xb @ w_gu[e]
        return (jax.nn.silu(gu[:, :D_EXPERT]) * gu[:, D_EXPERT:]) @ w_dn[e]

    ybuf = lax.map(expert, (xbuf.reshape(n_blocks, MOE_BLOCK, D), blk_e)).reshape(-1, D)
    return jax.ops.segment_sum(ybuf[dest] * w_s[:, None].astype(x.dtype), tok_s, num_segments=T)


def kernel(x_prompt, x_sample, cache_nsa_kv, state_win_kv, page_table, rel_bias, ln_mix, ln_ffn,
           nsa_w_in, nsa_w_out, nsa_q_gain, nsa_k_gain, cmp_pe, cmp_w1, cmp_w2,
           gm_w_in, gm_b_in, gm_ln_g, gm_ln_b, gm_w_s, gm_b_s, gm_w_out,
           moe_w_grp, moe_b_grp, moe_w_exp, moe_b_exp, moe_w_gu, moe_w_dn):
    xp = x_prompt
    xs = x_sample
    kv_p, kv_s, win_p, win_s, gv_p, gv_s = [], [], [], [], [], []
    for i in range(DEPTH):
        a = i // N_MIXERS
        if i % N_MIXERS == 0:
            nsa = (ln_mix[i], nsa_w_in[a], nsa_w_out[a], nsa_q_gain[a], nsa_k_gain[a],
                   cmp_pe[a], cmp_w1[a], cmp_w2[a], rel_bias)
            xp, rp, wp = _nsa_prompt_layer(xp, *nsa)
            cmp_rows, sel_kvt = _page_gather(cache_nsa_kv, page_table, a)
            xs, rs, ws = _nsa_decode_layer(xs, cmp_rows, sel_kvt, state_win_kv[a], *nsa)
            kv_p.append(rp)
            kv_s.append(rs)
            win_p.append(wp)
            win_s.append(ws)
        else:
            gm = (ln_mix[i], gm_w_in[a], gm_b_in[a], gm_ln_g[a], gm_ln_b[a], gm_w_s[a], gm_b_s[a], gm_w_out[a])
            bp, sp = xp.shape[:2]
            bs_, ss = xs.shape[:2]
            xp2, vp = _gmlp_residual(xp.reshape(-1, D_MODEL), *gm, seq=sp)
            xs2, vs = _gmlp_residual(xs.reshape(-1, D_MODEL), *gm, seq=ss)
            xp = xp2.reshape(xp.shape)
            xs = xs2.reshape(xs.shape)
            start = ((sp - 1) // CHUNK) * CHUNK
            gv_p.append(vp[:, vp.shape[1] - (sp - start):])
            gv_s.append(vs.reshape(bs_, ss, GM_WIDTH))
        moe = (ln_ffn[i], moe_w_grp[i], moe_b_grp[i], moe_w_exp[i], moe_b_exp[i], moe_w_gu[i], moe_w_dn[i])
        xp = _hier_moe_residual(xp.reshape(-1, D_MODEL), *moe).reshape(xp.shape)
        xs = _hier_moe_residual(xs.reshape(-1, D_MODEL), *moe).reshape(xs.shape)
    new_kv_prompt = jnp.stack(kv_p, axis=2)
    new_kv_sample = jnp.stack(kv_s, axis=2)
    new_win_prompt = jnp.stack(win_p, axis=0)
    new_win_sample = jnp.stack(win_s, axis=0)
    new_gm_v_prompt = jnp.stack(gv_p, axis=0)
    new_gm_v_sample = jnp.stack(gv_s, axis=0)
    return (xp, xs, new_kv_prompt, new_kv_sample, new_win_prompt, new_win_sample, new_gm_v_prompt, new_gm_v_sample)
```

```python
import functools
import math

import jax
import jax.numpy as jnp
import numpy as np
from jax import lax
from jax.experimental import pallas as pl
from jax.experimental.pallas import tpu as pltpu

D_MODEL = 1024
PAGE_SIZE = 128
DEPTH = 2
N_MIXERS = 2
N_HEADS = 16
HEAD_DIM = 64
KV_HEADS = 4
Q_PER_KV = N_HEADS // KV_HEADS
CMP_LEN = 32
CMP_STRIDE = 16
CMP_HID = 64
SEL_BLOCK = 64
CMP_PER_SEL = SEL_BLOCK // CMP_STRIDE
N_SEL = 16
N_LOCAL = 2
WINDOW = 512
Q_BLOCK = 128
N_BUCKETS = 32
MAX_DISTANCE = 128
CHUNK = 128
GM_WIDTH = 2048
GM_GROUPS = 8
GM_GROUP_DIM = GM_WIDTH // GM_GROUPS
N_GROUPS = 4
EXPERTS_PER_GROUP = 8
N_EXPERTS = N_GROUPS * EXPERTS_PER_GROUP
TOP_K = 2
D_EXPERT = 512
MOE_BLOCK = 128
EPS = 1e-6


def _mm_kernel(x_ref, w_ref, o_ref):
    o_ref[...] = jnp.dot(x_ref[...].astype(jnp.bfloat16), w_ref[...].astype(jnp.bfloat16),
                         preferred_element_type=jnp.float32)


def _mm(x, w, tm=512, tn=512):
    m, k = x.shape
    n = w.shape[1]
    tm = min(tm, m)
    tn = min(tn, n)
    return pl.pallas_call(
        _mm_kernel,
        grid=(m // tm, n // tn),
        in_specs=[pl.BlockSpec((tm, k), lambda i, j: (i, 0)),
                  pl.BlockSpec((k, tn), lambda i, j: (0, j))],
        out_specs=pl.BlockSpec((tm, tn), lambda i, j: (i, j)),
        out_shape=jax.ShapeDtypeStruct((m, n), jnp.float32),
        name="mm",
    )(x, w)


NEG = -1e30
MXU_DTYPE = jnp.bfloat16
TQ = 256
TK = 256
GROUP_CHUNKS = 3
LANES = 128
BIAS_SPAN = 128
LOG2E = 1.4426950408889634


def _bucket_of_distance():
    d = np.arange(BIAS_SPAN)
    max_exact = N_BUCKETS // 2
    nf = np.maximum(d, 1).astype(np.float32)
    large = max_exact + (np.log(nf / np.float32(max_exact)) / np.float32(math.log(MAX_DISTANCE / max_exact))
                         * np.float32(N_BUCKETS - max_exact)).astype(np.int32)
    large = np.minimum(large, N_BUCKETS - 1)
    return np.where(d < max_exact, d, large).astype(np.int32)


def _attn_bias_tiles(rel_bias, seq):
    fd = rel_bias[_bucket_of_distance()].T
    far = fd[:, BIAS_SPAN - 1]
    fd = fd - far[:, None]

    def by_distance(d):
        return jnp.where(d >= 0, LOG2E * jnp.take(fd, np.clip(d, 0, BIAS_SPAN - 1), axis=1), NEG)

    def toeplitz(first_distance):
        c = np.arange(2 * TK)
        g = by_distance(first_distance - np.where(c < TK, c, c - 2 * TK))
        flat = jnp.tile(g, (1, TQ))[:, :TQ * (2 * TK - 1)]
        return flat.reshape(-1, TQ, 2 * TK - 1)[:, :, :TK]

    b0 = toeplitz(0)
    b1 = toeplitz(TQ)
    i = np.arange(TQ)[:, None]
    j = np.arange(TK)[None, :]
    w2 = np.where(2 * TQ + i - j <= WINDOW, 0.0, NEG).astype(np.float32)
    nc = seq // CMP_STRIDE
    span = -(-(BIAS_SPAN + TQ) // CMP_STRIDE)
    r = np.arange(nc - span, nc + span)[:, None]
    dc = np.arange(TQ)[None, :] - CMP_STRIDE * (r - nc) - (CMP_LEN - 1)
    n_heads = fd.shape[0]
    pat = jnp.concatenate([jnp.zeros((n_heads, nc - span, TQ), jnp.float32), by_distance(dc),
                           jnp.full((n_heads, nc - span, TQ), NEG, jnp.float32)], axis=1)
    return b0, b1, jnp.asarray(w2), pat


def _nsa_attn_kernel(q_ref, kc_ref, vct_ref, kat_ref, vsa_ref, kwt_ref, vwa_ref, g_ref,
                     b0_ref, b1_ref, w2_ref, pat_ref, o_ref, m_sc, acc_sc, imp_sc, qa_sc, p_sc, a_sc, *, nc, nb):
    f32 = jnp.float32
    cd = MXU_DTYPE
    qb = pl.program_id(2)

    r0 = pl.multiple_of(nc - (TQ // CMP_STRIDE) * qb, TQ // CMP_STRIDE)
    t_row = qb * TQ + lax.broadcasted_iota(jnp.int32, (1, TQ), 1)
    has_cmp = t_row >= CMP_LEN - 1
    for h in range(Q_PER_KV):
        st = lax.dot_general(kc_ref[0, 0], q_ref[0, h], (((1,), (1,)), ((), ())),
                             preferred_element_type=f32)
        st = st + pat_ref[h, pl.ds(r0, nc), :]
        e = jnp.exp2(st - jnp.max(st, axis=0, keepdims=True))
        inv = jnp.where(has_cmp, 1.0 / jnp.sum(e, axis=0, keepdims=True), 0.0)
        pt = e * inv
        for c in range(TQ // LANES):
            part = pt[:, c * LANES:(c + 1) * LANES]
            if h == 0:
                imp_sc[c] = part
            else:
                imp_sc[c] += part
        oct_h = jnp.dot(vct_ref[0, 0], pt.astype(MXU_DTYPE), preferred_element_type=f32)
        o_ref[0, :, h * HEAD_DIM:(h + 1) * HEAD_DIM] = g_ref[0][:, 3 * h:3 * h + 1] * oct_h.T

    imp = jnp.concatenate(
        [sum(imp_sc[c, pl.ds(r, nb, stride=CMP_PER_SEL), :] for r in range(CMP_PER_SEL))
         for c in range(TQ // LANES)], axis=1)
    blk = lax.broadcasted_iota(jnp.int32, (nb, TQ), 0)
    t_blk = (qb * TQ + lax.broadcasted_iota(jnp.int32, (nb, TQ), 1)) // SEL_BLOCK
    dd = t_blk - blk
    forced = (blk == 0) | ((dd >= 0) & (dd < N_LOCAL))
    score = jnp.where(forced, jnp.inf, jnp.where(dd >= 0, imp, -jnp.inf))
    pen_t = jnp.full((nb, TQ), NEG, f32)
    for _ in range(min(N_SEL, nb)):
        best = jnp.max(score, axis=0, keepdims=True)
        first = jnp.min(jnp.where(score == best, blk, nb), axis=0, keepdims=True)
        hit = blk == first
        pen_t = jnp.where(hit, 0.0, pen_t)
        score = jnp.where(hit, -jnp.inf, score)
    pen = pen_t.T.astype(MXU_DTYPE)

    pad = jnp.zeros((TQ, qa_sc.shape[2] - nb - HEAD_DIM), cd)
    for h in range(Q_PER_KV):
        qa_sc[h] = jnp.concatenate([pen, q_ref[0, h], pad], axis=1)

    def reset():
        m_sc[...] = jnp.full(m_sc.shape, NEG, f32)
        acc_sc[...] = jnp.zeros(acc_sc.shape, f32)

    def run(lhs, k_ref, v_ref, chunks):
        for slot, (c, bias) in enumerate(chunks):
            for h in range(Q_PER_KV):
                s = jnp.dot(lhs(h), k_ref[0, 0, c], preferred_element_type=f32)
                if bias is not None:
                    s = s + bias(h)
                m_prev = m_sc[h]
                m_new = jnp.maximum(m_prev, jnp.max(s, axis=1, keepdims=True))
                p_sc[slot, h] = jnp.exp2(s - jnp.concatenate([m_new] * (TK // LANES), axis=1)).astype(cd)
                a_sc[slot, h] = jnp.exp2(m_prev - m_new)
                m_sc[h] = m_new
        for slot, (c, _) in enumerate(chunks):
            for h in range(Q_PER_KV):
                acc_sc[h] = a_sc[slot, h] * acc_sc[h] + jnp.dot(p_sc[slot, h], v_ref[0, 0, c],
                                                                preferred_element_type=f32)

    def emit(gate_col):
        for h in range(Q_PER_KV):
            acc = acc_sc[h]
            w = g_ref[0][:, 3 * h + gate_col:3 * h + gate_col + 1] / acc[:, HEAD_DIM:HEAD_DIM + 1]
            o_ref[0, :, h * HEAD_DIM:(h + 1) * HEAD_DIM] += w * acc[:, :HEAD_DIM]

    own = lambda h: b0_ref[h]
    prev = lambda h: b1_ref[h]
    edge = lambda h: w2_ref[...]

    sel_lhs = lambda h: qa_sc[h]
    reset()
    n_far = jnp.maximum(qb - 1, 0)

    def far_group(i, carry):
        run(sel_lhs, kat_ref, vsa_ref, [(GROUP_CHUNKS * i + j, None) for j in range(GROUP_CHUNKS)])
        return carry

    lax.fori_loop(0, n_far // GROUP_CHUNKS, far_group, 0)
    for left in range(1, GROUP_CHUNKS):
        @pl.when(n_far % GROUP_CHUNKS == left)
        def _():
            run(sel_lhs, kat_ref, vsa_ref, [(n_far - left + j, None) for j in range(left)])

    @pl.when(qb >= 1)
    def _():
        run(sel_lhs, kat_ref, vsa_ref, [(qb - 1, prev), (qb, own)])

    @pl.when(qb == 0)
    def _():
        run(sel_lhs, kat_ref, vsa_ref, [(qb, own)])

    emit(1)

    win_lhs = lambda h: q_ref[0, h]
    reset()

    @pl.when(qb >= 2)
    def _():
        run(win_lhs, kwt_ref, vwa_ref, [(qb - 2, edge), (qb - 1, prev), (qb, own)])

    @pl.when(qb == 1)
    def _():
        run(win_lhs, kwt_ref, vwa_ref, [(qb - 1, prev), (qb, own)])

    @pl.when(qb == 0)
    def _():
        run(win_lhs, kwt_ref, vwa_ref, [(qb, own)])

    emit(2)


def _nsa_attention(qs, kcp, vct, kat, vsa, kwt, vwa, gt, rel_bias):
    B, _, S, _ = qs.shape
    assert S % TQ == 0 and TQ == TK and WINDOW == 2 * TQ and S // SEL_BLOCK >= N_SEL
    nc = S // CMP_STRIDE
    nb = S // SEL_BLOCK
    nch = S // TK
    ka = kat.shape[3]
    b0, b1, w2, pat = _attn_bias_tiles(rel_bias, S)
    G = Q_PER_KV
    kv_chunks = pl.BlockSpec((1, 1, nch, HEAD_DIM, TK), lambda b, k, i: (b, k, 0, 0, 0))
    ka_chunks = pl.BlockSpec((1, 1, nch, ka, TK), lambda b, k, i: (b, k, 0, 0, 0))
    v_chunks = pl.BlockSpec((1, 1, nch, TK, LANES), lambda b, k, i: (b, k, 0, 0, 0))
    return pl.pallas_call(
        functools.partial(_nsa_attn_kernel, nc=nc, nb=nb),
        grid=(B, KV_HEADS, S // TQ),
        in_specs=[
            pl.BlockSpec((1, G, TQ, HEAD_DIM), lambda b, k, i: (b, k, i, 0)),
            pl.BlockSpec((1, 1, nc, HEAD_DIM), lambda b, k, i: (b, k, 0, 0)),
            pl.BlockSpec((1, 1, HEAD_DIM, nc), lambda b, k, i: (b, k, 0, 0)),
            ka_chunks, v_chunks, kv_chunks, v_chunks,
            pl.BlockSpec((1, TQ, LANES), lambda b, k, i: (b, i, k)),
            pl.BlockSpec((G, TQ, TK), lambda b, k, i: (k, 0, 0)),
            pl.BlockSpec((G, TQ, TK), lambda b, k, i: (k, 0, 0)),
            pl.BlockSpec((TQ, TK), lambda b, k, i: (0, 0)),
            pl.BlockSpec((G, 2 * nc, TQ), lambda b, k, i: (k, 0, 0)),
        ],
        out_specs=pl.BlockSpec((1, TQ, G * HEAD_DIM), lambda b, k, i: (b, i, k)),
        out_shape=jax.ShapeDtypeStruct((B, S, N_HEADS * HEAD_DIM), jnp.float32),
        scratch_shapes=[
            pltpu.VMEM((G, TQ, LANES), jnp.float32),
            pltpu.VMEM((G, TQ, LANES), jnp.float32),
            pltpu.VMEM((TQ // LANES, nc, LANES), jnp.float32),
            pltpu.VMEM((G, TQ, ka), MXU_DTYPE),
            pltpu.VMEM((GROUP_CHUNKS, G, TQ, TK), MXU_DTYPE),
            pltpu.VMEM((GROUP_CHUNKS, G, TQ, LANES), jnp.float32),
        ],
        compiler_params=pltpu.CompilerParams(
            dimension_semantics=("arbitrary", "arbitrary", "arbitrary"),
            vmem_limit_bytes=56 * 1024 * 1024),
        name="nsa_attn",
    )(qs, kcp, vct, kat, vsa, kwt, vwa, gt, b0, b1, w2, pat)


def _nsa_prompt_layer(x, ln_g, w_in, w_out, q_gain, k_gain, pe, w1, w2, rel_bias):
    B, S, D = x.shape
    x2 = x.reshape(B * S, D)
    qs, rows, win, kat, vsa, kwt, vwa, c, gates = _nsa_project(x2, ln_g, w_in, q_gain, k_gain, B, S)
    nc = S // CMP_STRIDE
    cmp = _compress(c.reshape(2, B, nc, CMP_STRIDE * SLOT), pe, w1, w2, k_gain[0])
    cmp = cmp.reshape(2, B, nc, KV_HEADS, HEAD_DIM).astype(MXU_DTYPE)
    gt = gates.reshape(B, S, KV_HEADS * LANES)
    o = _nsa_attention(qs, cmp[0].transpose(0, 2, 1, 3), cmp[1].transpose(0, 2, 3, 1), kat, vsa, kwt, vwa, gt,
                       rel_bias)
    y = _proj_residual(x2, o.reshape(B * S, HQ), w_out)
    return (y.reshape(B, S, D), rows.reshape(B, S, 4, KV_HEADS, HEAD_DIM),
            win.reshape(B, WINDOW, 2, KV_HEADS, HEAD_DIM))


PAGES_PER_STEP = 8


def _page_gather_kernel(pt_ref, *refs):
    del pt_ref
    pages = refs[:PAGES_PER_STEP]
    c_ref, kvt_ref, rows_sc = refs[PAGES_PER_STEP:]
    for p, pg in enumerate(pages):
        pos = slice(p * PAGE_SIZE, (p + 1) * PAGE_SIZE)
        for slot in range(2):
            for pair in range(KV_HEADS // 2):
                tile = pg[0, 0, slot, 2 * pair:2 * pair + 2].reshape(LANES, PAGE_SIZE)
                rows_sc[slot * 2 + pair, pos, :] = tile.T
        for slot in range(2):
            kvt_ref[slot, 0, :, :, pos] = pg[0, 0, 2 + slot].astype(MXU_DTYPE)
    n_chunks = PAGES_PER_STEP * PAGE_SIZE // CMP_STRIDE
    for l in range(CMP_STRIDE):
        for t in range(4):
            lo = l * SLOT + (t % 2) * LANES
            c_ref[t // 2, 0, :, lo:lo + LANES] = rows_sc[t, pl.ds(l, n_chunks, stride=CMP_STRIDE), :].astype(MXU_DTYPE)


def _page_gather(cache, page_table, layer):
    nseq, n_pages = page_table.shape
    assert n_pages % PAGES_PER_STEP == 0 and 2 * LANES == SLOT
    P = n_pages * PAGE_SIZE
    cache_t = cache.transpose(0, 2, 3, 4, 5, 1)
    page_specs = [
        pl.BlockSpec((1, 1, 4, KV_HEADS, HEAD_DIM, PAGE_SIZE),
                     functools.partial(lambda r, n, j, pt: (pt[n, PAGES_PER_STEP * j + r], layer, 0, 0, 0, 0), r))
        for r in range(PAGES_PER_STEP)]
    rows_per_step = PAGES_PER_STEP * PAGE_SIZE
    return pl.pallas_call(
        _page_gather_kernel,
        grid_spec=pltpu.PrefetchScalarGridSpec(
            num_scalar_prefetch=1,
            grid=(nseq, n_pages // PAGES_PER_STEP),
            in_specs=page_specs,
            out_specs=[pl.BlockSpec((2, 1, rows_per_step // CMP_STRIDE, CMP_STRIDE * SLOT),
                                    lambda n, j, pt: (0, n, j, 0)),
                       pl.BlockSpec((2, 1, KV_HEADS, HEAD_DIM, rows_per_step), lambda n, j, pt: (0, n, 0, 0, j))],
            scratch_shapes=[pltpu.VMEM((4, rows_per_step, LANES), jnp.float32)]),
        out_shape=[jax.ShapeDtypeStruct((2, nseq, P // CMP_STRIDE, CMP_STRIDE * SLOT), MXU_DTYPE),
                   jax.ShapeDtypeStruct((2, nseq, KV_HEADS, HEAD_DIM, P), MXU_DTYPE)],
        compiler_params=pltpu.CompilerParams(dimension_semantics=("arbitrary", "arbitrary")),
        name="page_gather",
    )(page_table, *([cache_t] * PAGES_PER_STEP))


def _nsa_decode_kernel(q_ref, kc_ref, vc_ref, kst_ref, vst_ref, win_ref, rows_ref, wnew_ref, g_ref,
                       oh_ref, bc_ref, bs_ref, bw_ref, bn_ref, o_ref, ka_sc, *, nb):
    f32 = jnp.float32
    cd = MXU_DTYPE
    nt = (((1,), (1,)), ((), ()))

    @pl.when(pl.program_id(0) == 0)
    def _():
        ka_sc[0:nb, :] = oh_ref[...]
        ka_sc[nb + HEAD_DIM:, :] = jnp.zeros((ka_sc.shape[0] - nb - HEAD_DIM, ka_sc.shape[1]), cd)

    rows = rows_ref[0]
    wnew = wnew_ref[0]

    def softmax_with_new(s, s_new):
        m = jnp.maximum(jnp.max(s, axis=1, keepdims=True), s_new)
        e = jnp.exp(s - m)
        e_new = jnp.exp(s_new - m)
        return e, e_new, 1.0 / (jnp.sum(e, axis=1, keepdims=True) + e_new)

    def bf(x):
        return x.astype(cd).astype(f32)

    for k in range(KV_HEADS):
        hs = slice(k * Q_PER_KV, (k + 1) * Q_PER_KV)
        lanes = slice(k * HEAD_DIM, (k + 1) * HEAD_DIM)
        q = q_ref[0, hs, :]
        qf = q.astype(f32)
        gates = g_ref[0, hs, :]

        def new_key(row):
            return jnp.sum(qf * bf(row), axis=1, keepdims=True) + bn_ref[hs, :]

        sc = lax.dot_general(q, kc_ref[0, k], nt, preferred_element_type=f32) + bc_ref[hs, :]
        ec = jnp.exp(sc - jnp.max(sc, axis=1, keepdims=True))
        pc = ec / jnp.sum(ec, axis=1, keepdims=True)
        out = gates[:, 0:1] * jnp.dot(pc.astype(cd), vc_ref[0, k], preferred_element_type=f32)

        imp = sum(pc[:, r * nb:(r + 1) * nb] for r in range(CMP_PER_SEL))
        imp = jnp.sum(imp, axis=0, keepdims=True)
        blk = lax.broadcasted_iota(jnp.int32, imp.shape, 1)
        forced = (blk == 0) | (blk == nb - 1)
        score = jnp.where(forced, jnp.inf, imp)
        pen = jnp.full(imp.shape, NEG, f32)
        for _ in range(N_SEL - 1):
            best = jnp.max(score, axis=1, keepdims=True)
            first = jnp.min(jnp.where(score == best, blk, nb), axis=1, keepdims=True)
            hit = blk == first
            pen = jnp.where(hit, 0.0, pen)
            score = jnp.where(hit, -jnp.inf, score)
        ka_sc[nb:nb + HEAD_DIM, :] = kst_ref[0, 0, k]
        qa = jnp.concatenate([jnp.broadcast_to(pen, (Q_PER_KV, nb)).astype(cd), q,
                              jnp.zeros((Q_PER_KV, ka_sc.shape[0] - nb - HEAD_DIM), cd)], axis=1)

        ss = jnp.dot(qa, ka_sc[...], preferred_element_type=f32) + bs_ref[hs, :]
        e, e_new, inv = softmax_with_new(ss, new_key(rows[:, 2 * SLOT:3 * SLOT][:, lanes]))
        acc = (lax.dot_general(e.astype(cd), vst_ref[0, 0, k], nt, preferred_element_type=f32)
               + bf(e_new) * bf(rows[:, 3 * SLOT:4 * SLOT][:, lanes]))
        out = out + gates[:, 1:2] * inv * acc

        sw = lax.dot_general(q, win_ref[0, 0, k], nt, preferred_element_type=f32) + bw_ref[hs, :]
        e, e_new, inv = softmax_with_new(sw, new_key(wnew[:, :SLOT][:, lanes]))
        acc = (jnp.dot(e.astype(cd), win_ref[0, 1, k], preferred_element_type=f32)
               + bf(e_new) * bf(wnew[:, SLOT:][:, lanes]))
        o_ref[0, hs, :] = out + gates[:, 2:3] * inv * acc


def _nsa_decode_layer(x, cmp_rows, sel_kvt, win_buf, ln_g, w_in, w_out, q_gain, k_gain, pe, w1, w2, rel_bias):
    N, L, D = x.shape
    P = sel_kvt.shape[-1]
    wb = win_buf.shape[1]
    assert L == 1 and P % SEL_BLOCK == 0 and wb == WINDOW and P >= BIAS_SPAN and P // SEL_BLOCK >= N_SEL - 1
    f32 = jnp.float32
    cd = MXU_DTYPE
    nc = P // CMP_STRIDE
    nb = P // SEL_BLOCK
    x2 = x.reshape(N, D)
    q, rows, wnew, gates = _nsa_project(x2, ln_g, w_in, q_gain, k_gain, N, 1)
    cmp = _compress(cmp_rows, pe, w1, w2, k_gain[0])
    cmp = cmp.reshape(2, N, nb, CMP_PER_SEL, KV_HEADS, HEAD_DIM).transpose(0, 1, 4, 3, 2, 5)
    cmp = cmp.reshape(2, N, KV_HEADS, nc, HEAD_DIM).astype(cd)
    qs = (q.reshape(N, N_HEADS, HEAD_DIM) * HEAD_DIM ** -0.5).astype(cd)
    win = win_buf.transpose(0, 2, 3, 1, 4).astype(cd)
    onehot = jnp.asarray(np.arange(nb)[:, None] == (np.arange(P) // SEL_BLOCK)[None, :], dtype=cd)
    fd = rel_bias[_bucket_of_distance()].T
    fd = fd - fd[:, BIAS_SPAN - 1:]
    cidx = (np.arange(nc) % nb) * CMP_PER_SEL + np.arange(nc) // nb
    dc = P - (cidx * CMP_STRIDE + CMP_LEN - 1)
    b_cmp = jnp.where(dc >= 0, jnp.take(fd, np.clip(dc, 0, BIAS_SPAN - 1), axis=1), NEG)
    b_sel = jnp.take(fd, np.clip(P - np.arange(P), 0, BIAS_SPAN - 1), axis=1)
    b_win = jnp.take(fd, np.clip(wb - np.arange(wb), 0, BIAS_SPAN - 1), axis=1)
    b_new = fd[:, 0:1]
    seq3 = lambda n: (n, 0, 0)
    seq4 = lambda n: (n, 0, 0, 0)
    fixed = lambda n: (0, 0)
    o = pl.pallas_call(
        functools.partial(_nsa_decode_kernel, nb=nb),
        grid=(N,),
        in_specs=[pl.BlockSpec((1, N_HEADS, HEAD_DIM), seq3),
                  pl.BlockSpec((1, KV_HEADS, nc, HEAD_DIM), seq4),
                  pl.BlockSpec((1, KV_HEADS, nc, HEAD_DIM), seq4),
                  pl.BlockSpec((1, 1, KV_HEADS, HEAD_DIM, P), lambda n: (0, n, 0, 0, 0)),
                  pl.BlockSpec((1, 1, KV_HEADS, HEAD_DIM, P), lambda n: (1, n, 0, 0, 0)),
                  pl.BlockSpec((1, 2, KV_HEADS, wb, HEAD_DIM), lambda n: (n, 0, 0, 0, 0)),
                  pl.BlockSpec((1, 1, 4 * SLOT), seq3),
                  pl.BlockSpec((1, 1, 2 * SLOT), seq3),
                  pl.BlockSpec((1, N_HEADS, 3), seq3),
                  pl.BlockSpec((nb, P), fixed), pl.BlockSpec((N_HEADS, nc), fixed),
                  pl.BlockSpec((N_HEADS, P), fixed), pl.BlockSpec((N_HEADS, wb), fixed),
                  pl.BlockSpec((N_HEADS, 1), fixed)],
        out_specs=pl.BlockSpec((1, N_HEADS, HEAD_DIM), seq3),
        out_shape=jax.ShapeDtypeStruct((N, N_HEADS, HEAD_DIM), f32),
        scratch_shapes=[pltpu.VMEM((-(-(nb + HEAD_DIM) // LANES) * LANES, P), cd)],
        compiler_params=pltpu.CompilerParams(dimension_semantics=("arbitrary",),
                                             vmem_limit_bytes=48 * 1024 * 1024),
        name="nsa_decode_attn",
    )(qs, cmp[0], cmp[1], sel_kvt, sel_kvt, win, rows.reshape(N, 1, 4 * SLOT), wnew.reshape(N, 1, 2 * SLOT),
      gates[:, :3 * N_HEADS].reshape(N, N_HEADS, 3), onehot, b_cmp, b_sel, b_win, b_new)
    y = _proj_residual(x2, o.reshape(N, HQ), w_out)
    new_win = jnp.concatenate([win_buf, wnew.reshape(N, 1, 2, KV_HEADS, HEAD_DIM)], axis=1)[:, 1:]
    return y.reshape(N, 1, D), rows.reshape(N, 1, 4, KV_HEADS, HEAD_DIM), new_win


def _split3(x):
    hi = x.astype(MXU_DTYPE)
    r1 = x - hi.astype(jnp.float32)
    mid = r1.astype(MXU_DTYPE)
    return hi, mid, (r1 - mid.astype(jnp.float32)).astype(MXU_DTYPE)


def _nsa_sample_kernel(q_ref, kc_ref, vc_ref, ks_ref, vs_ref, win_ref, rows_ref, wnew_ref, g_ref,
                       oh_ref, grp_ref, own_ref, bc_ref, bs_ref, bw_ref, bn_ref, o_ref, *, nb):
    f32 = jnp.float32
    cd = MXU_DTYPE
    nt = (((1,), (1,)), ((), ()))
    q = q_ref[0]
    qf = q.astype(f32)

    def softmax_with_new(s, s_new):
        m = jnp.maximum(jnp.max(s, axis=1, keepdims=True), s_new)
        e = jnp.exp(s - m)
        e_new = jnp.exp(s_new - m)
        return e, e_new, 1.0 / (jnp.sum(e, axis=1, keepdims=True) + e_new)

    def new_key(row):
        return jnp.sum(qf * row.astype(cd).astype(f32), axis=1, keepdims=True) + bn_ref[...]

    sc = lax.dot_general(q, kc_ref[0], nt, preferred_element_type=f32) + bc_ref[...]
    ec = jnp.exp(sc - jnp.max(sc, axis=1, keepdims=True))
    pc = ec / jnp.sum(ec, axis=1, keepdims=True)
    out = g_ref[0][:, 0:1] * jnp.dot(pc.astype(cd), vc_ref[0], preferred_element_type=f32)

    imp = sum(pc[:, r * nb:(r + 1) * nb] for r in range(CMP_PER_SEL))
    imp = sum(jnp.dot(grp_ref[...], part, preferred_element_type=f32) for part in _split3(imp))
    blk = lax.broadcasted_iota(jnp.int32, imp.shape, 1)
    forced = (blk == 0) | (blk == nb - 1)
    score = jnp.where(forced, jnp.inf, imp)
    pen = jnp.full(imp.shape, NEG, f32)
    for _ in range(N_SEL - 1):
        best = jnp.max(score, axis=1, keepdims=True)
        first = jnp.min(jnp.where(score == best, blk, nb), axis=1, keepdims=True)
        hit = blk == first
        pen = jnp.where(hit, 0.0, pen)
        score = jnp.where(hit, -jnp.inf, score)

    ss = (lax.dot_general(q, ks_ref[0, 0].astype(cd), nt, preferred_element_type=f32)
          + jnp.dot(pen.astype(cd), oh_ref[...], preferred_element_type=f32) + bs_ref[...])
    rows = rows_ref[0]
    e, e_new, inv = softmax_with_new(ss, new_key(rows[:, 2 * SLOT:3 * SLOT]))
    acc = (jnp.dot(e.astype(cd), vs_ref[0, 0].astype(cd), preferred_element_type=f32)
           + e_new.astype(cd).astype(f32) * rows[:, 3 * SLOT:4 * SLOT].astype(cd).astype(f32))
    out = out + g_ref[0][:, 1:2] * inv * acc

    win = win_ref[0]
    wnew = wnew_ref[0]
    sw = lax.dot_general(q, win[:, :SLOT].astype(cd), nt, preferred_element_type=f32) + bw_ref[...]
    e, e_new, inv = softmax_with_new(sw, new_key(wnew[:, :SLOT]))
    acc = (jnp.dot(e.astype(cd), win[:, SLOT:].astype(cd), preferred_element_type=f32)
           + e_new.astype(cd).astype(f32) * wnew[:, SLOT:].astype(cd).astype(f32))
    out = out + g_ref[0][:, 2:3] * inv * acc
    o_ref[0] = out * own_ref[...]


def _nsa_sample_layer(x, past, win_buf, ln_g, w_in, w_out, q_gain, k_gain, pe, w1, w2, rel_bias):
    N, L, D = x.shape
    P = past.shape[2]
    wb = win_buf.shape[1]
    assert L == 1 and P % SEL_BLOCK == 0 and wb == WINDOW and P >= BIAS_SPAN and P // SEL_BLOCK >= N_SEL - 1
    f32 = jnp.float32
    cd = MXU_DTYPE
    nc = P // CMP_STRIDE
    nb = P // SEL_BLOCK
    x2 = x.reshape(N, D)
    q, rows, wnew, gates = _nsa_project(x2, ln_g, w_in, q_gain, k_gain, N, 1)
    cmp = _compress(past.reshape(4, N, nc, CMP_STRIDE * SLOT), pe, w1, w2, k_gain[0])
    cmp = cmp.reshape(2, N, nb, CMP_PER_SEL, SLOT).transpose(0, 1, 3, 2, 4).reshape(2, N, nc, SLOT).astype(cd)
    head_kv = np.arange(N_HEADS) // Q_PER_KV
    own = (head_kv[:, None] == (np.arange(SLOT) // HEAD_DIM)[None, :]).astype(np.float32)
    qbd = (jnp.tile(q.reshape(N, N_HEADS, HEAD_DIM), (1, 1, KV_HEADS)) * own * HEAD_DIM ** -0.5).astype(cd)
    grp = jnp.asarray(head_kv[:, None] == head_kv[None, :], dtype=cd)
    onehot = jnp.asarray(np.arange(nb)[:, None] == (np.arange(P) // SEL_BLOCK)[None, :], dtype=cd)
    fd = rel_bias[_bucket_of_distance()].T
    fd = fd - fd[:, BIAS_SPAN - 1:]
    cidx = (np.arange(nc) % nb) * CMP_PER_SEL + np.arange(nc) // nb
    dc = P - (cidx * CMP_STRIDE + CMP_LEN - 1)
    b_cmp = jnp.where(dc >= 0, jnp.take(fd, np.clip(dc, 0, BIAS_SPAN - 1), axis=1), NEG)
    b_sel = jnp.take(fd, np.clip(P - np.arange(P), 0, BIAS_SPAN - 1), axis=1)
    b_win = jnp.take(fd, np.clip(wb - np.arange(wb), 0, BIAS_SPAN - 1), axis=1)
    b_new = fd[:, 0:1]
    seq3 = lambda n: (n, 0, 0)
    fixed = lambda n: (0, 0)
    o = pl.pallas_call(
        functools.partial(_nsa_sample_kernel, nb=nb),
        grid=(N,),
        in_specs=[pl.BlockSpec((1, N_HEADS, SLOT), seq3),
                  pl.BlockSpec((1, nc, SLOT), seq3),
                  pl.BlockSpec((1, nc, SLOT), seq3),
                  pl.BlockSpec((1, 1, P, SLOT), lambda n: (2, n, 0, 0)),
                  pl.BlockSpec((1, 1, P, SLOT), lambda n: (3, n, 0, 0)),
                  pl.BlockSpec((1, wb, 2 * SLOT), seq3),
                  pl.BlockSpec((1, 1, 4 * SLOT), seq3),
                  pl.BlockSpec((1, 1, 2 * SLOT), seq3),
                  pl.BlockSpec((1, N_HEADS, 3), seq3),
                  pl.BlockSpec((nb, P), fixed), pl.BlockSpec((N_HEADS, N_HEADS), fixed),
                  pl.BlockSpec((N_HEADS, SLOT), fixed), pl.BlockSpec((N_HEADS, nc), fixed),
                  pl.BlockSpec((N_HEADS, P), fixed), pl.BlockSpec((N_HEADS, wb), fixed),
                  pl.BlockSpec((N_HEADS, 1), fixed)],
        out_specs=pl.BlockSpec((1, N_HEADS, SLOT), seq3),
        out_shape=jax.ShapeDtypeStruct((N, N_HEADS, SLOT), f32),
        compiler_params=pltpu.CompilerParams(dimension_semantics=("arbitrary",),
                                             vmem_limit_bytes=56 * 1024 * 1024),
        name="nsa_sample_attn",
    )(qbd, cmp[0], cmp[1], past, past, win_buf.reshape(N, wb, 2 * SLOT), rows.reshape(N, 1, 4 * SLOT),
      wnew.reshape(N, 1, 2 * SLOT), gates[:, :3 * N_HEADS].reshape(N, N_HEADS, 3),
      onehot, grp, jnp.asarray(own), b_cmp, b_sel, b_win, b_new)
    o = o.reshape(N, KV_HEADS, Q_PER_KV, KV_HEADS, HEAD_DIM)
    o = jnp.stack([o[:, k, :, k] for k in range(KV_HEADS)], axis=1).reshape(N, HQ)
    y = _proj_residual(x2, o, w_out)
    new_win = jnp.concatenate([win_buf, wnew.reshape(N, 1, 2, KV_HEADS, HEAD_DIM)], axis=1)[:, 1:]
    return y.reshape(N, 1, D), rows.reshape(N, 1, 4, KV_HEADS, HEAD_DIM), new_win


HQ = N_HEADS * HEAD_DIM
SLOT = KV_HEADS * HEAD_DIM
NSA_COLS = HQ + 6 * SLOT + 3 * N_HEADS
NSA_COLS_PAD = -(-NSA_COLS // LANES) * LANES
GATE_COLS = NSA_COLS_PAD - HQ - 6 * SLOT


def _nsa_proj_kernel(x_ref, g_ref, w_ref, qg_ref, kg_ref, *outs, prompt, n_blocks):
    f32 = jnp.float32
    x = x_ref[...]
    xn = x * lax.rsqrt(jnp.mean(x * x, axis=-1, keepdims=True) + EPS) * g_ref[...]
    h = jnp.dot(xn.astype(MXU_DTYPE), w_ref[...], preferred_element_type=f32)

    def head_norm(v, gain):
        return v * lax.rsqrt(jnp.mean(v * v, axis=-1, keepdims=True) + EPS) * gain

    def slot(s, kv):
        lo = HQ + s * SLOT + kv * HEAD_DIM
        return h[:, lo:lo + HEAD_DIM]

    gates = 1.0 / (1.0 + jnp.exp(-h[:, HQ + 6 * SLOT:]))
    if prompt:
        q_ref, rows_ref, win_ref, kat_ref, vsa_ref, kwt_ref, vwa_ref, c_ref, gate_ref = outs
        for hh in range(N_HEADS):
            qn = head_norm(h[:, hh * HEAD_DIM:(hh + 1) * HEAD_DIM], qg_ref[...])
            q_ref[0, hh] = (qn * (HEAD_DIM ** -0.5 * LOG2E)).astype(MXU_DTYPE)
        c_ref[0, 0] = h[:, HQ:HQ + SLOT].astype(MXU_DTYPE)
        c_ref[1, 0] = h[:, HQ + SLOT:HQ + 2 * SLOT].astype(MXU_DTYPE)
        rows, ka = x.shape[0], kat_ref.shape[3]
        nb = n_blocks
        first_key = (pl.program_id(0) % (nb * SEL_BLOCK // rows)) * rows
        key_blk = (first_key + lax.broadcasted_iota(jnp.int32, (nb, rows), 1)) // SEL_BLOCK
        onehot = jnp.where(key_blk == lax.broadcasted_iota(jnp.int32, (nb, rows), 0), 1.0, 0.0).astype(MXU_DTYPE)
        kpad = jnp.zeros((ka - nb - HEAD_DIM, rows), MXU_DTYPE)
        ones = jnp.ones((rows, LANES - HEAD_DIM), MXU_DTYPE)
    else:
        q_ref, rows_ref, win_ref, gate_ref = outs
        for hh in range(N_HEADS):
            q_ref[:, hh * HEAD_DIM:(hh + 1) * HEAD_DIM] = head_norm(h[:, hh * HEAD_DIM:(hh + 1) * HEAD_DIM],
                                                                    qg_ref[...])
    rows_ref[:, 0:2 * SLOT] = h[:, HQ:HQ + 2 * SLOT]
    rows_ref[:, 3 * SLOT:4 * SLOT] = h[:, HQ + 3 * SLOT:HQ + 4 * SLOT]
    for kv in range(KV_HEADS):
        lanes = slice(kv * HEAD_DIM, (kv + 1) * HEAD_DIM)
        ksn = head_norm(slot(2, kv), kg_ref[1:2, :])
        kwn = head_norm(slot(4, kv), kg_ref[2:3, :])
        rows_ref[:, 2 * SLOT + kv * HEAD_DIM:2 * SLOT + (kv + 1) * HEAD_DIM] = ksn
        if prompt:
            win_ref[0, 0, :, lanes] = kwn
            kat_ref[0, kv, 0] = jnp.concatenate([onehot, ksn.T.astype(MXU_DTYPE), kpad], axis=0)
            kwt_ref[0, kv, 0] = kwn.T.astype(MXU_DTYPE)
            vsa_ref[0, kv, 0] = jnp.concatenate([slot(3, kv).astype(MXU_DTYPE), ones], axis=1)
            vwa_ref[0, kv, 0] = jnp.concatenate([slot(5, kv).astype(MXU_DTYPE), ones], axis=1)
        else:
            win_ref[:, lanes] = kwn
    if prompt:
        win_ref[0, 0, :, SLOT:2 * SLOT] = h[:, HQ + 5 * SLOT:HQ + 6 * SLOT]
    else:
        win_ref[:, SLOT:2 * SLOT] = h[:, HQ + 5 * SLOT:HQ + 6 * SLOT]
    if prompt:
        gate_ref[...] = jnp.zeros(gate_ref.shape, f32)
        per_kv = 3 * Q_PER_KV
        for kv in range(KV_HEADS):
            gate_ref[:, kv * LANES:kv * LANES + per_kv] = gates[:, kv * per_kv:(kv + 1) * per_kv]
    else:
        gate_ref[...] = gates


def _nsa_project(x, ln_g, w_in, q_gain, k_gain, batch, seq):
    T, D = x.shape
    f32 = jnp.float32
    cd = MXU_DTYPE
    prompt = seq > 1
    tb = TK if prompt else T
    w = jnp.pad(w_in, ((0, 0), (0, NSA_COLS_PAD - NSA_COLS))).astype(cd)
    fixed = lambda i: (0, 0)
    row = lambda i: (i, 0)
    in_specs = [pl.BlockSpec((tb, D), row), pl.BlockSpec((1, D), fixed), pl.BlockSpec((D, NSA_COLS_PAD), fixed),
                pl.BlockSpec((1, HEAD_DIM), fixed), pl.BlockSpec((3, HEAD_DIM), fixed)]
    if prompt:
        assert seq % tb == 0 and WINDOW == 2 * tb
        n = seq // tb
        n_blocks = seq // SEL_BLOCK
        ka = -(-(n_blocks + HEAD_DIM) // LANES) * LANES
        chunk = lambda i: (i // n, 0, i % n, 0, 0)
        kt_spec = pl.BlockSpec((1, KV_HEADS, 1, HEAD_DIM, tb), chunk)
        ka_spec = pl.BlockSpec((1, KV_HEADS, 1, ka, tb), chunk)
        v_spec = pl.BlockSpec((1, KV_HEADS, 1, tb, LANES), chunk)
        kt_shape = jax.ShapeDtypeStruct((batch, KV_HEADS, n, HEAD_DIM, tb), cd)
        ka_shape = jax.ShapeDtypeStruct((batch, KV_HEADS, n, ka, tb), cd)
        v_shape = jax.ShapeDtypeStruct((batch, KV_HEADS, n, tb, LANES), cd)
        out_specs = [
            pl.BlockSpec((1, N_HEADS, tb, HEAD_DIM), lambda i: (i // n, 0, i % n, 0)),
            pl.BlockSpec((tb, 4 * SLOT), row),
            pl.BlockSpec((1, 1, tb, 2 * SLOT), lambda i: (i // n, jnp.maximum(i % n - (n - 2), 0), 0, 0)),
            ka_spec, v_spec, kt_spec, v_spec,
            pl.BlockSpec((2, 1, tb, SLOT), lambda i: (0, i // n, i % n, 0)),
            pl.BlockSpec((tb, KV_HEADS * LANES), row)]
        out_shape = [
            jax.ShapeDtypeStruct((batch, N_HEADS, seq, HEAD_DIM), cd),
            jax.ShapeDtypeStruct((T, 4 * SLOT), f32),
            jax.ShapeDtypeStruct((batch, 2, tb, 2 * SLOT), f32),
            ka_shape, v_shape, kt_shape, v_shape,
            jax.ShapeDtypeStruct((2, batch, seq, SLOT), cd),
            jax.ShapeDtypeStruct((T, KV_HEADS * LANES), f32)]
    else:
        out_specs = [pl.BlockSpec((tb, HQ), row), pl.BlockSpec((tb, 4 * SLOT), row),
                     pl.BlockSpec((tb, 2 * SLOT), row), pl.BlockSpec((tb, GATE_COLS), row)]
        n_blocks = 0
        out_shape = [jax.ShapeDtypeStruct((T, HQ), f32), jax.ShapeDtypeStruct((T, 4 * SLOT), f32),
                     jax.ShapeDtypeStruct((T, 2 * SLOT), f32), jax.ShapeDtypeStruct((T, GATE_COLS), f32)]
    return pl.pallas_call(
        functools.partial(_nsa_proj_kernel, prompt=prompt, n_blocks=n_blocks),
        grid=(T // tb,),
        in_specs=in_specs, out_specs=out_specs, out_shape=out_shape,
        compiler_params=pltpu.CompilerParams(dimension_semantics=("arbitrary",),
                                             vmem_limit_bytes=48 * 1024 * 1024),
        name="nsa_proj",
    )(x, ln_g.reshape(1, D), w, q_gain.reshape(1, HEAD_DIM), k_gain)


def _compress_kernel(c_ref, w1_ref, pe_ref, w2_ref, kg_ref, o_ref):
    f32 = jnp.float32
    pairs = SLOT // LANES
    c = c_ref[0, 0]
    u = [jnp.dot(jnp.concatenate([c[:, l * SLOT + p * LANES:l * SLOT + (p + 1) * LANES]
                                  for l in range(CMP_STRIDE)], axis=1).astype(MXU_DTYPE),
                 w1_ref[0], preferred_element_type=f32) for p in range(pairs)]
    first = jnp.concatenate([up[:, :LANES] for up in u], axis=1)
    second = jnp.concatenate([up[:, LANES:] for up in u], axis=1)
    pe = jnp.dot(pe_ref[0], w1_ref[0], preferred_element_type=f32)
    pe = jnp.concatenate([pe[0:1, :LANES] + pe[1:2, LANES:]] * pairs, axis=1)
    nc = first.shape[0]
    hid = first + pltpu.roll(second, nc - 1, 0) + pe
    out = jnp.dot(_gelu_tanh(hid).astype(MXU_DTYPE), w2_ref[0], preferred_element_type=f32)
    is_key = pl.program_id(1) == 0
    for kv in range(KV_HEADS):
        lanes = slice(kv * HEAD_DIM, (kv + 1) * HEAD_DIM)
        v = out[:, lanes]
        vn = v * lax.rsqrt(jnp.mean(v * v, axis=-1, keepdims=True) + EPS) * kg_ref[...]
        o_ref[0, 0, :, lanes] = jnp.where(is_key, vn, v)


def _compress(c, pe, w1, w2, k_gain0):
    n, nc = c.shape[1:3]
    cd = MXU_DTYPE
    kc = CMP_STRIDE * SLOT
    eye = jnp.eye(KV_HEADS, dtype=jnp.float32)
    per_tile = LANES // HEAD_DIM
    kp = CMP_STRIDE * LANES
    w1h = w1.reshape(2, 2, CMP_STRIDE, HEAD_DIM, CMP_HID)
    w1b = jnp.einsum('shldf,kj->slkdhjf', w1h, eye[:per_tile, :per_tile]).reshape(2, kp, 2 * LANES).astype(cd)
    w2b = jnp.einsum('sdf,kj->skdjf', w2, eye).reshape(2, SLOT, SLOT).astype(cd)
    peh = jnp.broadcast_to(pe.reshape(2, 2, CMP_STRIDE, 1, HEAD_DIM), (2, 2, CMP_STRIDE, per_tile, HEAD_DIM))
    peh = jnp.pad(peh.reshape(2, 2, kp), ((0, 0), (0, 14), (0, 0))).astype(cd)
    return pl.pallas_call(
        _compress_kernel,
        grid=(n, 2),
        in_specs=[pl.BlockSpec((1, 1, nc, kc), lambda b, s: (s, b, 0, 0)),
                  pl.BlockSpec((1, kp, 2 * LANES), lambda b, s: (s, 0, 0)),
                  pl.BlockSpec((1, 16, kp), lambda b, s: (s, 0, 0)),
                  pl.BlockSpec((1, SLOT, SLOT), lambda b, s: (s, 0, 0)),
                  pl.BlockSpec((1, HEAD_DIM), lambda b, s: (0, 0))],
        out_specs=pl.BlockSpec((1, 1, nc, SLOT), lambda b, s: (s, b, 0, 0)),
        out_shape=jax.ShapeDtypeStruct((2, n, nc, SLOT), jnp.float32),
        compiler_params=pltpu.CompilerParams(dimension_semantics=("arbitrary", "arbitrary"),
                                             vmem_limit_bytes=48 * 1024 * 1024),
        name="nsa_compress",
    )(c, w1b, peh, w2b, k_gain0.reshape(1, HEAD_DIM))


def _proj_residual_kernel(x_ref, a_ref, w_ref, o_ref):
    o_ref[...] = x_ref[...] + jnp.dot(a_ref[...].astype(MXU_DTYPE), w_ref[...],
                                      preferred_element_type=jnp.float32)


def _proj_residual(x, a, w):
    T, D = x.shape
    k = a.shape[1]
    tb = min(512, T)
    assert T % tb == 0
    return pl.pallas_call(
        _proj_residual_kernel,
        grid=(T // tb,),
        in_specs=[pl.BlockSpec((tb, D), lambda i: (i, 0)), pl.BlockSpec((tb, k), lambda i: (i, 0)),
                  pl.BlockSpec((k, D), lambda i: (0, 0))],
        out_specs=pl.BlockSpec((tb, D), lambda i: (i, 0)),
        out_shape=jax.ShapeDtypeStruct((T, D), jnp.float32),
        compiler_params=pltpu.CompilerParams(dimension_semantics=("arbitrary",)),
        name="proj_residual",
    )(x, a, w.astype(MXU_DTYPE))


MOE_ROWS = 256
ROUTE_COLS = LANES
ROW_DMA_UNROLL = 8


def _moe_route_kernel(x_ref, g_ref, w_ref, b_ref, tri_ref, h_ref, meta_ref, wt_ref, cnt_ref, carry_sc):
    f32 = jnp.float32
    step = pl.program_id(0)

    @pl.when(step == 0)
    def _():
        carry_sc[...] = jnp.zeros(carry_sc.shape, f32)

    x = x_ref[...]
    h = x * lax.rsqrt(jnp.mean(x * x, axis=-1, keepdims=True) + EPS) * g_ref[...]
    h_ref[...] = h
    logits = jnp.dot(h.astype(MXU_DTYPE), w_ref[...], preferred_element_type=f32) + b_ref[...]
    tb = logits.shape[0]
    col = lax.broadcasted_iota(jnp.int32, (tb, ROUTE_COLS), 1)

    def first_max(vals):
        best = jnp.max(vals, axis=1, keepdims=True)
        return best, jnp.min(jnp.where(vals == best, col, ROUTE_COLS), axis=1, keepdims=True)

    lg = jnp.where(col < N_GROUPS, logits, -jnp.inf)
    g_best, grp = first_max(lg)
    g_w = 1.0 / jnp.sum(jnp.exp(lg - g_best), axis=1, keepdims=True)
    lo = N_GROUPS + EXPERTS_PER_GROUP * grp
    le = jnp.where((col >= lo) & (col < lo + EXPERTS_PER_GROUP), logits, -jnp.inf)
    v0, c0 = first_max(le)
    v1, c1 = first_max(jnp.where(col == c0, -jnp.inf, le))
    e1 = jnp.exp(v1 - v0)
    w0 = g_w / (1.0 + e1)
    w1 = g_w * e1 / (1.0 + e1)
    chosen = (col == c0) | (col == c1)
    before = jnp.dot(tri_ref[...], jnp.where(chosen, 1.0, 0.0).astype(MXU_DTYPE),
                     preferred_element_type=f32) + carry_sc[...]
    r0 = jnp.sum(jnp.where(col == c0, before, 0.0), axis=1, keepdims=True).astype(jnp.int32)
    r1 = jnp.sum(jnp.where(col == c1, before, 0.0), axis=1, keepdims=True).astype(jnp.int32)
    carry_sc[...] += jnp.sum(jnp.where(chosen, 1.0, 0.0), axis=0, keepdims=True)
    cnt_ref[...] = carry_sc[...]
    meta_ref[...] = jnp.where(col == 0, c0 - N_GROUPS, jnp.where(col == 1, c1 - N_GROUPS,
                              jnp.where(col == 2, r0, jnp.where(col == 3, r1, 0))))
    wt_ref[...] = jnp.where(col == 0, w0, jnp.where(col == 1, w1, 0.0))


def _moe_dispatch_kernel(dest_ref, h_ref, xbuf_in, xbuf_ref, sem):
    del xbuf_in
    tb = h_ref.shape[0]

    def row_copy(r, k):
        return pltpu.make_async_copy(h_ref.at[pl.ds(r, 1)], xbuf_ref.at[pl.ds(dest_ref[0, 0, 2 * r + k], 1)], sem)

    def start(r, c):
        row_copy(r, 0).start()
        row_copy(r, 1).start()
        return c

    def wait(r, c):
        row_copy(r, 0).wait()
        row_copy(r, 1).wait()
        return c

    lax.fori_loop(0, tb, start, 0, unroll=ROW_DMA_UNROLL)
    lax.fori_loop(0, tb, wait, 0, unroll=ROW_DMA_UNROLL)


def _moe_expert_kernel(blk_e_ref, nblk_ref, x_ref, wgu_ref, wdn_ref, y_ref, wgu_sc, wdn_sc):
    i = pl.program_id(0)
    f32 = jnp.float32

    @pl.when(i < nblk_ref[0])
    def _():
        changed = jnp.logical_or(i == 0, blk_e_ref[i] != blk_e_ref[jnp.maximum(i - 1, 0)])

        @pl.when(changed)
        def _():
            wgu_sc[...] = wgu_ref[0].astype(MXU_DTYPE)
            wdn_sc[...] = wdn_ref[0].astype(MXU_DTYPE)

        gu = jnp.dot(x_ref[...].astype(MXU_DTYPE), wgu_sc[...], preferred_element_type=f32)
        gate = gu[:, :D_EXPERT]
        act = gate * (1.0 / (1.0 + jnp.exp(-gate))) * gu[:, D_EXPERT:]
        y_ref[...] = jnp.dot(act.astype(MXU_DTYPE), wdn_sc[...], preferred_element_type=f32)

    @pl.when(i >= nblk_ref[0])
    def _():
        y_ref[...] = jnp.zeros(y_ref.shape, f32)


def _moe_combine_kernel(dest_ref, x_ref, wt_ref, ybuf_ref, o_ref, rows_sc, sem):
    tb = x_ref.shape[0]

    def row_copy(r, k):
        return pltpu.make_async_copy(ybuf_ref.at[pl.ds(dest_ref[0, 0, 2 * r + k], 1)],
                                     rows_sc.at[k, pl.ds(r, 1)], sem)

    def start(r, c):
        row_copy(r, 0).start()
        row_copy(r, 1).start()
        return c

    def wait(r, c):
        row_copy(r, 0).wait()
        row_copy(r, 1).wait()
        return c

    lax.fori_loop(0, tb, start, 0, unroll=ROW_DMA_UNROLL)
    lax.fori_loop(0, tb, wait, 0, unroll=ROW_DMA_UNROLL)
    wt = wt_ref[...]
    o_ref[...] = x_ref[...] + (wt[:, 0:1] * rows_sc[0] + wt[:, 1:2] * rows_sc[1])


def _hier_moe_residual(x, ln_g, w_grp, b_grp, w_exp, b_exp, w_gu, w_dn):
    T, D = x.shape
    f32 = jnp.float32
    tb = min(256, T)
    assert T % tb == 0
    nt = T // tb
    pad_cols = ROUTE_COLS - N_GROUPS - N_EXPERTS
    w_r = jnp.pad(jnp.concatenate([w_grp, w_exp], axis=1), ((0, 0), (0, pad_cols))).astype(MXU_DTYPE)
    b_r = jnp.pad(jnp.concatenate([b_grp, b_exp]), (0, pad_cols)).reshape(1, ROUTE_COLS)
    tri = jnp.asarray(np.tril(np.ones((tb, tb), np.float32), -1), dtype=MXU_DTYPE)
    row = lambda i: (i, 0)
    fixed = lambda i: (0, 0)
    h, meta, wt, cnt = pl.pallas_call(
        _moe_route_kernel,
        grid=(nt,),
        in_specs=[pl.BlockSpec((tb, D), row), pl.BlockSpec((1, D), fixed),
                  pl.BlockSpec((D, ROUTE_COLS), fixed), pl.BlockSpec((1, ROUTE_COLS), fixed),
                  pl.BlockSpec((tb, tb), fixed)],
        out_specs=[pl.BlockSpec((tb, D), row), pl.BlockSpec((tb, ROUTE_COLS), row),
                   pl.BlockSpec((tb, ROUTE_COLS), row), pl.BlockSpec((1, ROUTE_COLS), fixed)],
        out_shape=[jax.ShapeDtypeStruct((T, D), f32), jax.ShapeDtypeStruct((T, ROUTE_COLS), jnp.int32),
                   jax.ShapeDtypeStruct((T, ROUTE_COLS), f32), jax.ShapeDtypeStruct((1, ROUTE_COLS), f32)],
        scratch_shapes=[pltpu.VMEM((1, ROUTE_COLS), f32)],
        compiler_params=pltpu.CompilerParams(dimension_semantics=("arbitrary",)),
        name="moe_route",
    )(x, ln_g.reshape(1, D), w_r, b_r, tri)

    counts = cnt[0, N_GROUPS:N_GROUPS + N_EXPERTS].astype(jnp.int32)
    padded = (counts + MOE_ROWS - 1) // MOE_ROWS * MOE_ROWS
    pad_end = jnp.cumsum(padded)
    pad_start = pad_end - padded
    n_blocks = -(-(T * TOP_K) // MOE_ROWS) + N_EXPERTS
    n_slots = n_blocks * MOE_ROWS
    dest = pad_start[meta[:, 0:2]] + meta[:, 2:4]
    dest = dest.reshape(nt, 1, 2 * tb)
    starts = jnp.arange(n_blocks, dtype=jnp.int32) * MOE_ROWS
    used = (pad_end[-1] // MOE_ROWS).astype(jnp.int32).reshape(1)
    blk_start = jnp.minimum(starts, pad_end[-1] - 1)
    blk_e = jnp.minimum(jnp.sum(pad_end[None, :] <= blk_start[:, None], axis=1), N_EXPERTS - 1).astype(jnp.int32)

    dest_spec = pl.BlockSpec((1, 1, 2 * tb), lambda i: (i, 0, 0), memory_space=pltpu.SMEM)
    xbuf = pl.pallas_call(
        _moe_dispatch_kernel,
        grid=(nt,),
        in_specs=[dest_spec, pl.BlockSpec((tb, D), row), pl.BlockSpec(memory_space=pl.ANY)],
        out_specs=pl.BlockSpec(memory_space=pl.ANY),
        out_shape=jax.ShapeDtypeStruct((n_slots, D), f32),
        scratch_shapes=[pltpu.SemaphoreType.DMA(())],
        input_output_aliases={2: 0},
        compiler_params=pltpu.CompilerParams(dimension_semantics=("arbitrary",)),
        name="moe_dispatch",
    )(dest, h, jnp.zeros((n_slots, D), f32))

    ybuf = pl.pallas_call(
        _moe_expert_kernel,
        grid_spec=pltpu.PrefetchScalarGridSpec(
            num_scalar_prefetch=2,
            grid=(n_blocks,),
            in_specs=[pl.BlockSpec((MOE_ROWS, D), lambda i, be, nb: (i, 0)),
                      pl.BlockSpec((1, D, 2 * D_EXPERT), lambda i, be, nb: (be[i], 0, 0)),
                      pl.BlockSpec((1, D_EXPERT, D), lambda i, be, nb: (be[i], 0, 0))],
            out_specs=pl.BlockSpec((MOE_ROWS, D), lambda i, be, nb: (i, 0)),
            scratch_shapes=[pltpu.VMEM((D, 2 * D_EXPERT), MXU_DTYPE), pltpu.VMEM((D_EXPERT, D), MXU_DTYPE)]),
        out_shape=jax.ShapeDtypeStruct((n_slots, D), f32),
        compiler_params=pltpu.CompilerParams(dimension_semantics=("arbitrary",),
                                             vmem_limit_bytes=48 * 1024 * 1024),
        name="moe_experts",
    )(blk_e, used, xbuf, w_gu, w_dn)

    return pl.pallas_call(
        _moe_combine_kernel,
        grid=(nt,),
        in_specs=[dest_spec, pl.BlockSpec((tb, D), row), pl.BlockSpec((tb, ROUTE_COLS), row),
                  pl.BlockSpec(memory_space=pl.ANY)],
        out_specs=pl.BlockSpec((tb, D), row),
        out_shape=jax.ShapeDtypeStruct((T, D), f32),
        scratch_shapes=[pltpu.VMEM((TOP_K, tb, D), f32), pltpu.SemaphoreType.DMA(())],
        compiler_params=pltpu.CompilerParams(dimension_semantics=("arbitrary",)),
        name="moe_combine",
    )(dest, x, wt, ybuf)


def _gelu_tanh(x):
    return 0.5 * x * (1.0 + jnp.tanh(math.sqrt(2.0 / math.pi) * (x + 0.044715 * (x * x * x))))


def _gmlp_kernel(x_ref, g_ref, win_ref, bin_ref, lng_ref, lnb_ref, ws_ref, bs_ref, wout_ref,
                 o_ref, v_ref, *, single_position):
    f32 = jnp.float32
    x = x_ref[...]
    rows = x.shape[0]
    h = x * lax.rsqrt(jnp.mean(x * x, axis=-1, keepdims=True) + EPS) * g_ref[...]
    z = _gelu_tanh(jnp.dot(h.astype(MXU_DTYPE), win_ref[...], preferred_element_type=f32) + bin_ref[...])
    u = z[:, :GM_WIDTH]
    v = z[:, GM_WIDTH:]
    mu = jnp.mean(v, axis=-1, keepdims=True)
    var = jnp.mean(jnp.square(v - mu), axis=-1, keepdims=True)
    v = (v - mu) * lax.rsqrt(var + EPS) * lng_ref[...] + lnb_ref[...]
    v_ref[0] = v
    vb = v.astype(MXU_DTYPE)
    if single_position:
        s = ws_ref[...].astype(f32) * vb.astype(f32) + bs_ref[...]
    else:
        parts = []
        for c in range(rows // CHUNK):
            vc = vb[c * CHUNK:(c + 1) * CHUNK]
            parts.append(jnp.concatenate(
                [jnp.dot(ws_ref[g], vc[:, g * GM_GROUP_DIM:(g + 1) * GM_GROUP_DIM], preferred_element_type=f32)
                 + bs_ref[g] for g in range(GM_GROUPS)], axis=1))
        s = jnp.concatenate(parts, axis=0)
    y = jnp.dot((u * s).astype(MXU_DTYPE), wout_ref[...], preferred_element_type=f32)
    o_ref[...] = x + y


def _gmlp_residual(x, ln_g, w_in, b_in, ln2_g, ln2_b, w_s, b_s, w_out, seq):
    T, D = x.shape
    f32 = jnp.float32
    cd = MXU_DTYPE
    single = seq == 1
    tb = T if single else 2 * CHUNK
    assert T % tb == 0 and (single or seq % tb == 0)
    steps_per_seq = 1 if single else seq // tb
    if single:
        ws = jnp.repeat(w_s[:, 0, 0], GM_GROUP_DIM).reshape(1, GM_WIDTH).astype(cd)
        bs = jnp.repeat(b_s[:, 0], GM_GROUP_DIM).reshape(1, GM_WIDTH)
        ws_spec = pl.BlockSpec((1, GM_WIDTH), lambda i: (0, 0))
        bs_spec = pl.BlockSpec((1, GM_WIDTH), lambda i: (0, 0))
    else:
        causal = np.tril(np.ones((CHUNK, CHUNK), bool))
        ws = jnp.where(causal[None], w_s, 0).astype(cd)
        bs = jnp.broadcast_to(b_s[:, :, None], (GM_GROUPS, CHUNK, GM_GROUP_DIM))
        ws_spec = pl.BlockSpec((GM_GROUPS, CHUNK, CHUNK), lambda i: (0, 0, 0))
        bs_spec = pl.BlockSpec((GM_GROUPS, CHUNK, GM_GROUP_DIM), lambda i: (0, 0, 0))
    fixed = lambda i: (0, 0)
    row = lambda i: (i, 0)
    return pl.pallas_call(
        functools.partial(_gmlp_kernel, single_position=single),
        grid=(T // tb,),
        in_specs=[pl.BlockSpec((tb, D), row), pl.BlockSpec((1, D), fixed),
                  pl.BlockSpec((D, 2 * GM_WIDTH), fixed), pl.BlockSpec((1, 2 * GM_WIDTH), fixed),
                  pl.BlockSpec((1, GM_WIDTH), fixed), pl.BlockSpec((1, GM_WIDTH), fixed),
                  ws_spec, bs_spec, pl.BlockSpec((GM_WIDTH, D), fixed)],
        out_specs=[pl.BlockSpec((tb, D), row), pl.BlockSpec((1, tb, GM_WIDTH), lambda i: (i // steps_per_seq, 0, 0))],
        out_shape=[jax.ShapeDtypeStruct((T, D), f32),
                   jax.ShapeDtypeStruct((T // (tb * steps_per_seq), tb, GM_WIDTH), f32)],
        compiler_params=pltpu.CompilerParams(dimension_semantics=("arbitrary",),
                                             vmem_limit_bytes=56 * 1024 * 1024),
        name="gmlp",
    )(x, ln_g.reshape(1, D), w_in.astype(cd), b_in.reshape(1, -1), ln2_g.reshape(1, -1), ln2_b.reshape(1, -1),
      ws, bs, w_out.astype(cd))


def rmsnorm(x, g):
    xf = x.astype(jnp.float32)
    y = xf * lax.rsqrt(jnp.mean(xf * xf, axis=-1, keepdims=True) + EPS)
    return (y * g.astype(jnp.float32)).astype(x.dtype)


def layernorm(x, g, b):
    xf = x.astype(jnp.float32)
    mu = jnp.mean(xf, axis=-1, keepdims=True)
    var = jnp.mean(jnp.square(xf - mu), axis=-1, keepdims=True)
    return ((xf - mu) * lax.rsqrt(var + EPS) * g.astype(jnp.float32) + b.astype(jnp.float32)).astype(x.dtype)


def rel_bucket(n):
    n = jnp.maximum(n, 0)
    max_exact = N_BUCKETS // 2
    nf = jnp.maximum(n, 1).astype(jnp.float32)
    large = max_exact + (jnp.log(nf / max_exact) / math.log(MAX_DISTANCE / max_exact)
                         * (N_BUCKETS - max_exact)).astype(jnp.int32)
    large = jnp.minimum(large, N_BUCKETS - 1)
    return jnp.where(n < max_exact, n, large)


def head_bias(rel, rel_bias):
    q, k = rel.shape
    b = rel_bias[rel_bucket(rel)]
    return b.reshape(q, k, KV_HEADS, Q_PER_KV).transpose(2, 3, 0, 1)


def masked_softmax(logits, mask):
    z = jnp.where(mask, logits.astype(jnp.float32), -1e30)
    return jax.nn.softmax(z, axis=-1) * mask


def nsa_project(xn, w_in, q_gain, k_gain):
    B, T, _ = xn.shape
    hq = N_HEADS * HEAD_DIM
    hkv = 6 * KV_HEADS * HEAD_DIM
    h = xn @ w_in
    q = rmsnorm(h[..., :hq].reshape(B, T, N_HEADS, HEAD_DIM), q_gain)
    kv = h[..., hq:hq + hkv].reshape(B, T, 6, KV_HEADS, HEAD_DIM)
    gates = jax.nn.sigmoid(h[..., hq + hkv:].astype(jnp.float32)).reshape(B, T, N_HEADS, 3)
    k_sel = rmsnorm(kv[:, :, 2], k_gain[1])
    k_win = rmsnorm(kv[:, :, 4], k_gain[2])
    return q, kv[:, :, 0], kv[:, :, 1], k_sel, kv[:, :, 3], k_win, kv[:, :, 5], gates


def compress(rows, pe, w1, w2):
    B, T = rows.shape[:2]
    n_ch = T // CMP_STRIDE
    c = rows[:, :n_ch * CMP_STRIDE].reshape(B, n_ch, CMP_STRIDE, KV_HEADS, HEAD_DIM)
    blk = jnp.concatenate([c[:, :-1], c[:, 1:]], axis=2) + pe[None, None, :, None, :]
    h = jax.nn.gelu(jnp.einsum('bnlkd,ldf->bnkf', blk, w1.reshape(CMP_LEN, HEAD_DIM, CMP_HID)))
    return jnp.einsum('bnkf,fd->bnkd', h, w2)


def compressed_kv(k_raw, v_raw, k_gain0, pe, w1, w2):
    kc = rmsnorm(compress(k_raw, pe[0], w1[0], w2[0]), k_gain0)
    vc = compress(v_raw, pe[1], w1[1], w2[1])
    end = jnp.arange(kc.shape[1]) * CMP_STRIDE + (CMP_LEN - 1)
    return kc, vc, end


def nsa_core(q, q_pos, kc, vc, kc_end, ks, vs, kw, vw, kw_pos, gates, rel_bias):
    B, Q = q.shape[:2]
    nc = kc.shape[1]
    scale = HEAD_DIM ** -0.5
    qh = q.reshape(B, Q, KV_HEADS, Q_PER_KV, HEAD_DIM)
    lc = jnp.einsum('bqkgd,bnkd->bkgqn', qh, kc) * scale + head_bias(q_pos[:, None] - kc_end[None, :], rel_bias)
    pc = masked_softmax(lc, kc_end[None, :] <= q_pos[:, None])
    oc = jnp.einsum('bkgqn,bnkd->bqkgd', pc.astype(vc.dtype), vc)
    nb = ks.shape[1] // SEL_BLOCK
    imp = jnp.pad(pc.sum(axis=2), ((0, 0), (0, 0), (0, 0), (0, nb * CMP_PER_SEL - nc)))
    imp = imp.reshape(B, KV_HEADS, Q, nb, CMP_PER_SEL).sum(-1)
    blk = jnp.arange(nb)
    d = (q_pos // SEL_BLOCK)[:, None] - blk[None, :]
    forced = (blk[None, :] == 0) | ((d >= 0) & (d < N_LOCAL))
    score = jnp.where(forced, jnp.inf, jnp.where(d >= 0, imp, -jnp.inf))
    n_sel = min(N_SEL, nb)
    _, idx = lax.top_k(score, n_sel)
    bi = jnp.arange(B)[:, None, None, None]
    ki = jnp.arange(KV_HEADS)[None, :, None, None]
    ksb = ks.reshape(B, nb, SEL_BLOCK, KV_HEADS, HEAD_DIM).transpose(0, 3, 1, 2, 4)
    vsb = vs.reshape(B, nb, SEL_BLOCK, KV_HEADS, HEAD_DIM).transpose(0, 3, 1, 2, 4)
    kg = ksb[bi, ki, idx].reshape(B, KV_HEADS, Q, n_sel * SEL_BLOCK, HEAD_DIM)
    vg = vsb[bi, ki, idx].reshape(B, KV_HEADS, Q, n_sel * SEL_BLOCK, HEAD_DIM)
    kpos = (idx[..., None] * SEL_BLOCK + jnp.arange(SEL_BLOCK)).reshape(B, KV_HEADS, Q, n_sel * SEL_BLOCK)
    rel_s = q_pos[None, None, :, None] - kpos
    tb = rel_bias.reshape(N_BUCKETS, KV_HEADS, Q_PER_KV)
    bs = tb[rel_bucket(rel_s), ki].transpose(0, 1, 4, 2, 3)
    ls = jnp.einsum('bqkgd,bkqjd->bkgqj', qh, kg) * scale + bs
    ps = masked_softmax(ls, (rel_s >= 0)[:, :, None])
    os_ = jnp.einsum('bkgqj,bkqjd->bqkgd', ps.astype(vg.dtype), vg)
    rel_w = q_pos[:, None] - kw_pos[None, :]
    mw = (rel_w >= 0) & (rel_w <= WINDOW) & (kw_pos[None, :] >= 0)
    lw = jnp.einsum('bqkgd,bwkd->bkgqw', qh, kw) * scale + head_bias(rel_w, rel_bias)
    pw = masked_softmax(lw, mw)
    ow = jnp.einsum('bkgqw,bwkd->bqkgd', pw.astype(vw.dtype), vw)
    g = gates.reshape(B, Q, KV_HEADS, Q_PER_KV, 3).astype(oc.dtype)
    o = g[..., 0:1] * oc + g[..., 1:2] * os_ + g[..., 2:3] * ow
    return o.reshape(B, Q, N_HEADS * HEAD_DIM)


def nsa_prompt(xn, w_in, w_out, q_gain, k_gain, pe, w1, w2, rel_bias):
    B, S, _ = xn.shape
    q, kcr, vcr, ks, vs, kw, vw, gates = nsa_project(xn, w_in, q_gain, k_gain)
    kc, vc, kc_end = compressed_kv(kcr, vcr, k_gain[0], pe, w1, w2)
    o = _nsa_attention(q, kc, vc, ks, vs, kw, vw, gates, rel_bias)
    w_keep = min(WINDOW, S)
    new_rows = jnp.stack([kcr, vcr, ks, vs], axis=2)
    new_win = jnp.stack([kw[:, S - w_keep:], vw[:, S - w_keep:]], axis=2)
    y = _mm(o.reshape(B * S, -1), w_out).reshape(B, S, -1)
    return y, new_rows, new_win


def nsa_sample(xn, past_rows, win_buf, w_in, w_out, q_gain, k_gain, pe, w1, w2, rel_bias):
    B, L, _ = xn.shape
    P = past_rows.shape[1]
    q, kcr, vcr, ks, vs, kw, vw, gates = nsa_project(xn, w_in, q_gain, k_gain)
    new_rows = jnp.stack([kcr, vcr, ks, vs], axis=2)
    rows = jnp.concatenate([past_rows, new_rows], axis=1)
    kc, vc, kc_end = compressed_kv(rows[:, :, 0], rows[:, :, 1], k_gain[0], pe, w1, w2)
    T = P + L
    tp = -(-T // SEL_BLOCK) * SEL_BLOCK
    pad = ((0, 0), (0, tp - T), (0, 0), (0, 0))
    ks_all = jnp.pad(rows[:, :, 2], pad)
    vs_all = jnp.pad(rows[:, :, 3], pad)
    win_all = jnp.concatenate([win_buf, jnp.stack([kw, vw], axis=2)], axis=1)
    wb = win_buf.shape[1]
    kw_pos = P - wb + jnp.arange(wb + L)
    q_pos = P + jnp.arange(L)
    o = nsa_core(q, q_pos, kc, vc, kc_end, ks_all, vs_all, win_all[:, :, 0], win_all[:, :, 1],
                 kw_pos, gates, rel_bias)
    w_keep = min(WINDOW, wb + L)
    return o @ w_out, new_rows, win_all[:, wb + L - w_keep:]


def gmlp_mixer(xn, w_in, b_in, ln_g, ln_b, w_s, b_s, w_out):
    B, L, _ = xn.shape
    z = jax.nn.gelu(xn @ w_in + b_in)
    u = z[..., :GM_WIDTH]
    v = layernorm(z[..., GM_WIDTH:], ln_g, ln_b)
    lp = -(-L // CHUNK) * CHUNK
    vp = jnp.pad(v, ((0, 0), (0, lp - L), (0, 0))).reshape(B, lp // CHUNK, CHUNK, GM_GROUPS, GM_GROUP_DIM)
    causal = jnp.tril(jnp.ones((CHUNK, CHUNK), dtype=bool))
    ws = jnp.where(causal[None], w_s, 0)
    s = jnp.einsum('gts,bcsgd->bctgd', ws, vp) + b_s.T[None, None, :, :, None]
    s = s.reshape(B, lp, GM_WIDTH)[:, :L]
    start = ((L - 1) // CHUNK) * CHUNK
    return (u * s) @ w_out, v[:, start:]


def hier_moe(x, w_grp, b_grp, w_exp, b_exp, w_gu, w_dn):
    T, D = x.shape
    pg = jax.nn.softmax((x @ w_grp + b_grp).astype(jnp.float32), axis=-1)
    grp = jnp.argmax(pg, axis=-1)
    g_w = jnp.take_along_axis(pg, grp[:, None], axis=-1)
    le = (x @ w_exp + b_exp).astype(jnp.float32).reshape(T, N_GROUPS, EXPERTS_PER_GROUP)
    le = jnp.take_along_axis(le, grp[:, None, None], axis=1)[:, 0]
    top_v, top_i = lax.top_k(le, TOP_K)
    wts = (jax.nn.softmax(top_v, axis=-1) * g_w).reshape(-1)
    eid = (grp[:, None] * EXPERTS_PER_GROUP + top_i).reshape(-1)
    n = T * TOP_K
    order = jnp.argsort(eid)
    e_s = eid[order]
    tok_s = order // TOP_K
    w_s = wts[order]
    counts = jnp.bincount(eid, length=N_EXPERTS)
    padded = (counts + MOE_BLOCK - 1) // MOE_BLOCK * MOE_BLOCK
    pad_end = jnp.cumsum(padded)
    pad_start = pad_end - padded
    start = jnp.cumsum(counts) - counts
    dest = pad_start[e_s] + jnp.arange(n) - start[e_s]
    n_blocks = -(-n // MOE_BLOCK) + N_EXPERTS
    xbuf = jnp.zeros((n_blocks * MOE_BLOCK, D), x.dtype).at[dest].set(x[tok_s])
    blk_e = jnp.minimum(jnp.searchsorted(pad_end, jnp.arange(n_blocks) * MOE_BLOCK, side='right'), N_EXPERTS - 1)

    def expert(args):
        xb, e = args
        gu = xb @ w_gu[e]
        return (jax.nn.silu(gu[:, :D_EXPERT]) * gu[:, D_EXPERT:]) @ w_dn[e]

    ybuf = lax.map(expert, (xbuf.reshape(n_blocks, MOE_BLOCK, D), blk_e)).reshape(-1, D)
    return jax.ops.segment_sum(ybuf[dest] * w_s[:, None].astype(x.dtype), tok_s, num_segments=T)


def kernel(x_prompt, x_sample, cache_nsa_kv, state_win_kv, page_table, rel_bias, ln_mix, ln_ffn,
           nsa_w_in, nsa_w_out, nsa_q_gain, nsa_k_gain, cmp_pe, cmp_w1, cmp_w2,
           gm_w_in, gm_b_in, gm_ln_g, gm_ln_b, gm_w_s, gm_b_s, gm_w_out,
           moe_w_grp, moe_b_grp, moe_w_exp, moe_b_exp, moe_w_gu, moe_w_dn):
    xp = x_prompt
    xs = x_sample
    kv_p, kv_s, win_p, win_s, gv_p, gv_s = [], [], [], [], [], []
    for i in range(DEPTH):
        a = i // N_MIXERS
        if i % N_MIXERS == 0:
            nsa = (ln_mix[i], nsa_w_in[a], nsa_w_out[a], nsa_q_gain[a], nsa_k_gain[a],
                   cmp_pe[a], cmp_w1[a], cmp_w2[a], rel_bias)
            xp, rp, wp = _nsa_prompt_layer(xp, *nsa)
            cmp_rows, sel_kvt = _page_gather(cache_nsa_kv, page_table, a)
            xs, rs, ws = _nsa_decode_layer(xs, cmp_rows, sel_kvt, state_win_kv[a], *nsa)
            kv_p.append(rp)
            kv_s.append(rs)
            win_p.append(wp)
            win_s.append(ws)
        else:
            gm = (ln_mix[i], gm_w_in[a], gm_b_in[a], gm_ln_g[a], gm_ln_b[a], gm_w_s[a], gm_b_s[a], gm_w_out[a])
            bp, sp = xp.shape[:2]
            bs_, ss = xs.shape[:2]
            xp2, vp = _gmlp_residual(xp.reshape(-1, D_MODEL), *gm, seq=sp)
            xs2, vs = _gmlp_residual(xs.reshape(-1, D_MODEL), *gm, seq=ss)
            xp = xp2.reshape(xp.shape)
            xs = xs2.reshape(xs.shape)
            start = ((sp - 1) // CHUNK) * CHUNK
            gv_p.append(vp[:, vp.shape[1] - (sp - start):])
            gv_s.append(vs.reshape(bs_, ss, GM_WIDTH))
        moe = (ln_ffn[i], moe_w_grp[i], moe_b_grp[i], moe_w_exp[i], moe_b_exp[i], moe_w_gu[i], moe_w_dn[i])
        xp = _hier_moe_residual(xp.reshape(-1, D_MODEL), *moe).reshape(xp.shape)
        xs = _hier_moe_residual(xs.reshape(-1, D_MODEL), *moe).reshape(xs.shape)
    new_kv_prompt = jnp.stack(kv_p, axis=2)
    new_kv_sample = jnp.stack(kv_s, axis=2)
    new_win_prompt = jnp.stack(win_p, axis=0)
    new_win_sample = jnp.stack(win_s, axis=0)
    new_gm_v_prompt = jnp.stack(gv_p, axis=0)
    new_gm_v_sample = jnp.stack(gv_s, axis=0)
    return (xp, xs, new_kv_prompt, new_kv_sample, new_win_prompt, new_win_sample, new_gm_v_prompt, new_gm_v_sample)
```

```python
import functools
import math

import jax
import jax.numpy as jnp
import numpy as np
from jax import lax
from jax.experimental import pallas as pl
from jax.experimental.pallas import tpu as pltpu

D_MODEL = 1024
PAGE_SIZE = 128
DEPTH = 2
N_MIXERS = 2
N_HEADS = 16
HEAD_DIM = 64
KV_HEADS = 4
Q_PER_KV = N_HEADS // KV_HEADS
CMP_LEN = 32
CMP_STRIDE = 16
CMP_HID = 64
SEL_BLOCK = 64
CMP_PER_SEL = SEL_BLOCK // CMP_STRIDE
N_SEL = 16
N_LOCAL = 2
WINDOW = 512
Q_BLOCK = 128
N_BUCKETS = 32
MAX_DISTANCE = 128
CHUNK = 128
GM_WIDTH = 2048
GM_GROUPS = 8
GM_GROUP_DIM = GM_WIDTH // GM_GROUPS
N_GROUPS = 4
EXPERTS_PER_GROUP = 8
N_EXPERTS = N_GROUPS * EXPERTS_PER_GROUP
TOP_K = 2
D_EXPERT = 512
MOE_BLOCK = 128
EPS = 1e-6


def _mm_kernel(x_ref, w_ref, o_ref):
    o_ref[...] = jnp.dot(x_ref[...].astype(jnp.bfloat16), w_ref[...].astype(jnp.bfloat16),
                         preferred_element_type=jnp.float32)


def _mm(x, w, tm=512, tn=512):
    m, k = x.shape
    n = w.shape[1]
    tm = min(tm, m)
    tn = min(tn, n)
    return pl.pallas_call(
        _mm_kernel,
        grid=(m // tm, n // tn),
        in_specs=[pl.BlockSpec((tm, k), lambda i, j: (i, 0)),
                  pl.BlockSpec((k, tn), lambda i, j: (0, j))],
        out_specs=pl.BlockSpec((tm, tn), lambda i, j: (i, j)),
        out_shape=jax.ShapeDtypeStruct((m, n), jnp.float32),
        name="mm",
    )(x, w)


NEG = -1e30
MXU_DTYPE = jnp.bfloat16
TQ = 256
TK = 256
GROUP_CHUNKS = 3
LANES = 128
BIAS_SPAN = 128
LOG2E = 1.4426950408889634


def _bucket_of_distance():
    d = np.arange(BIAS_SPAN)
    max_exact = N_BUCKETS // 2
    nf = np.maximum(d, 1).astype(np.float32)
    large = max_exact + (np.log(nf / np.float32(max_exact)) / np.float32(math.log(MAX_DISTANCE / max_exact))
                         * np.float32(N_BUCKETS - max_exact)).astype(np.int32)
    large = np.minimum(large, N_BUCKETS - 1)
    return np.where(d < max_exact, d, large).astype(np.int32)


def _attn_bias_tiles(rel_bias, seq):
    fd = rel_bias[_bucket_of_distance()].T
    far = fd[:, BIAS_SPAN - 1]
    fd = fd - far[:, None]

    def by_distance(d):
        return jnp.where(d >= 0, LOG2E * jnp.take(fd, np.clip(d, 0, BIAS_SPAN - 1), axis=1), NEG)

    def toeplitz(first_distance):
        c = np.arange(2 * TK)
        g = by_distance(first_distance - np.where(c < TK, c, c - 2 * TK))
        flat = jnp.tile(g, (1, TQ))[:, :TQ * (2 * TK - 1)]
        return flat.reshape(-1, TQ, 2 * TK - 1)[:, :, :TK]

    b0 = toeplitz(0)
    b1 = toeplitz(TQ)
    i = np.arange(TQ)[:, None]
    j = np.arange(TK)[None, :]
    w2 = np.where(2 * TQ + i - j <= WINDOW, 0.0, NEG).astype(np.float32)
    nc = seq // CMP_STRIDE
    span = -(-(BIAS_SPAN + TQ) // CMP_STRIDE)
    r = np.arange(nc - span, nc + span)[:, None]
    dc = np.arange(TQ)[None, :] - CMP_STRIDE * (r - nc) - (CMP_LEN - 1)
    n_heads = fd.shape[0]
    pat = jnp.concatenate([jnp.zeros((n_heads, nc - span, TQ), jnp.float32), by_distance(dc),
                           jnp.full((n_heads, nc - span, TQ), NEG, jnp.float32)], axis=1)
    return b0, b1, jnp.asarray(w2), pat


def _nsa_attn_kernel(q_ref, kc_ref, vct_ref, kat_ref, vsa_ref, kwt_ref, vwa_ref, g_ref,
                     b0_ref, b1_ref, w2_ref, pat_ref, o_ref, m_sc, acc_sc, imp_sc, qa_sc, p_sc, a_sc, *, nc, nb):
    f32 = jnp.float32
    cd = MXU_DTYPE
    qb = pl.program_id(2)

    def compress_and_select(ncu, all_forced):
        nbu = ncu // CMP_PER_SEL
        r0 = pl.multiple_of(nc - (TQ // CMP_STRIDE) * qb, TQ // CMP_STRIDE)
        t_row = qb * TQ + lax.broadcasted_iota(jnp.int32, (1, TQ), 1)
        has_cmp = t_row >= CMP_LEN - 1
        for h in range(Q_PER_KV):
            st = lax.dot_general(kc_ref[0, 0, 0:ncu, :], q_ref[0, h], (((1,), (1,)), ((), ())),
                                 preferred_element_type=f32)
            st = st + pat_ref[h, pl.ds(r0, ncu), :]
            e = jnp.exp2(st - jnp.max(st, axis=0, keepdims=True))
            inv = jnp.where(has_cmp, 1.0 / jnp.sum(e, axis=0, keepdims=True), 0.0)
            pt = e * inv
            for c in range(TQ // LANES):
                part = pt[:, c * LANES:(c + 1) * LANES]
                if h == 0:
                    imp_sc[c, 0:ncu, :] = part
                else:
                    imp_sc[c, 0:ncu, :] += part
            oct_h = jnp.dot(vct_ref[0, 0, :, 0:ncu], pt.astype(cd), preferred_element_type=f32)
            o_ref[0, :, h * HEAD_DIM:(h + 1) * HEAD_DIM] = g_ref[0][:, 3 * h:3 * h + 1] * oct_h.T

        imp = jnp.concatenate(
            [sum(imp_sc[c, pl.ds(r, nbu, stride=CMP_PER_SEL), :] for r in range(CMP_PER_SEL))
             for c in range(TQ // LANES)], axis=1)
        blk = lax.broadcasted_iota(jnp.int32, (nbu, TQ), 0)
        t_blk = (qb * TQ + lax.broadcasted_iota(jnp.int32, (nbu, TQ), 1)) // SEL_BLOCK
        dd = t_blk - blk
        forced = (blk == 0) | ((dd >= 0) & (dd < N_LOCAL))
        if all_forced:
            score = jnp.where(forced | (dd < 0), -jnp.inf, imp)
            pen_t = jnp.where(forced, 0.0, NEG)
            rounds = N_SEL - (N_LOCAL + 1)
        else:
            score = jnp.where(forced, jnp.inf, jnp.where(dd >= 0, imp, -jnp.inf))
            pen_t = jnp.full((nbu, TQ), NEG, f32)
            rounds = min(N_SEL, nbu)
        for _ in range(rounds):
            best = jnp.max(score, axis=0, keepdims=True)
            first = jnp.min(jnp.where(score == best, blk, nbu), axis=0, keepdims=True)
            hit = blk == first
            pen_t = jnp.where(hit, 0.0, pen_t)
            score = jnp.where(hit, -jnp.inf, score)
        if nbu < nb:
            pen_t = jnp.concatenate([pen_t, jnp.full((nb - nbu, TQ), NEG, f32)], axis=0)
        pen = pen_t.T.astype(cd)
        pad = jnp.zeros((TQ, qa_sc.shape[2] - nb - HEAD_DIM), cd)
        for h in range(Q_PER_KV):
            qa_sc[h] = jnp.concatenate([pen, q_ref[0, h], pad], axis=1)

    n_tiles = nc * CMP_STRIDE // TQ
    n_var = 4 if (n_tiles % 4 == 0 and (nc // 4) % (2 * CMP_STRIDE) == 0) else 1
    for v in range(n_var):
        lo, hi = v * n_tiles // n_var, (v + 1) * n_tiles // n_var
        ncu = (v + 1) * nc // n_var
        if v == 0:
            @pl.when(qb == 0)
            def _():
                compress_and_select(ncu, all_forced=False)
            lo = 1

        @pl.when((qb >= lo) & (qb < hi))
        def _():
            compress_and_select(ncu, all_forced=TQ >= N_LOCAL * SEL_BLOCK)


    def reset():
        m_sc[...] = jnp.full(m_sc.shape, NEG, f32)
        acc_sc[...] = jnp.zeros(acc_sc.shape, f32)

    def run(lhs, k_ref, v_ref, chunks):
        for slot, (c, bias) in enumerate(chunks):
            for h in range(Q_PER_KV):
                s = jnp.dot(lhs(h), k_ref[0, 0, c], preferred_element_type=f32)
                if bias is not None:
                    s = s + bias(h)
                m_prev = m_sc[h]
                m_new = jnp.maximum(m_prev, jnp.max(s, axis=1, keepdims=True))
                p_sc[slot, h] = jnp.exp2(s - jnp.concatenate([m_new] * (TK // LANES), axis=1)).astype(cd)
                a_sc[slot, h] = jnp.exp2(m_prev - m_new)
                m_sc[h] = m_new
        for slot, (c, _) in enumerate(chunks):
            for h in range(Q_PER_KV):
                acc_sc[h] = a_sc[slot, h] * acc_sc[h] + jnp.dot(p_sc[slot, h], v_ref[0, 0, c],
                                                                preferred_element_type=f32)

    def emit(gate_col):
        for h in range(Q_PER_KV):
            acc = acc_sc[h]
            w = g_ref[0][:, 3 * h + gate_col:3 * h + gate_col + 1] / acc[:, HEAD_DIM:HEAD_DIM + 1]
            o_ref[0, :, h * HEAD_DIM:(h + 1) * HEAD_DIM] += w * acc[:, :HEAD_DIM]

    own = lambda h: b0_ref[h]
    prev = lambda h: b1_ref[h]
    edge = lambda h: w2_ref[...]

    sel_lhs = lambda h: qa_sc[h]
    reset()
    n_far = jnp.maximum(qb - 1, 0)

    def far_group(i, carry):
        run(sel_lhs, kat_ref, vsa_ref, [(GROUP_CHUNKS * i + j, None) for j in range(GROUP_CHUNKS)])
        return carry

    lax.fori_loop(0, n_far // GROUP_CHUNKS, far_group, 0)
    for left in range(1, GROUP_CHUNKS):
        @pl.when(n_far % GROUP_CHUNKS == left)
        def _():
            run(sel_lhs, kat_ref, vsa_ref, [(n_far - left + j, None) for j in range(left)])

    @pl.when(qb >= 1)
    def _():
        run(sel_lhs, kat_ref, vsa_ref, [(qb - 1, prev), (qb, own)])

    @pl.when(qb == 0)
    def _():
        run(sel_lhs, kat_ref, vsa_ref, [(qb, own)])

    emit(1)

    win_lhs = lambda h: q_ref[0, h]
    reset()

    @pl.when(qb >= 2)
    def _():
        run(win_lhs, kwt_ref, vwa_ref, [(qb - 2, edge), (qb - 1, prev), (qb, own)])

    @pl.when(qb == 1)
    def _():
        run(win_lhs, kwt_ref, vwa_ref, [(qb - 1, prev), (qb, own)])

    @pl.when(qb == 0)
    def _():
        run(win_lhs, kwt_ref, vwa_ref, [(qb, own)])

    emit(2)


def _nsa_attention(qs, kcp, vct, kat, vsa, kwt, vwa, gt, rel_bias):
    B, _, S, _ = qs.shape
    assert S % TQ == 0 and TQ == TK and WINDOW == 2 * TQ and S // SEL_BLOCK >= N_SEL
    nc = S // CMP_STRIDE
    nb = S // SEL_BLOCK
    nch = S // TK
    ka = kat.shape[3]
    b0, b1, w2, pat = _attn_bias_tiles(rel_bias, S)
    G = Q_PER_KV
    kv_chunks = pl.BlockSpec((1, 1, nch, HEAD_DIM, TK), lambda b, k, i: (b, k, 0, 0, 0))
    ka_chunks = pl.BlockSpec((1, 1, nch, ka, TK), lambda b, k, i: (b, k, 0, 0, 0))
    v_chunks = pl.BlockSpec((1, 1, nch, TK, LANES), lambda b, k, i: (b, k, 0, 0, 0))
    return pl.pallas_call(
        functools.partial(_nsa_attn_kernel, nc=nc, nb=nb),
        grid=(B, KV_HEADS, S // TQ),
        in_specs=[
            pl.BlockSpec((1, G, TQ, HEAD_DIM), lambda b, k, i: (b, k, i, 0)),
            pl.BlockSpec((1, 1, nc, HEAD_DIM), lambda b, k, i: (b, k, 0, 0)),
            pl.BlockSpec((1, 1, HEAD_DIM, nc), lambda b, k, i: (b, k, 0, 0)),
            ka_chunks, v_chunks, kv_chunks, v_chunks,
            pl.BlockSpec((1, TQ, LANES), lambda b, k, i: (b, i, k)),
            pl.BlockSpec((G, TQ, TK), lambda b, k, i: (k, 0, 0)),
            pl.BlockSpec((G, TQ, TK), lambda b, k, i: (k, 0, 0)),
            pl.BlockSpec((TQ, TK), lambda b, k, i: (0, 0)),
            pl.BlockSpec((G, 2 * nc, TQ), lambda b, k, i: (k, 0, 0)),
        ],
        out_specs=pl.BlockSpec((1, TQ, G * HEAD_DIM), lambda b, k, i: (b, i, k)),
        out_shape=jax.ShapeDtypeStruct((B, S, N_HEADS * HEAD_DIM), jnp.float32),
        scratch_shapes=[
            pltpu.VMEM((G, TQ, LANES), jnp.float32),
            pltpu.VMEM((G, TQ, LANES), jnp.float32),
            pltpu.VMEM((TQ // LANES, nc, LANES), jnp.float32),
            pltpu.VMEM((G, TQ, ka), MXU_DTYPE),
            pltpu.VMEM((GROUP_CHUNKS, G, TQ, TK), MXU_DTYPE),
            pltpu.VMEM((GROUP_CHUNKS, G, TQ, LANES), jnp.float32),
        ],
        compiler_params=pltpu.CompilerParams(
            dimension_semantics=("arbitrary", "arbitrary", "arbitrary"),
            vmem_limit_bytes=56 * 1024 * 1024),
        name="nsa_attn",
    )(qs, kcp, vct, kat, vsa, kwt, vwa, gt, b0, b1, w2, pat)


def _nsa_prompt_layer(x, ln_g, w_in, w_out, q_gain, k_gain, pe, w1, w2, rel_bias):
    B, S, D = x.shape
    x2 = x.reshape(B * S, D)
    qs, rows, win, kat, vsa, kwt, vwa, c, gates = _nsa_project(x2, ln_g, w_in, q_gain, k_gain, B, S)
    nc = S // CMP_STRIDE
    cmp = _compress(c.reshape(2, B, nc, CMP_STRIDE * SLOT), pe, w1, w2, k_gain[0])
    cmp = cmp.reshape(2, B, nc, KV_HEADS, HEAD_DIM).astype(MXU_DTYPE)
    gt = gates.reshape(B, S, KV_HEADS * LANES)
    o = _nsa_attention(qs, cmp[0].transpose(0, 2, 1, 3), cmp[1].transpose(0, 2, 3, 1), kat, vsa, kwt, vwa, gt,
                       rel_bias)
    y = _proj_residual(x2, o.reshape(B * S, HQ), w_out)
    return (y.reshape(B, S, D), rows.reshape(B, S, 4, KV_HEADS, HEAD_DIM),
            win.reshape(B, WINDOW, 2, KV_HEADS, HEAD_DIM))


PAGES_PER_STEP = 8


def _page_gather_kernel(pt_ref, *refs):
    del pt_ref
    pages = refs[:PAGES_PER_STEP]
    c_ref, kvt_ref, rows_sc = refs[PAGES_PER_STEP:]
    for p, pg in enumerate(pages):
        pos = slice(p * PAGE_SIZE, (p + 1) * PAGE_SIZE)
        for slot in range(2):
            for pair in range(KV_HEADS // 2):
                tile = pg[0, 0, slot, 2 * pair:2 * pair + 2].reshape(LANES, PAGE_SIZE)
                rows_sc[slot * 2 + pair, pos, :] = tile.T
        for slot in range(2):
            kvt_ref[slot, 0, :, :, pos] = pg[0, 0, 2 + slot].astype(MXU_DTYPE)
    n_chunks = PAGES_PER_STEP * PAGE_SIZE // CMP_STRIDE
    for l in range(CMP_STRIDE):
        for t in range(4):
            lo = l * SLOT + (t % 2) * LANES
            c_ref[t // 2, 0, :, lo:lo + LANES] = rows_sc[t, pl.ds(l, n_chunks, stride=CMP_STRIDE), :].astype(MXU_DTYPE)


def _page_gather(cache, page_table, layer):
    nseq, n_pages = page_table.shape
    assert n_pages % PAGES_PER_STEP == 0 and 2 * LANES == SLOT
    P = n_pages * PAGE_SIZE
    cache_t = cache.transpose(0, 2, 3, 4, 5, 1)
    page_specs = [
        pl.BlockSpec((1, 1, 4, KV_HEADS, HEAD_DIM, PAGE_SIZE),
                     functools.partial(lambda r, n, j, pt: (pt[n, PAGES_PER_STEP * j + r], layer, 0, 0, 0, 0), r))
        for r in range(PAGES_PER_STEP)]
    rows_per_step = PAGES_PER_STEP * PAGE_SIZE
    return pl.pallas_call(
        _page_gather_kernel,
        grid_spec=pltpu.PrefetchScalarGridSpec(
            num_scalar_prefetch=1,
            grid=(nseq, n_pages // PAGES_PER_STEP),
            in_specs=page_specs,
            out_specs=[pl.BlockSpec((2, 1, rows_per_step // CMP_STRIDE, CMP_STRIDE * SLOT),
                                    lambda n, j, pt: (0, n, j, 0)),
                       pl.BlockSpec((2, 1, KV_HEADS, HEAD_DIM, rows_per_step), lambda n, j, pt: (0, n, 0, 0, j))],
            scratch_shapes=[pltpu.VMEM((4, rows_per_step, LANES), jnp.float32)]),
        out_shape=[jax.ShapeDtypeStruct((2, nseq, P // CMP_STRIDE, CMP_STRIDE * SLOT), MXU_DTYPE),
                   jax.ShapeDtypeStruct((2, nseq, KV_HEADS, HEAD_DIM, P), MXU_DTYPE)],
        compiler_params=pltpu.CompilerParams(dimension_semantics=("arbitrary", "arbitrary")),
        name="page_gather",
    )(page_table, *([cache_t] * PAGES_PER_STEP))


def _nsa_decode_kernel(q_ref, qt_ref, kc_ref, vc_ref, kst_ref, vst_ref, win_ref, rows_ref, wnew_ref, g_ref,
                       oh_ref, bc_ref, bs_ref, bw_ref, bn_ref, o_ref, pen_sc, s_sc, *, nb):
    f32 = jnp.float32
    cd = MXU_DTYPE
    nt = (((1,), (1,)), ((), ()))
    pen_sc[...] = jnp.zeros(pen_sc.shape, f32)
    rows = rows_ref[0]
    wnew = wnew_ref[0]

    def softmax_with_new(s, s_new):
        m = jnp.maximum(jnp.max(s, axis=1, keepdims=True), s_new)
        e = jnp.exp(s - m)
        e_new = jnp.exp(s_new - m)
        return e, e_new, 1.0 / (jnp.sum(e, axis=1, keepdims=True) + e_new)

    def bf(x):
        return x.astype(cd).astype(f32)

    for k in range(KV_HEADS):
        hs = slice(k * Q_PER_KV, (k + 1) * Q_PER_KV)
        q = q_ref[0, hs, :]
        gates = g_ref[0, hs, :]
        sc = lax.dot_general(q, kc_ref[0, k], nt, preferred_element_type=f32) + bc_ref[hs, :]
        ec = jnp.exp(sc - jnp.max(sc, axis=1, keepdims=True))
        pc = ec / jnp.sum(ec, axis=1, keepdims=True)
        o_ref[0, hs, :] = gates[:, 0:1] * jnp.dot(pc.astype(cd), vc_ref[0, k], preferred_element_type=f32)

        imp = sum(pc[:, r * nb:(r + 1) * nb] for r in range(CMP_PER_SEL))
        imp = jnp.sum(imp, axis=0, keepdims=True)
        blk = lax.broadcasted_iota(jnp.int32, imp.shape, 1)
        forced = (blk == 0) | (blk == nb - 1)
        score = jnp.where(forced, jnp.inf, imp)
        pen = jnp.full(imp.shape, NEG, f32)
        for _ in range(N_SEL - 1):
            best = jnp.max(score, axis=1, keepdims=True)
            first = jnp.min(jnp.where(score == best, blk, nb), axis=1, keepdims=True)
            hit = blk == first
            pen = jnp.where(hit, 0.0, pen)
            score = jnp.where(hit, -jnp.inf, score)
        pen_sc[k:k + 1, :] = pen

    key_pen = jnp.dot(pen_sc[...].astype(cd), oh_ref[...], preferred_element_type=f32)

    lane = lax.broadcasted_iota(jnp.int32, (HEAD_DIM, LANES), 1)
    for k in range(KV_HEADS):
        hs = slice(k * Q_PER_KV, (k + 1) * Q_PER_KV)
        lanes = slice(k * HEAD_DIM, (k + 1) * HEAD_DIM)
        q = q_ref[0, hs, :]
        qf = q.astype(f32)
        gates = g_ref[0, hs, :]

        def new_key(row):
            return jnp.sum(qf * bf(row), axis=1, keepdims=True) + bn_ref[hs, :]

        kt = kst_ref[0, 0, k].astype(f32)
        for g in range(Q_PER_KV):
            qcol = qt_ref[0][:, k * Q_PER_KV + g:k * Q_PER_KV + g + 1].astype(f32)
            s_sc[g:g + 1, :] = jnp.sum(kt * qcol, axis=0, keepdims=True)
        ss = s_sc[0:Q_PER_KV, :] + key_pen[k:k + 1, :] + bs_ref[hs, :]
        e, e_new, inv = softmax_with_new(ss, new_key(rows[:, 2 * SLOT:3 * SLOT][:, lanes]))
        vt = vst_ref[0, 0, k].astype(f32)
        eb = bf(e)
        acc_t = jnp.zeros((HEAD_DIM, LANES), f32)
        for g in range(Q_PER_KV):
            acc_t = jnp.where(lane == g, jnp.sum(vt * eb[g:g + 1, :], axis=1, keepdims=True), acc_t)
        acc = acc_t.T[0:Q_PER_KV, :] + bf(e_new) * bf(rows[:, 3 * SLOT:4 * SLOT][:, lanes])
        out = o_ref[0, hs, :] + gates[:, 1:2] * inv * acc

        sw = lax.dot_general(q, win_ref[0, 0, k], nt, preferred_element_type=f32) + bw_ref[hs, :]
        e, e_new, inv = softmax_with_new(sw, new_key(wnew[:, :SLOT][:, lanes]))
        acc = (jnp.dot(e.astype(cd), win_ref[0, 1, k], preferred_element_type=f32)
               + bf(e_new) * bf(wnew[:, SLOT:][:, lanes]))
        o_ref[0, hs, :] = out + gates[:, 2:3] * inv * acc


def _nsa_decode_layer(x, cmp_rows, sel_kvt, win_buf, ln_g, w_in, w_out, q_gain, k_gain, pe, w1, w2, rel_bias):
    N, L, D = x.shape
    P = sel_kvt.shape[-1]
    wb = win_buf.shape[1]
    assert L == 1 and P % SEL_BLOCK == 0 and wb == WINDOW and P >= BIAS_SPAN and P // SEL_BLOCK >= N_SEL - 1
    f32 = jnp.float32
    cd = MXU_DTYPE
    nc = P // CMP_STRIDE
    nb = P // SEL_BLOCK
    x2 = x.reshape(N, D)
    q, rows, wnew, gates = _nsa_project(x2, ln_g, w_in, q_gain, k_gain, N, 1)
    cmp = _compress(cmp_rows, pe, w1, w2, k_gain[0])
    cmp = cmp.reshape(2, N, nb, CMP_PER_SEL, KV_HEADS, HEAD_DIM).transpose(0, 1, 4, 3, 2, 5)
    cmp = cmp.reshape(2, N, KV_HEADS, nc, HEAD_DIM).astype(cd)
    qs = (q.reshape(N, N_HEADS, HEAD_DIM) * HEAD_DIM ** -0.5).astype(cd)
    win = win_buf.transpose(0, 2, 3, 1, 4).astype(cd)
    onehot = jnp.asarray(np.arange(nb)[:, None] == (np.arange(P) // SEL_BLOCK)[None, :], dtype=cd)
    fd = rel_bias[_bucket_of_distance()].T
    fd = fd - fd[:, BIAS_SPAN - 1:]
    cidx = (np.arange(nc) % nb) * CMP_PER_SEL + np.arange(nc) // nb
    dc = P - (cidx * CMP_STRIDE + CMP_LEN - 1)
    b_cmp = jnp.where(dc >= 0, jnp.take(fd, np.clip(dc, 0, BIAS_SPAN - 1), axis=1), NEG)
    b_sel = jnp.take(fd, np.clip(P - np.arange(P), 0, BIAS_SPAN - 1), axis=1)
    b_win = jnp.take(fd, np.clip(wb - np.arange(wb), 0, BIAS_SPAN - 1), axis=1)
    b_new = fd[:, 0:1]
    seq3 = lambda n: (n, 0, 0)
    seq4 = lambda n: (n, 0, 0, 0)
    fixed = lambda n: (0, 0)
    o = pl.pallas_call(
        functools.partial(_nsa_decode_kernel, nb=nb),
        grid=(N,),
        in_specs=[pl.BlockSpec((1, N_HEADS, HEAD_DIM), seq3),
                  pl.BlockSpec((1, HEAD_DIM, N_HEADS), seq3),
                  pl.BlockSpec((1, KV_HEADS, nc, HEAD_DIM), seq4),
                  pl.BlockSpec((1, KV_HEADS, nc, HEAD_DIM), seq4),
                  pl.BlockSpec((1, 1, KV_HEADS, HEAD_DIM, P), lambda n: (0, n, 0, 0, 0)),
                  pl.BlockSpec((1, 1, KV_HEADS, HEAD_DIM, P), lambda n: (1, n, 0, 0, 0)),
                  pl.BlockSpec((1, 2, KV_HEADS, wb, HEAD_DIM), lambda n: (n, 0, 0, 0, 0)),
                  pl.BlockSpec((1, 1, 4 * SLOT), seq3),
                  pl.BlockSpec((1, 1, 2 * SLOT), seq3),
                  pl.BlockSpec((1, N_HEADS, 3), seq3),
                  pl.BlockSpec((nb, P), fixed), pl.BlockSpec((N_HEADS, nc), fixed),
                  pl.BlockSpec((N_HEADS, P), fixed), pl.BlockSpec((N_HEADS, wb), fixed),
                  pl.BlockSpec((N_HEADS, 1), fixed)],
        out_specs=pl.BlockSpec((1, N_HEADS, HEAD_DIM), seq3),
        out_shape=jax.ShapeDtypeStruct((N, N_HEADS, HEAD_DIM), f32),
        scratch_shapes=[pltpu.VMEM((8, nb), f32), pltpu.VMEM((8, P), f32)],
        compiler_params=pltpu.CompilerParams(dimension_semantics=("arbitrary",),
                                             vmem_limit_bytes=48 * 1024 * 1024),
        name="nsa_decode_attn",
    )(qs, qs.transpose(0, 2, 1), cmp[0], cmp[1], sel_kvt, sel_kvt, win, rows.reshape(N, 1, 4 * SLOT), wnew.reshape(N, 1, 2 * SLOT),
      gates[:, :3 * N_HEADS].reshape(N, N_HEADS, 3), onehot, b_cmp, b_sel, b_win, b_new)
    y = _proj_residual(x2, o.reshape(N, HQ), w_out)
    new_win = jnp.concatenate([win_buf, wnew.reshape(N, 1, 2, KV_HEADS, HEAD_DIM)], axis=1)[:, 1:]
    return y.reshape(N, 1, D), rows.reshape(N, 1, 4, KV_HEADS, HEAD_DIM), new_win


def _split3(x):
    hi = x.astype(MXU_DTYPE)
    r1 = x - hi.astype(jnp.float32)
    mid = r1.astype(MXU_DTYPE)
    return hi, mid, (r1 - mid.astype(jnp.float32)).astype(MXU_DTYPE)


def _nsa_sample_kernel(q_ref, kc_ref, vc_ref, ks_ref, vs_ref, win_ref, rows_ref, wnew_ref, g_ref,
                       oh_ref, grp_ref, own_ref, bc_ref, bs_ref, bw_ref, bn_ref, o_ref, *, nb):
    f32 = jnp.float32
    cd = MXU_DTYPE
    nt = (((1,), (1,)), ((), ()))
    q = q_ref[0]
    qf = q.astype(f32)

    def softmax_with_new(s, s_new):
        m = jnp.maximum(jnp.max(s, axis=1, keepdims=True), s_new)
        e = jnp.exp(s - m)
        e_new = jnp.exp(s_new - m)
        return e, e_new, 1.0 / (jnp.sum(e, axis=1, keepdims=True) + e_new)

    def new_key(row):
        return jnp.sum(qf * row.astype(cd).astype(f32), axis=1, keepdims=True) + bn_ref[...]

    sc = lax.dot_general(q, kc_ref[0], nt, preferred_element_type=f32) + bc_ref[...]
    ec = jnp.exp(sc - jnp.max(sc, axis=1, keepdims=True))
    pc = ec / jnp.sum(ec, axis=1, keepdims=True)
    out = g_ref[0][:, 0:1] * jnp.dot(pc.astype(cd), vc_ref[0], preferred_element_type=f32)

    imp = sum(pc[:, r * nb:(r + 1) * nb] for r in range(CMP_PER_SEL))
    imp = sum(jnp.dot(grp_ref[...], part, preferred_element_type=f32) for part in _split3(imp))
    blk = lax.broadcasted_iota(jnp.int32, imp.shape, 1)
    forced = (blk == 0) | (blk == nb - 1)
    score = jnp.where(forced, jnp.inf, imp)
    pen = jnp.full(imp.shape, NEG, f32)
    for _ in range(N_SEL - 1):
        best = jnp.max(score, axis=1, keepdims=True)
        first = jnp.min(jnp.where(score == best, blk, nb), axis=1, keepdims=True)
        hit = blk == first
        pen = jnp.where(hit, 0.0, pen)
        score = jnp.where(hit, -jnp.inf, score)

    ss = (lax.dot_general(q, ks_ref[0, 0].astype(cd), nt, preferred_element_type=f32)
          + jnp.dot(pen.astype(cd), oh_ref[...], preferred_element_type=f32) + bs_ref[...])
    rows = rows_ref[0]
    e, e_new, inv = softmax_with_new(ss, new_key(rows[:, 2 * SLOT:3 * SLOT]))
    acc = (jnp.dot(e.astype(cd), vs_ref[0, 0].astype(cd), preferred_element_type=f32)
           + e_new.astype(cd).astype(f32) * rows[:, 3 * SLOT:4 * SLOT].astype(cd).astype(f32))
    out = out + g_ref[0][:, 1:2] * inv * acc

    win = win_ref[0]
    wnew = wnew_ref[0]
    sw = lax.dot_general(q, win[:, :SLOT].astype(cd), nt, preferred_element_type=f32) + bw_ref[...]
    e, e_new, inv = softmax_with_new(sw, new_key(wnew[:, :SLOT]))
    acc = (jnp.dot(e.astype(cd), win[:, SLOT:].astype(cd), preferred_element_type=f32)
           + e_new.astype(cd).astype(f32) * wnew[:, SLOT:].astype(cd).astype(f32))
    out = out + g_ref[0][:, 2:3] * inv * acc
    o_ref[0] = out * own_ref[...]


def _nsa_sample_layer(x, past, win_buf, ln_g, w_in, w_out, q_gain, k_gain, pe, w1, w2, rel_bias):
    N, L, D = x.shape
    P = past.shape[2]
    wb = win_buf.shape[1]
    assert L == 1 and P % SEL_BLOCK == 0 and wb == WINDOW and P >= BIAS_SPAN and P // SEL_BLOCK >= N_SEL - 1
    f32 = jnp.float32
    cd = MXU_DTYPE
    nc = P // CMP_STRIDE
    nb = P // SEL_BLOCK
    x2 = x.reshape(N, D)
    q, rows, wnew, gates = _nsa_project(x2, ln_g, w_in, q_gain, k_gain, N, 1)
    cmp = _compress(past.reshape(4, N, nc, CMP_STRIDE * SLOT), pe, w1, w2, k_gain[0])
    cmp = cmp.reshape(2, N, nb, CMP_PER_SEL, SLOT).transpose(0, 1, 3, 2, 4).reshape(2, N, nc, SLOT).astype(cd)
    head_kv = np.arange(N_HEADS) // Q_PER_KV
    own = (head_kv[:, None] == (np.arange(SLOT) // HEAD_DIM)[None, :]).astype(np.float32)
    qbd = (jnp.tile(q.reshape(N, N_HEADS, HEAD_DIM), (1, 1, KV_HEADS)) * own * HEAD_DIM ** -0.5).astype(cd)
    grp = jnp.asarray(head_kv[:, None] == head_kv[None, :], dtype=cd)
    onehot = jnp.asarray(np.arange(nb)[:, None] == (np.arange(P) // SEL_BLOCK)[None, :], dtype=cd)
    fd = rel_bias[_bucket_of_distance()].T
    fd = fd - fd[:, BIAS_SPAN - 1:]
    cidx = (np.arange(nc) % nb) * CMP_PER_SEL + np.arange(nc) // nb
    dc = P - (cidx * CMP_STRIDE + CMP_LEN - 1)
    b_cmp = jnp.where(dc >= 0, jnp.take(fd, np.clip(dc, 0, BIAS_SPAN - 1), axis=1), NEG)
    b_sel = jnp.take(fd, np.clip(P - np.arange(P), 0, BIAS_SPAN - 1), axis=1)
    b_win = jnp.take(fd, np.clip(wb - np.arange(wb), 0, BIAS_SPAN - 1), axis=1)
    b_new = fd[:, 0:1]
    seq3 = lambda n: (n, 0, 0)
    fixed = lambda n: (0, 0)
    o = pl.pallas_call(
        functools.partial(_nsa_sample_kernel, nb=nb),
        grid=(N,),
        in_specs=[pl.BlockSpec((1, N_HEADS, SLOT), seq3),
                  pl.BlockSpec((1, nc, SLOT), seq3),
                  pl.BlockSpec((1, nc, SLOT), seq3),
                  pl.BlockSpec((1, 1, P, SLOT), lambda n: (2, n, 0, 0)),
                  pl.BlockSpec((1, 1, P, SLOT), lambda n: (3, n, 0, 0)),
                  pl.BlockSpec((1, wb, 2 * SLOT), seq3),
                  pl.BlockSpec((1, 1, 4 * SLOT), seq3),
                  pl.BlockSpec((1, 1, 2 * SLOT), seq3),
                  pl.BlockSpec((1, N_HEADS, 3), seq3),
                  pl.BlockSpec((nb, P), fixed), pl.BlockSpec((N_HEADS, N_HEADS), fixed),
                  pl.BlockSpec((N_HEADS, SLOT), fixed), pl.BlockSpec((N_HEADS, nc), fixed),
                  pl.BlockSpec((N_HEADS, P), fixed), pl.BlockSpec((N_HEADS, wb), fixed),
                  pl.BlockSpec((N_HEADS, 1), fixed)],
        out_specs=pl.BlockSpec((1, N_HEADS, SLOT), seq3),
        out_shape=jax.ShapeDtypeStruct((N, N_HEADS, SLOT), f32),
        compiler_params=pltpu.CompilerParams(dimension_semantics=("arbitrary",),
                                             vmem_limit_bytes=56 * 1024 * 1024),
        name="nsa_sample_attn",
    )(qbd, cmp[0], cmp[1], past, past, win_buf.reshape(N, wb, 2 * SLOT), rows.reshape(N, 1, 4 * SLOT),
      wnew.reshape(N, 1, 2 * SLOT), gates[:, :3 * N_HEADS].reshape(N, N_HEADS, 3),
      onehot, grp, jnp.asarray(own), b_cmp, b_sel, b_win, b_new)
    o = o.reshape(N, KV_HEADS, Q_PER_KV, KV_HEADS, HEAD_DIM)
    o = jnp.stack([o[:, k, :, k] for k in range(KV_HEADS)], axis=1).reshape(N, HQ)
    y = _proj_residual(x2, o, w_out)
    new_win = jnp.concatenate([win_buf, wnew.reshape(N, 1, 2, KV_HEADS, HEAD_DIM)], axis=1)[:, 1:]
    return y.reshape(N, 1, D), rows.reshape(N, 1, 4, KV_HEADS, HEAD_DIM), new_win


HQ = N_HEADS * HEAD_DIM
SLOT = KV_HEADS * HEAD_DIM
NSA_COLS = HQ + 6 * SLOT + 3 * N_HEADS
NSA_COLS_PAD = -(-NSA_COLS // LANES) * LANES
GATE_COLS = NSA_COLS_PAD - HQ - 6 * SLOT


def _nsa_proj_kernel(x_ref, g_ref, w_ref, qg_ref, kg_ref, *outs, prompt, n_blocks):
    f32 = jnp.float32
    x = x_ref[...]
    xn = x * lax.rsqrt(jnp.mean(x * x, axis=-1, keepdims=True) + EPS) * g_ref[...]
    h = jnp.dot(xn.astype(MXU_DTYPE), w_ref[...], preferred_element_type=f32)

    def head_norm(v, gain):
        return v * lax.rsqrt(jnp.mean(v * v, axis=-1, keepdims=True) + EPS) * gain

    def slot(s, kv):
        lo = HQ + s * SLOT + kv * HEAD_DIM
        return h[:, lo:lo + HEAD_DIM]

    gates = 1.0 / (1.0 + jnp.exp(-h[:, HQ + 6 * SLOT:]))
    if prompt:
        q_ref, rows_ref, win_ref, kat_ref, vsa_ref, kwt_ref, vwa_ref, c_ref, gate_ref = outs
        for hh in range(N_HEADS):
            qn = head_norm(h[:, hh * HEAD_DIM:(hh + 1) * HEAD_DIM], qg_ref[...])
            q_ref[0, hh] = (qn * (HEAD_DIM ** -0.5 * LOG2E)).astype(MXU_DTYPE)
        c_ref[0, 0] = h[:, HQ:HQ + SLOT].astype(MXU_DTYPE)
        c_ref[1, 0] = h[:, HQ + SLOT:HQ + 2 * SLOT].astype(MXU_DTYPE)
        rows, ka = x.shape[0], kat_ref.shape[3]
        nb = n_blocks
        first_key = (pl.program_id(0) % (nb * SEL_BLOCK // rows)) * rows
        key_blk = (first_key + lax.broadcasted_iota(jnp.int32, (nb, rows), 1)) // SEL_BLOCK
        onehot = jnp.where(key_blk == lax.broadcasted_iota(jnp.int32, (nb, rows), 0), 1.0, 0.0).astype(MXU_DTYPE)
        kpad = jnp.zeros((ka - nb - HEAD_DIM, rows), MXU_DTYPE)
        ones = jnp.ones((rows, LANES - HEAD_DIM), MXU_DTYPE)
    else:
        q_ref, rows_ref, win_ref, gate_ref = outs
        for hh in range(N_HEADS):
            q_ref[:, hh * HEAD_DIM:(hh + 1) * HEAD_DIM] = head_norm(h[:, hh * HEAD_DIM:(hh + 1) * HEAD_DIM],
                                                                    qg_ref[...])
    rows_ref[:, 0:2 * SLOT] = h[:, HQ:HQ + 2 * SLOT]
    rows_ref[:, 3 * SLOT:4 * SLOT] = h[:, HQ + 3 * SLOT:HQ + 4 * SLOT]
    for kv in range(KV_HEADS):
        lanes = slice(kv * HEAD_DIM, (kv + 1) * HEAD_DIM)
        ksn = head_norm(slot(2, kv), kg_ref[1:2, :])
        kwn = head_norm(slot(4, kv), kg_ref[2:3, :])
        rows_ref[:, 2 * SLOT + kv * HEAD_DIM:2 * SLOT + (kv + 1) * HEAD_DIM] = ksn
        if prompt:
            win_ref[0, 0, :, lanes] = kwn
            kat_ref[0, kv, 0] = jnp.concatenate([onehot, ksn.T.astype(MXU_DTYPE), kpad], axis=0)
            kwt_ref[0, kv, 0] = kwn.T.astype(MXU_DTYPE)
            vsa_ref[0, kv, 0] = jnp.concatenate([slot(3, kv).astype(MXU_DTYPE), ones], axis=1)
            vwa_ref[0, kv, 0] = jnp.concatenate([slot(5, kv).astype(MXU_DTYPE), ones], axis=1)
        else:
            win_ref[:, lanes] = kwn
    if prompt:
        win_ref[0, 0, :, SLOT:2 * SLOT] = h[:, HQ + 5 * SLOT:HQ + 6 * SLOT]
    else:
        win_ref[:, SLOT:2 * SLOT] = h[:, HQ + 5 * SLOT:HQ + 6 * SLOT]
    if prompt:
        gate_ref[...] = jnp.zeros(gate_ref.shape, f32)
        per_kv = 3 * Q_PER_KV
        for kv in range(KV_HEADS):
            gate_ref[:, kv * LANES:kv * LANES + per_kv] = gates[:, kv * per_kv:(kv + 1) * per_kv]
    else:
        gate_ref[...] = gates


def _nsa_project(x, ln_g, w_in, q_gain, k_gain, batch, seq):
    T, D = x.shape
    f32 = jnp.float32
    cd = MXU_DTYPE
    prompt = seq > 1
    tb = TK if prompt else T
    w = jnp.pad(w_in, ((0, 0), (0, NSA_COLS_PAD - NSA_COLS))).astype(cd)
    fixed = lambda i: (0, 0)
    row = lambda i: (i, 0)
    in_specs = [pl.BlockSpec((tb, D), row), pl.BlockSpec((1, D), fixed), pl.BlockSpec((D, NSA_COLS_PAD), fixed),
                pl.BlockSpec((1, HEAD_DIM), fixed), pl.BlockSpec((3, HEAD_DIM), fixed)]
    if prompt:
        assert seq % tb == 0 and WINDOW == 2 * tb
        n = seq // tb
        n_blocks = seq // SEL_BLOCK
        ka = -(-(n_blocks + HEAD_DIM) // LANES) * LANES
        chunk = lambda i: (i // n, 0, i % n, 0, 0)
        kt_spec = pl.BlockSpec((1, KV_HEADS, 1, HEAD_DIM, tb), chunk)
        ka_spec = pl.BlockSpec((1, KV_HEADS, 1, ka, tb), chunk)
        v_spec = pl.BlockSpec((1, KV_HEADS, 1, tb, LANES), chunk)
        kt_shape = jax.ShapeDtypeStruct((batch, KV_HEADS, n, HEAD_DIM, tb), cd)
        ka_shape = jax.ShapeDtypeStruct((batch, KV_HEADS, n, ka, tb), cd)
        v_shape = jax.ShapeDtypeStruct((batch, KV_HEADS, n, tb, LANES), cd)
        out_specs = [
            pl.BlockSpec((1, N_HEADS, tb, HEAD_DIM), lambda i: (i // n, 0, i % n, 0)),
            pl.BlockSpec((tb, 4 * SLOT), row),
            pl.BlockSpec((1, 1, tb, 2 * SLOT), lambda i: (i // n, jnp.maximum(i % n - (n - 2), 0), 0, 0)),
            ka_spec, v_spec, kt_spec, v_spec,
            pl.BlockSpec((2, 1, tb, SLOT), lambda i: (0, i // n, i % n, 0)),
            pl.BlockSpec((tb, KV_HEADS * LANES), row)]
        out_shape = [
            jax.ShapeDtypeStruct((batch, N_HEADS, seq, HEAD_DIM), cd),
            jax.ShapeDtypeStruct((T, 4 * SLOT), f32),
            jax.ShapeDtypeStruct((batch, 2, tb, 2 * SLOT), f32),
            ka_shape, v_shape, kt_shape, v_shape,
            jax.ShapeDtypeStruct((2, batch, seq, SLOT), cd),
            jax.ShapeDtypeStruct((T, KV_HEADS * LANES), f32)]
    else:
        out_specs = [pl.BlockSpec((tb, HQ), row), pl.BlockSpec((tb, 4 * SLOT), row),
                     pl.BlockSpec((tb, 2 * SLOT), row), pl.BlockSpec((tb, GATE_COLS), row)]
        n_blocks = 0
        out_shape = [jax.ShapeDtypeStruct((T, HQ), f32), jax.ShapeDtypeStruct((T, 4 * SLOT), f32),
                     jax.ShapeDtypeStruct((T, 2 * SLOT), f32), jax.ShapeDtypeStruct((T, GATE_COLS), f32)]
    return pl.pallas_call(
        functools.partial(_nsa_proj_kernel, prompt=prompt, n_blocks=n_blocks),
        grid=(T // tb,),
        in_specs=in_specs, out_specs=out_specs, out_shape=out_shape,
        compiler_params=pltpu.CompilerParams(dimension_semantics=("arbitrary",),
                                             vmem_limit_bytes=48 * 1024 * 1024),
        name="nsa_proj",
    )(x, ln_g.reshape(1, D), w, q_gain.reshape(1, HEAD_DIM), k_gain)


def _compress_kernel(c_ref, w1_ref, pe_ref, w2_ref, kg_ref, o_ref):
    f32 = jnp.float32
    pairs = SLOT // LANES
    c = c_ref[0, 0]
    u = [jnp.dot(jnp.concatenate([c[:, l * SLOT + p * LANES:l * SLOT + (p + 1) * LANES]
                                  for l in range(CMP_STRIDE)], axis=1).astype(MXU_DTYPE),
                 w1_ref[0], preferred_element_type=f32) for p in range(pairs)]
    first = jnp.concatenate([up[:, :LANES] for up in u], axis=1)
    second = jnp.concatenate([up[:, LANES:] for up in u], axis=1)
    pe = jnp.dot(pe_ref[0], w1_ref[0], preferred_element_type=f32)
    pe = jnp.concatenate([pe[0:1, :LANES] + pe[1:2, LANES:]] * pairs, axis=1)
    nc = first.shape[0]
    hid = first + pltpu.roll(second, nc - 1, 0) + pe
    out = jnp.dot(_gelu_tanh(hid).astype(MXU_DTYPE), w2_ref[0], preferred_element_type=f32)
    is_key = pl.program_id(1) == 0
    for kv in range(KV_HEADS):
        lanes = slice(kv * HEAD_DIM, (kv + 1) * HEAD_DIM)
        v = out[:, lanes]
        vn = v * lax.rsqrt(jnp.mean(v * v, axis=-1, keepdims=True) + EPS) * kg_ref[...]
        o_ref[0, 0, :, lanes] = jnp.where(is_key, vn, v)


def _compress(c, pe, w1, w2, k_gain0):
    n, nc = c.shape[1:3]
    cd = MXU_DTYPE
    kc = CMP_STRIDE * SLOT
    eye = jnp.eye(KV_HEADS, dtype=jnp.float32)
    per_tile = LANES // HEAD_DIM
    kp = CMP_STRIDE * LANES
    w1h = w1.reshape(2, 2, CMP_STRIDE, HEAD_DIM, CMP_HID)
    w1b = jnp.einsum('shldf,kj->slkdhjf', w1h, eye[:per_tile, :per_tile]).reshape(2, kp, 2 * LANES).astype(cd)
    w2b = jnp.einsum('sdf,kj->skdjf', w2, eye).reshape(2, SLOT, SLOT).astype(cd)
    peh = jnp.broadcast_to(pe.reshape(2, 2, CMP_STRIDE, 1, HEAD_DIM), (2, 2, CMP_STRIDE, per_tile, HEAD_DIM))
    peh = jnp.pad(peh.reshape(2, 2, kp), ((0, 0), (0, 14), (0, 0))).astype(cd)
    return pl.pallas_call(
        _compress_kernel,
        grid=(n, 2),
        in_specs=[pl.BlockSpec((1, 1, nc, kc), lambda b, s: (s, b, 0, 0)),
                  pl.BlockSpec((1, kp, 2 * LANES), lambda b, s: (s, 0, 0)),
                  pl.BlockSpec((1, 16, kp), lambda b, s: (s, 0, 0)),
                  pl.BlockSpec((1, SLOT, SLOT), lambda b, s: (s, 0, 0)),
                  pl.BlockSpec((1, HEAD_DIM), lambda b, s: (0, 0))],
        out_specs=pl.BlockSpec((1, 1, nc, SLOT), lambda b, s: (s, b, 0, 0)),
        out_shape=jax.ShapeDtypeStruct((2, n, nc, SLOT), jnp.float32),
        compiler_params=pltpu.CompilerParams(dimension_semantics=("arbitrary", "arbitrary"),
                                             vmem_limit_bytes=48 * 1024 * 1024),
        name="nsa_compress",
    )(c, w1b, peh, w2b, k_gain0.reshape(1, HEAD_DIM))


def _proj_residual_kernel(x_ref, a_ref, w_ref, o_ref):
    o_ref[...] = x_ref[...] + jnp.dot(a_ref[...].astype(MXU_DTYPE), w_ref[...],
                                      preferred_element_type=jnp.float32)


def _proj_residual(x, a, w):
    T, D = x.shape
    k = a.shape[1]
    tb = min(512, T)
    assert T % tb == 0
    return pl.pallas_call(
        _proj_residual_kernel,
        grid=(T // tb,),
        in_specs=[pl.BlockSpec((tb, D), lambda i: (i, 0)), pl.BlockSpec((tb, k), lambda i: (i, 0)),
                  pl.BlockSpec((k, D), lambda i: (0, 0))],
        out_specs=pl.BlockSpec((tb, D), lambda i: (i, 0)),
        out_shape=jax.ShapeDtypeStruct((T, D), jnp.float32),
        compiler_params=pltpu.CompilerParams(dimension_semantics=("arbitrary",)),
        name="proj_residual",
    )(x, a, w.astype(MXU_DTYPE))


MOE_ROWS = 256
ROUTE_COLS = LANES
ROW_DMA_UNROLL = 8


def _moe_route_kernel(x_ref, g_ref, w_ref, b_ref, tri_ref, h_ref, meta_ref, wt_ref, cnt_ref, carry_sc):
    f32 = jnp.float32
    step = pl.program_id(0)

    @pl.when(step == 0)
    def _():
        carry_sc[...] = jnp.zeros(carry_sc.shape, f32)

    x = x_ref[...]
    h = x * lax.rsqrt(jnp.mean(x * x, axis=-1, keepdims=True) + EPS) * g_ref[...]
    h_ref[...] = h
    logits = jnp.dot(h.astype(MXU_DTYPE), w_ref[...], preferred_element_type=f32) + b_ref[...]
    tb = logits.shape[0]
    col = lax.broadcasted_iota(jnp.int32, (tb, ROUTE_COLS), 1)

    def first_max(vals):
        best = jnp.max(vals, axis=1, keepdims=True)
        return best, jnp.min(jnp.where(vals == best, col, ROUTE_COLS), axis=1, keepdims=True)

    lg = jnp.where(col < N_GROUPS, logits, -jnp.inf)
    g_best, grp = first_max(lg)
    g_w = 1.0 / jnp.sum(jnp.exp(lg - g_best), axis=1, keepdims=True)
    lo = N_GROUPS + EXPERTS_PER_GROUP * grp
    le = jnp.where((col >= lo) & (col < lo + EXPERTS_PER_GROUP), logits, -jnp.inf)
    v0, c0 = first_max(le)
    v1, c1 = first_max(jnp.where(col == c0, -jnp.inf, le))
    e1 = jnp.exp(v1 - v0)
    w0 = g_w / (1.0 + e1)
    w1 = g_w * e1 / (1.0 + e1)
    chosen = (col == c0) | (col == c1)
    before = jnp.dot(tri_ref[...], jnp.where(chosen, 1.0, 0.0).astype(MXU_DTYPE),
                     preferred_element_type=f32) + carry_sc[...]
    r0 = jnp.sum(jnp.where(col == c0, before, 0.0), axis=1, keepdims=True).astype(jnp.int32)
    r1 = jnp.sum(jnp.where(col == c1, before, 0.0), axis=1, keepdims=True).astype(jnp.int32)
    carry_sc[...] += jnp.sum(jnp.where(chosen, 1.0, 0.0), axis=0, keepdims=True)
    cnt_ref[...] = carry_sc[...]
    meta_ref[...] = jnp.where(col == 0, c0 - N_GROUPS, jnp.where(col == 1, c1 - N_GROUPS,
                              jnp.where(col == 2, r0, jnp.where(col == 3, r1, 0))))
    wt_ref[...] = jnp.where(col == 0, w0, jnp.where(col == 1, w1, 0.0))


def _moe_dispatch_kernel(dest_ref, h_ref, xbuf_in, xbuf_ref, sem):
    del xbuf_in
    tb = h_ref.shape[0]

    def row_copy(r, k):
        return pltpu.make_async_copy(h_ref.at[pl.ds(r, 1)], xbuf_ref.at[pl.ds(dest_ref[0, 0, 2 * r + k], 1)], sem)

    def start(r, c):
        row_copy(r, 0).start()
        row_copy(r, 1).start()
        return c

    def wait(r, c):
        row_copy(r, 0).wait()
        row_copy(r, 1).wait()
        return c

    lax.fori_loop(0, tb, start, 0, unroll=ROW_DMA_UNROLL)
    lax.fori_loop(0, tb, wait, 0, unroll=ROW_DMA_UNROLL)


def _moe_expert_kernel(blk_e_ref, nblk_ref, x_ref, wgu_ref, wdn_ref, y_ref, wgu_sc, wdn_sc):
    i = pl.program_id(0)
    f32 = jnp.float32

    @pl.when(i < nblk_ref[0])
    def _():
        changed = jnp.logical_or(i == 0, blk_e_ref[i] != blk_e_ref[jnp.maximum(i - 1, 0)])

        @pl.when(changed)
        def _():
            wgu_sc[...] = wgu_ref[0].astype(MXU_DTYPE)
            wdn_sc[...] = wdn_ref[0].astype(MXU_DTYPE)

        gu = jnp.dot(x_ref[...].astype(MXU_DTYPE), wgu_sc[...], preferred_element_type=f32)
        gate = gu[:, :D_EXPERT]
        act = gate * (1.0 / (1.0 + jnp.exp(-gate))) * gu[:, D_EXPERT:]
        y_ref[...] = jnp.dot(act.astype(MXU_DTYPE), wdn_sc[...], preferred_element_type=f32)

    @pl.when(i >= nblk_ref[0])
    def _():
        y_ref[...] = jnp.zeros(y_ref.shape, f32)


def _moe_combine_kernel(dest_ref, x_ref, wt_ref, ybuf_ref, o_ref, rows_sc, sem):
    tb = x_ref.shape[0]

    def row_copy(r, k):
        return pltpu.make_async_copy(ybuf_ref.at[pl.ds(dest_ref[0, 0, 2 * r + k], 1)],
                                     rows_sc.at[k, pl.ds(r, 1)], sem)

    def start(r, c):
        row_copy(r, 0).start()
        row_copy(r, 1).start()
        return c

    def wait(r, c):
        row_copy(r, 0).wait()
        row_copy(r, 1).wait()
        return c

    lax.fori_loop(0, tb, start, 0, unroll=ROW_DMA_UNROLL)
    lax.fori_loop(0, tb, wait, 0, unroll=ROW_DMA_UNROLL)
    wt = wt_ref[...]
    o_ref[...] = x_ref[...] + (wt[:, 0:1] * rows_sc[0] + wt[:, 1:2] * rows_sc[1])


def _hier_moe_residual(x, ln_g, w_grp, b_grp, w_exp, b_exp, w_gu, w_dn):
    T, D = x.shape
    f32 = jnp.float32
    tb = min(256, T)
    assert T % tb == 0
    nt = T // tb
    pad_cols = ROUTE_COLS - N_GROUPS - N_EXPERTS
    w_r = jnp.pad(jnp.concatenate([w_grp, w_exp], axis=1), ((0, 0), (0, pad_cols))).astype(MXU_DTYPE)
    b_r = jnp.pad(jnp.concatenate([b_grp, b_exp]), (0, pad_cols)).reshape(1, ROUTE_COLS)
    tri = jnp.asarray(np.tril(np.ones((tb, tb), np.float32), -1), dtype=MXU_DTYPE)
    row = lambda i: (i, 0)
    fixed = lambda i: (0, 0)
    h, meta, wt, cnt = pl.pallas_call(
        _moe_route_kernel,
        grid=(nt,),
        in_specs=[pl.BlockSpec((tb, D), row), pl.BlockSpec((1, D), fixed),
                  pl.BlockSpec((D, ROUTE_COLS), fixed), pl.BlockSpec((1, ROUTE_COLS), fixed),
                  pl.BlockSpec((tb, tb), fixed)],
        out_specs=[pl.BlockSpec((tb, D), row), pl.BlockSpec((tb, ROUTE_COLS), row),
                   pl.BlockSpec((tb, ROUTE_COLS), row), pl.BlockSpec((1, ROUTE_COLS), fixed)],
        out_shape=[jax.ShapeDtypeStruct((T, D), f32), jax.ShapeDtypeStruct((T, ROUTE_COLS), jnp.int32),
                   jax.ShapeDtypeStruct((T, ROUTE_COLS), f32), jax.ShapeDtypeStruct((1, ROUTE_COLS), f32)],
        scratch_shapes=[pltpu.VMEM((1, ROUTE_COLS), f32)],
        compiler_params=pltpu.CompilerParams(dimension_semantics=("arbitrary",)),
        name="moe_route",
    )(x, ln_g.reshape(1, D), w_r, b_r, tri)

    counts = cnt[0, N_GROUPS:N_GROUPS + N_EXPERTS].astype(jnp.int32)
    padded = (counts + MOE_ROWS - 1) // MOE_ROWS * MOE_ROWS
    pad_end = jnp.cumsum(padded)
    pad_start = pad_end - padded
    n_blocks = -(-(T * TOP_K) // MOE_ROWS) + N_EXPERTS
    n_slots = n_blocks * MOE_ROWS
    dest = pad_start[meta[:, 0:2]] + meta[:, 2:4]
    dest = dest.reshape(nt, 1, 2 * tb)
    starts = jnp.arange(n_blocks, dtype=jnp.int32) * MOE_ROWS
    used = (pad_end[-1] // MOE_ROWS).astype(jnp.int32).reshape(1)
    blk_start = jnp.minimum(starts, pad_end[-1] - 1)
    blk_e = jnp.minimum(jnp.sum(pad_end[None, :] <= blk_start[:, None], axis=1), N_EXPERTS - 1).astype(jnp.int32)

    dest_spec = pl.BlockSpec((1, 1, 2 * tb), lambda i: (i, 0, 0), memory_space=pltpu.SMEM)
    xbuf = pl.pallas_call(
        _moe_dispatch_kernel,
        grid=(nt,),
        in_specs=[dest_spec, pl.BlockSpec((tb, D), row), pl.BlockSpec(memory_space=pl.ANY)],
        out_specs=pl.BlockSpec(memory_space=pl.ANY),
        out_shape=jax.ShapeDtypeStruct((n_slots, D), f32),
        scratch_shapes=[pltpu.SemaphoreType.DMA(())],
        input_output_aliases={2: 0},
        compiler_params=pltpu.CompilerParams(dimension_semantics=("arbitrary",)),
        name="moe_dispatch",
    )(dest, h, jnp.zeros((n_slots, D), f32))

    ybuf = pl.pallas_call(
        _moe_expert_kernel,
        grid_spec=pltpu.PrefetchScalarGridSpec(
            num_scalar_prefetch=2,
            grid=(n_blocks,),
            in_specs=[pl.BlockSpec((MOE_ROWS, D), lambda i, be, nb: (i, 0)),
                      pl.BlockSpec((1, D, 2 * D_EXPERT), lambda i, be, nb: (be[i], 0, 0)),
                      pl.BlockSpec((1, D_EXPERT, D), lambda i, be, nb: (be[i], 0, 0))],
            out_specs=pl.BlockSpec((MOE_ROWS, D), lambda i, be, nb: (i, 0)),
            scratch_shapes=[pltpu.VMEM((D, 2 * D_EXPERT), MXU_DTYPE), pltpu.VMEM((D_EXPERT, D), MXU_DTYPE)]),
        out_shape=jax.ShapeDtypeStruct((n_slots, D), f32),
        compiler_params=pltpu.CompilerParams(dimension_semantics=("arbitrary",),
                                             vmem_limit_bytes=48 * 1024 * 1024),
        name="moe_experts",
    )(blk_e, used, xbuf, w_gu, w_dn)

    return pl.pallas_call(
        _moe_combine_kernel,
        grid=(nt,),
        in_specs=[dest_spec, pl.BlockSpec((tb, D), row), pl.BlockSpec((tb, ROUTE_COLS), row),
                  pl.BlockSpec(memory_space=pl.ANY)],
        out_specs=pl.BlockSpec((tb, D), row),
        out_shape=jax.ShapeDtypeStruct((T, D), f32),
        scratch_shapes=[pltpu.VMEM((TOP_K, tb, D), f32), pltpu.SemaphoreType.DMA(())],
        compiler_params=pltpu.CompilerParams(dimension_semantics=("arbitrary",)),
        name="moe_combine",
    )(dest, x, wt, ybuf)


def _gelu_tanh(x):
    return 0.5 * x * (1.0 + jnp.tanh(math.sqrt(2.0 / math.pi) * (x + 0.044715 * (x * x * x))))


def _gmlp_kernel(x_ref, g_ref, win_ref, bin_ref, lng_ref, lnb_ref, ws_ref, bs_ref, wout_ref,
                 o_ref, v_ref, *, single_position):
    f32 = jnp.float32
    x = x_ref[...]
    rows = x.shape[0]
    h = x * lax.rsqrt(jnp.mean(x * x, axis=-1, keepdims=True) + EPS) * g_ref[...]
    z = _gelu_tanh(jnp.dot(h.astype(MXU_DTYPE), win_ref[...], preferred_element_type=f32) + bin_ref[...])
    u = z[:, :GM_WIDTH]
    v = z[:, GM_WIDTH:]
    mu = jnp.mean(v, axis=-1, keepdims=True)
    var = jnp.mean(jnp.square(v - mu), axis=-1, keepdims=True)
    v = (v - mu) * lax.rsqrt(var + EPS) * lng_ref[...] + lnb_ref[...]
    v_ref[0] = v
    vb = v.astype(MXU_DTYPE)
    if single_position:
        s = ws_ref[...].astype(f32) * vb.astype(f32) + bs_ref[...]
    else:
        parts = []
        for c in range(rows // CHUNK):
            vc = vb[c * CHUNK:(c + 1) * CHUNK]
            parts.append(jnp.concatenate(
                [jnp.dot(ws_ref[g], vc[:, g * GM_GROUP_DIM:(g + 1) * GM_GROUP_DIM], preferred_element_type=f32)
                 + bs_ref[g] for g in range(GM_GROUPS)], axis=1))
        s = jnp.concatenate(parts, axis=0)
    y = jnp.dot((u * s).astype(MXU_DTYPE), wout_ref[...], preferred_element_type=f32)
    o_ref[...] = x + y


def _gmlp_residual(x, ln_g, w_in, b_in, ln2_g, ln2_b, w_s, b_s, w_out, seq):
    T, D = x.shape
    f32 = jnp.float32
    cd = MXU_DTYPE
    single = seq == 1
    tb = T if single else 2 * CHUNK
    assert T % tb == 0 and (single or seq % tb == 0)
    steps_per_seq = 1 if single else seq // tb
    if single:
        ws = jnp.repeat(w_s[:, 0, 0], GM_GROUP_DIM).reshape(1, GM_WIDTH).astype(cd)
        bs = jnp.repeat(b_s[:, 0], GM_GROUP_DIM).reshape(1, GM_WIDTH)
        ws_spec = pl.BlockSpec((1, GM_WIDTH), lambda i: (0, 0))
        bs_spec = pl.BlockSpec((1, GM_WIDTH), lambda i: (0, 0))
    else:
        causal = np.tril(np.ones((CHUNK, CHUNK), bool))
        ws = jnp.where(causal[None], w_s, 0).astype(cd)
        bs = jnp.broadcast_to(b_s[:, :, None], (GM_GROUPS, CHUNK, GM_GROUP_DIM))
        ws_spec = pl.BlockSpec((GM_GROUPS, CHUNK, CHUNK), lambda i: (0, 0, 0))
        bs_spec = pl.BlockSpec((GM_GROUPS, CHUNK, GM_GROUP_DIM), lambda i: (0, 0, 0))
    fixed = lambda i: (0, 0)
    row = lambda i: (i, 0)
    return pl.pallas_call(
        functools.partial(_gmlp_kernel, single_position=single),
        grid=(T // tb,),
        in_specs=[pl.BlockSpec((tb, D), row), pl.BlockSpec((1, D), fixed),
                  pl.BlockSpec((D, 2 * GM_WIDTH), fixed), pl.BlockSpec((1, 2 * GM_WIDTH), fixed),
                  pl.BlockSpec((1, GM_WIDTH), fixed), pl.BlockSpec((1, GM_WIDTH), fixed),
                  ws_spec, bs_spec, pl.BlockSpec((GM_WIDTH, D), fixed)],
        out_specs=[pl.BlockSpec((tb, D), row), pl.BlockSpec((1, tb, GM_WIDTH), lambda i: (i // steps_per_seq, 0, 0))],
        out_shape=[jax.ShapeDtypeStruct((T, D), f32),
                   jax.ShapeDtypeStruct((T // (tb * steps_per_seq), tb, GM_WIDTH), f32)],
        compiler_params=pltpu.CompilerParams(dimension_semantics=("arbitrary",),
                                             vmem_limit_bytes=56 * 1024 * 1024),
        name="gmlp",
    )(x, ln_g.reshape(1, D), w_in.astype(cd), b_in.reshape(1, -1), ln2_g.reshape(1, -1), ln2_b.reshape(1, -1),
      ws, bs, w_out.astype(cd))


def rmsnorm(x, g):
    xf = x.astype(jnp.float32)
    y = xf * lax.rsqrt(jnp.mean(xf * xf, axis=-1, keepdims=True) + EPS)
    return (y * g.astype(jnp.float32)).astype(x.dtype)


def layernorm(x, g, b):
    xf = x.astype(jnp.float32)
    mu = jnp.mean(xf, axis=-1, keepdims=True)
    var = jnp.mean(jnp.square(xf - mu), axis=-1, keepdims=True)
    return ((xf - mu) * lax.rsqrt(var + EPS) * g.astype(jnp.float32) + b.astype(jnp.float32)).astype(x.dtype)


def rel_bucket(n):
    n = jnp.maximum(n, 0)
    max_exact = N_BUCKETS // 2
    nf = jnp.maximum(n, 1).astype(jnp.float32)
    large = max_exact + (jnp.log(nf / max_exact) / math.log(MAX_DISTANCE / max_exact)
                         * (N_BUCKETS - max_exact)).astype(jnp.int32)
    large = jnp.minimum(large, N_BUCKETS - 1)
    return jnp.where(n < max_exact, n, large)


def head_bias(rel, rel_bias):
    q, k = rel.shape
    b = rel_bias[rel_bucket(rel)]
    return b.reshape(q, k, KV_HEADS, Q_PER_KV).transpose(2, 3, 0, 1)


def masked_softmax(logits, mask):
    z = jnp.where(mask, logits.astype(jnp.float32), -1e30)
    return jax.nn.softmax(z, axis=-1) * mask


def nsa_project(xn, w_in, q_gain, k_gain):
    B, T, _ = xn.shape
    hq = N_HEADS * HEAD_DIM
    hkv = 6 * KV_HEADS * HEAD_DIM
    h = xn @ w_in
    q = rmsnorm(h[..., :hq].reshape(B, T, N_HEADS, HEAD_DIM), q_gain)
    kv = h[..., hq:hq + hkv].reshape(B, T, 6, KV_HEADS, HEAD_DIM)
    gates = jax.nn.sigmoid(h[..., hq + hkv:].astype(jnp.float32)).reshape(B, T, N_HEADS, 3)
    k_sel = rmsnorm(kv[:, :, 2], k_gain[1])
    k_win = rmsnorm(kv[:, :, 4], k_gain[2])
    return q, kv[:, :, 0], kv[:, :, 1], k_sel, kv[:, :, 3], k_win, kv[:, :, 5], gates


def compress(rows, pe, w1, w2):
    B, T = rows.shape[:2]
    n_ch = T // CMP_STRIDE
    c = rows[:, :n_ch * CMP_STRIDE].reshape(B, n_ch, CMP_STRIDE, KV_HEADS, HEAD_DIM)
    blk = jnp.concatenate([c[:, :-1], c[:, 1:]], axis=2) + pe[None, None, :, None, :]
    h = jax.nn.gelu(jnp.einsum('bnlkd,ldf->bnkf', blk, w1.reshape(CMP_LEN, HEAD_DIM, CMP_HID)))
    return jnp.einsum('bnkf,fd->bnkd', h, w2)


def compressed_kv(k_raw, v_raw, k_gain0, pe, w1, w2):
    kc = rmsnorm(compress(k_raw, pe[0], w1[0], w2[0]), k_gain0)
    vc = compress(v_raw, pe[1], w1[1], w2[1])
    end = jnp.arange(kc.shape[1]) * CMP_STRIDE + (CMP_LEN - 1)
    return kc, vc, end


def nsa_core(q, q_pos, kc, vc, kc_end, ks, vs, kw, vw, kw_pos, gates, rel_bias):
    B, Q = q.shape[:2]
    nc = kc.shape[1]
    scale = HEAD_DIM ** -0.5
    qh = q.reshape(B, Q, KV_HEADS, Q_PER_KV, HEAD_DIM)
    lc = jnp.einsum('bqkgd,bnkd->bkgqn', qh, kc) * scale + head_bias(q_pos[:, None] - kc_end[None, :], rel_bias)
    pc = masked_softmax(lc, kc_end[None, :] <= q_pos[:, None])
    oc = jnp.einsum('bkgqn,bnkd->bqkgd', pc.astype(vc.dtype), vc)
    nb = ks.shape[1] // SEL_BLOCK
    imp = jnp.pad(pc.sum(axis=2), ((0, 0), (0, 0), (0, 0), (0, nb * CMP_PER_SEL - nc)))
    imp = imp.reshape(B, KV_HEADS, Q, nb, CMP_PER_SEL).sum(-1)
    blk = jnp.arange(nb)
    d = (q_pos // SEL_BLOCK)[:, None] - blk[None, :]
    forced = (blk[None, :] == 0) | ((d >= 0) & (d < N_LOCAL))
    score = jnp.where(forced, jnp.inf, jnp.where(d >= 0, imp, -jnp.inf))
    n_sel = min(N_SEL, nb)
    _, idx = lax.top_k(score, n_sel)
    bi = jnp.arange(B)[:, None, None, None]
    ki = jnp.arange(KV_HEADS)[None, :, None, None]
    ksb = ks.reshape(B, nb, SEL_BLOCK, KV_HEADS, HEAD_DIM).transpose(0, 3, 1, 2, 4)
    vsb = vs.reshape(B, nb, SEL_BLOCK, KV_HEADS, HEAD_DIM).transpose(0, 3, 1, 2, 4)
    kg = ksb[bi, ki, idx].reshape(B, KV_HEADS, Q, n_sel * SEL_BLOCK, HEAD_DIM)
    vg = vsb[bi, ki, idx].reshape(B, KV_HEADS, Q, n_sel * SEL_BLOCK, HEAD_DIM)
    kpos = (idx[..., None] * SEL_BLOCK + jnp.arange(SEL_BLOCK)).reshape(B, KV_HEADS, Q, n_sel * SEL_BLOCK)
    rel_s = q_pos[None, None, :, None] - kpos
    tb = rel_bias.reshape(N_BUCKETS, KV_HEADS, Q_PER_KV)
    bs = tb[rel_bucket(rel_s), ki].transpose(0, 1, 4, 2, 3)
    ls = jnp.einsum('bqkgd,bkqjd->bkgqj', qh, kg) * scale + bs
    ps = masked_softmax(ls, (rel_s >= 0)[:, :, None])
    os_ = jnp.einsum('bkgqj,bkqjd->bqkgd', ps.astype(vg.dtype), vg)
    rel_w = q_pos[:, None] - kw_pos[None, :]
    mw = (rel_w >= 0) & (rel_w <= WINDOW) & (kw_pos[None, :] >= 0)
    lw = jnp.einsum('bqkgd,bwkd->bkgqw', qh, kw) * scale + head_bias(rel_w, rel_bias)
    pw = masked_softmax(lw, mw)
    ow = jnp.einsum('bkgqw,bwkd->bqkgd', pw.astype(vw.dtype), vw)
    g = gates.reshape(B, Q, KV_HEADS, Q_PER_KV, 3).astype(oc.dtype)
    o = g[..., 0:1] * oc + g[..., 1:2] * os_ + g[..., 2:3] * ow
    return o.reshape(B, Q, N_HEADS * HEAD_DIM)


def nsa_prompt(xn, w_in, w_out, q_gain, k_gain, pe, w1, w2, rel_bias):
    B, S, _ = xn.shape
    q, kcr, vcr, ks, vs, kw, vw, gates = nsa_project(xn, w_in, q_gain, k_gain)
    kc, vc, kc_end = compressed_kv(kcr, vcr, k_gain[0], pe, w1, w2)
    o = _nsa_attention(q, kc, vc, ks, vs, kw, vw, gates, rel_bias)
    w_keep = min(WINDOW, S)
    new_rows = jnp.stack([kcr, vcr, ks, vs], axis=2)
    new_win = jnp.stack([kw[:, S - w_keep:], vw[:, S - w_keep:]], axis=2)
    y = _mm(o.reshape(B * S, -1), w_out).reshape(B, S, -1)
    return y, new_rows, new_win


def nsa_sample(xn, past_rows, win_buf, w_in, w_out, q_gain, k_gain, pe, w1, w2, rel_bias):
    B, L, _ = xn.shape
    P = past_rows.shape[1]
    q, kcr, vcr, ks, vs, kw, vw, gates = nsa_project(xn, w_in, q_gain, k_gain)
    new_rows = jnp.stack([kcr, vcr, ks, vs], axis=2)
    rows = jnp.concatenate([past_rows, new_rows], axis=1)
    kc, vc, kc_end = compressed_kv(rows[:, :, 0], rows[:, :, 1], k_gain[0], pe, w1, w2)
    T = P + L
    tp = -(-T // SEL_BLOCK) * SEL_BLOCK
    pad = ((0, 0), (0, tp - T), (0, 0), (0, 0))
    ks_all = jnp.pad(rows[:, :, 2], pad)
    vs_all = jnp.pad(rows[:, :, 3], pad)
    win_all = jnp.concatenate([win_buf, jnp.stack([kw, vw], axis=2)], axis=1)
    wb = win_buf.shape[1]
    kw_pos = P - wb + jnp.arange(wb + L)
    q_pos = P + jnp.arange(L)
    o = nsa_core(q, q_pos, kc, vc, kc_end, ks_all, vs_all, win_all[:, :, 0], win_all[:, :, 1],
                 kw_pos, gates, rel_bias)
    w_keep = min(WINDOW, wb + L)
    return o @ w_out, new_rows, win_all[:, wb + L - w_keep:]


def gmlp_mixer(xn, w_in, b_in, ln_g, ln_b, w_s, b_s, w_out):
    B, L, _ = xn.shape
    z = jax.nn.gelu(xn @ w_in + b_in)
    u = z[..., :GM_WIDTH]
    v = layernorm(z[..., GM_WIDTH:], ln_g, ln_b)
    lp = -(-L // CHUNK) * CHUNK
    vp = jnp.pad(v, ((0, 0), (0, lp - L), (0, 0))).reshape(B, lp // CHUNK, CHUNK, GM_GROUPS, GM_GROUP_DIM)
    causal = jnp.tril(jnp.ones((CHUNK, CHUNK), dtype=bool))
    ws = jnp.where(causal[None], w_s, 0)
    s = jnp.einsum('gts,bcsgd->bctgd', ws, vp) + b_s.T[None, None, :, :, None]
    s = s.reshape(B, lp, GM_WIDTH)[:, :L]
    start = ((L - 1) // CHUNK) * CHUNK
    return (u * s) @ w_out, v[:, start:]


def hier_moe(x, w_grp, b_grp, w_exp, b_exp, w_gu, w_dn):
    T, D = x.shape
    pg = jax.nn.softmax((x @ w_grp + b_grp).astype(jnp.float32), axis=-1)
    grp = jnp.argmax(pg, axis=-1)
    g_w = jnp.take_along_axis(pg, grp[:, None], axis=-1)
    le = (x @ w_exp + b_exp).astype(jnp.float32).reshape(T, N_GROUPS, EXPERTS_PER_GROUP)
    le = jnp.take_along_axis(le, grp[:, None, None], axis=1)[:, 0]
    top_v, top_i = lax.top_k(le, TOP_K)
    wts = (jax.nn.softmax(top_v, axis=-1) * g_w).reshape(-1)
    eid = (grp[:, None] * EXPERTS_PER_GROUP + top_i).reshape(-1)
    n = T * TOP_K
    order = jnp.argsort(eid)
    e_s = eid[order]
    tok_s = order // TOP_K
    w_s = wts[order]
    counts = jnp.bincount(eid, length=N_EXPERTS)
    padded = (counts + MOE_BLOCK - 1) // MOE_BLOCK * MOE_BLOCK
    pad_end = jnp.cumsum(padded)
    pad_start = pad_end - padded
    start = jnp.cumsum(counts) - counts
    dest = pad_start[e_s] + jnp.arange(n) - start[e_s]
    n_blocks = -(-n // MOE_BLOCK) + N_EXPERTS
    xbuf = jnp.zeros((n_blocks * MOE_BLOCK, D), x.dtype).at[dest].set(x[tok_s])
    blk_e = jnp.minimum(jnp.searchsorted(pad_end, jnp.arange(n_blocks) * MOE_BLOCK, side='right'), N_EXPERTS - 1)

    def expert(args):
        xb, e = args
        gu = xb @ w_gu[e]
        return (jax.nn.silu(gu[:, :D_EXPERT]) * gu[:, D_EXPERT:]) @ w_dn[e]

    ybuf = lax.map(expert, (xbuf.reshape(n_blocks, MOE_BLOCK, D), blk_e)).reshape(-1, D)
    return jax.ops.segment_sum(ybuf[dest] * w_s[:, None].astype(x.dtype), tok_s, num_segments=T)


def kernel(x_prompt, x_sample, cache_nsa_kv, state_win_kv, page_table, rel_bias, ln_mix, ln_ffn,
           nsa_w_in, nsa_w_out, nsa_q_gain, nsa_k_gain, cmp_pe, cmp_w1, cmp_w2,
           gm_w_in, gm_b_in, gm_ln_g, gm_ln_b, gm_w_s, gm_b_s, gm_w_out,
           moe_w_grp, moe_b_grp, moe_w_exp, moe_b_exp, moe_w_gu, moe_w_dn):
    xp = x_prompt
    xs = x_sample
    kv_p, kv_s, win_p, win_s, gv_p, gv_s = [], [], [], [], [], []
    for i in range(DEPTH):
        a = i // N_MIXERS
        if i % N_MIXERS == 0:
            nsa = (ln_mix[i], nsa_w_in[a], nsa_w_out[a], nsa_q_gain[a], nsa_k_gain[a],
                   cmp_pe[a], cmp_w1[a], cmp_w2[a], rel_bias)
            xp, rp, wp = _nsa_prompt_layer(xp, *nsa)
            cmp_rows, sel_kvt = _page_gather(cache_nsa_kv, page_table, a)
            xs, rs, ws = _nsa_decode_layer(xs, cmp_rows, sel_kvt, state_win_kv[a], *nsa)
            kv_p.append(rp)
            kv_s.append(rs)
            win_p.append(wp)
            win_s.append(ws)
        else:
            gm = (ln_mix[i], gm_w_in[a], gm_b_in[a], gm_ln_g[a], gm_ln_b[a], gm_w_s[a], gm_b_s[a], gm_w_out[a])
            bp, sp = xp.shape[:2]
            bs_, ss = xs.shape[:2]
            xp2, vp = _gmlp_residual(xp.reshape(-1, D_MODEL), *gm, seq=sp)
            xs2, vs = _gmlp_residual(xs.reshape(-1, D_MODEL), *gm, seq=ss)
            xp = xp2.reshape(xp.shape)
            xs = xs2.reshape(xs.shape)
            start = ((sp - 1) // CHUNK) * CHUNK
            gv_p.append(vp[:, vp.shape[1] - (sp - start):])
            gv_s.append(vs.reshape(bs_, ss, GM_WIDTH))
        moe = (ln_ffn[i], moe_w_grp[i], moe_b_grp[i], moe_w_exp[i], moe_b_exp[i], moe_w_gu[i], moe_w_dn[i])
        xp = _hier_moe_residual(xp.reshape(-1, D_MODEL), *moe).reshape(xp.shape)
        xs = _hier_moe_residual(xs.reshape(-1, D_MODEL), *moe).reshape(xs.shape)
    new_kv_prompt = jnp.stack(kv_p, axis=2)
    new_kv_sample = jnp.stack(kv_s, axis=2)
    new_win_prompt = jnp.stack(win_p, axis=0)
    new_win_sample = jnp.stack(win_s, axis=0)
    new_gm_v_prompt = jnp.stack(gv_p, axis=0)
    new_gm_v_sample = jnp.stack(gv_s, axis=0)
    return (xp, xs, new_kv_prompt, new_kv_sample, new_win_prompt, new_win_sample, new_gm_v_prompt, new_gm_v_sample)
```

```python
import functools
import math

import jax
import jax.numpy as jnp
import numpy as np
from jax import lax
from jax.experimental import pallas as pl
from jax.experimental.pallas import tpu as pltpu

D_MODEL = 1024
PAGE_SIZE = 128
DEPTH = 2
N_MIXERS = 2
N_HEADS = 16
HEAD_DIM = 64
KV_HEADS = 4
Q_PER_KV = N_HEADS // KV_HEADS
CMP_LEN = 32
CMP_STRIDE = 16
CMP_HID = 64
SEL_BLOCK = 64
CMP_PER_SEL = SEL_BLOCK // CMP_STRIDE
N_SEL = 16
N_LOCAL = 2
WINDOW = 512
Q_BLOCK = 128
N_BUCKETS = 32
MAX_DISTANCE = 128
CHUNK = 128
GM_WIDTH = 2048
GM_GROUPS = 8
GM_GROUP_DIM = GM_WIDTH // GM_GROUPS
N_GROUPS = 4
EXPERTS_PER_GROUP = 8
N_EXPERTS = N_GROUPS * EXPERTS_PER_GROUP
TOP_K = 2
D_EXPERT = 512
MOE_BLOCK = 128
EPS = 1e-6


def _mm_kernel(x_ref, w_ref, o_ref):
    o_ref[...] = jnp.dot(x_ref[...].astype(jnp.bfloat16), w_ref[...].astype(jnp.bfloat16),
                         preferred_element_type=jnp.float32)


def _mm(x, w, tm=512, tn=512):
    m, k = x.shape
    n = w.shape[1]
    tm = min(tm, m)
    tn = min(tn, n)
    return pl.pallas_call(
        _mm_kernel,
        grid=(m // tm, n // tn),
        in_specs=[pl.BlockSpec((tm, k), lambda i, j: (i, 0)),
                  pl.BlockSpec((k, tn), lambda i, j: (0, j))],
        out_specs=pl.BlockSpec((tm, tn), lambda i, j: (i, j)),
        out_shape=jax.ShapeDtypeStruct((m, n), jnp.float32),
        name="mm",
    )(x, w)


NEG = -1e30
MXU_DTYPE = jnp.bfloat16
TQ = 256
TK = 256
GROUP_CHUNKS = 3
LANES = 128
BIAS_SPAN = 128
LOG2E = 1.4426950408889634


def _bucket_of_distance():
    d = np.arange(BIAS_SPAN)
    max_exact = N_BUCKETS // 2
    nf = np.maximum(d, 1).astype(np.float32)
    large = max_exact + (np.log(nf / np.float32(max_exact)) / np.float32(math.log(MAX_DISTANCE / max_exact))
                         * np.float32(N_BUCKETS - max_exact)).astype(np.int32)
    large = np.minimum(large, N_BUCKETS - 1)
    return np.where(d < max_exact, d, large).astype(np.int32)


def _attn_bias_tiles(rel_bias, seq):
    fd = rel_bias[_bucket_of_distance()].T
    far = fd[:, BIAS_SPAN - 1]
    fd = fd - far[:, None]

    def by_distance(d):
        return jnp.where(d >= 0, LOG2E * jnp.take(fd, np.clip(d, 0, BIAS_SPAN - 1), axis=1), NEG)

    def toeplitz(first_distance):
        c = np.arange(2 * TK)
        g = by_distance(first_distance - np.where(c < TK, c, c - 2 * TK))
        flat = jnp.tile(g, (1, TQ))[:, :TQ * (2 * TK - 1)]
        return flat.reshape(-1, TQ, 2 * TK - 1)[:, :, :TK]

    b0 = toeplitz(0)
    b1 = toeplitz(TQ)
    i = np.arange(TQ)[:, None]
    j = np.arange(TK)[None, :]
    w2 = np.where(2 * TQ + i - j <= WINDOW, 0.0, NEG).astype(np.float32)
    nc = seq // CMP_STRIDE
    span = -(-(BIAS_SPAN + TQ) // CMP_STRIDE)
    r = np.arange(nc - span, nc + span)[:, None]
    dc = np.arange(TQ)[None, :] - CMP_STRIDE * (r - nc) - (CMP_LEN - 1)
    n_heads = fd.shape[0]
    pat = jnp.concatenate([jnp.zeros((n_heads, nc - span, TQ), jnp.float32), by_distance(dc),
                           jnp.full((n_heads, nc - span, TQ), NEG, jnp.float32)], axis=1)
    return b0, b1, jnp.asarray(w2), pat


def _nsa_attn_kernel(q_ref, kc_ref, vct_ref, kat_ref, vsa_ref, kwt_ref, vwa_ref, g_ref,
                     b0_ref, b1_ref, w2_ref, pat_ref, o_ref, m_sc, acc_sc, imp_sc, qa_sc, p_sc, a_sc, *, nc, nb):
    f32 = jnp.float32
    cd = MXU_DTYPE
    qb = pl.program_id(2)

    def compress_and_select(ncu, all_forced):
        nbu = ncu // CMP_PER_SEL
        r0 = pl.multiple_of(nc - (TQ // CMP_STRIDE) * qb, TQ // CMP_STRIDE)
        t_row = qb * TQ + lax.broadcasted_iota(jnp.int32, (1, TQ), 1)
        has_cmp = t_row >= CMP_LEN - 1
        for h in range(Q_PER_KV):
            st = lax.dot_general(kc_ref[0, 0, 0:ncu, :], q_ref[0, h], (((1,), (1,)), ((), ())),
                                 preferred_element_type=f32)
            st = st + pat_ref[h, pl.ds(r0, ncu), :]
            e = jnp.exp2(st - jnp.max(st, axis=0, keepdims=True))
            inv = jnp.where(has_cmp, 1.0 / jnp.sum(e, axis=0, keepdims=True), 0.0)
            pt = e * inv
            for c in range(TQ // LANES):
                part = pt[:, c * LANES:(c + 1) * LANES]
                if h == 0:
                    imp_sc[c, 0:ncu, :] = part
                else:
                    imp_sc[c, 0:ncu, :] += part
            oct_h = jnp.dot(vct_ref[0, 0, :, 0:ncu], pt.astype(cd), preferred_element_type=f32)
            o_ref[0, :, h * HEAD_DIM:(h + 1) * HEAD_DIM] = g_ref[0][:, 3 * h:3 * h + 1] * oct_h.T

        imp = jnp.concatenate(
            [sum(imp_sc[c, pl.ds(r, nbu, stride=CMP_PER_SEL), :] for r in range(CMP_PER_SEL))
             for c in range(TQ // LANES)], axis=1)
        blk = lax.broadcasted_iota(jnp.int32, (nbu, TQ), 0)
        t_blk = (qb * TQ + lax.broadcasted_iota(jnp.int32, (nbu, TQ), 1)) // SEL_BLOCK
        dd = t_blk - blk
        forced = (blk == 0) | ((dd >= 0) & (dd < N_LOCAL))
        if all_forced:
            score = jnp.where(forced | (dd < 0), -jnp.inf, imp)
            pen_t = jnp.where(forced, 0.0, NEG)
            rounds = N_SEL - (N_LOCAL + 1)
        else:
            score = jnp.where(forced, jnp.inf, jnp.where(dd >= 0, imp, -jnp.inf))
            pen_t = jnp.full((nbu, TQ), NEG, f32)
            rounds = min(N_SEL, nbu)
        for _ in range(rounds):
            best = jnp.max(score, axis=0, keepdims=True)
            first = jnp.min(jnp.where(score == best, blk, nbu), axis=0, keepdims=True)
            hit = blk == first
            pen_t = jnp.where(hit, 0.0, pen_t)
            score = jnp.where(hit, -jnp.inf, score)
        if nbu < nb:
            pen_t = jnp.concatenate([pen_t, jnp.full((nb - nbu, TQ), NEG, f32)], axis=0)
        pen = pen_t.T.astype(cd)
        pad = jnp.zeros((TQ, qa_sc.shape[2] - nb - HEAD_DIM), cd)
        for h in range(Q_PER_KV):
            qa_sc[h] = jnp.concatenate([pen, q_ref[0, h], pad], axis=1)

    n_tiles = nc * CMP_STRIDE // TQ
    n_var = 4 if (n_tiles % 4 == 0 and (nc // 4) % (2 * CMP_STRIDE) == 0) else 1
    for v in range(n_var):
        lo, hi = v * n_tiles // n_var, (v + 1) * n_tiles // n_var
        ncu = (v + 1) * nc // n_var
        if v == 0:
            @pl.when(qb == 0)
            def _():
                compress_and_select(ncu, all_forced=False)
            lo = 1

        @pl.when((qb >= lo) & (qb < hi))
        def _():
            compress_and_select(ncu, all_forced=TQ >= N_LOCAL * SEL_BLOCK)


    def reset():
        m_sc[...] = jnp.full(m_sc.shape, NEG, f32)
        acc_sc[...] = jnp.zeros(acc_sc.shape, f32)

    def run(lhs, k_ref, v_ref, chunks):
        for slot, (c, bias) in enumerate(chunks):
            for h in range(Q_PER_KV):
                s = jnp.dot(lhs(h), k_ref[0, 0, c], preferred_element_type=f32)
                if bias is not None:
                    s = s + bias(h)
                m_prev = m_sc[h]
                m_new = jnp.maximum(m_prev, jnp.max(s, axis=1, keepdims=True))
                p_sc[slot, h] = jnp.exp2(s - jnp.concatenate([m_new] * (TK // LANES), axis=1)).astype(cd)
                a_sc[slot, h] = jnp.exp2(m_prev - m_new)
                m_sc[h] = m_new
        for slot, (c, _) in enumerate(chunks):
            for h in range(Q_PER_KV):
                acc_sc[h] = a_sc[slot, h] * acc_sc[h] + jnp.dot(p_sc[slot, h], v_ref[0, 0, c],
                                                                preferred_element_type=f32)

    def emit(gate_col):
        for h in range(Q_PER_KV):
            acc = acc_sc[h]
            w = g_ref[0][:, 3 * h + gate_col:3 * h + gate_col + 1] / acc[:, HEAD_DIM:HEAD_DIM + 1]
            o_ref[0, :, h * HEAD_DIM:(h + 1) * HEAD_DIM] += w * acc[:, :HEAD_DIM]

    own = lambda h: b0_ref[h]
    prev = lambda h: b1_ref[h]
    edge = lambda h: w2_ref[...]

    sel_lhs = lambda h: qa_sc[h]
    reset()
    n_far = jnp.maximum(qb - 1, 0)

    def far_group(i, carry):
        run(sel_lhs, kat_ref, vsa_ref, [(GROUP_CHUNKS * i + j, None) for j in range(GROUP_CHUNKS)])
        return carry

    lax.fori_loop(0, n_far // GROUP_CHUNKS, far_group, 0)
    for left in range(1, GROUP_CHUNKS):
        @pl.when(n_far % GROUP_CHUNKS == left)
        def _():
            run(sel_lhs, kat_ref, vsa_ref, [(n_far - left + j, None) for j in range(left)])

    @pl.when(qb >= 1)
    def _():
        run(sel_lhs, kat_ref, vsa_ref, [(qb - 1, prev), (qb, own)])

    @pl.when(qb == 0)
    def _():
        run(sel_lhs, kat_ref, vsa_ref, [(qb, own)])

    emit(1)

    win_lhs = lambda h: q_ref[0, h]
    reset()

    @pl.when(qb >= 2)
    def _():
        run(win_lhs, kwt_ref, vwa_ref, [(qb - 2, edge), (qb - 1, prev), (qb, own)])

    @pl.when(qb == 1)
    def _():
        run(win_lhs, kwt_ref, vwa_ref, [(qb - 1, prev), (qb, own)])

    @pl.when(qb == 0)
    def _():
        run(win_lhs, kwt_ref, vwa_ref, [(qb, own)])

    emit(2)


def _nsa_attention(qs, kcp, vct, kat, vsa, kwt, vwa, gt, rel_bias):
    B, _, S, _ = qs.shape
    assert S % TQ == 0 and TQ == TK and WINDOW == 2 * TQ and S // SEL_BLOCK >= N_SEL
    nc = S // CMP_STRIDE
    nb = S // SEL_BLOCK
    nch = S // TK
    ka = kat.shape[3]
    b0, b1, w2, pat = _attn_bias_tiles(rel_bias, S)
    G = Q_PER_KV
    kv_chunks = pl.BlockSpec((1, 1, nch, HEAD_DIM, TK), lambda b, k, i: (b, k, 0, 0, 0))
    ka_chunks = pl.BlockSpec((1, 1, nch, ka, TK), lambda b, k, i: (b, k, 0, 0, 0))
    v_chunks = pl.BlockSpec((1, 1, nch, TK, LANES), lambda b, k, i: (b, k, 0, 0, 0))
    return pl.pallas_call(
        functools.partial(_nsa_attn_kernel, nc=nc, nb=nb),
        grid=(B, KV_HEADS, S // TQ),
        in_specs=[
            pl.BlockSpec((1, G, TQ, HEAD_DIM), lambda b, k, i: (b, k, i, 0)),
            pl.BlockSpec((1, 1, nc, HEAD_DIM), lambda b, k, i: (b, k, 0, 0)),
            pl.BlockSpec((1, 1, HEAD_DIM, nc), lambda b, k, i: (b, k, 0, 0)),
            ka_chunks, v_chunks, kv_chunks, v_chunks,
            pl.BlockSpec((1, TQ, LANES), lambda b, k, i: (b, i, k)),
            pl.BlockSpec((G, TQ, TK), lambda b, k, i: (k, 0, 0)),
            pl.BlockSpec((G, TQ, TK), lambda b, k, i: (k, 0, 0)),
            pl.BlockSpec((TQ, TK), lambda b, k, i: (0, 0)),
            pl.BlockSpec((G, 2 * nc, TQ), lambda b, k, i: (k, 0, 0)),
        ],
        out_specs=pl.BlockSpec((1, TQ, G * HEAD_DIM), lambda b, k, i: (b, i, k)),
        out_shape=jax.ShapeDtypeStruct((B, S, N_HEADS * HEAD_DIM), jnp.float32),
        scratch_shapes=[
            pltpu.VMEM((G, TQ, LANES), jnp.float32),
            pltpu.VMEM((G, TQ, LANES), jnp.float32),
            pltpu.VMEM((TQ // LANES, nc, LANES), jnp.float32),
            pltpu.VMEM((G, TQ, ka), MXU_DTYPE),
            pltpu.VMEM((GROUP_CHUNKS, G, TQ, TK), MXU_DTYPE),
            pltpu.VMEM((GROUP_CHUNKS, G, TQ, LANES), jnp.float32),
        ],
        compiler_params=pltpu.CompilerParams(
            dimension_semantics=("arbitrary", "arbitrary", "arbitrary"),
            vmem_limit_bytes=56 * 1024 * 1024),
        name="nsa_attn",
    )(qs, kcp, vct, kat, vsa, kwt, vwa, gt, b0, b1, w2, pat)


def _nsa_prompt_layer(x, ln_g, w_in, w_out, q_gain, k_gain, pe, w1, w2, rel_bias):
    B, S, D = x.shape
    x2 = x.reshape(B * S, D)
    qs, rows, win, kat, vsa, kwt, vwa, c, gates = _nsa_project(x2, ln_g, w_in, q_gain, k_gain, B, S)
    nc = S // CMP_STRIDE
    cmp = _compress(c.reshape(2, B, nc, CMP_STRIDE * SLOT), pe, w1, w2, k_gain[0])
    cmp = cmp.reshape(2, B, nc, KV_HEADS, HEAD_DIM).astype(MXU_DTYPE)
    gt = gates.reshape(B, S, KV_HEADS * LANES)
    o = _nsa_attention(qs, cmp[0].transpose(0, 2, 1, 3), cmp[1].transpose(0, 2, 3, 1), kat, vsa, kwt, vwa, gt,
                       rel_bias)
    y = _proj_residual(x2, o.reshape(B * S, HQ), w_out)
    return (y.reshape(B, S, D), rows.reshape(B, 4, KV_HEADS, HEAD_DIM, S).transpose(0, 4, 1, 2, 3),
            win.reshape(B, WINDOW, 2, KV_HEADS, HEAD_DIM))


PAGES_PER_STEP = 8


def _page_gather_kernel(pt_ref, *refs):
    del pt_ref
    pages = refs[:PAGES_PER_STEP]
    c_ref, kvt_ref, rows_sc = refs[PAGES_PER_STEP:]
    for p, pg in enumerate(pages):
        pos = slice(p * PAGE_SIZE, (p + 1) * PAGE_SIZE)
        for slot in range(2):
            for pair in range(KV_HEADS // 2):
                tile = pg[0, 0, slot, 2 * pair:2 * pair + 2].reshape(LANES, PAGE_SIZE)
                rows_sc[slot * 2 + pair, pos, :] = tile.T
        for slot in range(2):
            kvt_ref[slot, 0, :, :, pos] = pg[0, 0, 2 + slot].astype(MXU_DTYPE)
    n_chunks = PAGES_PER_STEP * PAGE_SIZE // CMP_STRIDE
    for l in range(CMP_STRIDE):
        for t in range(4):
            lo = l * SLOT + (t % 2) * LANES
            c_ref[t // 2, 0, :, lo:lo + LANES] = rows_sc[t, pl.ds(l, n_chunks, stride=CMP_STRIDE), :].astype(MXU_DTYPE)


def _page_gather(cache, page_table, layer):
    nseq, n_pages = page_table.shape
    assert n_pages % PAGES_PER_STEP == 0 and 2 * LANES == SLOT
    P = n_pages * PAGE_SIZE
    cache_t = cache.transpose(0, 2, 3, 4, 5, 1)
    page_specs = [
        pl.BlockSpec((1, 1, 4, KV_HEADS, HEAD_DIM, PAGE_SIZE),
                     functools.partial(lambda r, n, j, pt: (pt[n, PAGES_PER_STEP * j + r], layer, 0, 0, 0, 0), r))
        for r in range(PAGES_PER_STEP)]
    rows_per_step = PAGES_PER_STEP * PAGE_SIZE
    return pl.pallas_call(
        _page_gather_kernel,
        grid_spec=pltpu.PrefetchScalarGridSpec(
            num_scalar_prefetch=1,
            grid=(nseq, n_pages // PAGES_PER_STEP),
            in_specs=page_specs,
            out_specs=[pl.BlockSpec((2, 1, rows_per_step // CMP_STRIDE, CMP_STRIDE * SLOT),
                                    lambda n, j, pt: (0, n, j, 0)),
                       pl.BlockSpec((2, 1, KV_HEADS, HEAD_DIM, rows_per_step), lambda n, j, pt: (0, n, 0, 0, j))],
            scratch_shapes=[pltpu.VMEM((4, rows_per_step, LANES), jnp.float32)]),
        out_shape=[jax.ShapeDtypeStruct((2, nseq, P // CMP_STRIDE, CMP_STRIDE * SLOT), MXU_DTYPE),
                   jax.ShapeDtypeStruct((2, nseq, KV_HEADS, HEAD_DIM, P), MXU_DTYPE)],
        compiler_params=pltpu.CompilerParams(dimension_semantics=("arbitrary", "arbitrary")),
        name="page_gather",
    )(page_table, *([cache_t] * PAGES_PER_STEP))


def _nsa_decode_kernel(q_ref, qt_ref, kc_ref, vc_ref, kst_ref, vst_ref, win_ref, rows_ref, wnew_ref, g_ref,
                       oh_ref, bc_ref, bs_ref, bw_ref, bn_ref, o_ref, pen_sc, s_sc, *, nb):
    f32 = jnp.float32
    cd = MXU_DTYPE
    nt = (((1,), (1,)), ((), ()))
    pen_sc[...] = jnp.zeros(pen_sc.shape, f32)
    rows = rows_ref[0]
    wnew = wnew_ref[0]

    def softmax_with_new(s, s_new):
        m = jnp.maximum(jnp.max(s, axis=1, keepdims=True), s_new)
        e = jnp.exp(s - m)
        e_new = jnp.exp(s_new - m)
        return e, e_new, 1.0 / (jnp.sum(e, axis=1, keepdims=True) + e_new)

    def bf(x):
        return x.astype(cd).astype(f32)

    for k in range(KV_HEADS):
        hs = slice(k * Q_PER_KV, (k + 1) * Q_PER_KV)
        q = q_ref[0, hs, :]
        gates = g_ref[0, hs, :]
        sc = lax.dot_general(q, kc_ref[0, k], nt, preferred_element_type=f32) + bc_ref[hs, :]
        ec = jnp.exp(sc - jnp.max(sc, axis=1, keepdims=True))
        pc = ec / jnp.sum(ec, axis=1, keepdims=True)
        o_ref[0, hs, :] = gates[:, 0:1] * jnp.dot(pc.astype(cd), vc_ref[0, k], preferred_element_type=f32)

        imp = sum(pc[:, r * nb:(r + 1) * nb] for r in range(CMP_PER_SEL))
        imp = jnp.sum(imp, axis=0, keepdims=True)
        blk = lax.broadcasted_iota(jnp.int32, imp.shape, 1)
        forced = (blk == 0) | (blk == nb - 1)
        score = jnp.where(forced, jnp.inf, imp)
        pen = jnp.full(imp.shape, NEG, f32)
        for _ in range(N_SEL - 1):
            best = jnp.max(score, axis=1, keepdims=True)
            first = jnp.min(jnp.where(score == best, blk, nb), axis=1, keepdims=True)
            hit = blk == first
            pen = jnp.where(hit, 0.0, pen)
            score = jnp.where(hit, -jnp.inf, score)
        pen_sc[k:k + 1, :] = pen

    key_pen = jnp.dot(pen_sc[...].astype(cd), oh_ref[...], preferred_element_type=f32)

    lane = lax.broadcasted_iota(jnp.int32, (HEAD_DIM, LANES), 1)
    for k in range(KV_HEADS):
        hs = slice(k * Q_PER_KV, (k + 1) * Q_PER_KV)
        lanes = slice(k * HEAD_DIM, (k + 1) * HEAD_DIM)
        q = q_ref[0, hs, :]
        qf = q.astype(f32)
        gates = g_ref[0, hs, :]

        def new_key(row):
            return jnp.sum(qf * bf(row), axis=1, keepdims=True) + bn_ref[hs, :]

        kt = kst_ref[0, 0, k].astype(f32)
        for g in range(Q_PER_KV):
            qcol = qt_ref[0][:, k * Q_PER_KV + g:k * Q_PER_KV + g + 1].astype(f32)
            s_sc[g:g + 1, :] = jnp.sum(kt * qcol, axis=0, keepdims=True)
        ss = s_sc[0:Q_PER_KV, :] + key_pen[k:k + 1, :] + bs_ref[hs, :]
        e, e_new, inv = softmax_with_new(ss, new_key(rows[:, 2 * SLOT:3 * SLOT][:, lanes]))
        vt = vst_ref[0, 0, k].astype(f32)
        eb = bf(e)
        acc_t = jnp.zeros((HEAD_DIM, LANES), f32)
        for g in range(Q_PER_KV):
            acc_t = jnp.where(lane == g, jnp.sum(vt * eb[g:g + 1, :], axis=1, keepdims=True), acc_t)
        acc = acc_t.T[0:Q_PER_KV, :] + bf(e_new) * bf(rows[:, 3 * SLOT:4 * SLOT][:, lanes])
        out = o_ref[0, hs, :] + gates[:, 1:2] * inv * acc

        sw = lax.dot_general(q, win_ref[0, 0, k], nt, preferred_element_type=f32) + bw_ref[hs, :]
        e, e_new, inv = softmax_with_new(sw, new_key(wnew[:, :SLOT][:, lanes]))
        acc = (jnp.dot(e.astype(cd), win_ref[0, 1, k], preferred_element_type=f32)
               + bf(e_new) * bf(wnew[:, SLOT:][:, lanes]))
        o_ref[0, hs, :] = out + gates[:, 2:3] * inv * acc


def _nsa_decode_layer(x, cmp_rows, sel_kvt, win_buf, ln_g, w_in, w_out, q_gain, k_gain, pe, w1, w2, rel_bias):
    N, L, D = x.shape
    P = sel_kvt.shape[-1]
    wb = win_buf.shape[1]
    assert L == 1 and P % SEL_BLOCK == 0 and wb == WINDOW and P >= BIAS_SPAN and P // SEL_BLOCK >= N_SEL - 1
    f32 = jnp.float32
    cd = MXU_DTYPE
    nc = P // CMP_STRIDE
    nb = P // SEL_BLOCK
    x2 = x.reshape(N, D)
    q, rows, wnew, gates = _nsa_project(x2, ln_g, w_in, q_gain, k_gain, N, 1)
    cmp = _compress(cmp_rows, pe, w1, w2, k_gain[0])
    cmp = cmp.reshape(2, N, nb, CMP_PER_SEL, KV_HEADS, HEAD_DIM).transpose(0, 1, 4, 3, 2, 5)
    cmp = cmp.reshape(2, N, KV_HEADS, nc, HEAD_DIM).astype(cd)
    qs = (q.reshape(N, N_HEADS, HEAD_DIM) * HEAD_DIM ** -0.5).astype(cd)
    win = win_buf.transpose(0, 2, 3, 1, 4).astype(cd)
    onehot = jnp.asarray(np.arange(nb)[:, None] == (np.arange(P) // SEL_BLOCK)[None, :], dtype=cd)
    fd = rel_bias[_bucket_of_distance()].T
    fd = fd - fd[:, BIAS_SPAN - 1:]
    cidx = (np.arange(nc) % nb) * CMP_PER_SEL + np.arange(nc) // nb
    dc = P - (cidx * CMP_STRIDE + CMP_LEN - 1)
    b_cmp = jnp.where(dc >= 0, jnp.take(fd, np.clip(dc, 0, BIAS_SPAN - 1), axis=1), NEG)
    b_sel = jnp.take(fd, np.clip(P - np.arange(P), 0, BIAS_SPAN - 1), axis=1)
    b_win = jnp.take(fd, np.clip(wb - np.arange(wb), 0, BIAS_SPAN - 1), axis=1)
    b_new = fd[:, 0:1]
    seq3 = lambda n: (n, 0, 0)
    seq4 = lambda n: (n, 0, 0, 0)
    fixed = lambda n: (0, 0)
    o = pl.pallas_call(
        functools.partial(_nsa_decode_kernel, nb=nb),
        grid=(N,),
        in_specs=[pl.BlockSpec((1, N_HEADS, HEAD_DIM), seq3),
                  pl.BlockSpec((1, HEAD_DIM, N_HEADS), seq3),
                  pl.BlockSpec((1, KV_HEADS, nc, HEAD_DIM), seq4),
                  pl.BlockSpec((1, KV_HEADS, nc, HEAD_DIM), seq4),
                  pl.BlockSpec((1, 1, KV_HEADS, HEAD_DIM, P), lambda n: (0, n, 0, 0, 0)),
                  pl.BlockSpec((1, 1, KV_HEADS, HEAD_DIM, P), lambda n: (1, n, 0, 0, 0)),
                  pl.BlockSpec((1, 2, KV_HEADS, wb, HEAD_DIM), lambda n: (n, 0, 0, 0, 0)),
                  pl.BlockSpec((1, 1, 4 * SLOT), seq3),
                  pl.BlockSpec((1, 1, 2 * SLOT), seq3),
                  pl.BlockSpec((1, N_HEADS, 3), seq3),
                  pl.BlockSpec((nb, P), fixed), pl.BlockSpec((N_HEADS, nc), fixed),
                  pl.BlockSpec((N_HEADS, P), fixed), pl.BlockSpec((N_HEADS, wb), fixed),
                  pl.BlockSpec((N_HEADS, 1), fixed)],
        out_specs=pl.BlockSpec((1, N_HEADS, HEAD_DIM), seq3),
        out_shape=jax.ShapeDtypeStruct((N, N_HEADS, HEAD_DIM), f32),
        scratch_shapes=[pltpu.VMEM((8, nb), f32), pltpu.VMEM((8, P), f32)],
        compiler_params=pltpu.CompilerParams(dimension_semantics=("arbitrary",),
                                             vmem_limit_bytes=48 * 1024 * 1024),
        name="nsa_decode_attn",
    )(qs, qs.transpose(0, 2, 1), cmp[0], cmp[1], sel_kvt, sel_kvt, win, rows.reshape(N, 1, 4 * SLOT), wnew.reshape(N, 1, 2 * SLOT),
      gates[:, :3 * N_HEADS].reshape(N, N_HEADS, 3), onehot, b_cmp, b_sel, b_win, b_new)
    y = _proj_residual(x2, o.reshape(N, HQ), w_out)
    new_win = jnp.concatenate([win_buf, wnew.reshape(N, 1, 2, KV_HEADS, HEAD_DIM)], axis=1)[:, 1:]
    return y.reshape(N, 1, D), rows.reshape(N, 1, 4, KV_HEADS, HEAD_DIM), new_win


def _split3(x):
    hi = x.astype(MXU_DTYPE)
    r1 = x - hi.astype(jnp.float32)
    mid = r1.astype(MXU_DTYPE)
    return hi, mid, (r1 - mid.astype(jnp.float32)).astype(MXU_DTYPE)


def _nsa_sample_kernel(q_ref, kc_ref, vc_ref, ks_ref, vs_ref, win_ref, rows_ref, wnew_ref, g_ref,
                       oh_ref, grp_ref, own_ref, bc_ref, bs_ref, bw_ref, bn_ref, o_ref, *, nb):
    f32 = jnp.float32
    cd = MXU_DTYPE
    nt = (((1,), (1,)), ((), ()))
    q = q_ref[0]
    qf = q.astype(f32)

    def softmax_with_new(s, s_new):
        m = jnp.maximum(jnp.max(s, axis=1, keepdims=True), s_new)
        e = jnp.exp(s - m)
        e_new = jnp.exp(s_new - m)
        return e, e_new, 1.0 / (jnp.sum(e, axis=1, keepdims=True) + e_new)

    def new_key(row):
        return jnp.sum(qf * row.astype(cd).astype(f32), axis=1, keepdims=True) + bn_ref[...]

    sc = lax.dot_general(q, kc_ref[0], nt, preferred_element_type=f32) + bc_ref[...]
    ec = jnp.exp(sc - jnp.max(sc, axis=1, keepdims=True))
    pc = ec / jnp.sum(ec, axis=1, keepdims=True)
    out = g_ref[0][:, 0:1] * jnp.dot(pc.astype(cd), vc_ref[0], preferred_element_type=f32)

    imp = sum(pc[:, r * nb:(r + 1) * nb] for r in range(CMP_PER_SEL))
    imp = sum(jnp.dot(grp_ref[...], part, preferred_element_type=f32) for part in _split3(imp))
    blk = lax.broadcasted_iota(jnp.int32, imp.shape, 1)
    forced = (blk == 0) | (blk == nb - 1)
    score = jnp.where(forced, jnp.inf, imp)
    pen = jnp.full(imp.shape, NEG, f32)
    for _ in range(N_SEL - 1):
        best = jnp.max(score, axis=1, keepdims=True)
        first = jnp.min(jnp.where(score == best, blk, nb), axis=1, keepdims=True)
        hit = blk == first
        pen = jnp.where(hit, 0.0, pen)
        score = jnp.where(hit, -jnp.inf, score)

    ss = (lax.dot_general(q, ks_ref[0, 0].astype(cd), nt, preferred_element_type=f32)
          + jnp.dot(pen.astype(cd), oh_ref[...], preferred_element_type=f32) + bs_ref[...])
    rows = rows_ref[0]
    e, e_new, inv = softmax_with_new(ss, new_key(rows[:, 2 * SLOT:3 * SLOT]))
    acc = (jnp.dot(e.astype(cd), vs_ref[0, 0].astype(cd), preferred_element_type=f32)
           + e_new.astype(cd).astype(f32) * rows[:, 3 * SLOT:4 * SLOT].astype(cd).astype(f32))
    out = out + g_ref[0][:, 1:2] * inv * acc

    win = win_ref[0]
    wnew = wnew_ref[0]
    sw = lax.dot_general(q, win[:, :SLOT].astype(cd), nt, preferred_element_type=f32) + bw_ref[...]
    e, e_new, inv = softmax_with_new(sw, new_key(wnew[:, :SLOT]))
    acc = (jnp.dot(e.astype(cd), win[:, SLOT:].astype(cd), preferred_element_type=f32)
           + e_new.astype(cd).astype(f32) * wnew[:, SLOT:].astype(cd).astype(f32))
    out = out + g_ref[0][:, 2:3] * inv * acc
    o_ref[0] = out * own_ref[...]


def _nsa_sample_layer(x, past, win_buf, ln_g, w_in, w_out, q_gain, k_gain, pe, w1, w2, rel_bias):
    N, L, D = x.shape
    P = past.shape[2]
    wb = win_buf.shape[1]
    assert L == 1 and P % SEL_BLOCK == 0 and wb == WINDOW and P >= BIAS_SPAN and P // SEL_BLOCK >= N_SEL - 1
    f32 = jnp.float32
    cd = MXU_DTYPE
    nc = P // CMP_STRIDE
    nb = P // SEL_BLOCK
    x2 = x.reshape(N, D)
    q, rows, wnew, gates = _nsa_project(x2, ln_g, w_in, q_gain, k_gain, N, 1)
    cmp = _compress(past.reshape(4, N, nc, CMP_STRIDE * SLOT), pe, w1, w2, k_gain[0])
    cmp = cmp.reshape(2, N, nb, CMP_PER_SEL, SLOT).transpose(0, 1, 3, 2, 4).reshape(2, N, nc, SLOT).astype(cd)
    head_kv = np.arange(N_HEADS) // Q_PER_KV
    own = (head_kv[:, None] == (np.arange(SLOT) // HEAD_DIM)[None, :]).astype(np.float32)
    qbd = (jnp.tile(q.reshape(N, N_HEADS, HEAD_DIM), (1, 1, KV_HEADS)) * own * HEAD_DIM ** -0.5).astype(cd)
    grp = jnp.asarray(head_kv[:, None] == head_kv[None, :], dtype=cd)
    onehot = jnp.asarray(np.arange(nb)[:, None] == (np.arange(P) // SEL_BLOCK)[None, :], dtype=cd)
    fd = rel_bias[_bucket_of_distance()].T
    fd = fd - fd[:, BIAS_SPAN - 1:]
    cidx = (np.arange(nc) % nb) * CMP_PER_SEL + np.arange(nc) // nb
    dc = P - (cidx * CMP_STRIDE + CMP_LEN - 1)
    b_cmp = jnp.where(dc >= 0, jnp.take(fd, np.clip(dc, 0, BIAS_SPAN - 1), axis=1), NEG)
    b_sel = jnp.take(fd, np.clip(P - np.arange(P), 0, BIAS_SPAN - 1), axis=1)
    b_win = jnp.take(fd, np.clip(wb - np.arange(wb), 0, BIAS_SPAN - 1), axis=1)
    b_new = fd[:, 0:1]
    seq3 = lambda n: (n, 0, 0)
    fixed = lambda n: (0, 0)
    o = pl.pallas_call(
        functools.partial(_nsa_sample_kernel, nb=nb),
        grid=(N,),
        in_specs=[pl.BlockSpec((1, N_HEADS, SLOT), seq3),
                  pl.BlockSpec((1, nc, SLOT), seq3),
                  pl.BlockSpec((1, nc, SLOT), seq3),
                  pl.BlockSpec((1, 1, P, SLOT), lambda n: (2, n, 0, 0)),
                  pl.BlockSpec((1, 1, P, SLOT), lambda n: (3, n, 0, 0)),
                  pl.BlockSpec((1, wb, 2 * SLOT), seq3),
                  pl.BlockSpec((1, 1, 4 * SLOT), seq3),
                  pl.BlockSpec((1, 1, 2 * SLOT), seq3),
                  pl.BlockSpec((1, N_HEADS, 3), seq3),
                  pl.BlockSpec((nb, P), fixed), pl.BlockSpec((N_HEADS, N_HEADS), fixed),
                  pl.BlockSpec((N_HEADS, SLOT), fixed), pl.BlockSpec((N_HEADS, nc), fixed),
                  pl.BlockSpec((N_HEADS, P), fixed), pl.BlockSpec((N_HEADS, wb), fixed),
                  pl.BlockSpec((N_HEADS, 1), fixed)],
        out_specs=pl.BlockSpec((1, N_HEADS, SLOT), seq3),
        out_shape=jax.ShapeDtypeStruct((N, N_HEADS, SLOT), f32),
        compiler_params=pltpu.CompilerParams(dimension_semantics=("arbitrary",),
                                             vmem_limit_bytes=56 * 1024 * 1024),
        name="nsa_sample_attn",
    )(qbd, cmp[0], cmp[1], past, past, win_buf.reshape(N, wb, 2 * SLOT), rows.reshape(N, 1, 4 * SLOT),
      wnew.reshape(N, 1, 2 * SLOT), gates[:, :3 * N_HEADS].reshape(N, N_HEADS, 3),
      onehot, grp, jnp.asarray(own), b_cmp, b_sel, b_win, b_new)
    o = o.reshape(N, KV_HEADS, Q_PER_KV, KV_HEADS, HEAD_DIM)
    o = jnp.stack([o[:, k, :, k] for k in range(KV_HEADS)], axis=1).reshape(N, HQ)
    y = _proj_residual(x2, o, w_out)
    new_win = jnp.concatenate([win_buf, wnew.reshape(N, 1, 2, KV_HEADS, HEAD_DIM)], axis=1)[:, 1:]
    return y.reshape(N, 1, D), rows.reshape(N, 1, 4, KV_HEADS, HEAD_DIM), new_win


HQ = N_HEADS * HEAD_DIM
SLOT = KV_HEADS * HEAD_DIM
NSA_COLS = HQ + 6 * SLOT + 3 * N_HEADS
NSA_COLS_PAD = -(-NSA_COLS // LANES) * LANES
GATE_COLS = NSA_COLS_PAD - HQ - 6 * SLOT


def _nsa_proj_kernel(x_ref, g_ref, w_ref, qg_ref, kg_ref, *outs, prompt, n_blocks):
    f32 = jnp.float32
    x = x_ref[...]
    xn = x * lax.rsqrt(jnp.mean(x * x, axis=-1, keepdims=True) + EPS) * g_ref[...]
    h = jnp.dot(xn.astype(MXU_DTYPE), w_ref[...], preferred_element_type=f32)

    def head_norm(v, gain):
        return v * lax.rsqrt(jnp.mean(v * v, axis=-1, keepdims=True) + EPS) * gain

    def slot(s, kv):
        lo = HQ + s * SLOT + kv * HEAD_DIM
        return h[:, lo:lo + HEAD_DIM]

    gates = 1.0 / (1.0 + jnp.exp(-h[:, HQ + 6 * SLOT:]))
    if prompt:
        q_ref, rows_ref, win_ref, kat_ref, vsa_ref, kwt_ref, vwa_ref, c_ref, gate_ref = outs
        for hh in range(N_HEADS):
            qn = head_norm(h[:, hh * HEAD_DIM:(hh + 1) * HEAD_DIM], qg_ref[...])
            q_ref[0, hh] = (qn * (HEAD_DIM ** -0.5 * LOG2E)).astype(MXU_DTYPE)
        c_ref[0, 0] = h[:, HQ:HQ + SLOT].astype(MXU_DTYPE)
        c_ref[1, 0] = h[:, HQ + SLOT:HQ + 2 * SLOT].astype(MXU_DTYPE)
        rows, ka = x.shape[0], kat_ref.shape[3]
        nb = n_blocks
        first_key = (pl.program_id(0) % (nb * SEL_BLOCK // rows)) * rows
        key_blk = (first_key + lax.broadcasted_iota(jnp.int32, (nb, rows), 1)) // SEL_BLOCK
        onehot = jnp.where(key_blk == lax.broadcasted_iota(jnp.int32, (nb, rows), 0), 1.0, 0.0).astype(MXU_DTYPE)
        kpad = jnp.zeros((ka - nb - HEAD_DIM, rows), MXU_DTYPE)
        ones = jnp.ones((rows, LANES - HEAD_DIM), MXU_DTYPE)
    else:
        q_ref, rows_ref, win_ref, gate_ref = outs
        for hh in range(N_HEADS):
            q_ref[:, hh * HEAD_DIM:(hh + 1) * HEAD_DIM] = head_norm(h[:, hh * HEAD_DIM:(hh + 1) * HEAD_DIM],
                                                                    qg_ref[...])
    if prompt:
        rows_ref[0, 0:2 * SLOT, :] = h[:, HQ:HQ + 2 * SLOT].T
        rows_ref[0, 3 * SLOT:4 * SLOT, :] = h[:, HQ + 3 * SLOT:HQ + 4 * SLOT].T
    else:
        rows_ref[:, 0:2 * SLOT] = h[:, HQ:HQ + 2 * SLOT]
        rows_ref[:, 3 * SLOT:4 * SLOT] = h[:, HQ + 3 * SLOT:HQ + 4 * SLOT]
    for kv in range(KV_HEADS):
        lanes = slice(kv * HEAD_DIM, (kv + 1) * HEAD_DIM)
        ksn = head_norm(slot(2, kv), kg_ref[1:2, :])
        kwn = head_norm(slot(4, kv), kg_ref[2:3, :])
        if prompt:
            ksn_t = ksn.T
            rows_ref[0, 2 * SLOT + kv * HEAD_DIM:2 * SLOT + (kv + 1) * HEAD_DIM, :] = ksn_t
            win_ref[0, 0, :, lanes] = kwn
            kat_ref[0, kv, 0] = jnp.concatenate([onehot, ksn_t.astype(MXU_DTYPE), kpad], axis=0)
            kwt_ref[0, kv, 0] = kwn.T.astype(MXU_DTYPE)
            vsa_ref[0, kv, 0] = jnp.concatenate([slot(3, kv).astype(MXU_DTYPE), ones], axis=1)
            vwa_ref[0, kv, 0] = jnp.concatenate([slot(5, kv).astype(MXU_DTYPE), ones], axis=1)
        else:
            rows_ref[:, 2 * SLOT + kv * HEAD_DIM:2 * SLOT + (kv + 1) * HEAD_DIM] = ksn
            win_ref[:, lanes] = kwn
    if prompt:
        win_ref[0, 0, :, SLOT:2 * SLOT] = h[:, HQ + 5 * SLOT:HQ + 6 * SLOT]
    else:
        win_ref[:, SLOT:2 * SLOT] = h[:, HQ + 5 * SLOT:HQ + 6 * SLOT]
    if prompt:
        gate_ref[...] = jnp.zeros(gate_ref.shape, f32)
        per_kv = 3 * Q_PER_KV
        for kv in range(KV_HEADS):
            gate_ref[:, kv * LANES:kv * LANES + per_kv] = gates[:, kv * per_kv:(kv + 1) * per_kv]
    else:
        gate_ref[...] = gates


def _nsa_project(x, ln_g, w_in, q_gain, k_gain, batch, seq):
    T, D = x.shape
    f32 = jnp.float32
    cd = MXU_DTYPE
    prompt = seq > 1
    tb = TK if prompt else T
    w = jnp.pad(w_in, ((0, 0), (0, NSA_COLS_PAD - NSA_COLS))).astype(cd)
    fixed = lambda i: (0, 0)
    row = lambda i: (i, 0)
    in_specs = [pl.BlockSpec((tb, D), row), pl.BlockSpec((1, D), fixed), pl.BlockSpec((D, NSA_COLS_PAD), fixed),
                pl.BlockSpec((1, HEAD_DIM), fixed), pl.BlockSpec((3, HEAD_DIM), fixed)]
    if prompt:
        assert seq % tb == 0 and WINDOW == 2 * tb
        n = seq // tb
        n_blocks = seq // SEL_BLOCK
        ka = -(-(n_blocks + HEAD_DIM) // LANES) * LANES
        chunk = lambda i: (i // n, 0, i % n, 0, 0)
        kt_spec = pl.BlockSpec((1, KV_HEADS, 1, HEAD_DIM, tb), chunk)
        ka_spec = pl.BlockSpec((1, KV_HEADS, 1, ka, tb), chunk)
        v_spec = pl.BlockSpec((1, KV_HEADS, 1, tb, LANES), chunk)
        kt_shape = jax.ShapeDtypeStruct((batch, KV_HEADS, n, HEAD_DIM, tb), cd)
        ka_shape = jax.ShapeDtypeStruct((batch, KV_HEADS, n, ka, tb), cd)
        v_shape = jax.ShapeDtypeStruct((batch, KV_HEADS, n, tb, LANES), cd)
        out_specs = [
            pl.BlockSpec((1, N_HEADS, tb, HEAD_DIM), lambda i: (i // n, 0, i % n, 0)),
            pl.BlockSpec((1, 4 * SLOT, tb), lambda i: (i // n, 0, i % n)),
            pl.BlockSpec((1, 1, tb, 2 * SLOT), lambda i: (i // n, jnp.maximum(i % n - (n - 2), 0), 0, 0)),
            ka_spec, v_spec, kt_spec, v_spec,
            pl.BlockSpec((2, 1, tb, SLOT), lambda i: (0, i // n, i % n, 0)),
            pl.BlockSpec((tb, KV_HEADS * LANES), row)]
        out_shape = [
            jax.ShapeDtypeStruct((batch, N_HEADS, seq, HEAD_DIM), cd),
            jax.ShapeDtypeStruct((batch, 4 * SLOT, seq), f32),
            jax.ShapeDtypeStruct((batch, 2, tb, 2 * SLOT), f32),
            ka_shape, v_shape, kt_shape, v_shape,
            jax.ShapeDtypeStruct((2, batch, seq, SLOT), cd),
            jax.ShapeDtypeStruct((T, KV_HEADS * LANES), f32)]
    else:
        out_specs = [pl.BlockSpec((tb, HQ), row), pl.BlockSpec((tb, 4 * SLOT), row),
                     pl.BlockSpec((tb, 2 * SLOT), row), pl.BlockSpec((tb, GATE_COLS), row)]
        n_blocks = 0
        out_shape = [jax.ShapeDtypeStruct((T, HQ), f32), jax.ShapeDtypeStruct((T, 4 * SLOT), f32),
                     jax.ShapeDtypeStruct((T, 2 * SLOT), f32), jax.ShapeDtypeStruct((T, GATE_COLS), f32)]
    return pl.pallas_call(
        functools.partial(_nsa_proj_kernel, prompt=prompt, n_blocks=n_blocks),
        grid=(T // tb,),
        in_specs=in_specs, out_specs=out_specs, out_shape=out_shape,
        compiler_params=pltpu.CompilerParams(dimension_semantics=("arbitrary",),
                                             vmem_limit_bytes=48 * 1024 * 1024),
        name="nsa_proj",
    )(x, ln_g.reshape(1, D), w, q_gain.reshape(1, HEAD_DIM), k_gain)


def _compress_kernel(c_ref, w1_ref, pe_ref, w2_ref, kg_ref, o_ref):
    f32 = jnp.float32
    pairs = SLOT // LANES
    c = c_ref[0, 0]
    u = [jnp.dot(jnp.concatenate([c[:, l * SLOT + p * LANES:l * SLOT + (p + 1) * LANES]
                                  for l in range(CMP_STRIDE)], axis=1).astype(MXU_DTYPE),
                 w1_ref[0], preferred_element_type=f32) for p in range(pairs)]
    first = jnp.concatenate([up[:, :LANES] for up in u], axis=1)
    second = jnp.concatenate([up[:, LANES:] for up in u], axis=1)
    pe = jnp.dot(pe_ref[0], w1_ref[0], preferred_element_type=f32)
    pe = jnp.concatenate([pe[0:1, :LANES] + pe[1:2, LANES:]] * pairs, axis=1)
    nc = first.shape[0]
    hid = first + pltpu.roll(second, nc - 1, 0) + pe
    out = jnp.dot(_gelu_tanh(hid).astype(MXU_DTYPE), w2_ref[0], preferred_element_type=f32)
    is_key = pl.program_id(1) == 0
    for kv in range(KV_HEADS):
        lanes = slice(kv * HEAD_DIM, (kv + 1) * HEAD_DIM)
        v = out[:, lanes]
        vn = v * lax.rsqrt(jnp.mean(v * v, axis=-1, keepdims=True) + EPS) * kg_ref[...]
        o_ref[0, 0, :, lanes] = jnp.where(is_key, vn, v)


def _compress(c, pe, w1, w2, k_gain0):
    n, nc = c.shape[1:3]
    cd = MXU_DTYPE
    kc = CMP_STRIDE * SLOT
    eye = jnp.eye(KV_HEADS, dtype=jnp.float32)
    per_tile = LANES // HEAD_DIM
    kp = CMP_STRIDE * LANES
    w1h = w1.reshape(2, 2, CMP_STRIDE, HEAD_DIM, CMP_HID)
    w1b = jnp.einsum('shldf,kj->slkdhjf', w1h, eye[:per_tile, :per_tile]).reshape(2, kp, 2 * LANES).astype(cd)
    w2b = jnp.einsum('sdf,kj->skdjf', w2, eye).reshape(2, SLOT, SLOT).astype(cd)
    peh = jnp.broadcast_to(pe.reshape(2, 2, CMP_STRIDE, 1, HEAD_DIM), (2, 2, CMP_STRIDE, per_tile, HEAD_DIM))
    peh = jnp.pad(peh.reshape(2, 2, kp), ((0, 0), (0, 14), (0, 0))).astype(cd)
    return pl.pallas_call(
        _compress_kernel,
        grid=(n, 2),
        in_specs=[pl.BlockSpec((1, 1, nc, kc), lambda b, s: (s, b, 0, 0)),
                  pl.BlockSpec((1, kp, 2 * LANES), lambda b, s: (s, 0, 0)),
                  pl.BlockSpec((1, 16, kp), lambda b, s: (s, 0, 0)),
                  pl.BlockSpec((1, SLOT, SLOT), lambda b, s: (s, 0, 0)),
                  pl.BlockSpec((1, HEAD_DIM), lambda b, s: (0, 0))],
        out_specs=pl.BlockSpec((1, 1, nc, SLOT), lambda b, s: (s, b, 0, 0)),
        out_shape=jax.ShapeDtypeStruct((2, n, nc, SLOT), jnp.float32),
        compiler_params=pltpu.CompilerParams(dimension_semantics=("arbitrary", "arbitrary"),
                                             vmem_limit_bytes=48 * 1024 * 1024),
        name="nsa_compress",
    )(c, w1b, peh, w2b, k_gain0.reshape(1, HEAD_DIM))


def _proj_residual_kernel(x_ref, a_ref, w_ref, o_ref):
    o_ref[...] = x_ref[...] + jnp.dot(a_ref[...].astype(MXU_DTYPE), w_ref[...],
                                      preferred_element_type=jnp.float32)


def _proj_residual(x, a, w):
    T, D = x.shape
    k = a.shape[1]
    tb = min(512, T)
    assert T % tb == 0
    return pl.pallas_call(
        _proj_residual_kernel,
        grid=(T // tb,),
        in_specs=[pl.BlockSpec((tb, D), lambda i: (i, 0)), pl.BlockSpec((tb, k), lambda i: (i, 0)),
                  pl.BlockSpec((k, D), lambda i: (0, 0))],
        out_specs=pl.BlockSpec((tb, D), lambda i: (i, 0)),
        out_shape=jax.ShapeDtypeStruct((T, D), jnp.float32),
        compiler_params=pltpu.CompilerParams(dimension_semantics=("arbitrary",)),
        name="proj_residual",
    )(x, a, w.astype(MXU_DTYPE))


MOE_ROWS = 256
ROUTE_COLS = LANES
ROW_DMA_UNROLL = 8


def _moe_route_kernel(x_ref, g_ref, w_ref, b_ref, tri_ref, h_ref, meta_ref, wt_ref, cnt_ref, carry_sc):
    f32 = jnp.float32
    step = pl.program_id(0)

    @pl.when(step == 0)
    def _():
        carry_sc[...] = jnp.zeros(carry_sc.shape, f32)

    x = x_ref[...]
    h = x * lax.rsqrt(jnp.mean(x * x, axis=-1, keepdims=True) + EPS) * g_ref[...]
    h_ref[...] = h
    logits = jnp.dot(h.astype(MXU_DTYPE), w_ref[...], preferred_element_type=f32) + b_ref[...]
    tb = logits.shape[0]
    col = lax.broadcasted_iota(jnp.int32, (tb, ROUTE_COLS), 1)

    def first_max(vals):
        best = jnp.max(vals, axis=1, keepdims=True)
        return best, jnp.min(jnp.where(vals == best, col, ROUTE_COLS), axis=1, keepdims=True)

    lg = jnp.where(col < N_GROUPS, logits, -jnp.inf)
    g_best, grp = first_max(lg)
    g_w = 1.0 / jnp.sum(jnp.exp(lg - g_best), axis=1, keepdims=True)
    lo = N_GROUPS + EXPERTS_PER_GROUP * grp
    le = jnp.where((col >= lo) & (col < lo + EXPERTS_PER_GROUP), logits, -jnp.inf)
    v0, c0 = first_max(le)
    v1, c1 = first_max(jnp.where(col == c0, -jnp.inf, le))
    e1 = jnp.exp(v1 - v0)
    w0 = g_w / (1.0 + e1)
    w1 = g_w * e1 / (1.0 + e1)
    chosen = (col == c0) | (col == c1)
    before = jnp.dot(tri_ref[...], jnp.where(chosen, 1.0, 0.0).astype(MXU_DTYPE),
                     preferred_element_type=f32) + carry_sc[...]
    r0 = jnp.sum(jnp.where(col == c0, before, 0.0), axis=1, keepdims=True).astype(jnp.int32)
    r1 = jnp.sum(jnp.where(col == c1, before, 0.0), axis=1, keepdims=True).astype(jnp.int32)
    carry_sc[...] += jnp.sum(jnp.where(chosen, 1.0, 0.0), axis=0, keepdims=True)
    cnt_ref[...] = carry_sc[...]
    meta_ref[...] = jnp.where(col == 0, c0 - N_GROUPS, jnp.where(col == 1, c1 - N_GROUPS,
                              jnp.where(col == 2, r0, jnp.where(col == 3, r1, 0))))
    wt_ref[...] = jnp.where(col == 0, w0, jnp.where(col == 1, w1, 0.0))


def _moe_dispatch_kernel(dest_ref, h_ref, xbuf_in, xbuf_ref, sem):
    del xbuf_in
    tb = h_ref.shape[0]

    def row_copy(r, k):
        return pltpu.make_async_copy(h_ref.at[pl.ds(r, 1)], xbuf_ref.at[pl.ds(dest_ref[0, 0, 2 * r + k], 1)], sem)

    def start(r, c):
        row_copy(r, 0).start()
        row_copy(r, 1).start()
        return c

    def wait(r, c):
        row_copy(r, 0).wait()
        row_copy(r, 1).wait()
        return c

    lax.fori_loop(0, tb, start, 0, unroll=ROW_DMA_UNROLL)
    lax.fori_loop(0, tb, wait, 0, unroll=ROW_DMA_UNROLL)


def _moe_expert_kernel(blk_e_ref, nblk_ref, x_ref, wgu_ref, wdn_ref, y_ref, wgu_sc, wdn_sc):
    i = pl.program_id(0)
    f32 = jnp.float32

    @pl.when(i < nblk_ref[0])
    def _():
        changed = jnp.logical_or(i == 0, blk_e_ref[i] != blk_e_ref[jnp.maximum(i - 1, 0)])

        @pl.when(changed)
        def _():
            wgu_sc[...] = wgu_ref[0, 0].astype(MXU_DTYPE)
            wdn_sc[...] = wdn_ref[0, 0].astype(MXU_DTYPE)

        gu = jnp.dot(x_ref[...].astype(MXU_DTYPE), wgu_sc[...], preferred_element_type=f32)
        gate = gu[:, :D_EXPERT]
        act = gate * (1.0 / (1.0 + jnp.exp(-gate))) * gu[:, D_EXPERT:]
        y_ref[...] = jnp.dot(act.astype(MXU_DTYPE), wdn_sc[...], preferred_element_type=f32)

    @pl.when(i >= nblk_ref[0])
    def _():
        y_ref[...] = jnp.zeros(y_ref.shape, f32)


def _moe_combine_kernel(dest_ref, x_ref, wt_ref, ybuf_ref, o_ref, rows_sc, sem):
    tb = x_ref.shape[0]

    def row_copy(r, k):
        return pltpu.make_async_copy(ybuf_ref.at[pl.ds(dest_ref[0, 0, 2 * r + k], 1)],
                                     rows_sc.at[k, pl.ds(r, 1)], sem)

    def start(r, c):
        row_copy(r, 0).start()
        row_copy(r, 1).start()
        return c

    def wait(r, c):
        row_copy(r, 0).wait()
        row_copy(r, 1).wait()
        return c

    lax.fori_loop(0, tb, start, 0, unroll=ROW_DMA_UNROLL)
    lax.fori_loop(0, tb, wait, 0, unroll=ROW_DMA_UNROLL)
    wt = wt_ref[...]
    o_ref[...] = x_ref[...] + (wt[:, 0:1] * rows_sc[0] + wt[:, 1:2] * rows_sc[1])


def _hier_moe_residual(x, ln_g, w_grp, b_grp, w_exp, b_exp, w_gu, w_dn, layer):
    T, D = x.shape
    f32 = jnp.float32
    tb = min(256, T)
    assert T % tb == 0
    nt = T // tb
    pad_cols = ROUTE_COLS - N_GROUPS - N_EXPERTS
    w_r = jnp.pad(jnp.concatenate([w_grp, w_exp], axis=1), ((0, 0), (0, pad_cols))).astype(MXU_DTYPE)
    b_r = jnp.pad(jnp.concatenate([b_grp, b_exp]), (0, pad_cols)).reshape(1, ROUTE_COLS)
    tri = jnp.asarray(np.tril(np.ones((tb, tb), np.float32), -1), dtype=MXU_DTYPE)
    row = lambda i: (i, 0)
    fixed = lambda i: (0, 0)
    h, meta, wt, cnt = pl.pallas_call(
        _moe_route_kernel,
        grid=(nt,),
        in_specs=[pl.BlockSpec((tb, D), row), pl.BlockSpec((1, D), fixed),
                  pl.BlockSpec((D, ROUTE_COLS), fixed), pl.BlockSpec((1, ROUTE_COLS), fixed),
                  pl.BlockSpec((tb, tb), fixed)],
        out_specs=[pl.BlockSpec((tb, D), row), pl.BlockSpec((tb, ROUTE_COLS), row),
                   pl.BlockSpec((tb, ROUTE_COLS), row), pl.BlockSpec((1, ROUTE_COLS), fixed)],
        out_shape=[jax.ShapeDtypeStruct((T, D), f32), jax.ShapeDtypeStruct((T, ROUTE_COLS), jnp.int32),
                   jax.ShapeDtypeStruct((T, ROUTE_COLS), f32), jax.ShapeDtypeStruct((1, ROUTE_COLS), f32)],
        scratch_shapes=[pltpu.VMEM((1, ROUTE_COLS), f32)],
        compiler_params=pltpu.CompilerParams(dimension_semantics=("arbitrary",)),
        name="moe_route",
    )(x, ln_g.reshape(1, D), w_r, b_r, tri)

    counts = cnt[0, N_GROUPS:N_GROUPS + N_EXPERTS].astype(jnp.int32)
    padded = (counts + MOE_ROWS - 1) // MOE_ROWS * MOE_ROWS
    pad_end = jnp.cumsum(padded)
    pad_start = pad_end - padded
    n_blocks = -(-(T * TOP_K) // MOE_ROWS) + N_EXPERTS
    n_slots = n_blocks * MOE_ROWS
    dest = pad_start[meta[:, 0:2]] + meta[:, 2:4]
    dest = dest.reshape(nt, 1, 2 * tb)
    starts = jnp.arange(n_blocks, dtype=jnp.int32) * MOE_ROWS
    used = (pad_end[-1] // MOE_ROWS).astype(jnp.int32).reshape(1)
    blk_start = jnp.minimum(starts, pad_end[-1] - 1)
    blk_e = jnp.minimum(jnp.sum(pad_end[None, :] <= blk_start[:, None], axis=1), N_EXPERTS - 1).astype(jnp.int32)

    dest_spec = pl.BlockSpec((1, 1, 2 * tb), lambda i: (i, 0, 0), memory_space=pltpu.SMEM)
    xbuf = pl.pallas_call(
        _moe_dispatch_kernel,
        grid=(nt,),
        in_specs=[dest_spec, pl.BlockSpec((tb, D), row), pl.BlockSpec(memory_space=pl.ANY)],
        out_specs=pl.BlockSpec(memory_space=pl.ANY),
        out_shape=jax.ShapeDtypeStruct((n_slots, D), f32),
        scratch_shapes=[pltpu.SemaphoreType.DMA(())],
        input_output_aliases={2: 0},
        compiler_params=pltpu.CompilerParams(dimension_semantics=("arbitrary",)),
        name="moe_dispatch",
    )(dest, h, jnp.zeros((n_slots, D), f32))

    ybuf = pl.pallas_call(
        _moe_expert_kernel,
        grid_spec=pltpu.PrefetchScalarGridSpec(
            num_scalar_prefetch=2,
            grid=(n_blocks,),
            in_specs=[pl.BlockSpec((MOE_ROWS, D), lambda i, be, nb: (i, 0)),
                      pl.BlockSpec((1, 1, D, 2 * D_EXPERT), lambda i, be, nb: (layer, be[i], 0, 0)),
                      pl.BlockSpec((1, 1, D_EXPERT, D), lambda i, be, nb: (layer, be[i], 0, 0))],
            out_specs=pl.BlockSpec((MOE_ROWS, D), lambda i, be, nb: (i, 0)),
            scratch_shapes=[pltpu.VMEM((D, 2 * D_EXPERT), MXU_DTYPE), pltpu.VMEM((D_EXPERT, D), MXU_DTYPE)]),
        out_shape=jax.ShapeDtypeStruct((n_slots, D), f32),
        compiler_params=pltpu.CompilerParams(dimension_semantics=("arbitrary",),
                                             vmem_limit_bytes=48 * 1024 * 1024),
        name="moe_experts",
    )(blk_e, used, xbuf, w_gu, w_dn)

    return pl.pallas_call(
        _moe_combine_kernel,
        grid=(nt,),
        in_specs=[dest_spec, pl.BlockSpec((tb, D), row), pl.BlockSpec((tb, ROUTE_COLS), row),
                  pl.BlockSpec(memory_space=pl.ANY)],
        out_specs=pl.BlockSpec((tb, D), row),
        out_shape=jax.ShapeDtypeStruct((T, D), f32),
        scratch_shapes=[pltpu.VMEM((TOP_K, tb, D), f32), pltpu.SemaphoreType.DMA(())],
        compiler_params=pltpu.CompilerParams(dimension_semantics=("arbitrary",)),
        name="moe_combine",
    )(dest, x, wt, ybuf)


def _gelu_tanh(x):
    return 0.5 * x * (1.0 + jnp.tanh(math.sqrt(2.0 / math.pi) * (x + 0.044715 * (x * x * x))))


def _gmlp_kernel(x_ref, g_ref, win_ref, bin_ref, lng_ref, lnb_ref, ws_ref, bs_ref, wout_ref,
                 o_ref, v_ref, *, single_position):
    f32 = jnp.float32
    x = x_ref[...]
    rows = x.shape[0]
    h = x * lax.rsqrt(jnp.mean(x * x, axis=-1, keepdims=True) + EPS) * g_ref[...]
    z = _gelu_tanh(jnp.dot(h.astype(MXU_DTYPE), win_ref[...], preferred_element_type=f32) + bin_ref[...])
    u = z[:, :GM_WIDTH]
    v = z[:, GM_WIDTH:]
    mu = jnp.mean(v, axis=-1, keepdims=True)
    var = jnp.mean(jnp.square(v - mu), axis=-1, keepdims=True)
    v = (v - mu) * lax.rsqrt(var + EPS) * lng_ref[...] + lnb_ref[...]
    v_ref[0] = v
    vb = v.astype(MXU_DTYPE)
    if single_position:
        s = ws_ref[...].astype(f32) * vb.astype(f32) + bs_ref[...]
    else:
        parts = []
        for c in range(rows // CHUNK):
            vc = vb[c * CHUNK:(c + 1) * CHUNK]
            parts.append(jnp.concatenate(
                [jnp.dot(ws_ref[g], vc[:, g * GM_GROUP_DIM:(g + 1) * GM_GROUP_DIM], preferred_element_type=f32)
                 + bs_ref[g] for g in range(GM_GROUPS)], axis=1))
        s = jnp.concatenate(parts, axis=0)
    y = jnp.dot((u * s).astype(MXU_DTYPE), wout_ref[...], preferred_element_type=f32)
    o_ref[...] = x + y


def _gmlp_residual(x, ln_g, w_in, b_in, ln2_g, ln2_b, w_s, b_s, w_out, seq):
    T, D = x.shape
    f32 = jnp.float32
    cd = MXU_DTYPE
    single = seq == 1
    tb = T if single else 2 * CHUNK
    assert T % tb == 0 and (single or seq % tb == 0)
    steps_per_seq = 1 if single else seq // tb
    if single:
        ws = jnp.repeat(w_s[:, 0, 0], GM_GROUP_DIM).reshape(1, GM_WIDTH).astype(cd)
        bs = jnp.repeat(b_s[:, 0], GM_GROUP_DIM).reshape(1, GM_WIDTH)
        ws_spec = pl.BlockSpec((1, GM_WIDTH), lambda i: (0, 0))
        bs_spec = pl.BlockSpec((1, GM_WIDTH), lambda i: (0, 0))
    else:
        causal = np.tril(np.ones((CHUNK, CHUNK), bool))
        ws = jnp.where(causal[None], w_s, 0).astype(cd)
        bs = jnp.broadcast_to(b_s[:, :, None], (GM_GROUPS, CHUNK, GM_GROUP_DIM))
        ws_spec = pl.BlockSpec((GM_GROUPS, CHUNK, CHUNK), lambda i: (0, 0, 0))
        bs_spec = pl.BlockSpec((GM_GROUPS, CHUNK, GM_GROUP_DIM), lambda i: (0, 0, 0))
    fixed = lambda i: (0, 0)
    row = lambda i: (i, 0)
    return pl.pallas_call(
        functools.partial(_gmlp_kernel, single_position=single),
        grid=(T // tb,),
        in_specs=[pl.BlockSpec((tb, D), row), pl.BlockSpec((1, D), fixed),
                  pl.BlockSpec((D, 2 * GM_WIDTH), fixed), pl.BlockSpec((1, 2 * GM_WIDTH), fixed),
                  pl.BlockSpec((1, GM_WIDTH), fixed), pl.BlockSpec((1, GM_WIDTH), fixed),
                  ws_spec, bs_spec, pl.BlockSpec((GM_WIDTH, D), fixed)],
        out_specs=[pl.BlockSpec((tb, D), row), pl.BlockSpec((1, tb, GM_WIDTH), lambda i: (i // steps_per_seq, 0, 0))],
        out_shape=[jax.ShapeDtypeStruct((T, D), f32),
                   jax.ShapeDtypeStruct((T // (tb * steps_per_seq), tb, GM_WIDTH), f32)],
        compiler_params=pltpu.CompilerParams(dimension_semantics=("arbitrary",),
                                             vmem_limit_bytes=56 * 1024 * 1024),
        name="gmlp",
    )(x, ln_g.reshape(1, D), w_in.astype(cd), b_in.reshape(1, -1), ln2_g.reshape(1, -1), ln2_b.reshape(1, -1),
      ws, bs, w_out.astype(cd))


def rmsnorm(x, g):
    xf = x.astype(jnp.float32)
    y = xf * lax.rsqrt(jnp.mean(xf * xf, axis=-1, keepdims=True) + EPS)
    return (y * g.astype(jnp.float32)).astype(x.dtype)


def layernorm(x, g, b):
    xf = x.astype(jnp.float32)
    mu = jnp.mean(xf, axis=-1, keepdims=True)
    var = jnp.mean(jnp.square(xf - mu), axis=-1, keepdims=True)
    return ((xf - mu) * lax.rsqrt(var + EPS) * g.astype(jnp.float32) + b.astype(jnp.float32)).astype(x.dtype)


def rel_bucket(n):
    n = jnp.maximum(n, 0)
    max_exact = N_BUCKETS // 2
    nf = jnp.maximum(n, 1).astype(jnp.float32)
    large = max_exact + (jnp.log(nf / max_exact) / math.log(MAX_DISTANCE / max_exact)
                         * (N_BUCKETS - max_exact)).astype(jnp.int32)
    large = jnp.minimum(large, N_BUCKETS - 1)
    return jnp.where(n < max_exact, n, large)


def head_bias(rel, rel_bias):
    q, k = rel.shape
    b = rel_bias[rel_bucket(rel)]
    return b.reshape(q, k, KV_HEADS, Q_PER_KV).transpose(2, 3, 0, 1)


def masked_softmax(logits, mask):
    z = jnp.where(mask, logits.astype(jnp.float32), -1e30)
    return jax.nn.softmax(z, axis=-1) * mask


def nsa_project(xn, w_in, q_gain, k_gain):
    B, T, _ = xn.shape
    hq = N_HEADS * HEAD_DIM
    hkv = 6 * KV_HEADS * HEAD_DIM
    h = xn @ w_in
    q = rmsnorm(h[..., :hq].reshape(B, T, N_HEADS, HEAD_DIM), q_gain)
    kv = h[..., hq:hq + hkv].reshape(B, T, 6, KV_HEADS, HEAD_DIM)
    gates = jax.nn.sigmoid(h[..., hq + hkv:].astype(jnp.float32)).reshape(B, T, N_HEADS, 3)
    k_sel = rmsnorm(kv[:, :, 2], k_gain[1])
    k_win = rmsnorm(kv[:, :, 4], k_gain[2])
    return q, kv[:, :, 0], kv[:, :, 1], k_sel, kv[:, :, 3], k_win, kv[:, :, 5], gates


def compress(rows, pe, w1, w2):
    B, T = rows.shape[:2]
    n_ch = T // CMP_STRIDE
    c = rows[:, :n_ch * CMP_STRIDE].reshape(B, n_ch, CMP_STRIDE, KV_HEADS, HEAD_DIM)
    blk = jnp.concatenate([c[:, :-1], c[:, 1:]], axis=2) + pe[None, None, :, None, :]
    h = jax.nn.gelu(jnp.einsum('bnlkd,ldf->bnkf', blk, w1.reshape(CMP_LEN, HEAD_DIM, CMP_HID)))
    return jnp.einsum('bnkf,fd->bnkd', h, w2)


def compressed_kv(k_raw, v_raw, k_gain0, pe, w1, w2):
    kc = rmsnorm(compress(k_raw, pe[0], w1[0], w2[0]), k_gain0)
    vc = compress(v_raw, pe[1], w1[1], w2[1])
    end = jnp.arange(kc.shape[1]) * CMP_STRIDE + (CMP_LEN - 1)
    return kc, vc, end


def nsa_core(q, q_pos, kc, vc, kc_end, ks, vs, kw, vw, kw_pos, gates, rel_bias):
    B, Q = q.shape[:2]
    nc = kc.shape[1]
    scale = HEAD_DIM ** -0.5
    qh = q.reshape(B, Q, KV_HEADS, Q_PER_KV, HEAD_DIM)
    lc = jnp.einsum('bqkgd,bnkd->bkgqn', qh, kc) * scale + head_bias(q_pos[:, None] - kc_end[None, :], rel_bias)
    pc = masked_softmax(lc, kc_end[None, :] <= q_pos[:, None])
    oc = jnp.einsum('bkgqn,bnkd->bqkgd', pc.astype(vc.dtype), vc)
    nb = ks.shape[1] // SEL_BLOCK
    imp = jnp.pad(pc.sum(axis=2), ((0, 0), (0, 0), (0, 0), (0, nb * CMP_PER_SEL - nc)))
    imp = imp.reshape(B, KV_HEADS, Q, nb, CMP_PER_SEL).sum(-1)
    blk = jnp.arange(nb)
    d = (q_pos // SEL_BLOCK)[:, None] - blk[None, :]
    forced = (blk[None, :] == 0) | ((d >= 0) & (d < N_LOCAL))
    score = jnp.where(forced, jnp.inf, jnp.where(d >= 0, imp, -jnp.inf))
    n_sel = min(N_SEL, nb)
    _, idx = lax.top_k(score, n_sel)
    bi = jnp.arange(B)[:, None, None, None]
    ki = jnp.arange(KV_HEADS)[None, :, None, None]
    ksb = ks.reshape(B, nb, SEL_BLOCK, KV_HEADS, HEAD_DIM).transpose(0, 3, 1, 2, 4)
    vsb = vs.reshape(B, nb, SEL_BLOCK, KV_HEADS, HEAD_DIM).transpose(0, 3, 1, 2, 4)
    kg = ksb[bi, ki, idx].reshape(B, KV_HEADS, Q, n_sel * SEL_BLOCK, HEAD_DIM)
    vg = vsb[bi, ki, idx].reshape(B, KV_HEADS, Q, n_sel * SEL_BLOCK, HEAD_DIM)
    kpos = (idx[..., None] * SEL_BLOCK + jnp.arange(SEL_BLOCK)).reshape(B, KV_HEADS, Q, n_sel * SEL_BLOCK)
    rel_s = q_pos[None, None, :, None] - kpos
    tb = rel_bias.reshape(N_BUCKETS, KV_HEADS, Q_PER_KV)
    bs = tb[rel_bucket(rel_s), ki].transpose(0, 1, 4, 2, 3)
    ls = jnp.einsum('bqkgd,bkqjd->bkgqj', qh, kg) * scale + bs
    ps = masked_softmax(ls, (rel_s >= 0)[:, :, None])
    os_ = jnp.einsum('bkgqj,bkqjd->bqkgd', ps.astype(vg.dtype), vg)
    rel_w = q_pos[:, None] - kw_pos[None, :]
    mw = (rel_w >= 0) & (rel_w <= WINDOW) & (kw_pos[None, :] >= 0)
    lw = jnp.einsum('bqkgd,bwkd->bkgqw', qh, kw) * scale + head_bias(rel_w, rel_bias)
    pw = masked_softmax(lw, mw)
    ow = jnp.einsum('bkgqw,bwkd->bqkgd', pw.astype(vw.dtype), vw)
    g = gates.reshape(B, Q, KV_HEADS, Q_PER_KV, 3).astype(oc.dtype)
    o = g[..., 0:1] * oc + g[..., 1:2] * os_ + g[..., 2:3] * ow
    return o.reshape(B, Q, N_HEADS * HEAD_DIM)


def nsa_prompt(xn, w_in, w_out, q_gain, k_gain, pe, w1, w2, rel_bias):
    B, S, _ = xn.shape
    q, kcr, vcr, ks, vs, kw, vw, gates = nsa_project(xn, w_in, q_gain, k_gain)
    kc, vc, kc_end = compressed_kv(kcr, vcr, k_gain[0], pe, w1, w2)
    o = _nsa_attention(q, kc, vc, ks, vs, kw, vw, gates, rel_bias)
    w_keep = min(WINDOW, S)
    new_rows = jnp.stack([kcr, vcr, ks, vs], axis=2)
    new_win = jnp.stack([kw[:, S - w_keep:], vw[:, S - w_keep:]], axis=2)
    y = _mm(o.reshape(B * S, -1), w_out).reshape(B, S, -1)
    return y, new_rows, new_win


def nsa_sample(xn, past_rows, win_buf, w_in, w_out, q_gain, k_gain, pe, w1, w2, rel_bias):
    B, L, _ = xn.shape
    P = past_rows.shape[1]
    q, kcr, vcr, ks, vs, kw, vw, gates = nsa_project(xn, w_in, q_gain, k_gain)
    new_rows = jnp.stack([kcr, vcr, ks, vs], axis=2)
    rows = jnp.concatenate([past_rows, new_rows], axis=1)
    kc, vc, kc_end = compressed_kv(rows[:, :, 0], rows[:, :, 1], k_gain[0], pe, w1, w2)
    T = P + L
    tp = -(-T // SEL_BLOCK) * SEL_BLOCK
    pad = ((0, 0), (0, tp - T), (0, 0), (0, 0))
    ks_all = jnp.pad(rows[:, :, 2], pad)
    vs_all = jnp.pad(rows[:, :, 3], pad)
    win_all = jnp.concatenate([win_buf, jnp.stack([kw, vw], axis=2)], axis=1)
    wb = win_buf.shape[1]
    kw_pos = P - wb + jnp.arange(wb + L)
    q_pos = P + jnp.arange(L)
    o = nsa_core(q, q_pos, kc, vc, kc_end, ks_all, vs_all, win_all[:, :, 0], win_all[:, :, 1],
                 kw_pos, gates, rel_bias)
    w_keep = min(WINDOW, wb + L)
    return o @ w_out, new_rows, win_all[:, wb + L - w_keep:]


def gmlp_mixer(xn, w_in, b_in, ln_g, ln_b, w_s, b_s, w_out):
    B, L, _ = xn.shape
    z = jax.nn.gelu(xn @ w_in + b_in)
    u = z[..., :GM_WIDTH]
    v = layernorm(z[..., GM_WIDTH:], ln_g, ln_b)
    lp = -(-L // CHUNK) * CHUNK
    vp = jnp.pad(v, ((0, 0), (0, lp - L), (0, 0))).reshape(B, lp // CHUNK, CHUNK, GM_GROUPS, GM_GROUP_DIM)
    causal = jnp.tril(jnp.ones((CHUNK, CHUNK), dtype=bool))
    ws = jnp.where(causal[None], w_s, 0)
    s = jnp.einsum('gts,bcsgd->bctgd', ws, vp) + b_s.T[None, None, :, :, None]
    s = s.reshape(B, lp, GM_WIDTH)[:, :L]
    start = ((L - 1) // CHUNK) * CHUNK
    return (u * s) @ w_out, v[:, start:]


def hier_moe(x, w_grp, b_grp, w_exp, b_exp, w_gu, w_dn):
    T, D = x.shape
    pg = jax.nn.softmax((x @ w_grp + b_grp).astype(jnp.float32), axis=-1)
    grp = jnp.argmax(pg, axis=-1)
    g_w = jnp.take_along_axis(pg, grp[:, None], axis=-1)
    le = (x @ w_exp + b_exp).astype(jnp.float32).reshape(T, N_GROUPS, EXPERTS_PER_GROUP)
    le = jnp.take_along_axis(le, grp[:, None, None], axis=1)[:, 0]
    top_v, top_i = lax.top_k(le, TOP_K)
    wts = (jax.nn.softmax(top_v, axis=-1) * g_w).reshape(-1)
    eid = (grp[:, None] * EXPERTS_PER_GROUP + top_i).reshape(-1)
    n = T * TOP_K
    order = jnp.argsort(eid)
    e_s = eid[order]
    tok_s = order // TOP_K
    w_s = wts[order]
    counts = jnp.bincount(eid, length=N_EXPERTS)
    padded = (counts + MOE_BLOCK - 1) // MOE_BLOCK * MOE_BLOCK
    pad_end = jnp.cumsum(padded)
    pad_start = pad_end - padded
    start = jnp.cumsum(counts) - counts
    dest = pad_start[e_s] + jnp.arange(n) - start[e_s]
    n_blocks = -(-n // MOE_BLOCK) + N_EXPERTS
    xbuf = jnp.zeros((n_blocks * MOE_BLOCK, D), x.dtype).at[dest].set(x[tok_s])
    blk_e = jnp.minimum(jnp.searchsorted(pad_end, jnp.arange(n_blocks) * MOE_BLOCK, side='right'), N_EXPERTS - 1)

    def expert(args):
        xb, e = args
        gu = xb @ w_gu[e]
        return (jax.nn.silu(gu[:, :D_EXPERT]) * gu[:, D_EXPERT:]) @ w_dn[e]

    ybuf = lax.map(expert, (xbuf.reshape(n_blocks, MOE_BLOCK, D), blk_e)).reshape(-1, D)
    return jax.ops.segment_sum(ybuf[dest] * w_s[:, None].astype(x.dtype), tok_s, num_segments=T)


def kernel(x_prompt, x_sample, cache_nsa_kv, state_win_kv, page_table, rel_bias, ln_mix, ln_ffn,
           nsa_w_in, nsa_w_out, nsa_q_gain, nsa_k_gain, cmp_pe, cmp_w1, cmp_w2,
           gm_w_in, gm_b_in, gm_ln_g, gm_ln_b, gm_w_s, gm_b_s, gm_w_out,
           moe_w_grp, moe_b_grp, moe_w_exp, moe_b_exp, moe_w_gu, moe_w_dn):
    xp = x_prompt
    xs = x_sample
    kv_p, kv_s, win_p, win_s, gv_p, gv_s = [], [], [], [], [], []
    for i in range(DEPTH):
        a = i // N_MIXERS
        if i % N_MIXERS == 0:
            nsa = (ln_mix[i], nsa_w_in[a], nsa_w_out[a], nsa_q_gain[a], nsa_k_gain[a],
                   cmp_pe[a], cmp_w1[a], cmp_w2[a], rel_bias)
            xp, rp, wp = _nsa_prompt_layer(xp, *nsa)
            cmp_rows, sel_kvt = _page_gather(cache_nsa_kv, page_table, a)
            xs, rs, ws = _nsa_decode_layer(xs, cmp_rows, sel_kvt, state_win_kv[a], *nsa)
            kv_p.append(rp)
            kv_s.append(rs)
            win_p.append(wp)
            win_s.append(ws)
        else:
            gm = (ln_mix[i], gm_w_in[a], gm_b_in[a], gm_ln_g[a], gm_ln_b[a], gm_w_s[a], gm_b_s[a], gm_w_out[a])
            bp, sp = xp.shape[:2]
            bs_, ss = xs.shape[:2]
            xp2, vp = _gmlp_residual(xp.reshape(-1, D_MODEL), *gm, seq=sp)
            xs2, vs = _gmlp_residual(xs.reshape(-1, D_MODEL), *gm, seq=ss)
            xp = xp2.reshape(xp.shape)
            xs = xs2.reshape(xs.shape)
            start = ((sp - 1) // CHUNK) * CHUNK
            gv_p.append(vp[:, vp.shape[1] - (sp - start):])
            gv_s.append(vs.reshape(bs_, ss, GM_WIDTH))
        moe = (ln_ffn[i], moe_w_grp[i], moe_b_grp[i], moe_w_exp[i], moe_b_exp[i], moe_w_gu, moe_w_dn, i)
        xp = _hier_moe_residual(xp.reshape(-1, D_MODEL), *moe).reshape(xp.shape)
        xs = _hier_moe_residual(xs.reshape(-1, D_MODEL), *moe).reshape(xs.shape)
    new_kv_prompt = jnp.stack(kv_p, axis=2)
    new_kv_sample = jnp.stack(kv_s, axis=2)
    new_win_prompt = jnp.stack(win_p, axis=0)
    new_win_sample = jnp.stack(win_s, axis=0)
    new_gm_v_prompt = jnp.stack(gv_p, axis=0)
    new_gm_v_sample = jnp.stack(gv_s, axis=0)
    return (xp, xs, new_kv_prompt, new_kv_sample, new_win_prompt, new_win_sample, new_gm_v_prompt, new_gm_v_sample)
```

```python
import functools
import math

import jax
import jax.numpy as jnp
import numpy as np
from jax import lax
from jax.experimental import pallas as pl
from jax.experimental.pallas import tpu as pltpu

D_MODEL = 1024
PAGE_SIZE = 128
DEPTH = 2
N_MIXERS = 2
N_HEADS = 16
HEAD_DIM = 64
KV_HEADS = 4
Q_PER_KV = N_HEADS // KV_HEADS
CMP_LEN = 32
CMP_STRIDE = 16
CMP_HID = 64
SEL_BLOCK = 64
CMP_PER_SEL = SEL_BLOCK // CMP_STRIDE
N_SEL = 16
N_LOCAL = 2
WINDOW = 512
N_BUCKETS = 32
MAX_DISTANCE = 128
CHUNK = 128
GM_WIDTH = 2048
GM_GROUPS = 8
GM_GROUP_DIM = GM_WIDTH // GM_GROUPS
N_GROUPS = 4
EXPERTS_PER_GROUP = 8
N_EXPERTS = N_GROUPS * EXPERTS_PER_GROUP
TOP_K = 2
D_EXPERT = 512
EPS = 1e-6

LANES = 128
VMEM_LIMIT_BYTES = 56 * 1024 * 1024


NEG = -1e30
MXU_DTYPE = jnp.bfloat16
TQ = 256
TK = 256
GROUP_CHUNKS = 3
NEAR_CHUNKS = 5
BIAS_SPAN = 128
LOG2E = 1.4426950408889634


def _bucket_of_distance():
    d = np.arange(BIAS_SPAN)
    max_exact = N_BUCKETS // 2
    nf = np.maximum(d, 1).astype(np.float32)
    large = max_exact + (np.log(nf / np.float32(max_exact)) / np.float32(math.log(MAX_DISTANCE / max_exact))
                         * np.float32(N_BUCKETS - max_exact)).astype(np.int32)
    large = np.minimum(large, N_BUCKETS - 1)
    return np.where(d < max_exact, d, large).astype(np.int32)


def _attn_bias_tiles(rel_bias, seq):
    fd = rel_bias[_bucket_of_distance()].T
    far = fd[:, BIAS_SPAN - 1]
    fd = fd - far[:, None]

    def by_distance(d):
        return jnp.where(d >= 0, LOG2E * jnp.take(fd, np.clip(d, 0, BIAS_SPAN - 1), axis=1), NEG)

    def toeplitz(first_distance):
        c = np.arange(2 * TK)
        g = by_distance(first_distance - np.where(c < TK, c, c - 2 * TK))
        flat = jnp.tile(g, (1, TQ))[:, :TQ * (2 * TK - 1)]
        return flat.reshape(-1, TQ, 2 * TK - 1)[:, :, :TK]

    b0 = toeplitz(0)
    b1 = toeplitz(TQ)
    i = np.arange(TQ)[:, None]
    j = np.arange(TK)[None, :]
    w2 = np.where(2 * TQ + i - j <= WINDOW, 0.0, NEG).astype(np.float32)
    nc = seq // CMP_STRIDE
    span = -(-(BIAS_SPAN + TQ) // CMP_STRIDE)
    r = np.arange(nc - span, nc + span)[:, None]
    dc = np.arange(TQ)[None, :] - CMP_STRIDE * (r - nc) - (CMP_LEN - 1)
    n_heads = fd.shape[0]
    pat = jnp.concatenate([jnp.zeros((n_heads, nc - span, TQ), jnp.float32), by_distance(dc),
                           jnp.full((n_heads, nc - span, TQ), NEG, jnp.float32)], axis=1)
    return b0, b1, jnp.asarray(w2), pat


def _nsa_attn_kernel(q_ref, kc_ref, vct_ref, kat_ref, vsa_ref, kwt_ref, vwa_ref, g_ref,
                     b0_ref, b1_ref, w2_ref, pat_ref, o_ref, m_sc, acc_sc, imp_sc, qa_sc, p_sc, a_sc, *, nc, nb):
    f32 = jnp.float32
    cd = MXU_DTYPE
    qb = pl.program_id(2)

    def compress_and_select(ncu, all_forced):
        nbu = ncu // CMP_PER_SEL
        r0 = pl.multiple_of(nc - (TQ // CMP_STRIDE) * qb, TQ // CMP_STRIDE)
        t_row = qb * TQ + lax.broadcasted_iota(jnp.int32, (1, TQ), 1)
        has_cmp = t_row >= CMP_LEN - 1
        for h in range(Q_PER_KV):
            st = lax.dot_general(kc_ref[0, 0, 0:ncu, :], q_ref[0, h], (((1,), (1,)), ((), ())),
                                 preferred_element_type=f32)
            st = st + pat_ref[h, pl.ds(r0, ncu), :]
            e = jnp.exp2(st - jnp.max(st, axis=0, keepdims=True))
            inv = jnp.where(has_cmp, 1.0 / jnp.sum(e, axis=0, keepdims=True), 0.0)
            pt = e * inv
            for c in range(TQ // LANES):
                part = pt[:, c * LANES:(c + 1) * LANES]
                if h == 0:
                    imp_sc[c, 0:ncu, :] = part
                else:
                    imp_sc[c, 0:ncu, :] += part
            oct_h = jnp.dot(vct_ref[0, 0, :, 0:ncu], pt.astype(cd), preferred_element_type=f32)
            o_ref[0, :, h * HEAD_DIM:(h + 1) * HEAD_DIM] = g_ref[0][:, 3 * h:3 * h + 1] * oct_h.T

        imp = jnp.concatenate(
            [sum(imp_sc[c, pl.ds(r, nbu, stride=CMP_PER_SEL), :] for r in range(CMP_PER_SEL))
             for c in range(TQ // LANES)], axis=1)
        blk = lax.broadcasted_iota(jnp.int32, (nbu, TQ), 0)
        t_blk = (qb * TQ + lax.broadcasted_iota(jnp.int32, (nbu, TQ), 1)) // SEL_BLOCK
        dd = t_blk - blk
        forced = (blk == 0) | ((dd >= 0) & (dd < N_LOCAL))
        if all_forced:
            score = jnp.where(forced | (dd < 0), -jnp.inf, imp)
            pen_t = jnp.where(forced, 0.0, NEG)
            rounds = N_SEL - (N_LOCAL + 1)
        else:
            score = jnp.where(forced, jnp.inf, jnp.where(dd >= 0, imp, -jnp.inf))
            pen_t = jnp.full((nbu, TQ), NEG, f32)
            rounds = min(N_SEL, nbu)
        for _ in range(rounds):
            best = jnp.max(score, axis=0, keepdims=True)
            first = jnp.min(jnp.where(score == best, blk, nbu), axis=0, keepdims=True)
            hit = blk == first
            pen_t = jnp.where(hit, 0.0, pen_t)
            score = jnp.where(hit, -jnp.inf, score)
        if nbu < nb:
            pen_t = jnp.concatenate([pen_t, jnp.full((nb - nbu, TQ), NEG, f32)], axis=0)
        pen = pen_t.T.astype(cd)
        pad = jnp.zeros((TQ, qa_sc.shape[2] - nb - HEAD_DIM), cd)
        for h in range(Q_PER_KV):
            qa_sc[h] = jnp.concatenate([pen, q_ref[0, h], pad], axis=1)

    n_tiles = nc * CMP_STRIDE // TQ
    n_var = 4 if (n_tiles % 4 == 0 and (nc // 4) % (2 * CMP_STRIDE) == 0) else 1
    for v in range(n_var):
        lo, hi = v * n_tiles // n_var, (v + 1) * n_tiles // n_var
        ncu = (v + 1) * nc // n_var
        if v == 0:
            @pl.when(qb == 0)
            def _():
                compress_and_select(ncu, all_forced=False)
            lo = 1

        @pl.when((qb >= lo) & (qb < hi))
        def _():
            compress_and_select(ncu, all_forced=TQ >= N_LOCAL * SEL_BLOCK)


    m_sc[...] = jnp.full(m_sc.shape, NEG, f32)
    acc_sc[...] = jnp.zeros(acc_sc.shape, f32)

    def run(*work):
        jobs = [(br, c, bias) for br, chunks in work for c, bias in chunks]
        for slot, ((b, lhs, k_ref, _), c, bias) in enumerate(jobs):
            for h in range(Q_PER_KV):
                s = jnp.dot(lhs(h), k_ref[0, 0, c], preferred_element_type=f32)
                if bias is not None:
                    s = s + bias(h)
                m_prev = m_sc[b, h]
                m_new = jnp.maximum(m_prev, jnp.max(s, axis=1, keepdims=True))
                p_sc[slot, h] = jnp.exp2(s - jnp.concatenate([m_new] * (TK // LANES), axis=1)).astype(cd)
                a_sc[slot, h] = jnp.exp2(m_prev - m_new)
                m_sc[b, h] = m_new
        for slot, ((b, _, _, v_ref), c, _) in enumerate(jobs):
            for h in range(Q_PER_KV):
                acc_sc[b, h] = a_sc[slot, h] * acc_sc[b, h] + jnp.dot(p_sc[slot, h], v_ref[0, 0, c],
                                                                      preferred_element_type=f32)

    def emit(b, gate_col):
        for h in range(Q_PER_KV):
            acc = acc_sc[b, h]
            w = g_ref[0][:, 3 * h + gate_col:3 * h + gate_col + 1] / acc[:, HEAD_DIM:HEAD_DIM + 1]
            o_ref[0, :, h * HEAD_DIM:(h + 1) * HEAD_DIM] += w * acc[:, :HEAD_DIM]

    own = lambda h: b0_ref[h]
    prev = lambda h: b1_ref[h]
    edge = lambda h: w2_ref[...]

    sel = (0, lambda h: qa_sc[h], kat_ref, vsa_ref)
    win = (1, lambda h: q_ref[0, h], kwt_ref, vwa_ref)
    n_far = jnp.maximum(qb - 1, 0)

    def far_group(i, carry):
        run((sel, [(GROUP_CHUNKS * i + j, None) for j in range(GROUP_CHUNKS)]))
        return carry

    lax.fori_loop(0, n_far // GROUP_CHUNKS, far_group, 0)
    for left in range(1, GROUP_CHUNKS):
        @pl.when(n_far % GROUP_CHUNKS == left)
        def _():
            run((sel, [(n_far - left + j, None) for j in range(left)]))

    @pl.when(qb >= 2)
    def _():
        run((sel, [(qb - 1, prev), (qb, own)]), (win, [(qb - 2, edge), (qb - 1, prev), (qb, own)]))

    @pl.when(qb == 1)
    def _():
        run((sel, [(qb - 1, prev), (qb, own)]), (win, [(qb - 1, prev), (qb, own)]))

    @pl.when(qb == 0)
    def _():
        run((sel, [(qb, own)]), (win, [(qb, own)]))

    emit(0, 1)
    emit(1, 2)


def _nsa_attention(qs, kcp, vct, kat, vsa, kwt, vwa, gt, rel_bias):
    B, _, S, _ = qs.shape
    assert S % TQ == 0 and TQ == TK and WINDOW == 2 * TQ and S // SEL_BLOCK >= N_SEL
    nc = S // CMP_STRIDE
    nb = S // SEL_BLOCK
    nch = S // TK
    ka = kat.shape[3]
    b0, b1, w2, pat = _attn_bias_tiles(rel_bias, S)
    G = Q_PER_KV
    kv_chunks = pl.BlockSpec((1, 1, nch, HEAD_DIM, TK), lambda b, k, i: (b, k, 0, 0, 0))
    ka_chunks = pl.BlockSpec((1, 1, nch, ka, TK), lambda b, k, i: (b, k, 0, 0, 0))
    v_chunks = pl.BlockSpec((1, 1, nch, TK, LANES), lambda b, k, i: (b, k, 0, 0, 0))
    return pl.pallas_call(
        functools.partial(_nsa_attn_kernel, nc=nc, nb=nb),
        grid=(B, KV_HEADS, S // TQ),
        in_specs=[
            pl.BlockSpec((1, G, TQ, HEAD_DIM), lambda b, k, i: (b, k, i, 0)),
            pl.BlockSpec((1, 1, nc, HEAD_DIM), lambda b, k, i: (b, k, 0, 0)),
            pl.BlockSpec((1, 1, HEAD_DIM, nc), lambda b, k, i: (b, k, 0, 0)),
            ka_chunks, v_chunks, kv_chunks, v_chunks,
            pl.BlockSpec((1, TQ, LANES), lambda b, k, i: (b, i, k)),
            pl.BlockSpec((G, TQ, TK), lambda b, k, i: (k, 0, 0)),
            pl.BlockSpec((G, TQ, TK), lambda b, k, i: (k, 0, 0)),
            pl.BlockSpec((TQ, TK), lambda b, k, i: (0, 0)),
            pl.BlockSpec((G, 2 * nc, TQ), lambda b, k, i: (k, 0, 0)),
        ],
        out_specs=pl.BlockSpec((1, TQ, G * HEAD_DIM), lambda b, k, i: (b, i, k)),
        out_shape=jax.ShapeDtypeStruct((B, S, N_HEADS * HEAD_DIM), jnp.float32),
        scratch_shapes=[
            pltpu.VMEM((2, G, TQ, LANES), jnp.float32),
            pltpu.VMEM((2, G, TQ, LANES), jnp.float32),
            pltpu.VMEM((TQ // LANES, nc, LANES), jnp.float32),
            pltpu.VMEM((G, TQ, ka), MXU_DTYPE),
            pltpu.VMEM((NEAR_CHUNKS, G, TQ, TK), MXU_DTYPE),
            pltpu.VMEM((NEAR_CHUNKS, G, TQ, LANES), jnp.float32),
        ],
        compiler_params=pltpu.CompilerParams(
            dimension_semantics=("arbitrary", "arbitrary", "arbitrary"),
            vmem_limit_bytes=VMEM_LIMIT_BYTES),
        name="nsa_attn",
    )(qs, kcp, vct, kat, vsa, kwt, vwa, gt, b0, b1, w2, pat)


def _nsa_prompt_layer(x, ln_g, w_in, w_out, q_gain, k_gain, pe, w1, w2, rel_bias):
    B, S, D = x.shape
    x2 = x.reshape(B * S, D)
    qs, rows, win, kat, vsa, kwt, vwa, c, gates = _nsa_project(x2, ln_g, w_in, q_gain, k_gain, B, S)
    nc = S // CMP_STRIDE
    cmp = _compress(c.reshape(2, B, nc, CMP_STRIDE * SLOT), pe, w1, w2, k_gain[0])
    cmp = cmp.reshape(2, B, nc, KV_HEADS, HEAD_DIM).astype(MXU_DTYPE)
    gt = gates.reshape(B, S, KV_HEADS * LANES)
    o = _nsa_attention(qs, cmp[0].transpose(0, 2, 1, 3), cmp[1].transpose(0, 2, 3, 1), kat, vsa, kwt, vwa, gt,
                       rel_bias)
    y = _proj_residual(x2, o.reshape(B * S, HQ), w_out)
    return (y.reshape(B, S, D), rows.reshape(B, 4, KV_HEADS, HEAD_DIM, S).transpose(0, 4, 1, 2, 3),
            win.reshape(B, WINDOW, 2, KV_HEADS, HEAD_DIM))


PAGES_PER_STEP = 8


def _page_gather_kernel(pt_ref, *refs):
    del pt_ref
    pages = refs[:PAGES_PER_STEP]
    c_ref, kvt_ref, rows_sc = refs[PAGES_PER_STEP:]
    for p, pg in enumerate(pages):
        pos = slice(p * PAGE_SIZE, (p + 1) * PAGE_SIZE)
        for slot in range(2):
            for pair in range(KV_HEADS // 2):
                tile = pg[0, 0, slot, 2 * pair:2 * pair + 2].reshape(LANES, PAGE_SIZE)
                rows_sc[slot * 2 + pair, pos, :] = tile.T
        for slot in range(2):
            kvt_ref[slot, 0, :, :, pos] = pg[0, 0, 2 + slot].astype(MXU_DTYPE)
    n_chunks = PAGES_PER_STEP * PAGE_SIZE // CMP_STRIDE
    for l in range(CMP_STRIDE):
        for t in range(4):
            lo = l * SLOT + (t % 2) * LANES
            c_ref[t // 2, 0, :, lo:lo + LANES] = rows_sc[t, pl.ds(l, n_chunks, stride=CMP_STRIDE), :].astype(MXU_DTYPE)


def _page_gather(cache, page_table, layer):
    nseq, n_pages = page_table.shape
    assert n_pages % PAGES_PER_STEP == 0 and 2 * LANES == SLOT
    P = n_pages * PAGE_SIZE
    cache_t = cache.transpose(0, 2, 3, 4, 5, 1)
    page_specs = [
        pl.BlockSpec((1, 1, 4, KV_HEADS, HEAD_DIM, PAGE_SIZE),
                     functools.partial(lambda r, n, j, pt: (pt[n, PAGES_PER_STEP * j + r], layer, 0, 0, 0, 0), r))
        for r in range(PAGES_PER_STEP)]
    rows_per_step = PAGES_PER_STEP * PAGE_SIZE
    return pl.pallas_call(
        _page_gather_kernel,
        grid_spec=pltpu.PrefetchScalarGridSpec(
            num_scalar_prefetch=1,
            grid=(nseq, n_pages // PAGES_PER_STEP),
            in_specs=page_specs,
            out_specs=[pl.BlockSpec((2, 1, rows_per_step // CMP_STRIDE, CMP_STRIDE * SLOT),
                                    lambda n, j, pt: (0, n, j, 0)),
                       pl.BlockSpec((2, 1, KV_HEADS, HEAD_DIM, rows_per_step), lambda n, j, pt: (0, n, 0, 0, j))],
            scratch_shapes=[pltpu.VMEM((4, rows_per_step, LANES), jnp.float32)]),
        out_shape=[jax.ShapeDtypeStruct((2, nseq, P // CMP_STRIDE, CMP_STRIDE * SLOT), MXU_DTYPE),
                   jax.ShapeDtypeStruct((2, nseq, KV_HEADS, HEAD_DIM, P), MXU_DTYPE)],
        compiler_params=pltpu.CompilerParams(dimension_semantics=("arbitrary", "arbitrary")),
        name="page_gather",
    )(page_table, *([cache_t] * PAGES_PER_STEP))


def _nsa_decode_kernel(q_ref, qt_ref, kc_ref, vc_ref, kst_ref, vst_ref, win_ref, rows_ref, wnew_ref, g_ref,
                       oh_ref, bc_ref, bs_ref, bw_ref, bn_ref, o_ref, pen_sc, s_sc, *, nb):
    f32 = jnp.float32
    cd = MXU_DTYPE
    nt = (((1,), (1,)), ((), ()))
    pen_sc[...] = jnp.zeros(pen_sc.shape, f32)
    rows = rows_ref[0]
    wnew = wnew_ref[0]

    def softmax_with_new(s, s_new):
        m = jnp.maximum(jnp.max(s, axis=1, keepdims=True), s_new)
        e = jnp.exp(s - m)
        e_new = jnp.exp(s_new - m)
        return e, e_new, 1.0 / (jnp.sum(e, axis=1, keepdims=True) + e_new)

    def bf(x):
        return x.astype(cd).astype(f32)

    for k in range(KV_HEADS):
        hs = slice(k * Q_PER_KV, (k + 1) * Q_PER_KV)
        q = q_ref[0, hs, :]
        gates = g_ref[0, hs, :]
        sc = lax.dot_general(q, kc_ref[0, k], nt, preferred_element_type=f32) + bc_ref[hs, :]
        ec = jnp.exp(sc - jnp.max(sc, axis=1, keepdims=True))
        pc = ec / jnp.sum(ec, axis=1, keepdims=True)
        o_ref[0, hs, :] = gates[:, 0:1] * jnp.dot(pc.astype(cd), vc_ref[0, k], preferred_element_type=f32)

        imp = sum(pc[:, r * nb:(r + 1) * nb] for r in range(CMP_PER_SEL))
        imp = jnp.sum(imp, axis=0, keepdims=True)
        blk = lax.broadcasted_iota(jnp.int32, imp.shape, 1)
        forced = (blk == 0) | (blk == nb - 1)
        score = jnp.where(forced, jnp.inf, imp)
        pen = jnp.full(imp.shape, NEG, f32)
        for _ in range(N_SEL - 1):
            best = jnp.max(score, axis=1, keepdims=True)
            first = jnp.min(jnp.where(score == best, blk, nb), axis=1, keepdims=True)
            hit = blk == first
            pen = jnp.where(hit, 0.0, pen)
            score = jnp.where(hit, -jnp.inf, score)
        pen_sc[k:k + 1, :] = pen

    key_pen = jnp.dot(pen_sc[...].astype(cd), oh_ref[...], preferred_element_type=f32)

    lane = lax.broadcasted_iota(jnp.int32, (HEAD_DIM, LANES), 1)
    for k in range(KV_HEADS):
        hs = slice(k * Q_PER_KV, (k + 1) * Q_PER_KV)
        lanes = slice(k * HEAD_DIM, (k + 1) * HEAD_DIM)
        q = q_ref[0, hs, :]
        qf = q.astype(f32)
        gates = g_ref[0, hs, :]

        def new_key(row):
            return jnp.sum(qf * bf(row), axis=1, keepdims=True) + bn_ref[hs, :]

        kt = kst_ref[0, 0, k].astype(f32)
        for g in range(Q_PER_KV):
            qcol = qt_ref[0][:, k * Q_PER_KV + g:k * Q_PER_KV + g + 1].astype(f32)
            s_sc[g:g + 1, :] = jnp.sum(kt * qcol, axis=0, keepdims=True)
        ss = s_sc[0:Q_PER_KV, :] + key_pen[k:k + 1, :] + bs_ref[hs, :]
        e, e_new, inv = softmax_with_new(ss, new_key(rows[:, 2 * SLOT:3 * SLOT][:, lanes]))
        vt = vst_ref[0, 0, k].astype(f32)
        eb = bf(e)
        acc_t = jnp.zeros((HEAD_DIM, LANES), f32)
        for g in range(Q_PER_KV):
            acc_t = jnp.where(lane == g, jnp.sum(vt * eb[g:g + 1, :], axis=1, keepdims=True), acc_t)
        acc = acc_t.T[0:Q_PER_KV, :] + bf(e_new) * bf(rows[:, 3 * SLOT:4 * SLOT][:, lanes])
        out = o_ref[0, hs, :] + gates[:, 1:2] * inv * acc

        sw = lax.dot_general(q, win_ref[0, 0, k], nt, preferred_element_type=f32) + bw_ref[hs, :]
        e, e_new, inv = softmax_with_new(sw, new_key(wnew[:, :SLOT][:, lanes]))
        acc = (jnp.dot(e.astype(cd), win_ref[0, 1, k], preferred_element_type=f32)
               + bf(e_new) * bf(wnew[:, SLOT:][:, lanes]))
        o_ref[0, hs, :] = out + gates[:, 2:3] * inv * acc


def _nsa_decode_layer(x, cmp_rows, sel_kvt, win_buf, ln_g, w_in, w_out, q_gain, k_gain, pe, w1, w2, rel_bias):
    N, L, D = x.shape
    P = sel_kvt.shape[-1]
    wb = win_buf.shape[1]
    assert L == 1 and P % SEL_BLOCK == 0 and wb == WINDOW and P >= BIAS_SPAN and P // SEL_BLOCK >= N_SEL - 1
    f32 = jnp.float32
    cd = MXU_DTYPE
    nc = P // CMP_STRIDE
    nb = P // SEL_BLOCK
    x2 = x.reshape(N, D)
    q, rows, wnew, gates = _nsa_project(x2, ln_g, w_in, q_gain, k_gain, N, 1)
    cmp = _compress(cmp_rows, pe, w1, w2, k_gain[0])
    cmp = cmp.reshape(2, N, nb, CMP_PER_SEL, KV_HEADS, HEAD_DIM).transpose(0, 1, 4, 3, 2, 5)
    cmp = cmp.reshape(2, N, KV_HEADS, nc, HEAD_DIM).astype(cd)
    qs = (q.reshape(N, N_HEADS, HEAD_DIM) * HEAD_DIM ** -0.5).astype(cd)
    win = win_buf.transpose(0, 2, 3, 1, 4).astype(cd)
    onehot = jnp.asarray(np.arange(nb)[:, None] == (np.arange(P) // SEL_BLOCK)[None, :], dtype=cd)
    fd = rel_bias[_bucket_of_distance()].T
    fd = fd - fd[:, BIAS_SPAN - 1:]
    cidx = (np.arange(nc) % nb) * CMP_PER_SEL + np.arange(nc) // nb
    dc = P - (cidx * CMP_STRIDE + CMP_LEN - 1)
    b_cmp = jnp.where(dc >= 0, jnp.take(fd, np.clip(dc, 0, BIAS_SPAN - 1), axis=1), NEG)
    b_sel = jnp.take(fd, np.clip(P - np.arange(P), 0, BIAS_SPAN - 1), axis=1)
    b_win = jnp.take(fd, np.clip(wb - np.arange(wb), 0, BIAS_SPAN - 1), axis=1)
    b_new = fd[:, 0:1]
    seq3 = lambda n: (n, 0, 0)
    seq4 = lambda n: (n, 0, 0, 0)
    fixed = lambda n: (0, 0)
    o = pl.pallas_call(
        functools.partial(_nsa_decode_kernel, nb=nb),
        grid=(N,),
        in_specs=[pl.BlockSpec((1, N_HEADS, HEAD_DIM), seq3),
                  pl.BlockSpec((1, HEAD_DIM, N_HEADS), seq3),
                  pl.BlockSpec((1, KV_HEADS, nc, HEAD_DIM), seq4),
                  pl.BlockSpec((1, KV_HEADS, nc, HEAD_DIM), seq4),
                  pl.BlockSpec((1, 1, KV_HEADS, HEAD_DIM, P), lambda n: (0, n, 0, 0, 0)),
                  pl.BlockSpec((1, 1, KV_HEADS, HEAD_DIM, P), lambda n: (1, n, 0, 0, 0)),
                  pl.BlockSpec((1, 2, KV_HEADS, wb, HEAD_DIM), lambda n: (n, 0, 0, 0, 0)),
                  pl.BlockSpec((1, 1, 4 * SLOT), seq3),
                  pl.BlockSpec((1, 1, 2 * SLOT), seq3),
                  pl.BlockSpec((1, N_HEADS, 3), seq3),
                  pl.BlockSpec((nb, P), fixed), pl.BlockSpec((N_HEADS, nc), fixed),
                  pl.BlockSpec((N_HEADS, P), fixed), pl.BlockSpec((N_HEADS, wb), fixed),
                  pl.BlockSpec((N_HEADS, 1), fixed)],
        out_specs=pl.BlockSpec((1, N_HEADS, HEAD_DIM), seq3),
        out_shape=jax.ShapeDtypeStruct((N, N_HEADS, HEAD_DIM), f32),
        scratch_shapes=[pltpu.VMEM((8, nb), f32), pltpu.VMEM((8, P), f32)],
        compiler_params=pltpu.CompilerParams(dimension_semantics=("arbitrary",),
                                             vmem_limit_bytes=VMEM_LIMIT_BYTES),
        name="nsa_decode_attn",
    )(qs, qs.transpose(0, 2, 1), cmp[0], cmp[1], sel_kvt, sel_kvt, win, rows.reshape(N, 1, 4 * SLOT), wnew.reshape(N, 1, 2 * SLOT),
      gates[:, :3 * N_HEADS].reshape(N, N_HEADS, 3), onehot, b_cmp, b_sel, b_win, b_new)
    y = _proj_residual(x2, o.reshape(N, HQ), w_out)
    new_win = jnp.concatenate([win_buf, wnew.reshape(N, 1, 2, KV_HEADS, HEAD_DIM)], axis=1)[:, 1:]
    return y.reshape(N, 1, D), rows.reshape(N, 1, 4, KV_HEADS, HEAD_DIM), new_win


HQ = N_HEADS * HEAD_DIM
SLOT = KV_HEADS * HEAD_DIM
NSA_COLS = HQ + 6 * SLOT + 3 * N_HEADS
NSA_COLS_PAD = -(-NSA_COLS // LANES) * LANES
GATE_COLS = NSA_COLS_PAD - HQ - 6 * SLOT


def _nsa_proj_kernel(x_ref, g_ref, w_ref, qg_ref, kg_ref, *outs, prompt, n_blocks):
    f32 = jnp.float32
    x = x_ref[...]
    xn = x * lax.rsqrt(jnp.mean(x * x, axis=-1, keepdims=True) + EPS) * g_ref[...]
    h = jnp.dot(xn.astype(MXU_DTYPE), w_ref[...], preferred_element_type=f32)

    def head_norm(v, gain):
        return v * lax.rsqrt(jnp.mean(v * v, axis=-1, keepdims=True) + EPS) * gain

    def slot(s, kv):
        lo = HQ + s * SLOT + kv * HEAD_DIM
        return h[:, lo:lo + HEAD_DIM]

    gates = 1.0 / (1.0 + jnp.exp(-h[:, HQ + 6 * SLOT:]))
    if prompt:
        q_ref, rows_ref, win_ref, kat_ref, vsa_ref, kwt_ref, vwa_ref, c_ref, gate_ref = outs
        for hh in range(N_HEADS):
            qn = head_norm(h[:, hh * HEAD_DIM:(hh + 1) * HEAD_DIM], qg_ref[...])
            q_ref[0, hh] = (qn * (HEAD_DIM ** -0.5 * LOG2E)).astype(MXU_DTYPE)
        c_ref[0, 0] = h[:, HQ:HQ + SLOT].astype(MXU_DTYPE)
        c_ref[1, 0] = h[:, HQ + SLOT:HQ + 2 * SLOT].astype(MXU_DTYPE)
        rows, ka = x.shape[0], kat_ref.shape[3]
        nb = n_blocks
        first_key = (pl.program_id(0) % (nb * SEL_BLOCK // rows)) * rows
        key_blk = (first_key + lax.broadcasted_iota(jnp.int32, (nb, rows), 1)) // SEL_BLOCK
        onehot = jnp.where(key_blk == lax.broadcasted_iota(jnp.int32, (nb, rows), 0), 1.0, 0.0).astype(MXU_DTYPE)
        kpad = jnp.zeros((ka - nb - HEAD_DIM, rows), MXU_DTYPE)
        ones = jnp.ones((rows, LANES - HEAD_DIM), MXU_DTYPE)
    else:
        q_ref, rows_ref, win_ref, gate_ref = outs
        for hh in range(N_HEADS):
            q_ref[:, hh * HEAD_DIM:(hh + 1) * HEAD_DIM] = head_norm(h[:, hh * HEAD_DIM:(hh + 1) * HEAD_DIM],
                                                                    qg_ref[...])
    if prompt:
        rows_ref[0, 0:2 * SLOT, :] = h[:, HQ:HQ + 2 * SLOT].T
        rows_ref[0, 3 * SLOT:4 * SLOT, :] = h[:, HQ + 3 * SLOT:HQ + 4 * SLOT].T
    else:
        rows_ref[:, 0:2 * SLOT] = h[:, HQ:HQ + 2 * SLOT]
        rows_ref[:, 3 * SLOT:4 * SLOT] = h[:, HQ + 3 * SLOT:HQ + 4 * SLOT]
    for kv in range(KV_HEADS):
        lanes = slice(kv * HEAD_DIM, (kv + 1) * HEAD_DIM)
        ksn = head_norm(slot(2, kv), kg_ref[1:2, :])
        kwn = head_norm(slot(4, kv), kg_ref[2:3, :])
        if prompt:
            ksn_t = ksn.T
            rows_ref[0, 2 * SLOT + kv * HEAD_DIM:2 * SLOT + (kv + 1) * HEAD_DIM, :] = ksn_t
            win_ref[0, 0, :, lanes] = kwn
            kat_ref[0, kv, 0] = jnp.concatenate([onehot, ksn_t.astype(MXU_DTYPE), kpad], axis=0)
            kwt_ref[0, kv, 0] = kwn.T.astype(MXU_DTYPE)
            vsa_ref[0, kv, 0] = jnp.concatenate([slot(3, kv).astype(MXU_DTYPE), ones], axis=1)
            vwa_ref[0, kv, 0] = jnp.concatenate([slot(5, kv).astype(MXU_DTYPE), ones], axis=1)
        else:
            rows_ref[:, 2 * SLOT + kv * HEAD_DIM:2 * SLOT + (kv + 1) * HEAD_DIM] = ksn
            win_ref[:, lanes] = kwn
    if prompt:
        win_ref[0, 0, :, SLOT:2 * SLOT] = h[:, HQ + 5 * SLOT:HQ + 6 * SLOT]
    else:
        win_ref[:, SLOT:2 * SLOT] = h[:, HQ + 5 * SLOT:HQ + 6 * SLOT]
    if prompt:
        gate_ref[...] = jnp.zeros(gate_ref.shape, f32)
        per_kv = 3 * Q_PER_KV
        for kv in range(KV_HEADS):
            gate_ref[:, kv * LANES:kv * LANES + per_kv] = gates[:, kv * per_kv:(kv + 1) * per_kv]
    else:
        gate_ref[...] = gates


def _nsa_project(x, ln_g, w_in, q_gain, k_gain, batch, seq):
    T, D = x.shape
    f32 = jnp.float32
    cd = MXU_DTYPE
    prompt = seq > 1
    tb = TK if prompt else T
    w = jnp.pad(w_in, ((0, 0), (0, NSA_COLS_PAD - NSA_COLS))).astype(cd)
    fixed = lambda i: (0, 0)
    row = lambda i: (i, 0)
    in_specs = [pl.BlockSpec((tb, D), row), pl.BlockSpec((1, D), fixed), pl.BlockSpec((D, NSA_COLS_PAD), fixed),
                pl.BlockSpec((1, HEAD_DIM), fixed), pl.BlockSpec((3, HEAD_DIM), fixed)]
    if prompt:
        assert seq % tb == 0 and WINDOW == 2 * tb
        n = seq // tb
        n_blocks = seq // SEL_BLOCK
        ka = -(-(n_blocks + HEAD_DIM) // LANES) * LANES
        chunk = lambda i: (i // n, 0, i % n, 0, 0)
        kt_spec = pl.BlockSpec((1, KV_HEADS, 1, HEAD_DIM, tb), chunk)
        ka_spec = pl.BlockSpec((1, KV_HEADS, 1, ka, tb), chunk)
        v_spec = pl.BlockSpec((1, KV_HEADS, 1, tb, LANES), chunk)
        kt_shape = jax.ShapeDtypeStruct((batch, KV_HEADS, n, HEAD_DIM, tb), cd)
        ka_shape = jax.ShapeDtypeStruct((batch, KV_HEADS, n, ka, tb), cd)
        v_shape = jax.ShapeDtypeStruct((batch, KV_HEADS, n, tb, LANES), cd)
        out_specs = [
            pl.BlockSpec((1, N_HEADS, tb, HEAD_DIM), lambda i: (i // n, 0, i % n, 0)),
            pl.BlockSpec((1, 4 * SLOT, tb), lambda i: (i // n, 0, i % n)),
            pl.BlockSpec((1, 1, tb, 2 * SLOT), lambda i: (i // n, jnp.maximum(i % n - (n - 2), 0), 0, 0)),
            ka_spec, v_spec, kt_spec, v_spec,
            pl.BlockSpec((2, 1, tb, SLOT), lambda i: (0, i // n, i % n, 0)),
            pl.BlockSpec((tb, KV_HEADS * LANES), row)]
        out_shape = [
            jax.ShapeDtypeStruct((batch, N_HEADS, seq, HEAD_DIM), cd),
            jax.ShapeDtypeStruct((batch, 4 * SLOT, seq), f32),
            jax.ShapeDtypeStruct((batch, 2, tb, 2 * SLOT), f32),
            ka_shape, v_shape, kt_shape, v_shape,
            jax.ShapeDtypeStruct((2, batch, seq, SLOT), cd),
            jax.ShapeDtypeStruct((T, KV_HEADS * LANES), f32)]
    else:
        out_specs = [pl.BlockSpec((tb, HQ), row), pl.BlockSpec((tb, 4 * SLOT), row),
                     pl.BlockSpec((tb, 2 * SLOT), row), pl.BlockSpec((tb, GATE_COLS), row)]
        n_blocks = 0
        out_shape = [jax.ShapeDtypeStruct((T, HQ), f32), jax.ShapeDtypeStruct((T, 4 * SLOT), f32),
                     jax.ShapeDtypeStruct((T, 2 * SLOT), f32), jax.ShapeDtypeStruct((T, GATE_COLS), f32)]
    return pl.pallas_call(
        functools.partial(_nsa_proj_kernel, prompt=prompt, n_blocks=n_blocks),
        grid=(T // tb,),
        in_specs=in_specs, out_specs=out_specs, out_shape=out_shape,
        compiler_params=pltpu.CompilerParams(dimension_semantics=("arbitrary",),
                                             vmem_limit_bytes=VMEM_LIMIT_BYTES),
        name="nsa_proj",
    )(x, ln_g.reshape(1, D), w, q_gain.reshape(1, HEAD_DIM), k_gain)


def _compress_kernel(c_ref, w1_ref, pe_ref, w2_ref, kg_ref, o_ref):
    f32 = jnp.float32
    pairs = SLOT // LANES
    c = c_ref[0, 0]
    u = [jnp.dot(jnp.concatenate([c[:, l * SLOT + p * LANES:l * SLOT + (p + 1) * LANES]
                                  for l in range(CMP_STRIDE)], axis=1).astype(MXU_DTYPE),
                 w1_ref[0], preferred_element_type=f32) for p in range(pairs)]
    first = jnp.concatenate([up[:, :LANES] for up in u], axis=1)
    second = jnp.concatenate([up[:, LANES:] for up in u], axis=1)
    pe = jnp.dot(pe_ref[0], w1_ref[0], preferred_element_type=f32)
    pe = jnp.concatenate([pe[0:1, :LANES] + pe[1:2, LANES:]] * pairs, axis=1)
    nc = first.shape[0]
    hid = first + pltpu.roll(second, nc - 1, 0) + pe
    out = jnp.dot(_gelu_tanh(hid).astype(MXU_DTYPE), w2_ref[0], preferred_element_type=f32)
    is_key = pl.program_id(1) == 0
    for kv in range(KV_HEADS):
        lanes = slice(kv * HEAD_DIM, (kv + 1) * HEAD_DIM)
        v = out[:, lanes]
        vn = v * lax.rsqrt(jnp.mean(v * v, axis=-1, keepdims=True) + EPS) * kg_ref[...]
        o_ref[0, 0, :, lanes] = jnp.where(is_key, vn, v)


def _compress(c, pe, w1, w2, k_gain0):
    n, nc = c.shape[1:3]
    cd = MXU_DTYPE
    kc = CMP_STRIDE * SLOT
    eye = jnp.eye(KV_HEADS, dtype=jnp.float32)
    per_tile = LANES // HEAD_DIM
    kp = CMP_STRIDE * LANES
    w1h = w1.reshape(2, 2, CMP_STRIDE, HEAD_DIM, CMP_HID)
    w1b = jnp.einsum('shldf,kj->slkdhjf', w1h, eye[:per_tile, :per_tile]).reshape(2, kp, 2 * LANES).astype(cd)
    w2b = jnp.einsum('sdf,kj->skdjf', w2, eye).reshape(2, SLOT, SLOT).astype(cd)
    peh = jnp.broadcast_to(pe.reshape(2, 2, CMP_STRIDE, 1, HEAD_DIM), (2, 2, CMP_STRIDE, per_tile, HEAD_DIM))
    peh = jnp.pad(peh.reshape(2, 2, kp), ((0, 0), (0, 14), (0, 0))).astype(cd)
    return pl.pallas_call(
        _compress_kernel,
        grid=(n, 2),
        in_specs=[pl.BlockSpec((1, 1, nc, kc), lambda b, s: (s, b, 0, 0)),
                  pl.BlockSpec((1, kp, 2 * LANES), lambda b, s: (s, 0, 0)),
                  pl.BlockSpec((1, 16, kp), lambda b, s: (s, 0, 0)),
                  pl.BlockSpec((1, SLOT, SLOT), lambda b, s: (s, 0, 0)),
                  pl.BlockSpec((1, HEAD_DIM), lambda b, s: (0, 0))],
        out_specs=pl.BlockSpec((1, 1, nc, SLOT), lambda b, s: (s, b, 0, 0)),
        out_shape=jax.ShapeDtypeStruct((2, n, nc, SLOT), jnp.float32),
        compiler_params=pltpu.CompilerParams(dimension_semantics=("arbitrary", "arbitrary"),
                                             vmem_limit_bytes=VMEM_LIMIT_BYTES),
        name="nsa_compress",
    )(c, w1b, peh, w2b, k_gain0.reshape(1, HEAD_DIM))


def _proj_residual_kernel(x_ref, a_ref, w_ref, o_ref):
    o_ref[...] = x_ref[...] + jnp.dot(a_ref[...].astype(MXU_DTYPE), w_ref[...],
                                      preferred_element_type=jnp.float32)


def _proj_residual(x, a, w):
    T, D = x.shape
    k = a.shape[1]
    tb = min(512, T)
    assert T % tb == 0
    return pl.pallas_call(
        _proj_residual_kernel,
        grid=(T // tb,),
        in_specs=[pl.BlockSpec((tb, D), lambda i: (i, 0)), pl.BlockSpec((tb, k), lambda i: (i, 0)),
                  pl.BlockSpec((k, D), lambda i: (0, 0))],
        out_specs=pl.BlockSpec((tb, D), lambda i: (i, 0)),
        out_shape=jax.ShapeDtypeStruct((T, D), jnp.float32),
        compiler_params=pltpu.CompilerParams(dimension_semantics=("arbitrary",)),
        name="proj_residual",
    )(x, a, w.astype(MXU_DTYPE))


MOE_ROWS = 256
ROUTE_COLS = LANES
ROW_DMA_UNROLL = 8


def _moe_route_kernel(x_ref, g_ref, w_ref, b_ref, tri_ref, h_ref, meta_ref, wt_ref, cnt_ref, carry_sc):
    f32 = jnp.float32
    step = pl.program_id(0)

    @pl.when(step == 0)
    def _():
        carry_sc[...] = jnp.zeros(carry_sc.shape, f32)

    x = x_ref[...]
    h = x * lax.rsqrt(jnp.mean(x * x, axis=-1, keepdims=True) + EPS) * g_ref[...]
    h_ref[...] = h
    logits = jnp.dot(h.astype(MXU_DTYPE), w_ref[...], preferred_element_type=f32) + b_ref[...]
    tb = logits.shape[0]
    col = lax.broadcasted_iota(jnp.int32, (tb, ROUTE_COLS), 1)

    def first_max(vals):
        best = jnp.max(vals, axis=1, keepdims=True)
        return best, jnp.min(jnp.where(vals == best, col, ROUTE_COLS), axis=1, keepdims=True)

    lg = jnp.where(col < N_GROUPS, logits, -jnp.inf)
    g_best, grp = first_max(lg)
    g_w = 1.0 / jnp.sum(jnp.exp(lg - g_best), axis=1, keepdims=True)
    lo = N_GROUPS + EXPERTS_PER_GROUP * grp
    le = jnp.where((col >= lo) & (col < lo + EXPERTS_PER_GROUP), logits, -jnp.inf)
    v0, c0 = first_max(le)
    v1, c1 = first_max(jnp.where(col == c0, -jnp.inf, le))
    e1 = jnp.exp(v1 - v0)
    w0 = g_w / (1.0 + e1)
    w1 = g_w * e1 / (1.0 + e1)
    chosen = (col == c0) | (col == c1)
    before = jnp.dot(tri_ref[...], jnp.where(chosen, 1.0, 0.0).astype(MXU_DTYPE),
                     preferred_element_type=f32) + carry_sc[...]
    r0 = jnp.sum(jnp.where(col == c0, before, 0.0), axis=1, keepdims=True).astype(jnp.int32)
    r1 = jnp.sum(jnp.where(col == c1, before, 0.0), axis=1, keepdims=True).astype(jnp.int32)
    carry_sc[...] += jnp.sum(jnp.where(chosen, 1.0, 0.0), axis=0, keepdims=True)
    cnt_ref[...] = carry_sc[...]
    meta_ref[...] = jnp.where(col == 0, c0 - N_GROUPS, jnp.where(col == 1, c1 - N_GROUPS,
                              jnp.where(col == 2, r0, jnp.where(col == 3, r1, 0))))
    wt_ref[...] = jnp.where(col == 0, w0, jnp.where(col == 1, w1, 0.0))


def _moe_dispatch_kernel(dest_ref, h_ref, xbuf_in, xbuf_ref, sem):
    del xbuf_in
    tb = h_ref.shape[0]

    def row_copy(r, k):
        return pltpu.make_async_copy(h_ref.at[pl.ds(r, 1)], xbuf_ref.at[pl.ds(dest_ref[0, 0, 2 * r + k], 1)], sem)

    def start(r, c):
        row_copy(r, 0).start()
        row_copy(r, 1).start()
        return c

    def wait(r, c):
        row_copy(r, 0).wait()
        row_copy(r, 1).wait()
        return c

    lax.fori_loop(0, tb, start, 0, unroll=ROW_DMA_UNROLL)
    lax.fori_loop(0, tb, wait, 0, unroll=ROW_DMA_UNROLL)


def _moe_expert_kernel(blk_e_ref, nblk_ref, x_ref, wgu_ref, wdn_ref, y_ref, wgu_sc, wdn_sc):
    i = pl.program_id(0)
    f32 = jnp.float32

    @pl.when(i < nblk_ref[0])
    def _():
        changed = jnp.logical_or(i == 0, blk_e_ref[i] != blk_e_ref[jnp.maximum(i - 1, 0)])

        @pl.when(changed)
        def _():
            wgu_sc[...] = wgu_ref[0, 0].astype(MXU_DTYPE)
            wdn_sc[...] = wdn_ref[0, 0].astype(MXU_DTYPE)

        gu = jnp.dot(x_ref[...].astype(MXU_DTYPE), wgu_sc[...], preferred_element_type=f32)
        gate = gu[:, :D_EXPERT]
        act = gate * (1.0 / (1.0 + jnp.exp(-gate))) * gu[:, D_EXPERT:]
        y_ref[...] = jnp.dot(act.astype(MXU_DTYPE), wdn_sc[...], preferred_element_type=f32)

    @pl.when(i >= nblk_ref[0])
    def _():
        y_ref[...] = jnp.zeros(y_ref.shape, f32)


def _moe_combine_kernel(dest_ref, x_ref, wt_ref, ybuf_ref, o_ref, rows_sc, sem):
    tb = x_ref.shape[0]

    def row_copy(r, k):
        return pltpu.make_async_copy(ybuf_ref.at[pl.ds(dest_ref[0, 0, 2 * r + k], 1)],
                                     rows_sc.at[k, pl.ds(r, 1)], sem)

    def start(r, c):
        row_copy(r, 0).start()
        row_copy(r, 1).start()
        return c

    def wait(r, c):
        row_copy(r, 0).wait()
        row_copy(r, 1).wait()
        return c

    lax.fori_loop(0, tb, start, 0, unroll=ROW_DMA_UNROLL)
    lax.fori_loop(0, tb, wait, 0, unroll=ROW_DMA_UNROLL)
    wt = wt_ref[...]
    o_ref[...] = x_ref[...] + (wt[:, 0:1] * rows_sc[0] + wt[:, 1:2] * rows_sc[1])


def _hier_moe_residual(x, ln_g, w_grp, b_grp, w_exp, b_exp, w_gu, w_dn, layer):
    T, D = x.shape
    f32 = jnp.float32
    tb = min(256, T)
    assert T % tb == 0
    nt = T // tb
    pad_cols = ROUTE_COLS - N_GROUPS - N_EXPERTS
    w_r = jnp.pad(jnp.concatenate([w_grp, w_exp], axis=1), ((0, 0), (0, pad_cols))).astype(MXU_DTYPE)
    b_r = jnp.pad(jnp.concatenate([b_grp, b_exp]), (0, pad_cols)).reshape(1, ROUTE_COLS)
    tri = jnp.asarray(np.tril(np.ones((tb, tb), np.float32), -1), dtype=MXU_DTYPE)
    row = lambda i: (i, 0)
    fixed = lambda i: (0, 0)
    h, meta, wt, cnt = pl.pallas_call(
        _moe_route_kernel,
        grid=(nt,),
        in_specs=[pl.BlockSpec((tb, D), row), pl.BlockSpec((1, D), fixed),
                  pl.BlockSpec((D, ROUTE_COLS), fixed), pl.BlockSpec((1, ROUTE_COLS), fixed),
                  pl.BlockSpec((tb, tb), fixed)],
        out_specs=[pl.BlockSpec((tb, D), row), pl.BlockSpec((tb, ROUTE_COLS), row),
                   pl.BlockSpec((tb, ROUTE_COLS), row), pl.BlockSpec((1, ROUTE_COLS), fixed)],
        out_shape=[jax.ShapeDtypeStruct((T, D), f32), jax.ShapeDtypeStruct((T, ROUTE_COLS), jnp.int32),
                   jax.ShapeDtypeStruct((T, ROUTE_COLS), f32), jax.ShapeDtypeStruct((1, ROUTE_COLS), f32)],
        scratch_shapes=[pltpu.VMEM((1, ROUTE_COLS), f32)],
        compiler_params=pltpu.CompilerParams(dimension_semantics=("arbitrary",)),
        name="moe_route",
    )(x, ln_g.reshape(1, D), w_r, b_r, tri)

    counts = cnt[0, N_GROUPS:N_GROUPS + N_EXPERTS].astype(jnp.int32)
    padded = (counts + MOE_ROWS - 1) // MOE_ROWS * MOE_ROWS
    pad_end = jnp.cumsum(padded)
    pad_start = pad_end - padded
    n_blocks = -(-(T * TOP_K) // MOE_ROWS) + N_EXPERTS
    n_slots = n_blocks * MOE_ROWS
    dest = pad_start[meta[:, 0:2]] + meta[:, 2:4]
    dest = dest.reshape(nt, 1, 2 * tb)
    starts = jnp.arange(n_blocks, dtype=jnp.int32) * MOE_ROWS
    used = (pad_end[-1] // MOE_ROWS).astype(jnp.int32).reshape(1)
    blk_start = jnp.minimum(starts, pad_end[-1] - 1)
    blk_e = jnp.minimum(jnp.sum(pad_end[None, :] <= blk_start[:, None], axis=1), N_EXPERTS - 1).astype(jnp.int32)

    dest_spec = pl.BlockSpec((1, 1, 2 * tb), lambda i: (i, 0, 0), memory_space=pltpu.SMEM)
    xbuf = pl.pallas_call(
        _moe_dispatch_kernel,
        grid=(nt,),
        in_specs=[dest_spec, pl.BlockSpec((tb, D), row), pl.BlockSpec(memory_space=pl.ANY)],
        out_specs=pl.BlockSpec(memory_space=pl.ANY),
        out_shape=jax.ShapeDtypeStruct((n_slots, D), f32),
        scratch_shapes=[pltpu.SemaphoreType.DMA(())],
        input_output_aliases={2: 0},
        compiler_params=pltpu.CompilerParams(dimension_semantics=("arbitrary",)),
        name="moe_dispatch",
    )(dest, h, jnp.zeros((n_slots, D), f32))

    ybuf = pl.pallas_call(
        _moe_expert_kernel,
        grid_spec=pltpu.PrefetchScalarGridSpec(
            num_scalar_prefetch=2,
            grid=(n_blocks,),
            in_specs=[pl.BlockSpec((MOE_ROWS, D), lambda i, be, nb: (i, 0)),
                      pl.BlockSpec((1, 1, D, 2 * D_EXPERT), lambda i, be, nb: (layer, be[i], 0, 0)),
                      pl.BlockSpec((1, 1, D_EXPERT, D), lambda i, be, nb: (layer, be[i], 0, 0))],
            out_specs=pl.BlockSpec((MOE_ROWS, D), lambda i, be, nb: (i, 0)),
            scratch_shapes=[pltpu.VMEM((D, 2 * D_EXPERT), MXU_DTYPE), pltpu.VMEM((D_EXPERT, D), MXU_DTYPE)]),
        out_shape=jax.ShapeDtypeStruct((n_slots, D), f32),
        compiler_params=pltpu.CompilerParams(dimension_semantics=("arbitrary",),
                                             vmem_limit_bytes=VMEM_LIMIT_BYTES),
        name="moe_experts",
    )(blk_e, used, xbuf, w_gu, w_dn)

    return pl.pallas_call(
        _moe_combine_kernel,
        grid=(nt,),
        in_specs=[dest_spec, pl.BlockSpec((tb, D), row), pl.BlockSpec((tb, ROUTE_COLS), row),
                  pl.BlockSpec(memory_space=pl.ANY)],
        out_specs=pl.BlockSpec((tb, D), row),
        out_shape=jax.ShapeDtypeStruct((T, D), f32),
        scratch_shapes=[pltpu.VMEM((TOP_K, tb, D), f32), pltpu.SemaphoreType.DMA(())],
        compiler_params=pltpu.CompilerParams(dimension_semantics=("arbitrary",)),
        name="moe_combine",
    )(dest, x, wt, ybuf)


def _gelu_tanh(x):
    return 0.5 * x * (1.0 + jnp.tanh(math.sqrt(2.0 / math.pi) * (x + 0.044715 * (x * x * x))))


def _gmlp_kernel(x_ref, g_ref, win_ref, bin_ref, lng_ref, lnb_ref, ws_ref, bs_ref, wout_ref,
                 o_ref, v_ref, *, single_position):
    f32 = jnp.float32
    x = x_ref[...]
    rows = x.shape[0]
    h = x * lax.rsqrt(jnp.mean(x * x, axis=-1, keepdims=True) + EPS) * g_ref[...]
    z = _gelu_tanh(jnp.dot(h.astype(MXU_DTYPE), win_ref[...], preferred_element_type=f32) + bin_ref[...])
    u = z[:, :GM_WIDTH]
    v = z[:, GM_WIDTH:]
    mu = jnp.mean(v, axis=-1, keepdims=True)
    var = jnp.mean(jnp.square(v - mu), axis=-1, keepdims=True)
    v = (v - mu) * lax.rsqrt(var + EPS) * lng_ref[...] + lnb_ref[...]
    v_ref[0] = v
    vb = v.astype(MXU_DTYPE)
    if single_position:
        s = ws_ref[...].astype(f32) * vb.astype(f32) + bs_ref[...]
    else:
        parts = []
        for c in range(rows // CHUNK):
            vc = vb[c * CHUNK:(c + 1) * CHUNK]
            parts.append(jnp.concatenate(
                [jnp.dot(ws_ref[g], vc[:, g * GM_GROUP_DIM:(g + 1) * GM_GROUP_DIM], preferred_element_type=f32)
                 + bs_ref[g] for g in range(GM_GROUPS)], axis=1))
        s = jnp.concatenate(parts, axis=0)
    y = jnp.dot((u * s).astype(MXU_DTYPE), wout_ref[...], preferred_element_type=f32)
    o_ref[...] = x + y


def _gmlp_residual(x, ln_g, w_in, b_in, ln2_g, ln2_b, w_s, b_s, w_out, seq):
    T, D = x.shape
    f32 = jnp.float32
    cd = MXU_DTYPE
    single = seq == 1
    tb = T if single else 2 * CHUNK
    assert T % tb == 0 and (single or seq % tb == 0)
    steps_per_seq = 1 if single else seq // tb
    if single:
        ws = jnp.repeat(w_s[:, 0, 0], GM_GROUP_DIM).reshape(1, GM_WIDTH).astype(cd)
        bs = jnp.repeat(b_s[:, 0], GM_GROUP_DIM).reshape(1, GM_WIDTH)
        ws_spec = pl.BlockSpec((1, GM_WIDTH), lambda i: (0, 0))
        bs_spec = pl.BlockSpec((1, GM_WIDTH), lambda i: (0, 0))
    else:
        causal = np.tril(np.ones((CHUNK, CHUNK), bool))
        ws = jnp.where(causal[None], w_s, 0).astype(cd)
        bs = jnp.broadcast_to(b_s[:, :, None], (GM_GROUPS, CHUNK, GM_GROUP_DIM))
        ws_spec = pl.BlockSpec((GM_GROUPS, CHUNK, CHUNK), lambda i: (0, 0, 0))
        bs_spec = pl.BlockSpec((GM_GROUPS, CHUNK, GM_GROUP_DIM), lambda i: (0, 0, 0))
    fixed = lambda i: (0, 0)
    row = lambda i: (i, 0)
    return pl.pallas_call(
        functools.partial(_gmlp_kernel, single_position=single),
        grid=(T // tb,),
        in_specs=[pl.BlockSpec((tb, D), row), pl.BlockSpec((1, D), fixed),
                  pl.BlockSpec((D, 2 * GM_WIDTH), fixed), pl.BlockSpec((1, 2 * GM_WIDTH), fixed),
                  pl.BlockSpec((1, GM_WIDTH), fixed), pl.BlockSpec((1, GM_WIDTH), fixed),
                  ws_spec, bs_spec, pl.BlockSpec((GM_WIDTH, D), fixed)],
        out_specs=[pl.BlockSpec((tb, D), row), pl.BlockSpec((1, tb, GM_WIDTH), lambda i: (i // steps_per_seq, 0, 0))],
        out_shape=[jax.ShapeDtypeStruct((T, D), f32),
                   jax.ShapeDtypeStruct((T // (tb * steps_per_seq), tb, GM_WIDTH), f32)],
        compiler_params=pltpu.CompilerParams(dimension_semantics=("arbitrary",),
                                             vmem_limit_bytes=VMEM_LIMIT_BYTES),
        name="gmlp",
    )(x, ln_g.reshape(1, D), w_in.astype(cd), b_in.reshape(1, -1), ln2_g.reshape(1, -1), ln2_b.reshape(1, -1),
      ws, bs, w_out.astype(cd))


def kernel(x_prompt, x_sample, cache_nsa_kv, state_win_kv, page_table, rel_bias, ln_mix, ln_ffn,
           nsa_w_in, nsa_w_out, nsa_q_gain, nsa_k_gain, cmp_pe, cmp_w1, cmp_w2,
           gm_w_in, gm_b_in, gm_ln_g, gm_ln_b, gm_w_s, gm_b_s, gm_w_out,
           moe_w_grp, moe_b_grp, moe_w_exp, moe_b_exp, moe_w_gu, moe_w_dn):
    xp = x_prompt
    xs = x_sample
    kv_p, kv_s, win_p, win_s, gv_p, gv_s = [], [], [], [], [], []
    for i in range(DEPTH):
        a = i // N_MIXERS
        if i % N_MIXERS == 0:
            nsa = (ln_mix[i], nsa_w_in[a], nsa_w_out[a], nsa_q_gain[a], nsa_k_gain[a],
                   cmp_pe[a], cmp_w1[a], cmp_w2[a], rel_bias)
            xp, rp, wp = _nsa_prompt_layer(xp, *nsa)
            cmp_rows, sel_kvt = _page_gather(cache_nsa_kv, page_table, a)
            xs, rs, ws = _nsa_decode_layer(xs, cmp_rows, sel_kvt, state_win_kv[a], *nsa)
            kv_p.append(rp)
            kv_s.append(rs)
            win_p.append(wp)
            win_s.append(ws)
        else:
            gm = (ln_mix[i], gm_w_in[a], gm_b_in[a], gm_ln_g[a], gm_ln_b[a], gm_w_s[a], gm_b_s[a], gm_w_out[a])
            bp, sp = xp.shape[:2]
            bs_, ss = xs.shape[:2]
            xp2, vp = _gmlp_residual(xp.reshape(-1, D_MODEL), *gm, seq=sp)
            xs2, vs = _gmlp_residual(xs.reshape(-1, D_MODEL), *gm, seq=ss)
            xp = xp2.reshape(xp.shape)
            xs = xs2.reshape(xs.shape)
            start = ((sp - 1) // CHUNK) * CHUNK
            gv_p.append(vp[:, vp.shape[1] - (sp - start):])
            gv_s.append(vs.reshape(bs_, ss, GM_WIDTH))
        moe = (ln_ffn[i], moe_w_grp[i], moe_b_grp[i], moe_w_exp[i], moe_b_exp[i], moe_w_gu, moe_w_dn, i)
        xp = _hier_moe_residual(xp.reshape(-1, D_MODEL), *moe).reshape(xp.shape)
        xs = _hier_moe_residual(xs.reshape(-1, D_MODEL), *moe).reshape(xs.shape)
    new_kv_prompt = jnp.stack(kv_p, axis=2)
    new_kv_sample = jnp.stack(kv_s, axis=2)
    new_win_prompt = jnp.stack(win_p, axis=0)
    new_win_sample = jnp.stack(win_s, axis=0)
    new_gm_v_prompt = jnp.stack(gv_p, axis=0)
    new_gm_v_sample = jnp.stack(gv_s, axis=0)
    return (xp, xs, new_kv_prompt, new_kv_sample, new_win_prompt, new_win_sample, new_gm_v_prompt, new_gm_v_sample)
```

```python
import functools
import math

import jax
import jax.numpy as jnp
import numpy as np
from jax import lax
from jax.experimental import pallas as pl
from jax.experimental.pallas import tpu as pltpu

D_MODEL = 1024
PAGE_SIZE = 128
DEPTH = 2
N_MIXERS = 2
N_HEADS = 16
HEAD_DIM = 64
KV_HEADS = 4
Q_PER_KV = N_HEADS // KV_HEADS
CMP_LEN = 32
CMP_STRIDE = 16
CMP_HID = 64
SEL_BLOCK = 64
CMP_PER_SEL = SEL_BLOCK // CMP_STRIDE
N_SEL = 16
N_LOCAL = 2
WINDOW = 512
N_BUCKETS = 32
MAX_DISTANCE = 128
CHUNK = 128
GM_WIDTH = 2048
GM_GROUPS = 8
GM_GROUP_DIM = GM_WIDTH // GM_GROUPS
N_GROUPS = 4
EXPERTS_PER_GROUP = 8
N_EXPERTS = N_GROUPS * EXPERTS_PER_GROUP
TOP_K = 2
D_EXPERT = 512
EPS = 1e-6

LANES = 128
VMEM_LIMIT_BYTES = 56 * 1024 * 1024


NEG = -1e30
MXU_DTYPE = jnp.bfloat16
TQ = 256
TK = 256
GROUP_CHUNKS = 3
NEAR_CHUNKS = 5
BIAS_SPAN = 128
LOG2E = 1.4426950408889634


def _bucket_of_distance():
    d = np.arange(BIAS_SPAN)
    max_exact = N_BUCKETS // 2
    nf = np.maximum(d, 1).astype(np.float32)
    large = max_exact + (np.log(nf / np.float32(max_exact)) / np.float32(math.log(MAX_DISTANCE / max_exact))
                         * np.float32(N_BUCKETS - max_exact)).astype(np.int32)
    large = np.minimum(large, N_BUCKETS - 1)
    return np.where(d < max_exact, d, large).astype(np.int32)


def _attn_bias_tiles(rel_bias, seq):
    fd = rel_bias[_bucket_of_distance()].T
    far = fd[:, BIAS_SPAN - 1]
    fd = fd - far[:, None]

    def by_distance(d):
        return jnp.where(d >= 0, LOG2E * jnp.take(fd, np.clip(d, 0, BIAS_SPAN - 1), axis=1), NEG)

    def toeplitz(first_distance):
        c = np.arange(2 * TK)
        g = by_distance(first_distance - np.where(c < TK, c, c - 2 * TK))
        flat = jnp.tile(g, (1, TQ))[:, :TQ * (2 * TK - 1)]
        return flat.reshape(-1, TQ, 2 * TK - 1)[:, :, :TK]

    b0 = toeplitz(0)
    b1 = toeplitz(TQ)
    i = np.arange(TQ)[:, None]
    j = np.arange(TK)[None, :]
    w2 = np.where(2 * TQ + i - j <= WINDOW, 0.0, NEG).astype(np.float32)
    nc = seq // CMP_STRIDE
    span = -(-(BIAS_SPAN + TQ) // CMP_STRIDE)
    r = np.arange(nc - span, nc + span)[:, None]
    dc = np.arange(TQ)[None, :] - CMP_STRIDE * (r - nc) - (CMP_LEN - 1)
    n_heads = fd.shape[0]
    pat = jnp.concatenate([jnp.zeros((n_heads, nc - span, TQ), jnp.float32), by_distance(dc),
                           jnp.full((n_heads, nc - span, TQ), NEG, jnp.float32)], axis=1)
    return b0, b1, jnp.asarray(w2), pat


def _nsa_attn_kernel(q_ref, kc_ref, vct_ref, kat_ref, vsa_ref, kwt_ref, vwa_ref, g_ref,
                     b0_ref, b1_ref, w2_ref, pat_ref, o_ref, m_sc, acc_sc, imp_sc, qa_sc, p_sc, a_sc, *, nc, nb):
    f32 = jnp.float32
    cd = MXU_DTYPE
    qb = pl.program_id(2)

    def compress_and_select(ncu, all_forced):
        nbu = ncu // CMP_PER_SEL
        r0 = pl.multiple_of(nc - (TQ // CMP_STRIDE) * qb, TQ // CMP_STRIDE)
        t_row = qb * TQ + lax.broadcasted_iota(jnp.int32, (1, TQ), 1)
        has_cmp = t_row >= CMP_LEN - 1
        for h in range(Q_PER_KV):
            st = lax.dot_general(kc_ref[0, 0, 0:ncu, :], q_ref[0, h], (((1,), (1,)), ((), ())),
                                 preferred_element_type=f32)
            st = st + pat_ref[h, pl.ds(r0, ncu), :]
            e = jnp.exp2(st - jnp.max(st, axis=0, keepdims=True))
            inv = jnp.where(has_cmp, 1.0 / jnp.sum(e, axis=0, keepdims=True), 0.0)
            pt = e * inv
            for c in range(TQ // LANES):
                part = pt[:, c * LANES:(c + 1) * LANES]
                if h == 0:
                    imp_sc[c, 0:ncu, :] = part
                else:
                    imp_sc[c, 0:ncu, :] += part
            oct_h = jnp.dot(vct_ref[0, 0, :, 0:ncu], pt.astype(cd), preferred_element_type=f32)
            o_ref[0, :, h * HEAD_DIM:(h + 1) * HEAD_DIM] = g_ref[0][:, 3 * h:3 * h + 1] * oct_h.T

        imp = jnp.concatenate(
            [sum(imp_sc[c, pl.ds(r, nbu, stride=CMP_PER_SEL), :] for r in range(CMP_PER_SEL))
             for c in range(TQ // LANES)], axis=1)
        blk = lax.broadcasted_iota(jnp.int32, (nbu, TQ), 0)
        t_blk = (qb * TQ + lax.broadcasted_iota(jnp.int32, (nbu, TQ), 1)) // SEL_BLOCK
        dd = t_blk - blk
        forced = (blk == 0) | ((dd >= 0) & (dd < N_LOCAL))
        if all_forced:
            score = jnp.where(forced | (dd < 0), -jnp.inf, imp)
            pen_t = jnp.where(forced, 0.0, NEG)
            rounds = N_SEL - (N_LOCAL + 1)
        else:
            score = jnp.where(forced, jnp.inf, jnp.where(dd >= 0, imp, -jnp.inf))
            pen_t = jnp.full((nbu, TQ), NEG, f32)
            rounds = min(N_SEL, nbu)
        for _ in range(rounds):
            best = jnp.max(score, axis=0, keepdims=True)
            first = jnp.min(jnp.where(score == best, blk, nbu), axis=0, keepdims=True)
            hit = blk == first
            pen_t = jnp.where(hit, 0.0, pen_t)
            score = jnp.where(hit, -jnp.inf, score)
        if nbu < nb:
            pen_t = jnp.concatenate([pen_t, jnp.full((nb - nbu, TQ), NEG, f32)], axis=0)
        pen = pen_t.T.astype(cd)
        pad = jnp.zeros((TQ, qa_sc.shape[2] - nb - HEAD_DIM), cd)
        for h in range(Q_PER_KV):
            qa_sc[h] = jnp.concatenate([pen, q_ref[0, h], pad], axis=1)

    n_tiles = nc * CMP_STRIDE // TQ
    n_var = 4 if (n_tiles % 4 == 0 and (nc // 4) % (2 * CMP_STRIDE) == 0) else 1
    for v in range(n_var):
        lo, hi = v * n_tiles // n_var, (v + 1) * n_tiles // n_var
        ncu = (v + 1) * nc // n_var
        if v == 0:
            @pl.when(qb == 0)
            def _():
                compress_and_select(ncu, all_forced=False)
            lo = 1

        @pl.when((qb >= lo) & (qb < hi))
        def _():
            compress_and_select(ncu, all_forced=TQ >= N_LOCAL * SEL_BLOCK)


    m_sc[...] = jnp.full(m_sc.shape, NEG, f32)
    acc_sc[...] = jnp.zeros(acc_sc.shape, f32)

    def run(*work):
        jobs = [(br, c, bias) for br, chunks in work for c, bias in chunks]
        for slot, ((b, lhs, k_ref, _), c, bias) in enumerate(jobs):
            for h in range(Q_PER_KV):
                s = jnp.dot(lhs(h), k_ref[0, 0, c], preferred_element_type=f32)
                if bias is not None:
                    s = s + bias(h)
                m_prev = m_sc[b, h]
                m_new = jnp.maximum(m_prev, jnp.max(s, axis=1, keepdims=True))
                p_sc[slot, h] = jnp.exp2(s - jnp.concatenate([m_new] * (TK // LANES), axis=1)).astype(cd)
                a_sc[slot, h] = jnp.exp2(m_prev - m_new)
                m_sc[b, h] = m_new
        for slot, ((b, _, _, v_ref), c, _) in enumerate(jobs):
            for h in range(Q_PER_KV):
                acc_sc[b, h] = a_sc[slot, h] * acc_sc[b, h] + jnp.dot(p_sc[slot, h], v_ref[0, 0, c],
                                                                      preferred_element_type=f32)

    def emit(b, gate_col):
        for h in range(Q_PER_KV):
            acc = acc_sc[b, h]
            w = g_ref[0][:, 3 * h + gate_col:3 * h + gate_col + 1] / acc[:, HEAD_DIM:HEAD_DIM + 1]
            o_ref[0, :, h * HEAD_DIM:(h + 1) * HEAD_DIM] += w * acc[:, :HEAD_DIM]

    own = lambda h: b0_ref[h]
    prev = lambda h: b1_ref[h]
    edge = lambda h: w2_ref[...]

    sel = (0, lambda h: qa_sc[h], kat_ref, vsa_ref)
    win = (1, lambda h: q_ref[0, h], kwt_ref, vwa_ref)
    n_far = jnp.maximum(qb - 1, 0)

    def far_group(i, carry):
        run((sel, [(GROUP_CHUNKS * i + j, None) for j in range(GROUP_CHUNKS)]))
        return carry

    lax.fori_loop(0, n_far // GROUP_CHUNKS, far_group, 0)
    for left in range(1, GROUP_CHUNKS):
        @pl.when(n_far % GROUP_CHUNKS == left)
        def _():
            run((sel, [(n_far - left + j, None) for j in range(left)]))

    @pl.when(qb >= 2)
    def _():
        run((sel, [(qb - 1, prev), (qb, own)]), (win, [(qb - 2, edge), (qb - 1, prev), (qb, own)]))

    @pl.when(qb == 1)
    def _():
        run((sel, [(qb - 1, prev), (qb, own)]), (win, [(qb - 1, prev), (qb, own)]))

    @pl.when(qb == 0)
    def _():
        run((sel, [(qb, own)]), (win, [(qb, own)]))

    emit(0, 1)
    emit(1, 2)


def _nsa_attention(qs, kcp, vct, kat, vsa, kwt, vwa, gt, rel_bias):
    B, _, S, _ = qs.shape
    assert S % TQ == 0 and TQ == TK and WINDOW == 2 * TQ and S // SEL_BLOCK >= N_SEL
    nc = S // CMP_STRIDE
    nb = S // SEL_BLOCK
    nch = S // TK
    ka = kat.shape[3]
    b0, b1, w2, pat = _attn_bias_tiles(rel_bias, S)
    G = Q_PER_KV
    kv_chunks = pl.BlockSpec((1, 1, nch, HEAD_DIM, TK), lambda b, k, i: (b, k, 0, 0, 0))
    ka_chunks = pl.BlockSpec((1, 1, nch, ka, TK), lambda b, k, i: (b, k, 0, 0, 0))
    v_chunks = pl.BlockSpec((1, 1, nch, TK, LANES), lambda b, k, i: (b, k, 0, 0, 0))
    return pl.pallas_call(
        functools.partial(_nsa_attn_kernel, nc=nc, nb=nb),
        grid=(B, KV_HEADS, S // TQ),
        in_specs=[
            pl.BlockSpec((1, G, TQ, HEAD_DIM), lambda b, k, i: (b, k, i, 0)),
            pl.BlockSpec((1, 1, nc, HEAD_DIM), lambda b, k, i: (b, k, 0, 0)),
            pl.BlockSpec((1, 1, HEAD_DIM, nc), lambda b, k, i: (b, k, 0, 0)),
            ka_chunks, v_chunks, kv_chunks, v_chunks,
            pl.BlockSpec((1, TQ, LANES), lambda b, k, i: (b, i, k)),
            pl.BlockSpec((G, TQ, TK), lambda b, k, i: (k, 0, 0)),
            pl.BlockSpec((G, TQ, TK), lambda b, k, i: (k, 0, 0)),
            pl.BlockSpec((TQ, TK), lambda b, k, i: (0, 0)),
            pl.BlockSpec((G, 2 * nc, TQ), lambda b, k, i: (k, 0, 0)),
        ],
        out_specs=pl.BlockSpec((1, TQ, G * HEAD_DIM), lambda b, k, i: (b, i, k)),
        out_shape=jax.ShapeDtypeStruct((B, S, N_HEADS * HEAD_DIM), jnp.float32),
        scratch_shapes=[
            pltpu.VMEM((2, G, TQ, LANES), jnp.float32),
            pltpu.VMEM((2, G, TQ, LANES), jnp.float32),
            pltpu.VMEM((TQ // LANES, nc, LANES), jnp.float32),
            pltpu.VMEM((G, TQ, ka), MXU_DTYPE),
            pltpu.VMEM((NEAR_CHUNKS, G, TQ, TK), MXU_DTYPE),
            pltpu.VMEM((NEAR_CHUNKS, G, TQ, LANES), jnp.float32),
        ],
        compiler_params=pltpu.CompilerParams(
            dimension_semantics=("arbitrary", "arbitrary", "arbitrary"),
            vmem_limit_bytes=VMEM_LIMIT_BYTES),
        name="nsa_attn",
    )(qs, kcp, vct, kat, vsa, kwt, vwa, gt, b0, b1, w2, pat)


def _nsa_prompt_layer(x, ln_g, w_in, w_out, q_gain, k_gain, pe, w1, w2, rel_bias):
    B, S, D = x.shape
    x2 = x.reshape(B * S, D)
    qs, rows, win, kat, vsa, kwt, vwa, c, gates = _nsa_project(x2, ln_g, w_in, q_gain, k_gain, B, S)
    nc = S // CMP_STRIDE
    cmp = _compress(c.reshape(2, B, nc, CMP_STRIDE * SLOT), pe, w1, w2, k_gain[0])
    cmp = cmp.reshape(2, B, nc, KV_HEADS, HEAD_DIM).astype(MXU_DTYPE)
    gt = gates.reshape(B, S, KV_HEADS * LANES)
    o = _nsa_attention(qs, cmp[0].transpose(0, 2, 1, 3), cmp[1].transpose(0, 2, 3, 1), kat, vsa, kwt, vwa, gt,
                       rel_bias)
    y = _proj_residual(x2, o.reshape(B * S, HQ), w_out)
    return (y.reshape(B, S, D), rows.reshape(B, 4, KV_HEADS, HEAD_DIM, S).transpose(0, 4, 1, 2, 3),
            win.reshape(B, WINDOW, 2, KV_HEADS, HEAD_DIM))


PAGES_PER_STEP = 8


def _page_gather_kernel(pt_ref, *refs):
    del pt_ref
    pages = refs[:PAGES_PER_STEP]
    c_ref, kvt_ref, rows_sc = refs[PAGES_PER_STEP:]
    for p, pg in enumerate(pages):
        pos = slice(p * PAGE_SIZE, (p + 1) * PAGE_SIZE)
        for slot in range(2):
            for pair in range(KV_HEADS // 2):
                tile = pg[0, 0, slot, 2 * pair:2 * pair + 2].reshape(LANES, PAGE_SIZE)
                rows_sc[slot * 2 + pair, pos, :] = tile.T
        for slot in range(2):
            kvt_ref[slot, 0, :, :, pos] = pg[0, 0, 2 + slot].astype(MXU_DTYPE)
    n_chunks = PAGES_PER_STEP * PAGE_SIZE // CMP_STRIDE
    for l in range(CMP_STRIDE):
        for t in range(4):
            lo = l * SLOT + (t % 2) * LANES
            c_ref[t // 2, 0, :, lo:lo + LANES] = rows_sc[t, pl.ds(l, n_chunks, stride=CMP_STRIDE), :].astype(MXU_DTYPE)


def _page_gather(cache, page_table, layer):
    nseq, n_pages = page_table.shape
    assert n_pages % PAGES_PER_STEP == 0 and 2 * LANES == SLOT
    P = n_pages * PAGE_SIZE
    cache_t = cache.transpose(0, 2, 3, 4, 5, 1)
    page_specs = [
        pl.BlockSpec((1, 1, 4, KV_HEADS, HEAD_DIM, PAGE_SIZE),
                     functools.partial(lambda r, n, j, pt: (pt[n, PAGES_PER_STEP * j + r], layer, 0, 0, 0, 0), r))
        for r in range(PAGES_PER_STEP)]
    rows_per_step = PAGES_PER_STEP * PAGE_SIZE
    return pl.pallas_call(
        _page_gather_kernel,
        grid_spec=pltpu.PrefetchScalarGridSpec(
            num_scalar_prefetch=1,
            grid=(nseq, n_pages // PAGES_PER_STEP),
            in_specs=page_specs,
            out_specs=[pl.BlockSpec((2, 1, rows_per_step // CMP_STRIDE, CMP_STRIDE * SLOT),
                                    lambda n, j, pt: (0, n, j, 0)),
                       pl.BlockSpec((2, 1, KV_HEADS, HEAD_DIM, rows_per_step), lambda n, j, pt: (0, n, 0, 0, j))],
            scratch_shapes=[pltpu.VMEM((4, rows_per_step, LANES), jnp.float32)]),
        out_shape=[jax.ShapeDtypeStruct((2, nseq, P // CMP_STRIDE, CMP_STRIDE * SLOT), MXU_DTYPE),
                   jax.ShapeDtypeStruct((2, nseq, KV_HEADS, HEAD_DIM, P), MXU_DTYPE)],
        compiler_params=pltpu.CompilerParams(dimension_semantics=("arbitrary", "arbitrary")),
        name="page_gather",
    )(page_table, *([cache_t] * PAGES_PER_STEP))


def _nsa_decode_kernel(q_ref, qbd_ref, kc_ref, vc_ref, kst_ref, vst_ref, win_ref, rows_ref, wnew_ref, g_ref,
                       oh_ref, bc_ref, bs_ref, bw_ref, bn_ref, o_ref, pen_sc, *, nb):
    f32 = jnp.float32
    cd = MXU_DTYPE
    nt = (((1,), (1,)), ((), ()))
    rows = rows_ref[0]
    wnew = wnew_ref[0]

    def softmax_with_new(s, s_new):
        m = jnp.maximum(jnp.max(s, axis=1, keepdims=True), s_new)
        e = jnp.exp(s - m)
        e_new = jnp.exp(s_new - m)
        inv = 1.0 / (jnp.sum(e, axis=1, keepdims=True) + e_new)
        return e * inv, e_new * inv

    def bf(x):
        return x.astype(cd).astype(f32)

    for k in range(KV_HEADS):
        hs = slice(k * Q_PER_KV, (k + 1) * Q_PER_KV)
        q = q_ref[0, hs, :]
        gates = g_ref[0, hs, :]
        sc = lax.dot_general(q, kc_ref[0, k], nt, preferred_element_type=f32) + bc_ref[hs, :]
        ec = jnp.exp(sc - jnp.max(sc, axis=1, keepdims=True))
        pc = ec / jnp.sum(ec, axis=1, keepdims=True)
        o_ref[0, hs, :] = gates[:, 0:1] * jnp.dot(pc.astype(cd), vc_ref[0, k], preferred_element_type=f32)

        imp = sum(pc[:, r * nb:(r + 1) * nb] for r in range(CMP_PER_SEL))
        imp = jnp.sum(imp, axis=0, keepdims=True)
        blk = lax.broadcasted_iota(jnp.int32, imp.shape, 1)
        forced = (blk == 0) | (blk == nb - 1)
        score = jnp.where(forced, jnp.inf, imp)
        pen = jnp.full(imp.shape, NEG, f32)
        for _ in range(N_SEL - 1):
            best = jnp.max(score, axis=1, keepdims=True)
            first = jnp.min(jnp.where(score == best, blk, nb), axis=1, keepdims=True)
            hit = blk == first
            pen = jnp.where(hit, 0.0, pen)
            score = jnp.where(hit, -jnp.inf, score)
        pen_sc[hs, :] = jnp.broadcast_to(pen, (Q_PER_KV, nb))

    n_keys = kst_ref.shape[-1]
    qbd = qbd_ref[0]
    key_pen = jnp.dot(pen_sc[...].astype(cd), oh_ref[...], preferred_element_type=f32)
    ss = (jnp.dot(qbd, kst_ref[0, 0].reshape(SLOT, n_keys), preferred_element_type=f32)
          + key_pen + bs_ref[...])
    s_new = (jnp.sum(qbd.astype(f32) * bf(rows[:, 2 * SLOT:3 * SLOT]), axis=1, keepdims=True) + bn_ref[...])
    p, p_new = softmax_with_new(ss, s_new)
    sel_all = (lax.dot_general(p.astype(cd), vst_ref[0, 0].reshape(SLOT, n_keys), nt,
                               preferred_element_type=f32)
               + bf(p_new) * bf(rows[:, 3 * SLOT:4 * SLOT]))

    for k in range(KV_HEADS):
        hs = slice(k * Q_PER_KV, (k + 1) * Q_PER_KV)
        lanes = slice(k * HEAD_DIM, (k + 1) * HEAD_DIM)
        q = q_ref[0, hs, :]
        gates = g_ref[0, hs, :]
        sw = lax.dot_general(q, win_ref[0, 0, k], nt, preferred_element_type=f32) + bw_ref[hs, :]
        w_new = (jnp.sum(q.astype(f32) * bf(wnew[:, :SLOT][:, lanes]), axis=1, keepdims=True) + bn_ref[hs, :])
        p, p_new = softmax_with_new(sw, w_new)
        acc = (jnp.dot(p.astype(cd), win_ref[0, 1, k], preferred_element_type=f32)
               + bf(p_new) * bf(wnew[:, SLOT:][:, lanes]))
        o_ref[0, hs, :] = o_ref[0, hs, :] + gates[:, 1:2] * sel_all[hs, :][:, lanes] + gates[:, 2:3] * acc


def _nsa_decode_layer(x, cmp_rows, sel_kvt, win_buf, ln_g, w_in, w_out, q_gain, k_gain, pe, w1, w2, rel_bias):
    N, L, D = x.shape
    P = sel_kvt.shape[-1]
    wb = win_buf.shape[1]
    assert L == 1 and P % SEL_BLOCK == 0 and wb == WINDOW and P >= BIAS_SPAN and P // SEL_BLOCK >= N_SEL - 1
    f32 = jnp.float32
    cd = MXU_DTYPE
    nc = P // CMP_STRIDE
    nb = P // SEL_BLOCK
    x2 = x.reshape(N, D)
    q, rows, wnew, gates = _nsa_project(x2, ln_g, w_in, q_gain, k_gain, N, 1)
    cmp = _compress(cmp_rows, pe, w1, w2, k_gain[0])
    cmp = cmp.reshape(2, N, nb, CMP_PER_SEL, KV_HEADS, HEAD_DIM).transpose(0, 1, 4, 3, 2, 5)
    cmp = cmp.reshape(2, N, KV_HEADS, nc, HEAD_DIM).astype(cd)
    qs = (q.reshape(N, N_HEADS, HEAD_DIM) * HEAD_DIM ** -0.5).astype(cd)
    own = (np.arange(N_HEADS)[:, None] // Q_PER_KV == np.arange(SLOT)[None, :] // HEAD_DIM)
    qbd = jnp.where(own[None], jnp.tile(qs, (1, 1, KV_HEADS)), jnp.zeros((), cd))
    win = win_buf.transpose(0, 2, 3, 1, 4).astype(cd)
    onehot = jnp.asarray(np.arange(nb)[:, None] == (np.arange(P) // SEL_BLOCK)[None, :], dtype=cd)
    fd = rel_bias[_bucket_of_distance()].T
    fd = fd - fd[:, BIAS_SPAN - 1:]
    cidx = (np.arange(nc) % nb) * CMP_PER_SEL + np.arange(nc) // nb
    dc = P - (cidx * CMP_STRIDE + CMP_LEN - 1)
    b_cmp = jnp.where(dc >= 0, jnp.take(fd, np.clip(dc, 0, BIAS_SPAN - 1), axis=1), NEG)
    b_sel = jnp.take(fd, np.clip(P - np.arange(P), 0, BIAS_SPAN - 1), axis=1)
    b_win = jnp.take(fd, np.clip(wb - np.arange(wb), 0, BIAS_SPAN - 1), axis=1)
    b_new = fd[:, 0:1]
    seq3 = lambda n: (n, 0, 0)
    seq4 = lambda n: (n, 0, 0, 0)
    fixed = lambda n: (0, 0)
    o = pl.pallas_call(
        functools.partial(_nsa_decode_kernel, nb=nb),
        grid=(N,),
        in_specs=[pl.BlockSpec((1, N_HEADS, HEAD_DIM), seq3),
                  pl.BlockSpec((1, N_HEADS, SLOT), seq3),
                  pl.BlockSpec((1, KV_HEADS, nc, HEAD_DIM), seq4),
                  pl.BlockSpec((1, KV_HEADS, nc, HEAD_DIM), seq4),
                  pl.BlockSpec((1, 1, KV_HEADS, HEAD_DIM, P), lambda n: (0, n, 0, 0, 0)),
                  pl.BlockSpec((1, 1, KV_HEADS, HEAD_DIM, P), lambda n: (1, n, 0, 0, 0)),
                  pl.BlockSpec((1, 2, KV_HEADS, wb, HEAD_DIM), lambda n: (n, 0, 0, 0, 0)),
                  pl.BlockSpec((1, 1, 4 * SLOT), seq3),
                  pl.BlockSpec((1, 1, 2 * SLOT), seq3),
                  pl.BlockSpec((1, N_HEADS, 3), seq3),
                  pl.BlockSpec((nb, P), fixed), pl.BlockSpec((N_HEADS, nc), fixed),
                  pl.BlockSpec((N_HEADS, P), fixed), pl.BlockSpec((N_HEADS, wb), fixed),
                  pl.BlockSpec((N_HEADS, 1), fixed)],
        out_specs=pl.BlockSpec((1, N_HEADS, HEAD_DIM), seq3),
        out_shape=jax.ShapeDtypeStruct((N, N_HEADS, HEAD_DIM), f32),
        scratch_shapes=[pltpu.VMEM((N_HEADS, nb), f32)],
        compiler_params=pltpu.CompilerParams(dimension_semantics=("arbitrary",),
                                             vmem_limit_bytes=VMEM_LIMIT_BYTES),
        name="nsa_decode_attn",
    )(qs, qbd, cmp[0], cmp[1], sel_kvt, sel_kvt, win, rows.reshape(N, 1, 4 * SLOT), wnew.reshape(N, 1, 2 * SLOT),
      gates[:, :3 * N_HEADS].reshape(N, N_HEADS, 3), onehot, b_cmp, b_sel, b_win, b_new)
    y = _proj_residual(x2, o.reshape(N, HQ), w_out)
    new_win = jnp.concatenate([win_buf, wnew.reshape(N, 1, 2, KV_HEADS, HEAD_DIM)], axis=1)[:, 1:]
    return y.reshape(N, 1, D), rows.reshape(N, 1, 4, KV_HEADS, HEAD_DIM), new_win


HQ = N_HEADS * HEAD_DIM
SLOT = KV_HEADS * HEAD_DIM
NSA_COLS = HQ + 6 * SLOT + 3 * N_HEADS
NSA_COLS_PAD = -(-NSA_COLS // LANES) * LANES
GATE_COLS = NSA_COLS_PAD - HQ - 6 * SLOT


def _nsa_proj_kernel(x_ref, g_ref, w_ref, qg_ref, kg_ref, *outs, prompt, n_blocks):
    f32 = jnp.float32
    x = x_ref[...]
    xn = x * lax.rsqrt(jnp.mean(x * x, axis=-1, keepdims=True) + EPS) * g_ref[...]
    h = jnp.dot(xn.astype(MXU_DTYPE), w_ref[...], preferred_element_type=f32)

    def head_norm(v, gain):
        return v * lax.rsqrt(jnp.mean(v * v, axis=-1, keepdims=True) + EPS) * gain

    def slot(s, kv):
        lo = HQ + s * SLOT + kv * HEAD_DIM
        return h[:, lo:lo + HEAD_DIM]

    gates = 1.0 / (1.0 + jnp.exp(-h[:, HQ + 6 * SLOT:]))
    if prompt:
        q_ref, rows_ref, win_ref, kat_ref, vsa_ref, kwt_ref, vwa_ref, c_ref, gate_ref = outs
        for hh in range(N_HEADS):
            qn = head_norm(h[:, hh * HEAD_DIM:(hh + 1) * HEAD_DIM], qg_ref[...])
            q_ref[0, hh] = (qn * (HEAD_DIM ** -0.5 * LOG2E)).astype(MXU_DTYPE)
        c_ref[0, 0] = h[:, HQ:HQ + SLOT].astype(MXU_DTYPE)
        c_ref[1, 0] = h[:, HQ + SLOT:HQ + 2 * SLOT].astype(MXU_DTYPE)
        rows, ka = x.shape[0], kat_ref.shape[3]
        nb = n_blocks
        first_key = (pl.program_id(0) % (nb * SEL_BLOCK // rows)) * rows
        key_blk = (first_key + lax.broadcasted_iota(jnp.int32, (nb, rows), 1)) // SEL_BLOCK
        onehot = jnp.where(key_blk == lax.broadcasted_iota(jnp.int32, (nb, rows), 0), 1.0, 0.0).astype(MXU_DTYPE)
        kpad = jnp.zeros((ka - nb - HEAD_DIM, rows), MXU_DTYPE)
        ones = jnp.ones((rows, LANES - HEAD_DIM), MXU_DTYPE)
    else:
        q_ref, rows_ref, win_ref, gate_ref = outs
        for hh in range(N_HEADS):
            q_ref[:, hh * HEAD_DIM:(hh + 1) * HEAD_DIM] = head_norm(h[:, hh * HEAD_DIM:(hh + 1) * HEAD_DIM],
                                                                    qg_ref[...])
    if prompt:
        rows_ref[0, 0:2 * SLOT, :] = h[:, HQ:HQ + 2 * SLOT].T
        rows_ref[0, 3 * SLOT:4 * SLOT, :] = h[:, HQ + 3 * SLOT:HQ + 4 * SLOT].T
    else:
        rows_ref[:, 0:2 * SLOT] = h[:, HQ:HQ + 2 * SLOT]
        rows_ref[:, 3 * SLOT:4 * SLOT] = h[:, HQ + 3 * SLOT:HQ + 4 * SLOT]
    for kv in range(KV_HEADS):
        lanes = slice(kv * HEAD_DIM, (kv + 1) * HEAD_DIM)
        ksn = head_norm(slot(2, kv), kg_ref[1:2, :])
        kwn = head_norm(slot(4, kv), kg_ref[2:3, :])
        if prompt:
            ksn_t = ksn.T
            rows_ref[0, 2 * SLOT + kv * HEAD_DIM:2 * SLOT + (kv + 1) * HEAD_DIM, :] = ksn_t
            win_ref[0, 0, :, lanes] = kwn
            kat_ref[0, kv, 0] = jnp.concatenate([onehot, ksn_t.astype(MXU_DTYPE), kpad], axis=0)
            kwt_ref[0, kv, 0] = kwn.T.astype(MXU_DTYPE)
            vsa_ref[0, kv, 0] = jnp.concatenate([slot(3, kv).astype(MXU_DTYPE), ones], axis=1)
            vwa_ref[0, kv, 0] = jnp.concatenate([slot(5, kv).astype(MXU_DTYPE), ones], axis=1)
        else:
            rows_ref[:, 2 * SLOT + kv * HEAD_DIM:2 * SLOT + (kv + 1) * HEAD_DIM] = ksn
            win_ref[:, lanes] = kwn
    if prompt:
        win_ref[0, 0, :, SLOT:2 * SLOT] = h[:, HQ + 5 * SLOT:HQ + 6 * SLOT]
    else:
        win_ref[:, SLOT:2 * SLOT] = h[:, HQ + 5 * SLOT:HQ + 6 * SLOT]
    if prompt:
        gate_ref[...] = jnp.zeros(gate_ref.shape, f32)
        per_kv = 3 * Q_PER_KV
        for kv in range(KV_HEADS):
            gate_ref[:, kv * LANES:kv * LANES + per_kv] = gates[:, kv * per_kv:(kv + 1) * per_kv]
    else:
        gate_ref[...] = gates


def _nsa_project(x, ln_g, w_in, q_gain, k_gain, batch, seq):
    T, D = x.shape
    f32 = jnp.float32
    cd = MXU_DTYPE
    prompt = seq > 1
    tb = TK if prompt else T
    w = jnp.pad(w_in, ((0, 0), (0, NSA_COLS_PAD - NSA_COLS))).astype(cd)
    fixed = lambda i: (0, 0)
    row = lambda i: (i, 0)
    in_specs = [pl.BlockSpec((tb, D), row), pl.BlockSpec((1, D), fixed), pl.BlockSpec((D, NSA_COLS_PAD), fixed),
                pl.BlockSpec((1, HEAD_DIM), fixed), pl.BlockSpec((3, HEAD_DIM), fixed)]
    if prompt:
        assert seq % tb == 0 and WINDOW == 2 * tb
        n = seq // tb
        n_blocks = seq // SEL_BLOCK
        ka = -(-(n_blocks + HEAD_DIM) // LANES) * LANES
        chunk = lambda i: (i // n, 0, i % n, 0, 0)
        kt_spec = pl.BlockSpec((1, KV_HEADS, 1, HEAD_DIM, tb), chunk)
        ka_spec = pl.BlockSpec((1, KV_HEADS, 1, ka, tb), chunk)
        v_spec = pl.BlockSpec((1, KV_HEADS, 1, tb, LANES), chunk)
        kt_shape = jax.ShapeDtypeStruct((batch, KV_HEADS, n, HEAD_DIM, tb), cd)
        ka_shape = jax.ShapeDtypeStruct((batch, KV_HEADS, n, ka, tb), cd)
        v_shape = jax.ShapeDtypeStruct((batch, KV_HEADS, n, tb, LANES), cd)
        out_specs = [
            pl.BlockSpec((1, N_HEADS, tb, HEAD_DIM), lambda i: (i // n, 0, i % n, 0)),
            pl.BlockSpec((1, 4 * SLOT, tb), lambda i: (i // n, 0, i % n)),
            pl.BlockSpec((1, 1, tb, 2 * SLOT), lambda i: (i // n, jnp.maximum(i % n - (n - 2), 0), 0, 0)),
            ka_spec, v_spec, kt_spec, v_spec,
            pl.BlockSpec((2, 1, tb, SLOT), lambda i: (0, i // n, i % n, 0)),
            pl.BlockSpec((tb, KV_HEADS * LANES), row)]
        out_shape = [
            jax.ShapeDtypeStruct((batch, N_HEADS, seq, HEAD_DIM), cd),
            jax.ShapeDtypeStruct((batch, 4 * SLOT, seq), f32),
            jax.ShapeDtypeStruct((batch, 2, tb, 2 * SLOT), f32),
            ka_shape, v_shape, kt_shape, v_shape,
            jax.ShapeDtypeStruct((2, batch, seq, SLOT), cd),
            jax.ShapeDtypeStruct((T, KV_HEADS * LANES), f32)]
    else:
        out_specs = [pl.BlockSpec((tb, HQ), row), pl.BlockSpec((tb, 4 * SLOT), row),
                     pl.BlockSpec((tb, 2 * SLOT), row), pl.BlockSpec((tb, GATE_COLS), row)]
        n_blocks = 0
        out_shape = [jax.ShapeDtypeStruct((T, HQ), f32), jax.ShapeDtypeStruct((T, 4 * SLOT), f32),
                     jax.ShapeDtypeStruct((T, 2 * SLOT), f32), jax.ShapeDtypeStruct((T, GATE_COLS), f32)]
    return pl.pallas_call(
        functools.partial(_nsa_proj_kernel, prompt=prompt, n_blocks=n_blocks),
        grid=(T // tb,),
        in_specs=in_specs, out_specs=out_specs, out_shape=out_shape,
        compiler_params=pltpu.CompilerParams(dimension_semantics=("arbitrary",),
                                             vmem_limit_bytes=VMEM_LIMIT_BYTES),
        name="nsa_proj",
    )(x, ln_g.reshape(1, D), w, q_gain.reshape(1, HEAD_DIM), k_gain)


def _compress_kernel(c_ref, w1_ref, pe_ref, w2_ref, kg_ref, o_ref):
    f32 = jnp.float32
    pairs = SLOT // LANES
    c = c_ref[0, 0]
    u = [jnp.dot(jnp.concatenate([c[:, l * SLOT + p * LANES:l * SLOT + (p + 1) * LANES]
                                  for l in range(CMP_STRIDE)], axis=1).astype(MXU_DTYPE),
                 w1_ref[0], preferred_element_type=f32) for p in range(pairs)]
    first = jnp.concatenate([up[:, :LANES] for up in u], axis=1)
    second = jnp.concatenate([up[:, LANES:] for up in u], axis=1)
    pe = jnp.dot(pe_ref[0], w1_ref[0], preferred_element_type=f32)
    pe = jnp.concatenate([pe[0:1, :LANES] + pe[1:2, LANES:]] * pairs, axis=1)
    nc = first.shape[0]
    hid = first + pltpu.roll(second, nc - 1, 0) + pe
    out = jnp.dot(_gelu_tanh(hid).astype(MXU_DTYPE), w2_ref[0], preferred_element_type=f32)
    is_key = pl.program_id(1) == 0
    for kv in range(KV_HEADS):
        lanes = slice(kv * HEAD_DIM, (kv + 1) * HEAD_DIM)
        v = out[:, lanes]
        vn = v * lax.rsqrt(jnp.mean(v * v, axis=-1, keepdims=True) + EPS) * kg_ref[...]
        o_ref[0, 0, :, lanes] = jnp.where(is_key, vn, v)


def _compress(c, pe, w1, w2, k_gain0):
    n, nc = c.shape[1:3]
    cd = MXU_DTYPE
    kc = CMP_STRIDE * SLOT
    eye = jnp.eye(KV_HEADS, dtype=jnp.float32)
    per_tile = LANES // HEAD_DIM
    kp = CMP_STRIDE * LANES
    w1h = w1.reshape(2, 2, CMP_STRIDE, HEAD_DIM, CMP_HID)
    w1b = jnp.einsum('shldf,kj->slkdhjf', w1h, eye[:per_tile, :per_tile]).reshape(2, kp, 2 * LANES).astype(cd)
    w2b = jnp.einsum('sdf,kj->skdjf', w2, eye).reshape(2, SLOT, SLOT).astype(cd)
    peh = jnp.broadcast_to(pe.reshape(2, 2, CMP_STRIDE, 1, HEAD_DIM), (2, 2, CMP_STRIDE, per_tile, HEAD_DIM))
    peh = jnp.pad(peh.reshape(2, 2, kp), ((0, 0), (0, 14), (0, 0))).astype(cd)
    return pl.pallas_call(
        _compress_kernel,
        grid=(n, 2),
        in_specs=[pl.BlockSpec((1, 1, nc, kc), lambda b, s: (s, b, 0, 0)),
                  pl.BlockSpec((1, kp, 2 * LANES), lambda b, s: (s, 0, 0)),
                  pl.BlockSpec((1, 16, kp), lambda b, s: (s, 0, 0)),
                  pl.BlockSpec((1, SLOT, SLOT), lambda b, s: (s, 0, 0)),
                  pl.BlockSpec((1, HEAD_DIM), lambda b, s: (0, 0))],
        out_specs=pl.BlockSpec((1, 1, nc, SLOT), lambda b, s: (s, b, 0, 0)),
        out_shape=jax.ShapeDtypeStruct((2, n, nc, SLOT), jnp.float32),
        compiler_params=pltpu.CompilerParams(dimension_semantics=("arbitrary", "arbitrary"),
                                             vmem_limit_bytes=VMEM_LIMIT_BYTES),
        name="nsa_compress",
    )(c, w1b, peh, w2b, k_gain0.reshape(1, HEAD_DIM))


def _proj_residual_kernel(x_ref, a_ref, w_ref, o_ref):
    o_ref[...] = x_ref[...] + jnp.dot(a_ref[...].astype(MXU_DTYPE), w_ref[...],
                                      preferred_element_type=jnp.float32)


def _proj_residual(x, a, w):
    T, D = x.shape
    k = a.shape[1]
    tb = min(512, T)
    assert T % tb == 0
    return pl.pallas_call(
        _proj_residual_kernel,
        grid=(T // tb,),
        in_specs=[pl.BlockSpec((tb, D), lambda i: (i, 0)), pl.BlockSpec((tb, k), lambda i: (i, 0)),
                  pl.BlockSpec((k, D), lambda i: (0, 0))],
        out_specs=pl.BlockSpec((tb, D), lambda i: (i, 0)),
        out_shape=jax.ShapeDtypeStruct((T, D), jnp.float32),
        compiler_params=pltpu.CompilerParams(dimension_semantics=("arbitrary",)),
        name="proj_residual",
    )(x, a, w.astype(MXU_DTYPE))


MOE_ROWS = 256
ROUTE_COLS = LANES
ROW_DMA_UNROLL = 8


def _moe_route_kernel(x_ref, g_ref, w_ref, b_ref, tri_ref, h_ref, meta_ref, wt_ref, cnt_ref, carry_sc):
    f32 = jnp.float32
    step = pl.program_id(0)

    @pl.when(step == 0)
    def _():
        carry_sc[...] = jnp.zeros(carry_sc.shape, f32)

    x = x_ref[...]
    h = x * lax.rsqrt(jnp.mean(x * x, axis=-1, keepdims=True) + EPS) * g_ref[...]
    h_ref[...] = h
    logits = jnp.dot(h.astype(MXU_DTYPE), w_ref[...], preferred_element_type=f32) + b_ref[...]
    tb = logits.shape[0]
    col = lax.broadcasted_iota(jnp.int32, (tb, ROUTE_COLS), 1)

    def first_max(vals):
        best = jnp.max(vals, axis=1, keepdims=True)
        return best, jnp.min(jnp.where(vals == best, col, ROUTE_COLS), axis=1, keepdims=True)

    lg = jnp.where(col < N_GROUPS, logits, -jnp.inf)
    g_best, grp = first_max(lg)
    g_w = 1.0 / jnp.sum(jnp.exp(lg - g_best), axis=1, keepdims=True)
    lo = N_GROUPS + EXPERTS_PER_GROUP * grp
    le = jnp.where((col >= lo) & (col < lo + EXPERTS_PER_GROUP), logits, -jnp.inf)
    v0, c0 = first_max(le)
    v1, c1 = first_max(jnp.where(col == c0, -jnp.inf, le))
    e1 = jnp.exp(v1 - v0)
    w0 = g_w / (1.0 + e1)
    w1 = g_w * e1 / (1.0 + e1)
    chosen = (col == c0) | (col == c1)
    before = jnp.dot(tri_ref[...], jnp.where(chosen, 1.0, 0.0).astype(MXU_DTYPE),
                     preferred_element_type=f32) + carry_sc[...]
    r0 = jnp.sum(jnp.where(col == c0, before, 0.0), axis=1, keepdims=True).astype(jnp.int32)
    r1 = jnp.sum(jnp.where(col == c1, before, 0.0), axis=1, keepdims=True).astype(jnp.int32)
    carry_sc[...] += jnp.sum(jnp.where(chosen, 1.0, 0.0), axis=0, keepdims=True)
    cnt_ref[...] = carry_sc[...]
    meta_ref[...] = jnp.where(col == 0, c0 - N_GROUPS, jnp.where(col == 1, c1 - N_GROUPS,
                              jnp.where(col == 2, r0, jnp.where(col == 3, r1, 0))))
    wt_ref[...] = jnp.where(col == 0, w0, jnp.where(col == 1, w1, 0.0))


def _moe_dispatch_kernel(dest_ref, h_ref, xbuf_in, xbuf_ref, sem):
    del xbuf_in
    tb = h_ref.shape[0]

    def row_copy(r, k):
        return pltpu.make_async_copy(h_ref.at[pl.ds(r, 1)], xbuf_ref.at[pl.ds(dest_ref[0, 0, 2 * r + k], 1)], sem)

    def start(r, c):
        row_copy(r, 0).start()
        row_copy(r, 1).start()
        return c

    def wait(r, c):
        row_copy(r, 0).wait()
        row_copy(r, 1).wait()
        return c

    lax.fori_loop(0, tb, start, 0, unroll=ROW_DMA_UNROLL)
    lax.fori_loop(0, tb, wait, 0, unroll=ROW_DMA_UNROLL)


def _moe_expert_kernel(blk_e_ref, nblk_ref, x_ref, wgu_ref, wdn_ref, y_ref, wgu_sc, wdn_sc):
    i = pl.program_id(0)
    f32 = jnp.float32

    @pl.when(i < nblk_ref[0])
    def _():
        changed = jnp.logical_or(i == 0, blk_e_ref[i] != blk_e_ref[jnp.maximum(i - 1, 0)])

        @pl.when(changed)
        def _():
            wgu_sc[...] = wgu_ref[0, 0].astype(MXU_DTYPE)
            wdn_sc[...] = wdn_ref[0, 0].astype(MXU_DTYPE)

        gu = jnp.dot(x_ref[...].astype(MXU_DTYPE), wgu_sc[...], preferred_element_type=f32)
        gate = gu[:, :D_EXPERT]
        act = gate * (1.0 / (1.0 + jnp.exp(-gate))) * gu[:, D_EXPERT:]
        y_ref[...] = jnp.dot(act.astype(MXU_DTYPE), wdn_sc[...], preferred_element_type=f32)

    @pl.when(i >= nblk_ref[0])
    def _():
        y_ref[...] = jnp.zeros(y_ref.shape, f32)


def _moe_combine_kernel(dest_ref, x_ref, wt_ref, ybuf_ref, o_ref, rows_sc, sem):
    tb = x_ref.shape[0]

    def row_copy(r, k):
        return pltpu.make_async_copy(ybuf_ref.at[pl.ds(dest_ref[0, 0, 2 * r + k], 1)],
                                     rows_sc.at[k, pl.ds(r, 1)], sem)

    def start(r, c):
        row_copy(r, 0).start()
        row_copy(r, 1).start()
        return c

    def wait(r, c):
        row_copy(r, 0).wait()
        row_copy(r, 1).wait()
        return c

    lax.fori_loop(0, tb, start, 0, unroll=ROW_DMA_UNROLL)
    lax.fori_loop(0, tb, wait, 0, unroll=ROW_DMA_UNROLL)
    wt = wt_ref[...]
    o_ref[...] = x_ref[...] + (wt[:, 0:1] * rows_sc[0] + wt[:, 1:2] * rows_sc[1])


def _hier_moe_residual(x, ln_g, w_grp, b_grp, w_exp, b_exp, w_gu, w_dn, layer):
    T, D = x.shape
    f32 = jnp.float32
    tb = min(256, T)
    assert T % tb == 0
    nt = T // tb
    pad_cols = ROUTE_COLS - N_GROUPS - N_EXPERTS
    w_r = jnp.pad(jnp.concatenate([w_grp, w_exp], axis=1), ((0, 0), (0, pad_cols))).astype(MXU_DTYPE)
    b_r = jnp.pad(jnp.concatenate([b_grp, b_exp]), (0, pad_cols)).reshape(1, ROUTE_COLS)
    tri = jnp.asarray(np.tril(np.ones((tb, tb), np.float32), -1), dtype=MXU_DTYPE)
    row = lambda i: (i, 0)
    fixed = lambda i: (0, 0)
    h, meta, wt, cnt = pl.pallas_call(
        _moe_route_kernel,
        grid=(nt,),
        in_specs=[pl.BlockSpec((tb, D), row), pl.BlockSpec((1, D), fixed),
                  pl.BlockSpec((D, ROUTE_COLS), fixed), pl.BlockSpec((1, ROUTE_COLS), fixed),
                  pl.BlockSpec((tb, tb), fixed)],
        out_specs=[pl.BlockSpec((tb, D), row), pl.BlockSpec((tb, ROUTE_COLS), row),
                   pl.BlockSpec((tb, ROUTE_COLS), row), pl.BlockSpec((1, ROUTE_COLS), fixed)],
        out_shape=[jax.ShapeDtypeStruct((T, D), f32), jax.ShapeDtypeStruct((T, ROUTE_COLS), jnp.int32),
                   jax.ShapeDtypeStruct((T, ROUTE_COLS), f32), jax.ShapeDtypeStruct((1, ROUTE_COLS), f32)],
        scratch_shapes=[pltpu.VMEM((1, ROUTE_COLS), f32)],
        compiler_params=pltpu.CompilerParams(dimension_semantics=("arbitrary",)),
        name="moe_route",
    )(x, ln_g.reshape(1, D), w_r, b_r, tri)

    counts = cnt[0, N_GROUPS:N_GROUPS + N_EXPERTS].astype(jnp.int32)
    padded = (counts + MOE_ROWS - 1) // MOE_ROWS * MOE_ROWS
    pad_end = jnp.cumsum(padded)
    pad_start = pad_end - padded
    n_blocks = -(-(T * TOP_K) // MOE_ROWS) + N_EXPERTS
    n_slots = n_blocks * MOE_ROWS
    dest = pad_start[meta[:, 0:2]] + meta[:, 2:4]
    dest = dest.reshape(nt, 1, 2 * tb)
    starts = jnp.arange(n_blocks, dtype=jnp.int32) * MOE_ROWS
    used = (pad_end[-1] // MOE_ROWS).astype(jnp.int32).reshape(1)
    blk_start = jnp.minimum(starts, pad_end[-1] - 1)
    blk_e = jnp.minimum(jnp.sum(pad_end[None, :] <= blk_start[:, None], axis=1), N_EXPERTS - 1).astype(jnp.int32)

    dest_spec = pl.BlockSpec((1, 1, 2 * tb), lambda i: (i, 0, 0), memory_space=pltpu.SMEM)
    xbuf = pl.pallas_call(
        _moe_dispatch_kernel,
        grid=(nt,),
        in_specs=[dest_spec, pl.BlockSpec((tb, D), row), pl.BlockSpec(memory_space=pl.ANY)],
        out_specs=pl.BlockSpec(memory_space=pl.ANY),
        out_shape=jax.ShapeDtypeStruct((n_slots, D), f32),
        scratch_shapes=[pltpu.SemaphoreType.DMA(())],
        input_output_aliases={2: 0},
        compiler_params=pltpu.CompilerParams(dimension_semantics=("arbitrary",)),
        name="moe_dispatch",
    )(dest, h, jnp.zeros((n_slots, D), f32))

    ybuf = pl.pallas_call(
        _moe_expert_kernel,
        grid_spec=pltpu.PrefetchScalarGridSpec(
            num_scalar_prefetch=2,
            grid=(n_blocks,),
            in_specs=[pl.BlockSpec((MOE_ROWS, D), lambda i, be, nb: (i, 0)),
                      pl.BlockSpec((1, 1, D, 2 * D_EXPERT), lambda i, be, nb: (layer, be[i], 0, 0)),
                      pl.BlockSpec((1, 1, D_EXPERT, D), lambda i, be, nb: (layer, be[i], 0, 0))],
            out_specs=pl.BlockSpec((MOE_ROWS, D), lambda i, be, nb: (i, 0)),
            scratch_shapes=[pltpu.VMEM((D, 2 * D_EXPERT), MXU_DTYPE), pltpu.VMEM((D_EXPERT, D), MXU_DTYPE)]),
        out_shape=jax.ShapeDtypeStruct((n_slots, D), f32),
        compiler_params=pltpu.CompilerParams(dimension_semantics=("arbitrary",),
                                             vmem_limit_bytes=VMEM_LIMIT_BYTES),
        name="moe_experts",
    )(blk_e, used, xbuf, w_gu, w_dn)

    return pl.pallas_call(
        _moe_combine_kernel,
        grid=(nt,),
        in_specs=[dest_spec, pl.BlockSpec((tb, D), row), pl.BlockSpec((tb, ROUTE_COLS), row),
                  pl.BlockSpec(memory_space=pl.ANY)],
        out_specs=pl.BlockSpec((tb, D), row),
        out_shape=jax.ShapeDtypeStruct((T, D), f32),
        scratch_shapes=[pltpu.VMEM((TOP_K, tb, D), f32), pltpu.SemaphoreType.DMA(())],
        compiler_params=pltpu.CompilerParams(dimension_semantics=("arbitrary",)),
        name="moe_combine",
    )(dest, x, wt, ybuf)


def _gelu_tanh(x):
    return 0.5 * x * (1.0 + jnp.tanh(math.sqrt(2.0 / math.pi) * (x + 0.044715 * (x * x * x))))


def _gmlp_kernel(x_ref, g_ref, win_ref, bin_ref, lng_ref, lnb_ref, ws_ref, bs_ref, wout_ref,
                 o_ref, v_ref, *, single_position):
    f32 = jnp.float32
    x = x_ref[...]
    rows = x.shape[0]
    h = x * lax.rsqrt(jnp.mean(x * x, axis=-1, keepdims=True) + EPS) * g_ref[...]
    z = _gelu_tanh(jnp.dot(h.astype(MXU_DTYPE), win_ref[...], preferred_element_type=f32) + bin_ref[...])
    u = z[:, :GM_WIDTH]
    v = z[:, GM_WIDTH:]
    mu = jnp.mean(v, axis=-1, keepdims=True)
    var = jnp.mean(jnp.square(v - mu), axis=-1, keepdims=True)
    v = (v - mu) * lax.rsqrt(var + EPS) * lng_ref[...] + lnb_ref[...]
    v_ref[0] = v
    vb = v.astype(MXU_DTYPE)
    if single_position:
        s = ws_ref[...].astype(f32) * vb.astype(f32) + bs_ref[...]
    else:
        parts = []
        for c in range(rows // CHUNK):
            vc = vb[c * CHUNK:(c + 1) * CHUNK]
            parts.append(jnp.concatenate(
                [jnp.dot(ws_ref[g], vc[:, g * GM_GROUP_DIM:(g + 1) * GM_GROUP_DIM], preferred_element_type=f32)
                 + bs_ref[g] for g in range(GM_GROUPS)], axis=1))
        s = jnp.concatenate(parts, axis=0)
    y = jnp.dot((u * s).astype(MXU_DTYPE), wout_ref[...], preferred_element_type=f32)
    o_ref[...] = x + y


def _gmlp_residual(x, ln_g, w_in, b_in, ln2_g, ln2_b, w_s, b_s, w_out, seq):
    T, D = x.shape
    f32 = jnp.float32
    cd = MXU_DTYPE
    single = seq == 1
    tb = T if single else 2 * CHUNK
    assert T % tb == 0 and (single or seq % tb == 0)
    steps_per_seq = 1 if single else seq // tb
    if single:
        ws = jnp.repeat(w_s[:, 0, 0], GM_GROUP_DIM).reshape(1, GM_WIDTH).astype(cd)
        bs = jnp.repeat(b_s[:, 0], GM_GROUP_DIM).reshape(1, GM_WIDTH)
        ws_spec = pl.BlockSpec((1, GM_WIDTH), lambda i: (0, 0))
        bs_spec = pl.BlockSpec((1, GM_WIDTH), lambda i: (0, 0))
    else:
        causal = np.tril(np.ones((CHUNK, CHUNK), bool))
        ws = jnp.where(causal[None], w_s, 0).astype(cd)
        bs = jnp.broadcast_to(b_s[:, :, None], (GM_GROUPS, CHUNK, GM_GROUP_DIM))
        ws_spec = pl.BlockSpec((GM_GROUPS, CHUNK, CHUNK), lambda i: (0, 0, 0))
        bs_spec = pl.BlockSpec((GM_GROUPS, CHUNK, GM_GROUP_DIM), lambda i: (0, 0, 0))
    fixed = lambda i: (0, 0)
    row = lambda i: (i, 0)
    return pl.pallas_call(
        functools.partial(_gmlp_kernel, single_position=single),
        grid=(T // tb,),
        in_specs=[pl.BlockSpec((tb, D), row), pl.BlockSpec((1, D), fixed),
                  pl.BlockSpec((D, 2 * GM_WIDTH), fixed), pl.BlockSpec((1, 2 * GM_WIDTH), fixed),
                  pl.BlockSpec((1, GM_WIDTH), fixed), pl.BlockSpec((1, GM_WIDTH), fixed),
                  ws_spec, bs_spec, pl.BlockSpec((GM_WIDTH, D), fixed)],
        out_specs=[pl.BlockSpec((tb, D), row), pl.BlockSpec((1, tb, GM_WIDTH), lambda i: (i // steps_per_seq, 0, 0))],
        out_shape=[jax.ShapeDtypeStruct((T, D), f32),
                   jax.ShapeDtypeStruct((T // (tb * steps_per_seq), tb, GM_WIDTH), f32)],
        compiler_params=pltpu.CompilerParams(dimension_semantics=("arbitrary",),
                                             vmem_limit_bytes=VMEM_LIMIT_BYTES),
        name="gmlp",
    )(x, ln_g.reshape(1, D), w_in.astype(cd), b_in.reshape(1, -1), ln2_g.reshape(1, -1), ln2_b.reshape(1, -1),
      ws, bs, w_out.astype(cd))


def kernel(x_prompt, x_sample, cache_nsa_kv, state_win_kv, page_table, rel_bias, ln_mix, ln_ffn,
           nsa_w_in, nsa_w_out, nsa_q_gain, nsa_k_gain, cmp_pe, cmp_w1, cmp_w2,
           gm_w_in, gm_b_in, gm_ln_g, gm_ln_b, gm_w_s, gm_b_s, gm_w_out,
           moe_w_grp, moe_b_grp, moe_w_exp, moe_b_exp, moe_w_gu, moe_w_dn):
    xp = x_prompt
    xs = x_sample
    kv_p, kv_s, win_p, win_s, gv_p, gv_s = [], [], [], [], [], []
    for i in range(DEPTH):
        a = i // N_MIXERS
        if i % N_MIXERS == 0:
            nsa = (ln_mix[i], nsa_w_in[a], nsa_w_out[a], nsa_q_gain[a], nsa_k_gain[a],
                   cmp_pe[a], cmp_w1[a], cmp_w2[a], rel_bias)
            xp, rp, wp = _nsa_prompt_layer(xp, *nsa)
            cmp_rows, sel_kvt = _page_gather(cache_nsa_kv, page_table, a)
            xs, rs, ws = _nsa_decode_layer(xs, cmp_rows, sel_kvt, state_win_kv[a], *nsa)
            kv_p.append(rp)
            kv_s.append(rs)
            win_p.append(wp)
            win_s.append(ws)
        else:
            gm = (ln_mix[i], gm_w_in[a], gm_b_in[a], gm_ln_g[a], gm_ln_b[a], gm_w_s[a], gm_b_s[a], gm_w_out[a])
            bp, sp = xp.shape[:2]
            bs_, ss = xs.shape[:2]
            xp2, vp = _gmlp_residual(xp.reshape(-1, D_MODEL), *gm, seq=sp)
            xs2, vs = _gmlp_residual(xs.reshape(-1, D_MODEL), *gm, seq=ss)
            xp = xp2.reshape(xp.shape)
            xs = xs2.reshape(xs.shape)
            start = ((sp - 1) // CHUNK) * CHUNK
            gv_p.append(vp[:, vp.shape[1] - (sp - start):])
            gv_s.append(vs.reshape(bs_, ss, GM_WIDTH))
        moe = (ln_ffn[i], moe_w_grp[i], moe_b_grp[i], moe_w_exp[i], moe_b_exp[i], moe_w_gu, moe_w_dn, i)
        xp = _hier_moe_residual(xp.reshape(-1, D_MODEL), *moe).reshape(xp.shape)
        xs = _hier_moe_residual(xs.reshape(-1, D_MODEL), *moe).reshape(xs.shape)
    new_kv_prompt = jnp.stack(kv_p, axis=2)
    new_kv_sample = jnp.stack(kv_s, axis=2)
    new_win_prompt = jnp.stack(win_p, axis=0)
    new_win_sample = jnp.stack(win_s, axis=0)
    new_gm_v_prompt = jnp.stack(gv_p, axis=0)
    new_gm_v_sample = jnp.stack(gv_s, axis=0)
    return (xp, xs, new_kv_prompt, new_kv_sample, new_win_prompt, new_win_sample, new_gm_v_prompt, new_gm_v_sample)
```

```python
import functools
import math

import jax
import jax.numpy as jnp
import numpy as np
from jax import lax
from jax.experimental import pallas as pl
from jax.experimental.pallas import tpu as pltpu

D_MODEL = 1024
PAGE_SIZE = 128
DEPTH = 2
N_MIXERS = 2
N_HEADS = 16
HEAD_DIM = 64
KV_HEADS = 4
Q_PER_KV = N_HEADS // KV_HEADS
CMP_LEN = 32
CMP_STRIDE = 16
CMP_HID = 64
SEL_BLOCK = 64
CMP_PER_SEL = SEL_BLOCK // CMP_STRIDE
N_SEL = 16
N_LOCAL = 2
WINDOW = 512
N_BUCKETS = 32
MAX_DISTANCE = 128
CHUNK = 128
GM_WIDTH = 2048
GM_GROUPS = 8
GM_GROUP_DIM = GM_WIDTH // GM_GROUPS
N_GROUPS = 4
EXPERTS_PER_GROUP = 8
N_EXPERTS = N_GROUPS * EXPERTS_PER_GROUP
TOP_K = 2
D_EXPERT = 512
EPS = 1e-6

LANES = 128
VMEM_LIMIT_BYTES = 56 * 1024 * 1024


NEG = -1e30
MXU_DTYPE = jnp.bfloat16
TQ = 256
TK = 256
GROUP_CHUNKS = 3
NEAR_CHUNKS = 5
BIAS_SPAN = 128
LOG2E = 1.4426950408889634


def _bucket_of_distance():
    d = np.arange(BIAS_SPAN)
    max_exact = N_BUCKETS // 2
    nf = np.maximum(d, 1).astype(np.float32)
    large = max_exact + (np.log(nf / np.float32(max_exact)) / np.float32(math.log(MAX_DISTANCE / max_exact))
                         * np.float32(N_BUCKETS - max_exact)).astype(np.int32)
    large = np.minimum(large, N_BUCKETS - 1)
    return np.where(d < max_exact, d, large).astype(np.int32)


def _attn_bias_tiles(rel_bias, seq):
    fd = rel_bias[_bucket_of_distance()].T
    far = fd[:, BIAS_SPAN - 1]
    fd = fd - far[:, None]

    def by_distance(d):
        return jnp.where(d >= 0, LOG2E * jnp.take(fd, np.clip(d, 0, BIAS_SPAN - 1), axis=1), NEG)

    def toeplitz(first_distance):
        c = np.arange(2 * TK)
        g = by_distance(first_distance - np.where(c < TK, c, c - 2 * TK))
        flat = jnp.tile(g, (1, TQ))[:, :TQ * (2 * TK - 1)]
        return flat.reshape(-1, TQ, 2 * TK - 1)[:, :, :TK]

    b0 = toeplitz(0)
    b1 = toeplitz(TQ)
    i = np.arange(TQ)[:, None]
    j = np.arange(TK)[None, :]
    w2 = np.where(2 * TQ + i - j <= WINDOW, 0.0, NEG).astype(np.float32)
    nc = seq // CMP_STRIDE
    span = -(-(BIAS_SPAN + TQ) // CMP_STRIDE)
    r = np.arange(nc - span, nc + span)[:, None]
    dc = np.arange(TQ)[None, :] - CMP_STRIDE * (r - nc) - (CMP_LEN - 1)
    n_heads = fd.shape[0]
    pat = jnp.concatenate([jnp.zeros((n_heads, nc - span, TQ), jnp.float32), by_distance(dc),
                           jnp.full((n_heads, nc - span, TQ), NEG, jnp.float32)], axis=1)
    return b0, b1, jnp.asarray(w2), pat


def _nsa_attn_kernel(q_ref, kc_ref, vct_ref, kat_ref, vsa_ref, kwt_ref, vwa_ref, g_ref,
                     b0_ref, b1_ref, w2_ref, pat_ref, o_ref, m_sc, acc_sc, imp_sc, qa_sc, p_sc, a_sc, *, nc, nb):
    f32 = jnp.float32
    cd = MXU_DTYPE
    qb = pl.program_id(2)

    def compress_and_select(ncu, all_forced):
        nbu = ncu // CMP_PER_SEL
        r0 = pl.multiple_of(nc - (TQ // CMP_STRIDE) * qb, TQ // CMP_STRIDE)
        t_row = qb * TQ + lax.broadcasted_iota(jnp.int32, (1, TQ), 1)
        has_cmp = t_row >= CMP_LEN - 1
        for h in range(Q_PER_KV):
            st = lax.dot_general(kc_ref[0, 0, 0:ncu, :], q_ref[0, h], (((1,), (1,)), ((), ())),
                                 preferred_element_type=f32)
            st = st + pat_ref[h, pl.ds(r0, ncu), :]
            e = jnp.exp2(st - jnp.max(st, axis=0, keepdims=True))
            inv = jnp.where(has_cmp, 1.0 / jnp.sum(e, axis=0, keepdims=True), 0.0)
            pt = e * inv
            for c in range(TQ // LANES):
                part = pt[:, c * LANES:(c + 1) * LANES]
                if h == 0:
                    imp_sc[c, 0:ncu, :] = part
                else:
                    imp_sc[c, 0:ncu, :] += part
            oct_h = jnp.dot(vct_ref[0, 0, :, 0:ncu], pt.astype(cd), preferred_element_type=f32)
            o_ref[0, :, h * HEAD_DIM:(h + 1) * HEAD_DIM] = g_ref[0][:, 3 * h:3 * h + 1] * oct_h.T

        imp = jnp.concatenate(
            [sum(imp_sc[c, pl.ds(r, nbu, stride=CMP_PER_SEL), :] for r in range(CMP_PER_SEL))
             for c in range(TQ // LANES)], axis=1)
        blk = lax.broadcasted_iota(jnp.int32, (nbu, TQ), 0)
        t_blk = (qb * TQ + lax.broadcasted_iota(jnp.int32, (nbu, TQ), 1)) // SEL_BLOCK
        dd = t_blk - blk
        forced = (blk == 0) | ((dd >= 0) & (dd < N_LOCAL))
        if all_forced:
            score = jnp.where(forced | (dd < 0), -jnp.inf, imp)
            pen_t = jnp.where(forced, 0.0, NEG)
            rounds = N_SEL - (N_LOCAL + 1)
        else:
            score = jnp.where(forced, jnp.inf, jnp.where(dd >= 0, imp, -jnp.inf))
            pen_t = jnp.full((nbu, TQ), NEG, f32)
            rounds = min(N_SEL, nbu)
        for _ in range(rounds):
            best = jnp.max(score, axis=0, keepdims=True)
            first = jnp.min(jnp.where(score == best, blk, nbu), axis=0, keepdims=True)
            hit = blk == first
            pen_t = jnp.where(hit, 0.0, pen_t)
            score = jnp.where(hit, -jnp.inf, score)
        if nbu < nb:
            pen_t = jnp.concatenate([pen_t, jnp.full((nb - nbu, TQ), NEG, f32)], axis=0)
        pen = pen_t.T.astype(cd)
        pad = jnp.zeros((TQ, qa_sc.shape[2] - nb - HEAD_DIM), cd)
        for h in range(Q_PER_KV):
            qa_sc[h] = jnp.concatenate([pen, q_ref[0, h], pad], axis=1)

    n_tiles = nc * CMP_STRIDE // TQ
    n_var = 4 if (n_tiles % 4 == 0 and (nc // 4) % (2 * CMP_STRIDE) == 0) else 1
    for v in range(n_var):
        lo, hi = v * n_tiles // n_var, (v + 1) * n_tiles // n_var
        ncu = (v + 1) * nc // n_var
        if v == 0:
            @pl.when(qb == 0)
            def _():
                compress_and_select(ncu, all_forced=False)
            lo = 1

        @pl.when((qb >= lo) & (qb < hi))
        def _():
            compress_and_select(ncu, all_forced=TQ >= N_LOCAL * SEL_BLOCK)


    m_sc[...] = jnp.full(m_sc.shape, NEG, f32)
    acc_sc[...] = jnp.zeros(acc_sc.shape, f32)

    def run(*work):
        jobs = [(br, c, bias) for br, chunks in work for c, bias in chunks]
        for slot, ((b, lhs, k_ref, _), c, bias) in enumerate(jobs):
            for h in range(Q_PER_KV):
                s = jnp.dot(lhs(h), k_ref[0, 0, c], preferred_element_type=f32)
                if bias is not None:
                    s = s + bias(h)
                m_prev = m_sc[b, h]
                m_new = jnp.maximum(m_prev, jnp.max(s, axis=1, keepdims=True))
                p_sc[slot, h] = jnp.exp2(s - jnp.concatenate([m_new] * (TK // LANES), axis=1)).astype(cd)
                a_sc[slot, h] = jnp.exp2(m_prev - m_new)
                m_sc[b, h] = m_new
        for slot, ((b, _, _, v_ref), c, _) in enumerate(jobs):
            for h in range(Q_PER_KV):
                acc_sc[b, h] = a_sc[slot, h] * acc_sc[b, h] + jnp.dot(p_sc[slot, h], v_ref[0, 0, c],
                                                                      preferred_element_type=f32)

    def emit(b, gate_col):
        for h in range(Q_PER_KV):
            acc = acc_sc[b, h]
            w = g_ref[0][:, 3 * h + gate_col:3 * h + gate_col + 1] / acc[:, HEAD_DIM:HEAD_DIM + 1]
            o_ref[0, :, h * HEAD_DIM:(h + 1) * HEAD_DIM] += w * acc[:, :HEAD_DIM]

    own = lambda h: b0_ref[h]
    prev = lambda h: b1_ref[h]
    edge = lambda h: w2_ref[...]

    sel = (0, lambda h: qa_sc[h], kat_ref, vsa_ref)
    win = (1, lambda h: q_ref[0, h], kwt_ref, vwa_ref)
    n_far = jnp.maximum(qb - 1, 0)

    def far_group(i, carry):
        run((sel, [(GROUP_CHUNKS * i + j, None) for j in range(GROUP_CHUNKS)]))
        return carry

    lax.fori_loop(0, n_far // GROUP_CHUNKS, far_group, 0)
    for left in range(1, GROUP_CHUNKS):
        @pl.when(n_far % GROUP_CHUNKS == left)
        def _():
            run((sel, [(n_far - left + j, None) for j in range(left)]))

    @pl.when(qb >= 2)
    def _():
        run((sel, [(qb - 1, prev), (qb, own)]), (win, [(qb - 2, edge), (qb - 1, prev), (qb, own)]))

    @pl.when(qb == 1)
    def _():
        run((sel, [(qb - 1, prev), (qb, own)]), (win, [(qb - 1, prev), (qb, own)]))

    @pl.when(qb == 0)
    def _():
        run((sel, [(qb, own)]), (win, [(qb, own)]))

    emit(0, 1)
    emit(1, 2)


def _nsa_attention(qs, kcp, vct, kat, vsa, kwt, vwa, gt, rel_bias):
    B, _, S, _ = qs.shape
    assert S % TQ == 0 and TQ == TK and WINDOW == 2 * TQ and S // SEL_BLOCK >= N_SEL
    nc = S // CMP_STRIDE
    nb = S // SEL_BLOCK
    nch = S // TK
    ka = kat.shape[3]
    b0, b1, w2, pat = _attn_bias_tiles(rel_bias, S)
    G = Q_PER_KV
    kv_chunks = pl.BlockSpec((1, 1, nch, HEAD_DIM, TK), lambda b, k, i: (b, k, 0, 0, 0))
    ka_chunks = pl.BlockSpec((1, 1, nch, ka, TK), lambda b, k, i: (b, k, 0, 0, 0))
    v_chunks = pl.BlockSpec((1, 1, nch, TK, LANES), lambda b, k, i: (b, k, 0, 0, 0))
    return pl.pallas_call(
        functools.partial(_nsa_attn_kernel, nc=nc, nb=nb),
        grid=(B, KV_HEADS, S // TQ),
        in_specs=[
            pl.BlockSpec((1, G, TQ, HEAD_DIM), lambda b, k, i: (b, k, i, 0)),
            pl.BlockSpec((1, 1, nc, HEAD_DIM), lambda b, k, i: (b, k, 0, 0)),
            pl.BlockSpec((1, 1, HEAD_DIM, nc), lambda b, k, i: (b, k, 0, 0)),
            ka_chunks, v_chunks, kv_chunks, v_chunks,
            pl.BlockSpec((1, TQ, LANES), lambda b, k, i: (b, i, k)),
            pl.BlockSpec((G, TQ, TK), lambda b, k, i: (k, 0, 0)),
            pl.BlockSpec((G, TQ, TK), lambda b, k, i: (k, 0, 0)),
            pl.BlockSpec((TQ, TK), lambda b, k, i: (0, 0)),
            pl.BlockSpec((G, 2 * nc, TQ), lambda b, k, i: (k, 0, 0)),
        ],
        out_specs=pl.BlockSpec((1, TQ, G * HEAD_DIM), lambda b, k, i: (b, i, k)),
        out_shape=jax.ShapeDtypeStruct((B, S, N_HEADS * HEAD_DIM), jnp.float32),
        scratch_shapes=[
            pltpu.VMEM((2, G, TQ, LANES), jnp.float32),
            pltpu.VMEM((2, G, TQ, LANES), jnp.float32),
            pltpu.VMEM((TQ // LANES, nc, LANES), jnp.float32),
            pltpu.VMEM((G, TQ, ka), MXU_DTYPE),
            pltpu.VMEM((NEAR_CHUNKS, G, TQ, TK), MXU_DTYPE),
            pltpu.VMEM((NEAR_CHUNKS, G, TQ, LANES), jnp.float32),
        ],
        compiler_params=pltpu.CompilerParams(
            dimension_semantics=("arbitrary", "arbitrary", "arbitrary"),
            vmem_limit_bytes=VMEM_LIMIT_BYTES),
        name="nsa_attn",
    )(qs, kcp, vct, kat, vsa, kwt, vwa, gt, b0, b1, w2, pat)


def _nsa_prompt_layer(x, ln_g, w_in, w_out, q_gain, k_gain, pe, w1, w2, rel_bias):
    B, S, D = x.shape
    x2 = x.reshape(B * S, D)
    qs, rows, win, kat, vsa, kwt, vwa, c, gates = _nsa_project(x2, ln_g, w_in, q_gain, k_gain, B, S)
    nc = S // CMP_STRIDE
    cmp = _compress(c.reshape(2, B, nc, CMP_STRIDE * SLOT), pe, w1, w2, k_gain[0])
    cmp = cmp.reshape(2, B, nc, KV_HEADS, HEAD_DIM).astype(MXU_DTYPE)
    gt = gates.reshape(B, S, KV_HEADS * LANES)
    o = _nsa_attention(qs, cmp[0].transpose(0, 2, 1, 3), cmp[1].transpose(0, 2, 3, 1), kat, vsa, kwt, vwa, gt,
                       rel_bias)
    y = _proj_residual(x2, o.reshape(B * S, HQ), w_out)
    return (y.reshape(B, S, D), rows.reshape(B, 4, KV_HEADS, HEAD_DIM, S).transpose(0, 4, 1, 2, 3),
            win.reshape(B, WINDOW, 2, KV_HEADS, HEAD_DIM))


PAGES_PER_STEP = 8


def _page_gather_kernel(pt_ref, *refs):
    del pt_ref
    pages = refs[:PAGES_PER_STEP]
    c_ref, kvt_ref, rows_sc = refs[PAGES_PER_STEP:]
    for p, pg in enumerate(pages):
        pos = slice(p * PAGE_SIZE, (p + 1) * PAGE_SIZE)
        for slot in range(2):
            for pair in range(KV_HEADS // 2):
                tile = pg[0, 0, slot, 2 * pair:2 * pair + 2].reshape(LANES, PAGE_SIZE)
                rows_sc[slot * 2 + pair, pos, :] = tile.T
        for slot in range(2):
            kvt_ref[slot, 0, :, :, pos] = pg[0, 0, 2 + slot].astype(MXU_DTYPE)
    n_chunks = PAGES_PER_STEP * PAGE_SIZE // CMP_STRIDE
    for l in range(CMP_STRIDE):
        for t in range(4):
            lo = l * SLOT + (t % 2) * LANES
            c_ref[t // 2, 0, :, lo:lo + LANES] = rows_sc[t, pl.ds(l, n_chunks, stride=CMP_STRIDE), :].astype(MXU_DTYPE)


def _page_gather(cache, page_table, layer):
    nseq, n_pages = page_table.shape
    assert n_pages % PAGES_PER_STEP == 0 and 2 * LANES == SLOT
    P = n_pages * PAGE_SIZE
    cache_t = cache.transpose(0, 2, 3, 4, 5, 1)
    page_specs = [
        pl.BlockSpec((1, 1, 4, KV_HEADS, HEAD_DIM, PAGE_SIZE),
                     functools.partial(lambda r, n, j, pt: (pt[n, PAGES_PER_STEP * j + r], layer, 0, 0, 0, 0), r))
        for r in range(PAGES_PER_STEP)]
    rows_per_step = PAGES_PER_STEP * PAGE_SIZE
    return pl.pallas_call(
        _page_gather_kernel,
        grid_spec=pltpu.PrefetchScalarGridSpec(
            num_scalar_prefetch=1,
            grid=(nseq, n_pages // PAGES_PER_STEP),
            in_specs=page_specs,
            out_specs=[pl.BlockSpec((2, 1, rows_per_step // CMP_STRIDE, CMP_STRIDE * SLOT),
                                    lambda n, j, pt: (0, n, j, 0)),
                       pl.BlockSpec((2, 1, KV_HEADS, HEAD_DIM, rows_per_step), lambda n, j, pt: (0, n, 0, 0, j))],
            scratch_shapes=[pltpu.VMEM((4, rows_per_step, LANES), jnp.float32)]),
        out_shape=[jax.ShapeDtypeStruct((2, nseq, P // CMP_STRIDE, CMP_STRIDE * SLOT), MXU_DTYPE),
                   jax.ShapeDtypeStruct((2, nseq, KV_HEADS, HEAD_DIM, P), MXU_DTYPE)],
        compiler_params=pltpu.CompilerParams(dimension_semantics=("arbitrary", "arbitrary")),
        name="page_gather",
    )(page_table, *([cache_t] * PAGES_PER_STEP))


def _split3(x):
    hi = x.astype(MXU_DTYPE)
    r1 = x - hi.astype(jnp.float32)
    mid = r1.astype(MXU_DTYPE)
    return hi, mid, (r1 - mid.astype(jnp.float32)).astype(MXU_DTYPE)


def _nsa_decode_kernel(q_ref, qbd_ref, kc_ref, vc_ref, kst_ref, vst_ref, win_ref, rows_ref, wnew_ref, g_ref,
                       oh_ref, grp_ref, bc_ref, bs_ref, bw_ref, bn_ref, o_ref, nwin_ref, pen_sc, *, nb):
    f32 = jnp.float32
    cd = MXU_DTYPE
    nt = (((1,), (1,)), ((), ()))
    wb = win_ref.shape[-1]
    rows = rows_ref[0]
    wnew = wnew_ref[0]

    def softmax_with_new(s, s_new):
        m = jnp.maximum(jnp.max(s, axis=1, keepdims=True), s_new)
        e = jnp.exp(s - m)
        e_new = jnp.exp(s_new - m)
        inv = 1.0 / (jnp.sum(e, axis=1, keepdims=True) + e_new)
        return e * inv, e_new * inv

    def bf(x):
        return x.astype(cd).astype(f32)

    for k in range(KV_HEADS):
        hs = slice(k * Q_PER_KV, (k + 1) * Q_PER_KV)
        lanes = slice(k * HEAD_DIM, (k + 1) * HEAD_DIM)
        q = q_ref[0, hs, :]
        gates = g_ref[0, hs, :]
        sc = (lax.dot_general(q, kc_ref[0, 0][:, lanes].astype(cd), nt, preferred_element_type=f32)
              + bc_ref[hs, :])
        ec = jnp.exp(sc - jnp.max(sc, axis=1, keepdims=True))
        pc = ec / jnp.sum(ec, axis=1, keepdims=True)
        o_ref[0, hs, :] = gates[:, 0:1] * jnp.dot(pc.astype(cd), vc_ref[0, 0][:, lanes].astype(cd),
                                                  preferred_element_type=f32)

        imp = sum(jnp.dot(part, grp_ref[...], preferred_element_type=f32) for part in _split3(pc))
        imp = jnp.sum(imp, axis=0, keepdims=True)
        blk = lax.broadcasted_iota(jnp.int32, imp.shape, 1)
        forced = (blk == 0) | (blk == nb - 1)
        score = jnp.where(forced, jnp.inf, imp)
        pen = jnp.full(imp.shape, NEG, f32)
        for _ in range(N_SEL - 1):
            best = jnp.max(score, axis=1, keepdims=True)
            first = jnp.min(jnp.where(score == best, blk, nb), axis=1, keepdims=True)
            hit = blk == first
            pen = jnp.where(hit, 0.0, pen)
            score = jnp.where(hit, -jnp.inf, score)
        pen_sc[hs, :] = jnp.broadcast_to(pen, (Q_PER_KV, nb))

    n_keys = kst_ref.shape[-1]
    qbd = qbd_ref[0]
    key_pen = jnp.dot(pen_sc[...].astype(cd), oh_ref[...], preferred_element_type=f32)
    ss = (jnp.dot(qbd, kst_ref[0, 0].reshape(SLOT, n_keys), preferred_element_type=f32)
          + key_pen + bs_ref[...])
    s_new = (jnp.sum(qbd.astype(f32) * bf(rows[:, 2 * SLOT:3 * SLOT]), axis=1, keepdims=True) + bn_ref[...])
    p, p_new = softmax_with_new(ss, s_new)
    sel_all = (lax.dot_general(p.astype(cd), vst_ref[0, 0].reshape(SLOT, n_keys), nt,
                               preferred_element_type=f32)
               + bf(p_new) * bf(rows[:, 3 * SLOT:4 * SLOT]))

    eye = (lax.broadcasted_iota(jnp.int32, (HEAD_DIM, HEAD_DIM), 0)
           == lax.broadcasted_iota(jnp.int32, (HEAD_DIM, HEAD_DIM), 1))
    last = lax.broadcasted_iota(jnp.int32, (HEAD_DIM, wb), 1) == wb - 1
    for k in range(KV_HEADS):
        hs = slice(k * Q_PER_KV, (k + 1) * Q_PER_KV)
        lanes = slice(k * HEAD_DIM, (k + 1) * HEAD_DIM)
        q = q_ref[0, hs, :]
        gates = g_ref[0, hs, :]
        kw_t = win_ref[0, 0, k]
        vw_t = win_ref[0, 1, k]
        k_new = wnew[:, :SLOT][:, lanes]
        v_new = wnew[:, SLOT:][:, lanes]
        sw = jnp.dot(q, kw_t.astype(cd), preferred_element_type=f32) + bw_ref[hs, :]
        w_new = jnp.sum(q.astype(f32) * bf(k_new), axis=1, keepdims=True) + bn_ref[hs, :]
        p, p_new = softmax_with_new(sw, w_new)
        acc = (lax.dot_general(p.astype(cd), vw_t.astype(cd), nt, preferred_element_type=f32)
               + bf(p_new) * bf(v_new))
        o_ref[0, hs, :] = o_ref[0, hs, :] + gates[:, 1:2] * sel_all[hs, :][:, lanes] + gates[:, 2:3] * acc
        for s, (tile, new_row) in enumerate(((kw_t, k_new), (vw_t, v_new))):
            col = jnp.sum(jnp.where(eye, jnp.broadcast_to(new_row, (HEAD_DIM, HEAD_DIM)), 0.0),
                          axis=1, keepdims=True)
            nwin_ref[0, s, k] = jnp.where(last, col, pltpu.roll(tile, wb - 1, 1))


def _nsa_decode_layer(x, cmp_rows, sel_kvt, win_buf, ln_g, w_in, w_out, q_gain, k_gain, pe, w1, w2, rel_bias):
    N, L, D = x.shape
    P = sel_kvt.shape[-1]
    wb = win_buf.shape[1]
    assert L == 1 and P % SEL_BLOCK == 0 and wb == WINDOW and P >= BIAS_SPAN and P // SEL_BLOCK >= N_SEL - 1
    f32 = jnp.float32
    cd = MXU_DTYPE
    nc = P // CMP_STRIDE
    nb = P // SEL_BLOCK
    x2 = x.reshape(N, D)
    q, rows, wnew, gates = _nsa_project(x2, ln_g, w_in, q_gain, k_gain, N, 1)
    cmp = _compress(cmp_rows, pe, w1, w2, k_gain[0])
    qs =(q.reshape(N, N_HEADS, HEAD_DIM) * HEAD_DIM ** -0.5).astype(cd)
    own = (np.arange(N_HEADS)[:, None] // Q_PER_KV == np.arange(SLOT)[None, :] // HEAD_DIM)
    qbd = jnp.where(own[None], jnp.tile(qs, (1, 1, KV_HEADS)), jnp.zeros((), cd))
    win = win_buf.transpose(0, 2, 3, 4, 1)
    onehot = jnp.asarray(np.arange(nb)[:, None] == (np.arange(P) // SEL_BLOCK)[None, :], dtype=cd)
    grp = jnp.asarray((np.arange(nc) // CMP_PER_SEL)[:, None] == np.arange(nb)[None, :], dtype=cd)
    fd = rel_bias[_bucket_of_distance()].T
    fd = fd - fd[:, BIAS_SPAN - 1:]
    dc = P - (np.arange(nc) * CMP_STRIDE + CMP_LEN - 1)
    b_cmp = jnp.where(dc >= 0, jnp.take(fd, np.clip(dc, 0, BIAS_SPAN - 1), axis=1), NEG)
    b_sel = jnp.take(fd, np.clip(P - np.arange(P), 0, BIAS_SPAN - 1), axis=1)
    b_win = jnp.take(fd, np.clip(wb - np.arange(wb), 0, BIAS_SPAN - 1), axis=1)
    b_new = fd[:, 0:1]
    seq3 = lambda n: (n, 0, 0)
    fixed = lambda n: (0, 0)
    win_spec = pl.BlockSpec((1, 2, KV_HEADS, HEAD_DIM, wb), lambda n: (n, 0, 0, 0, 0))
    o, new_win = pl.pallas_call(
        functools.partial(_nsa_decode_kernel, nb=nb),
        grid=(N,),
        in_specs=[pl.BlockSpec((1, N_HEADS, HEAD_DIM), seq3),
                  pl.BlockSpec((1, N_HEADS, SLOT), seq3),
                  pl.BlockSpec((1, 1, nc, SLOT), lambda n: (0, n, 0, 0)),
                  pl.BlockSpec((1, 1, nc, SLOT), lambda n: (1, n, 0, 0)),
                  pl.BlockSpec((1, 1, KV_HEADS, HEAD_DIM, P), lambda n: (0, n, 0, 0, 0)),
                  pl.BlockSpec((1, 1, KV_HEADS, HEAD_DIM, P), lambda n: (1, n, 0, 0, 0)),
                  win_spec,
                  pl.BlockSpec((1, 1, 4 * SLOT), seq3),
                  pl.BlockSpec((1, 1, 2 * SLOT), seq3),
                  pl.BlockSpec((1, N_HEADS, 3), seq3),
                  pl.BlockSpec((nb, P), fixed), pl.BlockSpec((nc, nb), fixed), pl.BlockSpec((N_HEADS, nc), fixed),
                  pl.BlockSpec((N_HEADS, P), fixed), pl.BlockSpec((N_HEADS, wb), fixed),
                  pl.BlockSpec((N_HEADS, 1), fixed)],
        out_specs=[pl.BlockSpec((1, N_HEADS, HEAD_DIM), seq3), win_spec],
        out_shape=[jax.ShapeDtypeStruct((N, N_HEADS, HEAD_DIM), f32), jax.ShapeDtypeStruct(win.shape, win.dtype)],
        scratch_shapes=[pltpu.VMEM((N_HEADS, nb), f32)],
        compiler_params=pltpu.CompilerParams(dimension_semantics=("arbitrary",),
                                             vmem_limit_bytes=VMEM_LIMIT_BYTES),
        name="nsa_decode_attn",
    )(qs, qbd, cmp, cmp, sel_kvt, sel_kvt, win, rows.reshape(N, 1, 4 * SLOT), wnew.reshape(N, 1, 2 * SLOT),
      gates[:, :3 * N_HEADS].reshape(N, N_HEADS, 3), onehot, grp, b_cmp, b_sel, b_win, b_new)
    y = _proj_residual(x2, o.reshape(N, HQ), w_out)
    new_win = new_win.transpose(0, 4, 1, 2, 3)
    return y.reshape(N, 1, D), rows.reshape(N, 1, 4, KV_HEADS, HEAD_DIM), new_win


HQ = N_HEADS * HEAD_DIM
SLOT = KV_HEADS * HEAD_DIM
NSA_COLS = HQ + 6 * SLOT + 3 * N_HEADS
NSA_COLS_PAD = -(-NSA_COLS // LANES) * LANES
GATE_COLS = NSA_COLS_PAD - HQ - 6 * SLOT


def _nsa_proj_kernel(x_ref, g_ref, w_ref, qg_ref, kg_ref, *outs, prompt, n_blocks):
    f32 = jnp.float32
    x = x_ref[...]
    xn = x * lax.rsqrt(jnp.mean(x * x, axis=-1, keepdims=True) + EPS) * g_ref[...]
    h = jnp.dot(xn.astype(MXU_DTYPE), w_ref[...], preferred_element_type=f32)

    def head_norm(v, gain):
        return v * lax.rsqrt(jnp.mean(v * v, axis=-1, keepdims=True) + EPS) * gain

    def slot(s, kv):
        lo = HQ + s * SLOT + kv * HEAD_DIM
        return h[:, lo:lo + HEAD_DIM]

    gates = 1.0 / (1.0 + jnp.exp(-h[:, HQ + 6 * SLOT:]))
    if prompt:
        q_ref, rows_ref, win_ref, kat_ref, vsa_ref, kwt_ref, vwa_ref, c_ref, gate_ref = outs
        for hh in range(N_HEADS):
            qn = head_norm(h[:, hh * HEAD_DIM:(hh + 1) * HEAD_DIM], qg_ref[...])
            q_ref[0, hh] = (qn * (HEAD_DIM ** -0.5 * LOG2E)).astype(MXU_DTYPE)
        c_ref[0, 0] = h[:, HQ:HQ + SLOT].astype(MXU_DTYPE)
        c_ref[1, 0] = h[:, HQ + SLOT:HQ + 2 * SLOT].astype(MXU_DTYPE)
        rows, ka = x.shape[0], kat_ref.shape[3]
        nb = n_blocks
        first_key = (pl.program_id(0) % (nb * SEL_BLOCK // rows)) * rows
        key_blk = (first_key + lax.broadcasted_iota(jnp.int32, (nb, rows), 1)) // SEL_BLOCK
        onehot = jnp.where(key_blk == lax.broadcasted_iota(jnp.int32, (nb, rows), 0), 1.0, 0.0).astype(MXU_DTYPE)
        kpad = jnp.zeros((ka - nb - HEAD_DIM, rows), MXU_DTYPE)
        ones = jnp.ones((rows, LANES - HEAD_DIM), MXU_DTYPE)
    else:
        q_ref, rows_ref, win_ref, gate_ref = outs
        for hh in range(N_HEADS):
            q_ref[:, hh * HEAD_DIM:(hh + 1) * HEAD_DIM] = head_norm(h[:, hh * HEAD_DIM:(hh + 1) * HEAD_DIM],
                                                                    qg_ref[...])
    if prompt:
        rows_ref[0, 0:2 * SLOT, :] = h[:, HQ:HQ + 2 * SLOT].T
        rows_ref[0, 3 * SLOT:4 * SLOT, :] = h[:, HQ + 3 * SLOT:HQ + 4 * SLOT].T
    else:
        rows_ref[:, 0:2 * SLOT] = h[:, HQ:HQ + 2 * SLOT]
        rows_ref[:, 3 * SLOT:4 * SLOT] = h[:, HQ + 3 * SLOT:HQ + 4 * SLOT]
    for kv in range(KV_HEADS):
        lanes = slice(kv * HEAD_DIM, (kv + 1) * HEAD_DIM)
        ksn = head_norm(slot(2, kv), kg_ref[1:2, :])
        kwn = head_norm(slot(4, kv), kg_ref[2:3, :])
        if prompt:
            ksn_t = ksn.T
            rows_ref[0, 2 * SLOT + kv * HEAD_DIM:2 * SLOT + (kv + 1) * HEAD_DIM, :] = ksn_t
            win_ref[0, 0, :, lanes] = kwn
            kat_ref[0, kv, 0] = jnp.concatenate([onehot, ksn_t.astype(MXU_DTYPE), kpad], axis=0)
            kwt_ref[0, kv, 0] = kwn.T.astype(MXU_DTYPE)
            vsa_ref[0, kv, 0] = jnp.concatenate([slot(3, kv).astype(MXU_DTYPE), ones], axis=1)
            vwa_ref[0, kv, 0] = jnp.concatenate([slot(5, kv).astype(MXU_DTYPE), ones], axis=1)
        else:
            rows_ref[:, 2 * SLOT + kv * HEAD_DIM:2 * SLOT + (kv + 1) * HEAD_DIM] = ksn
            win_ref[:, lanes] = kwn
    if prompt:
        win_ref[0, 0, :, SLOT:2 * SLOT] = h[:, HQ + 5 * SLOT:HQ + 6 * SLOT]
    else:
        win_ref[:, SLOT:2 * SLOT] = h[:, HQ + 5 * SLOT:HQ + 6 * SLOT]
    if prompt:
        gate_ref[...] = jnp.zeros(gate_ref.shape, f32)
        per_kv = 3 * Q_PER_KV
        for kv in range(KV_HEADS):
            gate_ref[:, kv * LANES:kv * LANES + per_kv] = gates[:, kv * per_kv:(kv + 1) * per_kv]
    else:
        gate_ref[...] = gates


def _nsa_project(x, ln_g, w_in, q_gain, k_gain, batch, seq):
    T, D = x.shape
    f32 = jnp.float32
    cd = MXU_DTYPE
    prompt = seq > 1
    tb = TK if prompt else T
    w = jnp.pad(w_in, ((0, 0), (0, NSA_COLS_PAD - NSA_COLS))).astype(cd)
    fixed = lambda i: (0, 0)
    row = lambda i: (i, 0)
    in_specs = [pl.BlockSpec((tb, D), row), pl.BlockSpec((1, D), fixed), pl.BlockSpec((D, NSA_COLS_PAD), fixed),
                pl.BlockSpec((1, HEAD_DIM), fixed), pl.BlockSpec((3, HEAD_DIM), fixed)]
    if prompt:
        assert seq % tb == 0 and WINDOW == 2 * tb
        n = seq // tb
        n_blocks = seq // SEL_BLOCK
        ka = -(-(n_blocks + HEAD_DIM) // LANES) * LANES
        chunk = lambda i: (i // n, 0, i % n, 0, 0)
        kt_spec = pl.BlockSpec((1, KV_HEADS, 1, HEAD_DIM, tb), chunk)
        ka_spec = pl.BlockSpec((1, KV_HEADS, 1, ka, tb), chunk)
        v_spec = pl.BlockSpec((1, KV_HEADS, 1, tb, LANES), chunk)
        kt_shape = jax.ShapeDtypeStruct((batch, KV_HEADS, n, HEAD_DIM, tb), cd)
        ka_shape = jax.ShapeDtypeStruct((batch, KV_HEADS, n, ka, tb), cd)
        v_shape = jax.ShapeDtypeStruct((batch, KV_HEADS, n, tb, LANES), cd)
        out_specs = [
            pl.BlockSpec((1, N_HEADS, tb, HEAD_DIM), lambda i: (i // n, 0, i % n, 0)),
            pl.BlockSpec((1, 4 * SLOT, tb), lambda i: (i // n, 0, i % n)),
            pl.BlockSpec((1, 1, tb, 2 * SLOT), lambda i: (i // n, jnp.maximum(i % n - (n - 2), 0), 0, 0)),
            ka_spec, v_spec, kt_spec, v_spec,
            pl.BlockSpec((2, 1, tb, SLOT), lambda i: (0, i // n, i % n, 0)),
            pl.BlockSpec((tb, KV_HEADS * LANES), row)]
        out_shape = [
            jax.ShapeDtypeStruct((batch, N_HEADS, seq, HEAD_DIM), cd),
            jax.ShapeDtypeStruct((batch, 4 * SLOT, seq), f32),
            jax.ShapeDtypeStruct((batch, 2, tb, 2 * SLOT), f32),
            ka_shape, v_shape, kt_shape, v_shape,
            jax.ShapeDtypeStruct((2, batch, seq, SLOT), cd),
            jax.ShapeDtypeStruct((T, KV_HEADS * LANES), f32)]
    else:
        out_specs = [pl.BlockSpec((tb, HQ), row), pl.BlockSpec((tb, 4 * SLOT), row),
                     pl.BlockSpec((tb, 2 * SLOT), row), pl.BlockSpec((tb, GATE_COLS), row)]
        n_blocks = 0
        out_shape = [jax.ShapeDtypeStruct((T, HQ), f32), jax.ShapeDtypeStruct((T, 4 * SLOT), f32),
                     jax.ShapeDtypeStruct((T, 2 * SLOT), f32), jax.ShapeDtypeStruct((T, GATE_COLS), f32)]
    return pl.pallas_call(
        functools.partial(_nsa_proj_kernel, prompt=prompt, n_blocks=n_blocks),
        grid=(T // tb,),
        in_specs=in_specs, out_specs=out_specs, out_shape=out_shape,
        compiler_params=pltpu.CompilerParams(dimension_semantics=("arbitrary",),
                                             vmem_limit_bytes=VMEM_LIMIT_BYTES),
        name="nsa_proj",
    )(x, ln_g.reshape(1, D), w, q_gain.reshape(1, HEAD_DIM), k_gain)


def _compress_kernel(c_ref, w1_ref, pe_ref, w2_ref, kg_ref, o_ref):
    f32 = jnp.float32
    pairs = SLOT // LANES
    c = c_ref[0, 0]
    u = [jnp.dot(jnp.concatenate([c[:, l * SLOT + p * LANES:l * SLOT + (p + 1) * LANES]
                                  for l in range(CMP_STRIDE)], axis=1).astype(MXU_DTYPE),
                 w1_ref[0], preferred_element_type=f32) for p in range(pairs)]
    first = jnp.concatenate([up[:, :LANES] for up in u], axis=1)
    second = jnp.concatenate([up[:, LANES:] for up in u], axis=1)
    pe = jnp.dot(pe_ref[0], w1_ref[0], preferred_element_type=f32)
    pe = jnp.concatenate([pe[0:1, :LANES] + pe[1:2, LANES:]] * pairs, axis=1)
    nc = first.shape[0]
    hid = first + pltpu.roll(second, nc - 1, 0) + pe
    out = jnp.dot(_gelu_tanh(hid).astype(MXU_DTYPE), w2_ref[0], preferred_element_type=f32)
    is_key = pl.program_id(1) == 0
    for kv in range(KV_HEADS):
        lanes = slice(kv * HEAD_DIM, (kv + 1) * HEAD_DIM)
        v = out[:, lanes]
        vn = v * lax.rsqrt(jnp.mean(v * v, axis=-1, keepdims=True) + EPS) * kg_ref[...]
        o_ref[0, 0, :, lanes] = jnp.where(is_key, vn, v)


def _compress(c, pe, w1, w2, k_gain0):
    n, nc = c.shape[1:3]
    cd = MXU_DTYPE
    kc = CMP_STRIDE * SLOT
    eye = jnp.eye(KV_HEADS, dtype=jnp.float32)
    per_tile = LANES // HEAD_DIM
    kp = CMP_STRIDE * LANES
    w1h = w1.reshape(2, 2, CMP_STRIDE, HEAD_DIM, CMP_HID)
    w1b = jnp.einsum('shldf,kj->slkdhjf', w1h, eye[:per_tile, :per_tile]).reshape(2, kp, 2 * LANES).astype(cd)
    w2b = jnp.einsum('sdf,kj->skdjf', w2, eye).reshape(2, SLOT, SLOT).astype(cd)
    peh = jnp.broadcast_to(pe.reshape(2, 2, CMP_STRIDE, 1, HEAD_DIM), (2, 2, CMP_STRIDE, per_tile, HEAD_DIM))
    peh = jnp.pad(peh.reshape(2, 2, kp), ((0, 0), (0, 14), (0, 0))).astype(cd)
    return pl.pallas_call(
        _compress_kernel,
        grid=(n, 2),
        in_specs=[pl.BlockSpec((1, 1, nc, kc), lambda b, s: (s, b, 0, 0)),
                  pl.BlockSpec((1, kp, 2 * LANES), lambda b, s: (s, 0, 0)),
                  pl.BlockSpec((1, 16, kp), lambda b, s: (s, 0, 0)),
                  pl.BlockSpec((1, SLOT, SLOT), lambda b, s: (s, 0, 0)),
                  pl.BlockSpec((1, HEAD_DIM), lambda b, s: (0, 0))],
        out_specs=pl.BlockSpec((1, 1, nc, SLOT), lambda b, s: (s, b, 0, 0)),
        out_shape=jax.ShapeDtypeStruct((2, n, nc, SLOT), jnp.float32),
        compiler_params=pltpu.CompilerParams(dimension_semantics=("arbitrary", "arbitrary"),
                                             vmem_limit_bytes=VMEM_LIMIT_BYTES),
        name="nsa_compress",
    )(c, w1b, peh, w2b, k_gain0.reshape(1, HEAD_DIM))


def _proj_residual_kernel(x_ref, a_ref, w_ref, o_ref):
    o_ref[...] = x_ref[...] + jnp.dot(a_ref[...].astype(MXU_DTYPE), w_ref[...],
                                      preferred_element_type=jnp.float32)


def _proj_residual(x, a, w):
    T, D = x.shape
    k = a.shape[1]
    tb = min(512, T)
    assert T % tb == 0
    return pl.pallas_call(
        _proj_residual_kernel,
        grid=(T // tb,),
        in_specs=[pl.BlockSpec((tb, D), lambda i: (i, 0)), pl.BlockSpec((tb, k), lambda i: (i, 0)),
                  pl.BlockSpec((k, D), lambda i: (0, 0))],
        out_specs=pl.BlockSpec((tb, D), lambda i: (i, 0)),
        out_shape=jax.ShapeDtypeStruct((T, D), jnp.float32),
        compiler_params=pltpu.CompilerParams(dimension_semantics=("arbitrary",)),
        name="proj_residual",
    )(x, a, w.astype(MXU_DTYPE))


MOE_ROWS = 256
ROUTE_COLS = LANES
ROW_DMA_UNROLL = 8


def _moe_route_kernel(x_ref, g_ref, w_ref, b_ref, tri_ref, h_ref, meta_ref, wt_ref, cnt_ref, carry_sc):
    f32 = jnp.float32
    step = pl.program_id(0)

    @pl.when(step == 0)
    def _():
        carry_sc[...] = jnp.zeros(carry_sc.shape, f32)

    x = x_ref[...]
    h = x * lax.rsqrt(jnp.mean(x * x, axis=-1, keepdims=True) + EPS) * g_ref[...]
    h_ref[...] = h
    logits = jnp.dot(h.astype(MXU_DTYPE), w_ref[...], preferred_element_type=f32) + b_ref[...]
    tb = logits.shape[0]
    col = lax.broadcasted_iota(jnp.int32, (tb, ROUTE_COLS), 1)

    def first_max(vals):
        best = jnp.max(vals, axis=1, keepdims=True)
        return best, jnp.min(jnp.where(vals == best, col, ROUTE_COLS), axis=1, keepdims=True)

    lg = jnp.where(col < N_GROUPS, logits, -jnp.inf)
    g_best, grp = first_max(lg)
    g_w = 1.0 / jnp.sum(jnp.exp(lg - g_best), axis=1, keepdims=True)
    lo = N_GROUPS + EXPERTS_PER_GROUP * grp
    le = jnp.where((col >= lo) & (col < lo + EXPERTS_PER_GROUP), logits, -jnp.inf)
    v0, c0 = first_max(le)
    v1, c1 = first_max(jnp.where(col == c0, -jnp.inf, le))
    e1 = jnp.exp(v1 - v0)
    w0 = g_w / (1.0 + e1)
    w1 = g_w * e1 / (1.0 + e1)
    chosen = (col == c0) | (col == c1)
    before = jnp.dot(tri_ref[...], jnp.where(chosen, 1.0, 0.0).astype(MXU_DTYPE),
                     preferred_element_type=f32) + carry_sc[...]
    r0 = jnp.sum(jnp.where(col == c0, before, 0.0), axis=1, keepdims=True).astype(jnp.int32)
    r1 = jnp.sum(jnp.where(col == c1, before, 0.0), axis=1, keepdims=True).astype(jnp.int32)
    carry_sc[...] += jnp.sum(jnp.where(chosen, 1.0, 0.0), axis=0, keepdims=True)
    cnt_ref[...] = carry_sc[...]
    meta_ref[...] = jnp.where(col == 0, c0 - N_GROUPS, jnp.where(col == 1, c1 - N_GROUPS,
                              jnp.where(col == 2, r0, jnp.where(col == 3, r1, 0))))
    wt_ref[...] = jnp.where(col == 0, w0, jnp.where(col == 1, w1, 0.0))


def _moe_dispatch_kernel(dest_ref, h_ref, xbuf_in, xbuf_ref, sem):
    del xbuf_in
    tb = h_ref.shape[0]

    def row_copy(r, k):
        return pltpu.make_async_copy(h_ref.at[pl.ds(r, 1)], xbuf_ref.at[pl.ds(dest_ref[0, 0, 2 * r + k], 1)], sem)

    def start(r, c):
        row_copy(r, 0).start()
        row_copy(r, 1).start()
        return c

    def wait(r, c):
        row_copy(r, 0).wait()
        row_copy(r, 1).wait()
        return c

    lax.fori_loop(0, tb, start, 0, unroll=ROW_DMA_UNROLL)
    lax.fori_loop(0, tb, wait, 0, unroll=ROW_DMA_UNROLL)


def _moe_expert_kernel(blk_e_ref, nblk_ref, x_ref, wgu_ref, wdn_ref, y_ref, wgu_sc, wdn_sc):
    i = pl.program_id(0)
    f32 = jnp.float32

    @pl.when(i < nblk_ref[0])
    def _():
        changed = jnp.logical_or(i == 0, blk_e_ref[i] != blk_e_ref[jnp.maximum(i - 1, 0)])

        @pl.when(changed)
        def _():
            wgu_sc[...] = wgu_ref[0, 0].astype(MXU_DTYPE)
            wdn_sc[...] = wdn_ref[0, 0].astype(MXU_DTYPE)

        gu = jnp.dot(x_ref[...].astype(MXU_DTYPE), wgu_sc[...], preferred_element_type=f32)
        gate = gu[:, :D_EXPERT]
        act = gate * (1.0 / (1.0 + jnp.exp(-gate))) * gu[:, D_EXPERT:]
        y_ref[...] = jnp.dot(act.astype(MXU_DTYPE), wdn_sc[...], preferred_element_type=f32)

    @pl.when(i >= nblk_ref[0])
    def _():
        y_ref[...] = jnp.zeros(y_ref.shape, f32)


def _moe_combine_kernel(dest_ref, x_ref, wt_ref, ybuf_ref, o_ref, rows_sc, sem):
    tb = x_ref.shape[0]

    def row_copy(r, k):
        return pltpu.make_async_copy(ybuf_ref.at[pl.ds(dest_ref[0, 0, 2 * r + k], 1)],
                                     rows_sc.at[k, pl.ds(r, 1)], sem)

    def start(r, c):
        row_copy(r, 0).start()
        row_copy(r, 1).start()
        return c

    def wait(r, c):
        row_copy(r, 0).wait()
        row_copy(r, 1).wait()
        return c

    lax.fori_loop(0, tb, start, 0, unroll=ROW_DMA_UNROLL)
    lax.fori_loop(0, tb, wait, 0, unroll=ROW_DMA_UNROLL)
    wt = wt_ref[...]
    o_ref[...] = x_ref[...] + (wt[:, 0:1] * rows_sc[0] + wt[:, 1:2] * rows_sc[1])


def _hier_moe_residual(x, ln_g, w_grp, b_grp, w_exp, b_exp, w_gu, w_dn, layer):
    T, D = x.shape
    f32 = jnp.float32
    tb = min(256, T)
    assert T % tb == 0
    nt = T // tb
    pad_cols = ROUTE_COLS - N_GROUPS - N_EXPERTS
    w_r = jnp.pad(jnp.concatenate([w_grp, w_exp], axis=1), ((0, 0), (0, pad_cols))).astype(MXU_DTYPE)
    b_r = jnp.pad(jnp.concatenate([b_grp, b_exp]), (0, pad_cols)).reshape(1, ROUTE_COLS)
    tri = jnp.asarray(np.tril(np.ones((tb, tb), np.float32), -1), dtype=MXU_DTYPE)
    row = lambda i: (i, 0)
    fixed = lambda i: (0, 0)
    h, meta, wt, cnt = pl.pallas_call(
        _moe_route_kernel,
        grid=(nt,),
        in_specs=[pl.BlockSpec((tb, D), row), pl.BlockSpec((1, D), fixed),
                  pl.BlockSpec((D, ROUTE_COLS), fixed), pl.BlockSpec((1, ROUTE_COLS), fixed),
                  pl.BlockSpec((tb, tb), fixed)],
        out_specs=[pl.BlockSpec((tb, D), row), pl.BlockSpec((tb, ROUTE_COLS), row),
                   pl.BlockSpec((tb, ROUTE_COLS), row), pl.BlockSpec((1, ROUTE_COLS), fixed)],
        out_shape=[jax.ShapeDtypeStruct((T, D), f32), jax.ShapeDtypeStruct((T, ROUTE_COLS), jnp.int32),
                   jax.ShapeDtypeStruct((T, ROUTE_COLS), f32), jax.ShapeDtypeStruct((1, ROUTE_COLS), f32)],
        scratch_shapes=[pltpu.VMEM((1, ROUTE_COLS), f32)],
        compiler_params=pltpu.CompilerParams(dimension_semantics=("arbitrary",)),
        name="moe_route",
    )(x, ln_g.reshape(1, D), w_r, b_r, tri)

    counts = cnt[0, N_GROUPS:N_GROUPS + N_EXPERTS].astype(jnp.int32)
    padded = (counts + MOE_ROWS - 1) // MOE_ROWS * MOE_ROWS
    pad_end = jnp.cumsum(padded)
    pad_start = pad_end - padded
    n_blocks = -(-(T * TOP_K) // MOE_ROWS) + N_EXPERTS
    n_slots = n_blocks * MOE_ROWS
    dest = pad_start[meta[:, 0:2]] + meta[:, 2:4]
    dest = dest.reshape(nt, 1, 2 * tb)
    starts = jnp.arange(n_blocks, dtype=jnp.int32) * MOE_ROWS
    used = (pad_end[-1] // MOE_ROWS).astype(jnp.int32).reshape(1)
    blk_start = jnp.minimum(starts, pad_end[-1] - 1)
    blk_e = jnp.minimum(jnp.sum(pad_end[None, :] <= blk_start[:, None], axis=1), N_EXPERTS - 1).astype(jnp.int32)

    dest_spec = pl.BlockSpec((1, 1, 2 * tb), lambda i: (i, 0, 0), memory_space=pltpu.SMEM)
    xbuf = pl.pallas_call(
        _moe_dispatch_kernel,
        grid=(nt,),
        in_specs=[dest_spec, pl.BlockSpec((tb, D), row), pl.BlockSpec(memory_space=pl.ANY)],
        out_specs=pl.BlockSpec(memory_space=pl.ANY),
        out_shape=jax.ShapeDtypeStruct((n_slots, D), f32),
        scratch_shapes=[pltpu.SemaphoreType.DMA(())],
        input_output_aliases={2: 0},
        compiler_params=pltpu.CompilerParams(dimension_semantics=("arbitrary",)),
        name="moe_dispatch",
    )(dest, h, jnp.zeros((n_slots, D), f32))

    ybuf = pl.pallas_call(
        _moe_expert_kernel,
        grid_spec=pltpu.PrefetchScalarGridSpec(
            num_scalar_prefetch=2,
            grid=(n_blocks,),
            in_specs=[pl.BlockSpec((MOE_ROWS, D), lambda i, be, nb: (i, 0)),
                      pl.BlockSpec((1, 1, D, 2 * D_EXPERT), lambda i, be, nb: (layer, be[i], 0, 0)),
                      pl.BlockSpec((1, 1, D_EXPERT, D), lambda i, be, nb: (layer, be[i], 0, 0))],
            out_specs=pl.BlockSpec((MOE_ROWS, D), lambda i, be, nb: (i, 0)),
            scratch_shapes=[pltpu.VMEM((D, 2 * D_EXPERT), MXU_DTYPE), pltpu.VMEM((D_EXPERT, D), MXU_DTYPE)]),
        out_shape=jax.ShapeDtypeStruct((n_slots, D), f32),
        compiler_params=pltpu.CompilerParams(dimension_semantics=("arbitrary",),
                                             vmem_limit_bytes=VMEM_LIMIT_BYTES),
        name="moe_experts",
    )(blk_e, used, xbuf, w_gu, w_dn)

    return pl.pallas_call(
        _moe_combine_kernel,
        grid=(nt,),
        in_specs=[dest_spec, pl.BlockSpec((tb, D), row), pl.BlockSpec((tb, ROUTE_COLS), row),
                  pl.BlockSpec(memory_space=pl.ANY)],
        out_specs=pl.BlockSpec((tb, D), row),
        out_shape=jax.ShapeDtypeStruct((T, D), f32),
        scratch_shapes=[pltpu.VMEM((TOP_K, tb, D), f32), pltpu.SemaphoreType.DMA(())],
        compiler_params=pltpu.CompilerParams(dimension_semantics=("arbitrary",)),
        name="moe_combine",
    )(dest, x, wt, ybuf)


def _gelu_tanh(x):
    return 0.5 * x * (1.0 + jnp.tanh(math.sqrt(2.0 / math.pi) * (x + 0.044715 * (x * x * x))))


def _gmlp_kernel(x_ref, g_ref, win_ref, bin_ref, lng_ref, lnb_ref, ws_ref, bs_ref, wout_ref,
                 o_ref, v_ref, *, single_position):
    f32 = jnp.float32
    x = x_ref[...]
    rows = x.shape[0]
    h = x * lax.rsqrt(jnp.mean(x * x, axis=-1, keepdims=True) + EPS) * g_ref[...]
    z = _gelu_tanh(jnp.dot(h.astype(MXU_DTYPE), win_ref[...], preferred_element_type=f32) + bin_ref[...])
    u = z[:, :GM_WIDTH]
    v = z[:, GM_WIDTH:]
    mu = jnp.mean(v, axis=-1, keepdims=True)
    var = jnp.mean(jnp.square(v - mu), axis=-1, keepdims=True)
    v = (v - mu) * lax.rsqrt(var + EPS) * lng_ref[...] + lnb_ref[...]
    v_ref[0] = v
    vb = v.astype(MXU_DTYPE)
    if single_position:
        s = ws_ref[...].astype(f32) * vb.astype(f32) + bs_ref[...]
    else:
        parts = []
        for c in range(rows // CHUNK):
            vc = vb[c * CHUNK:(c + 1) * CHUNK]
            parts.append(jnp.concatenate(
                [jnp.dot(ws_ref[g], vc[:, g * GM_GROUP_DIM:(g + 1) * GM_GROUP_DIM], preferred_element_type=f32)
                 + bs_ref[g] for g in range(GM_GROUPS)], axis=1))
        s = jnp.concatenate(parts, axis=0)
    y = jnp.dot((u * s).astype(MXU_DTYPE), wout_ref[...], preferred_element_type=f32)
    o_ref[...] = x + y


def _gmlp_residual(x, ln_g, w_in, b_in, ln2_g, ln2_b, w_s, b_s, w_out, seq):
    T, D = x.shape
    f32 = jnp.float32
    cd = MXU_DTYPE
    single = seq == 1
    tb = T if single else 2 * CHUNK
    assert T % tb == 0 and (single or seq % tb == 0)
    steps_per_seq = 1 if single else seq // tb
    if single:
        ws = jnp.repeat(w_s[:, 0, 0], GM_GROUP_DIM).reshape(1, GM_WIDTH).astype(cd)
        bs = jnp.repeat(b_s[:, 0], GM_GROUP_DIM).reshape(1, GM_WIDTH)
        ws_spec = pl.BlockSpec((1, GM_WIDTH), lambda i: (0, 0))
        bs_spec = pl.BlockSpec((1, GM_WIDTH), lambda i: (0, 0))
    else:
        causal = np.tril(np.ones((CHUNK, CHUNK), bool))
        ws = jnp.where(causal[None], w_s, 0).astype(cd)
        bs = jnp.broadcast_to(b_s[:, :, None], (GM_GROUPS, CHUNK, GM_GROUP_DIM))
        ws_spec = pl.BlockSpec((GM_GROUPS, CHUNK, CHUNK), lambda i: (0, 0, 0))
        bs_spec = pl.BlockSpec((GM_GROUPS, CHUNK, GM_GROUP_DIM), lambda i: (0, 0, 0))
    fixed = lambda i: (0, 0)
    row = lambda i: (i, 0)
    return pl.pallas_call(
        functools.partial(_gmlp_kernel, single_position=single),
        grid=(T // tb,),
        in_specs=[pl.BlockSpec((tb, D), row), pl.BlockSpec((1, D), fixed),
                  pl.BlockSpec((D, 2 * GM_WIDTH), fixed), pl.BlockSpec((1, 2 * GM_WIDTH), fixed),
                  pl.BlockSpec((1, GM_WIDTH), fixed), pl.BlockSpec((1, GM_WIDTH), fixed),
                  ws_spec, bs_spec, pl.BlockSpec((GM_WIDTH, D), fixed)],
        out_specs=[pl.BlockSpec((tb, D), row), pl.BlockSpec((1, tb, GM_WIDTH), lambda i: (i // steps_per_seq, 0, 0))],
        out_shape=[jax.ShapeDtypeStruct((T, D), f32),
                   jax.ShapeDtypeStruct((T // (tb * steps_per_seq), tb, GM_WIDTH), f32)],
        compiler_params=pltpu.CompilerParams(dimension_semantics=("arbitrary",),
                                             vmem_limit_bytes=VMEM_LIMIT_BYTES),
        name="gmlp",
    )(x, ln_g.reshape(1, D), w_in.astype(cd), b_in.reshape(1, -1), ln2_g.reshape(1, -1), ln2_b.reshape(1, -1),
      ws, bs, w_out.astype(cd))


def kernel(x_prompt, x_sample, cache_nsa_kv, state_win_kv, page_table, rel_bias, ln_mix, ln_ffn,
           nsa_w_in, nsa_w_out, nsa_q_gain, nsa_k_gain, cmp_pe, cmp_w1, cmp_w2,
           gm_w_in, gm_b_in, gm_ln_g, gm_ln_b, gm_w_s, gm_b_s, gm_w_out,
           moe_w_grp, moe_b_grp, moe_w_exp, moe_b_exp, moe_w_gu, moe_w_dn):
    xp = x_prompt
    xs = x_sample
    kv_p, kv_s, win_p, win_s, gv_p, gv_s = [], [], [], [], [], []
    for i in range(DEPTH):
        a = i // N_MIXERS
        if i % N_MIXERS == 0:
            nsa = (ln_mix[i], nsa_w_in[a], nsa_w_out[a], nsa_q_gain[a], nsa_k_gain[a],
                   cmp_pe[a], cmp_w1[a], cmp_w2[a], rel_bias)
            xp, rp, wp = _nsa_prompt_layer(xp, *nsa)
            cmp_rows, sel_kvt = _page_gather(cache_nsa_kv, page_table, a)
            xs, rs, ws = _nsa_decode_layer(xs, cmp_rows, sel_kvt, state_win_kv[a], *nsa)
            kv_p.append(rp)
            kv_s.append(rs)
            win_p.append(wp)
            win_s.append(ws)
        else:
            gm = (ln_mix[i], gm_w_in[a], gm_b_in[a], gm_ln_g[a], gm_ln_b[a], gm_w_s[a], gm_b_s[a], gm_w_out[a])
            bp, sp = xp.shape[:2]
            bs_, ss = xs.shape[:2]
            xp2, vp = _gmlp_residual(xp.reshape(-1, D_MODEL), *gm, seq=sp)
            xs2, vs = _gmlp_residual(xs.reshape(-1, D_MODEL), *gm, seq=ss)
            xp = xp2.reshape(xp.shape)
            xs = xs2.reshape(xs.shape)
            start = ((sp - 1) // CHUNK) * CHUNK
            gv_p.append(vp[:, vp.shape[1] - (sp - start):])
            gv_s.append(vs.reshape(bs_, ss, GM_WIDTH))
        moe = (ln_ffn[i], moe_w_grp[i], moe_b_grp[i], moe_w_exp[i], moe_b_exp[i], moe_w_gu, moe_w_dn, i)
        xp = _hier_moe_residual(xp.reshape(-1, D_MODEL), *moe).reshape(xp.shape)
        xs = _hier_moe_residual(xs.reshape(-1, D_MODEL), *moe).reshape(xs.shape)
    new_kv_prompt = jnp.stack(kv_p, axis=2)
    new_kv_sample = jnp.stack(kv_s, axis=2)
    new_win_prompt = jnp.stack(win_p, axis=0)
    new_win_sample = jnp.stack(win_s, axis=0)
    new_gm_v_prompt = jnp.stack(gv_p, axis=0)
    new_gm_v_sample = jnp.stack(gv_s, axis=0)
    return (xp, xs, new_kv_prompt, new_kv_sample, new_win_prompt, new_win_sample, new_gm_v_prompt, new_gm_v_sample)
```

```python
import functools
import math

import jax
import jax.numpy as jnp
import numpy as np
from jax import lax
from jax.experimental import pallas as pl
from jax.experimental.pallas import tpu as pltpu

D_MODEL = 1024
PAGE_SIZE = 128
DEPTH = 2
N_MIXERS = 2
N_HEADS = 16
HEAD_DIM = 64
KV_HEADS = 4
Q_PER_KV = N_HEADS // KV_HEADS
CMP_LEN = 32
CMP_STRIDE = 16
CMP_HID = 64
SEL_BLOCK = 64
CMP_PER_SEL = SEL_BLOCK // CMP_STRIDE
N_SEL = 16
N_LOCAL = 2
WINDOW = 512
N_BUCKETS = 32
MAX_DISTANCE = 128
CHUNK = 128
GM_WIDTH = 2048
GM_GROUPS = 8
GM_GROUP_DIM = GM_WIDTH // GM_GROUPS
N_GROUPS = 4
EXPERTS_PER_GROUP = 8
N_EXPERTS = N_GROUPS * EXPERTS_PER_GROUP
TOP_K = 2
D_EXPERT = 512
EPS = 1e-6

LANES = 128
VMEM_LIMIT_BYTES = 56 * 1024 * 1024


NEG = -1e30
MXU_DTYPE = jnp.bfloat16
TQ = 256
TK = 256
GROUP_CHUNKS = 3
NEAR_CHUNKS = 5
BIAS_SPAN = 128
LOG2E = 1.4426950408889634


def _bucket_of_distance():
    d = np.arange(BIAS_SPAN)
    max_exact = N_BUCKETS // 2
    nf = np.maximum(d, 1).astype(np.float32)
    large = max_exact + (np.log(nf / np.float32(max_exact)) / np.float32(math.log(MAX_DISTANCE / max_exact))
                         * np.float32(N_BUCKETS - max_exact)).astype(np.int32)
    large = np.minimum(large, N_BUCKETS - 1)
    return np.where(d < max_exact, d, large).astype(np.int32)


def _attn_bias_tiles(rel_bias, seq):
    fd = rel_bias[_bucket_of_distance()].T
    far = fd[:, BIAS_SPAN - 1]
    fd = fd - far[:, None]

    def by_distance(d):
        return jnp.where(d >= 0, LOG2E * jnp.take(fd, np.clip(d, 0, BIAS_SPAN - 1), axis=1), NEG)

    def toeplitz(first_distance):
        c = np.arange(2 * TK)
        g = by_distance(first_distance - np.where(c < TK, c, c - 2 * TK))
        flat = jnp.tile(g, (1, TQ))[:, :TQ * (2 * TK - 1)]
        return flat.reshape(-1, TQ, 2 * TK - 1)[:, :, :TK]

    b0 = toeplitz(0)
    b1 = toeplitz(TQ)
    i = np.arange(TQ)[:, None]
    j = np.arange(TK)[None, :]
    w2 = np.where(2 * TQ + i - j <= WINDOW, 0.0, NEG).astype(np.float32)
    nc = seq // CMP_STRIDE
    span = -(-(BIAS_SPAN + TQ) // CMP_STRIDE)
    r = np.arange(nc - span, nc + span)[:, None]
    dc = np.arange(TQ)[None, :] - CMP_STRIDE * (r - nc) - (CMP_LEN - 1)
    n_heads = fd.shape[0]
    pat = jnp.concatenate([jnp.zeros((n_heads, nc - span, TQ), jnp.float32), by_distance(dc),
                           jnp.full((n_heads, nc - span, TQ), NEG, jnp.float32)], axis=1)
    return b0, b1, jnp.asarray(w2), pat


def _nsa_attn_kernel(q_ref, kc_ref, vct_ref, kat_ref, vsa_ref, kwt_ref, vwa_ref, g_ref,
                     b0_ref, b1_ref, w2_ref, pat_ref, o_ref, m_sc, acc_sc, imp_sc, qa_sc, p_sc, a_sc, *, nc, nb):
    f32 = jnp.float32
    cd = MXU_DTYPE
    qb = pl.program_id(2)

    def compress_and_select(ncu, all_forced):
        nbu = ncu // CMP_PER_SEL
        r0 = pl.multiple_of(nc - (TQ // CMP_STRIDE) * qb, TQ // CMP_STRIDE)
        t_row = qb * TQ + lax.broadcasted_iota(jnp.int32, (1, TQ), 1)
        has_cmp = t_row >= CMP_LEN - 1
        for h in range(Q_PER_KV):
            st = lax.dot_general(kc_ref[0, 0, 0:ncu, :], q_ref[0, h], (((1,), (1,)), ((), ())),
                                 preferred_element_type=f32)
            st = st + pat_ref[h, pl.ds(r0, ncu), :]
            e = jnp.exp2(st - jnp.max(st, axis=0, keepdims=True))
            inv = jnp.where(has_cmp, 1.0 / jnp.sum(e, axis=0, keepdims=True), 0.0)
            pt = e * inv
            for c in range(TQ // LANES):
                part = pt[:, c * LANES:(c + 1) * LANES]
                if h == 0:
                    imp_sc[c, 0:ncu, :] = part
                else:
                    imp_sc[c, 0:ncu, :] += part
            oct_h = jnp.dot(vct_ref[0, 0, :, 0:ncu], pt.astype(cd), preferred_element_type=f32)
            o_ref[0, :, h * HEAD_DIM:(h + 1) * HEAD_DIM] = g_ref[0][:, 3 * h:3 * h + 1] * oct_h.T

        imp = jnp.concatenate(
            [sum(imp_sc[c, pl.ds(r, nbu, stride=CMP_PER_SEL), :] for r in range(CMP_PER_SEL))
             for c in range(TQ // LANES)], axis=1)
        blk = lax.broadcasted_iota(jnp.int32, (nbu, TQ), 0)
        t_blk = (qb * TQ + lax.broadcasted_iota(jnp.int32, (nbu, TQ), 1)) // SEL_BLOCK
        dd = t_blk - blk
        forced = (blk == 0) | ((dd >= 0) & (dd < N_LOCAL))
        if all_forced:
            score = jnp.where(forced | (dd < 0), -jnp.inf, imp)
            pen_t = jnp.where(forced, 0.0, NEG)
            rounds = N_SEL - (N_LOCAL + 1)
        else:
            score = jnp.where(forced, jnp.inf, jnp.where(dd >= 0, imp, -jnp.inf))
            pen_t = jnp.full((nbu, TQ), NEG, f32)
            rounds = min(N_SEL, nbu)
        for _ in range(rounds):
            best = jnp.max(score, axis=0, keepdims=True)
            first = jnp.min(jnp.where(score == best, blk, nbu), axis=0, keepdims=True)
            hit = blk == first
            pen_t = jnp.where(hit, 0.0, pen_t)
            score = jnp.where(hit, -jnp.inf, score)
        if nbu < nb:
            pen_t = jnp.concatenate([pen_t, jnp.full((nb - nbu, TQ), NEG, f32)], axis=0)
        pen = pen_t.T.astype(cd)
        pad = jnp.zeros((TQ, qa_sc.shape[2] - nb - HEAD_DIM), cd)
        for h in range(Q_PER_KV):
            qa_sc[h] = jnp.concatenate([pen, q_ref[0, h], pad], axis=1)

    n_tiles = nc * CMP_STRIDE // TQ
    n_var = 4 if (n_tiles % 4 == 0 and (nc // 4) % (2 * CMP_STRIDE) == 0) else 1
    for v in range(n_var):
        lo, hi = v * n_tiles // n_var, (v + 1) * n_tiles // n_var
        ncu = (v + 1) * nc // n_var
        if v == 0:
            @pl.when(qb == 0)
            def _():
                compress_and_select(ncu, all_forced=False)
            lo = 1

        @pl.when((qb >= lo) & (qb < hi))
        def _():
            compress_and_select(ncu, all_forced=TQ >= N_LOCAL * SEL_BLOCK)


    m_sc[...] = jnp.full(m_sc.shape, NEG, f32)
    acc_sc[...] = jnp.zeros(acc_sc.shape, f32)

    def run(*work):
        jobs = [(br, c, bias) for br, chunks in work for c, bias in chunks]
        for slot, ((b, lhs, k_ref, _), c, bias) in enumerate(jobs):
            for h in range(Q_PER_KV):
                s = jnp.dot(lhs(h), k_ref[0, 0, c], preferred_element_type=f32)
                if bias is not None:
                    s = s + bias(h)
                m_prev = m_sc[b, h]
                m_new = jnp.maximum(m_prev, jnp.max(s, axis=1, keepdims=True))
                p_sc[slot, h] = jnp.exp2(s - jnp.concatenate([m_new] * (TK // LANES), axis=1)).astype(cd)
                a_sc[slot, h] = jnp.exp2(m_prev - m_new)
                m_sc[b, h] = m_new
        for slot, ((b, _, _, v_ref), c, _) in enumerate(jobs):
            for h in range(Q_PER_KV):
                acc_sc[b, h] = a_sc[slot, h] * acc_sc[b, h] + jnp.dot(p_sc[slot, h], v_ref[0, 0, c],
                                                                      preferred_element_type=f32)

    def emit(b, gate_col):
        for h in range(Q_PER_KV):
            acc = acc_sc[b, h]
            w = g_ref[0][:, 3 * h + gate_col:3 * h + gate_col + 1] / acc[:, HEAD_DIM:HEAD_DIM + 1]
            o_ref[0, :, h * HEAD_DIM:(h + 1) * HEAD_DIM] += w * acc[:, :HEAD_DIM]

    own = lambda h: b0_ref[h]
    prev = lambda h: b1_ref[h]
    edge = lambda h: w2_ref[...]

    sel = (0, lambda h: qa_sc[h], kat_ref, vsa_ref)
    win = (1, lambda h: q_ref[0, h], kwt_ref, vwa_ref)
    n_far = jnp.maximum(qb - 1, 0)

    def far_group(i, carry):
        run((sel, [(GROUP_CHUNKS * i + j, None) for j in range(GROUP_CHUNKS)]))
        return carry

    lax.fori_loop(0, n_far // GROUP_CHUNKS, far_group, 0)
    for left in range(1, GROUP_CHUNKS):
        @pl.when(n_far % GROUP_CHUNKS == left)
        def _():
            run((sel, [(n_far - left + j, None) for j in range(left)]))

    @pl.when(qb >= 2)
    def _():
        run((sel, [(qb - 1, prev), (qb, own)]), (win, [(qb - 2, edge), (qb - 1, prev), (qb, own)]))

    @pl.when(qb == 1)
    def _():
        run((sel, [(qb - 1, prev), (qb, own)]), (win, [(qb - 1, prev), (qb, own)]))

    @pl.when(qb == 0)
    def _():
        run((sel, [(qb, own)]), (win, [(qb, own)]))

    emit(0, 1)
    emit(1, 2)


def _nsa_attention(qs, kcp, vct, kat, vsa, kwt, vwa, gt, rel_bias):
    B, _, S, _ = qs.shape
    assert S % TQ == 0 and TQ == TK and WINDOW == 2 * TQ and S // SEL_BLOCK >= N_SEL
    nc = S // CMP_STRIDE
    nb = S // SEL_BLOCK
    nch = S // TK
    ka = kat.shape[3]
    b0, b1, w2, pat = _attn_bias_tiles(rel_bias, S)
    G = Q_PER_KV
    kv_chunks = pl.BlockSpec((1, 1, nch, HEAD_DIM, TK), lambda b, k, i: (b, k, 0, 0, 0))
    ka_chunks = pl.BlockSpec((1, 1, nch, ka, TK), lambda b, k, i: (b, k, 0, 0, 0))
    v_chunks = pl.BlockSpec((1, 1, nch, TK, LANES), lambda b, k, i: (b, k, 0, 0, 0))
    return pl.pallas_call(
        functools.partial(_nsa_attn_kernel, nc=nc, nb=nb),
        grid=(B, KV_HEADS, S // TQ),
        in_specs=[
            pl.BlockSpec((1, G, TQ, HEAD_DIM), lambda b, k, i: (b, k, i, 0)),
            pl.BlockSpec((1, 1, nc, HEAD_DIM), lambda b, k, i: (b, k, 0, 0)),
            pl.BlockSpec((1, 1, HEAD_DIM, nc), lambda b, k, i: (b, k, 0, 0)),
            ka_chunks, v_chunks, kv_chunks, v_chunks,
            pl.BlockSpec((1, TQ, LANES), lambda b, k, i: (b, i, k)),
            pl.BlockSpec((G, TQ, TK), lambda b, k, i: (k, 0, 0)),
            pl.BlockSpec((G, TQ, TK), lambda b, k, i: (k, 0, 0)),
            pl.BlockSpec((TQ, TK), lambda b, k, i: (0, 0)),
            pl.BlockSpec((G, 2 * nc, TQ), lambda b, k, i: (k, 0, 0)),
        ],
        out_specs=pl.BlockSpec((1, TQ, G * HEAD_DIM), lambda b, k, i: (b, i, k)),
        out_shape=jax.ShapeDtypeStruct((B, S, N_HEADS * HEAD_DIM), jnp.float32),
        scratch_shapes=[
            pltpu.VMEM((2, G, TQ, LANES), jnp.float32),
            pltpu.VMEM((2, G, TQ, LANES), jnp.float32),
            pltpu.VMEM((TQ // LANES, nc, LANES), jnp.float32),
            pltpu.VMEM((G, TQ, ka), MXU_DTYPE),
            pltpu.VMEM((NEAR_CHUNKS, G, TQ, TK), MXU_DTYPE),
            pltpu.VMEM((NEAR_CHUNKS, G, TQ, LANES), jnp.float32),
        ],
        compiler_params=pltpu.CompilerParams(
            dimension_semantics=("arbitrary", "arbitrary", "arbitrary"),
            vmem_limit_bytes=VMEM_LIMIT_BYTES),
        name="nsa_attn",
    )(qs, kcp, vct, kat, vsa, kwt, vwa, gt, b0, b1, w2, pat)


def _nsa_prompt_layer(x, ln_g, w_in, w_out, q_gain, k_gain, pe, w1, w2, rel_bias):
    B, S, D = x.shape
    x2 = x.reshape(B * S, D)
    qs, rows, win, kat, vsa, kwt, vwa, c, gates = _nsa_project(x2, ln_g, w_in, q_gain, k_gain, B, S)
    nc = S // CMP_STRIDE
    cmp = _compress(c.reshape(2, B, nc, CMP_STRIDE * SLOT), pe, w1, w2, k_gain[0])
    cmp = cmp.reshape(2, B, nc, KV_HEADS, HEAD_DIM).astype(MXU_DTYPE)
    gt = gates.reshape(B, S, KV_HEADS * LANES)
    o = _nsa_attention(qs, cmp[0].transpose(0, 2, 1, 3), cmp[1].transpose(0, 2, 3, 1), kat, vsa, kwt, vwa, gt,
                       rel_bias)
    y = _proj_residual(x2, o.reshape(B * S, HQ), w_out)
    return (y.reshape(B, S, D), rows.reshape(B, 4, KV_HEADS, HEAD_DIM, S).transpose(0, 4, 1, 2, 3),
            win.reshape(B, WINDOW, 2, KV_HEADS, HEAD_DIM))


PAGES_PER_STEP = 8


def _page_gather_kernel(pt_ref, *refs):
    del pt_ref
    pages = refs[:PAGES_PER_STEP]
    c_ref, kvt_ref, rows_sc = refs[PAGES_PER_STEP:]
    for p, pg in enumerate(pages):
        pos = slice(p * PAGE_SIZE, (p + 1) * PAGE_SIZE)
        for slot in range(2):
            for pair in range(KV_HEADS // 2):
                tile = pg[0, 0, slot, 2 * pair:2 * pair + 2].reshape(LANES, PAGE_SIZE)
                rows_sc[slot * 2 + pair, pos, :] = tile.T
        for slot in range(2):
            kvt_ref[slot, 0, :, :, pos] = pg[0, 0, 2 + slot].astype(MXU_DTYPE)
    n_chunks = PAGES_PER_STEP * PAGE_SIZE // CMP_STRIDE
    for l in range(CMP_STRIDE):
        for t in range(4):
            lo = l * SLOT + (t % 2) * LANES
            c_ref[t // 2, 0, :, lo:lo + LANES] = rows_sc[t, pl.ds(l, n_chunks, stride=CMP_STRIDE), :].astype(MXU_DTYPE)


def _page_gather(cache, page_table, layer):
    nseq, n_pages = page_table.shape
    assert n_pages % PAGES_PER_STEP == 0 and 2 * LANES == SLOT
    P = n_pages * PAGE_SIZE
    cache_t = cache.transpose(0, 2, 3, 4, 5, 1)
    page_specs = [
        pl.BlockSpec((1, 1, 4, KV_HEADS, HEAD_DIM, PAGE_SIZE),
                     functools.partial(lambda r, n, j, pt: (pt[n, PAGES_PER_STEP * j + r], layer, 0, 0, 0, 0), r))
        for r in range(PAGES_PER_STEP)]
    rows_per_step = PAGES_PER_STEP * PAGE_SIZE
    return pl.pallas_call(
        _page_gather_kernel,
        grid_spec=pltpu.PrefetchScalarGridSpec(
            num_scalar_prefetch=1,
            grid=(nseq, n_pages // PAGES_PER_STEP),
            in_specs=page_specs,
            out_specs=[pl.BlockSpec((2, 1, rows_per_step // CMP_STRIDE, CMP_STRIDE * SLOT),
                                    lambda n, j, pt: (0, n, j, 0)),
                       pl.BlockSpec((2, 1, KV_HEADS, HEAD_DIM, rows_per_step), lambda n, j, pt: (0, n, 0, 0, j))],
            scratch_shapes=[pltpu.VMEM((4, rows_per_step, LANES), jnp.float32)]),
        out_shape=[jax.ShapeDtypeStruct((2, nseq, P // CMP_STRIDE, CMP_STRIDE * SLOT), MXU_DTYPE),
                   jax.ShapeDtypeStruct((2, nseq, KV_HEADS, HEAD_DIM, P), MXU_DTYPE)],
        compiler_params=pltpu.CompilerParams(dimension_semantics=("arbitrary", "arbitrary")),
        name="page_gather",
    )(page_table, *([cache_t] * PAGES_PER_STEP))


def _split3(x):
    hi = x.astype(MXU_DTYPE)
    r1 = x - hi.astype(jnp.float32)
    mid = r1.astype(MXU_DTYPE)
    return hi, mid, (r1 - mid.astype(jnp.float32)).astype(MXU_DTYPE)


def _nsa_decode_kernel(q_ref, qbd_ref, kc_ref, vc_ref, kst_ref, vst_ref, win_ref, rows_ref, wnew_ref, g_ref,
                       oh_ref, grp_ref, bc_ref, bs_ref, bw_ref, bn_ref, o_ref, nwin_ref, pen_sc, *, nb):
    f32 = jnp.float32
    cd = MXU_DTYPE
    nt = (((1,), (1,)), ((), ()))
    wb = win_ref.shape[-1]
    rows = rows_ref[0]
    wnew = wnew_ref[0]

    def softmax_with_new(s, s_new):
        m = jnp.maximum(jnp.max(s, axis=1, keepdims=True), s_new)
        e = jnp.exp(s - m)
        e_new = jnp.exp(s_new - m)
        inv = 1.0 / (jnp.sum(e, axis=1, keepdims=True) + e_new)
        return e * inv, e_new * inv

    def bf(x):
        return x.astype(cd).astype(f32)

    for k in range(KV_HEADS):
        hs = slice(k * Q_PER_KV, (k + 1) * Q_PER_KV)
        lanes = slice(k * HEAD_DIM, (k + 1) * HEAD_DIM)
        q = q_ref[0, hs, :]
        gates = g_ref[0, hs, :]
        sc = (lax.dot_general(q, kc_ref[0, 0][:, lanes].astype(cd), nt, preferred_element_type=f32)
              + bc_ref[hs, :])
        ec = jnp.exp(sc - jnp.max(sc, axis=1, keepdims=True))
        pc = ec / jnp.sum(ec, axis=1, keepdims=True)
        o_ref[0, hs, :] = gates[:, 0:1] * jnp.dot(pc.astype(cd), vc_ref[0, 0][:, lanes].astype(cd),
                                                  preferred_element_type=f32)

        imp = sum(jnp.dot(part, grp_ref[...], preferred_element_type=f32) for part in _split3(pc))
        imp = jnp.sum(imp, axis=0, keepdims=True)
        blk = lax.broadcasted_iota(jnp.int32, imp.shape, 1)
        forced = (blk == 0) | (blk == nb - 1)
        score = jnp.where(forced, jnp.inf, imp)
        pen = jnp.full(imp.shape, NEG, f32)
        for _ in range(N_SEL - 1):
            best = jnp.max(score, axis=1, keepdims=True)
            first = jnp.min(jnp.where(score == best, blk, nb), axis=1, keepdims=True)
            hit = blk == first
            pen = jnp.where(hit, 0.0, pen)
            score = jnp.where(hit, -jnp.inf, score)
        pen_sc[hs, :] = jnp.broadcast_to(pen, (Q_PER_KV, nb))

    n_keys = kst_ref.shape[-1]
    qbd = qbd_ref[0]
    key_pen = jnp.dot(pen_sc[...].astype(cd), oh_ref[...], preferred_element_type=f32)
    ss = (jnp.dot(qbd, kst_ref[0, 0].reshape(SLOT, n_keys), preferred_element_type=f32)
          + key_pen + bs_ref[...])
    s_new = (jnp.sum(qbd.astype(f32) * bf(rows[:, 2 * SLOT:3 * SLOT]), axis=1, keepdims=True) + bn_ref[...])
    p, p_new = softmax_with_new(ss, s_new)
    sel_all = (lax.dot_general(p.astype(cd), vst_ref[0, 0].reshape(SLOT, n_keys), nt,
                               preferred_element_type=f32)
               + bf(p_new) * bf(rows[:, 3 * SLOT:4 * SLOT]))

    eye = (lax.broadcasted_iota(jnp.int32, (HEAD_DIM, HEAD_DIM), 0)
           == lax.broadcasted_iota(jnp.int32, (HEAD_DIM, HEAD_DIM), 1))
    last = lax.broadcasted_iota(jnp.int32, (HEAD_DIM, wb), 1) == wb - 1
    for k in range(KV_HEADS):
        hs = slice(k * Q_PER_KV, (k + 1) * Q_PER_KV)
        lanes = slice(k * HEAD_DIM, (k + 1) * HEAD_DIM)
        q = q_ref[0, hs, :]
        gates = g_ref[0, hs, :]
        kw_t = win_ref[0, 0, k]
        vw_t = win_ref[0, 1, k]
        k_new = wnew[:, :SLOT][:, lanes]
        v_new = wnew[:, SLOT:][:, lanes]
        sw = jnp.dot(q, kw_t.astype(cd), preferred_element_type=f32) + bw_ref[hs, :]
        w_new = jnp.sum(q.astype(f32) * bf(k_new), axis=1, keepdims=True) + bn_ref[hs, :]
        p, p_new = softmax_with_new(sw, w_new)
        acc = (lax.dot_general(p.astype(cd), vw_t.astype(cd), nt, preferred_element_type=f32)
               + bf(p_new) * bf(v_new))
        o_ref[0, hs, :] = o_ref[0, hs, :] + gates[:, 1:2] * sel_all[hs, :][:, lanes] + gates[:, 2:3] * acc
        for s, (tile, new_row) in enumerate(((kw_t, k_new), (vw_t, v_new))):
            col = jnp.sum(jnp.where(eye, jnp.broadcast_to(new_row, (HEAD_DIM, HEAD_DIM)), 0.0),
                          axis=1, keepdims=True)
            nwin_ref[0, s, k] = jnp.where(last, col, pltpu.roll(tile, wb - 1, 1))


def _nsa_decode_layer(x, cmp_rows, sel_kvt, win_buf, ln_g, w_in, w_out, q_gain, k_gain, pe, w1, w2, rel_bias):
    N, L, D = x.shape
    P = sel_kvt.shape[-1]
    wb = win_buf.shape[1]
    assert L == 1 and P % SEL_BLOCK == 0 and wb == WINDOW and P >= BIAS_SPAN and P // SEL_BLOCK >= N_SEL - 1
    f32 = jnp.float32
    cd = MXU_DTYPE
    nc = P // CMP_STRIDE
    nb = P // SEL_BLOCK
    x2 = x.reshape(N, D)
    q, rows, wnew, gates = _nsa_project(x2, ln_g, w_in, q_gain, k_gain, N, 1)
    cmp = _compress(cmp_rows, pe, w1, w2, k_gain[0])
    qs =(q.reshape(N, N_HEADS, HEAD_DIM) * HEAD_DIM ** -0.5).astype(cd)
    own = (np.arange(N_HEADS)[:, None] // Q_PER_KV == np.arange(SLOT)[None, :] // HEAD_DIM)
    qbd = jnp.where(own[None], jnp.tile(qs, (1, 1, KV_HEADS)), jnp.zeros((), cd))
    win = win_buf.transpose(0, 2, 3, 4, 1)
    onehot = jnp.asarray(np.arange(nb)[:, None] == (np.arange(P) // SEL_BLOCK)[None, :], dtype=cd)
    grp = jnp.asarray((np.arange(nc) // CMP_PER_SEL)[:, None] == np.arange(nb)[None, :], dtype=cd)
    fd = rel_bias[_bucket_of_distance()].T
    fd = fd - fd[:, BIAS_SPAN - 1:]
    dc = P - (np.arange(nc) * CMP_STRIDE + CMP_LEN - 1)
    b_cmp = jnp.where(dc >= 0, jnp.take(fd, np.clip(dc, 0, BIAS_SPAN - 1), axis=1), NEG)
    b_sel = jnp.take(fd, np.clip(P - np.arange(P), 0, BIAS_SPAN - 1), axis=1)
    b_win = jnp.take(fd, np.clip(wb - np.arange(wb), 0, BIAS_SPAN - 1), axis=1)
    b_new = fd[:, 0:1]
    seq3 = lambda n: (n, 0, 0)
    fixed = lambda n: (0, 0)
    win_spec = pl.BlockSpec((1, 2, KV_HEADS, HEAD_DIM, wb), lambda n: (n, 0, 0, 0, 0))
    o, new_win = pl.pallas_call(
        functools.partial(_nsa_decode_kernel, nb=nb),
        grid=(N,),
        in_specs=[pl.BlockSpec((1, N_HEADS, HEAD_DIM), seq3),
                  pl.BlockSpec((1, N_HEADS, SLOT), seq3),
                  pl.BlockSpec((1, 1, nc, SLOT), lambda n: (0, n, 0, 0)),
                  pl.BlockSpec((1, 1, nc, SLOT), lambda n: (1, n, 0, 0)),
                  pl.BlockSpec((1, 1, KV_HEADS, HEAD_DIM, P), lambda n: (0, n, 0, 0, 0)),
                  pl.BlockSpec((1, 1, KV_HEADS, HEAD_DIM, P), lambda n: (1, n, 0, 0, 0)),
                  win_spec,
                  pl.BlockSpec((1, 1, 4 * SLOT), seq3),
                  pl.BlockSpec((1, 1, 2 * SLOT), seq3),
                  pl.BlockSpec((1, N_HEADS, 3), seq3),
                  pl.BlockSpec((nb, P), fixed), pl.BlockSpec((nc, nb), fixed), pl.BlockSpec((N_HEADS, nc), fixed),
                  pl.BlockSpec((N_HEADS, P), fixed), pl.BlockSpec((N_HEADS, wb), fixed),
                  pl.BlockSpec((N_HEADS, 1), fixed)],
        out_specs=[pl.BlockSpec((1, N_HEADS, HEAD_DIM), seq3), win_spec],
        out_shape=[jax.ShapeDtypeStruct((N, N_HEADS, HEAD_DIM), f32), jax.ShapeDtypeStruct(win.shape, win.dtype)],
        scratch_shapes=[pltpu.VMEM((N_HEADS, nb), f32)],
        compiler_params=pltpu.CompilerParams(dimension_semantics=("arbitrary",),
                                             vmem_limit_bytes=VMEM_LIMIT_BYTES),
        name="nsa_decode_attn",
    )(qs, qbd, cmp, cmp, sel_kvt, sel_kvt, win, rows.reshape(N, 1, 4 * SLOT), wnew.reshape(N, 1, 2 * SLOT),
      gates[:, :3 * N_HEADS].reshape(N, N_HEADS, 3), onehot, grp, b_cmp, b_sel, b_win, b_new)
    y = _proj_residual(x2, o.reshape(N, HQ), w_out)
    new_win = new_win.transpose(0, 4, 1, 2, 3)
    return y.reshape(N, 1, D), rows.reshape(N, 1, 4, KV_HEADS, HEAD_DIM), new_win


HQ = N_HEADS * HEAD_DIM
SLOT = KV_HEADS * HEAD_DIM
NSA_COLS = HQ + 6 * SLOT + 3 * N_HEADS
NSA_COLS_PAD = -(-NSA_COLS // LANES) * LANES
GATE_COLS = NSA_COLS_PAD - HQ - 6 * SLOT


def _nsa_proj_kernel(x_ref, g_ref, w_ref, qg_ref, kg_ref, *outs, prompt, n_blocks):
    f32 = jnp.float32
    x = x_ref[...]
    xn = x * lax.rsqrt(jnp.mean(x * x, axis=-1, keepdims=True) + EPS) * g_ref[...]
    h = jnp.dot(xn.astype(MXU_DTYPE), w_ref[...], preferred_element_type=f32)

    def head_norm(v, gain):
        return v * lax.rsqrt(jnp.mean(v * v, axis=-1, keepdims=True) + EPS) * gain

    def slot(s, kv):
        lo = HQ + s * SLOT + kv * HEAD_DIM
        return h[:, lo:lo + HEAD_DIM]

    gates = 1.0 / (1.0 + jnp.exp(-h[:, HQ + 6 * SLOT:]))
    if prompt:
        q_ref, rows_ref, win_ref, kat_ref, vsa_ref, kwt_ref, vwa_ref, c_ref, gate_ref = outs
        for hh in range(N_HEADS):
            qn = head_norm(h[:, hh * HEAD_DIM:(hh + 1) * HEAD_DIM], qg_ref[...])
            q_ref[0, hh] = (qn * (HEAD_DIM ** -0.5 * LOG2E)).astype(MXU_DTYPE)
        c_ref[0, 0] = h[:, HQ:HQ + SLOT].astype(MXU_DTYPE)
        c_ref[1, 0] = h[:, HQ + SLOT:HQ + 2 * SLOT].astype(MXU_DTYPE)
        rows, ka = x.shape[0], kat_ref.shape[3]
        nb = n_blocks
        first_key = (pl.program_id(0) % (nb * SEL_BLOCK // rows)) * rows
        key_blk = (first_key + lax.broadcasted_iota(jnp.int32, (nb, rows), 1)) // SEL_BLOCK
        onehot = jnp.where(key_blk == lax.broadcasted_iota(jnp.int32, (nb, rows), 0), 1.0, 0.0).astype(MXU_DTYPE)
        kpad = jnp.zeros((ka - nb - HEAD_DIM, rows), MXU_DTYPE)
        ones = jnp.ones((rows, LANES - HEAD_DIM), MXU_DTYPE)
    else:
        q_ref, rows_ref, win_ref, gate_ref = outs
        for hh in range(N_HEADS):
            q_ref[:, hh * HEAD_DIM:(hh + 1) * HEAD_DIM] = head_norm(h[:, hh * HEAD_DIM:(hh + 1) * HEAD_DIM],
                                                                    qg_ref[...])
    if prompt:
        rows_ref[0, 0:2 * SLOT, :] = h[:, HQ:HQ + 2 * SLOT].T
        rows_ref[0, 3 * SLOT:4 * SLOT, :] = h[:, HQ + 3 * SLOT:HQ + 4 * SLOT].T
    else:
        rows_ref[:, 0:2 * SLOT] = h[:, HQ:HQ + 2 * SLOT]
        rows_ref[:, 3 * SLOT:4 * SLOT] = h[:, HQ + 3 * SLOT:HQ + 4 * SLOT]
    for kv in range(KV_HEADS):
        lanes = slice(kv * HEAD_DIM, (kv + 1) * HEAD_DIM)
        ksn = head_norm(slot(2, kv), kg_ref[1:2, :])
        kwn = head_norm(slot(4, kv), kg_ref[2:3, :])
        if prompt:
            ksn_t = ksn.T
            rows_ref[0, 2 * SLOT + kv * HEAD_DIM:2 * SLOT + (kv + 1) * HEAD_DIM, :] = ksn_t
            win_ref[0, 0, :, lanes] = kwn
            kat_ref[0, kv, 0] = jnp.concatenate([onehot, ksn_t.astype(MXU_DTYPE), kpad], axis=0)
            kwt_ref[0, kv, 0] = kwn.T.astype(MXU_DTYPE)
            vsa_ref[0, kv, 0] = jnp.concatenate([slot(3, kv).astype(MXU_DTYPE), ones], axis=1)
            vwa_ref[0, kv, 0] = jnp.concatenate([slot(5, kv).astype(MXU_DTYPE), ones], axis=1)
        else:
            rows_ref[:, 2 * SLOT + kv * HEAD_DIM:2 * SLOT + (kv + 1) * HEAD_DIM] = ksn
            win_ref[:, lanes] = kwn
    if prompt:
        win_ref[0, 0, :, SLOT:2 * SLOT] = h[:, HQ + 5 * SLOT:HQ + 6 * SLOT]
    else:
        win_ref[:, SLOT:2 * SLOT] = h[:, HQ + 5 * SLOT:HQ + 6 * SLOT]
    if prompt:
        gate_ref[...] = jnp.zeros(gate_ref.shape, f32)
        per_kv = 3 * Q_PER_KV
        for kv in range(KV_HEADS):
            gate_ref[:, kv * LANES:kv * LANES + per_kv] = gates[:, kv * per_kv:(kv + 1) * per_kv]
    else:
        gate_ref[...] = gates


def _nsa_project(x, ln_g, w_in, q_gain, k_gain, batch, seq):
    T, D = x.shape
    f32 = jnp.float32
    cd = MXU_DTYPE
    prompt = seq > 1
    tb = TK if prompt else T
    w = jnp.pad(w_in, ((0, 0), (0, NSA_COLS_PAD - NSA_COLS))).astype(cd)
    fixed = lambda i: (0, 0)
    row = lambda i: (i, 0)
    in_specs = [pl.BlockSpec((tb, D), row), pl.BlockSpec((1, D), fixed), pl.BlockSpec((D, NSA_COLS_PAD), fixed),
                pl.BlockSpec((1, HEAD_DIM), fixed), pl.BlockSpec((3, HEAD_DIM), fixed)]
    if prompt:
        assert seq % tb == 0 and WINDOW == 2 * tb
        n = seq // tb
        n_blocks = seq // SEL_BLOCK
        ka = -(-(n_blocks + HEAD_DIM) // LANES) * LANES
        chunk = lambda i: (i // n, 0, i % n, 0, 0)
        kt_spec = pl.BlockSpec((1, KV_HEADS, 1, HEAD_DIM, tb), chunk)
        ka_spec = pl.BlockSpec((1, KV_HEADS, 1, ka, tb), chunk)
        v_spec = pl.BlockSpec((1, KV_HEADS, 1, tb, LANES), chunk)
        kt_shape = jax.ShapeDtypeStruct((batch, KV_HEADS, n, HEAD_DIM, tb), cd)
        ka_shape = jax.ShapeDtypeStruct((batch, KV_HEADS, n, ka, tb), cd)
        v_shape = jax.ShapeDtypeStruct((batch, KV_HEADS, n, tb, LANES), cd)
        out_specs = [
            pl.BlockSpec((1, N_HEADS, tb, HEAD_DIM), lambda i: (i // n, 0, i % n, 0)),
            pl.BlockSpec((1, 4 * SLOT, tb), lambda i: (i // n, 0, i % n)),
            pl.BlockSpec((1, 1, tb, 2 * SLOT), lambda i: (i // n, jnp.maximum(i % n - (n - 2), 0), 0, 0)),
            ka_spec, v_spec, kt_spec, v_spec,
            pl.BlockSpec((2, 1, tb, SLOT), lambda i: (0, i // n, i % n, 0)),
            pl.BlockSpec((tb, KV_HEADS * LANES), row)]
        out_shape = [
            jax.ShapeDtypeStruct((batch, N_HEADS, seq, HEAD_DIM), cd),
            jax.ShapeDtypeStruct((batch, 4 * SLOT, seq), f32),
            jax.ShapeDtypeStruct((batch, 2, tb, 2 * SLOT), f32),
            ka_shape, v_shape, kt_shape, v_shape,
            jax.ShapeDtypeStruct((2, batch, seq, SLOT), cd),
            jax.ShapeDtypeStruct((T, KV_HEADS * LANES), f32)]
    else:
        out_specs = [pl.BlockSpec((tb, HQ), row), pl.BlockSpec((tb, 4 * SLOT), row),
                     pl.BlockSpec((tb, 2 * SLOT), row), pl.BlockSpec((tb, GATE_COLS), row)]
        n_blocks = 0
        out_shape = [jax.ShapeDtypeStruct((T, HQ), f32), jax.ShapeDtypeStruct((T, 4 * SLOT), f32),
                     jax.ShapeDtypeStruct((T, 2 * SLOT), f32), jax.ShapeDtypeStruct((T, GATE_COLS), f32)]
    return pl.pallas_call(
        functools.partial(_nsa_proj_kernel, prompt=prompt, n_blocks=n_blocks),
        grid=(T // tb,),
        in_specs=in_specs, out_specs=out_specs, out_shape=out_shape,
        compiler_params=pltpu.CompilerParams(dimension_semantics=("arbitrary",),
                                             vmem_limit_bytes=VMEM_LIMIT_BYTES),
        name="nsa_proj",
    )(x, ln_g.reshape(1, D), w, q_gain.reshape(1, HEAD_DIM), k_gain)


def _compress_kernel(c_ref, w1_ref, pe_ref, w2_ref, kg_ref, o_ref):
    f32 = jnp.float32
    pairs = SLOT // LANES
    c = c_ref[0, 0]
    u = [jnp.dot(jnp.concatenate([c[:, l * SLOT + p * LANES:l * SLOT + (p + 1) * LANES]
                                  for l in range(CMP_STRIDE)], axis=1).astype(MXU_DTYPE),
                 w1_ref[0], preferred_element_type=f32) for p in range(pairs)]
    first = jnp.concatenate([up[:, :LANES] for up in u], axis=1)
    second = jnp.concatenate([up[:, LANES:] for up in u], axis=1)
    pe = jnp.dot(pe_ref[0], w1_ref[0], preferred_element_type=f32)
    pe = jnp.concatenate([pe[0:1, :LANES] + pe[1:2, LANES:]] * pairs, axis=1)
    nc = first.shape[0]
    hid = first + pltpu.roll(second, nc - 1, 0) + pe
    out = jnp.dot(_gelu_tanh(hid).astype(MXU_DTYPE), w2_ref[0], preferred_element_type=f32)
    is_key = pl.program_id(1) == 0
    for kv in range(KV_HEADS):
        lanes = slice(kv * HEAD_DIM, (kv + 1) * HEAD_DIM)
        v = out[:, lanes]
        vn = v * lax.rsqrt(jnp.mean(v * v, axis=-1, keepdims=True) + EPS) * kg_ref[...]
        o_ref[0, 0, :, lanes] = jnp.where(is_key, vn, v)


def _compress(c, pe, w1, w2, k_gain0):
    n, nc = c.shape[1:3]
    cd = MXU_DTYPE
    kc = CMP_STRIDE * SLOT
    eye = jnp.eye(KV_HEADS, dtype=jnp.float32)
    per_tile = LANES // HEAD_DIM
    kp = CMP_STRIDE * LANES
    w1h = w1.reshape(2, 2, CMP_STRIDE, HEAD_DIM, CMP_HID)
    w1b = jnp.einsum('shldf,kj->slkdhjf', w1h, eye[:per_tile, :per_tile]).reshape(2, kp, 2 * LANES).astype(cd)
    w2b = jnp.einsum('sdf,kj->skdjf', w2, eye).reshape(2, SLOT, SLOT).astype(cd)
    peh = jnp.broadcast_to(pe.reshape(2, 2, CMP_STRIDE, 1, HEAD_DIM), (2, 2, CMP_STRIDE, per_tile, HEAD_DIM))
    peh = jnp.pad(peh.reshape(2, 2, kp), ((0, 0), (0, 14), (0, 0))).astype(cd)
    return pl.pallas_call(
        _compress_kernel,
        grid=(n, 2),
        in_specs=[pl.BlockSpec((1, 1, nc, kc), lambda b, s: (s, b, 0, 0)),
                  pl.BlockSpec((1, kp, 2 * LANES), lambda b, s: (s, 0, 0)),
                  pl.BlockSpec((1, 16, kp), lambda b, s: (s, 0, 0)),
                  pl.BlockSpec((1, SLOT, SLOT), lambda b, s: (s, 0, 0)),
                  pl.BlockSpec((1, HEAD_DIM), lambda b, s: (0, 0))],
        out_specs=pl.BlockSpec((1, 1, nc, SLOT), lambda b, s: (s, b, 0, 0)),
        out_shape=jax.ShapeDtypeStruct((2, n, nc, SLOT), jnp.float32),
        compiler_params=pltpu.CompilerParams(dimension_semantics=("arbitrary", "arbitrary"),
                                             vmem_limit_bytes=VMEM_LIMIT_BYTES),
        name="nsa_compress",
    )(c, w1b, peh, w2b, k_gain0.reshape(1, HEAD_DIM))


def _proj_residual_kernel(x_ref, a_ref, w_ref, o_ref):
    o_ref[...] = x_ref[...] + jnp.dot(a_ref[...].astype(MXU_DTYPE), w_ref[...],
                                      preferred_element_type=jnp.float32)


def _proj_residual(x, a, w):
    T, D = x.shape
    k = a.shape[1]
    tb = min(512, T)
    assert T % tb == 0
    return pl.pallas_call(
        _proj_residual_kernel,
        grid=(T // tb,),
        in_specs=[pl.BlockSpec((tb, D), lambda i: (i, 0)), pl.BlockSpec((tb, k), lambda i: (i, 0)),
                  pl.BlockSpec((k, D), lambda i: (0, 0))],
        out_specs=pl.BlockSpec((tb, D), lambda i: (i, 0)),
        out_shape=jax.ShapeDtypeStruct((T, D), jnp.float32),
        compiler_params=pltpu.CompilerParams(dimension_semantics=("arbitrary",)),
        name="proj_residual",
    )(x, a, w.astype(MXU_DTYPE))


MOE_ROWS = 256
ROUTE_COLS = LANES
ROW_DMA_UNROLL = 8


def _moe_route_kernel(x_ref, g_ref, w_ref, b_ref, tri_ref, h_ref, meta_ref, wt_ref, cnt_ref, carry_sc):
    f32 = jnp.float32
    step = pl.program_id(0)

    @pl.when(step == 0)
    def _():
        carry_sc[...] = jnp.zeros(carry_sc.shape, f32)

    x = x_ref[...]
    h = x * lax.rsqrt(jnp.mean(x * x, axis=-1, keepdims=True) + EPS) * g_ref[...]
    h_ref[...] = h
    logits = jnp.dot(h.astype(MXU_DTYPE), w_ref[...], preferred_element_type=f32) + b_ref[...]
    tb = logits.shape[0]
    col = lax.broadcasted_iota(jnp.int32, (tb, ROUTE_COLS), 1)

    def first_max(vals):
        best = jnp.max(vals, axis=1, keepdims=True)
        return best, jnp.min(jnp.where(vals == best, col, ROUTE_COLS), axis=1, keepdims=True)

    lg = jnp.where(col < N_GROUPS, logits, -jnp.inf)
    g_best, grp = first_max(lg)
    g_w = 1.0 / jnp.sum(jnp.exp(lg - g_best), axis=1, keepdims=True)
    lo = N_GROUPS + EXPERTS_PER_GROUP * grp
    le = jnp.where((col >= lo) & (col < lo + EXPERTS_PER_GROUP), logits, -jnp.inf)
    v0, c0 = first_max(le)
    v1, c1 = first_max(jnp.where(col == c0, -jnp.inf, le))
    e1 = jnp.exp(v1 - v0)
    w0 = g_w / (1.0 + e1)
    w1 = g_w * e1 / (1.0 + e1)
    chosen = (col == c0) | (col == c1)
    before = jnp.dot(tri_ref[...], jnp.where(chosen, 1.0, 0.0).astype(MXU_DTYPE),
                     preferred_element_type=f32) + carry_sc[...]
    r0 = jnp.sum(jnp.where(col == c0, before, 0.0), axis=1, keepdims=True).astype(jnp.int32)
    r1 = jnp.sum(jnp.where(col == c1, before, 0.0), axis=1, keepdims=True).astype(jnp.int32)
    carry_sc[...] += jnp.sum(jnp.where(chosen, 1.0, 0.0), axis=0, keepdims=True)
    cnt_ref[...] = carry_sc[...]
    meta_ref[...] = jnp.where(col == 0, c0 - N_GROUPS, jnp.where(col == 1, c1 - N_GROUPS,
                              jnp.where(col == 2, r0, jnp.where(col == 3, r1, 0))))
    wt_ref[...] = jnp.where(col == 0, w0, jnp.where(col == 1, w1, 0.0))


def _moe_dispatch_kernel(tail_ref, used_ref, dest_ref, h_ref, xbuf_ref, zero_sc, sem, zero_sem):
    tb = h_ref.shape[0]

    @pl.when(pl.program_id(0) == 0)
    def _():
        zero_sc[...] = jnp.zeros(zero_sc.shape, zero_sc.dtype)

        def clear(e):
            first_row = pl.multiple_of(jnp.maximum(tail_ref[e], 0), MOE_ROWS)
            return pltpu.make_async_copy(zero_sc, xbuf_ref.at[pl.ds(first_row, MOE_ROWS)], zero_sem)

        for e in range(N_EXPERTS):
            @pl.when(tail_ref[e] >= 0)
            def _():
                clear(e).start()
        for e in range(N_EXPERTS):
            @pl.when(tail_ref[e] >= 0)
            def _():
                clear(e).wait()

        def unused(i):
            return pltpu.make_async_copy(zero_sc, xbuf_ref.at[pl.ds(pl.multiple_of(i * MOE_ROWS, MOE_ROWS), MOE_ROWS)],
                                         zero_sem)

        n_all = xbuf_ref.shape[0] // MOE_ROWS
        lax.fori_loop(used_ref[0], n_all, lambda i, c: (unused(i).start(), c)[1], 0)
        lax.fori_loop(used_ref[0], n_all, lambda i, c: (unused(i).wait(), c)[1], 0)

    def row_copy(r, k):
        return pltpu.make_async_copy(h_ref.at[pl.ds(r, 1)], xbuf_ref.at[pl.ds(dest_ref[0, 0, 2 * r + k], 1)], sem)

    def start(r, c):
        row_copy(r, 0).start()
        row_copy(r, 1).start()
        return c

    def wait(r, c):
        row_copy(r, 0).wait()
        row_copy(r, 1).wait()
        return c

    lax.fori_loop(0, tb, start, 0, unroll=ROW_DMA_UNROLL)
    lax.fori_loop(0, tb, wait, 0, unroll=ROW_DMA_UNROLL)


def _moe_expert_kernel(blk_e_ref, nblk_ref, x_ref, wgu_ref, wdn_ref, y_ref, wgu_sc, wdn_sc):
    i = pl.program_id(0)
    f32 = jnp.float32

    @pl.when(i < nblk_ref[0])
    def _():
        changed = jnp.logical_or(i == 0, blk_e_ref[i] != blk_e_ref[jnp.maximum(i - 1, 0)])

        @pl.when(changed)
        def _():
            wgu_sc[...] = wgu_ref[0, 0].astype(MXU_DTYPE)
            wdn_sc[...] = wdn_ref[0, 0].astype(MXU_DTYPE)

        gu = jnp.dot(x_ref[...].astype(MXU_DTYPE), wgu_sc[...], preferred_element_type=f32)
        gate = gu[:, :D_EXPERT]
        act = gate * (1.0 / (1.0 + jnp.exp(-gate))) * gu[:, D_EXPERT:]
        y_ref[...] = jnp.dot(act.astype(MXU_DTYPE), wdn_sc[...], preferred_element_type=f32)

    @pl.when(i >= nblk_ref[0])
    def _():
        y_ref[...] = jnp.zeros(y_ref.shape, f32)


def _moe_combine_kernel(dest_ref, x_ref, wt_ref, ybuf_ref, o_ref, rows_sc, sem):
    tb = x_ref.shape[0]

    def row_copy(r, k):
        return pltpu.make_async_copy(ybuf_ref.at[pl.ds(dest_ref[0, 0, 2 * r + k], 1)],
                                     rows_sc.at[k, pl.ds(r, 1)], sem)

    def start(r, c):
        row_copy(r, 0).start()
        row_copy(r, 1).start()
        return c

    def wait(r, c):
        row_copy(r, 0).wait()
        row_copy(r, 1).wait()
        return c

    lax.fori_loop(0, tb, start, 0, unroll=ROW_DMA_UNROLL)
    lax.fori_loop(0, tb, wait, 0, unroll=ROW_DMA_UNROLL)
    wt = wt_ref[...]
    o_ref[...] = x_ref[...] + (wt[:, 0:1] * rows_sc[0] + wt[:, 1:2] * rows_sc[1])


def _hier_moe_residual(x, ln_g, w_grp, b_grp, w_exp, b_exp, w_gu, w_dn, layer):
    T, D = x.shape
    f32 = jnp.float32
    tb = min(256, T)
    assert T % tb == 0
    nt = T // tb
    pad_cols = ROUTE_COLS - N_GROUPS - N_EXPERTS
    w_r = jnp.pad(jnp.concatenate([w_grp, w_exp], axis=1), ((0, 0), (0, pad_cols))).astype(MXU_DTYPE)
    b_r = jnp.pad(jnp.concatenate([b_grp, b_exp]), (0, pad_cols)).reshape(1, ROUTE_COLS)
    tri = jnp.asarray(np.tril(np.ones((tb, tb), np.float32), -1), dtype=MXU_DTYPE)
    row = lambda i: (i, 0)
    fixed = lambda i: (0, 0)
    h, meta, wt, cnt = pl.pallas_call(
        _moe_route_kernel,
        grid=(nt,),
        in_specs=[pl.BlockSpec((tb, D), row), pl.BlockSpec((1, D), fixed),
                  pl.BlockSpec((D, ROUTE_COLS), fixed), pl.BlockSpec((1, ROUTE_COLS), fixed),
                  pl.BlockSpec((tb, tb), fixed)],
        out_specs=[pl.BlockSpec((tb, D), row), pl.BlockSpec((tb, ROUTE_COLS), row),
                   pl.BlockSpec((tb, ROUTE_COLS), row), pl.BlockSpec((1, ROUTE_COLS), fixed)],
        out_shape=[jax.ShapeDtypeStruct((T, D), f32), jax.ShapeDtypeStruct((T, ROUTE_COLS), jnp.int32),
                   jax.ShapeDtypeStruct((T, ROUTE_COLS), f32), jax.ShapeDtypeStruct((1, ROUTE_COLS), f32)],
        scratch_shapes=[pltpu.VMEM((1, ROUTE_COLS), f32)],
        compiler_params=pltpu.CompilerParams(dimension_semantics=("arbitrary",)),
        name="moe_route",
    )(x, ln_g.reshape(1, D), w_r, b_r, tri)

    counts = cnt[0, N_GROUPS:N_GROUPS + N_EXPERTS].astype(jnp.int32)
    padded = (counts + MOE_ROWS - 1) // MOE_ROWS * MOE_ROWS
    pad_end = jnp.cumsum(padded)
    pad_start = pad_end - padded
    n_blocks = -(-(T * TOP_K) // MOE_ROWS) + N_EXPERTS
    n_slots = n_blocks * MOE_ROWS
    first_row = jnp.sum(jnp.where(meta[:, 0:2, None] == jnp.arange(N_EXPERTS, dtype=jnp.int32), pad_start, 0),
                        axis=-1)
    dest = first_row + meta[:, 2:4]
    dest = dest.reshape(nt, 1, 2 * tb)
    starts = jnp.arange(n_blocks, dtype=jnp.int32) * MOE_ROWS
    used = (pad_end[-1] // MOE_ROWS).astype(jnp.int32).reshape(1)
    blk_start = jnp.minimum(starts, pad_end[-1] - 1)
    blk_e = jnp.minimum(jnp.sum(pad_end[None, :] <= blk_start[:, None], axis=1), N_EXPERTS - 1).astype(jnp.int32)

    dest_spec = pl.BlockSpec((1, 1, 2 * tb), lambda i: (i, 0, 0), memory_space=pltpu.SMEM)
    tail = jnp.where(padded > 0, pad_end - MOE_ROWS, -1).astype(jnp.int32)
    xbuf = pl.pallas_call(
        _moe_dispatch_kernel,
        grid_spec=pltpu.PrefetchScalarGridSpec(
            num_scalar_prefetch=2,
            grid=(nt,),
            in_specs=[pl.BlockSpec((1, 1, 2 * tb), lambda i, tl, nu: (i, 0, 0), memory_space=pltpu.SMEM),
                      pl.BlockSpec((tb, D), lambda i, tl, nu: (i, 0))],
            out_specs=pl.BlockSpec(memory_space=pl.ANY),
            scratch_shapes=[pltpu.VMEM((MOE_ROWS, D), f32), pltpu.SemaphoreType.DMA(()),
                            pltpu.SemaphoreType.DMA(())]),
        out_shape=jax.ShapeDtypeStruct((n_slots, D), f32),
        compiler_params=pltpu.CompilerParams(dimension_semantics=("arbitrary",)),
        name="moe_dispatch",
    )(tail, used, dest, h)

    ybuf = pl.pallas_call(
        _moe_expert_kernel,
        grid_spec=pltpu.PrefetchScalarGridSpec(
            num_scalar_prefetch=2,
            grid=(n_blocks,),
            in_specs=[pl.BlockSpec((MOE_ROWS, D), lambda i, be, nb: (jnp.minimum(i, nb[0] - 1), 0)),
                      pl.BlockSpec((1, 1, D, 2 * D_EXPERT), lambda i, be, nb: (layer, be[i], 0, 0)),
                      pl.BlockSpec((1, 1, D_EXPERT, D), lambda i, be, nb: (layer, be[i], 0, 0))],
            out_specs=pl.BlockSpec((MOE_ROWS, D), lambda i, be, nb: (i, 0)),
            scratch_shapes=[pltpu.VMEM((D, 2 * D_EXPERT), MXU_DTYPE), pltpu.VMEM((D_EXPERT, D), MXU_DTYPE)]),
        out_shape=jax.ShapeDtypeStruct((n_slots, D), f32),
        compiler_params=pltpu.CompilerParams(dimension_semantics=("arbitrary",),
                                             vmem_limit_bytes=VMEM_LIMIT_BYTES),
        name="moe_experts",
    )(blk_e, used, xbuf, w_gu, w_dn)

    return pl.pallas_call(
        _moe_combine_kernel,
        grid=(nt,),
        in_specs=[dest_spec, pl.BlockSpec((tb, D), row), pl.BlockSpec((tb, ROUTE_COLS), row),
                  pl.BlockSpec(memory_space=pl.ANY)],
        out_specs=pl.BlockSpec((tb, D), row),
        out_shape=jax.ShapeDtypeStruct((T, D), f32),
        scratch_shapes=[pltpu.VMEM((TOP_K, tb, D), f32), pltpu.SemaphoreType.DMA(())],
        compiler_params=pltpu.CompilerParams(dimension_semantics=("arbitrary",)),
        name="moe_combine",
    )(dest, x, wt, ybuf)


def _gelu_tanh(x):
    return 0.5 * x * (1.0 + jnp.tanh(math.sqrt(2.0 / math.pi) * (x + 0.044715 * (x * x * x))))


def _gmlp_kernel(x_ref, g_ref, win_ref, bin_ref, lng_ref, lnb_ref, ws_ref, bs_ref, wout_ref,
                 o_ref, v_ref, *, single_position):
    f32 = jnp.float32
    x = x_ref[...]
    rows = x.shape[0]
    h = x * lax.rsqrt(jnp.mean(x * x, axis=-1, keepdims=True) + EPS) * g_ref[...]
    z = _gelu_tanh(jnp.dot(h.astype(MXU_DTYPE), win_ref[...], preferred_element_type=f32) + bin_ref[...])
    u = z[:, :GM_WIDTH]
    v = z[:, GM_WIDTH:]
    mu = jnp.mean(v, axis=-1, keepdims=True)
    var = jnp.mean(jnp.square(v - mu), axis=-1, keepdims=True)
    v = (v - mu) * lax.rsqrt(var + EPS) * lng_ref[...] + lnb_ref[...]
    v_ref[0] = v
    vb = v.astype(MXU_DTYPE)
    if single_position:
        s = ws_ref[...].astype(f32) * vb.astype(f32) + bs_ref[...]
    else:
        parts = []
        for c in range(rows // CHUNK):
            vc = vb[c * CHUNK:(c + 1) * CHUNK]
            parts.append(jnp.concatenate(
                [jnp.dot(ws_ref[g], vc[:, g * GM_GROUP_DIM:(g + 1) * GM_GROUP_DIM], preferred_element_type=f32)
                 + bs_ref[g] for g in range(GM_GROUPS)], axis=1))
        s = jnp.concatenate(parts, axis=0)
    y = jnp.dot((u * s).astype(MXU_DTYPE), wout_ref[...], preferred_element_type=f32)
    o_ref[...] = x + y


def _gmlp_residual(x, ln_g, w_in, b_in, ln2_g, ln2_b, w_s, b_s, w_out, seq):
    T, D = x.shape
    f32 = jnp.float32
    cd = MXU_DTYPE
    single = seq == 1
    tb = T if single else 2 * CHUNK
    assert T % tb == 0 and (single or seq % tb == 0)
    steps_per_seq = 1 if single else seq // tb
    if single:
        ws = jnp.repeat(w_s[:, 0, 0], GM_GROUP_DIM).reshape(1, GM_WIDTH).astype(cd)
        bs = jnp.repeat(b_s[:, 0], GM_GROUP_DIM).reshape(1, GM_WIDTH)
        ws_spec = pl.BlockSpec((1, GM_WIDTH), lambda i: (0, 0))
        bs_spec = pl.BlockSpec((1, GM_WIDTH), lambda i: (0, 0))
    else:
        causal = np.tril(np.ones((CHUNK, CHUNK), bool))
        ws = jnp.where(causal[None], w_s, 0).astype(cd)
        bs = jnp.broadcast_to(b_s[:, :, None], (GM_GROUPS, CHUNK, GM_GROUP_DIM))
        ws_spec = pl.BlockSpec((GM_GROUPS, CHUNK, CHUNK), lambda i: (0, 0, 0))
        bs_spec = pl.BlockSpec((GM_GROUPS, CHUNK, GM_GROUP_DIM), lambda i: (0, 0, 0))
    fixed = lambda i: (0, 0)
    row = lambda i: (i, 0)
    return pl.pallas_call(
        functools.partial(_gmlp_kernel, single_position=single),
        grid=(T // tb,),
        in_specs=[pl.BlockSpec((tb, D), row), pl.BlockSpec((1, D), fixed),
                  pl.BlockSpec((D, 2 * GM_WIDTH), fixed), pl.BlockSpec((1, 2 * GM_WIDTH), fixed),
                  pl.BlockSpec((1, GM_WIDTH), fixed), pl.BlockSpec((1, GM_WIDTH), fixed),
                  ws_spec, bs_spec, pl.BlockSpec((GM_WIDTH, D), fixed)],
        out_specs=[pl.BlockSpec((tb, D), row), pl.BlockSpec((1, tb, GM_WIDTH), lambda i: (i // steps_per_seq, 0, 0))],
        out_shape=[jax.ShapeDtypeStruct((T, D), f32),
                   jax.ShapeDtypeStruct((T // (tb * steps_per_seq), tb, GM_WIDTH), f32)],
        compiler_params=pltpu.CompilerParams(dimension_semantics=("arbitrary",),
                                             vmem_limit_bytes=VMEM_LIMIT_BYTES),
        name="gmlp",
    )(x, ln_g.reshape(1, D), w_in.astype(cd), b_in.reshape(1, -1), ln2_g.reshape(1, -1), ln2_b.reshape(1, -1),
      ws, bs, w_out.astype(cd))


def kernel(x_prompt, x_sample, cache_nsa_kv, state_win_kv, page_table, rel_bias, ln_mix, ln_ffn,
           nsa_w_in, nsa_w_out, nsa_q_gain, nsa_k_gain, cmp_pe, cmp_w1, cmp_w2,
           gm_w_in, gm_b_in, gm_ln_g, gm_ln_b, gm_w_s, gm_b_s, gm_w_out,
           moe_w_grp, moe_b_grp, moe_w_exp, moe_b_exp, moe_w_gu, moe_w_dn):
    xp = x_prompt
    xs = x_sample
    kv_p, kv_s, win_p, win_s, gv_p, gv_s = [], [], [], [], [], []
    for i in range(DEPTH):
        a = i // N_MIXERS
        if i % N_MIXERS == 0:
            nsa = (ln_mix[i], nsa_w_in[a], nsa_w_out[a], nsa_q_gain[a], nsa_k_gain[a],
                   cmp_pe[a], cmp_w1[a], cmp_w2[a], rel_bias)
            xp, rp, wp = _nsa_prompt_layer(xp, *nsa)
            cmp_rows, sel_kvt = _page_gather(cache_nsa_kv, page_table, a)
            xs, rs, ws = _nsa_decode_layer(xs, cmp_rows, sel_kvt, state_win_kv[a], *nsa)
            kv_p.append(rp)
            kv_s.append(rs)
            win_p.append(wp)
            win_s.append(ws)
        else:
            gm = (ln_mix[i], gm_w_in[a], gm_b_in[a], gm_ln_g[a], gm_ln_b[a], gm_w_s[a], gm_b_s[a], gm_w_out[a])
            bp, sp = xp.shape[:2]
            bs_, ss = xs.shape[:2]
            xp2, vp = _gmlp_residual(xp.reshape(-1, D_MODEL), *gm, seq=sp)
            xs2, vs = _gmlp_residual(xs.reshape(-1, D_MODEL), *gm, seq=ss)
            xp = xp2.reshape(xp.shape)
            xs = xs2.reshape(xs.shape)
            start = ((sp - 1) // CHUNK) * CHUNK
            gv_p.append(vp[:, vp.shape[1] - (sp - start):])
            gv_s.append(vs.reshape(bs_, ss, GM_WIDTH))
        moe = (ln_ffn[i], moe_w_grp[i], moe_b_grp[i], moe_w_exp[i], moe_b_exp[i], moe_w_gu, moe_w_dn, i)
        xp = _hier_moe_residual(xp.reshape(-1, D_MODEL), *moe).reshape(xp.shape)
        xs = _hier_moe_residual(xs.reshape(-1, D_MODEL), *moe).reshape(xs.shape)
    new_kv_prompt = jnp.stack(kv_p, axis=2)
    new_kv_sample = jnp.stack(kv_s, axis=2)
    new_win_prompt = jnp.stack(win_p, axis=0)
    new_win_sample = jnp.stack(win_s, axis=0)
    new_gm_v_prompt = jnp.stack(gv_p, axis=0)
    new_gm_v_sample = jnp.stack(gv_s, axis=0)
    return (xp, xs, new_kv_prompt, new_kv_sample, new_win_prompt, new_win_sample, new_gm_v_prompt, new_gm_v_sample)
```

```python
import functools
import math

import jax
import jax.numpy as jnp
import numpy as np
from jax import lax
from jax.experimental import pallas as pl
from jax.experimental.pallas import tpu as pltpu

D_MODEL = 1024
PAGE_SIZE = 128
DEPTH = 2
N_MIXERS = 2
N_HEADS = 16
HEAD_DIM = 64
KV_HEADS = 4
Q_PER_KV = N_HEADS // KV_HEADS
CMP_LEN = 32
CMP_STRIDE = 16
CMP_HID = 64
SEL_BLOCK = 64
CMP_PER_SEL = SEL_BLOCK // CMP_STRIDE
N_SEL = 16
N_LOCAL = 2
WINDOW = 512
N_BUCKETS = 32
MAX_DISTANCE = 128
CHUNK = 128
GM_WIDTH = 2048
GM_GROUPS = 8
GM_GROUP_DIM = GM_WIDTH // GM_GROUPS
N_GROUPS = 4
EXPERTS_PER_GROUP = 8
N_EXPERTS = N_GROUPS * EXPERTS_PER_GROUP
TOP_K = 2
D_EXPERT = 512
EPS = 1e-6

LANES = 128
VMEM_LIMIT_BYTES = 56 * 1024 * 1024


NEG = -1e30
MXU_DTYPE = jnp.bfloat16
TQ = 256
TK = 256
GROUP_CHUNKS = 3
NEAR_CHUNKS = 5
BIAS_SPAN = 128
LOG2E = 1.4426950408889634


def _bucket_of_distance():
    d = np.arange(BIAS_SPAN)
    max_exact = N_BUCKETS // 2
    nf = np.maximum(d, 1).astype(np.float32)
    large = max_exact + (np.log(nf / np.float32(max_exact)) / np.float32(math.log(MAX_DISTANCE / max_exact))
                         * np.float32(N_BUCKETS - max_exact)).astype(np.int32)
    large = np.minimum(large, N_BUCKETS - 1)
    return np.where(d < max_exact, d, large).astype(np.int32)


def _attn_bias_tiles(rel_bias, seq):
    fd = rel_bias[_bucket_of_distance()].T
    far = fd[:, BIAS_SPAN - 1]
    fd = fd - far[:, None]

    def by_distance(d):
        return jnp.where(d >= 0, LOG2E * jnp.take(fd, np.clip(d, 0, BIAS_SPAN - 1), axis=1), NEG)

    def toeplitz(first_distance):
        c = np.arange(2 * TK)
        g = by_distance(first_distance - np.where(c < TK, c, c - 2 * TK))
        flat = jnp.tile(g, (1, TQ))[:, :TQ * (2 * TK - 1)]
        return flat.reshape(-1, TQ, 2 * TK - 1)[:, :, :TK]

    b0 = toeplitz(0)
    b1 = toeplitz(TQ)
    i = np.arange(TQ)[:, None]
    j = np.arange(TK)[None, :]
    w2 = np.where(2 * TQ + i - j <= WINDOW, 0.0, NEG).astype(np.float32)
    nc = seq // CMP_STRIDE
    span = -(-(BIAS_SPAN + TQ) // CMP_STRIDE)
    period = 1 << int(np.ceil(np.log2(TQ + 2 * span * CMP_STRIDE)))
    c = np.arange(period)
    g = by_distance(np.where(c < TQ, c, c - period) + span * CMP_STRIDE - (CMP_LEN - 1))
    band = jnp.tile(g, (1, 2 * span))[:, :2 * span * (period - CMP_STRIDE)]
    band = band.reshape(-1, 2 * span, period - CMP_STRIDE)[:, :, :TQ]
    n_heads = fd.shape[0]
    pat = jnp.concatenate([jnp.zeros((n_heads, nc - span, TQ), jnp.float32), band,
                           jnp.full((n_heads, nc - span, TQ), NEG, jnp.float32)], axis=1)
    return b0, b1, jnp.asarray(w2), pat


def _nsa_attn_kernel(q_ref, kc_ref, vct_ref, kat_ref, vsa_ref, kwt_ref, vwa_ref, g_ref,
                     b0_ref, b1_ref, w2_ref, pat_ref, o_ref, m_sc, acc_sc, imp_sc, qa_sc, p_sc, a_sc, *, nc, nb):
    f32 = jnp.float32
    cd = MXU_DTYPE
    qb = pl.program_id(2)

    def compress_and_select(ncu, all_forced):
        nbu = ncu // CMP_PER_SEL
        r0 = pl.multiple_of(nc - (TQ // CMP_STRIDE) * qb, TQ // CMP_STRIDE)
        t_row = qb * TQ + lax.broadcasted_iota(jnp.int32, (1, TQ), 1)
        has_cmp = t_row >= CMP_LEN - 1
        for h in range(Q_PER_KV):
            st = lax.dot_general(kc_ref[0, 0, 0:ncu, :], q_ref[0, h], (((1,), (1,)), ((), ())),
                                 preferred_element_type=f32)
            st = st + pat_ref[h, pl.ds(r0, ncu), :]
            e = jnp.exp2(st - jnp.max(st, axis=0, keepdims=True))
            inv = jnp.where(has_cmp, 1.0 / jnp.sum(e, axis=0, keepdims=True), 0.0)
            pt = e * inv
            for c in range(TQ // LANES):
                part = pt[:, c * LANES:(c + 1) * LANES]
                if h == 0:
                    imp_sc[c, 0:ncu, :] = part
                else:
                    imp_sc[c, 0:ncu, :] += part
            oct_h = jnp.dot(vct_ref[0, 0, :, 0:ncu], pt.astype(cd), preferred_element_type=f32)
            o_ref[0, :, h * HEAD_DIM:(h + 1) * HEAD_DIM] = g_ref[0][:, 3 * h:3 * h + 1] * oct_h.T

        imp = jnp.concatenate(
            [sum(imp_sc[c, pl.ds(r, nbu, stride=CMP_PER_SEL), :] for r in range(CMP_PER_SEL))
             for c in range(TQ // LANES)], axis=1)
        blk = lax.broadcasted_iota(jnp.int32, (nbu, TQ), 0)
        t_blk = (qb * TQ + lax.broadcasted_iota(jnp.int32, (nbu, TQ), 1)) // SEL_BLOCK
        dd = t_blk - blk
        forced = (blk == 0) | ((dd >= 0) & (dd < N_LOCAL))
        if all_forced:
            score = jnp.where(forced | (dd < 0), -jnp.inf, imp)
            pen_t = jnp.where(forced, 0.0, NEG)
            rounds = N_SEL - (N_LOCAL + 1)
        else:
            score = jnp.where(forced, jnp.inf, jnp.where(dd >= 0, imp, -jnp.inf))
            pen_t = jnp.full((nbu, TQ), NEG, f32)
            rounds = min(N_SEL, nbu)
        for _ in range(rounds):
            best = jnp.max(score, axis=0, keepdims=True)
            first = jnp.min(jnp.where(score == best, blk, nbu), axis=0, keepdims=True)
            hit = blk == first
            pen_t = jnp.where(hit, 0.0, pen_t)
            score = jnp.where(hit, -jnp.inf, score)
        if nbu < nb:
            pen_t = jnp.concatenate([pen_t, jnp.full((nb - nbu, TQ), NEG, f32)], axis=0)
        pen = pen_t.T.astype(cd)
        pad = jnp.zeros((TQ, qa_sc.shape[2] - nb - HEAD_DIM), cd)
        for h in range(Q_PER_KV):
            qa_sc[h] = jnp.concatenate([pen, q_ref[0, h], pad], axis=1)

    n_tiles = nc * CMP_STRIDE // TQ
    n_var = 4 if (n_tiles % 4 == 0 and (nc // 4) % (2 * CMP_STRIDE) == 0) else 1
    for v in range(n_var):
        lo, hi = v * n_tiles // n_var, (v + 1) * n_tiles // n_var
        ncu = (v + 1) * nc // n_var
        if v == 0:
            @pl.when(qb == 0)
            def _():
                compress_and_select(ncu, all_forced=False)
            lo = 1

        @pl.when((qb >= lo) & (qb < hi))
        def _():
            compress_and_select(ncu, all_forced=TQ >= N_LOCAL * SEL_BLOCK)


    m_sc[...] = jnp.full(m_sc.shape, NEG, f32)
    acc_sc[...] = jnp.zeros(acc_sc.shape, f32)

    def run(*work):
        jobs = [(br, c, bias) for br, chunks in work for c, bias in chunks]
        for slot, ((b, lhs, k_ref, _), c, bias) in enumerate(jobs):
            for h in range(Q_PER_KV):
                s = jnp.dot(lhs(h), k_ref[0, 0, c], preferred_element_type=f32)
                if bias is not None:
                    s = s + bias(h)
                m_prev = m_sc[b, h]
                m_new = jnp.maximum(m_prev, jnp.max(s, axis=1, keepdims=True))
                p_sc[slot, h] = jnp.exp2(s - jnp.concatenate([m_new] * (TK // LANES), axis=1)).astype(cd)
                a_sc[slot, h] = jnp.exp2(m_prev - m_new)
                m_sc[b, h] = m_new
        for slot, ((b, _, _, v_ref), c, _) in enumerate(jobs):
            for h in range(Q_PER_KV):
                acc_sc[b, h] = a_sc[slot, h] * acc_sc[b, h] + jnp.dot(p_sc[slot, h], v_ref[0, 0, c],
                                                                      preferred_element_type=f32)

    def emit(b, gate_col):
        for h in range(Q_PER_KV):
            acc = acc_sc[b, h]
            w = g_ref[0][:, 3 * h + gate_col:3 * h + gate_col + 1] / acc[:, HEAD_DIM:HEAD_DIM + 1]
            o_ref[0, :, h * HEAD_DIM:(h + 1) * HEAD_DIM] += w * acc[:, :HEAD_DIM]

    own = lambda h: b0_ref[h]
    prev = lambda h: b1_ref[h]
    edge = lambda h: w2_ref[...]

    sel = (0, lambda h: qa_sc[h], kat_ref, vsa_ref)
    win = (1, lambda h: q_ref[0, h], kwt_ref, vwa_ref)
    n_far = jnp.maximum(qb - 1, 0)

    def far_group(i, carry):
        run((sel, [(GROUP_CHUNKS * i + j, None) for j in range(GROUP_CHUNKS)]))
        return carry

    lax.fori_loop(0, n_far // GROUP_CHUNKS, far_group, 0)
    for left in range(1, GROUP_CHUNKS):
        @pl.when(n_far % GROUP_CHUNKS == left)
        def _():
            run((sel, [(n_far - left + j, None) for j in range(left)]))

    @pl.when(qb >= 2)
    def _():
        run((sel, [(qb - 1, prev), (qb, own)]), (win, [(qb - 2, edge), (qb - 1, prev), (qb, own)]))

    @pl.when(qb == 1)
    def _():
        run((sel, [(qb - 1, prev), (qb, own)]), (win, [(qb - 1, prev), (qb, own)]))

    @pl.when(qb == 0)
    def _():
        run((sel, [(qb, own)]), (win, [(qb, own)]))

    emit(0, 1)
    emit(1, 2)


def _nsa_attention(qs, kcp, vct, kat, vsa, kwt, vwa, gt, rel_bias):
    B, _, S, _ = qs.shape
    assert S % TQ == 0 and TQ == TK and WINDOW == 2 * TQ and S // SEL_BLOCK >= N_SEL
    nc = S // CMP_STRIDE
    nb = S // SEL_BLOCK
    nch = S // TK
    ka = kat.shape[3]
    b0, b1, w2, pat = _attn_bias_tiles(rel_bias, S)
    G = Q_PER_KV
    kv_chunks = pl.BlockSpec((1, 1, nch, HEAD_DIM, TK), lambda b, k, i: (b, k, 0, 0, 0))
    ka_chunks = pl.BlockSpec((1, 1, nch, ka, TK), lambda b, k, i: (b, k, 0, 0, 0))
    v_chunks = pl.BlockSpec((1, 1, nch, TK, LANES), lambda b, k, i: (b, k, 0, 0, 0))
    return pl.pallas_call(
        functools.partial(_nsa_attn_kernel, nc=nc, nb=nb),
        grid=(B, KV_HEADS, S // TQ),
        in_specs=[
            pl.BlockSpec((1, G, TQ, HEAD_DIM), lambda b, k, i: (b, k, i, 0)),
            pl.BlockSpec((1, 1, nc, HEAD_DIM), lambda b, k, i: (b, k, 0, 0)),
            pl.BlockSpec((1, 1, HEAD_DIM, nc), lambda b, k, i: (b, k, 0, 0)),
            ka_chunks, v_chunks, kv_chunks, v_chunks,
            pl.BlockSpec((1, TQ, LANES), lambda b, k, i: (b, i, k)),
            pl.BlockSpec((G, TQ, TK), lambda b, k, i: (k, 0, 0)),
            pl.BlockSpec((G, TQ, TK), lambda b, k, i: (k, 0, 0)),
            pl.BlockSpec((TQ, TK), lambda b, k, i: (0, 0)),
            pl.BlockSpec((G, 2 * nc, TQ), lambda b, k, i: (k, 0, 0)),
        ],
        out_specs=pl.BlockSpec((1, TQ, G * HEAD_DIM), lambda b, k, i: (b, i, k)),
        out_shape=jax.ShapeDtypeStruct((B, S, N_HEADS * HEAD_DIM), jnp.float32),
        scratch_shapes=[
            pltpu.VMEM((2, G, TQ, LANES), jnp.float32),
            pltpu.VMEM((2, G, TQ, LANES), jnp.float32),
            pltpu.VMEM((TQ // LANES, nc, LANES), jnp.float32),
            pltpu.VMEM((G, TQ, ka), MXU_DTYPE),
            pltpu.VMEM((NEAR_CHUNKS, G, TQ, TK), MXU_DTYPE),
            pltpu.VMEM((NEAR_CHUNKS, G, TQ, LANES), jnp.float32),
        ],
        compiler_params=pltpu.CompilerParams(
            dimension_semantics=("arbitrary", "arbitrary", "arbitrary"),
            vmem_limit_bytes=VMEM_LIMIT_BYTES),
        name="nsa_attn",
    )(qs, kcp, vct, kat, vsa, kwt, vwa, gt, b0, b1, w2, pat)


def _nsa_prompt_layer(x, ln_g, w_in, w_out, q_gain, k_gain, pe, w1, w2, rel_bias):
    B, S, D = x.shape
    x2 = x.reshape(B * S, D)
    qs, rows, win, kat, vsa, kwt, vwa, c, gates = _nsa_project(x2, ln_g, w_in, q_gain, k_gain, B, S)
    nc = S // CMP_STRIDE
    cmp = _compress(c.reshape(2, B, nc, CMP_STRIDE * SLOT), pe, w1, w2, k_gain[0])
    cmp = cmp.reshape(2, B, nc, KV_HEADS, HEAD_DIM).astype(MXU_DTYPE)
    gt = gates.reshape(B, S, KV_HEADS * LANES)
    o = _nsa_attention(qs, cmp[0].transpose(0, 2, 1, 3), cmp[1].transpose(0, 2, 3, 1), kat, vsa, kwt, vwa, gt,
                       rel_bias)
    y = _proj_residual(x2, o.reshape(B * S, HQ), w_out)
    return (y.reshape(B, S, D), rows.reshape(B, 4, KV_HEADS, HEAD_DIM, S).transpose(0, 4, 1, 2, 3),
            win.reshape(B, WINDOW, 2, KV_HEADS, HEAD_DIM))


PAGES_PER_STEP = 16


def _page_gather_kernel(pt_ref, *refs):
    del pt_ref
    pages = refs[:PAGES_PER_STEP]
    c_ref, kvt_ref, rows_sc = refs[PAGES_PER_STEP:]
    for p, pg in enumerate(pages):
        pos = slice(p * PAGE_SIZE, (p + 1) * PAGE_SIZE)
        for slot in range(2):
            for pair in range(KV_HEADS // 2):
                tile = pg[0, 0, slot, 2 * pair:2 * pair + 2].reshape(LANES, PAGE_SIZE)
                rows_sc[slot * 2 + pair, pos, :] = tile.T
        for slot in range(2):
            kvt_ref[slot, 0, :, :, pos] = pg[0, 0, 2 + slot].astype(MXU_DTYPE)
    n_chunks = PAGES_PER_STEP * PAGE_SIZE // CMP_STRIDE
    for l in range(CMP_STRIDE):
        for t in range(4):
            lo = l * SLOT + (t % 2) * LANES
            c_ref[t // 2, 0, :, lo:lo + LANES] = rows_sc[t, pl.ds(l, n_chunks, stride=CMP_STRIDE), :].astype(MXU_DTYPE)


def _page_gather(cache, page_table, layer):
    nseq, n_pages = page_table.shape
    assert n_pages % PAGES_PER_STEP == 0 and 2 * LANES == SLOT
    P = n_pages * PAGE_SIZE
    cache_t = cache.transpose(0, 2, 3, 4, 5, 1)
    page_specs = [
        pl.BlockSpec((1, 1, 4, KV_HEADS, HEAD_DIM, PAGE_SIZE),
                     functools.partial(lambda r, n, j, pt: (pt[n, PAGES_PER_STEP * j + r], layer, 0, 0, 0, 0), r))
        for r in range(PAGES_PER_STEP)]
    rows_per_step = PAGES_PER_STEP * PAGE_SIZE
    return pl.pallas_call(
        _page_gather_kernel,
        grid_spec=pltpu.PrefetchScalarGridSpec(
            num_scalar_prefetch=1,
            grid=(nseq, n_pages // PAGES_PER_STEP),
            in_specs=page_specs,
            out_specs=[pl.BlockSpec((2, 1, rows_per_step // CMP_STRIDE, CMP_STRIDE * SLOT),
                                    lambda n, j, pt: (0, n, j, 0)),
                       pl.BlockSpec((2, 1, KV_HEADS, HEAD_DIM, rows_per_step), lambda n, j, pt: (0, n, 0, 0, j))],
            scratch_shapes=[pltpu.VMEM((4, rows_per_step, LANES), jnp.float32)]),
        out_shape=[jax.ShapeDtypeStruct((2, nseq, P // CMP_STRIDE, CMP_STRIDE * SLOT), MXU_DTYPE),
                   jax.ShapeDtypeStruct((2, nseq, KV_HEADS, HEAD_DIM, P), MXU_DTYPE)],
        compiler_params=pltpu.CompilerParams(dimension_semantics=("arbitrary", "arbitrary")),
        name="page_gather",
    )(page_table, *([cache_t] * PAGES_PER_STEP))


def _split3(x):
    hi = x.astype(MXU_DTYPE)
    r1 = x - hi.astype(jnp.float32)
    mid = r1.astype(MXU_DTYPE)
    return hi, mid, (r1 - mid.astype(jnp.float32)).astype(MXU_DTYPE)


def _nsa_decode_kernel(q_ref, qbd_ref, kc_ref, vc_ref, kst_ref, vst_ref, win_ref, rows_ref, wnew_ref, g_ref,
                       oh_ref, grp_ref, bc_ref, bs_ref, bw_ref, bn_ref, o_ref, nwin_ref, pen_sc, *, nb):
    f32 = jnp.float32
    cd = MXU_DTYPE
    nt = (((1,), (1,)), ((), ()))
    wb = win_ref.shape[-1]
    rows = rows_ref[0]
    wnew = wnew_ref[0]

    def softmax_with_new(s, s_new):
        m = jnp.maximum(jnp.max(s, axis=1, keepdims=True), s_new)
        e = jnp.exp(s - m)
        e_new = jnp.exp(s_new - m)
        inv = 1.0 / (jnp.sum(e, axis=1, keepdims=True) + e_new)
        return e * inv, e_new * inv

    def bf(x):
        return x.astype(cd).astype(f32)

    for k in range(KV_HEADS):
        hs = slice(k * Q_PER_KV, (k + 1) * Q_PER_KV)
        lanes = slice(k * HEAD_DIM, (k + 1) * HEAD_DIM)
        q = q_ref[0, hs, :]
        gates = g_ref[0, hs, :]
        sc = (lax.dot_general(q, kc_ref[0, 0][:, lanes].astype(cd), nt, preferred_element_type=f32)
              + bc_ref[hs, :])
        ec = jnp.exp(sc - jnp.max(sc, axis=1, keepdims=True))
        pc = ec / jnp.sum(ec, axis=1, keepdims=True)
        o_ref[0, hs, :] = gates[:, 0:1] * jnp.dot(pc.astype(cd), vc_ref[0, 0][:, lanes].astype(cd),
                                                  preferred_element_type=f32)

        imp = sum(jnp.dot(part, grp_ref[...], preferred_element_type=f32) for part in _split3(pc))
        imp = jnp.sum(imp, axis=0, keepdims=True)
        blk = lax.broadcasted_iota(jnp.int32, imp.shape, 1)
        forced = (blk == 0) | (blk == nb - 1)
        score = jnp.where(forced, jnp.inf, imp)
        pen = jnp.full(imp.shape, NEG, f32)
        for _ in range(N_SEL - 1):
            best = jnp.max(score, axis=1, keepdims=True)
            first = jnp.min(jnp.where(score == best, blk, nb), axis=1, keepdims=True)
            hit = blk == first
            pen = jnp.where(hit, 0.0, pen)
            score = jnp.where(hit, -jnp.inf, score)
        pen_sc[hs, :] = jnp.broadcast_to(pen, (Q_PER_KV, nb))

    n_keys = kst_ref.shape[-1]
    qbd = qbd_ref[0]
    key_pen = jnp.dot(pen_sc[...].astype(cd), oh_ref[...], preferred_element_type=f32)
    ss = (jnp.dot(qbd, kst_ref[0, 0].reshape(SLOT, n_keys), preferred_element_type=f32)
          + key_pen + bs_ref[...])
    s_new = (jnp.sum(qbd.astype(f32) * bf(rows[:, 2 * SLOT:3 * SLOT]), axis=1, keepdims=True) + bn_ref[...])
    p, p_new = softmax_with_new(ss, s_new)
    sel_all = (lax.dot_general(p.astype(cd), vst_ref[0, 0].reshape(SLOT, n_keys), nt,
                               preferred_element_type=f32)
               + bf(p_new) * bf(rows[:, 3 * SLOT:4 * SLOT]))

    eye = (lax.broadcasted_iota(jnp.int32, (HEAD_DIM, HEAD_DIM), 0)
           == lax.broadcasted_iota(jnp.int32, (HEAD_DIM, HEAD_DIM), 1))
    last = lax.broadcasted_iota(jnp.int32, (HEAD_DIM, wb), 1) == wb - 1
    for k in range(KV_HEADS):
        hs = slice(k * Q_PER_KV, (k + 1) * Q_PER_KV)
        lanes = slice(k * HEAD_DIM, (k + 1) * HEAD_DIM)
        q = q_ref[0, hs, :]
        gates = g_ref[0, hs, :]
        kw_t = win_ref[0, 0, k]
        vw_t = win_ref[0, 1, k]
        k_new = wnew[:, :SLOT][:, lanes]
        v_new = wnew[:, SLOT:][:, lanes]
        sw = jnp.dot(q, kw_t.astype(cd), preferred_element_type=f32) + bw_ref[hs, :]
        w_new = jnp.sum(q.astype(f32) * bf(k_new), axis=1, keepdims=True) + bn_ref[hs, :]
        p, p_new = softmax_with_new(sw, w_new)
        acc = (lax.dot_general(p.astype(cd), vw_t.astype(cd), nt, preferred_element_type=f32)
               + bf(p_new) * bf(v_new))
        o_ref[0, hs, :] = o_ref[0, hs, :] + gates[:, 1:2] * sel_all[hs, :][:, lanes] + gates[:, 2:3] * acc
        for s, (tile, new_row) in enumerate(((kw_t, k_new), (vw_t, v_new))):
            col = jnp.sum(jnp.where(eye, jnp.broadcast_to(new_row, (HEAD_DIM, HEAD_DIM)), 0.0),
                          axis=1, keepdims=True)
            nwin_ref[0, s, k] = jnp.where(last, col, pltpu.roll(tile, wb - 1, 1))


def _nsa_decode_layer(x, cmp_rows, sel_kvt, win_buf, ln_g, w_in, w_out, q_gain, k_gain, pe, w1, w2, rel_bias):
    N, L, D = x.shape
    P = sel_kvt.shape[-1]
    wb = win_buf.shape[1]
    assert L == 1 and P % SEL_BLOCK == 0 and wb == WINDOW and P >= BIAS_SPAN and P // SEL_BLOCK >= N_SEL - 1
    f32 = jnp.float32
    cd = MXU_DTYPE
    nc = P // CMP_STRIDE
    nb = P // SEL_BLOCK
    x2 = x.reshape(N, D)
    q, rows, wnew, gates = _nsa_project(x2, ln_g, w_in, q_gain, k_gain, N, 1)
    cmp = _compress(cmp_rows, pe, w1, w2, k_gain[0])
    qs =(q.reshape(N, N_HEADS, HEAD_DIM) * HEAD_DIM ** -0.5).astype(cd)
    own = (np.arange(N_HEADS)[:, None] // Q_PER_KV == np.arange(SLOT)[None, :] // HEAD_DIM)
    qbd = jnp.where(own[None], jnp.tile(qs, (1, 1, KV_HEADS)), jnp.zeros((), cd))
    win = win_buf.transpose(0, 2, 3, 4, 1)
    onehot = jnp.asarray(np.arange(nb)[:, None] == (np.arange(P) // SEL_BLOCK)[None, :], dtype=cd)
    grp = jnp.asarray((np.arange(nc) // CMP_PER_SEL)[:, None] == np.arange(nb)[None, :], dtype=cd)
    fd = rel_bias[_bucket_of_distance()].T
    fd = fd - fd[:, BIAS_SPAN - 1:]
    dc = P - (np.arange(nc) * CMP_STRIDE + CMP_LEN - 1)
    b_cmp = jnp.where(dc >= 0, jnp.take(fd, np.clip(dc, 0, BIAS_SPAN - 1), axis=1), NEG)
    b_sel = jnp.concatenate([jnp.zeros((N_HEADS, P - BIAS_SPAN), f32),
                             jnp.take(fd, np.arange(BIAS_SPAN, 0, -1).clip(0, BIAS_SPAN - 1), axis=1)], axis=1)
    b_win = jnp.take(fd, np.clip(wb - np.arange(wb), 0, BIAS_SPAN - 1), axis=1)
    b_new = fd[:, 0:1]
    seq3 = lambda n: (n, 0, 0)
    fixed = lambda n: (0, 0)
    win_spec = pl.BlockSpec((1, 2, KV_HEADS, HEAD_DIM, wb), lambda n: (n, 0, 0, 0, 0))
    o, new_win = pl.pallas_call(
        functools.partial(_nsa_decode_kernel, nb=nb),
        grid=(N,),
        in_specs=[pl.BlockSpec((1, N_HEADS, HEAD_DIM), seq3),
                  pl.BlockSpec((1, N_HEADS, SLOT), seq3),
                  pl.BlockSpec((1, 1, nc, SLOT), lambda n: (0, n, 0, 0)),
                  pl.BlockSpec((1, 1, nc, SLOT), lambda n: (1, n, 0, 0)),
                  pl.BlockSpec((1, 1, KV_HEADS, HEAD_DIM, P), lambda n: (0, n, 0, 0, 0)),
                  pl.BlockSpec((1, 1, KV_HEADS, HEAD_DIM, P), lambda n: (1, n, 0, 0, 0)),
                  win_spec,
                  pl.BlockSpec((1, 1, 4 * SLOT), seq3),
                  pl.BlockSpec((1, 1, 2 * SLOT), seq3),
                  pl.BlockSpec((1, N_HEADS, 3), seq3),
                  pl.BlockSpec((nb, P), fixed), pl.BlockSpec((nc, nb), fixed), pl.BlockSpec((N_HEADS, nc), fixed),
                  pl.BlockSpec((N_HEADS, P), fixed), pl.BlockSpec((N_HEADS, wb), fixed),
                  pl.BlockSpec((N_HEADS, 1), fixed)],
        out_specs=[pl.BlockSpec((1, N_HEADS, HEAD_DIM), seq3), win_spec],
        out_shape=[jax.ShapeDtypeStruct((N, N_HEADS, HEAD_DIM), f32), jax.ShapeDtypeStruct(win.shape, win.dtype)],
        scratch_shapes=[pltpu.VMEM((N_HEADS, nb), f32)],
        compiler_params=pltpu.CompilerParams(dimension_semantics=("arbitrary",),
                                             vmem_limit_bytes=VMEM_LIMIT_BYTES),
        name="nsa_decode_attn",
    )(qs, qbd, cmp, cmp, sel_kvt, sel_kvt, win, rows.reshape(N, 1, 4 * SLOT), wnew.reshape(N, 1, 2 * SLOT),
      gates[:, :3 * N_HEADS].reshape(N, N_HEADS, 3), onehot, grp, b_cmp, b_sel, b_win, b_new)
    y = _proj_residual(x2, o.reshape(N, HQ), w_out)
    new_win = new_win.transpose(0, 4, 1, 2, 3)
    return y.reshape(N, 1, D), rows.reshape(N, 1, 4, KV_HEADS, HEAD_DIM), new_win


HQ = N_HEADS * HEAD_DIM
SLOT = KV_HEADS * HEAD_DIM
NSA_COLS = HQ + 6 * SLOT + 3 * N_HEADS
NSA_COLS_PAD = -(-NSA_COLS // LANES) * LANES
GATE_COLS = NSA_COLS_PAD - HQ - 6 * SLOT


def _nsa_proj_kernel(x_ref, g_ref, w_ref, qg_ref, kg_ref, *outs, prompt, n_blocks):
    f32 = jnp.float32
    x = x_ref[...]
    xn = x * lax.rsqrt(jnp.mean(x * x, axis=-1, keepdims=True) + EPS) * g_ref[...]
    h = jnp.dot(xn.astype(MXU_DTYPE), w_ref[...], preferred_element_type=f32)

    def head_norm(v, gain):
        return v * lax.rsqrt(jnp.mean(v * v, axis=-1, keepdims=True) + EPS) * gain

    def slot(s, kv):
        lo = HQ + s * SLOT + kv * HEAD_DIM
        return h[:, lo:lo + HEAD_DIM]

    gates = 1.0 / (1.0 + jnp.exp(-h[:, HQ + 6 * SLOT:]))
    if prompt:
        q_ref, rows_ref, win_ref, kat_ref, vsa_ref, kwt_ref, vwa_ref, c_ref, gate_ref = outs
        for hh in range(N_HEADS):
            qn = head_norm(h[:, hh * HEAD_DIM:(hh + 1) * HEAD_DIM], qg_ref[...])
            q_ref[0, hh] = (qn * (HEAD_DIM ** -0.5 * LOG2E)).astype(MXU_DTYPE)
        c_ref[0, 0] = h[:, HQ:HQ + SLOT].astype(MXU_DTYPE)
        c_ref[1, 0] = h[:, HQ + SLOT:HQ + 2 * SLOT].astype(MXU_DTYPE)
        rows, ka = x.shape[0], kat_ref.shape[3]
        nb = n_blocks
        first_key = (pl.program_id(0) % (nb * SEL_BLOCK // rows)) * rows
        key_blk = (first_key + lax.broadcasted_iota(jnp.int32, (nb, rows), 1)) // SEL_BLOCK
        onehot = jnp.where(key_blk == lax.broadcasted_iota(jnp.int32, (nb, rows), 0), 1.0, 0.0).astype(MXU_DTYPE)
        kpad = jnp.zeros((ka - nb - HEAD_DIM, rows), MXU_DTYPE)
        ones = jnp.ones((rows, LANES - HEAD_DIM), MXU_DTYPE)
    else:
        q_ref, rows_ref, win_ref, gate_ref = outs
        for hh in range(N_HEADS):
            q_ref[:, hh * HEAD_DIM:(hh + 1) * HEAD_DIM] = head_norm(h[:, hh * HEAD_DIM:(hh + 1) * HEAD_DIM],
                                                                    qg_ref[...])
    if prompt:
        rows_ref[0, 0:2 * SLOT, :] = h[:, HQ:HQ + 2 * SLOT].T
        rows_ref[0, 3 * SLOT:4 * SLOT, :] = h[:, HQ + 3 * SLOT:HQ + 4 * SLOT].T
    else:
        rows_ref[:, 0:2 * SLOT] = h[:, HQ:HQ + 2 * SLOT]
        rows_ref[:, 3 * SLOT:4 * SLOT] = h[:, HQ + 3 * SLOT:HQ + 4 * SLOT]
    for kv in range(KV_HEADS):
        lanes = slice(kv * HEAD_DIM, (kv + 1) * HEAD_DIM)
        ksn = head_norm(slot(2, kv), kg_ref[1:2, :])
        kwn = head_norm(slot(4, kv), kg_ref[2:3, :])
        if prompt:
            ksn_t = ksn.T
            rows_ref[0, 2 * SLOT + kv * HEAD_DIM:2 * SLOT + (kv + 1) * HEAD_DIM, :] = ksn_t
            win_ref[0, 0, :, lanes] = kwn
            kat_ref[0, kv, 0] = jnp.concatenate([onehot, ksn_t.astype(MXU_DTYPE), kpad], axis=0)
            kwt_ref[0, kv, 0] = kwn.T.astype(MXU_DTYPE)
            vsa_ref[0, kv, 0] = jnp.concatenate([slot(3, kv).astype(MXU_DTYPE), ones], axis=1)
            vwa_ref[0, kv, 0] = jnp.concatenate([slot(5, kv).astype(MXU_DTYPE), ones], axis=1)
        else:
            rows_ref[:, 2 * SLOT + kv * HEAD_DIM:2 * SLOT + (kv + 1) * HEAD_DIM] = ksn
            win_ref[:, lanes] = kwn
    if prompt:
        win_ref[0, 0, :, SLOT:2 * SLOT] = h[:, HQ + 5 * SLOT:HQ + 6 * SLOT]
    else:
        win_ref[:, SLOT:2 * SLOT] = h[:, HQ + 5 * SLOT:HQ + 6 * SLOT]
    if prompt:
        gate_ref[...] = jnp.zeros(gate_ref.shape, f32)
        per_kv = 3 * Q_PER_KV
        for kv in range(KV_HEADS):
            gate_ref[:, kv * LANES:kv * LANES + per_kv] = gates[:, kv * per_kv:(kv + 1) * per_kv]
    else:
        gate_ref[...] = gates


def _nsa_project(x, ln_g, w_in, q_gain, k_gain, batch, seq):
    T, D = x.shape
    f32 = jnp.float32
    cd = MXU_DTYPE
    prompt = seq > 1
    tb = TK if prompt else T
    w = jnp.pad(w_in, ((0, 0), (0, NSA_COLS_PAD - NSA_COLS))).astype(cd)
    fixed = lambda i: (0, 0)
    row = lambda i: (i, 0)
    in_specs = [pl.BlockSpec((tb, D), row), pl.BlockSpec((1, D), fixed), pl.BlockSpec((D, NSA_COLS_PAD), fixed),
                pl.BlockSpec((1, HEAD_DIM), fixed), pl.BlockSpec((3, HEAD_DIM), fixed)]
    if prompt:
        assert seq % tb == 0 and WINDOW == 2 * tb
        n = seq // tb
        n_blocks = seq // SEL_BLOCK
        ka = -(-(n_blocks + HEAD_DIM) // LANES) * LANES
        chunk = lambda i: (i // n, 0, i % n, 0, 0)
        kt_spec = pl.BlockSpec((1, KV_HEADS, 1, HEAD_DIM, tb), chunk)
        ka_spec = pl.BlockSpec((1, KV_HEADS, 1, ka, tb), chunk)
        v_spec = pl.BlockSpec((1, KV_HEADS, 1, tb, LANES), chunk)
        kt_shape = jax.ShapeDtypeStruct((batch, KV_HEADS, n, HEAD_DIM, tb), cd)
        ka_shape = jax.ShapeDtypeStruct((batch, KV_HEADS, n, ka, tb), cd)
        v_shape = jax.ShapeDtypeStruct((batch, KV_HEADS, n, tb, LANES), cd)
        out_specs = [
            pl.BlockSpec((1, N_HEADS, tb, HEAD_DIM), lambda i: (i // n, 0, i % n, 0)),
            pl.BlockSpec((1, 4 * SLOT, tb), lambda i: (i // n, 0, i % n)),
            pl.BlockSpec((1, 1, tb, 2 * SLOT), lambda i: (i // n, jnp.maximum(i % n - (n - 2), 0), 0, 0)),
            ka_spec, v_spec, kt_spec, v_spec,
            pl.BlockSpec((2, 1, tb, SLOT), lambda i: (0, i // n, i % n, 0)),
            pl.BlockSpec((tb, KV_HEADS * LANES), row)]
        out_shape = [
            jax.ShapeDtypeStruct((batch, N_HEADS, seq, HEAD_DIM), cd),
            jax.ShapeDtypeStruct((batch, 4 * SLOT, seq), f32),
            jax.ShapeDtypeStruct((batch, 2, tb, 2 * SLOT), f32),
            ka_shape, v_shape, kt_shape, v_shape,
            jax.ShapeDtypeStruct((2, batch, seq, SLOT), cd),
            jax.ShapeDtypeStruct((T, KV_HEADS * LANES), f32)]
    else:
        out_specs = [pl.BlockSpec((tb, HQ), row), pl.BlockSpec((tb, 4 * SLOT), row),
                     pl.BlockSpec((tb, 2 * SLOT), row), pl.BlockSpec((tb, GATE_COLS), row)]
        n_blocks = 0
        out_shape = [jax.ShapeDtypeStruct((T, HQ), f32), jax.ShapeDtypeStruct((T, 4 * SLOT), f32),
                     jax.ShapeDtypeStruct((T, 2 * SLOT), f32), jax.ShapeDtypeStruct((T, GATE_COLS), f32)]
    return pl.pallas_call(
        functools.partial(_nsa_proj_kernel, prompt=prompt, n_blocks=n_blocks),
        grid=(T // tb,),
        in_specs=in_specs, out_specs=out_specs, out_shape=out_shape,
        compiler_params=pltpu.CompilerParams(dimension_semantics=("arbitrary",),
                                             vmem_limit_bytes=VMEM_LIMIT_BYTES),
        name="nsa_proj",
    )(x, ln_g.reshape(1, D), w, q_gain.reshape(1, HEAD_DIM), k_gain)


def _compress_kernel(c_ref, w1_ref, pe_ref, w2_ref, kg_ref, o_ref):
    f32 = jnp.float32
    pairs = SLOT // LANES
    c = c_ref[0, 0]
    u = [jnp.dot(jnp.concatenate([c[:, l * SLOT + p * LANES:l * SLOT + (p + 1) * LANES]
                                  for l in range(CMP_STRIDE)], axis=1).astype(MXU_DTYPE),
                 w1_ref[0], preferred_element_type=f32) for p in range(pairs)]
    first = jnp.concatenate([up[:, :LANES] for up in u], axis=1)
    second = jnp.concatenate([up[:, LANES:] for up in u], axis=1)
    pe = jnp.dot(pe_ref[0], w1_ref[0], preferred_element_type=f32)
    pe = jnp.concatenate([pe[0:1, :LANES] + pe[1:2, LANES:]] * pairs, axis=1)
    nc = first.shape[0]
    hid = first + pltpu.roll(second, nc - 1, 0) + pe
    out = jnp.dot(_gelu_tanh(hid).astype(MXU_DTYPE), w2_ref[0], preferred_element_type=f32)
    is_key = pl.program_id(1) == 0
    for kv in range(KV_HEADS):
        lanes = slice(kv * HEAD_DIM, (kv + 1) * HEAD_DIM)
        v = out[:, lanes]
        vn = v * lax.rsqrt(jnp.mean(v * v, axis=-1, keepdims=True) + EPS) * kg_ref[...]
        o_ref[0, 0, :, lanes] = jnp.where(is_key, vn, v)


def _compress(c, pe, w1, w2, k_gain0):
    n, nc = c.shape[1:3]
    cd = MXU_DTYPE
    kc = CMP_STRIDE * SLOT
    eye = jnp.eye(KV_HEADS, dtype=jnp.float32)
    per_tile = LANES // HEAD_DIM
    kp = CMP_STRIDE * LANES
    w1h = w1.reshape(2, 2, CMP_STRIDE, HEAD_DIM, CMP_HID)
    w1b = jnp.einsum('shldf,kj->slkdhjf', w1h, eye[:per_tile, :per_tile]).reshape(2, kp, 2 * LANES).astype(cd)
    w2b = jnp.einsum('sdf,kj->skdjf', w2, eye).reshape(2, SLOT, SLOT).astype(cd)
    peh = jnp.broadcast_to(pe.reshape(2, 2, CMP_STRIDE, 1, HEAD_DIM), (2, 2, CMP_STRIDE, per_tile, HEAD_DIM))
    peh = jnp.pad(peh.reshape(2, 2, kp), ((0, 0), (0, 14), (0, 0))).astype(cd)
    return pl.pallas_call(
        _compress_kernel,
        grid=(n, 2),
        in_specs=[pl.BlockSpec((1, 1, nc, kc), lambda b, s: (s, b, 0, 0)),
                  pl.BlockSpec((1, kp, 2 * LANES), lambda b, s: (s, 0, 0)),
                  pl.BlockSpec((1, 16, kp), lambda b, s: (s, 0, 0)),
                  pl.BlockSpec((1, SLOT, SLOT), lambda b, s: (s, 0, 0)),
                  pl.BlockSpec((1, HEAD_DIM), lambda b, s: (0, 0))],
        out_specs=pl.BlockSpec((1, 1, nc, SLOT), lambda b, s: (s, b, 0, 0)),
        out_shape=jax.ShapeDtypeStruct((2, n, nc, SLOT), jnp.float32),
        compiler_params=pltpu.CompilerParams(dimension_semantics=("arbitrary", "arbitrary"),
                                             vmem_limit_bytes=VMEM_LIMIT_BYTES),
        name="nsa_compress",
    )(c, w1b, peh, w2b, k_gain0.reshape(1, HEAD_DIM))


def _proj_residual_kernel(x_ref, a_ref, w_ref, o_ref):
    o_ref[...] = x_ref[...] + jnp.dot(a_ref[...].astype(MXU_DTYPE), w_ref[...],
                                      preferred_element_type=jnp.float32)


def _proj_residual(x, a, w):
    T, D = x.shape
    k = a.shape[1]
    tb = min(512, T)
    assert T % tb == 0
    return pl.pallas_call(
        _proj_residual_kernel,
        grid=(T // tb,),
        in_specs=[pl.BlockSpec((tb, D), lambda i: (i, 0)), pl.BlockSpec((tb, k), lambda i: (i, 0)),
                  pl.BlockSpec((k, D), lambda i: (0, 0))],
        out_specs=pl.BlockSpec((tb, D), lambda i: (i, 0)),
        out_shape=jax.ShapeDtypeStruct((T, D), jnp.float32),
        compiler_params=pltpu.CompilerParams(dimension_semantics=("arbitrary",)),
        name="proj_residual",
    )(x, a, w.astype(MXU_DTYPE))


MOE_ROWS = 256
ROUTE_COLS = LANES
ROW_DMA_UNROLL = 8


def _moe_route_kernel(x_ref, g_ref, w_ref, b_ref, tri_ref, h_ref, meta_ref, wt_ref, cnt_ref, carry_sc):
    f32 = jnp.float32
    step = pl.program_id(0)

    @pl.when(step == 0)
    def _():
        carry_sc[...] = jnp.zeros(carry_sc.shape, f32)

    x = x_ref[...]
    h = x * lax.rsqrt(jnp.mean(x * x, axis=-1, keepdims=True) + EPS) * g_ref[...]
    h_ref[...] = h
    logits = jnp.dot(h.astype(MXU_DTYPE), w_ref[...], preferred_element_type=f32) + b_ref[...]
    tb = logits.shape[0]
    col = lax.broadcasted_iota(jnp.int32, (tb, ROUTE_COLS), 1)

    def first_max(vals):
        best = jnp.max(vals, axis=1, keepdims=True)
        return best, jnp.min(jnp.where(vals == best, col, ROUTE_COLS), axis=1, keepdims=True)

    lg = jnp.where(col < N_GROUPS, logits, -jnp.inf)
    g_best, grp = first_max(lg)
    g_w = 1.0 / jnp.sum(jnp.exp(lg - g_best), axis=1, keepdims=True)
    lo = N_GROUPS + EXPERTS_PER_GROUP * grp
    le = jnp.where((col >= lo) & (col < lo + EXPERTS_PER_GROUP), logits, -jnp.inf)
    v0, c0 = first_max(le)
    v1, c1 = first_max(jnp.where(col == c0, -jnp.inf, le))
    e1 = jnp.exp(v1 - v0)
    w0 = g_w / (1.0 + e1)
    w1 = g_w * e1 / (1.0 + e1)
    chosen = (col == c0) | (col == c1)
    before = jnp.dot(tri_ref[...], jnp.where(chosen, 1.0, 0.0).astype(MXU_DTYPE),
                     preferred_element_type=f32) + carry_sc[...]
    r0 = jnp.sum(jnp.where(col == c0, before, 0.0), axis=1, keepdims=True).astype(jnp.int32)
    r1 = jnp.sum(jnp.where(col == c1, before, 0.0), axis=1, keepdims=True).astype(jnp.int32)
    carry_sc[...] += jnp.sum(jnp.where(chosen, 1.0, 0.0), axis=0, keepdims=True)
    cnt_ref[...] = carry_sc[...]
    meta_ref[...] = jnp.where(col == 0, c0 - N_GROUPS, jnp.where(col == 1, c1 - N_GROUPS,
                              jnp.where(col == 2, r0, jnp.where(col == 3, r1, 0))))
    wt_ref[...] = jnp.where(col == 0, w0, jnp.where(col == 1, w1, 0.0))


def _moe_dispatch_kernel(tail_ref, used_ref, dest_ref, h_ref, xbuf_ref, zero_sc, sem, zero_sem):
    tb = h_ref.shape[0]

    @pl.when(pl.program_id(0) == 0)
    def _():
        zero_sc[...] = jnp.zeros(zero_sc.shape, zero_sc.dtype)

        def clear(e):
            first_row = pl.multiple_of(jnp.maximum(tail_ref[e], 0), MOE_ROWS)
            return pltpu.make_async_copy(zero_sc, xbuf_ref.at[pl.ds(first_row, MOE_ROWS)], zero_sem)

        for e in range(N_EXPERTS):
            @pl.when(tail_ref[e] >= 0)
            def _():
                clear(e).start()
        for e in range(N_EXPERTS):
            @pl.when(tail_ref[e] >= 0)
            def _():
                clear(e).wait()

        def unused(i):
            return pltpu.make_async_copy(zero_sc, xbuf_ref.at[pl.ds(pl.multiple_of(i * MOE_ROWS, MOE_ROWS), MOE_ROWS)],
                                         zero_sem)

        n_all = xbuf_ref.shape[0] // MOE_ROWS
        lax.fori_loop(used_ref[0], n_all, lambda i, c: (unused(i).start(), c)[1], 0)
        lax.fori_loop(used_ref[0], n_all, lambda i, c: (unused(i).wait(), c)[1], 0)

    def row_copy(r, k):
        return pltpu.make_async_copy(h_ref.at[pl.ds(r, 1)], xbuf_ref.at[pl.ds(dest_ref[0, 0, 2 * r + k], 1)], sem)

    def start(r, c):
        row_copy(r, 0).start()
        row_copy(r, 1).start()
        return c

    def wait(r, c):
        row_copy(r, 0).wait()
        row_copy(r, 1).wait()
        return c

    lax.fori_loop(0, tb, start, 0, unroll=ROW_DMA_UNROLL)
    lax.fori_loop(0, tb, wait, 0, unroll=ROW_DMA_UNROLL)


def _moe_expert_kernel(blk_e_ref, nblk_ref, x_ref, wgu_ref, wdn_ref, y_ref, wgu_sc, wdn_sc):
    i = pl.program_id(0)
    f32 = jnp.float32

    @pl.when(i < nblk_ref[0])
    def _():
        changed = jnp.logical_or(i == 0, blk_e_ref[i] != blk_e_ref[jnp.maximum(i - 1, 0)])

        @pl.when(changed)
        def _():
            wgu_sc[...] = wgu_ref[0, 0].astype(MXU_DTYPE)
            wdn_sc[...] = wdn_ref[0, 0].astype(MXU_DTYPE)

        gu = jnp.dot(x_ref[...].astype(MXU_DTYPE), wgu_sc[...], preferred_element_type=f32)
        gate = gu[:, :D_EXPERT]
        act = gate * (1.0 / (1.0 + jnp.exp(-gate))) * gu[:, D_EXPERT:]
        y_ref[...] = jnp.dot(act.astype(MXU_DTYPE), wdn_sc[...], preferred_element_type=f32)

    @pl.when(i >= nblk_ref[0])
    def _():
        y_ref[...] = jnp.zeros(y_ref.shape, f32)


def _moe_combine_kernel(dest_ref, x_ref, wt_ref, ybuf_ref, o_ref, rows_sc, sem):
    tb = x_ref.shape[0]

    def row_copy(r, k):
        return pltpu.make_async_copy(ybuf_ref.at[pl.ds(dest_ref[0, 0, 2 * r + k], 1)],
                                     rows_sc.at[k, pl.ds(r, 1)], sem)

    def start(r, c):
        row_copy(r, 0).start()
        row_copy(r, 1).start()
        return c

    def wait(r, c):
        row_copy(r, 0).wait()
        row_copy(r, 1).wait()
        return c

    lax.fori_loop(0, tb, start, 0, unroll=ROW_DMA_UNROLL)
    lax.fori_loop(0, tb, wait, 0, unroll=ROW_DMA_UNROLL)
    wt = wt_ref[...]
    o_ref[...] = x_ref[...] + (wt[:, 0:1] * rows_sc[0] + wt[:, 1:2] * rows_sc[1])


def _hier_moe_residual(x, ln_g, w_grp, b_grp, w_exp, b_exp, w_gu, w_dn, layer):
    T, D = x.shape
    f32 = jnp.float32
    tb = min(256, T)
    assert T % tb == 0
    nt = T // tb
    pad_cols = ROUTE_COLS - N_GROUPS - N_EXPERTS
    w_r = jnp.pad(jnp.concatenate([w_grp, w_exp], axis=1), ((0, 0), (0, pad_cols))).astype(MXU_DTYPE)
    b_r = jnp.pad(jnp.concatenate([b_grp, b_exp]), (0, pad_cols)).reshape(1, ROUTE_COLS)
    tri = jnp.asarray(np.tril(np.ones((tb, tb), np.float32), -1), dtype=MXU_DTYPE)
    row = lambda i: (i, 0)
    fixed = lambda i: (0, 0)
    h, meta, wt, cnt = pl.pallas_call(
        _moe_route_kernel,
        grid=(nt,),
        in_specs=[pl.BlockSpec((tb, D), row), pl.BlockSpec((1, D), fixed),
                  pl.BlockSpec((D, ROUTE_COLS), fixed), pl.BlockSpec((1, ROUTE_COLS), fixed),
                  pl.BlockSpec((tb, tb), fixed)],
        out_specs=[pl.BlockSpec((tb, D), row), pl.BlockSpec((tb, ROUTE_COLS), row),
                   pl.BlockSpec((tb, ROUTE_COLS), row), pl.BlockSpec((1, ROUTE_COLS), fixed)],
        out_shape=[jax.ShapeDtypeStruct((T, D), f32), jax.ShapeDtypeStruct((T, ROUTE_COLS), jnp.int32),
                   jax.ShapeDtypeStruct((T, ROUTE_COLS), f32), jax.ShapeDtypeStruct((1, ROUTE_COLS), f32)],
        scratch_shapes=[pltpu.VMEM((1, ROUTE_COLS), f32)],
        compiler_params=pltpu.CompilerParams(dimension_semantics=("arbitrary",)),
        name="moe_route",
    )(x, ln_g.reshape(1, D), w_r, b_r, tri)

    counts = cnt[0, N_GROUPS:N_GROUPS + N_EXPERTS].astype(jnp.int32)
    padded = (counts + MOE_ROWS - 1) // MOE_ROWS * MOE_ROWS
    pad_end = jnp.cumsum(padded)
    pad_start = pad_end - padded
    n_blocks = -(-(T * TOP_K) // MOE_ROWS) + N_EXPERTS
    n_slots = n_blocks * MOE_ROWS
    first_row = jnp.sum(jnp.where(meta[:, 0:2, None] == jnp.arange(N_EXPERTS, dtype=jnp.int32), pad_start, 0),
                        axis=-1)
    dest = first_row + meta[:, 2:4]
    dest = dest.reshape(nt, 1, 2 * tb)
    starts = jnp.arange(n_blocks, dtype=jnp.int32) * MOE_ROWS
    used = (pad_end[-1] // MOE_ROWS).astype(jnp.int32).reshape(1)
    blk_start = jnp.minimum(starts, pad_end[-1] - 1)
    blk_e = jnp.minimum(jnp.sum(pad_end[None, :] <= blk_start[:, None], axis=1), N_EXPERTS - 1).astype(jnp.int32)

    dest_spec = pl.BlockSpec((1, 1, 2 * tb), lambda i: (i, 0, 0), memory_space=pltpu.SMEM)
    tail = jnp.where(padded > 0, pad_end - MOE_ROWS, -1).astype(jnp.int32)
    xbuf = pl.pallas_call(
        _moe_dispatch_kernel,
        grid_spec=pltpu.PrefetchScalarGridSpec(
            num_scalar_prefetch=2,
            grid=(nt,),
            in_specs=[pl.BlockSpec((1, 1, 2 * tb), lambda i, tl, nu: (i, 0, 0), memory_space=pltpu.SMEM),
                      pl.BlockSpec((tb, D), lambda i, tl, nu: (i, 0))],
            out_specs=pl.BlockSpec(memory_space=pl.ANY),
            scratch_shapes=[pltpu.VMEM((MOE_ROWS, D), f32), pltpu.SemaphoreType.DMA(()),
                            pltpu.SemaphoreType.DMA(())]),
        out_shape=jax.ShapeDtypeStruct((n_slots, D), f32),
        compiler_params=pltpu.CompilerParams(dimension_semantics=("arbitrary",)),
        name="moe_dispatch",
    )(tail, used, dest, h)

    ybuf = pl.pallas_call(
        _moe_expert_kernel,
        grid_spec=pltpu.PrefetchScalarGridSpec(
            num_scalar_prefetch=2,
            grid=(n_blocks,),
            in_specs=[pl.BlockSpec((MOE_ROWS, D), lambda i, be, nb: (jnp.minimum(i, nb[0] - 1), 0)),
                      pl.BlockSpec((1, 1, D, 2 * D_EXPERT), lambda i, be, nb: (layer, be[i], 0, 0)),
                      pl.BlockSpec((1, 1, D_EXPERT, D), lambda i, be, nb: (layer, be[i], 0, 0))],
            out_specs=pl.BlockSpec((MOE_ROWS, D), lambda i, be, nb: (i, 0)),
            scratch_shapes=[pltpu.VMEM((D, 2 * D_EXPERT), MXU_DTYPE), pltpu.VMEM((D_EXPERT, D), MXU_DTYPE)]),
        out_shape=jax.ShapeDtypeStruct((n_slots, D), f32),
        compiler_params=pltpu.CompilerParams(dimension_semantics=("arbitrary",),
                                             vmem_limit_bytes=VMEM_LIMIT_BYTES),
        name="moe_experts",
    )(blk_e, used, xbuf, w_gu, w_dn)

    return pl.pallas_call(
        _moe_combine_kernel,
        grid=(nt,),
        in_specs=[dest_spec, pl.BlockSpec((tb, D), row), pl.BlockSpec((tb, ROUTE_COLS), row),
                  pl.BlockSpec(memory_space=pl.ANY)],
        out_specs=pl.BlockSpec((tb, D), row),
        out_shape=jax.ShapeDtypeStruct((T, D), f32),
        scratch_shapes=[pltpu.VMEM((TOP_K, tb, D), f32), pltpu.SemaphoreType.DMA(())],
        compiler_params=pltpu.CompilerParams(dimension_semantics=("arbitrary",)),
        name="moe_combine",
    )(dest, x, wt, ybuf)


def _gelu_tanh(x):
    return 0.5 * x * (1.0 + jnp.tanh(math.sqrt(2.0 / math.pi) * (x + 0.044715 * (x * x * x))))


def _gmlp_kernel(x_ref, g_ref, win_ref, bin_ref, lng_ref, lnb_ref, ws_ref, bs_ref, wout_ref,
                 o_ref, v_ref, *, single_position):
    f32 = jnp.float32
    x = x_ref[...]
    rows = x.shape[0]
    h = x * lax.rsqrt(jnp.mean(x * x, axis=-1, keepdims=True) + EPS) * g_ref[...]
    z = _gelu_tanh(jnp.dot(h.astype(MXU_DTYPE), win_ref[...], preferred_element_type=f32) + bin_ref[...])
    u = z[:, :GM_WIDTH]
    v = z[:, GM_WIDTH:]
    mu = jnp.mean(v, axis=-1, keepdims=True)
    var = jnp.mean(jnp.square(v - mu), axis=-1, keepdims=True)
    v = (v - mu) * lax.rsqrt(var + EPS) * lng_ref[...] + lnb_ref[...]
    v_ref[0] = v
    vb = v.astype(MXU_DTYPE)
    if single_position:
        s = ws_ref[...].astype(f32) * vb.astype(f32) + bs_ref[...]
    else:
        parts = []
        for c in range(rows // CHUNK):
            vc = vb[c * CHUNK:(c + 1) * CHUNK]
            parts.append(jnp.concatenate(
                [jnp.dot(ws_ref[g], vc[:, g * GM_GROUP_DIM:(g + 1) * GM_GROUP_DIM], preferred_element_type=f32)
                 + bs_ref[g] for g in range(GM_GROUPS)], axis=1))
        s = jnp.concatenate(parts, axis=0)
    y = jnp.dot((u * s).astype(MXU_DTYPE), wout_ref[...], preferred_element_type=f32)
    o_ref[...] = x + y


def _gmlp_residual(x, ln_g, w_in, b_in, ln2_g, ln2_b, w_s, b_s, w_out, seq):
    T, D = x.shape
    f32 = jnp.float32
    cd = MXU_DTYPE
    single = seq == 1
    tb = T if single else 2 * CHUNK
    assert T % tb == 0 and (single or seq % tb == 0)
    steps_per_seq = 1 if single else seq // tb
    if single:
        ws = jnp.repeat(w_s[:, 0, 0], GM_GROUP_DIM).reshape(1, GM_WIDTH).astype(cd)
        bs = jnp.repeat(b_s[:, 0], GM_GROUP_DIM).reshape(1, GM_WIDTH)
        ws_spec = pl.BlockSpec((1, GM_WIDTH), lambda i: (0, 0))
        bs_spec = pl.BlockSpec((1, GM_WIDTH), lambda i: (0, 0))
    else:
        causal = np.tril(np.ones((CHUNK, CHUNK), bool))
        ws = jnp.where(causal[None], w_s, 0).astype(cd)
        bs = jnp.broadcast_to(b_s[:, :, None], (GM_GROUPS, CHUNK, GM_GROUP_DIM))
        ws_spec = pl.BlockSpec((GM_GROUPS, CHUNK, CHUNK), lambda i: (0, 0, 0))
        bs_spec = pl.BlockSpec((GM_GROUPS, CHUNK, GM_GROUP_DIM), lambda i: (0, 0, 0))
    fixed = lambda i: (0, 0)
    row = lambda i: (i, 0)
    return pl.pallas_call(
        functools.partial(_gmlp_kernel, single_position=single),
        grid=(T // tb,),
        in_specs=[pl.BlockSpec((tb, D), row), pl.BlockSpec((1, D), fixed),
                  pl.BlockSpec((D, 2 * GM_WIDTH), fixed), pl.BlockSpec((1, 2 * GM_WIDTH), fixed),
                  pl.BlockSpec((1, GM_WIDTH), fixed), pl.BlockSpec((1, GM_WIDTH), fixed),
                  ws_spec, bs_spec, pl.BlockSpec((GM_WIDTH, D), fixed)],
        out_specs=[pl.BlockSpec((tb, D), row), pl.BlockSpec((1, tb, GM_WIDTH), lambda i: (i // steps_per_seq, 0, 0))],
        out_shape=[jax.ShapeDtypeStruct((T, D), f32),
                   jax.ShapeDtypeStruct((T // (tb * steps_per_seq), tb, GM_WIDTH), f32)],
        compiler_params=pltpu.CompilerParams(dimension_semantics=("arbitrary",),
                                             vmem_limit_bytes=VMEM_LIMIT_BYTES),
        name="gmlp",
    )(x, ln_g.reshape(1, D), w_in.astype(cd), b_in.reshape(1, -1), ln2_g.reshape(1, -1), ln2_b.reshape(1, -1),
      ws, bs, w_out.astype(cd))


def kernel(x_prompt, x_sample, cache_nsa_kv, state_win_kv, page_table, rel_bias, ln_mix, ln_ffn,
           nsa_w_in, nsa_w_out, nsa_q_gain, nsa_k_gain, cmp_pe, cmp_w1, cmp_w2,
           gm_w_in, gm_b_in, gm_ln_g, gm_ln_b, gm_w_s, gm_b_s, gm_w_out,
           moe_w_grp, moe_b_grp, moe_w_exp, moe_b_exp, moe_w_gu, moe_w_dn):
    xp = x_prompt
    xs = x_sample
    kv_p, kv_s, win_p, win_s, gv_p, gv_s = [], [], [], [], [], []
    for i in range(DEPTH):
        a = i // N_MIXERS
        if i % N_MIXERS == 0:
            nsa = (ln_mix[i], nsa_w_in[a], nsa_w_out[a], nsa_q_gain[a], nsa_k_gain[a],
                   cmp_pe[a], cmp_w1[a], cmp_w2[a], rel_bias)
            xp, rp, wp = _nsa_prompt_layer(xp, *nsa)
            cmp_rows, sel_kvt = _page_gather(cache_nsa_kv, page_table, a)
            xs, rs, ws = _nsa_decode_layer(xs, cmp_rows, sel_kvt, state_win_kv[a], *nsa)
            kv_p.append(rp)
            kv_s.append(rs)
            win_p.append(wp)
            win_s.append(ws)
        else:
            gm = (ln_mix[i], gm_w_in[a], gm_b_in[a], gm_ln_g[a], gm_ln_b[a], gm_w_s[a], gm_b_s[a], gm_w_out[a])
            bp, sp = xp.shape[:2]
            bs_, ss = xs.shape[:2]
            xp2, vp = _gmlp_residual(xp.reshape(-1, D_MODEL), *gm, seq=sp)
            xs2, vs = _gmlp_residual(xs.reshape(-1, D_MODEL), *gm, seq=ss)
            xp = xp2.reshape(xp.shape)
            xs = xs2.reshape(xs.shape)
            start = ((sp - 1) // CHUNK) * CHUNK
            gv_p.append(vp[:, vp.shape[1] - (sp - start):])
            gv_s.append(vs.reshape(bs_, ss, GM_WIDTH))
        moe = (ln_ffn[i], moe_w_grp[i], moe_b_grp[i], moe_w_exp[i], moe_b_exp[i], moe_w_gu, moe_w_dn, i)
        xp = _hier_moe_residual(xp.reshape(-1, D_MODEL), *moe).reshape(xp.shape)
        xs = _hier_moe_residual(xs.reshape(-1, D_MODEL), *moe).reshape(xs.shape)
    new_kv_prompt = jnp.stack(kv_p, axis=2)
    new_kv_sample = jnp.stack(kv_s, axis=2)
    new_win_prompt = jnp.stack(win_p, axis=0)
    new_win_sample = jnp.stack(win_s, axis=0)
    new_gm_v_prompt = jnp.stack(gv_p, axis=0)
    new_gm_v_sample = jnp.stack(gv_s, axis=0)
    return (xp, xs, new_kv_prompt, new_kv_sample, new_win_prompt, new_win_sample, new_gm_v_prompt, new_gm_v_sample)
```
